```python
import jax, jax.numpy as jnp
from jax import lax
import numpy as np

D_MODEL = 1024
BATCH = 32
SEQ = 2048
DEPTH = 1

N_META = 16
Q_BLOCK = 128
EPS = 1e-6
FOX_HEADS = 8
FOX_HEAD_DIM = 64
FOX_W = FOX_HEADS * FOX_HEAD_DIM
MLA_HEADS = 8
MLA_Q_RANK = 256
MLA_KV_RANK = 128
MLA_NOPE_DIM = 64
MLA_ROPE_DIM = 32
MLA_QK_DIM = MLA_NOPE_DIM + MLA_ROPE_DIM
MLA_V_DIM = 64
MLA_W = MLA_HEADS * MLA_V_DIM
ROPE_THETA = 10000.0
D_FF = 2816
N_BRANCH = 2
IN_SIZES = (FOX_W, FOX_W, FOX_W, FOX_HEADS, MLA_Q_RANK, MLA_KV_RANK, MLA_ROPE_DIM, N_BRANCH * D_MODEL)
IN_W = FOX_W * 3 + FOX_HEADS + MLA_Q_RANK + MLA_KV_RANK + MLA_ROPE_DIM + N_BRANCH * D_MODEL

kernel_name = "hybrid_fox_mla_macaron_meta"


def rms_norm(x, gain):
    xf = x.astype(jnp.float32)
    y = xf * lax.rsqrt(jnp.mean(xf * xf, axis=-1, keepdims=True) + EPS)
    return (y * gain.astype(jnp.float32)).astype(x.dtype)


def swiglu_ffn(u, w_gu, w_down):
    g, up = jnp.split(u @ w_gu, 2, axis=-1)
    return (jax.nn.silu(g) * up) @ w_down


def rope(x, cos, sin):
    xf = x.astype(jnp.float32)
    x1, x2 = jnp.split(xf, 2, axis=-1)
    out = jnp.concatenate([x1 * cos - x2 * sin, x1 * sin + x2 * cos], axis=-1)
    return out.astype(x.dtype)


def block_causal_attention(q, k, v, log_forget_cum=None):
    L = q.shape[2]
    scale = q.shape[-1] ** -0.5
    bounds = [(0, N_META)] + [(N_META + i * Q_BLOCK, N_META + (i + 1) * Q_BLOCK)
                              for i in range((L - N_META) // Q_BLOCK)]
    outs = []
    for qs, qe in bounds:
        qb = q[:, :, qs:qe].astype(jnp.float32)
        kb = k[:, :, :qe].astype(jnp.float32)
        s = jnp.einsum("bhqd,bhkd->bhqk", qb, kb) * scale
        if log_forget_cum is not None:
            s = s + log_forget_cum[:, :, qs:qe, None] - log_forget_cum[:, :, None, :qe]
        mask = jnp.arange(qs, qe)[:, None] >= jnp.arange(qe)[None, :]
        p = jax.nn.softmax(jnp.where(mask, s, -jnp.inf), axis=-1)
        outs.append(jnp.einsum("bhqk,bhkd->bhqd", p.astype(v.dtype), v[:, :, :qe]))
    return jnp.concatenate(outs, axis=2)


def split_heads(t, n_heads):
    B, L, _ = t.shape
    return t.reshape(B, L, n_heads, -1).transpose(0, 2, 1, 3)


def merge_heads(t):
    B, H, L, dh = t.shape
    return t.transpose(0, 2, 1, 3).reshape(B, L, H * dh)


def token_mixing(u, w_in, b_forget, b_gate, fox_q_norm, fox_k_norm, mla_cq_norm, mla_w_uq,
                 mla_ckv_norm, mla_w_ukv, mla_q_norm, mla_k_norm, w_branch_fox, w_branch_mla,
                 w_out):
    B, L, _ = u.shape
    cuts = list(np.cumsum(IN_SIZES)[:-1])
    fq, fk, fv, f_logit, c_q, c_kv, k_rope, gate_logit = jnp.split(u @ w_in, cuts, axis=-1)

    fq = rms_norm(split_heads(fq, FOX_HEADS), fox_q_norm)
    fk = rms_norm(split_heads(fk, FOX_HEADS), fox_k_norm)
    fv = split_heads(fv, FOX_HEADS)
    log_f = jax.nn.log_sigmoid(f_logit.astype(jnp.float32) + b_forget.astype(jnp.float32))
    log_f_cum = jnp.cumsum(log_f.transpose(0, 2, 1), axis=-1)
    o_fox = merge_heads(block_causal_attention(fq, fk, fv, log_f_cum)) @ w_branch_fox

    q = split_heads(rms_norm(c_q, mla_cq_norm) @ mla_w_uq, MLA_HEADS)
    kv = split_heads(rms_norm(c_kv, mla_ckv_norm) @ mla_w_ukv, MLA_HEADS)
    k_nope, v = jnp.split(kv, [MLA_NOPE_DIM], axis=-1)
    k_rope_h = jnp.broadcast_to(k_rope[:, None], (B, MLA_HEADS, L, MLA_ROPE_DIM))
    k = jnp.concatenate([k_nope, k_rope_h], axis=-1)
    q = rms_norm(q, mla_q_norm)
    k = rms_norm(k, mla_k_norm)
    pos = jnp.arange(L, dtype=jnp.float32)
    inv_freq = ROPE_THETA ** (-jnp.arange(0, MLA_ROPE_DIM, 2, dtype=jnp.float32) / MLA_ROPE_DIM)
    ang = pos[:, None] * inv_freq[None, :]
    cos, sin = jnp.cos(ang), jnp.sin(ang)
    q = jnp.concatenate([q[..., :MLA_NOPE_DIM], rope(q[..., MLA_NOPE_DIM:], cos, sin)], axis=-1)
    k = jnp.concatenate([k[..., :MLA_NOPE_DIM], rope(k[..., MLA_NOPE_DIM:], cos, sin)], axis=-1)
    o_mla = merge_heads(block_causal_attention(q, k, v)) @ w_branch_mla

    g_fox, g_mla = jnp.split(jax.nn.sigmoid(gate_logit + b_gate), N_BRANCH, axis=-1)
    return (g_fox * o_fox + g_mla * o_mla) @ w_out


def _fwd_setup_inputs(seed: int = 0) -> dict:
    key = jax.random.key(seed)
    ks = iter(jax.random.split(key, 32))

    def w(shape, fan_in):
        return jax.random.normal(next(ks), shape, jnp.float32) * fan_in ** -0.5

    def gain(shape):
        return 1.0 + 0.1 * jax.random.normal(next(ks), shape, jnp.float32)

    def bias(shape, scale=0.1, center=0.0):
        return center + scale * jax.random.normal(next(ks), shape, jnp.float32)

    Dp = DEPTH
    return {
        "x": jax.random.normal(next(ks), (BATCH, SEQ, D_MODEL), jnp.float32),
        "meta_tokens": jax.random.normal(next(ks), (N_META, D_MODEL), jnp.float32),
        "ffn1_norm": gain((Dp, D_MODEL)),
        "ffn1_w_gu": w((Dp, D_MODEL, 2 * D_FF), D_MODEL),
        "ffn1_w_down": w((Dp, D_FF, D_MODEL), D_FF),
        "mix_norm": gain((Dp, D_MODEL)),
        "w_in": w((Dp, D_MODEL, IN_W), D_MODEL),
        "b_forget": bias((Dp, FOX_HEADS), 0.1, 2.0),
        "b_gate": bias((Dp, N_BRANCH * D_MODEL)),
        "fox_q_norm": gain((Dp, FOX_HEAD_DIM)),
        "fox_k_norm": gain((Dp, FOX_HEAD_DIM)),
        "mla_cq_norm": gain((Dp, MLA_Q_RANK)),
        "mla_w_uq": w((Dp, MLA_Q_RANK, MLA_HEADS * MLA_QK_DIM), MLA_Q_RANK),
        "mla_ckv_norm": gain((Dp, MLA_KV_RANK)),
        "mla_w_ukv": w((Dp, MLA_KV_RANK, MLA_HEADS * (MLA_NOPE_DIM + MLA_V_DIM)), MLA_KV_RANK),
        "mla_q_norm": gain((Dp, MLA_QK_DIM)),
        "mla_k_norm": gain((Dp, MLA_QK_DIM)),
        "w_branch_fox": w((Dp, FOX_W, D_MODEL), FOX_W),
        "w_branch_mla": w((Dp, MLA_W, D_MODEL), MLA_W),
        "w_out": w((Dp, D_MODEL, D_MODEL), D_MODEL),
        "ffn2_norm": gain((Dp, D_MODEL)),
        "ffn2_w_gu": w((Dp, D_MODEL, 2 * D_FF), D_MODEL),
        "ffn2_w_down": w((Dp, D_FF, D_MODEL), D_FF),
    }


def _fwd_reference(x, meta_tokens, ffn1_norm, ffn1_w_gu, ffn1_w_down, mix_norm, w_in, b_forget, b_gate,
              fox_q_norm, fox_k_norm, mla_cq_norm, mla_w_uq, mla_ckv_norm, mla_w_ukv, mla_q_norm,
              mla_k_norm, w_branch_fox, w_branch_mla, w_out, ffn2_norm, ffn2_w_gu, ffn2_w_down):
    B = x.shape[0]
    meta = jnp.broadcast_to(meta_tokens.astype(x.dtype)[None], (B, N_META, D_MODEL))
    h = jnp.concatenate([meta, x], axis=1)
    for l in range(DEPTH):
        h = h + 0.5 * swiglu_ffn(rms_norm(h, ffn1_norm[l]), ffn1_w_gu[l], ffn1_w_down[l])
        h = h + token_mixing(rms_norm(h, mix_norm[l]), w_in[l], b_forget[l], b_gate[l],
                             fox_q_norm[l], fox_k_norm[l], mla_cq_norm[l], mla_w_uq[l],
                             mla_ckv_norm[l], mla_w_ukv[l], mla_q_norm[l], mla_k_norm[l],
                             w_branch_fox[l], w_branch_mla[l], w_out[l])
        h = h + 0.5 * swiglu_ffn(rms_norm(h, ffn2_norm[l]), ffn2_w_gu[l], ffn2_w_down[l])
    return h[:, N_META:]


import jax as _jax
import jax.numpy as _jnp

TWIN_FORMAT = 'train_step'
FWD_PARAMS = ['x', 'meta_tokens', 'ffn1_norm', 'ffn1_w_gu', 'ffn1_w_down', 'mix_norm', 'w_in', 'b_forget', 'b_gate', 'fox_q_norm', 'fox_k_norm', 'mla_cq_norm', 'mla_w_uq', 'mla_ckv_norm', 'mla_w_ukv', 'mla_q_norm', 'mla_k_norm', 'w_branch_fox', 'w_branch_mla', 'w_out', 'ffn2_norm', 'ffn2_w_gu', 'ffn2_w_down']
TWIN_WEIGHTS = ['meta_tokens', 'ffn1_norm', 'ffn1_w_gu', 'ffn1_w_down', 'mix_norm', 'w_in', 'b_forget', 'b_gate', 'fox_q_norm', 'fox_k_norm', 'mla_cq_norm', 'mla_w_uq', 'mla_ckv_norm', 'mla_w_ukv', 'mla_q_norm', 'mla_k_norm', 'w_branch_fox', 'w_branch_mla', 'w_out', 'ffn2_norm', 'ffn2_w_gu', 'ffn2_w_down']
TWIN_DIFF_INPUT = 'x'
TWIN_INPUTS = ['x', 'meta_tokens', 'ffn1_norm', 'ffn1_w_gu', 'ffn1_w_down', 'mix_norm', 'w_in', 'b_forget', 'b_gate', 'fox_q_norm', 'fox_k_norm', 'mla_cq_norm', 'mla_w_uq', 'mla_ckv_norm', 'mla_w_ukv', 'mla_q_norm', 'mla_k_norm', 'w_branch_fox', 'w_branch_mla', 'w_out', 'ffn2_norm', 'ffn2_w_gu', 'ffn2_w_down', 'loss_target', 'm_meta_tokens', 'm_ffn1_norm', 'm_ffn1_w_gu', 'm_ffn1_w_down', 'm_mix_norm', 'm_w_in', 'm_b_forget', 'm_b_gate', 'm_fox_q_norm', 'm_fox_k_norm', 'm_mla_cq_norm', 'm_mla_w_uq', 'm_mla_ckv_norm', 'm_mla_w_ukv', 'm_mla_q_norm', 'm_mla_k_norm', 'm_w_branch_fox', 'm_w_branch_mla', 'm_w_out', 'm_ffn2_norm', 'm_ffn2_w_gu', 'm_ffn2_w_down', 'v_meta_tokens', 'v_ffn1_norm', 'v_ffn1_w_gu', 'v_ffn1_w_down', 'v_mix_norm', 'v_w_in', 'v_b_forget', 'v_b_gate', 'v_fox_q_norm', 'v_fox_k_norm', 'v_mla_cq_norm', 'v_mla_w_uq', 'v_mla_ckv_norm', 'v_mla_w_ukv', 'v_mla_q_norm', 'v_mla_k_norm', 'v_w_branch_fox', 'v_w_branch_mla', 'v_w_out', 'v_ffn2_norm', 'v_ffn2_w_gu', 'v_ffn2_w_down']
TWIN_OUTPUTS = ['loss', 'grad_x', 'grad_meta_tokens', 'grad_ffn1_norm', 'grad_ffn1_w_gu', 'grad_ffn1_w_down', 'grad_mix_norm', 'grad_w_in', 'grad_b_forget', 'grad_b_gate', 'grad_fox_q_norm', 'grad_fox_k_norm', 'grad_mla_cq_norm', 'grad_mla_w_uq', 'grad_mla_ckv_norm', 'grad_mla_w_ukv', 'grad_mla_q_norm', 'grad_mla_k_norm', 'grad_w_branch_fox', 'grad_w_branch_mla', 'grad_w_out', 'grad_ffn2_norm', 'grad_ffn2_w_gu', 'grad_ffn2_w_down', 'delta_meta_tokens', 'delta_ffn1_norm', 'delta_ffn1_w_gu', 'delta_ffn1_w_down', 'delta_mix_norm', 'delta_w_in', 'delta_b_forget', 'delta_b_gate', 'delta_fox_q_norm', 'delta_fox_k_norm', 'delta_mla_cq_norm', 'delta_mla_w_uq', 'delta_mla_ckv_norm', 'delta_mla_w_ukv', 'delta_mla_q_norm', 'delta_mla_k_norm', 'delta_w_branch_fox', 'delta_w_branch_mla', 'delta_w_out', 'delta_ffn2_norm', 'delta_ffn2_w_gu', 'delta_ffn2_w_down', 'new_m_meta_tokens', 'new_m_ffn1_norm', 'new_m_ffn1_w_gu', 'new_m_ffn1_w_down', 'new_m_mix_norm', 'new_m_w_in', 'new_m_b_forget', 'new_m_b_gate', 'new_m_fox_q_norm', 'new_m_fox_k_norm', 'new_m_mla_cq_norm', 'new_m_mla_w_uq', 'new_m_mla_ckv_norm', 'new_m_mla_w_ukv', 'new_m_mla_q_norm', 'new_m_mla_k_norm', 'new_m_w_branch_fox', 'new_m_w_branch_mla', 'new_m_w_out', 'new_m_ffn2_norm', 'new_m_ffn2_w_gu', 'new_m_ffn2_w_down', 'new_v_meta_tokens', 'new_v_ffn1_norm', 'new_v_ffn1_w_gu', 'new_v_ffn1_w_down', 'new_v_mix_norm', 'new_v_w_in', 'new_v_b_forget', 'new_v_b_gate', 'new_v_fox_q_norm', 'new_v_fox_k_norm', 'new_v_mla_cq_norm', 'new_v_mla_w_uq', 'new_v_mla_ckv_norm', 'new_v_mla_w_ukv', 'new_v_mla_q_norm', 'new_v_mla_k_norm', 'new_v_w_branch_fox', 'new_v_w_branch_mla', 'new_v_w_out', 'new_v_ffn2_norm', 'new_v_ffn2_w_gu', 'new_v_ffn2_w_down']
TWIN_LEAF_KINDS = {'loss': 'loss', 'grad_x': 'grad_x', 'grad_meta_tokens': 'grad_w', 'grad_ffn1_norm': 'grad_w', 'grad_ffn1_w_gu': 'grad_w', 'grad_ffn1_w_down': 'grad_w', 'grad_mix_norm': 'grad_w', 'grad_w_in': 'grad_w', 'grad_b_forget': 'grad_w', 'grad_b_gate': 'grad_w', 'grad_fox_q_norm': 'grad_w', 'grad_fox_k_norm': 'grad_w', 'grad_mla_cq_norm': 'grad_w', 'grad_mla_w_uq': 'grad_w', 'grad_mla_ckv_norm': 'grad_w', 'grad_mla_w_ukv': 'grad_w', 'grad_mla_q_norm': 'grad_w', 'grad_mla_k_norm': 'grad_w', 'grad_w_branch_fox': 'grad_w', 'grad_w_branch_mla': 'grad_w', 'grad_w_out': 'grad_w', 'grad_ffn2_norm': 'grad_w', 'grad_ffn2_w_gu': 'grad_w', 'grad_ffn2_w_down': 'grad_w', 'delta_meta_tokens': 'delta_w', 'delta_ffn1_norm': 'delta_w', 'delta_ffn1_w_gu': 'delta_w', 'delta_ffn1_w_down': 'delta_w', 'delta_mix_norm': 'delta_w', 'delta_w_in': 'delta_w', 'delta_b_forget': 'delta_w', 'delta_b_gate': 'delta_w', 'delta_fox_q_norm': 'delta_w', 'delta_fox_k_norm': 'delta_w', 'delta_mla_cq_norm': 'delta_w', 'delta_mla_w_uq': 'delta_w', 'delta_mla_ckv_norm': 'delta_w', 'delta_mla_w_ukv': 'delta_w', 'delta_mla_q_norm': 'delta_w', 'delta_mla_k_norm': 'delta_w', 'delta_w_branch_fox': 'delta_w', 'delta_w_branch_mla': 'delta_w', 'delta_w_out': 'delta_w', 'delta_ffn2_norm': 'delta_w', 'delta_ffn2_w_gu': 'delta_w', 'delta_ffn2_w_down': 'delta_w', 'new_m_meta_tokens': 'new_m', 'new_m_ffn1_norm': 'new_m', 'new_m_ffn1_w_gu': 'new_m', 'new_m_ffn1_w_down': 'new_m', 'new_m_mix_norm': 'new_m', 'new_m_w_in': 'new_m', 'new_m_b_forget': 'new_m', 'new_m_b_gate': 'new_m', 'new_m_fox_q_norm': 'new_m', 'new_m_fox_k_norm': 'new_m', 'new_m_mla_cq_norm': 'new_m', 'new_m_mla_w_uq': 'new_m', 'new_m_mla_ckv_norm': 'new_m', 'new_m_mla_w_ukv': 'new_m', 'new_m_mla_q_norm': 'new_m', 'new_m_mla_k_norm': 'new_m', 'new_m_w_branch_fox': 'new_m', 'new_m_w_branch_mla': 'new_m', 'new_m_w_out': 'new_m', 'new_m_ffn2_norm': 'new_m', 'new_m_ffn2_w_gu': 'new_m', 'new_m_ffn2_w_down': 'new_m', 'new_v_meta_tokens': 'new_v', 'new_v_ffn1_norm': 'new_v', 'new_v_ffn1_w_gu': 'new_v', 'new_v_ffn1_w_down': 'new_v', 'new_v_mix_norm': 'new_v', 'new_v_w_in': 'new_v', 'new_v_b_forget': 'new_v', 'new_v_b_gate': 'new_v', 'new_v_fox_q_norm': 'new_v', 'new_v_fox_k_norm': 'new_v', 'new_v_mla_cq_norm': 'new_v', 'new_v_mla_w_uq': 'new_v', 'new_v_mla_ckv_norm': 'new_v', 'new_v_mla_w_ukv': 'new_v', 'new_v_mla_q_norm': 'new_v', 'new_v_mla_k_norm': 'new_v', 'new_v_w_branch_fox': 'new_v', 'new_v_w_branch_mla': 'new_v', 'new_v_w_out': 'new_v', 'new_v_ffn2_norm': 'new_v', 'new_v_ffn2_w_gu': 'new_v', 'new_v_ffn2_w_down': 'new_v'}


def _forward(args):
    return _fwd_reference(*[args[k] for k in FWD_PARAMS])


def _output_shape():
    out = _jax.eval_shape(lambda: _forward(_fwd_setup_inputs(0)))
    return out.shape, out.dtype

N_MICROBATCH = 1
ADAM_LR = 0.001
ADAM_B1 = 0.9
ADAM_B2 = 0.999
ADAM_EPS = 1e-08
ADAM_WD = 0.01
ADAM_STEP = 10
PER_EXAMPLE_BATCH_AXIS = {'x': 0, 'loss_target': 0}
SHARED_INPUTS = []
_WEIGHT_DTYPES = {'meta_tokens': _jnp.float32, 'ffn1_norm': _jnp.float32, 'ffn1_w_gu': _jnp.float32, 'ffn1_w_down': _jnp.float32, 'mix_norm': _jnp.float32, 'w_in': _jnp.float32, 'b_forget': _jnp.float32, 'b_gate': _jnp.float32, 'fox_q_norm': _jnp.float32, 'fox_k_norm': _jnp.float32, 'mla_cq_norm': _jnp.float32, 'mla_w_uq': _jnp.float32, 'mla_ckv_norm': _jnp.float32, 'mla_w_ukv': _jnp.float32, 'mla_q_norm': _jnp.float32, 'mla_k_norm': _jnp.float32, 'w_branch_fox': _jnp.float32, 'w_branch_mla': _jnp.float32, 'w_out': _jnp.float32, 'ffn2_norm': _jnp.float32, 'ffn2_w_gu': _jnp.float32, 'ffn2_w_down': _jnp.float32}
MOMENT_SCALE = {'meta_tokens': 1.341824e-02, 'ffn1_norm': 1.254851e+01, 'ffn1_w_gu': 9.991378e-02, 'ffn1_w_down': 1.723582e-01, 'mix_norm': 3.969793e+00, 'w_in': 1.194009e-01, 'b_forget': 1.323377e+02, 'b_gate': 7.499842e-01, 'fox_q_norm': 1.935699e+01, 'fox_k_norm': 1.934364e+01, 'mla_cq_norm': 8.397915e-02, 'mla_w_uq': 5.003927e-02, 'mla_ckv_norm': 7.758547e-01, 'mla_w_ukv': 6.598331e-02, 'mla_q_norm': 9.294710e-01, 'mla_k_norm': 9.113184e-01, 'w_branch_fox': 1.563801e-01, 'w_branch_mla': 4.897283e-02, 'w_out': 1.411476e-01, 'ffn2_norm': 1.283958e+01, 'ffn2_w_gu': 9.147620e-02, 'ffn2_w_down': 1.609867e-01}


def _to_microbatches(a, axis):
    t = _jnp.moveaxis(a, axis, 0)
    t = t.reshape((N_MICROBATCH, t.shape[0] // N_MICROBATCH) + t.shape[1:])
    return _jnp.moveaxis(t, 1, axis + 1)


def setup_inputs(seed: int = 0) -> dict:
    inp = _fwd_setup_inputs(seed)
    key = _jax.random.fold_in(_jax.random.key(seed), 7919)
    shape, _ = _output_shape()
    out = dict(inp)
    out["loss_target"] = _jax.random.normal(_jax.random.fold_in(key, 0), shape, _jnp.float32)
    for i, name in enumerate(TWIN_WEIGHTS):
        w = inp[name].astype(_jnp.float32)
        if MOMENT_SCALE is None:
            s = _jnp.sqrt(_jnp.mean(_jnp.square(w)) + 1e-30)
        else:
            s = MOMENT_SCALE[name]
        km, kv = _jax.random.split(_jax.random.fold_in(key, i + 1))
        out[name] = w
        out["m_" + name] = s * _jax.random.normal(km, w.shape, _jnp.float32)
        out["v_" + name] = (s * s) * _jax.random.uniform(kv, w.shape, _jnp.float32, 0.5, 1.5)
    if N_MICROBATCH > 1:
        for name, axis in PER_EXAMPLE_BATCH_AXIS.items():
            out[name] = _to_microbatches(out[name], axis)
    return {'x': out['x'], 'meta_tokens': out['meta_tokens'], 'ffn1_norm': out['ffn1_norm'], 'ffn1_w_gu': out['ffn1_w_gu'], 'ffn1_w_down': out['ffn1_w_down'], 'mix_norm': out['mix_norm'], 'w_in': out['w_in'], 'b_forget': out['b_forget'], 'b_gate': out['b_gate'], 'fox_q_norm': out['fox_q_norm'], 'fox_k_norm': out['fox_k_norm'], 'mla_cq_norm': out['mla_cq_norm'], 'mla_w_uq': out['mla_w_uq'], 'mla_ckv_norm': out['mla_ckv_norm'], 'mla_w_ukv': out['mla_w_ukv'], 'mla_q_norm': out['mla_q_norm'], 'mla_k_norm': out['mla_k_norm'], 'w_branch_fox': out['w_branch_fox'], 'w_branch_mla': out['w_branch_mla'], 'w_out': out['w_out'], 'ffn2_norm': out['ffn2_norm'], 'ffn2_w_gu': out['ffn2_w_gu'], 'ffn2_w_down': out['ffn2_w_down'], 'loss_target': out['loss_target'], 'm_meta_tokens': out['m_meta_tokens'], 'm_ffn1_norm': out['m_ffn1_norm'], 'm_ffn1_w_gu': out['m_ffn1_w_gu'], 'm_ffn1_w_down': out['m_ffn1_w_down'], 'm_mix_norm': out['m_mix_norm'], 'm_w_in': out['m_w_in'], 'm_b_forget': out['m_b_forget'], 'm_b_gate': out['m_b_gate'], 'm_fox_q_norm': out['m_fox_q_norm'], 'm_fox_k_norm': out['m_fox_k_norm'], 'm_mla_cq_norm': out['m_mla_cq_norm'], 'm_mla_w_uq': out['m_mla_w_uq'], 'm_mla_ckv_norm': out['m_mla_ckv_norm'], 'm_mla_w_ukv': out['m_mla_w_ukv'], 'm_mla_q_norm': out['m_mla_q_norm'], 'm_mla_k_norm': out['m_mla_k_norm'], 'm_w_branch_fox': out['m_w_branch_fox'], 'm_w_branch_mla': out['m_w_branch_mla'], 'm_w_out': out['m_w_out'], 'm_ffn2_norm': out['m_ffn2_norm'], 'm_ffn2_w_gu': out['m_ffn2_w_gu'], 'm_ffn2_w_down': out['m_ffn2_w_down'], 'v_meta_tokens': out['v_meta_tokens'], 'v_ffn1_norm': out['v_ffn1_norm'], 'v_ffn1_w_gu': out['v_ffn1_w_gu'], 'v_ffn1_w_down': out['v_ffn1_w_down'], 'v_mix_norm': out['v_mix_norm'], 'v_w_in': out['v_w_in'], 'v_b_forget': out['v_b_forget'], 'v_b_gate': out['v_b_gate'], 'v_fox_q_norm': out['v_fox_q_norm'], 'v_fox_k_norm': out['v_fox_k_norm'], 'v_mla_cq_norm': out['v_mla_cq_norm'], 'v_mla_w_uq': out['v_mla_w_uq'], 'v_mla_ckv_norm': out['v_mla_ckv_norm'], 'v_mla_w_ukv': out['v_mla_w_ukv'], 'v_mla_q_norm': out['v_mla_q_norm'], 'v_mla_k_norm': out['v_mla_k_norm'], 'v_w_branch_fox': out['v_w_branch_fox'], 'v_w_branch_mla': out['v_w_branch_mla'], 'v_w_out': out['v_w_out'], 'v_ffn2_norm': out['v_ffn2_norm'], 'v_ffn2_w_gu': out['v_ffn2_w_gu'], 'v_ffn2_w_down': out['v_ffn2_w_down']}


def _loss(weights, diff, rest, loss_target):
    with _jax.named_scope("forward"):
        args = {**rest, TWIN_DIFF_INPUT: diff, **{k: w.astype(_WEIGHT_DTYPES[k]) for k, w in weights.items()}}
        y = _forward(args)
    with _jax.named_scope("loss_head"):
        err = _jnp.square(y.astype(_jnp.float32) - loss_target)
        return 0.5 * _jnp.sum(_jnp.mean(err, axis=-1)) if err.ndim else 0.5 * err


def _adamw(w, g, m, v):
    m = ADAM_B1 * m + (1.0 - ADAM_B1) * g
    v = ADAM_B2 * v + (1.0 - ADAM_B2) * _jnp.square(g)
    m_hat = m / (1.0 - ADAM_B1 ** ADAM_STEP)
    v_hat = v / (1.0 - ADAM_B2 ** ADAM_STEP)
    delta = -ADAM_LR * (m_hat / (_jnp.sqrt(v_hat) + ADAM_EPS) + ADAM_WD * w)
    return delta, m, v


def reference(x, meta_tokens, ffn1_norm, ffn1_w_gu, ffn1_w_down, mix_norm, w_in, b_forget, b_gate, fox_q_norm, fox_k_norm, mla_cq_norm, mla_w_uq, mla_ckv_norm, mla_w_ukv, mla_q_norm, mla_k_norm, w_branch_fox, w_branch_mla, w_out, ffn2_norm, ffn2_w_gu, ffn2_w_down, loss_target, m_meta_tokens, m_ffn1_norm, m_ffn1_w_gu, m_ffn1_w_down, m_mix_norm, m_w_in, m_b_forget, m_b_gate, m_fox_q_norm, m_fox_k_norm, m_mla_cq_norm, m_mla_w_uq, m_mla_ckv_norm, m_mla_w_ukv, m_mla_q_norm, m_mla_k_norm, m_w_branch_fox, m_w_branch_mla, m_w_out, m_ffn2_norm, m_ffn2_w_gu, m_ffn2_w_down, v_meta_tokens, v_ffn1_norm, v_ffn1_w_gu, v_ffn1_w_down, v_mix_norm, v_w_in, v_b_forget, v_b_gate, v_fox_q_norm, v_fox_k_norm, v_mla_cq_norm, v_mla_w_uq, v_mla_ckv_norm, v_mla_w_ukv, v_mla_q_norm, v_mla_k_norm, v_w_branch_fox, v_w_branch_mla, v_w_out, v_ffn2_norm, v_ffn2_w_gu, v_ffn2_w_down):
    given = dict(x=x, meta_tokens=meta_tokens, ffn1_norm=ffn1_norm, ffn1_w_gu=ffn1_w_gu, ffn1_w_down=ffn1_w_down, mix_norm=mix_norm, w_in=w_in, b_forget=b_forget, b_gate=b_gate, fox_q_norm=fox_q_norm, fox_k_norm=fox_k_norm, mla_cq_norm=mla_cq_norm, mla_w_uq=mla_w_uq, mla_ckv_norm=mla_ckv_norm, mla_w_ukv=mla_w_ukv, mla_q_norm=mla_q_norm, mla_k_norm=mla_k_norm, w_branch_fox=w_branch_fox, w_branch_mla=w_branch_mla, w_out=w_out, ffn2_norm=ffn2_norm, ffn2_w_gu=ffn2_w_gu, ffn2_w_down=ffn2_w_down, loss_target=loss_target, m_meta_tokens=m_meta_tokens, m_ffn1_norm=m_ffn1_norm, m_ffn1_w_gu=m_ffn1_w_gu, m_ffn1_w_down=m_ffn1_w_down, m_mix_norm=m_mix_norm, m_w_in=m_w_in, m_b_forget=m_b_forget, m_b_gate=m_b_gate, m_fox_q_norm=m_fox_q_norm, m_fox_k_norm=m_fox_k_norm, m_mla_cq_norm=m_mla_cq_norm, m_mla_w_uq=m_mla_w_uq, m_mla_ckv_norm=m_mla_ckv_norm, m_mla_w_ukv=m_mla_w_ukv, m_mla_q_norm=m_mla_q_norm, m_mla_k_norm=m_mla_k_norm, m_w_branch_fox=m_w_branch_fox, m_w_branch_mla=m_w_branch_mla, m_w_out=m_w_out, m_ffn2_norm=m_ffn2_norm, m_ffn2_w_gu=m_ffn2_w_gu, m_ffn2_w_down=m_ffn2_w_down, v_meta_tokens=v_meta_tokens, v_ffn1_norm=v_ffn1_norm, v_ffn1_w_gu=v_ffn1_w_gu, v_ffn1_w_down=v_ffn1_w_down, v_mix_norm=v_mix_norm, v_w_in=v_w_in, v_b_forget=v_b_forget, v_b_gate=v_b_gate, v_fox_q_norm=v_fox_q_norm, v_fox_k_norm=v_fox_k_norm, v_mla_cq_norm=v_mla_cq_norm, v_mla_w_uq=v_mla_w_uq, v_mla_ckv_norm=v_mla_ckv_norm, v_mla_w_ukv=v_mla_w_ukv, v_mla_q_norm=v_mla_q_norm, v_mla_k_norm=v_mla_k_norm, v_w_branch_fox=v_w_branch_fox, v_w_branch_mla=v_w_branch_mla, v_w_out=v_w_out, v_ffn2_norm=v_ffn2_norm, v_ffn2_w_gu=v_ffn2_w_gu, v_ffn2_w_down=v_ffn2_w_down)
    weights = {n: given[n] for n in TWIN_WEIGHTS}
    shared = {n: given[n] for n in SHARED_INPUTS}
    per_example = {n: given[n] for n in ['x']}
    grad_fn = _jax.value_and_grad(_loss, argnums=(0, 1))

    def one_microbatch(ex, loss_target):
        ex = dict(ex)
        diff = ex.pop(TWIN_DIFF_INPUT)
        return grad_fn(weights, diff, {**shared, **ex}, loss_target)

    if N_MICROBATCH == 1:
        loss, (grad_w, grad_x) = one_microbatch(per_example, given["loss_target"])
    else:
        def body(carry, xs):
            loss_sum, grad_sum = carry
            l_k, (gw_k, gx_k) = one_microbatch(xs[0], xs[1])
            with _jax.named_scope("update"):
                return (loss_sum + l_k, _jax.tree.map(_jnp.add, grad_sum, gw_k)), gx_k

        init = (_jnp.zeros((), _jnp.float32), _jax.tree.map(_jnp.zeros_like, weights))
        (loss, grad_w), grad_x = _jax.lax.scan(body, init, (per_example, given["loss_target"]))
    with _jax.named_scope("update"):
        delta_w, new_m, new_v = {}, {}, {}
        for n in TWIN_WEIGHTS:
            delta_w[n], new_m[n], new_v[n] = _adamw(weights[n], grad_w[n], given["m_" + n], given["v_" + n])
    return (loss, grad_x, *[grad_w[n] for n in TWIN_WEIGHTS], *[delta_w[n] for n in TWIN_WEIGHTS],
            *[new_m[n] for n in TWIN_WEIGHTS], *[new_v[n] for n in TWIN_WEIGHTS])
```

```python
import functools

import jax
import jax.numpy as jnp
from jax import lax
from jax.experimental import pallas as pl
from jax.experimental.pallas import tpu as pltpu

F32 = jnp.float32
BF16 = jnp.bfloat16
MESH = pl.DeviceIdType.MESH

D = 1024
DFF = 2816
FH = DFF // 2
NMETA = 16
MPAD = 128
EPS = 1e-6
NH = 8
FOXW = 512
QR = 256
KVR = 128
ROPE = 32
MLA_QK = 96
PROJW = 4096
ROPE_THETA = 10000.0
N_CHIPS = 4
N_DEV = 8

ADAM_LR = 0.001
ADAM_B1 = 0.9
ADAM_B2 = 0.999
ADAM_EPS = 1e-08
ADAM_WD = 0.01
ADAM_STEP = 10

VMEM_LIMIT = 56 * 2**20
RS_DTYPE = jnp.float32

NT = (((1,), (1,)), ((), ()))
TN = (((0,), (0,)), ((), ()))


def _call(body, **kw):
    return pl.pallas_call(body, **kw)


def _cp(**kw):
    return pltpu.CompilerParams(vmem_limit_bytes=VMEM_LIMIT, **kw)


def _tile(n, cands):
    for c in cands:
        if n % c == 0:
            return c
    raise ValueError(f"no tile for {n} among {cands}")


def _dot(a, b, dims=None):
    if dims is None:
        return jnp.dot(a, b, preferred_element_type=F32)
    return lax.dot_general(a, b, dims, preferred_element_type=F32)


def _rms(x, gain, n):
    r = lax.rsqrt(jnp.sum(x * x, axis=-1, keepdims=True) * (1.0 / n) + EPS)
    xh = x * r
    return xh * gain, xh, r


def _rms_bwd(dy, xh, r, gain, n):
    dxh = dy * gain
    return r * (dxh - xh * (jnp.sum(dxh * xh, axis=-1, keepdims=True) * (1.0 / n)))


def _lane(shape):
    return lax.broadcasted_iota(jnp.int32, shape, len(shape) - 1)


def _half_sum(x):
    lo = _lane(x.shape) < 64
    s_lo = jnp.sum(jnp.where(lo, x, 0.0), axis=-1, keepdims=True)
    s_hi = jnp.sum(jnp.where(lo, 0.0, x), axis=-1, keepdims=True)
    return jnp.where(lo, s_lo, s_hi)


def _rope_swap(x):
    ln = _lane(x.shape)
    sw = jnp.where(ln < 80, pltpu.roll(x, 112, 1), pltpu.roll(x, 16, 1))
    return jnp.where(jnp.logical_and(ln >= 64, ln < 96), sw, 0.0)


def _colsum(x):
    return jnp.sum(x, axis=0, keepdims=True)


def _ffn_fwd(h, norm, wgu, wd3, name):
    T = h.shape[0]
    tm = _tile(T, (512, 384, 256, 128))

    def body(h_ref, n_ref, wg_ref, wu_ref, wd_ref, o_ref, u_sc):
        @pl.when(pl.program_id(1) == 0)
        def _():
            x = h_ref[...]
            u, _, _ = _rms(x, n_ref[...], D)
            u_sc[...] = u.astype(BF16)
            o_ref[...] = x

        u = u_sc[...]
        g = _dot(u, wg_ref[0])
        p = _dot(u, wu_ref[0])
        a = (g * jax.nn.sigmoid(g)) * p
        o_ref[...] += 0.5 * _dot(a.astype(BF16), wd_ref[0])

    return _call(
        body, name=name, grid=(T // tm, 2),
        in_specs=[pl.BlockSpec((tm, D), lambda i, j: (i, 0)),
                  pl.BlockSpec((1, D), lambda i, j: (0, 0)),
                  pl.BlockSpec((1, D, FH), lambda i, j: (j, 0, 0)),
                  pl.BlockSpec((1, D, FH), lambda i, j: (j + 2, 0, 0)),
                  pl.BlockSpec((1, FH, D), lambda i, j: (j, 0, 0))],
        out_specs=pl.BlockSpec((tm, D), lambda i, j: (i, 0)),
        out_shape=jax.ShapeDtypeStruct((T, D), F32),
        scratch_shapes=[pltpu.VMEM((tm, D), BF16)],
        compiler_params=_cp(),
    )(h, norm, wgu, wgu, wd3)


def _ffn_bwd(h, dout, norm, wgu, wd3, name):
    T = h.shape[0]
    tm = _tile(T, (256, 128))
    nt = T // tm

    def body(h_ref, d_ref, n_ref, wg_ref, wu_ref, wd_ref,
             dh_ref, u_ref, a_ref, dg_ref, dp_ref, dn_ref, xh_sc, r_sc, u_sc, du_sc):
        i = pl.program_id(0)
        j = pl.program_id(1)

        @pl.when(jnp.logical_and(i == 0, j == 0))
        def _():
            dn_ref[...] = jnp.zeros_like(dn_ref)

        @pl.when(j == 0)
        def _():
            x = h_ref[...]
            u, xh, r = _rms(x, n_ref[...], D)
            xh_sc[...] = xh
            r_sc[...] = r
            ub = u.astype(BF16)
            u_sc[...] = ub
            u_ref[...] = ub
            du_sc[...] = jnp.zeros_like(du_sc)

        u = u_sc[...]
        g = _dot(u, wg_ref[0])
        p = _dot(u, wu_ref[0])
        s = jax.nn.sigmoid(g)
        sl = g * s
        dz = (0.5 * d_ref[...]).astype(BF16)
        da = _dot(dz, wd_ref[0], NT)
        dp = da * sl
        dg = (da * p) * (s * (1.0 + g * (1.0 - s)))
        a_ref[...] = (sl * p).astype(BF16)
        dgb = dg.astype(BF16)
        dpb = dp.astype(BF16)
        dg_ref[...] = dgb
        dp_ref[...] = dpb
        du_sc[...] += _dot(dgb, wg_ref[0], NT) + _dot(dpb, wu_ref[0], NT)

        @pl.when(j == 1)
        def _():
            du = du_sc[...]
            xh = xh_sc[...]
            dn_ref[...] += _colsum(du * xh)
            dh_ref[...] = d_ref[...] + _rms_bwd(du, xh, r_sc[...], n_ref[...], D)

    return _call(
        body, name=name, grid=(nt, 2),
        in_specs=[pl.BlockSpec((tm, D), lambda i, j: (i, 0)),
                  pl.BlockSpec((tm, D), lambda i, j: (i, 0)),
                  pl.BlockSpec((1, D), lambda i, j: (0, 0)),
                  pl.BlockSpec((1, D, FH), lambda i, j: (j, 0, 0)),
                  pl.BlockSpec((1, D, FH), lambda i, j: (j + 2, 0, 0)),
                  pl.BlockSpec((1, FH, D), lambda i, j: (j, 0, 0))],
        out_specs=[pl.BlockSpec((tm, D), lambda i, j: (i, 0)),
                   pl.BlockSpec((tm, D), lambda i, j: (i, 0)),
                   pl.BlockSpec((tm, FH), lambda i, j: (i, j)),
                   pl.BlockSpec((tm, FH), lambda i, j: (i, j)),
                   pl.BlockSpec((tm, FH), lambda i, j: (i, j)),
                   pl.BlockSpec((1, D), lambda i, j: (0, 0))],
        out_shape=[jax.ShapeDtypeStruct((T, D), F32),
                   jax.ShapeDtypeStruct((T, D), BF16),
                   jax.ShapeDtypeStruct((T, DFF), BF16),
                   jax.ShapeDtypeStruct((T, DFF), BF16),
                   jax.ShapeDtypeStruct((T, DFF), BF16),
                   jax.ShapeDtypeStruct((1, D), F32)],
        scratch_shapes=[pltpu.VMEM((tm, D), F32), pltpu.VMEM((tm, 1), F32),
                        pltpu.VMEM((tm, D), BF16), pltpu.VMEM((tm, D), F32)],
        compiler_params=_cp(),
    )(h, dout, norm, wgu, wgu, wd3)


def _wgrad(x, y, name, scale=1.0, bk=None, bn=None, shard_major=False):
    T, K = x.shape
    N = y.shape[1]
    bk = bk or K
    bn = bn or N
    bt = _tile(T, (512, 384, 256, 128))
    nt = T // bt

    def body(x_ref, y_ref, o_ref):
        t = pl.program_id(2)

        @pl.when(t == 0)
        def _():
            o_ref[...] = jnp.zeros_like(o_ref)

        acc = _dot(x_ref[...].astype(BF16), y_ref[...].astype(BF16), TN)
        if shard_major:
            o_ref[0] += acc
        else:
            o_ref[...] += acc

        if scale != 1.0:
            @pl.when(t == nt - 1)
            def _():
                o_ref[...] = o_ref[...] * scale

    if shard_major:
        assert bk == K
        out_spec = pl.BlockSpec((1, K, bn), lambda i, j, t: (j, 0, 0))
        out_shape = jax.ShapeDtypeStruct((N // bn, K, bn), F32)
    else:
        out_spec = pl.BlockSpec((bk, bn), lambda i, j, t: (i, j))
        out_shape = jax.ShapeDtypeStruct((K, N), F32)
    return _call(
        body, name=name, grid=(K // bk, N // bn, nt),
        in_specs=[pl.BlockSpec((bt, bk), lambda i, j, t: (t, i)),
                  pl.BlockSpec((bt, bn), lambda i, j, t: (t, j))],
        out_specs=out_spec, out_shape=out_shape,
        compiler_params=_cp(),
    )(x, y)


def _inproj_fwd(h, norm, w, name):
    T = h.shape[0]
    tm = _tile(T, (512, 384, 256, 128))
    tn = 1024

    def body(h_ref, n_ref, w_ref, o_ref, u_ref):
        @pl.when(pl.program_id(1) == 0)
        def _():
            u, _, _ = _rms(h_ref[...], n_ref[...], D)
            u_ref[...] = u.astype(BF16)

        o_ref[...] = _dot(u_ref[...], w_ref[...])

    return _call(
        body, name=name, grid=(T // tm, PROJW // tn),
        in_specs=[pl.BlockSpec((tm, D), lambda i, j: (i, 0)),
                  pl.BlockSpec((1, D), lambda i, j: (0, 0)),
                  pl.BlockSpec((D, tn), lambda i, j: (0, j))],
        out_specs=[pl.BlockSpec((tm, tn), lambda i, j: (i, j)),
                   pl.BlockSpec((tm, D), lambda i, j: (i, 0))],
        out_shape=[jax.ShapeDtypeStruct((T, PROJW), F32), jax.ShapeDtypeStruct((T, D), BF16)],
        compiler_params=_cp(),
    )(h, norm, w)


def _inproj_bwd(h, dres, dlo, dhi, norm, w, name):
    T = h.shape[0]
    tm = _tile(T, (512, 384, 256, 128))
    hw = PROJW // 2

    def body(h_ref, d_ref, lo_ref, hi_ref, n_ref, wlo_ref, whi_ref, dh_ref, dn_ref):
        @pl.when(pl.program_id(0) == 0)
        def _():
            dn_ref[...] = jnp.zeros_like(dn_ref)

        _, xh, r = _rms(h_ref[...], n_ref[...], D)
        du = _dot(lo_ref[...], wlo_ref[...], NT) + _dot(hi_ref[...], whi_ref[...], NT)
        dn_ref[...] += _colsum(du * xh)
        dh_ref[...] = d_ref[...] + _rms_bwd(du, xh, r, n_ref[...], D)

    return _call(
        body, name=name, grid=(T // tm,),
        in_specs=[pl.BlockSpec((tm, D), lambda i: (i, 0)),
                  pl.BlockSpec((tm, D), lambda i: (i, 0)),
                  pl.BlockSpec((tm, hw), lambda i: (i, 0)),
                  pl.BlockSpec((tm, hw), lambda i: (i, 0)),
                  pl.BlockSpec((1, D), lambda i: (0, 0)),
                  pl.BlockSpec((D, hw), lambda i: (0, 0)),
                  pl.BlockSpec((D, hw), lambda i: (0, 1))],
        out_specs=[pl.BlockSpec((tm, D), lambda i: (i, 0)),
                   pl.BlockSpec((1, D), lambda i: (0, 0))],
        out_shape=[jax.ShapeDtypeStruct((T, D), F32), jax.ShapeDtypeStruct((1, D), F32)],
        compiler_params=_cp(),
    )(h, dres, dlo, dhi, norm, w, w)


C_FQ, C_FK, C_FV, C_CQ, C_CKV, C_MISC, C_GATE = 0, 512, 1024, 1536, 1792, 1920, 2048
L_KR, L_FL = 64, 96


def _prep_fwd(proj, rc, rs, gfq, gfk, gcq, gckv, gmq, gmk, bfv, wuq, wuk, wuv, name):
    T = proj.shape[0]
    tm = _tile(T, (256, 128))

    def body(p_ref, rc_ref, rs_ref, gfq_ref, gfk_ref, gcq_ref, gckv_ref, gmq_ref, gmk_ref, bf_ref,
             wuq_ref, wuk_ref, wuv_ref, fq_ref, fk_ref, fv_ref, qm_ref, km_ref, vm_ref, lf_ref):
        for blk in range(4):
            for (c0, g_ref, o_ref) in ((C_FQ, gfq_ref, fq_ref), (C_FK, gfk_ref, fk_ref)):
                x = p_ref[:, c0 + 128 * blk:c0 + 128 * (blk + 1)]
                r = lax.rsqrt(_half_sum(x * x) * (1.0 / 64) + EPS)
                o_ref[:, 128 * blk:128 * (blk + 1)] = (x * r * g_ref[...]).astype(BF16)
        fv_ref[...] = p_ref[:, C_FV:C_FV + 512].astype(BF16)

        rcv = rc_ref[...]
        rsv = rs_ref[...]
        cqn, _, _ = _rms(p_ref[:, C_CQ:C_CQ + QR], gcq_ref[...], QR)
        qpre = _dot(cqn.astype(BF16), wuq_ref[...])
        ckvn, _, _ = _rms(p_ref[:, C_CKV:C_CKV + KVR], gckv_ref[...], KVR)
        ckvb = ckvn.astype(BF16)
        kpre = _dot(ckvb, wuk_ref[...])
        vm_ref[...] = _dot(ckvb, wuv_ref[...]).astype(BF16)
        misc = p_ref[:, C_MISC:C_MISC + 128]
        ln = _lane(misc.shape)
        kr = jnp.where(jnp.logical_and(ln >= L_KR, ln < L_KR + ROPE), misc, 0.0)
        for hh in range(NH):
            sl = slice(128 * hh, 128 * (hh + 1))
            qn, _, _ = _rms(qpre[:, sl], gmq_ref[...], MLA_QK)
            qm_ref[:, sl] = (qn * rcv + _rope_swap(qn) * rsv).astype(BF16)
            kn, _, _ = _rms(kpre[:, sl] + kr, gmk_ref[...], MLA_QK)
            km_ref[:, sl] = (kn * rcv + _rope_swap(kn) * rsv).astype(BF16)
        z = misc + bf_ref[...]
        lf_ref[...] = jnp.minimum(z, 0.0) - jnp.log(1.0 + jnp.exp(-jnp.abs(z)))

    row = lambda w: pl.BlockSpec((tm, w), lambda i: (i, 0))
    full = lambda a: pl.BlockSpec(a.shape, lambda i: (0, 0))
    return _call(
        body, name=name, grid=(T // tm,),
        in_specs=[row(PROJW // 2), row(128), row(128)] + [full(a) for a in (gfq, gfk, gcq, gckv, gmq, gmk, bfv, wuq, wuk, wuv)],
        out_specs=[row(512), row(512), row(512), row(1024), row(1024), row(512), row(128)],
        out_shape=[jax.ShapeDtypeStruct((T, 512), BF16), jax.ShapeDtypeStruct((T, 512), BF16),
                   jax.ShapeDtypeStruct((T, 512), BF16), jax.ShapeDtypeStruct((T, 1024), BF16),
                   jax.ShapeDtypeStruct((T, 1024), BF16), jax.ShapeDtypeStruct((T, 512), BF16),
                   jax.ShapeDtypeStruct((T, 128), F32)],
        compiler_params=_cp(),
    )(proj, rc, rs, gfq, gfk, gcq, gckv, gmq, gmk, bfv, wuq, wuk, wuv)


def _prep_bwd(proj, rc, rs, gfq, gfk, gcq, gckv, gmq, gmk, bfv, wuq, wuk, wuv,
              dfq, dfk, dfv, dqm, dkm, dvm, dlf, name):
    T = proj.shape[0]
    tm = _tile(T, (256, 128))

    def body(p_ref, rc_ref, rs_ref, gfq_ref, gfk_ref, gcq_ref, gckv_ref, gmq_ref, gmk_ref, bf_ref,
             wuq_ref, wuk_ref, wuv_ref, dfq_ref, dfk_ref, dfv_ref, dqm_ref, dkm_ref, dvm_ref, dlf_ref,
             dp_ref, dgfq_ref, dgfk_ref, dgcq_ref, dgckv_ref, dgmq_ref, dgmk_ref, dbf_ref,
             dwuq_ref, dwuk_ref, dwuv_ref, dqpre_sc, dkpre_sc):
        accs = (dgfq_ref, dgfk_ref, dgcq_ref, dgckv_ref, dgmq_ref, dgmk_ref, dbf_ref, dwuq_ref, dwuk_ref, dwuv_ref)

        @pl.when(pl.program_id(0) == 0)
        def _():
            for a in accs:
                a[...] = jnp.zeros_like(a)

        for (c0, g_ref, d_ref, dg_ref) in ((C_FQ, gfq_ref, dfq_ref, dgfq_ref), (C_FK, gfk_ref, dfk_ref, dgfk_ref)):
            dg = jnp.zeros((1, 128), F32)
            for blk in range(4):
                x = p_ref[:, c0 + 128 * blk:c0 + 128 * (blk + 1)]
                r = lax.rsqrt(_half_sum(x * x) * (1.0 / 64) + EPS)
                xh = x * r
                dy = d_ref[:, 128 * blk:128 * (blk + 1)]
                dg = dg + _colsum(dy * xh)
                dxh = dy * g_ref[...]
                dx = r * (dxh - xh * (_half_sum(dxh * xh) * (1.0 / 64)))
                dp_ref[:, c0 + 128 * blk:c0 + 128 * (blk + 1)] = dx.astype(BF16)
            dg_ref[...] += dg
        dp_ref[:, C_FV:C_FV + 512] = dfv_ref[...].astype(BF16)

        rcv = rc_ref[...]
        rsv = rs_ref[...]
        cqn, cqh, cqr = _rms(p_ref[:, C_CQ:C_CQ + QR], gcq_ref[...], QR)
        cqb = cqn.astype(BF16)
        qpre = _dot(cqb, wuq_ref[...])
        dgq = jnp.zeros((1, 128), F32)
        for hh in range(NH):
            sl = slice(128 * hh, 128 * (hh + 1))
            _, xh, r = _rms(qpre[:, sl], gmq_ref[...], MLA_QK)
            dout = dqm_ref[:, sl]
            dqn = dout * rcv + _rope_swap(dout * rsv)
            dgq = dgq + _colsum(dqn * xh)
            dqpre_sc[:, sl] = _rms_bwd(dqn, xh, r, gmq_ref[...], MLA_QK).astype(BF16)
        dgmq_ref[...] += dgq
        dqpre = dqpre_sc[...]
        dwuq_ref[...] += _dot(cqb, dqpre, TN)
        dcqn = _dot(dqpre, wuq_ref[...], NT)
        dgcq_ref[...] += _colsum(dcqn * cqh)
        dp_ref[:, C_CQ:C_CQ + QR] = _rms_bwd(dcqn, cqh, cqr, gcq_ref[...], QR).astype(BF16)

        ckvn, ckvh, ckvr = _rms(p_ref[:, C_CKV:C_CKV + KVR], gckv_ref[...], KVR)
        ckvb = ckvn.astype(BF16)
        kpre = _dot(ckvb, wuk_ref[...])
        misc = p_ref[:, C_MISC:C_MISC + 128]
        ln = _lane(misc.shape)
        is_kr = jnp.logical_and(ln >= L_KR, ln < L_KR + ROPE)
        kr = jnp.where(is_kr, misc, 0.0)
        dgk = jnp.zeros((1, 128), F32)
        dkr = jnp.zeros(misc.shape, F32)
        for hh in range(NH):
            sl = slice(128 * hh, 128 * (hh + 1))
            _, xh, r = _rms(kpre[:, sl] + kr, gmk_ref[...], MLA_QK)
            dout = dkm_ref[:, sl]
            dkn = dout * rcv + _rope_swap(dout * rsv)
            dgk = dgk + _colsum(dkn * xh)
            dkx = _rms_bwd(dkn, xh, r, gmk_ref[...], MLA_QK)
            dkr = dkr + jnp.where(is_kr, dkx, 0.0)
            dkpre_sc[:, sl] = jnp.where(ln < 64, dkx, 0.0).astype(BF16)
        dgmk_ref[...] += dgk
        dkpre = dkpre_sc[...]
        dvmb = dvm_ref[...].astype(BF16)
        dwuk_ref[...] += _dot(ckvb, dkpre, TN)
        dwuv_ref[...] += _dot(ckvb, dvmb, TN)
        dckvn = _dot(dkpre, wuk_ref[...], NT) + _dot(dvmb, wuv_ref[...], NT)
        dgckv_ref[...] += _colsum(dckvn * ckvh)
        dp_ref[:, C_CKV:C_CKV + KVR] = _rms_bwd(dckvn, ckvh, ckvr, gckv_ref[...], KVR).astype(BF16)

        z = misc + bf_ref[...]
        dz = dlf_ref[...] * (1.0 - jax.nn.sigmoid(z))
        dbf_ref[...] += _colsum(dz)
        dp_ref[:, C_MISC:C_MISC + 128] = (dkr + dz).astype(BF16)

    row = lambda w: pl.BlockSpec((tm, w), lambda i: (i, 0))
    full = lambda a: pl.BlockSpec(a.shape, lambda i: (0, 0))
    small = (gfq, gfk, gcq, gckv, gmq, gmk, bfv, wuq, wuk, wuv)
    acc_shapes = [(1, 128), (1, 128), (1, QR), (1, KVR), (1, 128), (1, 128), (1, 128),
                  (QR, 1024), (KVR, 1024), (KVR, 512)]
    return _call(
        body, name=name, grid=(T // tm,),
        in_specs=[row(PROJW // 2), row(128), row(128)] + [full(a) for a in small]
                 + [row(512), row(512), row(512), row(1024), row(1024), row(512), row(128)],
        out_specs=[row(PROJW // 2)] + [pl.BlockSpec(s, lambda i: (0, 0)) for s in acc_shapes],
        out_shape=[jax.ShapeDtypeStruct((T, PROJW // 2), BF16)] + [jax.ShapeDtypeStruct(s, F32) for s in acc_shapes],
        scratch_shapes=[pltpu.VMEM((tm, 1024), BF16), pltpu.VMEM((tm, 1024), BF16)],
        compiler_params=_cp(),
    )(proj, rc, rs, *small, dfq, dfk, dfv, dqm, dkm, dvm, dlf)


def _scan_lanes(x, reverse):
    n = x.shape[-1]
    ln = _lane(x.shape)
    k = 1
    while k < n:
        if reverse:
            x = x + jnp.where(ln < n - k, pltpu.roll(x, n - k, x.ndim - 1), 0.0)
        else:
            x = x + jnp.where(ln >= k, pltpu.roll(x, k, x.ndim - 1), 0.0)
        k *= 2
    return x


def _forget_scan(lf, reverse, name):
    def body(x_ref, o_ref):
        x = x_ref[...]
        ln = _lane(x.shape)
        pad = jnp.logical_and(ln >= NMETA, ln < MPAD)
        o_ref[...] = jnp.where(pad, 0.0, _scan_lanes(jnp.where(pad, 0.0, x), reverse))

    return _call(body, name=name, out_shape=jax.ShapeDtypeStruct(lf.shape, F32), compiler_params=_cp())(lf)


def _attn_blocks(LP, tq):
    return [(0, MPAD, MPAD)] + [(MPAD + i * tq, tq, MPAD + (i + 1) * tq) for i in range((LP - MPAD) // tq)]


def _attn_scores(q_ref, k_ref, e, r0, rn, kend, wide, scale, bias):
    if wide:
        qe = q_ref[r0:r0 + rn, 128 * e:128 * (e + 1)]
        ke = k_ref[0:kend, 128 * e:128 * (e + 1)]
    else:
        qb = q_ref[r0:r0 + rn, :]
        mine = (_lane(qb.shape) < 64) if e == 0 else (_lane(qb.shape) >= 64)
        qe = jnp.where(mine, qb, jnp.zeros_like(qb))
        ke = k_ref[0:kend, :]
    s = _dot(qe, ke, NT) * scale
    if bias is not None:
        ct_ref, cr_ref = bias
        s = s + ct_ref[0, r0:r0 + rn, e:e + 1] - cr_ref[0, :, 0:kend]
    qi = r0 + lax.broadcasted_iota(jnp.int32, (rn, kend), 0)
    ki = lax.broadcasted_iota(jnp.int32, (rn, kend), 1)
    ok = jnp.logical_and(ki <= qi, jnp.logical_or(ki < NMETA, ki >= MPAD))
    s = jnp.where(ok, s, -1e30)
    m = jnp.max(s, axis=-1, keepdims=True)
    p = jnp.exp(s - m)
    l = jnp.sum(p, axis=-1, keepdims=True)
    return qe, ke, p, l


def _attn_specs(B, LP, wide, has_bias):
    qw = 256 if wide else 128
    specs = [pl.BlockSpec((LP, qw), lambda b, hp: (b, hp)),
             pl.BlockSpec((LP, qw), lambda b, hp: (b, hp)),
             pl.BlockSpec((LP, 128), lambda b, hp: (b, hp))]
    bias_specs = []
    if has_bias:
        bias_specs = [pl.BlockSpec((1, LP, 2), lambda b, hp: (b * 4 + hp, 0, 0)),
                      pl.BlockSpec((1, 1, LP), lambda b, hp: (b * 8 + 2 * hp, 0, 0)),
                      pl.BlockSpec((1, 1, LP), lambda b, hp: (b * 8 + 2 * hp + 1, 0, 0))]
    return qw, specs, bias_specs


def _attn_fwd(q, k, v, bias, B, LP, wide, scale, name):
    T = q.shape[0]
    tq = 256
    blocks = _attn_blocks(LP, tq)
    qw, specs, bias_specs = _attn_specs(B, LP, wide, bias is not None)

    def body(*refs):
        if bias is not None:
            q_ref, k_ref, v_ref, ct_ref, cr0_ref, cr1_ref, o_ref = refs
            crs = (cr0_ref, cr1_ref)
        else:
            q_ref, k_ref, v_ref, o_ref = refs
        for (r0, rn, kend) in blocks:
            outs = []
            for e in (0, 1):
                bs = (ct_ref, crs[e]) if bias is not None else None
                _, _, p, l = _attn_scores(q_ref, k_ref, e, r0, rn, kend, wide, scale, bs)
                outs.append(_dot(p.astype(BF16), v_ref[0:kend, :]) / l)
            o = jnp.where(_lane(outs[0].shape) < 64, outs[0], outs[1])
            o_ref[r0:r0 + rn, :] = o.astype(BF16)

    args = (q, k, v) + ((bias[0], bias[1], bias[1]) if bias is not None else ())
    return _call(
        body, name=name, grid=(B, 4),
        in_specs=specs + bias_specs,
        out_specs=pl.BlockSpec((LP, 128), lambda b, hp: (b, hp)),
        out_shape=jax.ShapeDtypeStruct((T, 512), BF16),
        compiler_params=_cp(),
    )(*args)


def _attn_bwd(q, k, v, do, bias, B, LP, wide, scale, name):
    T = q.shape[0]
    tq = 256
    blocks = _attn_blocks(LP, tq)
    qw, specs, bias_specs = _attn_specs(B, LP, wide, bias is not None)
    has_bias = bias is not None

    def body(*refs):
        if has_bias:
            (q_ref, k_ref, v_ref, do_ref, ct_ref, cr0_ref, cr1_ref,
             dq_ref, dk_ref, dv_ref, dc0_ref, dc1_ref) = refs
            crs = (cr0_ref, cr1_ref)
            dcs = (dc0_ref, dc1_ref)
            dc0_ref[...] = jnp.zeros_like(dc0_ref)
            dc1_ref[...] = jnp.zeros_like(dc1_ref)
        else:
            q_ref, k_ref, v_ref, do_ref, dq_ref, dk_ref, dv_ref = refs
        dk_ref[...] = jnp.zeros_like(dk_ref)
        dv_ref[...] = jnp.zeros_like(dv_ref)
        for (r0, rn, kend) in blocks:
            dqs = []
            for e in (0, 1):
                bs = (ct_ref, crs[e]) if has_bias else None
                qe, ke, p, l = _attn_scores(q_ref, k_ref, e, r0, rn, kend, wide, scale, bs)
                pn = p / l
                dob = do_ref[r0:r0 + rn, :]
                mine = (_lane(dob.shape) < 64) if e == 0 else (_lane(dob.shape) >= 64)
                doe = jnp.where(mine, dob, jnp.zeros_like(dob))
                dp = _dot(doe, v_ref[0:kend, :], NT)
                delta = jnp.sum(pn * dp, axis=-1, keepdims=True)
                ds = pn * (dp - delta)
                dsb = ds.astype(BF16)
                dqe = _dot(dsb, ke) * scale
                dke = _dot(dsb, qe, TN) * scale
                if wide:
                    dq_ref[r0:r0 + rn, 128 * e:128 * (e + 1)] = dqe
                    dk_ref[0:kend, 128 * e:128 * (e + 1)] += dke
                else:
                    dqs.append(dqe)
                    dk_ref[0:kend, :] += dke
                dv_ref[0:kend, :] += _dot(pn.astype(BF16), doe, TN)
                if has_bias:
                    dcs[e][0, :, 0:kend] -= _colsum(ds)
            if not wide:
                dq_ref[r0:r0 + rn, :] = jnp.where(_lane(dqs[0].shape) < 64, dqs[0], dqs[1])

    args = (q, k, v, do) + ((bias[0], bias[1], bias[1]) if has_bias else ())
    out_specs = [pl.BlockSpec((LP, qw), lambda b, hp: (b, hp)),
                 pl.BlockSpec((LP, qw), lambda b, hp: (b, hp)),
                 pl.BlockSpec((LP, 128), lambda b, hp: (b, hp))]
    out_shape = [jax.ShapeDtypeStruct(q.shape, F32), jax.ShapeDtypeStruct(q.shape, F32),
                 jax.ShapeDtypeStruct((T, 512), F32)]
    if has_bias:
        out_specs += [pl.BlockSpec((1, 1, LP), lambda b, hp: (b * 4 + hp, 0, 0))] * 2
        out_shape += [jax.ShapeDtypeStruct((B * 4, 1, LP), F32)] * 2
    return _call(
        body, name=name, grid=(B, 4),
        in_specs=specs + [pl.BlockSpec((LP, 128), lambda b, hp: (b, hp))] + bias_specs,
        out_specs=out_specs, out_shape=out_shape,
        compiler_params=_cp(),
    )(*args)


def _post_fwd(h, of, om, proj, bg, wbf, wbm, wout, name):
    T = h.shape[0]
    tm = _tile(T, (512, 384, 256, 128))

    def body(h_ref, of_ref, om_ref, gl_ref, bg_ref, wbf_ref, wbm_ref, wo_ref, o_ref, mix_ref):
        gate = jax.nn.sigmoid(gl_ref[...] + bg_ref[...])
        mix = gate[:, :D] * _dot(of_ref[...], wbf_ref[...]) + gate[:, D:] * _dot(om_ref[...], wbm_ref[...])
        mb = mix.astype(BF16)
        mix_ref[...] = mb
        o_ref[...] = h_ref[...] + _dot(mb, wo_ref[...])

    row = lambda w: pl.BlockSpec((tm, w), lambda i: (i, 0))
    full = lambda a: pl.BlockSpec(a.shape, lambda i: (0, 0))
    return _call(
        body, name=name, grid=(T // tm,),
        in_specs=[row(D), row(512), row(512), pl.BlockSpec((tm, 2 * D), lambda i: (i, 1)),
                  full(bg), full(wbf), full(wbm), full(wout)],
        out_specs=[row(D), row(D)],
        out_shape=[jax.ShapeDtypeStruct((T, D), F32), jax.ShapeDtypeStruct((T, D), BF16)],
        compiler_params=_cp(),
    )(h, of, om, proj, bg, wbf, wbm, wout)


def _post_bwd(dh, of, om, proj, bg, wbf, wbm, wout, name):
    T = dh.shape[0]
    tm = _tile(T, (512, 384, 256, 128))

    def body(d_ref, of_ref, om_ref, gl_ref, bg_ref, wbf_ref, wbm_ref, wo_ref,
             dgl_ref, dbf_ref, dbm_ref, dof_ref, dom_ref, dbg_ref):
        @pl.when(pl.program_id(0) == 0)
        def _():
            dbg_ref[...] = jnp.zeros_like(dbg_ref)

        gate = jax.nn.sigmoid(gl_ref[...] + bg_ref[...])
        dmix = _dot(d_ref[...].astype(BF16), wo_ref[...], NT)
        ofx = _dot(of_ref[...], wbf_ref[...])
        omx = _dot(om_ref[...], wbm_ref[...])
        gf = gate[:, :D]
        gm = gate[:, D:]
        dof = (dmix * gf).astype(BF16)
        dom = (dmix * gm).astype(BF16)
        dglf = dmix * ofx * gf * (1.0 - gf)
        dglm = dmix * omx * gm * (1.0 - gm)
        dgl_ref[:, :D] = dglf.astype(BF16)
        dgl_ref[:, D:] = dglm.astype(BF16)
        dbg_ref[:, :D] += _colsum(dglf)
        dbg_ref[:, D:] += _colsum(dglm)
        dbf_ref[...] = dof
        dbm_ref[...] = dom
        dof_ref[...] = _dot(dof, wbf_ref[...], NT).astype(BF16)
        dom_ref[...] = _dot(dom, wbm_ref[...], NT).astype(BF16)

    row = lambda w: pl.BlockSpec((tm, w), lambda i: (i, 0))
    full = lambda a: pl.BlockSpec(a.shape, lambda i: (0, 0))
    return _call(
        body, name=name, grid=(T // tm,),
        in_specs=[row(D), row(512), row(512), pl.BlockSpec((tm, 2 * D), lambda i: (i, 1)),
                  full(bg), full(wbf), full(wbm), full(wout)],
        out_specs=[row(2 * D), row(D), row(D), row(512), row(512), pl.BlockSpec((1, 2 * D), lambda i: (0, 0))],
        out_shape=[jax.ShapeDtypeStruct((T, 2 * D), BF16), jax.ShapeDtypeStruct((T, D), BF16),
                   jax.ShapeDtypeStruct((T, D), BF16), jax.ShapeDtypeStruct((T, 512), BF16),
                   jax.ShapeDtypeStruct((T, 512), BF16), jax.ShapeDtypeStruct((1, 2 * D), F32)],
        compiler_params=_cp(),
    )(dh, of, om, proj, bg, wbf, wbm, wout)


def _loss_head(h3, target, B, LP, name):
    S = LP - MPAD
    nb = LP // 128

    def body(h_ref, t_ref, dy_ref, l_ref):
        b = pl.program_id(0)
        p = pl.program_id(1)

        @pl.when(jnp.logical_and(b == 0, p == 0))
        def _():
            l_ref[...] = jnp.zeros_like(l_ref)

        @pl.when(p == 0)
        def _():
            dy_ref[...] = jnp.zeros_like(dy_ref)

        @pl.when(p > 0)
        def _():
            e = h_ref[...] - t_ref[0]
            dy_ref[...] = e * (1.0 / D)
            l_ref[...] += jnp.sum(e * e, axis=0, keepdims=True) * (0.5 / D)

    return _call(
        body, name=name, grid=(B, nb),
        in_specs=[pl.BlockSpec((128, D), lambda b, p: (b * nb + p, 0)),
                  pl.BlockSpec((1, 128, D), lambda b, p: (b, jnp.maximum(p - 1, 0), 0))],
        out_specs=[pl.BlockSpec((128, D), lambda b, p: (b * nb + p, 0)),
                   pl.BlockSpec((1, D), lambda b, p: (0, 0))],
        out_shape=[jax.ShapeDtypeStruct(h3.shape, F32), jax.ShapeDtypeStruct((1, D), F32)],
        compiler_params=_cp(),
    )(h3, target)


def _rope_tables(B, LP):
    pos = jnp.concatenate([jnp.arange(MPAD, dtype=F32), NMETA + jnp.arange(LP - MPAD, dtype=F32)])
    inv_freq = ROPE_THETA ** (-jnp.arange(0, ROPE, 2, dtype=F32) / ROPE)
    ang = pos[:, None] * inv_freq[None, :]
    cos, sin = jnp.cos(ang), jnp.sin(ang)
    z32 = jnp.zeros((LP, 32), F32)
    rc = jnp.concatenate([jnp.ones((LP, 64), F32), cos, cos, z32], axis=1)
    rs = jnp.concatenate([jnp.zeros((LP, 64), F32), -sin, sin, z32], axis=1)
    return jnp.tile(rc, (B, 1)), jnp.tile(rs, (B, 1))


def _pad_lanes(v, start, width=128):
    n = v.shape[1]
    return jnp.concatenate([jnp.zeros((1, start), F32), v, jnp.zeros((1, width - start - n), F32)], axis=1)


def _local_step(x, target, meta, w):
    B, S, _ = x.shape
    LP = MPAD + S
    T = B * LP
    h0 = jnp.concatenate([jnp.broadcast_to(meta[None], (B, NMETA, D)),
                          jnp.zeros((B, MPAD - NMETA, D), F32), x], axis=1).reshape(T, D)
    rc, rs = _rope_tables(B, LP)
    gfq = jnp.tile(w["fox_q_norm"], (1, 2))
    gfk = jnp.tile(w["fox_k_norm"], (1, 2))
    gmq = _pad_lanes(w["mla_q_norm"], 0)
    gmk = _pad_lanes(w["mla_k_norm"], 0)
    bfv = _pad_lanes(w["b_forget"], L_FL)
    small = (gfq, gfk, w["mla_cq_norm"], w["mla_ckv_norm"], gmq, gmk, bfv, w["wuq"], w["wuk"], w["wuv"])

    h1 = _ffn_fwd(h0, w["ffn1_norm"], w["ffn1_wgu"], w["ffn1_wd"], "ffn1_fwd")
    proj, u2 = _inproj_fwd(h1, w["mix_norm"], w["w_in"], "inproj_fwd")
    fq, fk, fv, qm, km, vm, lf = _prep_fwd(proj, rc, rs, *small, name="prep_fwd")
    lf_rows = lf[:, L_FL:L_FL + NH].reshape(B, LP, NH).transpose(0, 2, 1).reshape(B * NH, LP)
    crow = _forget_scan(lf_rows, False, "forget_scan")
    ctok = crow.reshape(B, 4, 2, LP).transpose(0, 1, 3, 2).reshape(B * 4, LP, 2)
    bias = (ctok, crow.reshape(B * NH, 1, LP))
    of = _attn_fwd(fq, fk, fv, bias, B, LP, False, 64 ** -0.5, "fox_fwd")
    om = _attn_fwd(qm, km, vm, None, B, LP, True, MLA_QK ** -0.5, "mla_fwd")
    h2, mix = _post_fwd(h1, of, om, proj, w["b_gate"], w["wbf"], w["wbm"], w["w_out"], "post_fwd")
    h3 = _ffn_fwd(h2, w["ffn2_norm"], w["ffn2_wgu"], w["ffn2_wd"], "ffn2_fwd")
    dy, lpart = _loss_head(h3, target, B, LP, "loss_head")

    g = {}
    dh2, u3, a2, dg2, dp2, g["ffn2_norm"] = _ffn_bwd(h2, dy, w["ffn2_norm"], w["ffn2_wgu"], w["ffn2_wd"], "ffn2_bwd")
    g["ffn2_wgu"] = jnp.concatenate([_wgrad(u3, dg2, "ffn2_dwg", bn=FH, shard_major=True),
                                     _wgrad(u3, dp2, "ffn2_dwu", bn=FH, shard_major=True)], axis=0)
    g["ffn2_wd"] = _wgrad(a2, dy, "ffn2_dwd", scale=0.5, bk=FH)

    dgl, dbf, dbm, dof, dom, g["b_gate"] = _post_bwd(dh2, of, om, proj, w["b_gate"], w["wbf"], w["wbm"], w["w_out"], "post_bwd")
    g["w_out"] = _wgrad(mix, dh2, "dw_out")
    g["wbf"] = _wgrad(of, dbf, "dw_bf")
    g["wbm"] = _wgrad(om, dbm, "dw_bm")
    dfq, dfk, dfv, dc0, dc1 = _attn_bwd(fq, fk, fv, dof, bias, B, LP, False, 64 ** -0.5, "fox_bwd")
    dqm, dkm, dvm = _attn_bwd(qm, km, vm, dom, None, B, LP, True, MLA_QK ** -0.5, "mla_bwd")
    dc = jnp.concatenate([dc0, dc1], axis=1).reshape(B * NH, LP)
    dlf_rows = _forget_scan(dc, True, "forget_scan_bwd")
    dlf = dlf_rows.reshape(B, NH, LP).transpose(0, 2, 1).reshape(T, NH)
    dlf = jnp.concatenate([jnp.zeros((T, L_FL), F32), dlf, jnp.zeros((T, 128 - L_FL - NH), F32)], axis=1)
    (dlo, dgfq, dgfk, g["mla_cq_norm"], g["mla_ckv_norm"], dgmq, dgmk, dbfv,
     g["wuq"], g["wuk"], g["wuv"]) = _prep_bwd(proj, rc, rs, *small, dfq, dfk, dfv, dqm, dkm, dvm, dlf, name="prep_bwd")
    g["fox_q_norm"] = dgfq[:, :64] + dgfq[:, 64:]
    g["fox_k_norm"] = dgfk[:, :64] + dgfk[:, 64:]
    g["mla_q_norm"] = dgmq[:, :MLA_QK]
    g["mla_k_norm"] = dgmk[:, :MLA_QK]
    g["b_forget"] = dbfv[:, L_FL:L_FL + NH]
    g["w_in"] = jnp.concatenate([_wgrad(u2, dlo, "dw_in_lo"), _wgrad(u2, dgl, "dw_in_hi")], axis=1)
    dh1, g["mix_norm"] = _inproj_bwd(h1, dh2, dlo, dgl, w["mix_norm"], w["w_in"], "inproj_bwd")

    dh0, u1, a1, dg1, dp1, g["ffn1_norm"] = _ffn_bwd(h0, dh1, w["ffn1_norm"], w["ffn1_wgu"], w["ffn1_wd"], "ffn1_bwd")
    g["ffn1_wgu"] = jnp.concatenate([_wgrad(u1, dg1, "ffn1_dwg", bn=FH, shard_major=True),
                                     _wgrad(u1, dp1, "ffn1_dwu", bn=FH, shard_major=True)], axis=0)
    g["ffn1_wd"] = _wgrad(a1, dh1, "ffn1_dwd", scale=0.5, bk=FH)
    dh0 = dh0.reshape(B, LP, D)
    grad_x = dh0[:, MPAD:]
    grad_meta = jnp.sum(dh0[:, :NMETA], axis=0)
    return lpart, grad_x, grad_meta, g


def _cols_from_shards(g4):
    n, r, c = g4.shape
    return g4.transpose(1, 0, 2).reshape(r, n * c)


def _cols_to_shards(full):
    r, c4 = full.shape
    return full.reshape(r, N_CHIPS, c4 // N_CHIPS).transpose(1, 0, 2)


def _win_to_kernel(wfull):
    z = lambda n: jnp.zeros((D, n), wfull.dtype)
    fl, cq, ckv, kr, gate = (wfull[:, 1536:1544], wfull[:, 1544:1800], wfull[:, 1800:1928],
                             wfull[:, 1928:1960], wfull[:, 1960:4008])
    misc = jnp.concatenate([z(L_KR), kr, fl, z(128 - L_FL - NH)], axis=1)
    return jnp.concatenate([wfull[:, :1536], cq, ckv, misc, gate], axis=1)


def _win_from_kernel(gk):
    m = C_MISC
    return jnp.concatenate([gk[:, :1536], gk[:, m + L_FL:m + L_FL + NH], gk[:, C_CQ:C_CQ + QR],
                            gk[:, C_CKV:C_CKV + KVR], gk[:, m + L_KR:m + L_KR + ROPE], gk[:, C_GATE:]], axis=1)


def _kernel_weights(gath, vec):
    w = dict(vec)
    w["ffn1_wgu"] = gath["ffn1_w_gu"]
    w["ffn2_wgu"] = gath["ffn2_w_gu"]
    w["ffn1_wd"] = gath["ffn1_w_down"].reshape(2, FH, D)
    w["ffn2_wd"] = gath["ffn2_w_down"].reshape(2, FH, D)
    w["w_in"] = _win_to_kernel(_cols_from_shards(gath["w_in"]))
    uq = _cols_from_shards(gath["mla_w_uq"]).reshape(QR, NH, MLA_QK)
    w["wuq"] = jnp.pad(uq, ((0, 0), (0, 0), (0, 128 - MLA_QK))).reshape(QR, NH * 128)
    ukv = _cols_from_shards(gath["mla_w_ukv"]).reshape(KVR, NH, 128)
    w["wuk"] = jnp.pad(ukv[:, :, :64], ((0, 0), (0, 0), (0, 64))).reshape(KVR, NH * 128)
    w["wuv"] = ukv[:, :, 64:].reshape(KVR, NH * 64)
    w["wbf"] = _cols_from_shards(gath["w_branch_fox"])
    w["wbm"] = _cols_from_shards(gath["w_branch_mla"])
    w["w_out"] = gath["w_out"].reshape(D, D)
    return w


def _grads_to_shards(g):
    out = {}
    out["ffn1_w_gu"] = g["ffn1_wgu"]
    out["ffn2_w_gu"] = g["ffn2_wgu"]
    out["ffn1_w_down"] = g["ffn1_wd"].reshape(N_CHIPS, DFF // N_CHIPS, D)
    out["ffn2_w_down"] = g["ffn2_wd"].reshape(N_CHIPS, DFF // N_CHIPS, D)
    out["w_in"] = _cols_to_shards(_win_from_kernel(g["w_in"]))
    out["mla_w_uq"] = _cols_to_shards(g["wuq"].reshape(QR, NH, 128)[:, :, :MLA_QK].reshape(QR, NH * MLA_QK))
    ukv = jnp.concatenate([g["wuk"].reshape(KVR, NH, 128)[:, :, :64], g["wuv"].reshape(KVR, NH, 64)], axis=2)
    out["mla_w_ukv"] = _cols_to_shards(ukv.reshape(KVR, NH * 128))
    out["w_branch_fox"] = _cols_to_shards(g["wbf"])
    out["w_branch_mla"] = _cols_to_shards(g["wbm"])
    out["w_out"] = g["w_out"].reshape(N_CHIPS, D // N_CHIPS, D)
    return out


HBM = pl.BlockSpec(memory_space=pltpu.HBM)


def _chip_peers(x, y):
    return [(1 - x, y), (x, 1 - y), (1 - x, 1 - y)]


def _all_gather_chips(shards, name):
    n = len(shards)

    def body(*refs):
        ins, outs = refs[:n], refs[n:2 * n]
        send, recv, loc = refs[2 * n:]
        x, y, c = lax.axis_index("x"), lax.axis_index("y"), lax.axis_index("c")
        j = 2 * x + y
        peers = _chip_peers(x, y)
        locs, sends = [], []
        for k in range(n):
            lc = pltpu.make_async_copy(ins[k], outs[k].at[j], loc.at[k])
            lc.start()
            locs.append(lc)
            for r, (px, py) in enumerate(peers):
                cp = pltpu.make_async_remote_copy(src_ref=ins[k], dst_ref=outs[k].at[j],
                                                  send_sem=send.at[3 * k + r], recv_sem=recv.at[3 * k + r],
                                                  device_id=(px, py, c), device_id_type=MESH)
                cp.start()
                sends.append(cp)
        for k in range(n):
            for r, (px, py) in enumerate(peers):
                pltpu.make_async_remote_copy(src_ref=ins[k], dst_ref=outs[k].at[2 * px + py],
                                             send_sem=send.at[3 * k + r], recv_sem=recv.at[3 * k + r],
                                             device_id=(px, py, c), device_id_type=MESH).wait_recv()
        for cp in sends:
            cp.wait_send()
        for lc in locs:
            lc.wait()

    return _call(
        body, name=name,
        in_specs=[HBM] * n, out_specs=[HBM] * n,
        out_shape=[jax.ShapeDtypeStruct((N_CHIPS,) + s.shape, s.dtype) for s in shards],
        scratch_shapes=[pltpu.SemaphoreType.DMA((3 * n,)), pltpu.SemaphoreType.DMA((3 * n,)),
                        pltpu.SemaphoreType.DMA((n,))],
    )(*shards)


def _sibling_swap_halves(gs, name):
    n = len(gs)

    def body(*refs):
        ins, outs = refs[:n], refs[n:2 * n]
        send, recv = refs[2 * n:]
        x, y, c = lax.axis_index("x"), lax.axis_index("y"), lax.axis_index("c")
        cps = []
        for k in range(n):
            h = gs[k].shape[1] // 2
            src = ins[k].at[:, pl.ds(pl.multiple_of((1 - c) * h, 8), h), :]
            cp = pltpu.make_async_remote_copy(src_ref=src, dst_ref=outs[k], send_sem=send.at[k], recv_sem=recv.at[k],
                                              device_id=(x, y, 1 - c), device_id_type=MESH)
            cp.start()
            cps.append(cp)
        for cp in cps:
            cp.wait()

    return _call(
        body, name=name, in_specs=[HBM] * n, out_specs=[HBM] * n,
        out_shape=[jax.ShapeDtypeStruct((a.shape[0], a.shape[1] // 2, a.shape[2]), a.dtype) for a in gs],
        scratch_shapes=[pltpu.SemaphoreType.DMA((n,)), pltpu.SemaphoreType.DMA((n,))],
    )(*gs)


def _add_half(g4, recv, cidx, name):
    n, r, c = g4.shape
    h = r // 2
    tr = _tile(h, (256, 176, 128, 64, 8))
    g5 = g4.reshape(n, 2, h, c)

    def body(c_ref, a_ref, b_ref, o_ref):
        o_ref[...] = (a_ref[0] + b_ref[...]).astype(o_ref.dtype)

    return _call(
        body, name=name,
        grid_spec=pltpu.PrefetchScalarGridSpec(
            num_scalar_prefetch=1, grid=(n, h // tr),
            in_specs=[pl.BlockSpec((1, 1, tr, c), lambda s, i, cr: (s, cr[0], i, 0)),
                      pl.BlockSpec((1, tr, c), lambda s, i, cr: (s, i, 0))],
            out_specs=pl.BlockSpec((1, tr, c), lambda s, i, cr: (s, i, 0))),
        out_shape=jax.ShapeDtypeStruct((n, h, c), RS_DTYPE),
        compiler_params=_cp(),
    )(cidx, g5, recv)


def _chip_exchange(ps, name):
    n = len(ps)

    def body(*refs):
        ins, outs = refs[:n], refs[n:2 * n]
        send, recv, loc = refs[2 * n:]
        x, y, c = lax.axis_index("x"), lax.axis_index("y"), lax.axis_index("c")
        j = 2 * x + y
        peers = _chip_peers(x, y)
        locs, sends = [], []
        for k in range(n):
            lc = pltpu.make_async_copy(ins[k].at[j], outs[k].at[j], loc.at[k])
            lc.start()
            locs.append(lc)
            for r, (px, py) in enumerate(peers):
                cp = pltpu.make_async_remote_copy(src_ref=ins[k].at[2 * px + py], dst_ref=outs[k].at[j],
                                                  send_sem=send.at[3 * k + r], recv_sem=recv.at[3 * k + r],
                                                  device_id=(px, py, c), device_id_type=MESH)
                cp.start()
                sends.append(cp)
        for k in range(n):
            for r, (px, py) in enumerate(peers):
                pltpu.make_async_remote_copy(src_ref=ins[k].at[j], dst_ref=outs[k].at[2 * px + py],
                                             send_sem=send.at[3 * k + r], recv_sem=recv.at[3 * k + r],
                                             device_id=(px, py, c), device_id_type=MESH).wait_recv()
        for cp in sends:
            cp.wait_send()
        for lc in locs:
            lc.wait()

    return _call(
        body, name=name, in_specs=[HBM] * n, out_specs=[HBM] * n,
        out_shape=[jax.ShapeDtypeStruct(a.shape, a.dtype) for a in ps],
        scratch_shapes=[pltpu.SemaphoreType.DMA((3 * n,)), pltpu.SemaphoreType.DMA((3 * n,)),
                        pltpu.SemaphoreType.DMA((n,))],
    )(*ps)


def _sum_chips(p4, name):
    n, h, c = p4.shape
    tr = _tile(h, (256, 176, 128, 64, 8))

    def body(a_ref, o_ref):
        acc = a_ref[0].astype(F32)
        for s in range(1, n):
            acc = acc + a_ref[s].astype(F32)
        o_ref[...] = acc

    return _call(
        body, name=name, grid=(h // tr,),
        in_specs=[pl.BlockSpec((n, tr, c), lambda i: (0, i, 0))],
        out_specs=pl.BlockSpec((tr, c), lambda i: (i, 0)),
        out_shape=jax.ShapeDtypeStruct((h, c), F32),
        compiler_params=_cp(),
    )(p4)


def _sibling_join_halves(ss, name):
    n = len(ss)

    def body(*refs):
        ins, outs = refs[:n], refs[n:2 * n]
        send, recv, loc = refs[2 * n:]
        x, y, c = lax.axis_index("x"), lax.axis_index("y"), lax.axis_index("c")
        cps, locs = [], []
        for k in range(n):
            h = ss[k].shape[0]
            mine = outs[k].at[pl.ds(pl.multiple_of(c * h, 8), h), :]
            lc = pltpu.make_async_copy(ins[k], mine, loc.at[k])
            lc.start()
            locs.append(lc)
            cp = pltpu.make_async_remote_copy(src_ref=ins[k], dst_ref=mine, send_sem=send.at[k], recv_sem=recv.at[k],
                                              device_id=(x, y, 1 - c), device_id_type=MESH)
            cp.start()
            cps.append(cp)
        for k in range(n):
            h = ss[k].shape[0]
            other = outs[k].at[pl.ds(pl.multiple_of((1 - c) * h, 8), h), :]
            pltpu.make_async_remote_copy(src_ref=ins[k], dst_ref=other, send_sem=send.at[k], recv_sem=recv.at[k],
                                         device_id=(x, y, 1 - c), device_id_type=MESH).wait_recv()
        for cp in cps:
            cp.wait_send()
        for lc in locs:
            lc.wait()

    return _call(
        body, name=name, in_specs=[HBM] * n, out_specs=[HBM] * n,
        out_shape=[jax.ShapeDtypeStruct((2 * a.shape[0], a.shape[1]), a.dtype) for a in ss],
        scratch_shapes=[pltpu.SemaphoreType.DMA((n,)), pltpu.SemaphoreType.DMA((n,)), pltpu.SemaphoreType.DMA((n,))],
    )(*ss)


def _all_reduce_small(v, name):
    r = v.shape[0]

    def body(v_ref, o_ref, gat_ref, send, recv):
        x, y, c = lax.axis_index("x"), lax.axis_index("y"), lax.axis_index("c")
        me = 4 * x + 2 * y + c
        gat_ref[me] = v_ref[...]
        rels = [(dx, dy, dc) for dx in (0, 1) for dy in (0, 1) for dc in (0, 1)][1:]
        flip = lambda a, d: (1 - a) if d else a
        cps = []
        for i, (dx, dy, dc) in enumerate(rels):
            cp = pltpu.make_async_remote_copy(src_ref=v_ref, dst_ref=gat_ref.at[me], send_sem=send.at[i], recv_sem=recv.at[i],
                                              device_id=(flip(x, dx), flip(y, dy), flip(c, dc)), device_id_type=MESH)
            cp.start()
            cps.append(cp)
        for i, (dx, dy, dc) in enumerate(rels):
            src = 4 * flip(x, dx) + 2 * flip(y, dy) + flip(c, dc)
            pltpu.make_async_remote_copy(src_ref=v_ref, dst_ref=gat_ref.at[src], send_sem=send.at[i], recv_sem=recv.at[i],
                                         device_id=(flip(x, dx), flip(y, dy), flip(c, dc)), device_id_type=MESH).wait_recv()
        for cp in cps:
            cp.wait_send()
        acc = gat_ref[0]
        for s in range(1, N_DEV):
            acc = acc + gat_ref[s]
        o_ref[...] = acc

    vm = pl.BlockSpec(memory_space=pltpu.VMEM)
    return _call(
        body, name=name, in_specs=[vm], out_specs=vm,
        out_shape=jax.ShapeDtypeStruct(v.shape, F32),
        scratch_shapes=[pltpu.VMEM((N_DEV, r, 128), F32), pltpu.SemaphoreType.DMA((N_DEV - 1,)),
                        pltpu.SemaphoreType.DMA((N_DEV - 1,))],
    )(v)


def _adamw(wt, g, m, v, name):
    r, c = wt.shape
    tr = _tile(r, (256, 176, 128, 64, 16, 8, 1)) if r * c > 2**18 else r
    c1 = 1.0 / (1.0 - ADAM_B1 ** ADAM_STEP)
    c2 = 1.0 / (1.0 - ADAM_B2 ** ADAM_STEP)

    def body(w_ref, g_ref, m_ref, v_ref, d_ref, nm_ref, nv_ref):
        gg = g_ref[...]
        nm = ADAM_B1 * m_ref[...] + (1.0 - ADAM_B1) * gg
        nv = ADAM_B2 * v_ref[...] + (1.0 - ADAM_B2) * (gg * gg)
        nm_ref[...] = nm
        nv_ref[...] = nv
        d_ref[...] = -ADAM_LR * ((nm * c1) / (jnp.sqrt(nv * c2) + ADAM_EPS) + ADAM_WD * w_ref[...])

    spec = pl.BlockSpec((tr, c), lambda i: (i, 0))
    return _call(
        body, name=name, grid=(r // tr,), in_specs=[spec] * 4, out_specs=[spec] * 3,
        out_shape=[jax.ShapeDtypeStruct((r, c), F32)] * 3,
        compiler_params=_cp(),
    )(wt, g, m, v)


MATS = ["ffn1_w_gu", "ffn1_w_down", "w_in", "mla_w_uq", "mla_w_ukv", "w_branch_fox", "w_branch_mla",
        "w_out", "ffn2_w_gu", "ffn2_w_down"]
VECS = ["ffn1_norm", "mix_norm", "b_forget", "b_gate", "fox_q_norm", "fox_k_norm", "mla_cq_norm",
        "mla_ckv_norm", "mla_q_norm", "mla_k_norm", "ffn2_norm"]
WEIGHTS = ["meta_tokens", "ffn1_norm", "ffn1_w_gu", "ffn1_w_down", "mix_norm", "w_in", "b_forget", "b_gate",
           "fox_q_norm", "fox_k_norm", "mla_cq_norm", "mla_w_uq", "mla_ckv_norm", "mla_w_ukv", "mla_q_norm",
           "mla_k_norm", "w_branch_fox", "w_branch_mla", "w_out", "ffn2_norm", "ffn2_w_gu", "ffn2_w_down"]


def _pack_small(parts):
    flat = jnp.concatenate([p.reshape(-1) for p in parts])
    n = flat.shape[0]
    rows = -(-n // 1024) * 8
    flat = jnp.concatenate([flat, jnp.zeros((rows * 128 - n,), F32)])
    return flat.reshape(rows, 128)


def _unpack_small(packed, like):
    flat = packed.reshape(-1)
    out, o = [], 0
    for p in like:
        out.append(flat[o:o + p.size].reshape(p.shape))
        o += p.size
    return out


def kernel(x, meta_tokens, ffn1_norm, ffn1_w_gu, ffn1_w_down, mix_norm, w_in, b_forget, b_gate, fox_q_norm, fox_k_norm, mla_cq_norm, mla_w_uq, mla_ckv_norm, mla_w_ukv, mla_q_norm, mla_k_norm, w_branch_fox, w_branch_mla, w_out, ffn2_norm, ffn2_w_gu, ffn2_w_down, loss_target, m_meta_tokens, m_ffn1_norm, m_ffn1_w_gu, m_ffn1_w_down, m_mix_norm, m_w_in, m_b_forget, m_b_gate, m_fox_q_norm, m_fox_k_norm, m_mla_cq_norm, m_mla_w_uq, m_mla_ckv_norm, m_mla_w_ukv, m_mla_q_norm, m_mla_k_norm, m_w_branch_fox, m_w_branch_mla, m_w_out, m_ffn2_norm, m_ffn2_w_gu, m_ffn2_w_down, v_meta_tokens, v_ffn1_norm, v_ffn1_w_gu, v_ffn1_w_down, v_mix_norm, v_w_in, v_b_forget, v_b_gate, v_fox_q_norm, v_fox_k_norm, v_mla_cq_norm, v_mla_w_uq, v_mla_ckv_norm, v_mla_w_ukv, v_mla_q_norm, v_mla_k_norm, v_w_branch_fox, v_w_branch_mla, v_w_out, v_ffn2_norm, v_ffn2_w_gu, v_ffn2_w_down):
    a = dict(locals())
    wts = {n: a[n] for n in WEIGHTS}
    ms = {n: a["m_" + n] for n in WEIGHTS}
    vs = {n: a["v_" + n] for n in WEIGHTS}
    cx, cy, cc = lax.axis_index("x"), lax.axis_index("y"), lax.axis_index("c")
    chip = 2 * cx + cy

    shards = [wts[n][0].astype(BF16) for n in MATS] + [meta_tokens]
    gathered = _all_gather_chips(shards, "gather_weights")
    gath = dict(zip(MATS, gathered[:-1]))
    meta_full = _cols_from_shards(gathered[-1])
    w = _kernel_weights(gath, {n: wts[n] for n in VECS})

    lpart, grad_x, gmeta, g = _local_step(x, loss_target, meta_full, w)
    loss = lax.psum(jnp.sum(lpart), ("x", "y", "c"))

    g4 = _grads_to_shards(g)
    gs = [g4[n] for n in MATS]
    recv_a = _sibling_swap_halves(gs, "rs_sibling_swap")
    cidx = cc.reshape(1).astype(jnp.int32)
    ps = [_add_half(gs[k], recv_a[k], cidx, "rs_add_" + MATS[k]) for k in range(len(MATS))]
    recv_b = _chip_exchange(ps, "rs_chip_exchange")
    ss = [_sum_chips(recv_b[k], "rs_sum_" + MATS[k]) for k in range(len(MATS))]
    gred = dict(zip(MATS, _sibling_join_halves(ss, "rs_sibling_join")))

    small_parts = [g[n] for n in VECS] + [gmeta]
    red = _unpack_small(_all_reduce_small(_pack_small(small_parts), "allreduce_small"), small_parts)
    gvec = dict(zip(VECS, red[:-1]))
    gmeta_shard = lax.dynamic_slice_in_dim(red[-1], chip * (D // N_CHIPS), D // N_CHIPS, axis=1)

    grads, delta, new_m, new_v = {}, {}, {}, {}
    for n in MATS:
        shp = wts[n].shape
        gr = gred[n]
        d_, m_, v_ = _adamw(wts[n][0], gr, ms[n][0], vs[n][0], "adamw_" + n)
        grads[n], delta[n], new_m[n], new_v[n] = (t.reshape(shp) for t in (gr, d_, m_, v_))
    sm_names = VECS + ["meta_tokens"]
    sm_g = [gvec[n] for n in VECS] + [gmeta_shard]
    pk = lambda d: _pack_small([d[n] for n in sm_names])
    d_, m_, v_ = _adamw(pk(wts), _pack_small(sm_g), pk(ms), pk(vs), "adamw_small")
    like = [wts[n] for n in sm_names]
    for n, gg, dd, mm, vv in zip(sm_names, sm_g, _unpack_small(d_, like), _unpack_small(m_, like), _unpack_small(v_, like)):
        grads[n], delta[n], new_m[n], new_v[n] = gg.reshape(wts[n].shape), dd, mm, vv

    return (loss, grad_x, *[grads[n] for n in WEIGHTS], *[delta[n] for n in WEIGHTS],
            *[new_m[n] for n in WEIGHTS], *[new_v[n] for n in WEIGHTS])
```

```python
import functools

import jax
import jax.numpy as jnp
from jax import lax
from jax.experimental import pallas as pl
from jax.experimental.pallas import tpu as pltpu

F32 = jnp.float32
BF16 = jnp.bfloat16
MESH = pl.DeviceIdType.MESH

D = 1024
DFF = 2816
FH = DFF // 2
NMETA = 16
MPAD = 128
EPS = 1e-6
NH = 8
FOXW = 512
QR = 256
KVR = 128
ROPE = 32
MLA_QK = 96
PROJW = 4096
ROPE_THETA = 10000.0
N_CHIPS = 4
N_DEV = 8

ADAM_LR = 0.001
ADAM_B1 = 0.9
ADAM_B2 = 0.999
ADAM_EPS = 1e-08
ADAM_WD = 0.01
ADAM_STEP = 10

VMEM_LIMIT = 56 * 2**20
GRAD_DTYPE = BF16

NT = (((1,), (1,)), ((), ()))
TN = (((0,), (0,)), ((), ()))


def _call(body, **kw):
    return pl.pallas_call(body, **kw)


def _cp(**kw):
    return pltpu.CompilerParams(vmem_limit_bytes=VMEM_LIMIT, **kw)


HBM = pl.BlockSpec(memory_space=pltpu.HBM)


class _Task:
    def __init__(self, ins, out_shapes, sems, descs):
        self.ins, self.out_shapes, self.sems, self.descs = list(ins), list(out_shapes), list(sems), descs

    def start(self, ins, outs, sems):
        locs, pairs = self.descs(ins, outs, sems)
        for lc in locs:
            lc().start()
        for snd, _ in pairs:
            snd().start()

    def wait(self, ins, outs, sems):
        locs, pairs = self.descs(ins, outs, sems)
        for _, rcv in pairs:
            rcv().wait_recv()
        for snd, _ in pairs:
            snd().wait_send()
        for lc in locs:
            lc().wait()


def _call_tasks(body, tasks, *, name, grid, in_specs, out_specs, out_shape, args, scratch_shapes=()):
    in_specs, out_specs, out_shape, scratch_shapes = map(list, (in_specs, out_specs, out_shape, scratch_shapes))
    n_in, n_out, n_sc = len(in_specs), len(out_specs), len(scratch_shapes)
    t_in = [len(t.ins) for t in tasks]
    t_out = [len(t.out_shapes) for t in tasks]
    t_sem = [len(t.sems) for t in tasks]

    def wrapped(*refs):
        pos = [0]

        def take(n):
            pos[0] += n
            return refs[pos[0] - n:pos[0]]

        ins, tins = take(n_in), [take(n) for n in t_in]
        outs, touts = take(n_out), [take(n) for n in t_out]
        sc, tsems = take(n_sc), [take(n) for n in t_sem]
        if tasks:
            first = functools.reduce(jnp.logical_and, [pl.program_id(a) == 0 for a in range(len(grid))])
            last = functools.reduce(jnp.logical_and, [pl.program_id(a) == grid[a] - 1 for a in range(len(grid))])

            @pl.when(first)
            def _():
                for t, a, b, s in zip(tasks, tins, touts, tsems):
                    t.start(a, b, s)

        body(*ins, *outs, *sc)
        if tasks:
            @pl.when(last)
            def _():
                for t, a, b, s in zip(tasks, tins, touts, tsems):
                    t.wait(a, b, s)

    res = _call(
        wrapped, name=name, grid=grid,
        in_specs=in_specs + [HBM] * sum(t_in), out_specs=out_specs + [HBM] * sum(t_out),
        out_shape=out_shape + [s for t in tasks for s in t.out_shapes],
        scratch_shapes=scratch_shapes + [s for t in tasks for s in t.sems],
        compiler_params=_cp(),
    )(*args, *[a for t in tasks for a in t.ins])
    res = list(res)
    touts, pos = [], n_out
    for n in t_out:
        touts.append(res[pos:pos + n])
        pos += n
    return res[:n_out], touts


def _run_tasks(tasks, name):
    t_in = [len(t.ins) for t in tasks]
    t_out = [len(t.out_shapes) for t in tasks]
    t_sem = [len(t.sems) for t in tasks]

    def body(*refs):
        pos = [0]

        def take(n):
            pos[0] += n
            return refs[pos[0] - n:pos[0]]

        tins, touts, tsems = [take(n) for n in t_in], [take(n) for n in t_out], [take(n) for n in t_sem]
        for t, a, b, s in zip(tasks, tins, touts, tsems):
            t.start(a, b, s)
        for t, a, b, s in zip(tasks, tins, touts, tsems):
            t.wait(a, b, s)

    res = list(_call(
        body, name=name, in_specs=[HBM] * sum(t_in), out_specs=[HBM] * sum(t_out),
        out_shape=[s for t in tasks for s in t.out_shapes],
        scratch_shapes=[s for t in tasks for s in t.sems],
    )(*[a for t in tasks for a in t.ins]))
    touts, pos = [], 0
    for n in t_out:
        touts.append(res[pos:pos + n])
        pos += n
    return touts


def _tile(n, cands):
    for c in cands:
        if n % c == 0:
            return c
    raise ValueError(f"no tile for {n} among {cands}")


def _dot(a, b, dims=None):
    if dims is None:
        return jnp.dot(a, b, preferred_element_type=F32)
    return lax.dot_general(a, b, dims, preferred_element_type=F32)


def _rms(x, gain, n):
    r = lax.rsqrt(jnp.sum(x * x, axis=-1, keepdims=True) * (1.0 / n) + EPS)
    xh = x * r
    return xh * gain, xh, r


def _rms_bwd(dy, xh, r, gain, n):
    dxh = dy * gain
    return r * (dxh - xh * (jnp.sum(dxh * xh, axis=-1, keepdims=True) * (1.0 / n)))


def _lane(shape):
    return lax.broadcasted_iota(jnp.int32, shape, len(shape) - 1)


def _half_sum(x):
    lo = _lane(x.shape) < 64
    s_lo = jnp.sum(jnp.where(lo, x, 0.0), axis=-1, keepdims=True)
    s_hi = jnp.sum(jnp.where(lo, 0.0, x), axis=-1, keepdims=True)
    return jnp.where(lo, s_lo, s_hi)


def _rope_swap(x):
    ln = _lane(x.shape)
    sw = jnp.where(ln < 80, pltpu.roll(x, 112, 1), pltpu.roll(x, 16, 1))
    return jnp.where(jnp.logical_and(ln >= 64, ln < 96), sw, 0.0)


def _colsum(x):
    return jnp.sum(x, axis=0, keepdims=True)


def _ffn_fwd(h, norm, wgu, wd3, name, tasks=()):
    T = h.shape[0]
    tm = _tile(T, (512, 384, 256, 128))

    def body(h_ref, n_ref, wg_ref, wu_ref, wd_ref, o_ref, u_sc):
        @pl.when(pl.program_id(1) == 0)
        def _():
            x = h_ref[...]
            u, _, _ = _rms(x, n_ref[...], D)
            u_sc[...] = u.astype(BF16)
            o_ref[...] = x

        u = u_sc[...]
        g = _dot(u, wg_ref[0])
        p = _dot(u, wu_ref[0])
        a = (g * jax.nn.sigmoid(g)) * p
        o_ref[...] += 0.5 * _dot(a.astype(BF16), wd_ref[0])

    (out,), touts = _call_tasks(
        body, tasks, name=name, grid=(T // tm, 2),
        in_specs=[pl.BlockSpec((tm, D), lambda i, j: (i, 0)),
                  pl.BlockSpec((1, D), lambda i, j: (0, 0)),
                  pl.BlockSpec((1, D, FH), lambda i, j: (j, 0, 0)),
                  pl.BlockSpec((1, D, FH), lambda i, j: (j + 2, 0, 0)),
                  pl.BlockSpec((1, FH, D), lambda i, j: (j, 0, 0))],
        out_specs=[pl.BlockSpec((tm, D), lambda i, j: (i, 0))],
        out_shape=[jax.ShapeDtypeStruct((T, D), F32)],
        scratch_shapes=[pltpu.VMEM((tm, D), BF16)],
        args=(h, norm, wgu, wgu, wd3))
    return out, touts


def _ffn_bwd(h, dout, norm, wgu, wd3, name, tasks=()):
    T = h.shape[0]
    tm = _tile(T, (256, 128))
    nt = T // tm

    def body(h_ref, d_ref, n_ref, wg_ref, wu_ref, wd_ref,
             dh_ref, u_ref, a_ref, dg_ref, dp_ref, dn_ref, xh_sc, r_sc, u_sc, du_sc):
        i = pl.program_id(0)
        j = pl.program_id(1)

        @pl.when(jnp.logical_and(i == 0, j == 0))
        def _():
            dn_ref[...] = jnp.zeros_like(dn_ref)

        @pl.when(j == 0)
        def _():
            x = h_ref[...]
            u, xh, r = _rms(x, n_ref[...], D)
            xh_sc[...] = xh
            r_sc[...] = r
            ub = u.astype(BF16)
            u_sc[...] = ub
            u_ref[...] = ub
            du_sc[...] = jnp.zeros_like(du_sc)

        u = u_sc[...]
        g = _dot(u, wg_ref[0])
        p = _dot(u, wu_ref[0])
        s = jax.nn.sigmoid(g)
        sl = g * s
        dz = (0.5 * d_ref[...]).astype(BF16)
        da = _dot(dz, wd_ref[0], NT)
        dp = da * sl
        dg = (da * p) * (s * (1.0 + g * (1.0 - s)))
        a_ref[...] = (sl * p).astype(BF16)
        dgb = dg.astype(BF16)
        dpb = dp.astype(BF16)
        dg_ref[...] = dgb
        dp_ref[...] = dpb
        du_sc[...] += _dot(dgb, wg_ref[0], NT) + _dot(dpb, wu_ref[0], NT)

        @pl.when(j == 1)
        def _():
            du = du_sc[...]
            xh = xh_sc[...]
            dn_ref[...] += _colsum(du * xh)
            dh_ref[...] = d_ref[...] + _rms_bwd(du, xh, r_sc[...], n_ref[...], D)

    return _call_tasks(
        body, tasks, name=name, grid=(nt, 2),
        in_specs=[pl.BlockSpec((tm, D), lambda i, j: (i, 0)),
                  pl.BlockSpec((tm, D), lambda i, j: (i, 0)),
                  pl.BlockSpec((1, D), lambda i, j: (0, 0)),
                  pl.BlockSpec((1, D, FH), lambda i, j: (j, 0, 0)),
                  pl.BlockSpec((1, D, FH), lambda i, j: (j + 2, 0, 0)),
                  pl.BlockSpec((1, FH, D), lambda i, j: (j, 0, 0))],
        out_specs=[pl.BlockSpec((tm, D), lambda i, j: (i, 0)),
                   pl.BlockSpec((tm, D), lambda i, j: (i, 0)),
                   pl.BlockSpec((tm, FH), lambda i, j: (i, j)),
                   pl.BlockSpec((tm, FH), lambda i, j: (i, j)),
                   pl.BlockSpec((tm, FH), lambda i, j: (i, j)),
                   pl.BlockSpec((1, D), lambda i, j: (0, 0))],
        out_shape=[jax.ShapeDtypeStruct((T, D), F32),
                   jax.ShapeDtypeStruct((T, D), BF16),
                   jax.ShapeDtypeStruct((T, DFF), BF16),
                   jax.ShapeDtypeStruct((T, DFF), BF16),
                   jax.ShapeDtypeStruct((T, DFF), BF16),
                   jax.ShapeDtypeStruct((1, D), F32)],
        scratch_shapes=[pltpu.VMEM((tm, D), F32), pltpu.VMEM((tm, 1), F32),
                        pltpu.VMEM((tm, D), BF16), pltpu.VMEM((tm, D), F32)],
        args=(h, dout, norm, wgu, wgu, wd3))


def _wgrad(x, y, name, scale=1.0, bk=None, bn=None, shard_major=False, tasks=()):
    T, K = x.shape
    N = y.shape[1]
    bk = bk or K
    bn = bn or N
    bt = _tile(T, (512, 384, 256, 128))
    nt = T // bt

    def body(x_ref, y_ref, o_ref, acc_ref):
        t = pl.program_id(2)

        @pl.when(t == 0)
        def _():
            acc_ref[...] = jnp.zeros_like(acc_ref)

        acc_ref[...] += _dot(x_ref[...].astype(BF16), y_ref[...].astype(BF16), TN)

        @pl.when(t == nt - 1)
        def _():
            o_ref[...] = (acc_ref[...] * scale).astype(o_ref.dtype).reshape(o_ref.shape)

    if shard_major:
        assert bk == K
        out_spec = pl.BlockSpec((1, K, bn), lambda i, j, t: (j, 0, 0))
        out_shape = jax.ShapeDtypeStruct((N // bn, K, bn), GRAD_DTYPE)
    else:
        out_spec = pl.BlockSpec((bk, bn), lambda i, j, t: (i, j))
        out_shape = jax.ShapeDtypeStruct((K, N), GRAD_DTYPE)
    (out,), touts = _call_tasks(
        body, tasks, name=name, grid=(K // bk, N // bn, nt),
        in_specs=[pl.BlockSpec((bt, bk), lambda i, j, t: (t, i)),
                  pl.BlockSpec((bt, bn), lambda i, j, t: (t, j))],
        out_specs=[out_spec], out_shape=[out_shape],
        scratch_shapes=[pltpu.VMEM((bk, bn), F32)],
        args=(x, y))
    return out, touts


def _inproj_fwd(h, norm, w, name):
    T = h.shape[0]
    tm = _tile(T, (512, 384, 256, 128))
    tn = 1024

    def body(h_ref, n_ref, w_ref, o_ref, u_ref):
        @pl.when(pl.program_id(1) == 0)
        def _():
            u, _, _ = _rms(h_ref[...], n_ref[...], D)
            u_ref[...] = u.astype(BF16)

        o_ref[...] = _dot(u_ref[...], w_ref[...])

    return _call(
        body, name=name, grid=(T // tm, PROJW // tn),
        in_specs=[pl.BlockSpec((tm, D), lambda i, j: (i, 0)),
                  pl.BlockSpec((1, D), lambda i, j: (0, 0)),
                  pl.BlockSpec((D, tn), lambda i, j: (0, j))],
        out_specs=[pl.BlockSpec((tm, tn), lambda i, j: (i, j)),
                   pl.BlockSpec((tm, D), lambda i, j: (i, 0))],
        out_shape=[jax.ShapeDtypeStruct((T, PROJW), F32), jax.ShapeDtypeStruct((T, D), BF16)],
        compiler_params=_cp(),
    )(h, norm, w)


def _inproj_bwd(h, dres, dlo, dhi, norm, w, name):
    T = h.shape[0]
    tm = _tile(T, (512, 384, 256, 128))
    hw = PROJW // 2

    def body(h_ref, d_ref, lo_ref, hi_ref, n_ref, wlo_ref, whi_ref, dh_ref, dn_ref):
        @pl.when(pl.program_id(0) == 0)
        def _():
            dn_ref[...] = jnp.zeros_like(dn_ref)

        _, xh, r = _rms(h_ref[...], n_ref[...], D)
        du = _dot(lo_ref[...], wlo_ref[...], NT) + _dot(hi_ref[...], whi_ref[...], NT)
        dn_ref[...] += _colsum(du * xh)
        dh_ref[...] = d_ref[...] + _rms_bwd(du, xh, r, n_ref[...], D)

    return _call(
        body, name=name, grid=(T // tm,),
        in_specs=[pl.BlockSpec((tm, D), lambda i: (i, 0)),
                  pl.BlockSpec((tm, D), lambda i: (i, 0)),
                  pl.BlockSpec((tm, hw), lambda i: (i, 0)),
                  pl.BlockSpec((tm, hw), lambda i: (i, 0)),
                  pl.BlockSpec((1, D), lambda i: (0, 0)),
                  pl.BlockSpec((D, hw), lambda i: (0, 0)),
                  pl.BlockSpec((D, hw), lambda i: (0, 1))],
        out_specs=[pl.BlockSpec((tm, D), lambda i: (i, 0)),
                   pl.BlockSpec((1, D), lambda i: (0, 0))],
        out_shape=[jax.ShapeDtypeStruct((T, D), F32), jax.ShapeDtypeStruct((1, D), F32)],
        compiler_params=_cp(),
    )(h, dres, dlo, dhi, norm, w, w)


C_FQ, C_FK, C_FV, C_CQ, C_CKV, C_MISC, C_GATE = 0, 512, 1024, 1536, 1792, 1920, 2048
L_KR, L_FL = 64, 96


def _prep_fwd(proj, rc, rs, gfq, gfk, gcq, gckv, gmq, gmk, bfv, wuq, wuk, wuv, name):
    T = proj.shape[0]
    tm = _tile(T, (256, 128))

    def body(p_ref, rc_ref, rs_ref, gfq_ref, gfk_ref, gcq_ref, gckv_ref, gmq_ref, gmk_ref, bf_ref,
             wuq_ref, wuk_ref, wuv_ref, fq_ref, fk_ref, fv_ref, qm_ref, km_ref, vm_ref, lf_ref):
        for blk in range(4):
            for (c0, g_ref, o_ref) in ((C_FQ, gfq_ref, fq_ref), (C_FK, gfk_ref, fk_ref)):
                x = p_ref[:, c0 + 128 * blk:c0 + 128 * (blk + 1)]
                r = lax.rsqrt(_half_sum(x * x) * (1.0 / 64) + EPS)
                o_ref[:, 128 * blk:128 * (blk + 1)] = (x * r * g_ref[...]).astype(BF16)
        fv_ref[...] = p_ref[:, C_FV:C_FV + 512].astype(BF16)

        rcv = rc_ref[...]
        rsv = rs_ref[...]
        cqn, _, _ = _rms(p_ref[:, C_CQ:C_CQ + QR], gcq_ref[...], QR)
        qpre = _dot(cqn.astype(BF16), wuq_ref[...])
        ckvn, _, _ = _rms(p_ref[:, C_CKV:C_CKV + KVR], gckv_ref[...], KVR)
        ckvb = ckvn.astype(BF16)
        kpre = _dot(ckvb, wuk_ref[...])
        vm_ref[...] = _dot(ckvb, wuv_ref[...]).astype(BF16)
        misc = p_ref[:, C_MISC:C_MISC + 128]
        ln = _lane(misc.shape)
        kr = jnp.where(jnp.logical_and(ln >= L_KR, ln < L_KR + ROPE), misc, 0.0)
        for hh in range(NH):
            sl = slice(128 * hh, 128 * (hh + 1))
            qn, _, _ = _rms(qpre[:, sl], gmq_ref[...], MLA_QK)
            qm_ref[:, sl] = (qn * rcv + _rope_swap(qn) * rsv).astype(BF16)
            kn, _, _ = _rms(kpre[:, sl] + kr, gmk_ref[...], MLA_QK)
            km_ref[:, sl] = (kn * rcv + _rope_swap(kn) * rsv).astype(BF16)
        z = misc + bf_ref[...]
        lf_ref[...] = jnp.minimum(z, 0.0) - jnp.log(1.0 + jnp.exp(-jnp.abs(z)))

    row = lambda w: pl.BlockSpec((tm, w), lambda i: (i, 0))
    full = lambda a: pl.BlockSpec(a.shape, lambda i: (0, 0))
    return _call(
        body, name=name, grid=(T // tm,),
        in_specs=[row(PROJW // 2), row(128), row(128)] + [full(a) for a in (gfq, gfk, gcq, gckv, gmq, gmk, bfv, wuq, wuk, wuv)],
        out_specs=[row(512), row(512), row(512), row(1024), row(1024), row(512), row(128)],
        out_shape=[jax.ShapeDtypeStruct((T, 512), BF16), jax.ShapeDtypeStruct((T, 512), BF16),
                   jax.ShapeDtypeStruct((T, 512), BF16), jax.ShapeDtypeStruct((T, 1024), BF16),
                   jax.ShapeDtypeStruct((T, 1024), BF16), jax.ShapeDtypeStruct((T, 512), BF16),
                   jax.ShapeDtypeStruct((T, 128), F32)],
        compiler_params=_cp(),
    )(proj, rc, rs, gfq, gfk, gcq, gckv, gmq, gmk, bfv, wuq, wuk, wuv)


def _prep_bwd(proj, rc, rs, gfq, gfk, gcq, gckv, gmq, gmk, bfv, wuq, wuk, wuv,
              dfq, dfk, dfv, dqm, dkm, dvm, dlf, name):
    T = proj.shape[0]
    tm = _tile(T, (256, 128))

    def body(p_ref, rc_ref, rs_ref, gfq_ref, gfk_ref, gcq_ref, gckv_ref, gmq_ref, gmk_ref, bf_ref,
             wuq_ref, wuk_ref, wuv_ref, dfq_ref, dfk_ref, dfv_ref, dqm_ref, dkm_ref, dvm_ref, dlf_ref,
             dp_ref, dgfq_ref, dgfk_ref, dgcq_ref, dgckv_ref, dgmq_ref, dgmk_ref, dbf_ref,
             dwuq_ref, dwuk_ref, dwuv_ref, dqpre_sc, dkpre_sc):
        accs = (dgfq_ref, dgfk_ref, dgcq_ref, dgckv_ref, dgmq_ref, dgmk_ref, dbf_ref, dwuq_ref, dwuk_ref, dwuv_ref)

        @pl.when(pl.program_id(0) == 0)
        def _():
            for a in accs:
                a[...] = jnp.zeros_like(a)

        for (c0, g_ref, d_ref, dg_ref) in ((C_FQ, gfq_ref, dfq_ref, dgfq_ref), (C_FK, gfk_ref, dfk_ref, dgfk_ref)):
            dg = jnp.zeros((1, 128), F32)
            for blk in range(4):
                x = p_ref[:, c0 + 128 * blk:c0 + 128 * (blk + 1)]
                r = lax.rsqrt(_half_sum(x * x) * (1.0 / 64) + EPS)
                xh = x * r
                dy = d_ref[:, 128 * blk:128 * (blk + 1)]
                dg = dg + _colsum(dy * xh)
                dxh = dy * g_ref[...]
                dx = r * (dxh - xh * (_half_sum(dxh * xh) * (1.0 / 64)))
                dp_ref[:, c0 + 128 * blk:c0 + 128 * (blk + 1)] = dx.astype(BF16)
            dg_ref[...] += dg
        dp_ref[:, C_FV:C_FV + 512] = dfv_ref[...].astype(BF16)

        rcv = rc_ref[...]
        rsv = rs_ref[...]
        cqn, cqh, cqr = _rms(p_ref[:, C_CQ:C_CQ + QR], gcq_ref[...], QR)
        cqb = cqn.astype(BF16)
        qpre = _dot(cqb, wuq_ref[...])
        dgq = jnp.zeros((1, 128), F32)
        for hh in range(NH):
            sl = slice(128 * hh, 128 * (hh + 1))
            _, xh, r = _rms(qpre[:, sl], gmq_ref[...], MLA_QK)
            dout = dqm_ref[:, sl]
            dqn = dout * rcv + _rope_swap(dout * rsv)
            dgq = dgq + _colsum(dqn * xh)
            dqpre_sc[:, sl] = _rms_bwd(dqn, xh, r, gmq_ref[...], MLA_QK).astype(BF16)
        dgmq_ref[...] += dgq
        dqpre = dqpre_sc[...]
        dwuq_ref[...] += _dot(cqb, dqpre, TN)
        dcqn = _dot(dqpre, wuq_ref[...], NT)
        dgcq_ref[...] += _colsum(dcqn * cqh)
        dp_ref[:, C_CQ:C_CQ + QR] = _rms_bwd(dcqn, cqh, cqr, gcq_ref[...], QR).astype(BF16)

        ckvn, ckvh, ckvr = _rms(p_ref[:, C_CKV:C_CKV + KVR], gckv_ref[...], KVR)
        ckvb = ckvn.astype(BF16)
        kpre = _dot(ckvb, wuk_ref[...])
        misc = p_ref[:, C_MISC:C_MISC + 128]
        ln = _lane(misc.shape)
        is_kr = jnp.logical_and(ln >= L_KR, ln < L_KR + ROPE)
        kr = jnp.where(is_kr, misc, 0.0)
        dgk = jnp.zeros((1, 128), F32)
        dkr = jnp.zeros(misc.shape, F32)
        for hh in range(NH):
            sl = slice(128 * hh, 128 * (hh + 1))
            _, xh, r = _rms(kpre[:, sl] + kr, gmk_ref[...], MLA_QK)
            dout = dkm_ref[:, sl]
            dkn = dout * rcv + _rope_swap(dout * rsv)
            dgk = dgk + _colsum(dkn * xh)
            dkx = _rms_bwd(dkn, xh, r, gmk_ref[...], MLA_QK)
            dkr = dkr + jnp.where(is_kr, dkx, 0.0)
            dkpre_sc[:, sl] = jnp.where(ln < 64, dkx, 0.0).astype(BF16)
        dgmk_ref[...] += dgk
        dkpre = dkpre_sc[...]
        dvmb = dvm_ref[...].astype(BF16)
        dwuk_ref[...] += _dot(ckvb, dkpre, TN)
        dwuv_ref[...] += _dot(ckvb, dvmb, TN)
        dckvn = _dot(dkpre, wuk_ref[...], NT) + _dot(dvmb, wuv_ref[...], NT)
        dgckv_ref[...] += _colsum(dckvn * ckvh)
        dp_ref[:, C_CKV:C_CKV + KVR] = _rms_bwd(dckvn, ckvh, ckvr, gckv_ref[...], KVR).astype(BF16)

        z = misc + bf_ref[...]
        dz = dlf_ref[...] * (1.0 - jax.nn.sigmoid(z))
        dbf_ref[...] += _colsum(dz)
        dp_ref[:, C_MISC:C_MISC + 128] = (dkr + dz).astype(BF16)

    row = lambda w: pl.BlockSpec((tm, w), lambda i: (i, 0))
    full = lambda a: pl.BlockSpec(a.shape, lambda i: (0, 0))
    small = (gfq, gfk, gcq, gckv, gmq, gmk, bfv, wuq, wuk, wuv)
    acc_shapes = [(1, 128), (1, 128), (1, QR), (1, KVR), (1, 128), (1, 128), (1, 128),
                  (QR, 1024), (KVR, 1024), (KVR, 512)]
    return _call(
        body, name=name, grid=(T // tm,),
        in_specs=[row(PROJW // 2), row(128), row(128)] + [full(a) for a in small]
                 + [row(512), row(512), row(512), row(1024), row(1024), row(512), row(128)],
        out_specs=[row(PROJW // 2)] + [pl.BlockSpec(s, lambda i: (0, 0)) for s in acc_shapes],
        out_shape=[jax.ShapeDtypeStruct((T, PROJW // 2), BF16)] + [jax.ShapeDtypeStruct(s, F32) for s in acc_shapes],
        scratch_shapes=[pltpu.VMEM((tm, 1024), BF16), pltpu.VMEM((tm, 1024), BF16)],
        compiler_params=_cp(),
    )(proj, rc, rs, *small, dfq, dfk, dfv, dqm, dkm, dvm, dlf)


def _scan_lanes(x, reverse):
    n = x.shape[-1]
    ln = _lane(x.shape)
    k = 1
    while k < n:
        if reverse:
            x = x + jnp.where(ln < n - k, pltpu.roll(x, n - k, x.ndim - 1), 0.0)
        else:
            x = x + jnp.where(ln >= k, pltpu.roll(x, k, x.ndim - 1), 0.0)
        k *= 2
    return x


def _forget_scan(lf, reverse, name):
    def body(x_ref, o_ref):
        x = x_ref[...]
        ln = _lane(x.shape)
        pad = jnp.logical_and(ln >= NMETA, ln < MPAD)
        o_ref[...] = jnp.where(pad, 0.0, _scan_lanes(jnp.where(pad, 0.0, x), reverse))

    return _call(body, name=name, out_shape=jax.ShapeDtypeStruct(lf.shape, F32), compiler_params=_cp())(lf)


def _attn_blocks(LP, tq):
    return [(0, MPAD, MPAD)] + [(MPAD + i * tq, tq, MPAD + (i + 1) * tq) for i in range((LP - MPAD) // tq)]


def _attn_scores(q_ref, k_ref, e, r0, rn, kend, wide, scale, bias):
    if wide:
        qe = q_ref[r0:r0 + rn, 128 * e:128 * (e + 1)]
        ke = k_ref[0:kend, 128 * e:128 * (e + 1)]
    else:
        qb = q_ref[r0:r0 + rn, :]
        mine = (_lane(qb.shape) < 64) if e == 0 else (_lane(qb.shape) >= 64)
        qe = jnp.where(mine, qb, jnp.zeros_like(qb))
        ke = k_ref[0:kend, :]
    s = _dot(qe, ke, NT) * scale
    if bias is not None:
        ct_ref, cr_ref = bias
        s = s + ct_ref[0, r0:r0 + rn, e:e + 1] - cr_ref[0, :, 0:kend]
    qi = r0 + lax.broadcasted_iota(jnp.int32, (rn, kend), 0)
    ki = lax.broadcasted_iota(jnp.int32, (rn, kend), 1)
    ok = jnp.logical_and(ki <= qi, jnp.logical_or(ki < NMETA, ki >= MPAD))
    s = jnp.where(ok, s, -1e30)
    m = jnp.max(s, axis=-1, keepdims=True)
    p = jnp.exp(s - m)
    l = jnp.sum(p, axis=-1, keepdims=True)
    return qe, ke, p, l


def _attn_specs(B, LP, wide, has_bias):
    qw = 256 if wide else 128
    specs = [pl.BlockSpec((LP, qw), lambda b, hp: (b, hp)),
             pl.BlockSpec((LP, qw), lambda b, hp: (b, hp)),
             pl.BlockSpec((LP, 128), lambda b, hp: (b, hp))]
    bias_specs = []
    if has_bias:
        bias_specs = [pl.BlockSpec((1, LP, 2), lambda b, hp: (b * 4 + hp, 0, 0)),
                      pl.BlockSpec((1, 1, LP), lambda b, hp: (b * 8 + 2 * hp, 0, 0)),
                      pl.BlockSpec((1, 1, LP), lambda b, hp: (b * 8 + 2 * hp + 1, 0, 0))]
    return qw, specs, bias_specs


def _attn_fwd(q, k, v, bias, B, LP, wide, scale, name, tasks=()):
    T = q.shape[0]
    tq = 256
    blocks = _attn_blocks(LP, tq)
    qw, specs, bias_specs = _attn_specs(B, LP, wide, bias is not None)

    def body(*refs):
        if bias is not None:
            q_ref, k_ref, v_ref, ct_ref, cr0_ref, cr1_ref, o_ref = refs
            crs = (cr0_ref, cr1_ref)
        else:
            q_ref, k_ref, v_ref, o_ref = refs
        for (r0, rn, kend) in blocks:
            outs = []
            for e in (0, 1):
                bs = (ct_ref, crs[e]) if bias is not None else None
                _, _, p, l = _attn_scores(q_ref, k_ref, e, r0, rn, kend, wide, scale, bs)
                outs.append(_dot(p.astype(BF16), v_ref[0:kend, :]) / l)
            o = jnp.where(_lane(outs[0].shape) < 64, outs[0], outs[1])
            o_ref[r0:r0 + rn, :] = o.astype(BF16)

    args = (q, k, v) + ((bias[0], bias[1], bias[1]) if bias is not None else ())
    (out,), touts = _call_tasks(
        body, tasks, name=name, grid=(B, 4),
        in_specs=specs + bias_specs,
        out_specs=[pl.BlockSpec((LP, 128), lambda b, hp: (b, hp))],
        out_shape=[jax.ShapeDtypeStruct((T, 512), BF16)],
        args=args)
    return out, touts


def _attn_bwd(q, k, v, do, bias, B, LP, wide, scale, name, tasks=()):
    T = q.shape[0]
    tq = 256
    blocks = _attn_blocks(LP, tq)
    qw, specs, bias_specs = _attn_specs(B, LP, wide, bias is not None)
    has_bias = bias is not None

    def body(*refs):
        if has_bias:
            (q_ref, k_ref, v_ref, do_ref, ct_ref, cr0_ref, cr1_ref,
             dq_ref, dk_ref, dv_ref, dc0_ref, dc1_ref) = refs
            crs = (cr0_ref, cr1_ref)
            dcs = (dc0_ref, dc1_ref)
            dc0_ref[...] = jnp.zeros_like(dc0_ref)
            dc1_ref[...] = jnp.zeros_like(dc1_ref)
        else:
            q_ref, k_ref, v_ref, do_ref, dq_ref, dk_ref, dv_ref = refs
        dk_ref[...] = jnp.zeros_like(dk_ref)
        dv_ref[...] = jnp.zeros_like(dv_ref)
        for (r0, rn, kend) in blocks:
            dqs = []
            for e in (0, 1):
                bs = (ct_ref, crs[e]) if has_bias else None
                qe, ke, p, l = _attn_scores(q_ref, k_ref, e, r0, rn, kend, wide, scale, bs)
                pn = p / l
                dob = do_ref[r0:r0 + rn, :]
                mine = (_lane(dob.shape) < 64) if e == 0 else (_lane(dob.shape) >= 64)
                doe = jnp.where(mine, dob, jnp.zeros_like(dob))
                dp = _dot(doe, v_ref[0:kend, :], NT)
                delta = jnp.sum(pn * dp, axis=-1, keepdims=True)
                ds = pn * (dp - delta)
                dsb = ds.astype(BF16)
                dqe = _dot(dsb, ke) * scale
                dke = _dot(dsb, qe, TN) * scale
                if wide:
                    dq_ref[r0:r0 + rn, 128 * e:128 * (e + 1)] = dqe
                    dk_ref[0:kend, 128 * e:128 * (e + 1)] += dke
                else:
                    dqs.append(dqe)
                    dk_ref[0:kend, :] += dke
                dv_ref[0:kend, :] += _dot(pn.astype(BF16), doe, TN)
                if has_bias:
                    dcs[e][0, :, 0:kend] -= _colsum(ds)
            if not wide:
                dq_ref[r0:r0 + rn, :] = jnp.where(_lane(dqs[0].shape) < 64, dqs[0], dqs[1])

    args = (q, k, v, do) + ((bias[0], bias[1], bias[1]) if has_bias else ())
    out_specs = [pl.BlockSpec((LP, qw), lambda b, hp: (b, hp)),
                 pl.BlockSpec((LP, qw), lambda b, hp: (b, hp)),
                 pl.BlockSpec((LP, 128), lambda b, hp: (b, hp))]
    out_shape = [jax.ShapeDtypeStruct(q.shape, F32), jax.ShapeDtypeStruct(q.shape, F32),
                 jax.ShapeDtypeStruct((T, 512), F32)]
    if has_bias:
        out_specs += [pl.BlockSpec((1, 1, LP), lambda b, hp: (b * 4 + hp, 0, 0))] * 2
        out_shape += [jax.ShapeDtypeStruct((B * 4, 1, LP), F32)] * 2
    return _call_tasks(
        body, tasks, name=name, grid=(B, 4),
        in_specs=specs + [pl.BlockSpec((LP, 128), lambda b, hp: (b, hp))] + bias_specs,
        out_specs=out_specs, out_shape=out_shape, args=args)


def _post_fwd(h, of, om, proj, bg, wbf, wbm, wout, name):
    T = h.shape[0]
    tm = _tile(T, (512, 384, 256, 128))

    def body(h_ref, of_ref, om_ref, gl_ref, bg_ref, wbf_ref, wbm_ref, wo_ref, o_ref, mix_ref):
        gate = jax.nn.sigmoid(gl_ref[...] + bg_ref[...])
        mix = gate[:, :D] * _dot(of_ref[...], wbf_ref[...]) + gate[:, D:] * _dot(om_ref[...], wbm_ref[...])
        mb = mix.astype(BF16)
        mix_ref[...] = mb
        o_ref[...] = h_ref[...] + _dot(mb, wo_ref[...])

    row = lambda w: pl.BlockSpec((tm, w), lambda i: (i, 0))
    full = lambda a: pl.BlockSpec(a.shape, lambda i: (0, 0))
    return _call(
        body, name=name, grid=(T // tm,),
        in_specs=[row(D), row(512), row(512), pl.BlockSpec((tm, 2 * D), lambda i: (i, 1)),
                  full(bg), full(wbf), full(wbm), full(wout)],
        out_specs=[row(D), row(D)],
        out_shape=[jax.ShapeDtypeStruct((T, D), F32), jax.ShapeDtypeStruct((T, D), BF16)],
        compiler_params=_cp(),
    )(h, of, om, proj, bg, wbf, wbm, wout)


def _post_bwd(dh, of, om, proj, bg, wbf, wbm, wout, name):
    T = dh.shape[0]
    tm = _tile(T, (512, 384, 256, 128))

    def body(d_ref, of_ref, om_ref, gl_ref, bg_ref, wbf_ref, wbm_ref, wo_ref,
             dgl_ref, dbf_ref, dbm_ref, dof_ref, dom_ref, dbg_ref):
        @pl.when(pl.program_id(0) == 0)
        def _():
            dbg_ref[...] = jnp.zeros_like(dbg_ref)

        gate = jax.nn.sigmoid(gl_ref[...] + bg_ref[...])
        dmix = _dot(d_ref[...].astype(BF16), wo_ref[...], NT)
        ofx = _dot(of_ref[...], wbf_ref[...])
        omx = _dot(om_ref[...], wbm_ref[...])
        gf = gate[:, :D]
        gm = gate[:, D:]
        dof = (dmix * gf).astype(BF16)
        dom = (dmix * gm).astype(BF16)
        dglf = dmix * ofx * gf * (1.0 - gf)
        dglm = dmix * omx * gm * (1.0 - gm)
        dgl_ref[:, :D] = dglf.astype(BF16)
        dgl_ref[:, D:] = dglm.astype(BF16)
        dbg_ref[:, :D] += _colsum(dglf)
        dbg_ref[:, D:] += _colsum(dglm)
        dbf_ref[...] = dof
        dbm_ref[...] = dom
        dof_ref[...] = _dot(dof, wbf_ref[...], NT).astype(BF16)
        dom_ref[...] = _dot(dom, wbm_ref[...], NT).astype(BF16)

    row = lambda w: pl.BlockSpec((tm, w), lambda i: (i, 0))
    full = lambda a: pl.BlockSpec(a.shape, lambda i: (0, 0))
    return _call(
        body, name=name, grid=(T // tm,),
        in_specs=[row(D), row(512), row(512), pl.BlockSpec((tm, 2 * D), lambda i: (i, 1)),
                  full(bg), full(wbf), full(wbm), full(wout)],
        out_specs=[row(2 * D), row(D), row(D), row(512), row(512), pl.BlockSpec((1, 2 * D), lambda i: (0, 0))],
        out_shape=[jax.ShapeDtypeStruct((T, 2 * D), BF16), jax.ShapeDtypeStruct((T, D), BF16),
                   jax.ShapeDtypeStruct((T, D), BF16), jax.ShapeDtypeStruct((T, 512), BF16),
                   jax.ShapeDtypeStruct((T, 512), BF16), jax.ShapeDtypeStruct((1, 2 * D), F32)],
        compiler_params=_cp(),
    )(dh, of, om, proj, bg, wbf, wbm, wout)


def _loss_head(h3, target, B, LP, name):
    S = LP - MPAD
    nb = LP // 128

    def body(h_ref, t_ref, dy_ref, l_ref):
        b = pl.program_id(0)
        p = pl.program_id(1)

        @pl.when(jnp.logical_and(b == 0, p == 0))
        def _():
            l_ref[...] = jnp.zeros_like(l_ref)

        @pl.when(p == 0)
        def _():
            dy_ref[...] = jnp.zeros_like(dy_ref)

        @pl.when(p > 0)
        def _():
            e = h_ref[...] - t_ref[0]
            dy_ref[...] = e * (1.0 / D)
            l_ref[...] += jnp.sum(e * e, axis=0, keepdims=True) * (0.5 / D)

    return _call(
        body, name=name, grid=(B, nb),
        in_specs=[pl.BlockSpec((128, D), lambda b, p: (b * nb + p, 0)),
                  pl.BlockSpec((1, 128, D), lambda b, p: (b, jnp.maximum(p - 1, 0), 0))],
        out_specs=[pl.BlockSpec((128, D), lambda b, p: (b * nb + p, 0)),
                   pl.BlockSpec((1, D), lambda b, p: (0, 0))],
        out_shape=[jax.ShapeDtypeStruct(h3.shape, F32), jax.ShapeDtypeStruct((1, D), F32)],
        compiler_params=_cp(),
    )(h3, target)


def _rope_tables(B, LP):
    pos = jnp.concatenate([jnp.arange(MPAD, dtype=F32), NMETA + jnp.arange(LP - MPAD, dtype=F32)])
    inv_freq = ROPE_THETA ** (-jnp.arange(0, ROPE, 2, dtype=F32) / ROPE)
    ang = pos[:, None] * inv_freq[None, :]
    cos, sin = jnp.cos(ang), jnp.sin(ang)
    z32 = jnp.zeros((LP, 32), F32)
    rc = jnp.concatenate([jnp.ones((LP, 64), F32), cos, cos, z32], axis=1)
    rs = jnp.concatenate([jnp.zeros((LP, 64), F32), -sin, sin, z32], axis=1)
    return jnp.tile(rc, (B, 1)), jnp.tile(rs, (B, 1))


def _pad_lanes(v, start, width=128):
    n = v.shape[1]
    return jnp.concatenate([jnp.zeros((1, start), F32), v, jnp.zeros((1, width - start - n), F32)], axis=1)


G_FFN1 = ["ffn1_w_gu", "ffn1_w_down"]
G_MIX = ["w_in", "mla_w_uq", "mla_w_ukv", "w_branch_fox", "w_branch_mla", "w_out"]
G_OUT = ["w_out", "w_branch_fox", "w_branch_mla"]
G_IN = ["w_in", "mla_w_uq", "mla_w_ukv"]


def _step(x, target, meta, vec, gath, shards):
    dist = shards is not None
    B, S, _ = x.shape
    LP = MPAD + S
    T = B * LP
    gath = dict(gath)

    def gather(names):
        return [_gather_task([shards[n] for n in names])] if dist else []

    def gathered(names, touts):
        if dist:
            gath.update(zip(names, touts[0]))

    g4, sums, red = {}, {}, {}

    def scatter(names):
        return [_a2a_task([_pieces(g4[n]) for n in names])] if dist else []

    def scattered(names, tout, me):
        for n, r in zip(names, tout):
            sums[n] = _sum_pieces(r, _pieces(g4[n]), me, "rs_sum_" + n)

    def join(names):
        return [_join_task([sums[n] for n in names])] if dist else []

    def joined(names, tout):
        for n, r in zip(names, tout):
            red[n] = (sums[n], r)

    me = None
    if dist:
        me = (4 * lax.axis_index("x") + 2 * lax.axis_index("y") + lax.axis_index("c")).reshape(1).astype(jnp.int32)

    h0 = jnp.concatenate([jnp.broadcast_to(meta[None], (B, NMETA, D)),
                          jnp.zeros((B, MPAD - NMETA, D), F32), x], axis=1).reshape(T, D)
    rc, rs = _rope_tables(B, LP)
    gfq = jnp.tile(vec["fox_q_norm"], (1, 2))
    gfk = jnp.tile(vec["fox_k_norm"], (1, 2))
    gmq = _pad_lanes(vec["mla_q_norm"], 0)
    gmk = _pad_lanes(vec["mla_k_norm"], 0)
    bfv = _pad_lanes(vec["b_forget"], L_FL)

    w1gu, w1d = gath["ffn1_w_gu"], gath["ffn1_w_down"].reshape(2, FH, D)
    h1, touts = _ffn_fwd(h0, vec["ffn1_norm"], w1gu, w1d, "ffn1_fwd", gather(G_MIX))
    gathered(G_MIX, touts)
    wm = _mixer_weights(gath)
    small = (gfq, gfk, vec["mla_cq_norm"], vec["mla_ckv_norm"], gmq, gmk, bfv, wm["wuq"], wm["wuk"], wm["wuv"])
    proj, u2 = _inproj_fwd(h1, vec["mix_norm"], wm["w_in"], "inproj_fwd")
    fq, fk, fv, qm, km, vm, lf = _prep_fwd(proj, rc, rs, *small, name="prep_fwd")
    lf_rows = lf[:, L_FL:L_FL + NH].reshape(B, LP, NH).transpose(0, 2, 1).reshape(B * NH, LP)
    crow = _forget_scan(lf_rows, False, "forget_scan")
    ctok = crow.reshape(B, 4, 2, LP).transpose(0, 1, 3, 2).reshape(B * 4, LP, 2)
    bias = (ctok, crow.reshape(B * NH, 1, LP))
    of, touts = _attn_fwd(fq, fk, fv, bias, B, LP, False, 64 ** -0.5, "fox_fwd", gather(["ffn2_w_gu"]))
    gathered(["ffn2_w_gu"], touts)
    om, touts = _attn_fwd(qm, km, vm, None, B, LP, True, MLA_QK ** -0.5, "mla_fwd", gather(["ffn2_w_down"]))
    gathered(["ffn2_w_down"], touts)
    h2, mix = _post_fwd(h1, of, om, proj, vec["b_gate"], wm["wbf"], wm["wbm"], wm["w_out"], "post_fwd")
    w2gu, w2d = gath["ffn2_w_gu"], gath["ffn2_w_down"].reshape(2, FH, D)
    h3, _ = _ffn_fwd(h2, vec["ffn2_norm"], w2gu, w2d, "ffn2_fwd")
    dy, lpart = _loss_head(h3, target, B, LP, "loss_head")

    gv = {}
    (dh2, u3, a2, dg2, dp2, gv["ffn2_norm"]), _ = _ffn_bwd(h2, dy, vec["ffn2_norm"], w2gu, w2d, "ffn2_bwd")
    g4["ffn2_w_gu"] = jnp.concatenate([_wgrad(u3, dg2, "ffn2_dwg", bn=FH, shard_major=True)[0],
                                       _wgrad(u3, dp2, "ffn2_dwu", bn=FH, shard_major=True)[0]], axis=0)
    g4["ffn2_w_down"] = _wgrad(a2, dy, "ffn2_dwd", scale=0.5, bk=FH)[0].reshape(N_CHIPS, DFF // N_CHIPS, D)

    dgl, dbf, dbm, dof, dom, gv["b_gate"] = _post_bwd(dh2, of, om, proj, vec["b_gate"], wm["wbf"], wm["wbm"], wm["w_out"], "post_bwd")
    g4["w_out"] = _wgrad(mix, dh2, "dw_out")[0].reshape(N_CHIPS, D // N_CHIPS, D)
    g4["w_branch_fox"] = _cols_to_shards(_wgrad(of, dbf, "dw_bf")[0])
    g4["w_branch_mla"] = _cols_to_shards(_wgrad(om, dbm, "dw_bm")[0])
    G_FFN2 = ["ffn2_w_gu", "ffn2_w_down"]
    (dfq, dfk, dfv, dc0, dc1), touts = _attn_bwd(fq, fk, fv, dof, bias, B, LP, False, 64 ** -0.5, "fox_bwd", scatter(G_FFN2))
    if dist:
        scattered(G_FFN2, touts[0], me)
    (dqm, dkm, dvm), touts = _attn_bwd(qm, km, vm, dom, None, B, LP, True, MLA_QK ** -0.5, "mla_bwd",
                                       scatter(G_OUT) + join(G_FFN2))
    if dist:
        scattered(G_OUT, touts[0], me)
        joined(G_FFN2, touts[1])
    dc = jnp.concatenate([dc0, dc1], axis=1).reshape(B * NH, LP)
    dlf_rows = _forget_scan(dc, True, "forget_scan_bwd")
    dlf = dlf_rows.reshape(B, NH, LP).transpose(0, 2, 1).reshape(T, NH)
    dlf = jnp.concatenate([jnp.zeros((T, L_FL), F32), dlf, jnp.zeros((T, 128 - L_FL - NH), F32)], axis=1)
    (dlo, dgfq, dgfk, gv["mla_cq_norm"], gv["mla_ckv_norm"], dgmq, dgmk, dbfv,
     dwuq, dwuk, dwuv) = _prep_bwd(proj, rc, rs, *small, dfq, dfk, dfv, dqm, dkm, dvm, dlf, name="prep_bwd")
    gv["fox_q_norm"] = dgfq[:, :64] + dgfq[:, 64:]
    gv["fox_k_norm"] = dgfk[:, :64] + dgfk[:, 64:]
    gv["mla_q_norm"] = dgmq[:, :MLA_QK]
    gv["mla_k_norm"] = dgmk[:, :MLA_QK]
    gv["b_forget"] = dbfv[:, L_FL:L_FL + NH]
    dwin = jnp.concatenate([_wgrad(u2, dlo, "dw_in_lo")[0], _wgrad(u2, dgl, "dw_in_hi")[0]], axis=1)
    g4["w_in"] = _cols_to_shards(_win_from_kernel(dwin))
    g4["mla_w_uq"] = _cols_to_shards(
        dwuq.astype(GRAD_DTYPE).reshape(QR, NH, 128)[:, :, :MLA_QK].reshape(QR, NH * MLA_QK))
    dukv = jnp.concatenate([dwuk.reshape(KVR, NH, 128)[:, :, :64], dwuv.reshape(KVR, NH, 64)], axis=2)
    g4["mla_w_ukv"] = _cols_to_shards(dukv.astype(GRAD_DTYPE).reshape(KVR, NH * 128))
    dh1, gv["mix_norm"] = _inproj_bwd(h1, dh2, dlo, dgl, vec["mix_norm"], wm["w_in"], "inproj_bwd")

    (dh0, u1, a1, dg1, dp1, gv["ffn1_norm"]), touts = _ffn_bwd(h0, dh1, vec["ffn1_norm"], w1gu, w1d, "ffn1_bwd",
                                                                scatter(G_IN) + join(G_OUT))
    if dist:
        scattered(G_IN, touts[0], me)
        joined(G_OUT, touts[1])
    g4["ffn1_w_gu"] = jnp.concatenate([_wgrad(u1, dg1, "ffn1_dwg", bn=FH, shard_major=True)[0],
                                       _wgrad(u1, dp1, "ffn1_dwu", bn=FH, shard_major=True)[0]], axis=0)
    dwd1, touts = _wgrad(a1, dh1, "ffn1_dwd", scale=0.5, bk=FH, tasks=scatter(["ffn1_w_gu"]) + join(G_IN))
    g4["ffn1_w_down"] = dwd1.reshape(N_CHIPS, DFF // N_CHIPS, D)
    if dist:
        scattered(["ffn1_w_gu"], touts[0], me)
        joined(G_IN, touts[1])
        scattered(["ffn1_w_down"], _run_tasks(scatter(["ffn1_w_down"]), "rs_ffn1_w_down")[0], me)
        joined(G_FFN1, _run_tasks(join(G_FFN1), "rs_join_ffn1")[0])
    dh0 = dh0.reshape(B, LP, D)
    grad_x = dh0[:, MPAD:]
    grad_meta = jnp.sum(dh0[:, :NMETA], axis=0)
    return lpart, grad_x, grad_meta, gv, (red if dist else g4)


def _cols_from_shards(g4):
    n, r, c = g4.shape
    return g4.transpose(1, 0, 2).reshape(r, n * c)


def _cols_to_shards(full):
    r, c4 = full.shape
    return full.reshape(r, N_CHIPS, c4 // N_CHIPS).transpose(1, 0, 2)


def _win_to_kernel(wfull):
    z = lambda n: jnp.zeros((D, n), wfull.dtype)
    fl, cq, ckv, kr, gate = (wfull[:, 1536:1544], wfull[:, 1544:1800], wfull[:, 1800:1928],
                             wfull[:, 1928:1960], wfull[:, 1960:4008])
    misc = jnp.concatenate([z(L_KR), kr, fl, z(128 - L_FL - NH)], axis=1)
    return jnp.concatenate([wfull[:, :1536], cq, ckv, misc, gate], axis=1)


def _win_from_kernel(gk):
    m = C_MISC
    return jnp.concatenate([gk[:, :1536], gk[:, m + L_FL:m + L_FL + NH], gk[:, C_CQ:C_CQ + QR],
                            gk[:, C_CKV:C_CKV + KVR], gk[:, m + L_KR:m + L_KR + ROPE], gk[:, C_GATE:]], axis=1)


def _pieces(g4):
    n, r, c = g4.shape
    return g4.reshape(2 * n, r // 2, c)


def _mixer_weights(gath):
    w = {}
    w["w_in"] = _win_to_kernel(_cols_from_shards(gath["w_in"]))
    uq = _cols_from_shards(gath["mla_w_uq"]).reshape(QR, NH, MLA_QK)
    w["wuq"] = jnp.pad(uq, ((0, 0), (0, 0), (0, 128 - MLA_QK))).reshape(QR, NH * 128)
    ukv = _cols_from_shards(gath["mla_w_ukv"]).reshape(KVR, NH, 128)
    w["wuk"] = jnp.pad(ukv[:, :, :64], ((0, 0), (0, 0), (0, 64))).reshape(KVR, NH * 128)
    w["wuv"] = ukv[:, :, 64:].reshape(KVR, NH * 64)
    w["wbf"] = _cols_from_shards(gath["w_branch_fox"])
    w["wbm"] = _cols_from_shards(gath["w_branch_mla"])
    w["w_out"] = gath["w_out"].reshape(D, D)
    return w


def _chip_peers(x, y):
    return [(1 - x, y), (x, 1 - y), (1 - x, 1 - y)]


RELS = [(dx, dy, dc) for dx in (0, 1) for dy in (0, 1) for dc in (0, 1)][1:]


def _here():
    return lax.axis_index("x"), lax.axis_index("y"), lax.axis_index("c")


def _flip(a, d):
    return (1 - a) if d else a


def _remote(src, dst, send, recv, i, dev):
    return functools.partial(pltpu.make_async_remote_copy, src_ref=src, dst_ref=dst, send_sem=send.at[i],
                             recv_sem=recv.at[i], device_id=dev, device_id_type=MESH)


def _gather_task(shards):
    n = len(shards)

    def descs(ins, outs, sems):
        send, recv, loc = sems
        x, y, c = _here()
        j = 2 * x + y
        locs, pairs = [], []
        for k in range(n):
            locs.append(functools.partial(pltpu.make_async_copy, ins[k], outs[k].at[j], loc.at[k]))
            for r, (px, py) in enumerate(_chip_peers(x, y)):
                dev = (px, py, c)
                pairs.append((_remote(ins[k], outs[k].at[j], send, recv, 3 * k + r, dev),
                              _remote(ins[k], outs[k].at[2 * px + py], send, recv, 3 * k + r, dev)))
        return locs, pairs

    return _Task(shards, [jax.ShapeDtypeStruct((N_CHIPS,) + s.shape, s.dtype) for s in shards],
                 [pltpu.SemaphoreType.DMA((3 * n,)), pltpu.SemaphoreType.DMA((3 * n,)), pltpu.SemaphoreType.DMA((n,))],
                 descs)


def _a2a_task(ps):
    n = len(ps)
    nr = len(RELS)

    def descs(ins, outs, sems):
        send, recv = sems
        x, y, c = _here()
        me = 4 * x + 2 * y + c
        pairs = []
        for k in range(n):
            for i, (dx, dy, dc) in enumerate(RELS):
                dev = (_flip(x, dx), _flip(y, dy), _flip(c, dc))
                peer = 4 * dev[0] + 2 * dev[1] + dev[2]
                pairs.append((_remote(ins[k].at[peer], outs[k].at[me], send, recv, nr * k + i, dev),
                              _remote(ins[k].at[peer], outs[k].at[peer], send, recv, nr * k + i, dev)))
        return [], pairs

    return _Task(ps, [jax.ShapeDtypeStruct(p.shape, p.dtype) for p in ps],
                 [pltpu.SemaphoreType.DMA((nr * n,)), pltpu.SemaphoreType.DMA((nr * n,))], descs)


def _join_task(ss):
    n = len(ss)

    def descs(ins, outs, sems):
        send, recv = sems
        x, y, c = _here()
        pairs = []
        for k in range(n):
            cp = _remote(ins[k], outs[k], send, recv, k, (x, y, 1 - c))
            pairs.append((cp, cp))
        return [], pairs

    return _Task(ss, [jax.ShapeDtypeStruct(s.shape, s.dtype) for s in ss],
                 [pltpu.SemaphoreType.DMA((n,)), pltpu.SemaphoreType.DMA((n,))], descs)


def _sum_pieces(recv, own, me, name):
    n, h, c = recv.shape
    tr = _tile(h, (256, 176, 128, 64))

    def body(me_ref, r_ref, o_ref, out_ref):
        s = pl.program_id(1)
        val = jnp.where(s == me_ref[0], o_ref[0], r_ref[0]).astype(F32)

        @pl.when(s == 0)
        def _():
            out_ref[...] = val

        @pl.when(s > 0)
        def _():
            out_ref[...] += val

    def other(s, m):
        return jnp.where(s == m[0], (s + 1) % n, s)

    return _call(
        body, name=name,
        grid_spec=pltpu.PrefetchScalarGridSpec(
            num_scalar_prefetch=1, grid=(h // tr, n),
            in_specs=[pl.BlockSpec((1, tr, c), lambda i, s, m: (other(s, m), i, 0)),
                      pl.BlockSpec((1, tr, c), lambda i, s, m: (m[0], i, 0))],
            out_specs=pl.BlockSpec((tr, c), lambda i, s, m: (i, 0))),
        out_shape=jax.ShapeDtypeStruct((h, c), F32),
        compiler_params=_cp(),
    )(me, recv, own)


def _all_reduce_small(v, name):
    r = v.shape[0]

    def body(v_ref, o_ref, gat_ref, send, recv):
        x, y, c = lax.axis_index("x"), lax.axis_index("y"), lax.axis_index("c")
        me = 4 * x + 2 * y + c
        gat_ref[me] = v_ref[...]
        rels = [(dx, dy, dc) for dx in (0, 1) for dy in (0, 1) for dc in (0, 1)][1:]
        flip = lambda a, d: (1 - a) if d else a
        cps = []
        for i, (dx, dy, dc) in enumerate(rels):
            cp = pltpu.make_async_remote_copy(src_ref=v_ref, dst_ref=gat_ref.at[me], send_sem=send.at[i], recv_sem=recv.at[i],
                                              device_id=(flip(x, dx), flip(y, dy), flip(c, dc)), device_id_type=MESH)
            cp.start()
            cps.append(cp)
        for i, (dx, dy, dc) in enumerate(rels):
            src = 4 * flip(x, dx) + 2 * flip(y, dy) + flip(c, dc)
            pltpu.make_async_remote_copy(src_ref=v_ref, dst_ref=gat_ref.at[src], send_sem=send.at[i], recv_sem=recv.at[i],
                                         device_id=(flip(x, dx), flip(y, dy), flip(c, dc)), device_id_type=MESH).wait_recv()
        for cp in cps:
            cp.wait_send()
        acc = gat_ref[0]
        for s in range(1, N_DEV):
            acc = acc + gat_ref[s]
        o_ref[...] = acc

    vm = pl.BlockSpec(memory_space=pltpu.VMEM)
    return _call(
        body, name=name, in_specs=[vm], out_specs=vm,
        out_shape=jax.ShapeDtypeStruct(v.shape, F32),
        scratch_shapes=[pltpu.VMEM((N_DEV, r, 128), F32), pltpu.SemaphoreType.DMA((N_DEV - 1,)),
                        pltpu.SemaphoreType.DMA((N_DEV - 1,))],
    )(v)


def _adamw_update(gg, w, m, v):
    c1 = 1.0 / (1.0 - ADAM_B1 ** ADAM_STEP)
    c2 = 1.0 / (1.0 - ADAM_B2 ** ADAM_STEP)
    nm = ADAM_B1 * m + (1.0 - ADAM_B1) * gg
    nv = ADAM_B2 * v + (1.0 - ADAM_B2) * (gg * gg)
    return -ADAM_LR * ((nm * c1) / (jnp.sqrt(nv * c2) + ADAM_EPS) + ADAM_WD * w), nm, nv


def _adamw(wt, g, m, v, name):
    def body(w_ref, g_ref, m_ref, v_ref, d_ref, nm_ref, nv_ref):
        d_ref[...], nm_ref[...], nv_ref[...] = _adamw_update(g_ref[...], w_ref[...], m_ref[...], v_ref[...])

    return _call(body, name=name, out_shape=[jax.ShapeDtypeStruct(wt.shape, F32)] * 3, compiler_params=_cp())(wt, g, m, v)


def _adamw_halves(wt, mine, theirs, m, v, core, name):
    r, c = wt.shape
    h = r // 2
    tr = _tile(h, (256, 176, 128, 64))
    nh = h // tr

    def body(c_ref, w_ref, a_ref, b_ref, m_ref, v_ref, g_ref, d_ref, nm_ref, nv_ref):
        gg = jnp.where(pl.program_id(0) // nh == c_ref[0], a_ref[...], b_ref[...])
        g_ref[...] = gg
        d_ref[...], nm_ref[...], nv_ref[...] = _adamw_update(gg, w_ref[...], m_ref[...], v_ref[...])

    full = pl.BlockSpec((tr, c), lambda i, cr: (i, 0))
    half = pl.BlockSpec((tr, c), lambda i, cr: (i % nh, 0))
    return _call(
        body, name=name,
        grid_spec=pltpu.PrefetchScalarGridSpec(
            num_scalar_prefetch=1, grid=(2 * nh,),
            in_specs=[full, half, half, full, full], out_specs=[full] * 4),
        out_shape=[jax.ShapeDtypeStruct((r, c), F32)] * 4,
        compiler_params=_cp(),
    )(core, wt, mine, theirs, m, v)


MATS = ["ffn1_w_gu", "ffn1_w_down", "w_in", "mla_w_uq", "mla_w_ukv", "w_branch_fox", "w_branch_mla",
        "w_out", "ffn2_w_gu", "ffn2_w_down"]
VECS = ["ffn1_norm", "mix_norm", "b_forget", "b_gate", "fox_q_norm", "fox_k_norm", "mla_cq_norm",
        "mla_ckv_norm", "mla_q_norm", "mla_k_norm", "ffn2_norm"]
WEIGHTS = ["meta_tokens", "ffn1_norm", "ffn1_w_gu", "ffn1_w_down", "mix_norm", "w_in", "b_forget", "b_gate",
           "fox_q_norm", "fox_k_norm", "mla_cq_norm", "mla_w_uq", "mla_ckv_norm", "mla_w_ukv", "mla_q_norm",
           "mla_k_norm", "w_branch_fox", "w_branch_mla", "w_out", "ffn2_norm", "ffn2_w_gu", "ffn2_w_down"]


def _pack_small(parts):
    flat = jnp.concatenate([p.reshape(-1) for p in parts])
    n = flat.shape[0]
    rows = -(-n // 1024) * 8
    flat = jnp.concatenate([flat, jnp.zeros((rows * 128 - n,), F32)])
    return flat.reshape(rows, 128)


def _unpack_small(packed, like):
    flat = packed.reshape(-1)
    out, o = [], 0
    for p in like:
        out.append(flat[o:o + p.size].reshape(p.shape))
        o += p.size
    return out


def kernel(x, meta_tokens, ffn1_norm, ffn1_w_gu, ffn1_w_down, mix_norm, w_in, b_forget, b_gate, fox_q_norm, fox_k_norm, mla_cq_norm, mla_w_uq, mla_ckv_norm, mla_w_ukv, mla_q_norm, mla_k_norm, w_branch_fox, w_branch_mla, w_out, ffn2_norm, ffn2_w_gu, ffn2_w_down, loss_target, m_meta_tokens, m_ffn1_norm, m_ffn1_w_gu, m_ffn1_w_down, m_mix_norm, m_w_in, m_b_forget, m_b_gate, m_fox_q_norm, m_fox_k_norm, m_mla_cq_norm, m_mla_w_uq, m_mla_ckv_norm, m_mla_w_ukv, m_mla_q_norm, m_mla_k_norm, m_w_branch_fox, m_w_branch_mla, m_w_out, m_ffn2_norm, m_ffn2_w_gu, m_ffn2_w_down, v_meta_tokens, v_ffn1_norm, v_ffn1_w_gu, v_ffn1_w_down, v_mix_norm, v_w_in, v_b_forget, v_b_gate, v_fox_q_norm, v_fox_k_norm, v_mla_cq_norm, v_mla_w_uq, v_mla_ckv_norm, v_mla_w_ukv, v_mla_q_norm, v_mla_k_norm, v_w_branch_fox, v_w_branch_mla, v_w_out, v_ffn2_norm, v_ffn2_w_gu, v_ffn2_w_down):
    a = dict(locals())
    wts = {n: a[n] for n in WEIGHTS}
    ms = {n: a["m_" + n] for n in WEIGHTS}
    vs = {n: a["v_" + n] for n in WEIGHTS}
    cx, cy, cc = lax.axis_index("x"), lax.axis_index("y"), lax.axis_index("c")
    chip = 2 * cx + cy

    shards = {n: wts[n][0].astype(BF16) for n in MATS}
    first = _run_tasks([_gather_task([shards[n] for n in G_FFN1] + [meta_tokens])], "gather_ffn1")[0]
    gath = dict(zip(G_FFN1, first[:-1]))
    meta_full = _cols_from_shards(first[-1])

    lpart, grad_x, gmeta, gv, gred = _step(x, loss_target, meta_full, {n: wts[n] for n in VECS}, gath, shards)
    loss = lax.psum(jnp.sum(lpart), ("x", "y", "c"))

    small_parts = [gv[n] for n in VECS] + [gmeta]
    red = _unpack_small(_all_reduce_small(_pack_small(small_parts), "allreduce_small"), small_parts)
    gvec = dict(zip(VECS, red[:-1]))
    gmeta_shard = lax.dynamic_slice_in_dim(red[-1], chip * (D // N_CHIPS), D // N_CHIPS, axis=1)

    grads, delta, new_m, new_v = {}, {}, {}, {}
    core = cc.reshape(1).astype(jnp.int32)
    for n in MATS:
        shp = wts[n].shape
        mine, theirs = gred[n]
        res = _adamw_halves(wts[n][0], mine, theirs, ms[n][0], vs[n][0], core, "adamw_" + n)
        grads[n], delta[n], new_m[n], new_v[n] = (t.reshape(shp) for t in res)
    sm_names = VECS + ["meta_tokens"]
    sm_g = [gvec[n] for n in VECS] + [gmeta_shard]
    pk = lambda d: _pack_small([d[n] for n in sm_names])
    d_, m_, v_ = _adamw(pk(wts), _pack_small(sm_g), pk(ms), pk(vs), "adamw_small")
    like = [wts[n] for n in sm_names]
    for n, gg, dd, mm, vv in zip(sm_names, sm_g, _unpack_small(d_, like), _unpack_small(m_, like), _unpack_small(v_, like)):
        grads[n], delta[n], new_m[n], new_v[n] = gg.reshape(wts[n].shape), dd, mm, vv

    return (loss, grad_x, *[grads[n] for n in WEIGHTS], *[delta[n] for n in WEIGHTS],
            *[new_m[n] for n in WEIGHTS], *[new_v[n] for n in WEIGHTS])
```

```python
import functools

import jax
import jax.numpy as jnp
from jax import lax
from jax.experimental import pallas as pl
from jax.experimental.pallas import tpu as pltpu

F32 = jnp.float32
BF16 = jnp.bfloat16
MESH = pl.DeviceIdType.MESH

D = 1024
DFF = 2816
FH = DFF // 2
NMETA = 16
MPAD = 128
EPS = 1e-6
NH = 8
FOXW = 512
QR = 256
KVR = 128
ROPE = 32
MLA_QK = 96
PROJW = 4096
ROPE_THETA = 10000.0
N_CHIPS = 4
N_DEV = 8

ADAM_LR = 0.001
ADAM_B1 = 0.9
ADAM_B2 = 0.999
ADAM_EPS = 1e-08
ADAM_WD = 0.01
ADAM_STEP = 10

VMEM_LIMIT = 56 * 2**20
GRAD_DTYPE = BF16

NT = (((1,), (1,)), ((), ()))
TN = (((0,), (0,)), ((), ()))


def _call(body, **kw):
    return pl.pallas_call(body, **kw)


def _cp(**kw):
    return pltpu.CompilerParams(vmem_limit_bytes=VMEM_LIMIT, **kw)


HBM = pl.BlockSpec(memory_space=pltpu.HBM)


class _Task:
    def __init__(self, ins, out_shapes, sems, descs):
        self.ins, self.out_shapes, self.sems, self.descs = list(ins), list(out_shapes), list(sems), descs

    def start(self, ins, outs, sems):
        locs, pairs = self.descs(ins, outs, sems)
        for lc in locs:
            lc().start()
        for snd, _ in pairs:
            snd().start()

    def wait(self, ins, outs, sems):
        locs, pairs = self.descs(ins, outs, sems)
        for _, rcv in pairs:
            rcv().wait_recv()
        for snd, _ in pairs:
            snd().wait_send()
        for lc in locs:
            lc().wait()


def _call_tasks(body, tasks, *, name, grid, in_specs, out_specs, out_shape, args, scratch_shapes=()):
    in_specs, out_specs, out_shape, scratch_shapes = map(list, (in_specs, out_specs, out_shape, scratch_shapes))
    n_in, n_out, n_sc = len(in_specs), len(out_specs), len(scratch_shapes)
    t_in = [len(t.ins) for t in tasks]
    t_out = [len(t.out_shapes) for t in tasks]
    t_sem = [len(t.sems) for t in tasks]

    def wrapped(*refs):
        pos = [0]

        def take(n):
            pos[0] += n
            return refs[pos[0] - n:pos[0]]

        ins, tins = take(n_in), [take(n) for n in t_in]
        outs, touts = take(n_out), [take(n) for n in t_out]
        sc, tsems = take(n_sc), [take(n) for n in t_sem]
        if tasks:
            first = functools.reduce(jnp.logical_and, [pl.program_id(a) == 0 for a in range(len(grid))])
            last = functools.reduce(jnp.logical_and, [pl.program_id(a) == grid[a] - 1 for a in range(len(grid))])

            @pl.when(first)
            def _():
                for t, a, b, s in zip(tasks, tins, touts, tsems):
                    t.start(a, b, s)

        body(*ins, *outs, *sc)
        if tasks:
            @pl.when(last)
            def _():
                for t, a, b, s in zip(tasks, tins, touts, tsems):
                    t.wait(a, b, s)

    res = _call(
        wrapped, name=name, grid=grid,
        in_specs=in_specs + [HBM] * sum(t_in), out_specs=out_specs + [HBM] * sum(t_out),
        out_shape=out_shape + [s for t in tasks for s in t.out_shapes],
        scratch_shapes=scratch_shapes + [s for t in tasks for s in t.sems],
        compiler_params=_cp(),
    )(*args, *[a for t in tasks for a in t.ins])
    res = list(res)
    touts, pos = [], n_out
    for n in t_out:
        touts.append(res[pos:pos + n])
        pos += n
    return res[:n_out], touts


def _run_tasks(tasks, name):
    t_in = [len(t.ins) for t in tasks]
    t_out = [len(t.out_shapes) for t in tasks]
    t_sem = [len(t.sems) for t in tasks]

    def body(*refs):
        pos = [0]

        def take(n):
            pos[0] += n
            return refs[pos[0] - n:pos[0]]

        tins, touts, tsems = [take(n) for n in t_in], [take(n) for n in t_out], [take(n) for n in t_sem]
        for t, a, b, s in zip(tasks, tins, touts, tsems):
            t.start(a, b, s)
        for t, a, b, s in zip(tasks, tins, touts, tsems):
            t.wait(a, b, s)

    res = list(_call(
        body, name=name, in_specs=[HBM] * sum(t_in), out_specs=[HBM] * sum(t_out),
        out_shape=[s for t in tasks for s in t.out_shapes],
        scratch_shapes=[s for t in tasks for s in t.sems],
    )(*[a for t in tasks for a in t.ins]))
    touts, pos = [], 0
    for n in t_out:
        touts.append(res[pos:pos + n])
        pos += n
    return touts


def _tile(n, cands):
    for c in cands:
        if n % c == 0:
            return c
    raise ValueError(f"no tile for {n} among {cands}")


def _dot(a, b, dims=None):
    if dims is None:
        return jnp.dot(a, b, preferred_element_type=F32)
    return lax.dot_general(a, b, dims, preferred_element_type=F32)


def _rms(x, gain, n):
    r = lax.rsqrt(jnp.sum(x * x, axis=-1, keepdims=True) * (1.0 / n) + EPS)
    xh = x * r
    return xh * gain, xh, r


def _rms_bwd(dy, xh, r, gain, n):
    dxh = dy * gain
    return r * (dxh - xh * (jnp.sum(dxh * xh, axis=-1, keepdims=True) * (1.0 / n)))


def _lane(shape):
    return lax.broadcasted_iota(jnp.int32, shape, len(shape) - 1)


def _half_sum(x):
    lo = _lane(x.shape) < 64
    s_lo = jnp.sum(jnp.where(lo, x, 0.0), axis=-1, keepdims=True)
    s_hi = jnp.sum(jnp.where(lo, 0.0, x), axis=-1, keepdims=True)
    return jnp.where(lo, s_lo, s_hi)


def _rope_swap(x):
    ln = _lane(x.shape)
    sw = jnp.where(ln < 80, pltpu.roll(x, 112, 1), pltpu.roll(x, 16, 1))
    return jnp.where(jnp.logical_and(ln >= 64, ln < 96), sw, 0.0)


def _colsum(x):
    return jnp.sum(x, axis=0, keepdims=True)


def _ffn_fwd(h, norm, wgu, wd3, name, tasks=()):
    T = h.shape[0]
    tm = _tile(T, (512, 384, 256, 128))

    def body(h_ref, n_ref, wg_ref, wu_ref, wd_ref, o_ref, u_sc):
        @pl.when(pl.program_id(1) == 0)
        def _():
            x = h_ref[...]
            u, _, _ = _rms(x, n_ref[...], D)
            u_sc[...] = u.astype(BF16)
            o_ref[...] = x

        u = u_sc[...]
        g = _dot(u, wg_ref[0])
        p = _dot(u, wu_ref[0])
        a = (g * jax.nn.sigmoid(g)) * p
        o_ref[...] += 0.5 * _dot(a.astype(BF16), wd_ref[0])

    (out,), touts = _call_tasks(
        body, tasks, name=name, grid=(T // tm, 2),
        in_specs=[pl.BlockSpec((tm, D), lambda i, j: (i, 0)),
                  pl.BlockSpec((1, D), lambda i, j: (0, 0)),
                  pl.BlockSpec((1, D, FH), lambda i, j: (j, 0, 0)),
                  pl.BlockSpec((1, D, FH), lambda i, j: (j + 2, 0, 0)),
                  pl.BlockSpec((1, FH, D), lambda i, j: (j, 0, 0))],
        out_specs=[pl.BlockSpec((tm, D), lambda i, j: (i, 0))],
        out_shape=[jax.ShapeDtypeStruct((T, D), F32)],
        scratch_shapes=[pltpu.VMEM((tm, D), BF16)],
        args=(h, norm, wgu, wgu, wd3))
    return out, touts


def _ffn_bwd(h, dout, norm, wgu, wd3, name, tasks=()):
    T = h.shape[0]
    tm = _tile(T, (256, 128))
    nt = T // tm

    def body(h_ref, d_ref, n_ref, wg_ref, wu_ref, wd_ref,
             dh_ref, u_ref, a_ref, dg_ref, dp_ref, dn_ref, xh_sc, r_sc, u_sc, du_sc):
        i = pl.program_id(0)
        j = pl.program_id(1)

        @pl.when(jnp.logical_and(i == 0, j == 0))
        def _():
            dn_ref[...] = jnp.zeros_like(dn_ref)

        @pl.when(j == 0)
        def _():
            x = h_ref[...]
            u, xh, r = _rms(x, n_ref[...], D)
            xh_sc[...] = xh
            r_sc[...] = r
            ub = u.astype(BF16)
            u_sc[...] = ub
            u_ref[...] = ub
            du_sc[...] = jnp.zeros_like(du_sc)

        u = u_sc[...]
        g = _dot(u, wg_ref[0])
        p = _dot(u, wu_ref[0])
        s = jax.nn.sigmoid(g)
        sl = g * s
        dz = (0.5 * d_ref[...]).astype(BF16)
        da = _dot(dz, wd_ref[0], NT)
        dp = da * sl
        dg = (da * p) * (s * (1.0 + g * (1.0 - s)))
        a_ref[...] = (sl * p).astype(BF16)
        dgb = dg.astype(BF16)
        dpb = dp.astype(BF16)
        dg_ref[...] = dgb
        dp_ref[...] = dpb
        du_sc[...] += _dot(dgb, wg_ref[0], NT) + _dot(dpb, wu_ref[0], NT)

        @pl.when(j == 1)
        def _():
            du = du_sc[...]
            xh = xh_sc[...]
            dn_ref[...] += _colsum(du * xh)
            dh_ref[...] = d_ref[...] + _rms_bwd(du, xh, r_sc[...], n_ref[...], D)

    return _call_tasks(
        body, tasks, name=name, grid=(nt, 2),
        in_specs=[pl.BlockSpec((tm, D), lambda i, j: (i, 0)),
                  pl.BlockSpec((tm, D), lambda i, j: (i, 0)),
                  pl.BlockSpec((1, D), lambda i, j: (0, 0)),
                  pl.BlockSpec((1, D, FH), lambda i, j: (j, 0, 0)),
                  pl.BlockSpec((1, D, FH), lambda i, j: (j + 2, 0, 0)),
                  pl.BlockSpec((1, FH, D), lambda i, j: (j, 0, 0))],
        out_specs=[pl.BlockSpec((tm, D), lambda i, j: (i, 0)),
                   pl.BlockSpec((tm, D), lambda i, j: (i, 0)),
                   pl.BlockSpec((tm, FH), lambda i, j: (i, j)),
                   pl.BlockSpec((tm, FH), lambda i, j: (i, j)),
                   pl.BlockSpec((tm, FH), lambda i, j: (i, j)),
                   pl.BlockSpec((1, D), lambda i, j: (0, 0))],
        out_shape=[jax.ShapeDtypeStruct((T, D), F32),
                   jax.ShapeDtypeStruct((T, D), BF16),
                   jax.ShapeDtypeStruct((T, DFF), BF16),
                   jax.ShapeDtypeStruct((T, DFF), BF16),
                   jax.ShapeDtypeStruct((T, DFF), BF16),
                   jax.ShapeDtypeStruct((1, D), F32)],
        scratch_shapes=[pltpu.VMEM((tm, D), F32), pltpu.VMEM((tm, 1), F32),
                        pltpu.VMEM((tm, D), BF16), pltpu.VMEM((tm, D), F32)],
        args=(h, dout, norm, wgu, wgu, wd3))


def _wgrad(x, y, name, scale=1.0, bk=None, bn=None, shard_major=False, tasks=()):
    T, K = x.shape
    N = y.shape[1]
    bk = bk or K
    bn = bn or N
    bt = _tile(T, (512, 384, 256, 128))
    nt = T // bt

    def body(x_ref, y_ref, o_ref, acc_ref):
        t = pl.program_id(2)

        @pl.when(t == 0)
        def _():
            acc_ref[...] = jnp.zeros_like(acc_ref)

        acc_ref[...] += _dot(x_ref[...].astype(BF16), y_ref[...].astype(BF16), TN)

        @pl.when(t == nt - 1)
        def _():
            o_ref[...] = (acc_ref[...] * scale).astype(o_ref.dtype).reshape(o_ref.shape)

    if shard_major:
        assert bk == K
        out_spec = pl.BlockSpec((1, K, bn), lambda i, j, t: (j, 0, 0))
        out_shape = jax.ShapeDtypeStruct((N // bn, K, bn), GRAD_DTYPE)
    else:
        out_spec = pl.BlockSpec((bk, bn), lambda i, j, t: (i, j))
        out_shape = jax.ShapeDtypeStruct((K, N), GRAD_DTYPE)
    (out,), touts = _call_tasks(
        body, tasks, name=name, grid=(K // bk, N // bn, nt),
        in_specs=[pl.BlockSpec((bt, bk), lambda i, j, t: (t, i)),
                  pl.BlockSpec((bt, bn), lambda i, j, t: (t, j))],
        out_specs=[out_spec], out_shape=[out_shape],
        scratch_shapes=[pltpu.VMEM((bk, bn), F32)],
        args=(x, y))
    return out, touts


def _inproj_fwd(h, norm, w, name):
    T = h.shape[0]
    tm = _tile(T, (1088, 512, 384, 256, 128))
    tn = 1024

    def body(h_ref, n_ref, w_ref, o_ref, u_ref):
        @pl.when(pl.program_id(1) == 0)
        def _():
            u, _, _ = _rms(h_ref[...], n_ref[...], D)
            u_ref[...] = u.astype(BF16)

        o_ref[...] = _dot(u_ref[...], w_ref[...])

    return _call(
        body, name=name, grid=(T // tm, PROJW // tn),
        in_specs=[pl.BlockSpec((tm, D), lambda i, j: (i, 0)),
                  pl.BlockSpec((1, D), lambda i, j: (0, 0)),
                  pl.BlockSpec((D, tn), lambda i, j: (0, j))],
        out_specs=[pl.BlockSpec((tm, tn), lambda i, j: (i, j)),
                   pl.BlockSpec((tm, D), lambda i, j: (i, 0))],
        out_shape=[jax.ShapeDtypeStruct((T, PROJW), F32), jax.ShapeDtypeStruct((T, D), BF16)],
        compiler_params=_cp(),
    )(h, norm, w)


def _inproj_bwd(h, dres, dlo, dhi, norm, w, name):
    T = h.shape[0]
    tm = _tile(T, (512, 384, 256, 128))
    hw = PROJW // 2

    def body(h_ref, d_ref, lo_ref, hi_ref, n_ref, wlo_ref, whi_ref, dh_ref, dn_ref):
        @pl.when(pl.program_id(0) == 0)
        def _():
            dn_ref[...] = jnp.zeros_like(dn_ref)

        _, xh, r = _rms(h_ref[...], n_ref[...], D)
        du = _dot(lo_ref[...], wlo_ref[...], NT) + _dot(hi_ref[...], whi_ref[...], NT)
        dn_ref[...] += _colsum(du * xh)
        dh_ref[...] = d_ref[...] + _rms_bwd(du, xh, r, n_ref[...], D)

    return _call(
        body, name=name, grid=(T // tm,),
        in_specs=[pl.BlockSpec((tm, D), lambda i: (i, 0)),
                  pl.BlockSpec((tm, D), lambda i: (i, 0)),
                  pl.BlockSpec((tm, hw), lambda i: (i, 0)),
                  pl.BlockSpec((tm, hw), lambda i: (i, 0)),
                  pl.BlockSpec((1, D), lambda i: (0, 0)),
                  pl.BlockSpec((D, hw), lambda i: (0, 0)),
                  pl.BlockSpec((D, hw), lambda i: (0, 1))],
        out_specs=[pl.BlockSpec((tm, D), lambda i: (i, 0)),
                   pl.BlockSpec((1, D), lambda i: (0, 0))],
        out_shape=[jax.ShapeDtypeStruct((T, D), F32), jax.ShapeDtypeStruct((1, D), F32)],
        compiler_params=_cp(),
    )(h, dres, dlo, dhi, norm, w, w)


C_FQ, C_FK, C_FV, C_CQ, C_CKV, C_MISC, C_GATE = 0, 512, 1024, 1536, 1792, 1920, 2048
L_KR, L_FL = 64, 96


def _prep_fwd(proj, rc, rs, gfq, gfk, gcq, gckv, gmq, gmk, bfv, wuq, wuk, wuv, name):
    T = proj.shape[0]
    tm = _tile(T, (256, 128))

    def body(p_ref, rc_ref, rs_ref, gfq_ref, gfk_ref, gcq_ref, gckv_ref, gmq_ref, gmk_ref, bf_ref,
             wuq_ref, wuk_ref, wuv_ref, fq_ref, fk_ref, fv_ref, qm_ref, km_ref, vm_ref, lf_ref):
        for blk in range(4):
            for (c0, g_ref, o_ref) in ((C_FQ, gfq_ref, fq_ref), (C_FK, gfk_ref, fk_ref)):
                x = p_ref[:, c0 + 128 * blk:c0 + 128 * (blk + 1)]
                r = lax.rsqrt(_half_sum(x * x) * (1.0 / 64) + EPS)
                o_ref[:, 128 * blk:128 * (blk + 1)] = (x * r * g_ref[...]).astype(BF16)
        fv_ref[...] = p_ref[:, C_FV:C_FV + 512].astype(BF16)

        rcv = rc_ref[...]
        rsv = rs_ref[...]
        cqn, _, _ = _rms(p_ref[:, C_CQ:C_CQ + QR], gcq_ref[...], QR)
        qpre = _dot(cqn.astype(BF16), wuq_ref[...])
        ckvn, _, _ = _rms(p_ref[:, C_CKV:C_CKV + KVR], gckv_ref[...], KVR)
        ckvb = ckvn.astype(BF16)
        kpre = _dot(ckvb, wuk_ref[...])
        vm_ref[...] = _dot(ckvb, wuv_ref[...]).astype(BF16)
        misc = p_ref[:, C_MISC:C_MISC + 128]
        ln = _lane(misc.shape)
        kr = jnp.where(jnp.logical_and(ln >= L_KR, ln < L_KR + ROPE), misc, 0.0)
        for hh in range(NH):
            sl = slice(128 * hh, 128 * (hh + 1))
            qn, _, _ = _rms(qpre[:, sl], gmq_ref[...], MLA_QK)
            qm_ref[:, sl] = (qn * rcv + _rope_swap(qn) * rsv).astype(BF16)
            kn, _, _ = _rms(kpre[:, sl] + kr, gmk_ref[...], MLA_QK)
            km_ref[:, sl] = (kn * rcv + _rope_swap(kn) * rsv).astype(BF16)
        z = misc + bf_ref[...]
        lf_ref[...] = jnp.minimum(z, 0.0) - jnp.log(1.0 + jnp.exp(-jnp.abs(z)))

    row = lambda w: pl.BlockSpec((tm, w), lambda i: (i, 0))
    full = lambda a: pl.BlockSpec(a.shape, lambda i: (0, 0))
    return _call(
        body, name=name, grid=(T // tm,),
        in_specs=[row(PROJW // 2), row(128), row(128)] + [full(a) for a in (gfq, gfk, gcq, gckv, gmq, gmk, bfv, wuq, wuk, wuv)],
        out_specs=[row(512), row(512), row(512), row(1024), row(1024), row(512), row(128)],
        out_shape=[jax.ShapeDtypeStruct((T, 512), BF16), jax.ShapeDtypeStruct((T, 512), BF16),
                   jax.ShapeDtypeStruct((T, 512), BF16), jax.ShapeDtypeStruct((T, 1024), BF16),
                   jax.ShapeDtypeStruct((T, 1024), BF16), jax.ShapeDtypeStruct((T, 512), BF16),
                   jax.ShapeDtypeStruct((T, 128), F32)],
        compiler_params=_cp(),
    )(proj, rc, rs, gfq, gfk, gcq, gckv, gmq, gmk, bfv, wuq, wuk, wuv)


def _prep_bwd(proj, rc, rs, gfq, gfk, gcq, gckv, gmq, gmk, bfv, wuq, wuk, wuv,
              dfq, dfk, dfv, dqm, dkm, dvm, dlf, name):
    T = proj.shape[0]
    tm = _tile(T, (256, 128))

    def body(p_ref, rc_ref, rs_ref, gfq_ref, gfk_ref, gcq_ref, gckv_ref, gmq_ref, gmk_ref, bf_ref,
             wuq_ref, wuk_ref, wuv_ref, dfq_ref, dfk_ref, dfv_ref, dqm_ref, dkm_ref, dvm_ref, dlf_ref,
             dp_ref, dgfq_ref, dgfk_ref, dgcq_ref, dgckv_ref, dgmq_ref, dgmk_ref, dbf_ref,
             dwuq_ref, dwuk_ref, dwuv_ref, dqpre_sc, dkpre_sc):
        accs = (dgfq_ref, dgfk_ref, dgcq_ref, dgckv_ref, dgmq_ref, dgmk_ref, dbf_ref, dwuq_ref, dwuk_ref, dwuv_ref)

        @pl.when(pl.program_id(0) == 0)
        def _():
            for a in accs:
                a[...] = jnp.zeros_like(a)

        for (c0, g_ref, d_ref, dg_ref) in ((C_FQ, gfq_ref, dfq_ref, dgfq_ref), (C_FK, gfk_ref, dfk_ref, dgfk_ref)):
            dg = jnp.zeros((1, 128), F32)
            for blk in range(4):
                x = p_ref[:, c0 + 128 * blk:c0 + 128 * (blk + 1)]
                r = lax.rsqrt(_half_sum(x * x) * (1.0 / 64) + EPS)
                xh = x * r
                dy = d_ref[:, 128 * blk:128 * (blk + 1)]
                dg = dg + _colsum(dy * xh)
                dxh = dy * g_ref[...]
                dx = r * (dxh - xh * (_half_sum(dxh * xh) * (1.0 / 64)))
                dp_ref[:, c0 + 128 * blk:c0 + 128 * (blk + 1)] = dx.astype(BF16)
            dg_ref[...] += dg
        dp_ref[:, C_FV:C_FV + 512] = dfv_ref[...].astype(BF16)

        rcv = rc_ref[...]
        rsv = rs_ref[...]
        cqn, cqh, cqr = _rms(p_ref[:, C_CQ:C_CQ + QR], gcq_ref[...], QR)
        cqb = cqn.astype(BF16)
        qpre = _dot(cqb, wuq_ref[...])
        dgq = jnp.zeros((1, 128), F32)
        for hh in range(NH):
            sl = slice(128 * hh, 128 * (hh + 1))
            _, xh, r = _rms(qpre[:, sl], gmq_ref[...], MLA_QK)
            dout = dqm_ref[:, sl]
            dqn = dout * rcv + _rope_swap(dout * rsv)
            dgq = dgq + _colsum(dqn * xh)
            dqpre_sc[:, sl] = _rms_bwd(dqn, xh, r, gmq_ref[...], MLA_QK).astype(BF16)
        dgmq_ref[...] += dgq
        dqpre = dqpre_sc[...]
        dwuq_ref[...] += _dot(cqb, dqpre, TN)
        dcqn = _dot(dqpre, wuq_ref[...], NT)
        dgcq_ref[...] += _colsum(dcqn * cqh)
        dp_ref[:, C_CQ:C_CQ + QR] = _rms_bwd(dcqn, cqh, cqr, gcq_ref[...], QR).astype(BF16)

        ckvn, ckvh, ckvr = _rms(p_ref[:, C_CKV:C_CKV + KVR], gckv_ref[...], KVR)
        ckvb = ckvn.astype(BF16)
        kpre = _dot(ckvb, wuk_ref[...])
        misc = p_ref[:, C_MISC:C_MISC + 128]
        ln = _lane(misc.shape)
        is_kr = jnp.logical_and(ln >= L_KR, ln < L_KR + ROPE)
        kr = jnp.where(is_kr, misc, 0.0)
        dgk = jnp.zeros((1, 128), F32)
        dkr = jnp.zeros(misc.shape, F32)
        for hh in range(NH):
            sl = slice(128 * hh, 128 * (hh + 1))
            _, xh, r = _rms(kpre[:, sl] + kr, gmk_ref[...], MLA_QK)
            dout = dkm_ref[:, sl]
            dkn = dout * rcv + _rope_swap(dout * rsv)
            dgk = dgk + _colsum(dkn * xh)
            dkx = _rms_bwd(dkn, xh, r, gmk_ref[...], MLA_QK)
            dkr = dkr + jnp.where(is_kr, dkx, 0.0)
            dkpre_sc[:, sl] = jnp.where(ln < 64, dkx, 0.0).astype(BF16)
        dgmk_ref[...] += dgk
        dkpre = dkpre_sc[...]
        dvmb = dvm_ref[...].astype(BF16)
        dwuk_ref[...] += _dot(ckvb, dkpre, TN)
        dwuv_ref[...] += _dot(ckvb, dvmb, TN)
        dckvn = _dot(dkpre, wuk_ref[...], NT) + _dot(dvmb, wuv_ref[...], NT)
        dgckv_ref[...] += _colsum(dckvn * ckvh)
        dp_ref[:, C_CKV:C_CKV + KVR] = _rms_bwd(dckvn, ckvh, ckvr, gckv_ref[...], KVR).astype(BF16)

        z = misc + bf_ref[...]
        dz = dlf_ref[...] * (1.0 - jax.nn.sigmoid(z))
        dbf_ref[...] += _colsum(dz)
        dp_ref[:, C_MISC:C_MISC + 128] = (dkr + dz).astype(BF16)

    row = lambda w: pl.BlockSpec((tm, w), lambda i: (i, 0))
    full = lambda a: pl.BlockSpec(a.shape, lambda i: (0, 0))
    small = (gfq, gfk, gcq, gckv, gmq, gmk, bfv, wuq, wuk, wuv)
    acc_shapes = [(1, 128), (1, 128), (1, QR), (1, KVR), (1, 128), (1, 128), (1, 128),
                  (QR, 1024), (KVR, 1024), (KVR, 512)]
    return _call(
        body, name=name, grid=(T // tm,),
        in_specs=[row(PROJW // 2), row(128), row(128)] + [full(a) for a in small]
                 + [row(512), row(512), row(512), row(1024), row(1024), row(512), row(128)],
        out_specs=[row(PROJW // 2)] + [pl.BlockSpec(s, lambda i: (0, 0)) for s in acc_shapes],
        out_shape=[jax.ShapeDtypeStruct((T, PROJW // 2), BF16)] + [jax.ShapeDtypeStruct(s, F32) for s in acc_shapes],
        scratch_shapes=[pltpu.VMEM((tm, 1024), BF16), pltpu.VMEM((tm, 1024), BF16)],
        compiler_params=_cp(),
    )(proj, rc, rs, *small, dfq, dfk, dfv, dqm, dkm, dvm, dlf)


def _scan_lanes(x, reverse):
    n = x.shape[-1]
    ln = _lane(x.shape)
    k = 1
    while k < n:
        if reverse:
            x = x + jnp.where(ln < n - k, pltpu.roll(x, n - k, x.ndim - 1), 0.0)
        else:
            x = x + jnp.where(ln >= k, pltpu.roll(x, k, x.ndim - 1), 0.0)
        k *= 2
    return x


def _forget_scan(lf, reverse, name):
    def body(x_ref, o_ref):
        x = x_ref[...]
        ln = _lane(x.shape)
        pad = jnp.logical_and(ln >= NMETA, ln < MPAD)
        o_ref[...] = jnp.where(pad, 0.0, _scan_lanes(jnp.where(pad, 0.0, x), reverse))

    return _call(body, name=name, out_shape=jax.ShapeDtypeStruct(lf.shape, F32), compiler_params=_cp())(lf)


def _attn_blocks(LP, tq):
    return [(0, MPAD, MPAD)] + [(MPAD + i * tq, tq, MPAD + (i + 1) * tq) for i in range((LP - MPAD) // tq)]


def _attn_scores(q_ref, k_ref, e, r0, rn, kend, wide, scale, bias):
    if wide:
        qe = q_ref[r0:r0 + rn, 128 * e:128 * (e + 1)]
        ke = k_ref[0:kend, 128 * e:128 * (e + 1)]
    else:
        qb = q_ref[r0:r0 + rn, :]
        mine = (_lane(qb.shape) < 64) if e == 0 else (_lane(qb.shape) >= 64)
        qe = jnp.where(mine, qb, jnp.zeros_like(qb))
        ke = k_ref[0:kend, :]
    s = _dot(qe, ke, NT) * scale
    if bias is not None:
        ct_ref, cr_ref = bias
        s = s + ct_ref[0, r0:r0 + rn, e:e + 1] - cr_ref[0, :, 0:kend]
    neg = -1e30
    if r0 == 0:
        qi = lax.broadcasted_iota(jnp.int32, (rn, kend), 0)
        ki = lax.broadcasted_iota(jnp.int32, (rn, kend), 1)
        s = jnp.where(jnp.logical_and(ki <= qi, ki < NMETA), s, neg)
    else:
        d0 = kend - rn
        head = jnp.where(_lane((rn, MPAD)) < NMETA, s[:, :MPAD], neg)
        qi = lax.broadcasted_iota(jnp.int32, (rn, rn), 0)
        diag = jnp.where(_lane((rn, rn)) <= qi, s[:, d0:], neg)
        s = jnp.concatenate([head] + ([s[:, MPAD:d0]] if d0 > MPAD else []) + [diag], axis=1)
    m = jnp.max(s, axis=-1, keepdims=True)
    p = jnp.exp(s - m)
    l = jnp.sum(p, axis=-1, keepdims=True)
    return qe, ke, p, l


def _attn_specs(B, LP, wide, has_bias):
    qw = 256 if wide else 128
    specs = [pl.BlockSpec((LP, qw), lambda b, hp: (b, hp)),
             pl.BlockSpec((LP, qw), lambda b, hp: (b, hp)),
             pl.BlockSpec((LP, 128), lambda b, hp: (b, hp))]
    bias_specs = []
    if has_bias:
        bias_specs = [pl.BlockSpec((1, LP, 2), lambda b, hp: (b * 4 + hp, 0, 0)),
                      pl.BlockSpec((1, 1, LP), lambda b, hp: (b * 8 + 2 * hp, 0, 0)),
                      pl.BlockSpec((1, 1, LP), lambda b, hp: (b * 8 + 2 * hp + 1, 0, 0))]
    return qw, specs, bias_specs


def _attn_fwd(q, k, v, bias, B, LP, wide, scale, name, tasks=()):
    T = q.shape[0]
    tq = 256
    blocks = _attn_blocks(LP, tq)
    qw, specs, bias_specs = _attn_specs(B, LP, wide, bias is not None)

    def body(*refs):
        if bias is not None:
            q_ref, k_ref, v_ref, ct_ref, cr0_ref, cr1_ref, o_ref = refs
            crs = (cr0_ref, cr1_ref)
        else:
            q_ref, k_ref, v_ref, o_ref = refs
        for (r0, rn, kend) in blocks:
            outs = []
            for e in (0, 1):
                bs = (ct_ref, crs[e]) if bias is not None else None
                _, _, p, l = _attn_scores(q_ref, k_ref, e, r0, rn, kend, wide, scale, bs)
                outs.append(_dot(p.astype(BF16), v_ref[0:kend, :]) / l)
            o = jnp.where(_lane(outs[0].shape) < 64, outs[0], outs[1])
            o_ref[r0:r0 + rn, :] = o.astype(BF16)

    args = (q, k, v) + ((bias[0], bias[1], bias[1]) if bias is not None else ())
    (out,), touts = _call_tasks(
        body, tasks, name=name, grid=(B, 4),
        in_specs=specs + bias_specs,
        out_specs=[pl.BlockSpec((LP, 128), lambda b, hp: (b, hp))],
        out_shape=[jax.ShapeDtypeStruct((T, 512), BF16)],
        args=args)
    return out, touts


def _attn_bwd(q, k, v, do, bias, B, LP, wide, scale, name, tasks=()):
    T = q.shape[0]
    tq = 256
    blocks = _attn_blocks(LP, tq)
    qw, specs, bias_specs = _attn_specs(B, LP, wide, bias is not None)
    has_bias = bias is not None

    def body(*refs):
        if has_bias:
            (q_ref, k_ref, v_ref, do_ref, ct_ref, cr0_ref, cr1_ref,
             dq_ref, dk_ref, dv_ref, dc0_ref, dc1_ref) = refs
            crs = (cr0_ref, cr1_ref)
            dcs = (dc0_ref, dc1_ref)
            dc0_ref[...] = jnp.zeros_like(dc0_ref)
            dc1_ref[...] = jnp.zeros_like(dc1_ref)
        else:
            q_ref, k_ref, v_ref, do_ref, dq_ref, dk_ref, dv_ref = refs
        dk_ref[...] = jnp.zeros_like(dk_ref)
        dv_ref[...] = jnp.zeros_like(dv_ref)
        for (r0, rn, kend) in blocks:
            dqs = []
            for e in (0, 1):
                bs = (ct_ref, crs[e]) if has_bias else None
                qe, ke, p, l = _attn_scores(q_ref, k_ref, e, r0, rn, kend, wide, scale, bs)
                pn = p * (1.0 / l)
                dob = do_ref[r0:r0 + rn, :]
                mine = (_lane(dob.shape) < 64) if e == 0 else (_lane(dob.shape) >= 64)
                doe = jnp.where(mine, dob, jnp.zeros_like(dob))
                dp = _dot(doe, v_ref[0:kend, :], NT)
                delta = jnp.sum(pn * dp, axis=-1, keepdims=True)
                ds = pn * (dp - delta)
                dsb = ds.astype(BF16)
                dqe = _dot(dsb, ke) * scale
                dke = _dot(dsb, qe, TN) * scale
                if wide:
                    dq_ref[r0:r0 + rn, 128 * e:128 * (e + 1)] = dqe
                    dk_ref[0:kend, 128 * e:128 * (e + 1)] += dke
                else:
                    dqs.append(dqe)
                    dk_ref[0:kend, :] += dke
                dv_ref[0:kend, :] += _dot(pn.astype(BF16), doe, TN)
                if has_bias:
                    dcs[e][0, :, 0:kend] -= _colsum(ds)
            if not wide:
                dq_ref[r0:r0 + rn, :] = jnp.where(_lane(dqs[0].shape) < 64, dqs[0], dqs[1])

    args = (q, k, v, do) + ((bias[0], bias[1], bias[1]) if has_bias else ())
    out_specs = [pl.BlockSpec((LP, qw), lambda b, hp: (b, hp)),
                 pl.BlockSpec((LP, qw), lambda b, hp: (b, hp)),
                 pl.BlockSpec((LP, 128), lambda b, hp: (b, hp))]
    out_shape = [jax.ShapeDtypeStruct(q.shape, F32), jax.ShapeDtypeStruct(q.shape, F32),
                 jax.ShapeDtypeStruct((T, 512), F32)]
    if has_bias:
        out_specs += [pl.BlockSpec((1, 1, LP), lambda b, hp: (b * 4 + hp, 0, 0))] * 2
        out_shape += [jax.ShapeDtypeStruct((B * 4, 1, LP), F32)] * 2
    return _call_tasks(
        body, tasks, name=name, grid=(B, 4),
        in_specs=specs + [pl.BlockSpec((LP, 128), lambda b, hp: (b, hp))] + bias_specs,
        out_specs=out_specs, out_shape=out_shape, args=args)


def _post_fwd(h, of, om, proj, bg, wbf, wbm, wout, name):
    T = h.shape[0]
    tm = _tile(T, (512, 384, 256, 128))

    def body(h_ref, of_ref, om_ref, gl_ref, bg_ref, wbf_ref, wbm_ref, wo_ref, o_ref, mix_ref):
        gate = jax.nn.sigmoid(gl_ref[...] + bg_ref[...])
        mix = gate[:, :D] * _dot(of_ref[...], wbf_ref[...]) + gate[:, D:] * _dot(om_ref[...], wbm_ref[...])
        mb = mix.astype(BF16)
        mix_ref[...] = mb
        o_ref[...] = h_ref[...] + _dot(mb, wo_ref[...])

    row = lambda w: pl.BlockSpec((tm, w), lambda i: (i, 0))
    full = lambda a: pl.BlockSpec(a.shape, lambda i: (0, 0))
    return _call(
        body, name=name, grid=(T // tm,),
        in_specs=[row(D), row(512), row(512), pl.BlockSpec((tm, 2 * D), lambda i: (i, 1)),
                  full(bg), full(wbf), full(wbm), full(wout)],
        out_specs=[row(D), row(D)],
        out_shape=[jax.ShapeDtypeStruct((T, D), F32), jax.ShapeDtypeStruct((T, D), BF16)],
        compiler_params=_cp(),
    )(h, of, om, proj, bg, wbf, wbm, wout)


def _post_bwd(dh, of, om, proj, bg, wbf, wbm, wout, name):
    T = dh.shape[0]
    tm = _tile(T, (512, 384, 256, 128))

    def body(d_ref, of_ref, om_ref, gl_ref, bg_ref, wbf_ref, wbm_ref, wo_ref,
             dgl_ref, dbf_ref, dbm_ref, dof_ref, dom_ref, dbg_ref):
        @pl.when(pl.program_id(0) == 0)
        def _():
            dbg_ref[...] = jnp.zeros_like(dbg_ref)

        gate = jax.nn.sigmoid(gl_ref[...] + bg_ref[...])
        dmix = _dot(d_ref[...].astype(BF16), wo_ref[...], NT)
        ofx = _dot(of_ref[...], wbf_ref[...])
        omx = _dot(om_ref[...], wbm_ref[...])
        gf = gate[:, :D]
        gm = gate[:, D:]
        dof = (dmix * gf).astype(BF16)
        dom = (dmix * gm).astype(BF16)
        dglf = dmix * ofx * gf * (1.0 - gf)
        dglm = dmix * omx * gm * (1.0 - gm)
        dgl_ref[:, :D] = dglf.astype(BF16)
        dgl_ref[:, D:] = dglm.astype(BF16)
        dbg_ref[:, :D] += _colsum(dglf)
        dbg_ref[:, D:] += _colsum(dglm)
        dbf_ref[...] = dof
        dbm_ref[...] = dom
        dof_ref[...] = _dot(dof, wbf_ref[...], NT).astype(BF16)
        dom_ref[...] = _dot(dom, wbm_ref[...], NT).astype(BF16)

    row = lambda w: pl.BlockSpec((tm, w), lambda i: (i, 0))
    full = lambda a: pl.BlockSpec(a.shape, lambda i: (0, 0))
    return _call(
        body, name=name, grid=(T // tm,),
        in_specs=[row(D), row(512), row(512), pl.BlockSpec((tm, 2 * D), lambda i: (i, 1)),
                  full(bg), full(wbf), full(wbm), full(wout)],
        out_specs=[row(2 * D), row(D), row(D), row(512), row(512), pl.BlockSpec((1, 2 * D), lambda i: (0, 0))],
        out_shape=[jax.ShapeDtypeStruct((T, 2 * D), BF16), jax.ShapeDtypeStruct((T, D), BF16),
                   jax.ShapeDtypeStruct((T, D), BF16), jax.ShapeDtypeStruct((T, 512), BF16),
                   jax.ShapeDtypeStruct((T, 512), BF16), jax.ShapeDtypeStruct((1, 2 * D), F32)],
        compiler_params=_cp(),
    )(dh, of, om, proj, bg, wbf, wbm, wout)


def _loss_head(h3, target, B, LP, name):
    S = LP - MPAD
    nb = LP // 128

    def body(h_ref, t_ref, dy_ref, l_ref):
        b = pl.program_id(0)
        p = pl.program_id(1)

        @pl.when(jnp.logical_and(b == 0, p == 0))
        def _():
            l_ref[...] = jnp.zeros_like(l_ref)

        @pl.when(p == 0)
        def _():
            dy_ref[...] = jnp.zeros_like(dy_ref)

        @pl.when(p > 0)
        def _():
            e = h_ref[...] - t_ref[0]
            dy_ref[...] = e * (1.0 / D)
            l_ref[...] += jnp.sum(e * e, axis=0, keepdims=True) * (0.5 / D)

    return _call(
        body, name=name, grid=(B, nb),
        in_specs=[pl.BlockSpec((128, D), lambda b, p: (b * nb + p, 0)),
                  pl.BlockSpec((1, 128, D), lambda b, p: (b, jnp.maximum(p - 1, 0), 0))],
        out_specs=[pl.BlockSpec((128, D), lambda b, p: (b * nb + p, 0)),
                   pl.BlockSpec((1, D), lambda b, p: (0, 0))],
        out_shape=[jax.ShapeDtypeStruct(h3.shape, F32), jax.ShapeDtypeStruct((1, D), F32)],
        compiler_params=_cp(),
    )(h3, target)


def _rope_tables(B, LP):
    pos = jnp.concatenate([jnp.arange(MPAD, dtype=F32), NMETA + jnp.arange(LP - MPAD, dtype=F32)])
    inv_freq = ROPE_THETA ** (-jnp.arange(0, ROPE, 2, dtype=F32) / ROPE)
    ang = pos[:, None] * inv_freq[None, :]
    cos, sin = jnp.cos(ang), jnp.sin(ang)
    z32 = jnp.zeros((LP, 32), F32)
    rc = jnp.concatenate([jnp.ones((LP, 64), F32), cos, cos, z32], axis=1)
    rs = jnp.concatenate([jnp.zeros((LP, 64), F32), -sin, sin, z32], axis=1)
    return jnp.tile(rc, (B, 1)), jnp.tile(rs, (B, 1))


def _pad_lanes(v, start, width=128):
    n = v.shape[1]
    return jnp.concatenate([jnp.zeros((1, start), F32), v, jnp.zeros((1, width - start - n), F32)], axis=1)


G_FFN1 = ["ffn1_w_gu", "ffn1_w_down"]
G_MIX = ["w_in", "mla_w_uq", "mla_w_ukv", "w_branch_fox", "w_branch_mla", "w_out"]
G_OUT = ["w_out", "w_branch_fox", "w_branch_mla"]
G_IN = ["w_in", "mla_w_uq", "mla_w_ukv"]


def _step(x, target, meta, vec, gath, shards):
    dist = shards is not None
    B, S, _ = x.shape
    LP = MPAD + S
    T = B * LP
    gath = dict(gath)

    def gather(names):
        return [_gather_task([shards[n] for n in names])] if dist else []

    def gathered(names, touts):
        if dist:
            gath.update(zip(names, touts[0]))

    g4, sums, red = {}, {}, {}

    def scatter(names):
        return [_a2a_task([_pieces(g4[n]) for n in names])] if dist else []

    def scattered(names, tout, me):
        for n, r in zip(names, tout):
            sums[n] = _sum_pieces(r, _pieces(g4[n]), me, "rs_sum_" + n)

    def join(names):
        return [_join_task([sums[n] for n in names])] if dist else []

    def joined(names, tout):
        for n, r in zip(names, tout):
            red[n] = (sums[n], r)

    me = None
    if dist:
        me = (4 * lax.axis_index("x") + 2 * lax.axis_index("y") + lax.axis_index("c")).reshape(1).astype(jnp.int32)

    h0 = jnp.concatenate([jnp.broadcast_to(meta[None], (B, NMETA, D)),
                          jnp.zeros((B, MPAD - NMETA, D), F32), x], axis=1).reshape(T, D)
    rc, rs = _rope_tables(B, LP)
    gfq = jnp.tile(vec["fox_q_norm"], (1, 2))
    gfk = jnp.tile(vec["fox_k_norm"], (1, 2))
    gmq = _pad_lanes(vec["mla_q_norm"], 0)
    gmk = _pad_lanes(vec["mla_k_norm"], 0)
    bfv = _pad_lanes(vec["b_forget"], L_FL)

    w1gu, w1d = gath["ffn1_w_gu"], gath["ffn1_w_down"].reshape(2, FH, D)
    h1, touts = _ffn_fwd(h0, vec["ffn1_norm"], w1gu, w1d, "ffn1_fwd", gather(G_MIX))
    gathered(G_MIX, touts)
    wm = _mixer_weights(gath)
    small = (gfq, gfk, vec["mla_cq_norm"], vec["mla_ckv_norm"], gmq, gmk, bfv, wm["wuq"], wm["wuk"], wm["wuv"])
    proj, u2 = _inproj_fwd(h1, vec["mix_norm"], wm["w_in"], "inproj_fwd")
    fq, fk, fv, qm, km, vm, lf = _prep_fwd(proj, rc, rs, *small, name="prep_fwd")
    lf_rows = lf[:, L_FL:L_FL + NH].reshape(B, LP, NH).transpose(0, 2, 1).reshape(B * NH, LP)
    crow = _forget_scan(lf_rows, False, "forget_scan")
    ctok = crow.reshape(B, 4, 2, LP).transpose(0, 1, 3, 2).reshape(B * 4, LP, 2)
    bias = (ctok, crow.reshape(B * NH, 1, LP))
    of, touts = _attn_fwd(fq, fk, fv, bias, B, LP, False, 64 ** -0.5, "fox_fwd", gather(["ffn2_w_gu"]))
    gathered(["ffn2_w_gu"], touts)
    om, touts = _attn_fwd(qm, km, vm, None, B, LP, True, MLA_QK ** -0.5, "mla_fwd", gather(["ffn2_w_down"]))
    gathered(["ffn2_w_down"], touts)
    h2, mix = _post_fwd(h1, of, om, proj, vec["b_gate"], wm["wbf"], wm["wbm"], wm["w_out"], "post_fwd")
    w2gu, w2d = gath["ffn2_w_gu"], gath["ffn2_w_down"].reshape(2, FH, D)
    h3, _ = _ffn_fwd(h2, vec["ffn2_norm"], w2gu, w2d, "ffn2_fwd")
    dy, lpart = _loss_head(h3, target, B, LP, "loss_head")

    gv = {}
    (dh2, u3, a2, dg2, dp2, gv["ffn2_norm"]), _ = _ffn_bwd(h2, dy, vec["ffn2_norm"], w2gu, w2d, "ffn2_bwd")
    g4["ffn2_w_gu"] = jnp.concatenate([_wgrad(u3, dg2, "ffn2_dwg", bn=FH, shard_major=True)[0],
                                       _wgrad(u3, dp2, "ffn2_dwu", bn=FH, shard_major=True)[0]], axis=0)
    g4["ffn2_w_down"] = _wgrad(a2, dy, "ffn2_dwd", scale=0.5, bk=FH)[0].reshape(N_CHIPS, DFF // N_CHIPS, D)

    dgl, dbf, dbm, dof, dom, gv["b_gate"] = _post_bwd(dh2, of, om, proj, vec["b_gate"], wm["wbf"], wm["wbm"], wm["w_out"], "post_bwd")
    g4["w_out"] = _wgrad(mix, dh2, "dw_out")[0].reshape(N_CHIPS, D // N_CHIPS, D)
    g4["w_branch_fox"] = _cols_to_shards(_wgrad(of, dbf, "dw_bf")[0])
    g4["w_branch_mla"] = _cols_to_shards(_wgrad(om, dbm, "dw_bm")[0])
    G_FFN2 = ["ffn2_w_gu", "ffn2_w_down"]
    (dfq, dfk, dfv, dc0, dc1), touts = _attn_bwd(fq, fk, fv, dof, bias, B, LP, False, 64 ** -0.5, "fox_bwd", scatter(G_FFN2))
    if dist:
        scattered(G_FFN2, touts[0], me)
    (dqm, dkm, dvm), touts = _attn_bwd(qm, km, vm, dom, None, B, LP, True, MLA_QK ** -0.5, "mla_bwd",
                                       scatter(G_OUT) + join(G_FFN2))
    if dist:
        scattered(G_OUT, touts[0], me)
        joined(G_FFN2, touts[1])
    dc = jnp.concatenate([dc0, dc1], axis=1).reshape(B * NH, LP)
    dlf_rows = _forget_scan(dc, True, "forget_scan_bwd")
    dlf = dlf_rows.reshape(B, NH, LP).transpose(0, 2, 1).reshape(T, NH)
    dlf = jnp.concatenate([jnp.zeros((T, L_FL), F32), dlf, jnp.zeros((T, 128 - L_FL - NH), F32)], axis=1)
    (dlo, dgfq, dgfk, gv["mla_cq_norm"], gv["mla_ckv_norm"], dgmq, dgmk, dbfv,
     dwuq, dwuk, dwuv) = _prep_bwd(proj, rc, rs, *small, dfq, dfk, dfv, dqm, dkm, dvm, dlf, name="prep_bwd")
    gv["fox_q_norm"] = dgfq[:, :64] + dgfq[:, 64:]
    gv["fox_k_norm"] = dgfk[:, :64] + dgfk[:, 64:]
    gv["mla_q_norm"] = dgmq[:, :MLA_QK]
    gv["mla_k_norm"] = dgmk[:, :MLA_QK]
    gv["b_forget"] = dbfv[:, L_FL:L_FL + NH]
    dwin = jnp.concatenate([_wgrad(u2, dlo, "dw_in_lo")[0], _wgrad(u2, dgl, "dw_in_hi")[0]], axis=1)
    g4["w_in"] = _cols_to_shards(_win_from_kernel(dwin))
    g4["mla_w_uq"] = _cols_to_shards(
        dwuq.astype(GRAD_DTYPE).reshape(QR, NH, 128)[:, :, :MLA_QK].reshape(QR, NH * MLA_QK))
    dukv = jnp.concatenate([dwuk.reshape(KVR, NH, 128)[:, :, :64], dwuv.reshape(KVR, NH, 64)], axis=2)
    g4["mla_w_ukv"] = _cols_to_shards(dukv.astype(GRAD_DTYPE).reshape(KVR, NH * 128))
    dh1, gv["mix_norm"] = _inproj_bwd(h1, dh2, dlo, dgl, vec["mix_norm"], wm["w_in"], "inproj_bwd")

    (dh0, u1, a1, dg1, dp1, gv["ffn1_norm"]), touts = _ffn_bwd(h0, dh1, vec["ffn1_norm"], w1gu, w1d, "ffn1_bwd",
                                                                scatter(G_IN) + join(G_OUT))
    if dist:
        scattered(G_IN, touts[0], me)
        joined(G_OUT, touts[1])
    g4["ffn1_w_gu"] = jnp.concatenate([_wgrad(u1, dg1, "ffn1_dwg", bn=FH, shard_major=True)[0],
                                       _wgrad(u1, dp1, "ffn1_dwu", bn=FH, shard_major=True)[0]], axis=0)
    dwd1, touts = _wgrad(a1, dh1, "ffn1_dwd", scale=0.5, bk=FH, tasks=scatter(["ffn1_w_gu"]) + join(G_IN))
    g4["ffn1_w_down"] = dwd1.reshape(N_CHIPS, DFF // N_CHIPS, D)
    if dist:
        scattered(["ffn1_w_gu"], touts[0], me)
        joined(G_IN, touts[1])
        scattered(["ffn1_w_down"], _run_tasks(scatter(["ffn1_w_down"]), "rs_ffn1_w_down")[0], me)
        joined(G_FFN1, _run_tasks(join(G_FFN1), "rs_join_ffn1")[0])
    dh0 = dh0.reshape(B, LP, D)
    grad_x = dh0[:, MPAD:]
    grad_meta = jnp.sum(dh0[:, :NMETA], axis=0)
    return lpart, grad_x, grad_meta, gv, (red if dist else g4)


def _cols_from_shards(g4):
    n, r, c = g4.shape
    return g4.transpose(1, 0, 2).reshape(r, n * c)


def _cols_to_shards(full):
    r, c4 = full.shape
    return full.reshape(r, N_CHIPS, c4 // N_CHIPS).transpose(1, 0, 2)


def _win_to_kernel(wfull):
    z = lambda n: jnp.zeros((D, n), wfull.dtype)
    fl, cq, ckv, kr, gate = (wfull[:, 1536:1544], wfull[:, 1544:1800], wfull[:, 1800:1928],
                             wfull[:, 1928:1960], wfull[:, 1960:4008])
    misc = jnp.concatenate([z(L_KR), kr, fl, z(128 - L_FL - NH)], axis=1)
    return jnp.concatenate([wfull[:, :1536], cq, ckv, misc, gate], axis=1)


def _win_from_kernel(gk):
    m = C_MISC
    return jnp.concatenate([gk[:, :1536], gk[:, m + L_FL:m + L_FL + NH], gk[:, C_CQ:C_CQ + QR],
                            gk[:, C_CKV:C_CKV + KVR], gk[:, m + L_KR:m + L_KR + ROPE], gk[:, C_GATE:]], axis=1)


def _pieces(g4):
    n, r, c = g4.shape
    return g4.reshape(2 * n, r // 2, c)


def _mixer_weights(gath):
    w = {}
    w["w_in"] = _win_to_kernel(_cols_from_shards(gath["w_in"]))
    uq = _cols_from_shards(gath["mla_w_uq"]).reshape(QR, NH, MLA_QK)
    w["wuq"] = jnp.pad(uq, ((0, 0), (0, 0), (0, 128 - MLA_QK))).reshape(QR, NH * 128)
    ukv = _cols_from_shards(gath["mla_w_ukv"]).reshape(KVR, NH, 128)
    w["wuk"] = jnp.pad(ukv[:, :, :64], ((0, 0), (0, 0), (0, 64))).reshape(KVR, NH * 128)
    w["wuv"] = ukv[:, :, 64:].reshape(KVR, NH * 64)
    w["wbf"] = _cols_from_shards(gath["w_branch_fox"])
    w["wbm"] = _cols_from_shards(gath["w_branch_mla"])
    w["w_out"] = gath["w_out"].reshape(D, D)
    return w


def _chip_peers(x, y):
    return [(1 - x, y), (x, 1 - y), (1 - x, 1 - y)]


RELS = [(dx, dy, dc) for dx in (0, 1) for dy in (0, 1) for dc in (0, 1)][1:]


def _here():
    return lax.axis_index("x"), lax.axis_index("y"), lax.axis_index("c")


def _flip(a, d):
    return (1 - a) if d else a


def _remote(src, dst, send, recv, i, dev):
    return functools.partial(pltpu.make_async_remote_copy, src_ref=src, dst_ref=dst, send_sem=send.at[i],
                             recv_sem=recv.at[i], device_id=dev, device_id_type=MESH)


def _gather_task(shards):
    n = len(shards)

    def descs(ins, outs, sems):
        send, recv, loc = sems
        x, y, c = _here()
        j = 2 * x + y
        locs, pairs = [], []
        for k in range(n):
            locs.append(functools.partial(pltpu.make_async_copy, ins[k], outs[k].at[j], loc.at[k]))
            for r, (px, py) in enumerate(_chip_peers(x, y)):
                dev = (px, py, c)
                pairs.append((_remote(ins[k], outs[k].at[j], send, recv, 3 * k + r, dev),
                              _remote(ins[k], outs[k].at[2 * px + py], send, recv, 3 * k + r, dev)))
        return locs, pairs

    return _Task(shards, [jax.ShapeDtypeStruct((N_CHIPS,) + s.shape, s.dtype) for s in shards],
                 [pltpu.SemaphoreType.DMA((3 * n,)), pltpu.SemaphoreType.DMA((3 * n,)), pltpu.SemaphoreType.DMA((n,))],
                 descs)


class _SplitGather(_Task):
    def __init__(self, shards):
        n = len(shards)
        dma = pltpu.SemaphoreType.DMA
        super().__init__(shards, [jax.ShapeDtypeStruct((N_CHIPS,) + s.shape, s.dtype) for s in shards],
                         [dma((3 * n,)), dma((3 * n,)), dma((3 * n,)), dma((3 * n,)), dma((n,))], None)

    def _plan(self, ins, outs, sems):
        send, recv, fsend, frecv, loc = sems
        x, y, c = _here()
        j = 2 * x + y
        locs, first, passed = [], [], []
        for k in range(len(ins)):
            h = self.ins[k].shape[0] // 2
            mine = pl.ds(pl.multiple_of(c * h, 8), h)
            theirs = pl.ds(pl.multiple_of((1 - c) * h, 8), h)
            locs.append(functools.partial(pltpu.make_async_copy, ins[k], outs[k].at[j], loc.at[k]))
            for r, (px, py) in enumerate(_chip_peers(x, y)):
                i, p = 3 * k + r, 2 * px + py
                first.append((_remote(ins[k].at[mine], outs[k].at[j, mine], send, recv, i, (px, py, c)),
                              _remote(ins[k].at[mine], outs[k].at[p, mine], send, recv, i, (px, py, c))))
                passed.append((_remote(outs[k].at[p, mine], outs[k].at[p, mine], fsend, frecv, i, (x, y, 1 - c)),
                               _remote(outs[k].at[p, mine], outs[k].at[p, theirs], fsend, frecv, i, (x, y, 1 - c))))
        return locs, first, passed

    def start(self, ins, outs, sems):
        locs, first, _ = self._plan(ins, outs, sems)
        for lc in locs:
            lc().start()
        for snd, _ in first:
            snd().start()

    def wait(self, ins, outs, sems):
        locs, first, passed = self._plan(ins, outs, sems)
        for (_, landed), (pass_on, _) in zip(first, passed):
            landed().wait_recv()
            pass_on().start()
        for _, rcv in passed:
            rcv().wait_recv()
        for snd, _ in first + passed:
            snd().wait_send()
        for lc in locs:
            lc().wait()


def _a2a_task(ps):
    n = len(ps)
    nr = len(RELS)

    def descs(ins, outs, sems):
        send, recv = sems
        x, y, c = _here()
        me = 4 * x + 2 * y + c
        pairs = []
        for k in range(n):
            for i, (dx, dy, dc) in enumerate(RELS):
                dev = (_flip(x, dx), _flip(y, dy), _flip(c, dc))
                peer = 4 * dev[0] + 2 * dev[1] + dev[2]
                pairs.append((_remote(ins[k].at[peer], outs[k].at[me], send, recv, nr * k + i, dev),
                              _remote(ins[k].at[peer], outs[k].at[peer], send, recv, nr * k + i, dev)))
        return [], pairs

    return _Task(ps, [jax.ShapeDtypeStruct(p.shape, p.dtype) for p in ps],
                 [pltpu.SemaphoreType.DMA((nr * n,)), pltpu.SemaphoreType.DMA((nr * n,))], descs)


def _join_task(ss):
    n = len(ss)

    def descs(ins, outs, sems):
        send, recv = sems
        x, y, c = _here()
        pairs = []
        for k in range(n):
            cp = _remote(ins[k], outs[k], send, recv, k, (x, y, 1 - c))
            pairs.append((cp, cp))
        return [], pairs

    return _Task(ss, [jax.ShapeDtypeStruct(s.shape, s.dtype) for s in ss],
                 [pltpu.SemaphoreType.DMA((n,)), pltpu.SemaphoreType.DMA((n,))], descs)


def _sum_pieces(recv, own, me, name):
    n, h, c = recv.shape
    tr = _tile(h, (256, 176, 128, 64))

    def body(me_ref, r_ref, o_ref, out_ref):
        s = pl.program_id(1)
        val = jnp.where(s == me_ref[0], o_ref[0], r_ref[0]).astype(F32)

        @pl.when(s == 0)
        def _():
            out_ref[...] = val

        @pl.when(s > 0)
        def _():
            out_ref[...] += val

    def other(s, m):
        return jnp.where(s == m[0], (s + 1) % n, s)

    return _call(
        body, name=name,
        grid_spec=pltpu.PrefetchScalarGridSpec(
            num_scalar_prefetch=1, grid=(h // tr, n),
            in_specs=[pl.BlockSpec((1, tr, c), lambda i, s, m: (other(s, m), i, 0)),
                      pl.BlockSpec((1, tr, c), lambda i, s, m: (m[0], i, 0))],
            out_specs=pl.BlockSpec((tr, c), lambda i, s, m: (i, 0))),
        out_shape=jax.ShapeDtypeStruct((h, c), F32),
        compiler_params=_cp(),
    )(me, recv, own)


def _all_reduce_small(vs, name):
    n = len(vs)
    nr = len(RELS)

    def body(*refs):
        v_refs, o_refs, gat_refs = refs[:n], refs[n:2 * n], refs[2 * n:3 * n]
        send, recv = refs[3 * n:]
        x, y, c = _here()
        me = 4 * x + 2 * y + c
        pairs = []
        for k in range(n):
            gat_refs[k][me] = v_refs[k][...]
            for i, (dx, dy, dc) in enumerate(RELS):
                dev = (_flip(x, dx), _flip(y, dy), _flip(c, dc))
                peer = 4 * dev[0] + 2 * dev[1] + dev[2]
                pairs.append((_remote(v_refs[k], gat_refs[k].at[me], send, recv, nr * k + i, dev),
                              _remote(v_refs[k], gat_refs[k].at[peer], send, recv, nr * k + i, dev)))
        for snd, _ in pairs:
            snd().start()
        for _, rcv in pairs:
            rcv().wait_recv()
        for snd, _ in pairs:
            snd().wait_send()
        for k in range(n):
            acc = gat_refs[k][0]
            for s in range(1, N_DEV):
                acc = acc + gat_refs[k][s]
            o_refs[k][...] = acc

    vm = pl.BlockSpec(memory_space=pltpu.VMEM)
    return _call(
        body, name=name, in_specs=[vm] * n, out_specs=[vm] * n,
        out_shape=[jax.ShapeDtypeStruct(v.shape, F32) for v in vs],
        scratch_shapes=[pltpu.VMEM((N_DEV,) + v.shape, F32) for v in vs]
                       + [pltpu.SemaphoreType.DMA((nr * n,)), pltpu.SemaphoreType.DMA((nr * n,))],
    )(*vs)


def _adamw_update(gg, w, m, v):
    c1 = 1.0 / (1.0 - ADAM_B1 ** ADAM_STEP)
    c2 = 1.0 / (1.0 - ADAM_B2 ** ADAM_STEP)
    nm = ADAM_B1 * m + (1.0 - ADAM_B1) * gg
    nv = ADAM_B2 * v + (1.0 - ADAM_B2) * (gg * gg)
    return -ADAM_LR * ((nm * c1) / (jnp.sqrt(nv * c2) + ADAM_EPS) + ADAM_WD * w), nm, nv


def _adamw_small(gvec, gmeta, chip, ws, ms, vs, name):
    nv_ = len(ws) - 1

    def body(c_ref, gv_ref, gm_ref, *refs):
        na = len(ws)
        w_refs, m_refs, v_refs = refs[:na], refs[na:2 * na], refs[2 * na:3 * na]
        outs = refs[3 * na:]
        g_refs, d_refs, nm_refs, nv_refs = outs[:na], outs[na:2 * na], outs[2 * na:3 * na], outs[3 * na:]
        for k in range(na):
            gg = gv_ref[k:k + 1, 0:ws[k].shape[1]] if k < nv_ else gm_ref[...]
            g_refs[k][...] = gg
            d_refs[k][...], nm_refs[k][...], nv_refs[k][...] = _adamw_update(gg, w_refs[k][...], m_refs[k][...], v_refs[k][...])

    whole = lambda a: pl.BlockSpec(a.shape, lambda i, c: (0, 0))
    arrs = list(ws) + list(ms) + list(vs)
    res = _call(
        body, name=name,
        grid_spec=pltpu.PrefetchScalarGridSpec(
            num_scalar_prefetch=1, grid=(1,),
            in_specs=[whole(gvec), pl.BlockSpec((NMETA, D // N_CHIPS), lambda i, c: (0, c[0]))] + [whole(a) for a in arrs],
            out_specs=[whole(a) for a in ws] * 4),
        out_shape=[jax.ShapeDtypeStruct(a.shape, F32) for a in ws] * 4,
        compiler_params=_cp(),
    )(chip, gvec, gmeta, *arrs)
    na = len(ws)
    return [list(res[i * na:(i + 1) * na]) for i in range(4)]


def _adamw_halves(wt, mine, theirs, m, v, core, name):
    r, c = wt.shape
    h = r // 2
    tr = _tile(h, (256, 176, 128, 64))
    nh = h // tr

    def body(c_ref, w_ref, a_ref, b_ref, m_ref, v_ref, g_ref, d_ref, nm_ref, nv_ref):
        gg = jnp.where(pl.program_id(0) // nh == c_ref[0], a_ref[...], b_ref[...])
        g_ref[...] = gg
        d_ref[...], nm_ref[...], nv_ref[...] = _adamw_update(gg, w_ref[...], m_ref[...], v_ref[...])

    full = pl.BlockSpec((tr, c), lambda i, cr: (i, 0))
    half = pl.BlockSpec((tr, c), lambda i, cr: (i % nh, 0))
    return _call(
        body, name=name,
        grid_spec=pltpu.PrefetchScalarGridSpec(
            num_scalar_prefetch=1, grid=(2 * nh,),
            in_specs=[full, half, half, full, full], out_specs=[full] * 4),
        out_shape=[jax.ShapeDtypeStruct((r, c), F32)] * 4,
        compiler_params=_cp(),
    )(core, wt, mine, theirs, m, v)


MATS = ["ffn1_w_gu", "ffn1_w_down", "w_in", "mla_w_uq", "mla_w_ukv", "w_branch_fox", "w_branch_mla",
        "w_out", "ffn2_w_gu", "ffn2_w_down"]
VECS = ["ffn1_norm", "mix_norm", "b_forget", "b_gate", "fox_q_norm", "fox_k_norm", "mla_cq_norm",
        "mla_ckv_norm", "mla_q_norm", "mla_k_norm", "ffn2_norm"]
WEIGHTS = ["meta_tokens", "ffn1_norm", "ffn1_w_gu", "ffn1_w_down", "mix_norm", "w_in", "b_forget", "b_gate",
           "fox_q_norm", "fox_k_norm", "mla_cq_norm", "mla_w_uq", "mla_ckv_norm", "mla_w_ukv", "mla_q_norm",
           "mla_k_norm", "w_branch_fox", "w_branch_mla", "w_out", "ffn2_norm", "ffn2_w_gu", "ffn2_w_down"]


VEC_LANES = 2048


def _stack_vectors(parts):
    rows = [_pad_lanes(p, 0, VEC_LANES) for p in parts]
    rows.append(jnp.zeros((-len(parts) % 8, VEC_LANES), F32))
    return jnp.concatenate(rows, axis=0)


def kernel(x, meta_tokens, ffn1_norm, ffn1_w_gu, ffn1_w_down, mix_norm, w_in, b_forget, b_gate, fox_q_norm, fox_k_norm, mla_cq_norm, mla_w_uq, mla_ckv_norm, mla_w_ukv, mla_q_norm, mla_k_norm, w_branch_fox, w_branch_mla, w_out, ffn2_norm, ffn2_w_gu, ffn2_w_down, loss_target, m_meta_tokens, m_ffn1_norm, m_ffn1_w_gu, m_ffn1_w_down, m_mix_norm, m_w_in, m_b_forget, m_b_gate, m_fox_q_norm, m_fox_k_norm, m_mla_cq_norm, m_mla_w_uq, m_mla_ckv_norm, m_mla_w_ukv, m_mla_q_norm, m_mla_k_norm, m_w_branch_fox, m_w_branch_mla, m_w_out, m_ffn2_norm, m_ffn2_w_gu, m_ffn2_w_down, v_meta_tokens, v_ffn1_norm, v_ffn1_w_gu, v_ffn1_w_down, v_mix_norm, v_w_in, v_b_forget, v_b_gate, v_fox_q_norm, v_fox_k_norm, v_mla_cq_norm, v_mla_w_uq, v_mla_ckv_norm, v_mla_w_ukv, v_mla_q_norm, v_mla_k_norm, v_w_branch_fox, v_w_branch_mla, v_w_out, v_ffn2_norm, v_ffn2_w_gu, v_ffn2_w_down):
    a = dict(locals())
    wts = {n: a[n] for n in WEIGHTS}
    ms = {n: a["m_" + n] for n in WEIGHTS}
    vs = {n: a["v_" + n] for n in WEIGHTS}
    cx, cy, cc = lax.axis_index("x"), lax.axis_index("y"), lax.axis_index("c")
    chip = 2 * cx + cy

    shards = {n: wts[n][0].astype(BF16) for n in MATS}
    first = _run_tasks([_SplitGather([shards[n] for n in G_FFN1] + [meta_tokens])], "gather_ffn1")[0]
    gath = dict(zip(G_FFN1, first[:-1]))
    meta_full = _cols_from_shards(first[-1])

    lpart, grad_x, gmeta, gv, gred = _step(x, loss_target, meta_full, {n: wts[n] for n in VECS}, gath, shards)
    loss = lax.psum(jnp.sum(lpart), ("x", "y", "c"))

    gvec_red, gmeta_red = _all_reduce_small([_stack_vectors([gv[n] for n in VECS]), gmeta], "allreduce_small")
    sm_names = VECS + ["meta_tokens"]
    sm = _adamw_small(gvec_red, gmeta_red, chip.reshape(1).astype(jnp.int32), [wts[n] for n in sm_names],
                      [ms[n] for n in sm_names], [vs[n] for n in sm_names], "adamw_small")

    grads, delta, new_m, new_v = {}, {}, {}, {}
    core = cc.reshape(1).astype(jnp.int32)
    for n in MATS:
        shp = wts[n].shape
        mine, theirs = gred[n]
        res = _adamw_halves(wts[n][0], mine, theirs, ms[n][0], vs[n][0], core, "adamw_" + n)
        grads[n], delta[n], new_m[n], new_v[n] = (t.reshape(shp) for t in res)
    for k, n in enumerate(sm_names):
        grads[n], delta[n], new_m[n], new_v[n] = (sm[i][k] for i in range(4))

    return (loss, grad_x, *[grads[n] for n in WEIGHTS], *[delta[n] for n in WEIGHTS],
            *[new_m[n] for n in WEIGHTS], *[new_v[n] for n in WEIGHTS])
```

```python
import functools

import jax
import jax.numpy as jnp
from jax import lax
from jax.experimental import pallas as pl
from jax.experimental.pallas import tpu as pltpu

F32 = jnp.float32
BF16 = jnp.bfloat16
MESH = pl.DeviceIdType.MESH

D = 1024
DFF = 2816
FH = DFF // 2
NMETA = 16
MPAD = 128
EPS = 1e-6
NH = 8
FOXW = 512
QR = 256
KVR = 128
ROPE = 32
MLA_QK = 96
PROJW = 4096
ROPE_THETA = 10000.0
N_CHIPS = 4
N_DEV = 8

ADAM_LR = 0.001
ADAM_B1 = 0.9
ADAM_B2 = 0.999
ADAM_EPS = 1e-08
ADAM_WD = 0.01
ADAM_STEP = 10

VMEM_LIMIT = 56 * 2**20
GRAD_DTYPE = BF16

NT = (((1,), (1,)), ((), ()))
TN = (((0,), (0,)), ((), ()))


def _call(body, **kw):
    return pl.pallas_call(body, **kw)


def _cp(**kw):
    return pltpu.CompilerParams(vmem_limit_bytes=VMEM_LIMIT, **kw)


HBM = pl.BlockSpec(memory_space=pltpu.HBM)


class _Task:
    def __init__(self, ins, out_shapes, sems, descs):
        self.ins, self.out_shapes, self.sems, self.descs = list(ins), list(out_shapes), list(sems), descs

    def start(self, ins, outs, sems):
        locs, pairs = self.descs(ins, outs, sems)
        for lc in locs:
            lc().start()
        for snd, _ in pairs:
            snd().start()

    def wait(self, ins, outs, sems):
        locs, pairs = self.descs(ins, outs, sems)
        for _, rcv in pairs:
            rcv().wait_recv()
        for snd, _ in pairs:
            snd().wait_send()
        for lc in locs:
            lc().wait()


def _call_tasks(body, tasks, *, name, grid, in_specs, out_specs, out_shape, args, scratch_shapes=()):
    in_specs, out_specs, out_shape, scratch_shapes = map(list, (in_specs, out_specs, out_shape, scratch_shapes))
    n_in, n_out, n_sc = len(in_specs), len(out_specs), len(scratch_shapes)
    t_in = [len(t.ins) for t in tasks]
    t_out = [len(t.out_shapes) for t in tasks]
    t_sem = [len(t.sems) for t in tasks]

    def wrapped(*refs):
        pos = [0]

        def take(n):
            pos[0] += n
            return refs[pos[0] - n:pos[0]]

        ins, tins = take(n_in), [take(n) for n in t_in]
        outs, touts = take(n_out), [take(n) for n in t_out]
        sc, tsems = take(n_sc), [take(n) for n in t_sem]
        if tasks:
            first = functools.reduce(jnp.logical_and, [pl.program_id(a) == 0 for a in range(len(grid))])
            last = functools.reduce(jnp.logical_and, [pl.program_id(a) == grid[a] - 1 for a in range(len(grid))])

            @pl.when(first)
            def _():
                for t, a, b, s in zip(tasks, tins, touts, tsems):
                    t.start(a, b, s)

        body(*ins, *outs, *sc)
        if tasks:
            @pl.when(last)
            def _():
                for t, a, b, s in zip(tasks, tins, touts, tsems):
                    t.wait(a, b, s)

    res = _call(
        wrapped, name=name, grid=grid,
        in_specs=in_specs + [HBM] * sum(t_in), out_specs=out_specs + [HBM] * sum(t_out),
        out_shape=out_shape + [s for t in tasks for s in t.out_shapes],
        scratch_shapes=scratch_shapes + [s for t in tasks for s in t.sems],
        compiler_params=_cp(),
    )(*args, *[a for t in tasks for a in t.ins])
    res = list(res)
    touts, pos = [], n_out
    for n in t_out:
        touts.append(res[pos:pos + n])
        pos += n
    return res[:n_out], touts


def _run_tasks(tasks, name):
    t_in = [len(t.ins) for t in tasks]
    t_out = [len(t.out_shapes) for t in tasks]
    t_sem = [len(t.sems) for t in tasks]

    def body(*refs):
        pos = [0]

        def take(n):
            pos[0] += n
            return refs[pos[0] - n:pos[0]]

        tins, touts, tsems = [take(n) for n in t_in], [take(n) for n in t_out], [take(n) for n in t_sem]
        for t, a, b, s in zip(tasks, tins, touts, tsems):
            t.start(a, b, s)
        for t, a, b, s in zip(tasks, tins, touts, tsems):
            t.wait(a, b, s)

    res = list(_call(
        body, name=name, in_specs=[HBM] * sum(t_in), out_specs=[HBM] * sum(t_out),
        out_shape=[s for t in tasks for s in t.out_shapes],
        scratch_shapes=[s for t in tasks for s in t.sems],
    )(*[a for t in tasks for a in t.ins]))
    touts, pos = [], 0
    for n in t_out:
        touts.append(res[pos:pos + n])
        pos += n
    return touts


def _tile(n, cands):
    for c in cands:
        if n % c == 0:
            return c
    raise ValueError(f"no tile for {n} among {cands}")


def _dot(a, b, dims=None):
    if dims is None:
        return jnp.dot(a, b, preferred_element_type=F32)
    return lax.dot_general(a, b, dims, preferred_element_type=F32)


def _rms(x, gain, n):
    r = lax.rsqrt(jnp.sum(x * x, axis=-1, keepdims=True) * (1.0 / n) + EPS)
    xh = x * r
    return xh * gain, xh, r


def _rms_bwd(dy, xh, r, gain, n):
    dxh = dy * gain
    return r * (dxh - xh * (jnp.sum(dxh * xh, axis=-1, keepdims=True) * (1.0 / n)))


def _lane(shape):
    return lax.broadcasted_iota(jnp.int32, shape, len(shape) - 1)


def _half_sum(x):
    lo = _lane(x.shape) < 64
    s_lo = jnp.sum(jnp.where(lo, x, 0.0), axis=-1, keepdims=True)
    s_hi = jnp.sum(jnp.where(lo, 0.0, x), axis=-1, keepdims=True)
    return jnp.where(lo, s_lo, s_hi)


def _rope_swap(x):
    ln = _lane(x.shape)
    sw = jnp.where(ln < 80, pltpu.roll(x, 112, 1), pltpu.roll(x, 16, 1))
    return jnp.where(jnp.logical_and(ln >= 64, ln < 96), sw, 0.0)


def _colsum(x):
    return jnp.sum(x, axis=0, keepdims=True)


def _ffn_fwd(h, norm, wgu, wd3, name, tasks=()):
    T = h.shape[0]
    tm = _tile(T, (512, 384, 256, 128))

    def body(h_ref, n_ref, wg_ref, wu_ref, wd_ref, o_ref, u_sc):
        @pl.when(pl.program_id(1) == 0)
        def _():
            x = h_ref[...]
            u, _, _ = _rms(x, n_ref[...], D)
            u_sc[...] = u.astype(BF16)
            o_ref[...] = x

        u = u_sc[...]
        g = _dot(u, wg_ref[0])
        p = _dot(u, wu_ref[0])
        a = (g * jax.nn.sigmoid(g)) * p
        o_ref[...] += 0.5 * _dot(a.astype(BF16), wd_ref[0])

    (out,), touts = _call_tasks(
        body, tasks, name=name, grid=(T // tm, 2),
        in_specs=[pl.BlockSpec((tm, D), lambda i, j: (i, 0)),
                  pl.BlockSpec((1, D), lambda i, j: (0, 0)),
                  pl.BlockSpec((1, D, FH), lambda i, j: (j, 0, 0)),
                  pl.BlockSpec((1, D, FH), lambda i, j: (j + 2, 0, 0)),
                  pl.BlockSpec((1, FH, D), lambda i, j: (j, 0, 0))],
        out_specs=[pl.BlockSpec((tm, D), lambda i, j: (i, 0))],
        out_shape=[jax.ShapeDtypeStruct((T, D), F32)],
        scratch_shapes=[pltpu.VMEM((tm, D), BF16)],
        args=(h, norm, wgu, wgu, wd3))
    return out, touts


def _ffn_bwd(h, dout, norm, wgu, wd3, name, tasks=()):
    T = h.shape[0]
    tm = _tile(T, (512, 256, 128))
    nt = T // tm

    def body(h_ref, d_ref, n_ref, wg_ref, wu_ref, wd_ref, dh_ref, u_ref, a_ref, dgp_ref, dn_ref):
        i = pl.program_id(0)
        j = pl.program_id(1)

        @pl.when(jnp.logical_and(i == 0, j == 0))
        def _():
            dn_ref[...] = jnp.zeros_like(dn_ref)

        @pl.when(j == 0)
        def _():
            u0, _, _ = _rms(h_ref[...], n_ref[...], D)
            u_ref[...] = u0.astype(BF16)

        u = u_ref[...]
        g = _dot(u, wg_ref[0])
        p = _dot(u, wu_ref[0])
        s = jax.nn.sigmoid(g)
        sl = g * s
        dz = (0.5 * d_ref[...]).astype(BF16)
        da = _dot(dz, wd_ref[0], NT)
        dp = da * sl
        dg = (da * p) * (s * (1.0 + g * (1.0 - s)))
        a_ref[...] = (sl * p).astype(BF16)
        dgb = dg.astype(BF16)
        dpb = dp.astype(BF16)
        dgp_ref[:, :FH] = dgb
        dgp_ref[:, FH:] = dpb
        du_half = _dot(dgb, wg_ref[0], NT) + _dot(dpb, wu_ref[0], NT)

        @pl.when(j == 0)
        def _():
            dh_ref[...] = du_half

        @pl.when(j == 1)
        def _():
            du = dh_ref[...] + du_half
            _, xh, r = _rms(h_ref[...], n_ref[...], D)
            dn_ref[...] += _colsum(du * xh)
            dh_ref[...] = d_ref[...] + _rms_bwd(du, xh, r, n_ref[...], D)

    return _call_tasks(
        body, tasks, name=name, grid=(nt, 2),
        in_specs=[pl.BlockSpec((tm, D), lambda i, j: (i, 0)),
                  pl.BlockSpec((tm, D), lambda i, j: (i, 0)),
                  pl.BlockSpec((1, D), lambda i, j: (0, 0)),
                  pl.BlockSpec((1, D, FH), lambda i, j: (j, 0, 0)),
                  pl.BlockSpec((1, D, FH), lambda i, j: (j + 2, 0, 0)),
                  pl.BlockSpec((1, FH, D), lambda i, j: (j, 0, 0))],
        out_specs=[pl.BlockSpec((tm, D), lambda i, j: (i, 0)),
                   pl.BlockSpec((tm, D), lambda i, j: (i, 0)),
                   pl.BlockSpec((tm, FH), lambda i, j: (i, j)),
                   pl.BlockSpec((tm, 2 * FH), lambda i, j: (i, j)),
                   pl.BlockSpec((1, D), lambda i, j: (0, 0))],
        out_shape=[jax.ShapeDtypeStruct((T, D), F32),
                   jax.ShapeDtypeStruct((T, D), BF16),
                   jax.ShapeDtypeStruct((T, DFF), BF16),
                   jax.ShapeDtypeStruct((T, 2 * DFF), BF16),
                   jax.ShapeDtypeStruct((1, D), F32)],
        args=(h, dout, norm, wgu, wgu, wd3))


def _wgrad(x, y, name, scale=1.0, bk=None, bn=None, shard_major=False, slot=lambda j: j, tasks=()):
    T, K = x.shape
    N = y.shape[1]
    bk = bk or K
    bn = bn or N
    bt = _tile(T, (1088, 512, 384, 256, 128))
    nt = T // bt

    def body(x_ref, y_ref, o_ref, acc_ref):
        t = pl.program_id(2)

        @pl.when(t == 0)
        def _():
            acc_ref[...] = jnp.zeros_like(acc_ref)

        acc_ref[...] += _dot(x_ref[...].astype(BF16), y_ref[...].astype(BF16), TN)

        @pl.when(t == nt - 1)
        def _():
            o_ref[...] = (acc_ref[...] * scale).astype(o_ref.dtype).reshape(o_ref.shape)

    if shard_major:
        assert bk == K
        out_spec = pl.BlockSpec((1, K, bn), lambda i, j, t: (slot(j), 0, 0))
        out_shape = jax.ShapeDtypeStruct((N // bn, K, bn), GRAD_DTYPE)
    else:
        out_spec = pl.BlockSpec((bk, bn), lambda i, j, t: (i, j))
        out_shape = jax.ShapeDtypeStruct((K, N), GRAD_DTYPE)
    (out,), touts = _call_tasks(
        body, tasks, name=name, grid=(K // bk, N // bn, nt),
        in_specs=[pl.BlockSpec((bt, bk), lambda i, j, t: (t, i)),
                  pl.BlockSpec((bt, bn), lambda i, j, t: (t, j))],
        out_specs=[out_spec], out_shape=[out_shape],
        scratch_shapes=[pltpu.VMEM((bk, bn), F32)],
        args=(x, y))
    return out, touts


def _inproj_fwd(h, norm, w, name):
    T = h.shape[0]
    tm = _tile(T, (1088, 512, 384, 256, 128))
    tn = 1024

    def body(h_ref, n_ref, w_ref, o_ref, u_ref):
        @pl.when(pl.program_id(1) == 0)
        def _():
            u, _, _ = _rms(h_ref[...], n_ref[...], D)
            u_ref[...] = u.astype(BF16)

        o_ref[...] = _dot(u_ref[...], w_ref[...])

    return _call(
        body, name=name, grid=(T // tm, PROJW // tn),
        in_specs=[pl.BlockSpec((tm, D), lambda i, j: (i, 0)),
                  pl.BlockSpec((1, D), lambda i, j: (0, 0)),
                  pl.BlockSpec((D, tn), lambda i, j: (0, j))],
        out_specs=[pl.BlockSpec((tm, tn), lambda i, j: (i, j)),
                   pl.BlockSpec((tm, D), lambda i, j: (i, 0))],
        out_shape=[jax.ShapeDtypeStruct((T, PROJW), F32), jax.ShapeDtypeStruct((T, D), BF16)],
        compiler_params=_cp(),
    )(h, norm, w)


def _inproj_bwd(h, dres, dlo, dhi, norm, w, name):
    T = h.shape[0]
    tm = _tile(T, (512, 384, 256, 128))
    hw = PROJW // 2

    def body(h_ref, d_ref, lo_ref, hi_ref, n_ref, wlo_ref, whi_ref, dh_ref, dn_ref):
        @pl.when(pl.program_id(0) == 0)
        def _():
            dn_ref[...] = jnp.zeros_like(dn_ref)

        _, xh, r = _rms(h_ref[...], n_ref[...], D)
        du = _dot(lo_ref[...], wlo_ref[...], NT) + _dot(hi_ref[...], whi_ref[...], NT)
        dn_ref[...] += _colsum(du * xh)
        dh_ref[...] = d_ref[...] + _rms_bwd(du, xh, r, n_ref[...], D)

    return _call(
        body, name=name, grid=(T // tm,),
        in_specs=[pl.BlockSpec((tm, D), lambda i: (i, 0)),
                  pl.BlockSpec((tm, D), lambda i: (i, 0)),
                  pl.BlockSpec((tm, hw), lambda i: (i, 0)),
                  pl.BlockSpec((tm, hw), lambda i: (i, 0)),
                  pl.BlockSpec((1, D), lambda i: (0, 0)),
                  pl.BlockSpec((D, hw), lambda i: (0, 0)),
                  pl.BlockSpec((D, hw), lambda i: (0, 1))],
        out_specs=[pl.BlockSpec((tm, D), lambda i: (i, 0)),
                   pl.BlockSpec((1, D), lambda i: (0, 0))],
        out_shape=[jax.ShapeDtypeStruct((T, D), F32), jax.ShapeDtypeStruct((1, D), F32)],
        compiler_params=_cp(),
    )(h, dres, dlo, dhi, norm, w, w)


C_FQ, C_FK, C_FV, C_CQ, C_CKV, C_MISC, C_GATE = 0, 512, 1024, 1536, 1792, 1920, 2048
L_KR, L_FL = 64, 96


def _prep_fwd(proj, rc, rs, gfq, gfk, gcq, gckv, gmq, gmk, bfv, wuq, wuk, wuv, name):
    T = proj.shape[0]
    tm = _tile(T, (256, 128))

    def body(p_ref, rc_ref, rs_ref, gfq_ref, gfk_ref, gcq_ref, gckv_ref, gmq_ref, gmk_ref, bf_ref,
             wuq_ref, wuk_ref, wuv_ref, fq_ref, fk_ref, fv_ref, qm_ref, km_ref, vm_ref, lf_ref):
        for blk in range(4):
            for (c0, g_ref, o_ref) in ((C_FQ, gfq_ref, fq_ref), (C_FK, gfk_ref, fk_ref)):
                x = p_ref[:, c0 + 128 * blk:c0 + 128 * (blk + 1)]
                r = lax.rsqrt(_half_sum(x * x) * (1.0 / 64) + EPS)
                o_ref[:, 128 * blk:128 * (blk + 1)] = (x * r * g_ref[...]).astype(BF16)
        fv_ref[...] = p_ref[:, C_FV:C_FV + 512].astype(BF16)

        rcv = rc_ref[...]
        rsv = rs_ref[...]
        cqn, _, _ = _rms(p_ref[:, C_CQ:C_CQ + QR], gcq_ref[...], QR)
        qpre = _dot(cqn.astype(BF16), wuq_ref[...])
        ckvn, _, _ = _rms(p_ref[:, C_CKV:C_CKV + KVR], gckv_ref[...], KVR)
        ckvb = ckvn.astype(BF16)
        kpre = _dot(ckvb, wuk_ref[...])
        vm_ref[...] = _dot(ckvb, wuv_ref[...]).astype(BF16)
        misc = p_ref[:, C_MISC:C_MISC + 128]
        ln = _lane(misc.shape)
        kr = jnp.where(jnp.logical_and(ln >= L_KR, ln < L_KR + ROPE), misc, 0.0)
        for hh in range(NH):
            sl = slice(128 * hh, 128 * (hh + 1))
            qn, _, _ = _rms(qpre[:, sl], gmq_ref[...], MLA_QK)
            qm_ref[:, sl] = (qn * rcv + _rope_swap(qn) * rsv).astype(BF16)
            kn, _, _ = _rms(kpre[:, sl] + kr, gmk_ref[...], MLA_QK)
            km_ref[:, sl] = (kn * rcv + _rope_swap(kn) * rsv).astype(BF16)
        z = misc + bf_ref[...]
        lf_ref[...] = jnp.minimum(z, 0.0) - jnp.log(1.0 + jnp.exp(-jnp.abs(z)))

    row = lambda w: pl.BlockSpec((tm, w), lambda i: (i, 0))
    full = lambda a: pl.BlockSpec(a.shape, lambda i: (0, 0))
    return _call(
        body, name=name, grid=(T // tm,),
        in_specs=[row(PROJW // 2), row(128), row(128)] + [full(a) for a in (gfq, gfk, gcq, gckv, gmq, gmk, bfv, wuq, wuk, wuv)],
        out_specs=[row(512), row(512), row(512), row(1024), row(1024), row(512), row(128)],
        out_shape=[jax.ShapeDtypeStruct((T, 512), BF16), jax.ShapeDtypeStruct((T, 512), BF16),
                   jax.ShapeDtypeStruct((T, 512), BF16), jax.ShapeDtypeStruct((T, 1024), BF16),
                   jax.ShapeDtypeStruct((T, 1024), BF16), jax.ShapeDtypeStruct((T, 512), BF16),
                   jax.ShapeDtypeStruct((T, 128), F32)],
        compiler_params=_cp(),
    )(proj, rc, rs, gfq, gfk, gcq, gckv, gmq, gmk, bfv, wuq, wuk, wuv)


def _prep_bwd(proj, rc, rs, gfq, gfk, gcq, gckv, gmq, gmk, bfv, wuq, wuk, wuv,
              dfq, dfk, dfv, dqm, dkm, dvm, dlf, name):
    T = proj.shape[0]
    tm = _tile(T, (256, 128))

    def body(p_ref, rc_ref, rs_ref, gfq_ref, gfk_ref, gcq_ref, gckv_ref, gmq_ref, gmk_ref, bf_ref,
             wuq_ref, wuk_ref, wuv_ref, dfq_ref, dfk_ref, dfv_ref, dqm_ref, dkm_ref, dvm_ref, dlf_ref,
             dp_ref, dgfq_ref, dgfk_ref, dgcq_ref, dgckv_ref, dgmq_ref, dgmk_ref, dbf_ref,
             dwuq_ref, dwuk_ref, dwuv_ref, dqpre_sc, dkpre_sc):
        accs = (dgfq_ref, dgfk_ref, dgcq_ref, dgckv_ref, dgmq_ref, dgmk_ref, dbf_ref, dwuq_ref, dwuk_ref, dwuv_ref)

        @pl.when(pl.program_id(0) == 0)
        def _():
            for a in accs:
                a[...] = jnp.zeros_like(a)

        for (c0, g_ref, d_ref, dg_ref) in ((C_FQ, gfq_ref, dfq_ref, dgfq_ref), (C_FK, gfk_ref, dfk_ref, dgfk_ref)):
            dg = jnp.zeros((1, 128), F32)
            for blk in range(4):
                x = p_ref[:, c0 + 128 * blk:c0 + 128 * (blk + 1)]
                r = lax.rsqrt(_half_sum(x * x) * (1.0 / 64) + EPS)
                xh = x * r
                dy = d_ref[:, 128 * blk:128 * (blk + 1)]
                dg = dg + _colsum(dy * xh)
                dxh = dy * g_ref[...]
                dx = r * (dxh - xh * (_half_sum(dxh * xh) * (1.0 / 64)))
                dp_ref[:, c0 + 128 * blk:c0 + 128 * (blk + 1)] = dx.astype(BF16)
            dg_ref[...] += dg
        dp_ref[:, C_FV:C_FV + 512] = dfv_ref[...].astype(BF16)

        rcv = rc_ref[...]
        rsv = rs_ref[...]
        cqn, cqh, cqr = _rms(p_ref[:, C_CQ:C_CQ + QR], gcq_ref[...], QR)
        cqb = cqn.astype(BF16)
        qpre = _dot(cqb, wuq_ref[...])
        dgq = jnp.zeros((1, 128), F32)
        for hh in range(NH):
            sl = slice(128 * hh, 128 * (hh + 1))
            _, xh, r = _rms(qpre[:, sl], gmq_ref[...], MLA_QK)
            dout = dqm_ref[:, sl]
            dqn = dout * rcv + _rope_swap(dout * rsv)
            dgq = dgq + _colsum(dqn * xh)
            dqpre_sc[:, sl] = _rms_bwd(dqn, xh, r, gmq_ref[...], MLA_QK).astype(BF16)
        dgmq_ref[...] += dgq
        dqpre = dqpre_sc[...]
        dwuq_ref[...] += _dot(cqb, dqpre, TN)
        dcqn = _dot(dqpre, wuq_ref[...], NT)
        dgcq_ref[...] += _colsum(dcqn * cqh)
        dp_ref[:, C_CQ:C_CQ + QR] = _rms_bwd(dcqn, cqh, cqr, gcq_ref[...], QR).astype(BF16)

        ckvn, ckvh, ckvr = _rms(p_ref[:, C_CKV:C_CKV + KVR], gckv_ref[...], KVR)
        ckvb = ckvn.astype(BF16)
        kpre = _dot(ckvb, wuk_ref[...])
        misc = p_ref[:, C_MISC:C_MISC + 128]
        ln = _lane(misc.shape)
        is_kr = jnp.logical_and(ln >= L_KR, ln < L_KR + ROPE)
        kr = jnp.where(is_kr, misc, 0.0)
        dgk = jnp.zeros((1, 128), F32)
        dkr = jnp.zeros(misc.shape, F32)
        for hh in range(NH):
            sl = slice(128 * hh, 128 * (hh + 1))
            _, xh, r = _rms(kpre[:, sl] + kr, gmk_ref[...], MLA_QK)
            dout = dkm_ref[:, sl]
            dkn = dout * rcv + _rope_swap(dout * rsv)
            dgk = dgk + _colsum(dkn * xh)
            dkx = _rms_bwd(dkn, xh, r, gmk_ref[...], MLA_QK)
            dkr = dkr + jnp.where(is_kr, dkx, 0.0)
            dkpre_sc[:, sl] = jnp.where(ln < 64, dkx, 0.0).astype(BF16)
        dgmk_ref[...] += dgk
        dkpre = dkpre_sc[...]
        dvmb = dvm_ref[...].astype(BF16)
        dwuk_ref[...] += _dot(ckvb, dkpre, TN)
        dwuv_ref[...] += _dot(ckvb, dvmb, TN)
        dckvn = _dot(dkpre, wuk_ref[...], NT) + _dot(dvmb, wuv_ref[...], NT)
        dgckv_ref[...] += _colsum(dckvn * ckvh)
        dp_ref[:, C_CKV:C_CKV + KVR] = _rms_bwd(dckvn, ckvh, ckvr, gckv_ref[...], KVR).astype(BF16)

        z = misc + bf_ref[...]
        dz = dlf_ref[...] * (1.0 - jax.nn.sigmoid(z))
        dbf_ref[...] += _colsum(dz)
        dp_ref[:, C_MISC:C_MISC + 128] = (dkr + dz).astype(BF16)

    row = lambda w: pl.BlockSpec((tm, w), lambda i: (i, 0))
    full = lambda a: pl.BlockSpec(a.shape, lambda i: (0, 0))
    small = (gfq, gfk, gcq, gckv, gmq, gmk, bfv, wuq, wuk, wuv)
    acc_shapes = [(1, 128), (1, 128), (1, QR), (1, KVR), (1, 128), (1, 128), (1, 128),
                  (QR, 1024), (KVR, 1024), (KVR, 512)]
    return _call(
        body, name=name, grid=(T // tm,),
        in_specs=[row(PROJW // 2), row(128), row(128)] + [full(a) for a in small]
                 + [row(512), row(512), row(512), row(1024), row(1024), row(512), row(128)],
        out_specs=[row(PROJW // 2)] + [pl.BlockSpec(s, lambda i: (0, 0)) for s in acc_shapes],
        out_shape=[jax.ShapeDtypeStruct((T, PROJW // 2), BF16)] + [jax.ShapeDtypeStruct(s, F32) for s in acc_shapes],
        scratch_shapes=[pltpu.VMEM((tm, 1024), BF16), pltpu.VMEM((tm, 1024), BF16)],
        compiler_params=_cp(),
    )(proj, rc, rs, *small, dfq, dfk, dfv, dqm, dkm, dvm, dlf)


def _scan_lanes(x, reverse):
    n = x.shape[-1]
    ln = _lane(x.shape)
    k = 1
    while k < n:
        if reverse:
            x = x + jnp.where(ln < n - k, pltpu.roll(x, n - k, x.ndim - 1), 0.0)
        else:
            x = x + jnp.where(ln >= k, pltpu.roll(x, k, x.ndim - 1), 0.0)
        k *= 2
    return x


def _forget_scan(lf, reverse, name):
    def body(x_ref, o_ref):
        x = x_ref[...]
        ln = _lane(x.shape)
        pad = jnp.logical_and(ln >= NMETA, ln < MPAD)
        o_ref[...] = jnp.where(pad, 0.0, _scan_lanes(jnp.where(pad, 0.0, x), reverse))

    return _call(body, name=name, out_shape=jax.ShapeDtypeStruct(lf.shape, F32), compiler_params=_cp())(lf)


def _attn_blocks(LP, tq):
    return [(0, MPAD, MPAD)] + [(MPAD + i * tq, tq, MPAD + (i + 1) * tq) for i in range((LP - MPAD) // tq)]


def _attn_scores(q_ref, k_ref, e, r0, rn, kend, wide, scale, bias):
    if wide:
        qe = q_ref[r0:r0 + rn, 128 * e:128 * (e + 1)]
        ke = k_ref[0:kend, 128 * e:128 * (e + 1)]
    else:
        qb = q_ref[r0:r0 + rn, :]
        mine = (_lane(qb.shape) < 64) if e == 0 else (_lane(qb.shape) >= 64)
        qe = jnp.where(mine, qb, jnp.zeros_like(qb))
        ke = k_ref[0:kend, :]
    s = _dot(qe, ke, NT) * scale
    if bias is not None:
        ct_ref, cr_ref = bias
        s = s + ct_ref[0, r0:r0 + rn, e:e + 1] - cr_ref[0, :, 0:kend]
    neg = -1e30
    if r0 == 0:
        qi = lax.broadcasted_iota(jnp.int32, (rn, kend), 0)
        ki = lax.broadcasted_iota(jnp.int32, (rn, kend), 1)
        s = jnp.where(jnp.logical_and(ki <= qi, ki < NMETA), s, neg)
    else:
        d0 = kend - rn
        head = jnp.where(_lane((rn, MPAD)) < NMETA, s[:, :MPAD], neg)
        qi = lax.broadcasted_iota(jnp.int32, (rn, rn), 0)
        diag = jnp.where(_lane((rn, rn)) <= qi, s[:, d0:], neg)
        s = jnp.concatenate([head] + ([s[:, MPAD:d0]] if d0 > MPAD else []) + [diag], axis=1)
    m = jnp.max(s, axis=-1, keepdims=True)
    p = jnp.exp(s - m)
    l = jnp.sum(p, axis=-1, keepdims=True)
    return qe, ke, p, l


def _attn_specs(B, LP, wide, has_bias):
    qw = 256 if wide else 128
    specs = [pl.BlockSpec((LP, qw), lambda b, hp: (b, hp)),
             pl.BlockSpec((LP, qw), lambda b, hp: (b, hp)),
             pl.BlockSpec((LP, 128), lambda b, hp: (b, hp))]
    bias_specs = []
    if has_bias:
        bias_specs = [pl.BlockSpec((1, LP, 2), lambda b, hp: (b * 4 + hp, 0, 0)),
                      pl.BlockSpec((1, 1, LP), lambda b, hp: (b * 8 + 2 * hp, 0, 0)),
                      pl.BlockSpec((1, 1, LP), lambda b, hp: (b * 8 + 2 * hp + 1, 0, 0))]
    return qw, specs, bias_specs


def _attn_fwd(q, k, v, bias, B, LP, wide, scale, name, tasks=()):
    T = q.shape[0]
    tq = 256
    blocks = _attn_blocks(LP, tq)
    qw, specs, bias_specs = _attn_specs(B, LP, wide, bias is not None)

    def body(*refs):
        if bias is not None:
            q_ref, k_ref, v_ref, ct_ref, cr0_ref, cr1_ref, o_ref = refs
            crs = (cr0_ref, cr1_ref)
        else:
            q_ref, k_ref, v_ref, o_ref = refs
        for (r0, rn, kend) in blocks:
            outs = []
            for e in (0, 1):
                bs = (ct_ref, crs[e]) if bias is not None else None
                _, _, p, l = _attn_scores(q_ref, k_ref, e, r0, rn, kend, wide, scale, bs)
                outs.append(_dot(p.astype(BF16), v_ref[0:kend, :]) / l)
            o = jnp.where(_lane(outs[0].shape) < 64, outs[0], outs[1])
            o_ref[r0:r0 + rn, :] = o.astype(BF16)

    args = (q, k, v) + ((bias[0], bias[1], bias[1]) if bias is not None else ())
    (out,), touts = _call_tasks(
        body, tasks, name=name, grid=(B, 4),
        in_specs=specs + bias_specs,
        out_specs=[pl.BlockSpec((LP, 128), lambda b, hp: (b, hp))],
        out_shape=[jax.ShapeDtypeStruct((T, 512), BF16)],
        args=args)
    return out, touts


def _attn_bwd(q, k, v, do, bias, B, LP, wide, scale, name, tasks=()):
    T = q.shape[0]
    tq = 256
    blocks = _attn_blocks(LP, tq)
    qw, specs, bias_specs = _attn_specs(B, LP, wide, bias is not None)
    has_bias = bias is not None

    def body(*refs):
        if has_bias:
            (q_ref, k_ref, v_ref, do_ref, ct_ref, cr0_ref, cr1_ref,
             dq_ref, dk_ref, dv_ref, dc0_ref, dc1_ref) = refs
            crs = (cr0_ref, cr1_ref)
            dcs = (dc0_ref, dc1_ref)
            dc0_ref[...] = jnp.zeros_like(dc0_ref)
            dc1_ref[...] = jnp.zeros_like(dc1_ref)
        else:
            q_ref, k_ref, v_ref, do_ref, dq_ref, dk_ref, dv_ref = refs
        dk_ref[...] = jnp.zeros_like(dk_ref)
        dv_ref[...] = jnp.zeros_like(dv_ref)
        for (r0, rn, kend) in blocks:
            dqs = []
            for e in (0, 1):
                bs = (ct_ref, crs[e]) if has_bias else None
                qe, ke, p, l = _attn_scores(q_ref, k_ref, e, r0, rn, kend, wide, scale, bs)
                pn = p * (1.0 / l)
                dob = do_ref[r0:r0 + rn, :]
                mine = (_lane(dob.shape) < 64) if e == 0 else (_lane(dob.shape) >= 64)
                doe = jnp.where(mine, dob, jnp.zeros_like(dob))
                dp = _dot(doe, v_ref[0:kend, :], NT)
                delta = jnp.sum(pn * dp, axis=-1, keepdims=True)
                ds = pn * (dp - delta)
                dsb = ds.astype(BF16)
                dqe = _dot(dsb, ke) * scale
                dke = _dot(dsb, qe, TN) * scale
                if wide:
                    dq_ref[r0:r0 + rn, 128 * e:128 * (e + 1)] = dqe
                    dk_ref[0:kend, 128 * e:128 * (e + 1)] += dke
                else:
                    dqs.append(dqe)
                    dk_ref[0:kend, :] += dke
                dv_ref[0:kend, :] += _dot(pn.astype(BF16), doe, TN)
                if has_bias:
                    dcs[e][0, :, 0:kend] -= _colsum(ds)
            if not wide:
                dq_ref[r0:r0 + rn, :] = jnp.where(_lane(dqs[0].shape) < 64, dqs[0], dqs[1])

    args = (q, k, v, do) + ((bias[0], bias[1], bias[1]) if has_bias else ())
    out_specs = [pl.BlockSpec((LP, qw), lambda b, hp: (b, hp)),
                 pl.BlockSpec((LP, qw), lambda b, hp: (b, hp)),
                 pl.BlockSpec((LP, 128), lambda b, hp: (b, hp))]
    out_shape = [jax.ShapeDtypeStruct(q.shape, F32), jax.ShapeDtypeStruct(q.shape, F32),
                 jax.ShapeDtypeStruct((T, 512), F32)]
    if has_bias:
        out_specs += [pl.BlockSpec((1, 1, LP), lambda b, hp: (b * 4 + hp, 0, 0))] * 2
        out_shape += [jax.ShapeDtypeStruct((B * 4, 1, LP), F32)] * 2
    return _call_tasks(
        body, tasks, name=name, grid=(B, 4),
        in_specs=specs + [pl.BlockSpec((LP, 128), lambda b, hp: (b, hp))] + bias_specs,
        out_specs=out_specs, out_shape=out_shape, args=args)


def _post_fwd(h, of, om, proj, bg, wbf, wbm, wout, name):
    T = h.shape[0]
    tm = _tile(T, (512, 384, 256, 128))

    def body(h_ref, of_ref, om_ref, gl_ref, bg_ref, wbf_ref, wbm_ref, wo_ref, o_ref, mix_ref):
        gate = jax.nn.sigmoid(gl_ref[...] + bg_ref[...])
        mix = gate[:, :D] * _dot(of_ref[...], wbf_ref[...]) + gate[:, D:] * _dot(om_ref[...], wbm_ref[...])
        mb = mix.astype(BF16)
        mix_ref[...] = mb
        o_ref[...] = h_ref[...] + _dot(mb, wo_ref[...])

    row = lambda w: pl.BlockSpec((tm, w), lambda i: (i, 0))
    full = lambda a: pl.BlockSpec(a.shape, lambda i: (0, 0))
    return _call(
        body, name=name, grid=(T // tm,),
        in_specs=[row(D), row(512), row(512), pl.BlockSpec((tm, 2 * D), lambda i: (i, 1)),
                  full(bg), full(wbf), full(wbm), full(wout)],
        out_specs=[row(D), row(D)],
        out_shape=[jax.ShapeDtypeStruct((T, D), F32), jax.ShapeDtypeStruct((T, D), BF16)],
        compiler_params=_cp(),
    )(h, of, om, proj, bg, wbf, wbm, wout)


def _post_bwd(dh, of, om, proj, bg, wbf, wbm, wout, name):
    T = dh.shape[0]
    tm = _tile(T, (512, 384, 256, 128))

    def body(d_ref, of_ref, om_ref, gl_ref, bg_ref, wbf_ref, wbm_ref, wo_ref,
             dgl_ref, dbf_ref, dbm_ref, dof_ref, dom_ref, dbg_ref):
        @pl.when(pl.program_id(0) == 0)
        def _():
            dbg_ref[...] = jnp.zeros_like(dbg_ref)

        gate = jax.nn.sigmoid(gl_ref[...] + bg_ref[...])
        dmix = _dot(d_ref[...].astype(BF16), wo_ref[...], NT)
        ofx = _dot(of_ref[...], wbf_ref[...])
        omx = _dot(om_ref[...], wbm_ref[...])
        gf = gate[:, :D]
        gm = gate[:, D:]
        dof = (dmix * gf).astype(BF16)
        dom = (dmix * gm).astype(BF16)
        dglf = dmix * ofx * gf * (1.0 - gf)
        dglm = dmix * omx * gm * (1.0 - gm)
        dgl_ref[:, :D] = dglf.astype(BF16)
        dgl_ref[:, D:] = dglm.astype(BF16)
        dbg_ref[:, :D] += _colsum(dglf)
        dbg_ref[:, D:] += _colsum(dglm)
        dbf_ref[...] = dof
        dbm_ref[...] = dom
        dof_ref[...] = _dot(dof, wbf_ref[...], NT).astype(BF16)
        dom_ref[...] = _dot(dom, wbm_ref[...], NT).astype(BF16)

    row = lambda w: pl.BlockSpec((tm, w), lambda i: (i, 0))
    full = lambda a: pl.BlockSpec(a.shape, lambda i: (0, 0))
    return _call(
        body, name=name, grid=(T // tm,),
        in_specs=[row(D), row(512), row(512), pl.BlockSpec((tm, 2 * D), lambda i: (i, 1)),
                  full(bg), full(wbf), full(wbm), full(wout)],
        out_specs=[row(2 * D), row(D), row(D), row(512), row(512), pl.BlockSpec((1, 2 * D), lambda i: (0, 0))],
        out_shape=[jax.ShapeDtypeStruct((T, 2 * D), BF16), jax.ShapeDtypeStruct((T, D), BF16),
                   jax.ShapeDtypeStruct((T, D), BF16), jax.ShapeDtypeStruct((T, 512), BF16),
                   jax.ShapeDtypeStruct((T, 512), BF16), jax.ShapeDtypeStruct((1, 2 * D), F32)],
        compiler_params=_cp(),
    )(dh, of, om, proj, bg, wbf, wbm, wout)


def _loss_head(h3, target, B, LP, name):
    S = LP - MPAD
    nb = LP // 128

    def body(h_ref, t_ref, dy_ref, l_ref):
        b = pl.program_id(0)
        p = pl.program_id(1)

        @pl.when(jnp.logical_and(b == 0, p == 0))
        def _():
            l_ref[...] = jnp.zeros_like(l_ref)

        @pl.when(p == 0)
        def _():
            dy_ref[...] = jnp.zeros_like(dy_ref)

        @pl.when(p > 0)
        def _():
            e = h_ref[...] - t_ref[0]
            dy_ref[...] = e * (1.0 / D)
            l_ref[...] += jnp.sum(e * e, axis=0, keepdims=True) * (0.5 / D)

    return _call(
        body, name=name, grid=(B, nb),
        in_specs=[pl.BlockSpec((128, D), lambda b, p: (b * nb + p, 0)),
                  pl.BlockSpec((1, 128, D), lambda b, p: (b, jnp.maximum(p - 1, 0), 0))],
        out_specs=[pl.BlockSpec((128, D), lambda b, p: (b * nb + p, 0)),
                   pl.BlockSpec((1, D), lambda b, p: (0, 0))],
        out_shape=[jax.ShapeDtypeStruct(h3.shape, F32), jax.ShapeDtypeStruct((1, D), F32)],
        compiler_params=_cp(),
    )(h3, target)


def _rope_tables(B, LP):
    pos = jnp.concatenate([jnp.arange(MPAD, dtype=F32), NMETA + jnp.arange(LP - MPAD, dtype=F32)])
    inv_freq = ROPE_THETA ** (-jnp.arange(0, ROPE, 2, dtype=F32) / ROPE)
    ang = pos[:, None] * inv_freq[None, :]
    cos, sin = jnp.cos(ang), jnp.sin(ang)
    z32 = jnp.zeros((LP, 32), F32)
    rc = jnp.concatenate([jnp.ones((LP, 64), F32), cos, cos, z32], axis=1)
    rs = jnp.concatenate([jnp.zeros((LP, 64), F32), -sin, sin, z32], axis=1)
    return jnp.tile(rc, (B, 1)), jnp.tile(rs, (B, 1))


def _pad_lanes(v, start, width=128):
    n = v.shape[1]
    return jnp.concatenate([jnp.zeros((1, start), F32), v, jnp.zeros((1, width - start - n), F32)], axis=1)


G_FFN1 = ["ffn1_w_gu", "ffn1_w_down"]
G_MIX = ["w_in", "mla_w_uq", "mla_w_ukv", "w_branch_fox", "w_branch_mla", "w_out"]
G_OUT = ["w_out", "w_branch_fox", "w_branch_mla"]
G_IN = ["w_in", "mla_w_uq", "mla_w_ukv"]


def _step(x, target, meta, vec, gath, shards):
    dist = shards is not None
    B, S, _ = x.shape
    LP = MPAD + S
    T = B * LP
    gath = dict(gath)

    def gather(names):
        return [_gather_task([shards[n] for n in names])] if dist else []

    def gathered(names, touts):
        if dist:
            gath.update(zip(names, touts[0]))

    g4, sums, red = {}, {}, {}

    def scatter(names):
        return [_a2a_task([_pieces(g4[n]) for n in names])] if dist else []

    def scattered(names, tout, me):
        for n, r in zip(names, tout):
            sums[n] = _sum_pieces(r, _pieces(g4[n]), me, "rs_sum_" + n)

    def join(names):
        return [_join_task([sums[n] for n in names])] if dist else []

    def joined(names, tout):
        for n, r in zip(names, tout):
            red[n] = (sums[n], r)

    me = None
    if dist:
        me = (4 * lax.axis_index("x") + 2 * lax.axis_index("y") + lax.axis_index("c")).reshape(1).astype(jnp.int32)

    h0 = jnp.concatenate([jnp.broadcast_to(meta[None], (B, NMETA, D)),
                          jnp.zeros((B, MPAD - NMETA, D), F32), x], axis=1).reshape(T, D)
    rc, rs = _rope_tables(B, LP)
    gfq = jnp.tile(vec["fox_q_norm"], (1, 2))
    gfk = jnp.tile(vec["fox_k_norm"], (1, 2))
    gmq = _pad_lanes(vec["mla_q_norm"], 0)
    gmk = _pad_lanes(vec["mla_k_norm"], 0)
    bfv = _pad_lanes(vec["b_forget"], L_FL)

    w1gu, w1d = gath["ffn1_w_gu"], gath["ffn1_w_down"].reshape(2, FH, D)
    h1, touts = _ffn_fwd(h0, vec["ffn1_norm"], w1gu, w1d, "ffn1_fwd", gather(G_MIX))
    gathered(G_MIX, touts)
    wm = _mixer_weights(gath)
    small = (gfq, gfk, vec["mla_cq_norm"], vec["mla_ckv_norm"], gmq, gmk, bfv, wm["wuq"], wm["wuk"], wm["wuv"])
    proj, u2 = _inproj_fwd(h1, vec["mix_norm"], wm["w_in"], "inproj_fwd")
    fq, fk, fv, qm, km, vm, lf = _prep_fwd(proj, rc, rs, *small, name="prep_fwd")
    lf_rows = lf[:, L_FL:L_FL + NH].reshape(B, LP, NH).transpose(0, 2, 1).reshape(B * NH, LP)
    crow = _forget_scan(lf_rows, False, "forget_scan")
    ctok = crow.reshape(B, 4, 2, LP).transpose(0, 1, 3, 2).reshape(B * 4, LP, 2)
    bias = (ctok, crow.reshape(B * NH, 1, LP))
    of, touts = _attn_fwd(fq, fk, fv, bias, B, LP, False, 64 ** -0.5, "fox_fwd", gather(["ffn2_w_gu"]))
    gathered(["ffn2_w_gu"], touts)
    om, touts = _attn_fwd(qm, km, vm, None, B, LP, True, MLA_QK ** -0.5, "mla_fwd", gather(["ffn2_w_down"]))
    gathered(["ffn2_w_down"], touts)
    h2, mix = _post_fwd(h1, of, om, proj, vec["b_gate"], wm["wbf"], wm["wbm"], wm["w_out"], "post_fwd")
    w2gu, w2d = gath["ffn2_w_gu"], gath["ffn2_w_down"].reshape(2, FH, D)
    h3, _ = _ffn_fwd(h2, vec["ffn2_norm"], w2gu, w2d, "ffn2_fwd")
    dy, lpart = _loss_head(h3, target, B, LP, "loss_head")

    gv = {}
    (dh2, u3, a2, dgp2, gv["ffn2_norm"]), _ = _ffn_bwd(h2, dy, vec["ffn2_norm"], w2gu, w2d, "ffn2_bwd")
    gu_slot = lambda j: lax.rem(j, 2) * 2 + lax.div(j, 2)
    g4["ffn2_w_gu"] = _wgrad(u3, dgp2, "ffn2_dwgu", bn=FH, shard_major=True, slot=gu_slot)[0]
    g4["ffn2_w_down"] = _wgrad(a2, dy, "ffn2_dwd", scale=0.5, bk=FH)[0].reshape(N_CHIPS, DFF // N_CHIPS, D)

    dgl, dbf, dbm, dof, dom, gv["b_gate"] = _post_bwd(dh2, of, om, proj, vec["b_gate"], wm["wbf"], wm["wbm"], wm["w_out"], "post_bwd")
    g4["w_out"] = _wgrad(mix, dh2, "dw_out")[0].reshape(N_CHIPS, D // N_CHIPS, D)
    g4["w_branch_fox"] = _cols_to_shards(_wgrad(of, dbf, "dw_bf")[0])
    g4["w_branch_mla"] = _cols_to_shards(_wgrad(om, dbm, "dw_bm")[0])
    G_FFN2 = ["ffn2_w_gu", "ffn2_w_down"]
    (dfq, dfk, dfv, dc0, dc1), touts = _attn_bwd(fq, fk, fv, dof, bias, B, LP, False, 64 ** -0.5, "fox_bwd", scatter(G_FFN2))
    if dist:
        scattered(G_FFN2, touts[0], me)
    (dqm, dkm, dvm), touts = _attn_bwd(qm, km, vm, dom, None, B, LP, True, MLA_QK ** -0.5, "mla_bwd",
                                       scatter(G_OUT) + join(G_FFN2))
    if dist:
        scattered(G_OUT, touts[0], me)
        joined(G_FFN2, touts[1])
    dc = jnp.concatenate([dc0, dc1], axis=1).reshape(B * NH, LP)
    dlf_rows = _forget_scan(dc, True, "forget_scan_bwd")
    dlf = dlf_rows.reshape(B, NH, LP).transpose(0, 2, 1).reshape(T, NH)
    dlf = jnp.concatenate([jnp.zeros((T, L_FL), F32), dlf, jnp.zeros((T, 128 - L_FL - NH), F32)], axis=1)
    (dlo, dgfq, dgfk, gv["mla_cq_norm"], gv["mla_ckv_norm"], dgmq, dgmk, dbfv,
     dwuq, dwuk, dwuv) = _prep_bwd(proj, rc, rs, *small, dfq, dfk, dfv, dqm, dkm, dvm, dlf, name="prep_bwd")
    gv["fox_q_norm"] = dgfq[:, :64] + dgfq[:, 64:]
    gv["fox_k_norm"] = dgfk[:, :64] + dgfk[:, 64:]
    gv["mla_q_norm"] = dgmq[:, :MLA_QK]
    gv["mla_k_norm"] = dgmk[:, :MLA_QK]
    gv["b_forget"] = dbfv[:, L_FL:L_FL + NH]
    dwin = jnp.concatenate([_wgrad(u2, dlo, "dw_in_lo")[0], _wgrad(u2, dgl, "dw_in_hi")[0]], axis=1)
    g4["w_in"] = _cols_to_shards(_win_from_kernel(dwin))
    g4["mla_w_uq"] = _cols_to_shards(
        dwuq.astype(GRAD_DTYPE).reshape(QR, NH, 128)[:, :, :MLA_QK].reshape(QR, NH * MLA_QK))
    dukv = jnp.concatenate([dwuk.reshape(KVR, NH, 128)[:, :, :64], dwuv.reshape(KVR, NH, 64)], axis=2)
    g4["mla_w_ukv"] = _cols_to_shards(dukv.astype(GRAD_DTYPE).reshape(KVR, NH * 128))
    dh1, gv["mix_norm"] = _inproj_bwd(h1, dh2, dlo, dgl, vec["mix_norm"], wm["w_in"], "inproj_bwd")

    (dh0, u1, a1, dgp1, gv["ffn1_norm"]), touts = _ffn_bwd(h0, dh1, vec["ffn1_norm"], w1gu, w1d, "ffn1_bwd",
                                                            scatter(G_IN) + join(G_OUT))
    if dist:
        scattered(G_IN, touts[0], me)
        joined(G_OUT, touts[1])
    g4["ffn1_w_gu"] = _wgrad(u1, dgp1, "ffn1_dwgu", bn=FH, shard_major=True, slot=gu_slot)[0]
    dwd1, touts = _wgrad(a1, dh1, "ffn1_dwd", scale=0.5, bk=FH, tasks=scatter(["ffn1_w_gu"]) + join(G_IN))
    g4["ffn1_w_down"] = dwd1.reshape(N_CHIPS, DFF // N_CHIPS, D)
    if dist:
        scattered(["ffn1_w_gu"], touts[0], me)
        joined(G_IN, touts[1])
        scattered(["ffn1_w_down"], _run_tasks(scatter(["ffn1_w_down"]), "rs_ffn1_w_down")[0], me)
        joined(G_FFN1, _run_tasks(join(G_FFN1), "rs_join_ffn1")[0])
    dh0 = dh0.reshape(B, LP, D)
    grad_x = dh0[:, MPAD:]
    grad_meta = jnp.sum(dh0[:, :NMETA], axis=0)
    return lpart, grad_x, grad_meta, gv, (red if dist else g4)


def _cols_from_shards(g4):
    n, r, c = g4.shape
    return g4.transpose(1, 0, 2).reshape(r, n * c)


def _cols_to_shards(full):
    r, c4 = full.shape
    return full.reshape(r, N_CHIPS, c4 // N_CHIPS).transpose(1, 0, 2)


def _win_to_kernel(wfull):
    z = lambda n: jnp.zeros((D, n), wfull.dtype)
    fl, cq, ckv, kr, gate = (wfull[:, 1536:1544], wfull[:, 1544:1800], wfull[:, 1800:1928],
                             wfull[:, 1928:1960], wfull[:, 1960:4008])
    misc = jnp.concatenate([z(L_KR), kr, fl, z(128 - L_FL - NH)], axis=1)
    return jnp.concatenate([wfull[:, :1536], cq, ckv, misc, gate], axis=1)


def _win_from_kernel(gk):
    m = C_MISC
    return jnp.concatenate([gk[:, :1536], gk[:, m + L_FL:m + L_FL + NH], gk[:, C_CQ:C_CQ + QR],
                            gk[:, C_CKV:C_CKV + KVR], gk[:, m + L_KR:m + L_KR + ROPE], gk[:, C_GATE:]], axis=1)


def _pieces(g4):
    n, r, c = g4.shape
    return g4.reshape(2 * n, r // 2, c)


def _mixer_weights(gath):
    w = {}
    w["w_in"] = _win_to_kernel(_cols_from_shards(gath["w_in"]))
    uq = _cols_from_shards(gath["mla_w_uq"]).reshape(QR, NH, MLA_QK)
    w["wuq"] = jnp.pad(uq, ((0, 0), (0, 0), (0, 128 - MLA_QK))).reshape(QR, NH * 128)
    ukv = _cols_from_shards(gath["mla_w_ukv"]).reshape(KVR, NH, 128)
    w["wuk"] = jnp.pad(ukv[:, :, :64], ((0, 0), (0, 0), (0, 64))).reshape(KVR, NH * 128)
    w["wuv"] = ukv[:, :, 64:].reshape(KVR, NH * 64)
    w["wbf"] = _cols_from_shards(gath["w_branch_fox"])
    w["wbm"] = _cols_from_shards(gath["w_branch_mla"])
    w["w_out"] = gath["w_out"].reshape(D, D)
    return w


def _chip_peers(x, y):
    return [(1 - x, y), (x, 1 - y), (1 - x, 1 - y)]


RELS = [(dx, dy, dc) for dx in (0, 1) for dy in (0, 1) for dc in (0, 1)][1:]


def _here():
    return lax.axis_index("x"), lax.axis_index("y"), lax.axis_index("c")


def _flip(a, d):
    return (1 - a) if d else a


def _remote(src, dst, send, recv, i, dev):
    return functools.partial(pltpu.make_async_remote_copy, src_ref=src, dst_ref=dst, send_sem=send.at[i],
                             recv_sem=recv.at[i], device_id=dev, device_id_type=MESH)


def _gather_task(shards):
    n = len(shards)

    def descs(ins, outs, sems):
        send, recv, loc = sems
        x, y, c = _here()
        j = 2 * x + y
        locs, pairs = [], []
        for k in range(n):
            locs.append(functools.partial(pltpu.make_async_copy, ins[k], outs[k].at[j], loc.at[k]))
            for r, (px, py) in enumerate(_chip_peers(x, y)):
                dev = (px, py, c)
                pairs.append((_remote(ins[k], outs[k].at[j], send, recv, 3 * k + r, dev),
                              _remote(ins[k], outs[k].at[2 * px + py], send, recv, 3 * k + r, dev)))
        return locs, pairs

    return _Task(shards, [jax.ShapeDtypeStruct((N_CHIPS,) + s.shape, s.dtype) for s in shards],
                 [pltpu.SemaphoreType.DMA((3 * n,)), pltpu.SemaphoreType.DMA((3 * n,)), pltpu.SemaphoreType.DMA((n,))],
                 descs)


class _SplitGather(_Task):
    def __init__(self, shards):
        n = len(shards)
        dma = pltpu.SemaphoreType.DMA
        super().__init__(shards, [jax.ShapeDtypeStruct((N_CHIPS,) + s.shape, s.dtype) for s in shards],
                         [dma((3 * n,)), dma((3 * n,)), dma((3 * n,)), dma((3 * n,)), dma((n,))], None)

    def _plan(self, ins, outs, sems):
        send, recv, fsend, frecv, loc = sems
        x, y, c = _here()
        j = 2 * x + y
        locs, first, passed = [], [], []
        for k in range(len(ins)):
            h = self.ins[k].shape[0] // 2
            mine = pl.ds(pl.multiple_of(c * h, 8), h)
            theirs = pl.ds(pl.multiple_of((1 - c) * h, 8), h)
            locs.append(functools.partial(pltpu.make_async_copy, ins[k], outs[k].at[j], loc.at[k]))
            for r, (px, py) in enumerate(_chip_peers(x, y)):
                i, p = 3 * k + r, 2 * px + py
                first.append((_remote(ins[k].at[mine], outs[k].at[j, mine], send, recv, i, (px, py, c)),
                              _remote(ins[k].at[mine], outs[k].at[p, mine], send, recv, i, (px, py, c))))
                passed.append((_remote(outs[k].at[p, mine], outs[k].at[p, mine], fsend, frecv, i, (x, y, 1 - c)),
                               _remote(outs[k].at[p, mine], outs[k].at[p, theirs], fsend, frecv, i, (x, y, 1 - c))))
        return locs, first, passed

    def start(self, ins, outs, sems):
        locs, first, _ = self._plan(ins, outs, sems)
        for lc in locs:
            lc().start()
        for snd, _ in first:
            snd().start()

    def wait(self, ins, outs, sems):
        locs, first, passed = self._plan(ins, outs, sems)
        for (_, landed), (pass_on, _) in zip(first, passed):
            landed().wait_recv()
            pass_on().start()
        for _, rcv in passed:
            rcv().wait_recv()
        for snd, _ in first + passed:
            snd().wait_send()
        for lc in locs:
            lc().wait()


def _a2a_task(ps):
    n = len(ps)
    nr = len(RELS)

    def descs(ins, outs, sems):
        send, recv = sems
        x, y, c = _here()
        me = 4 * x + 2 * y + c
        pairs = []
        for k in range(n):
            for i, (dx, dy, dc) in enumerate(RELS):
                dev = (_flip(x, dx), _flip(y, dy), _flip(c, dc))
                peer = 4 * dev[0] + 2 * dev[1] + dev[2]
                pairs.append((_remote(ins[k].at[peer], outs[k].at[me], send, recv, nr * k + i, dev),
                              _remote(ins[k].at[peer], outs[k].at[peer], send, recv, nr * k + i, dev)))
        return [], pairs

    return _Task(ps, [jax.ShapeDtypeStruct(p.shape, p.dtype) for p in ps],
                 [pltpu.SemaphoreType.DMA((nr * n,)), pltpu.SemaphoreType.DMA((nr * n,))], descs)


def _join_task(ss):
    n = len(ss)

    def descs(ins, outs, sems):
        send, recv = sems
        x, y, c = _here()
        pairs = []
        for k in range(n):
            cp = _remote(ins[k], outs[k], send, recv, k, (x, y, 1 - c))
            pairs.append((cp, cp))
        return [], pairs

    return _Task(ss, [jax.ShapeDtypeStruct(s.shape, s.dtype) for s in ss],
                 [pltpu.SemaphoreType.DMA((n,)), pltpu.SemaphoreType.DMA((n,))], descs)


def _sum_pieces(recv, own, me, name):
    n, h, c = recv.shape
    tr = _tile(h, (256, 176, 128, 64))

    def body(me_ref, r_ref, o_ref, out_ref):
        s = pl.program_id(1)
        val = jnp.where(s == me_ref[0], o_ref[0], r_ref[0]).astype(F32)

        @pl.when(s == 0)
        def _():
            out_ref[...] = val

        @pl.when(s > 0)
        def _():
            out_ref[...] += val

    def other(s, m):
        return jnp.where(s == m[0], (s + 1) % n, s)

    return _call(
        body, name=name,
        grid_spec=pltpu.PrefetchScalarGridSpec(
            num_scalar_prefetch=1, grid=(h // tr, n),
            in_specs=[pl.BlockSpec((1, tr, c), lambda i, s, m: (other(s, m), i, 0)),
                      pl.BlockSpec((1, tr, c), lambda i, s, m: (m[0], i, 0))],
            out_specs=pl.BlockSpec((tr, c), lambda i, s, m: (i, 0))),
        out_shape=jax.ShapeDtypeStruct((h, c), F32),
        compiler_params=_cp(),
    )(me, recv, own)


def _all_reduce_small(vs, name):
    n = len(vs)
    nr = len(RELS)

    def body(*refs):
        v_refs, o_refs, gat_refs = refs[:n], refs[n:2 * n], refs[2 * n:3 * n]
        send, recv = refs[3 * n:]
        x, y, c = _here()
        me = 4 * x + 2 * y + c
        pairs = []
        for k in range(n):
            gat_refs[k][me] = v_refs[k][...]
            for i, (dx, dy, dc) in enumerate(RELS):
                dev = (_flip(x, dx), _flip(y, dy), _flip(c, dc))
                peer = 4 * dev[0] + 2 * dev[1] + dev[2]
                pairs.append((_remote(v_refs[k], gat_refs[k].at[me], send, recv, nr * k + i, dev),
                              _remote(v_refs[k], gat_refs[k].at[peer], send, recv, nr * k + i, dev)))
        for snd, _ in pairs:
            snd().start()
        for _, rcv in pairs:
            rcv().wait_recv()
        for snd, _ in pairs:
            snd().wait_send()
        for k in range(n):
            acc = gat_refs[k][0]
            for s in range(1, N_DEV):
                acc = acc + gat_refs[k][s]
            o_refs[k][...] = acc

    vm = pl.BlockSpec(memory_space=pltpu.VMEM)
    return _call(
        body, name=name, in_specs=[vm] * n, out_specs=[vm] * n,
        out_shape=[jax.ShapeDtypeStruct(v.shape, F32) for v in vs],
        scratch_shapes=[pltpu.VMEM((N_DEV,) + v.shape, F32) for v in vs]
                       + [pltpu.SemaphoreType.DMA((nr * n,)), pltpu.SemaphoreType.DMA((nr * n,))],
    )(*vs)


def _adamw_update(gg, w, m, v):
    c1 = 1.0 / (1.0 - ADAM_B1 ** ADAM_STEP)
    c2 = 1.0 / (1.0 - ADAM_B2 ** ADAM_STEP)
    nm = ADAM_B1 * m + (1.0 - ADAM_B1) * gg
    nv = ADAM_B2 * v + (1.0 - ADAM_B2) * (gg * gg)
    return -ADAM_LR * ((nm * c1) / (jnp.sqrt(nv * c2) + ADAM_EPS) + ADAM_WD * w), nm, nv


def _adamw_small(gvec, gmeta, chip, ws, ms, vs, name):
    nv_ = len(ws) - 1

    def body(c_ref, gv_ref, gm_ref, *refs):
        na = len(ws)
        w_refs, m_refs, v_refs = refs[:na], refs[na:2 * na], refs[2 * na:3 * na]
        outs = refs[3 * na:]
        g_refs, d_refs, nm_refs, nv_refs = outs[:na], outs[na:2 * na], outs[2 * na:3 * na], outs[3 * na:]
        for k in range(na):
            gg = gv_ref[k:k + 1, 0:ws[k].shape[1]] if k < nv_ else gm_ref[...]
            g_refs[k][...] = gg
            d_refs[k][...], nm_refs[k][...], nv_refs[k][...] = _adamw_update(gg, w_refs[k][...], m_refs[k][...], v_refs[k][...])

    whole = lambda a: pl.BlockSpec(a.shape, lambda i, c: (0, 0))
    arrs = list(ws) + list(ms) + list(vs)
    res = _call(
        body, name=name,
        grid_spec=pltpu.PrefetchScalarGridSpec(
            num_scalar_prefetch=1, grid=(1,),
            in_specs=[whole(gvec), pl.BlockSpec((NMETA, D // N_CHIPS), lambda i, c: (0, c[0]))] + [whole(a) for a in arrs],
            out_specs=[whole(a) for a in ws] * 4),
        out_shape=[jax.ShapeDtypeStruct(a.shape, F32) for a in ws] * 4,
        compiler_params=_cp(),
    )(chip, gvec, gmeta, *arrs)
    na = len(ws)
    return [list(res[i * na:(i + 1) * na]) for i in range(4)]


def _adamw_halves(wt, mine, theirs, m, v, core, name):
    r, c = wt.shape
    h = r // 2
    tr = _tile(h, (256, 176, 128, 64))
    nh = h // tr

    def body(c_ref, w_ref, a_ref, b_ref, m_ref, v_ref, g_ref, d_ref, nm_ref, nv_ref):
        gg = jnp.where(pl.program_id(0) // nh == c_ref[0], a_ref[...], b_ref[...])
        g_ref[...] = gg
        d_ref[...], nm_ref[...], nv_ref[...] = _adamw_update(gg, w_ref[...], m_ref[...], v_ref[...])

    full = pl.BlockSpec((tr, c), lambda i, cr: (i, 0))
    half = pl.BlockSpec((tr, c), lambda i, cr: (i % nh, 0))
    return _call(
        body, name=name,
        grid_spec=pltpu.PrefetchScalarGridSpec(
            num_scalar_prefetch=1, grid=(2 * nh,),
            in_specs=[full, half, half, full, full], out_specs=[full] * 4),
        out_shape=[jax.ShapeDtypeStruct((r, c), F32)] * 4,
        compiler_params=_cp(),
    )(core, wt, mine, theirs, m, v)


MATS = ["ffn1_w_gu", "ffn1_w_down", "w_in", "mla_w_uq", "mla_w_ukv", "w_branch_fox", "w_branch_mla",
        "w_out", "ffn2_w_gu", "ffn2_w_down"]
VECS = ["ffn1_norm", "mix_norm", "b_forget", "b_gate", "fox_q_norm", "fox_k_norm", "mla_cq_norm",
        "mla_ckv_norm", "mla_q_norm", "mla_k_norm", "ffn2_norm"]
WEIGHTS = ["meta_tokens", "ffn1_norm", "ffn1_w_gu", "ffn1_w_down", "mix_norm", "w_in", "b_forget", "b_gate",
           "fox_q_norm", "fox_k_norm", "mla_cq_norm", "mla_w_uq", "mla_ckv_norm", "mla_w_ukv", "mla_q_norm",
           "mla_k_norm", "w_branch_fox", "w_branch_mla", "w_out", "ffn2_norm", "ffn2_w_gu", "ffn2_w_down"]


VEC_LANES = 2048


def _stack_vectors(parts):
    rows = [_pad_lanes(p, 0, VEC_LANES) for p in parts]
    rows.append(jnp.zeros((-len(parts) % 8, VEC_LANES), F32))
    return jnp.concatenate(rows, axis=0)


def kernel(x, meta_tokens, ffn1_norm, ffn1_w_gu, ffn1_w_down, mix_norm, w_in, b_forget, b_gate, fox_q_norm, fox_k_norm, mla_cq_norm, mla_w_uq, mla_ckv_norm, mla_w_ukv, mla_q_norm, mla_k_norm, w_branch_fox, w_branch_mla, w_out, ffn2_norm, ffn2_w_gu, ffn2_w_down, loss_target, m_meta_tokens, m_ffn1_norm, m_ffn1_w_gu, m_ffn1_w_down, m_mix_norm, m_w_in, m_b_forget, m_b_gate, m_fox_q_norm, m_fox_k_norm, m_mla_cq_norm, m_mla_w_uq, m_mla_ckv_norm, m_mla_w_ukv, m_mla_q_norm, m_mla_k_norm, m_w_branch_fox, m_w_branch_mla, m_w_out, m_ffn2_norm, m_ffn2_w_gu, m_ffn2_w_down, v_meta_tokens, v_ffn1_norm, v_ffn1_w_gu, v_ffn1_w_down, v_mix_norm, v_w_in, v_b_forget, v_b_gate, v_fox_q_norm, v_fox_k_norm, v_mla_cq_norm, v_mla_w_uq, v_mla_ckv_norm, v_mla_w_ukv, v_mla_q_norm, v_mla_k_norm, v_w_branch_fox, v_w_branch_mla, v_w_out, v_ffn2_norm, v_ffn2_w_gu, v_ffn2_w_down):
    a = dict(locals())
    wts = {n: a[n] for n in WEIGHTS}
    ms = {n: a["m_" + n] for n in WEIGHTS}
    vs = {n: a["v_" + n] for n in WEIGHTS}
    cx, cy, cc = lax.axis_index("x"), lax.axis_index("y"), lax.axis_index("c")
    chip = 2 * cx + cy

    shards = {n: wts[n][0].astype(BF16) for n in MATS}
    first = _run_tasks([_SplitGather([shards[n] for n in G_FFN1] + [meta_tokens])], "gather_ffn1")[0]
    gath = dict(zip(G_FFN1, first[:-1]))
    meta_full = _cols_from_shards(first[-1])

    lpart, grad_x, gmeta, gv, gred = _step(x, loss_target, meta_full, {n: wts[n] for n in VECS}, gath, shards)
    loss = lax.psum(jnp.sum(lpart), ("x", "y", "c"))

    gvec_red, gmeta_red = _all_reduce_small([_stack_vectors([gv[n] for n in VECS]), gmeta], "allreduce_small")
    sm_names = VECS + ["meta_tokens"]
    sm = _adamw_small(gvec_red, gmeta_red, chip.reshape(1).astype(jnp.int32), [wts[n] for n in sm_names],
                      [ms[n] for n in sm_names], [vs[n] for n in sm_names], "adamw_small")

    grads, delta, new_m, new_v = {}, {}, {}, {}
    core = cc.reshape(1).astype(jnp.int32)
    for n in MATS:
        shp = wts[n].shape
        mine, theirs = gred[n]
        res = _adamw_halves(wts[n][0], mine, theirs, ms[n][0], vs[n][0], core, "adamw_" + n)
        grads[n], delta[n], new_m[n], new_v[n] = (t.reshape(shp) for t in res)
    for k, n in enumerate(sm_names):
        grads[n], delta[n], new_m[n], new_v[n] = (sm[i][k] for i in range(4))

    return (loss, grad_x, *[grads[n] for n in WEIGHTS], *[delta[n] for n in WEIGHTS],
            *[new_m[n] for n in WEIGHTS], *[new_v[n] for n in WEIGHTS])
```

```python
import functools

import jax
import jax.numpy as jnp
from jax import lax
from jax.experimental import pallas as pl
from jax.experimental.pallas import tpu as pltpu

F32 = jnp.float32
BF16 = jnp.bfloat16
MESH = pl.DeviceIdType.MESH

D = 1024
DFF = 2816
FH = DFF // 2
NMETA = 16
MPAD = 128
EPS = 1e-6
NH = 8
FOXW = 512
QR = 256
KVR = 128
ROPE = 32
MLA_QK = 96
PROJW = 4096
ROPE_THETA = 10000.0
N_CHIPS = 4
N_DEV = 8

ADAM_LR = 0.001
ADAM_B1 = 0.9
ADAM_B2 = 0.999
ADAM_EPS = 1e-08
ADAM_WD = 0.01
ADAM_STEP = 10

VMEM_LIMIT = 56 * 2**20
GRAD_DTYPE = BF16

NT = (((1,), (1,)), ((), ()))
TN = (((0,), (0,)), ((), ()))


def _call(body, **kw):
    return pl.pallas_call(body, **kw)


def _cp(**kw):
    return pltpu.CompilerParams(vmem_limit_bytes=VMEM_LIMIT, **kw)


HBM = pl.BlockSpec(memory_space=pltpu.HBM)


class _Task:
    def __init__(self, ins, out_shapes, sems, descs):
        self.ins, self.out_shapes, self.sems, self.descs = list(ins), list(out_shapes), list(sems), descs

    def start(self, ins, outs, sems):
        locs, pairs = self.descs(ins, outs, sems)
        for lc in locs:
            lc().start()
        for snd, _ in pairs:
            snd().start()

    def wait(self, ins, outs, sems):
        locs, pairs = self.descs(ins, outs, sems)
        for _, rcv in pairs:
            rcv().wait_recv()
        for snd, _ in pairs:
            snd().wait_send()
        for lc in locs:
            lc().wait()


def _call_tasks(body, tasks, *, name, grid, in_specs, out_specs, out_shape, args, scratch_shapes=()):
    in_specs, out_specs, out_shape, scratch_shapes = map(list, (in_specs, out_specs, out_shape, scratch_shapes))
    n_in, n_out, n_sc = len(in_specs), len(out_specs), len(scratch_shapes)
    t_in = [len(t.ins) for t in tasks]
    t_out = [len(t.out_shapes) for t in tasks]
    t_sem = [len(t.sems) for t in tasks]

    def wrapped(*refs):
        pos = [0]

        def take(n):
            pos[0] += n
            return refs[pos[0] - n:pos[0]]

        ins, tins = take(n_in), [take(n) for n in t_in]
        outs, touts = take(n_out), [take(n) for n in t_out]
        sc, tsems = take(n_sc), [take(n) for n in t_sem]
        if tasks:
            first = functools.reduce(jnp.logical_and, [pl.program_id(a) == 0 for a in range(len(grid))])
            last = functools.reduce(jnp.logical_and, [pl.program_id(a) == grid[a] - 1 for a in range(len(grid))])

            @pl.when(first)
            def _():
                for t, a, b, s in zip(tasks, tins, touts, tsems):
                    t.start(a, b, s)

        body(*ins, *outs, *sc)
        if tasks:
            @pl.when(last)
            def _():
                for t, a, b, s in zip(tasks, tins, touts, tsems):
                    t.wait(a, b, s)

    res = _call(
        wrapped, name=name, grid=grid,
        in_specs=in_specs + [HBM] * sum(t_in), out_specs=out_specs + [HBM] * sum(t_out),
        out_shape=out_shape + [s for t in tasks for s in t.out_shapes],
        scratch_shapes=scratch_shapes + [s for t in tasks for s in t.sems],
        compiler_params=_cp(),
    )(*args, *[a for t in tasks for a in t.ins])
    res = list(res)
    touts, pos = [], n_out
    for n in t_out:
        touts.append(res[pos:pos + n])
        pos += n
    return res[:n_out], touts


def _run_tasks(tasks, name):
    t_in = [len(t.ins) for t in tasks]
    t_out = [len(t.out_shapes) for t in tasks]
    t_sem = [len(t.sems) for t in tasks]

    def body(*refs):
        pos = [0]

        def take(n):
            pos[0] += n
            return refs[pos[0] - n:pos[0]]

        tins, touts, tsems = [take(n) for n in t_in], [take(n) for n in t_out], [take(n) for n in t_sem]
        for t, a, b, s in zip(tasks, tins, touts, tsems):
            t.start(a, b, s)
        for t, a, b, s in zip(tasks, tins, touts, tsems):
            t.wait(a, b, s)

    res = list(_call(
        body, name=name, in_specs=[HBM] * sum(t_in), out_specs=[HBM] * sum(t_out),
        out_shape=[s for t in tasks for s in t.out_shapes],
        scratch_shapes=[s for t in tasks for s in t.sems],
    )(*[a for t in tasks for a in t.ins]))
    touts, pos = [], 0
    for n in t_out:
        touts.append(res[pos:pos + n])
        pos += n
    return touts


def _tile(n, cands):
    for c in cands:
        if n % c == 0:
            return c
    raise ValueError(f"no tile for {n} among {cands}")


def _dot(a, b, dims=None):
    if dims is None:
        return jnp.dot(a, b, preferred_element_type=F32)
    return lax.dot_general(a, b, dims, preferred_element_type=F32)


def _rms(x, gain, n):
    r = lax.rsqrt(jnp.sum(x * x, axis=-1, keepdims=True) * (1.0 / n) + EPS)
    xh = x * r
    return xh * gain, xh, r


def _rms_bwd(dy, xh, r, gain, n):
    dxh = dy * gain
    return r * (dxh - xh * (jnp.sum(dxh * xh, axis=-1, keepdims=True) * (1.0 / n)))


def _lane(shape):
    return lax.broadcasted_iota(jnp.int32, shape, len(shape) - 1)


def _half_sum(x):
    lo = _lane(x.shape) < 64
    s_lo = jnp.sum(jnp.where(lo, x, 0.0), axis=-1, keepdims=True)
    s_hi = jnp.sum(jnp.where(lo, 0.0, x), axis=-1, keepdims=True)
    return jnp.where(lo, s_lo, s_hi)


def _rope_swap(x):
    ln = _lane(x.shape)
    sw = jnp.where(ln < 80, pltpu.roll(x, 112, 1), pltpu.roll(x, 16, 1))
    return jnp.where(jnp.logical_and(ln >= 64, ln < 96), sw, 0.0)


def _colsum(x):
    return jnp.sum(x, axis=0, keepdims=True)


def _ffn_fwd(h, norm, wgu, wd3, name, tasks=()):
    T = h.shape[0]
    tm = _tile(T, (512, 384, 256, 128))

    def body(h_ref, n_ref, wg_ref, wu_ref, wd_ref, o_ref, u_sc):
        @pl.when(pl.program_id(1) == 0)
        def _():
            x = h_ref[...]
            u, _, _ = _rms(x, n_ref[...], D)
            u_sc[...] = u.astype(BF16)
            o_ref[...] = x

        u = u_sc[...]
        g = _dot(u, wg_ref[0])
        p = _dot(u, wu_ref[0])
        a = (g * jax.nn.sigmoid(g)) * p
        o_ref[...] += 0.5 * _dot(a.astype(BF16), wd_ref[0])

    (out,), touts = _call_tasks(
        body, tasks, name=name, grid=(T // tm, 2),
        in_specs=[pl.BlockSpec((tm, D), lambda i, j: (i, 0)),
                  pl.BlockSpec((1, D), lambda i, j: (0, 0)),
                  pl.BlockSpec((1, D, FH), lambda i, j: (j, 0, 0)),
                  pl.BlockSpec((1, D, FH), lambda i, j: (j + 2, 0, 0)),
                  pl.BlockSpec((1, FH, D), lambda i, j: (j, 0, 0))],
        out_specs=[pl.BlockSpec((tm, D), lambda i, j: (i, 0))],
        out_shape=[jax.ShapeDtypeStruct((T, D), F32)],
        scratch_shapes=[pltpu.VMEM((tm, D), BF16)],
        args=(h, norm, wgu, wgu, wd3))
    return out, touts


def _ffn_bwd(h, dout, norm, wgu, wd3, name, tasks=()):
    T = h.shape[0]
    tm = _tile(T, (512, 256, 128))
    nt = T // tm

    def body(h_ref, d_ref, n_ref, wg_ref, wu_ref, wd_ref, dh_ref, u_ref, a_ref, dgp_ref, dn_ref):
        i = pl.program_id(0)
        j = pl.program_id(1)

        @pl.when(jnp.logical_and(i == 0, j == 0))
        def _():
            dn_ref[...] = jnp.zeros_like(dn_ref)

        @pl.when(j == 0)
        def _():
            u0, _, _ = _rms(h_ref[...], n_ref[...], D)
            u_ref[...] = u0.astype(BF16)

        u = u_ref[...]
        g = _dot(u, wg_ref[0])
        p = _dot(u, wu_ref[0])
        s = jax.nn.sigmoid(g)
        sl = g * s
        dz = (0.5 * d_ref[...]).astype(BF16)
        da = _dot(dz, wd_ref[0], NT)
        dp = da * sl
        dg = (da * p) * (s * (1.0 + g * (1.0 - s)))
        a_ref[...] = (sl * p).astype(BF16)
        dgb = dg.astype(BF16)
        dpb = dp.astype(BF16)
        dgp_ref[:, :FH] = dgb
        dgp_ref[:, FH:] = dpb
        du_half = _dot(dgb, wg_ref[0], NT) + _dot(dpb, wu_ref[0], NT)

        @pl.when(j == 0)
        def _():
            dh_ref[...] = du_half

        @pl.when(j == 1)
        def _():
            du = dh_ref[...] + du_half
            _, xh, r = _rms(h_ref[...], n_ref[...], D)
            dn_ref[...] += _colsum(du * xh)
            dh_ref[...] = d_ref[...] + _rms_bwd(du, xh, r, n_ref[...], D)

    return _call_tasks(
        body, tasks, name=name, grid=(nt, 2),
        in_specs=[pl.BlockSpec((tm, D), lambda i, j: (i, 0)),
                  pl.BlockSpec((tm, D), lambda i, j: (i, 0)),
                  pl.BlockSpec((1, D), lambda i, j: (0, 0)),
                  pl.BlockSpec((1, D, FH), lambda i, j: (j, 0, 0)),
                  pl.BlockSpec((1, D, FH), lambda i, j: (j + 2, 0, 0)),
                  pl.BlockSpec((1, FH, D), lambda i, j: (j, 0, 0))],
        out_specs=[pl.BlockSpec((tm, D), lambda i, j: (i, 0)),
                   pl.BlockSpec((tm, D), lambda i, j: (i, 0)),
                   pl.BlockSpec((tm, FH), lambda i, j: (i, j)),
                   pl.BlockSpec((tm, 2 * FH), lambda i, j: (i, j)),
                   pl.BlockSpec((1, D), lambda i, j: (0, 0))],
        out_shape=[jax.ShapeDtypeStruct((T, D), F32),
                   jax.ShapeDtypeStruct((T, D), BF16),
                   jax.ShapeDtypeStruct((T, DFF), BF16),
                   jax.ShapeDtypeStruct((T, 2 * DFF), BF16),
                   jax.ShapeDtypeStruct((1, D), F32)],
        args=(h, dout, norm, wgu, wgu, wd3))


def _wgrad(x, y, name, scale=1.0, bk=None, bn=None, shard_major=False, slot=lambda j: j, tasks=()):
    T, K = x.shape
    N = y.shape[1]
    bk = bk or K
    bn = bn or N
    bt = _tile(T, (1088, 512, 384, 256, 128))
    nt = T // bt

    def body(x_ref, y_ref, o_ref, acc_ref):
        t = pl.program_id(2)

        @pl.when(t == 0)
        def _():
            acc_ref[...] = jnp.zeros_like(acc_ref)

        acc_ref[...] += _dot(x_ref[...].astype(BF16), y_ref[...].astype(BF16), TN)

        @pl.when(t == nt - 1)
        def _():
            o_ref[...] = (acc_ref[...] * scale).astype(o_ref.dtype).reshape(o_ref.shape)

    if shard_major:
        assert bk == K
        out_spec = pl.BlockSpec((1, K, bn), lambda i, j, t: (slot(j), 0, 0))
        out_shape = jax.ShapeDtypeStruct((N // bn, K, bn), GRAD_DTYPE)
    else:
        out_spec = pl.BlockSpec((bk, bn), lambda i, j, t: (i, j))
        out_shape = jax.ShapeDtypeStruct((K, N), GRAD_DTYPE)
    (out,), touts = _call_tasks(
        body, tasks, name=name, grid=(K // bk, N // bn, nt),
        in_specs=[pl.BlockSpec((bt, bk), lambda i, j, t: (t, i)),
                  pl.BlockSpec((bt, bn), lambda i, j, t: (t, j))],
        out_specs=[out_spec], out_shape=[out_shape],
        scratch_shapes=[pltpu.VMEM((bk, bn), F32)],
        args=(x, y))
    return out, touts


def _inproj_fwd(h, norm, w, name):
    T = h.shape[0]
    tm = _tile(T, (1088, 512, 384, 256, 128))
    tn = 1024

    def body(h_ref, n_ref, w_ref, o_ref, u_ref):
        @pl.when(pl.program_id(1) == 0)
        def _():
            u, _, _ = _rms(h_ref[...], n_ref[...], D)
            u_ref[...] = u.astype(BF16)

        o_ref[...] = _dot(u_ref[...], w_ref[...])

    return _call(
        body, name=name, grid=(T // tm, PROJW // tn),
        in_specs=[pl.BlockSpec((tm, D), lambda i, j: (i, 0)),
                  pl.BlockSpec((1, D), lambda i, j: (0, 0)),
                  pl.BlockSpec((D, tn), lambda i, j: (0, j))],
        out_specs=[pl.BlockSpec((tm, tn), lambda i, j: (i, j)),
                   pl.BlockSpec((tm, D), lambda i, j: (i, 0))],
        out_shape=[jax.ShapeDtypeStruct((T, PROJW), F32), jax.ShapeDtypeStruct((T, D), BF16)],
        compiler_params=_cp(),
    )(h, norm, w)


def _inproj_bwd(h, dres, dlo, dhi, norm, w, name):
    T = h.shape[0]
    tm = _tile(T, (512, 384, 256, 128))
    hw = PROJW // 2

    def body(h_ref, d_ref, lo_ref, hi_ref, n_ref, wlo_ref, whi_ref, dh_ref, dn_ref):
        @pl.when(pl.program_id(0) == 0)
        def _():
            dn_ref[...] = jnp.zeros_like(dn_ref)

        _, xh, r = _rms(h_ref[...], n_ref[...], D)
        du = _dot(lo_ref[...], wlo_ref[...], NT) + _dot(hi_ref[...], whi_ref[...], NT)
        dn_ref[...] += _colsum(du * xh)
        dh_ref[...] = d_ref[...] + _rms_bwd(du, xh, r, n_ref[...], D)

    return _call(
        body, name=name, grid=(T // tm,),
        in_specs=[pl.BlockSpec((tm, D), lambda i: (i, 0)),
                  pl.BlockSpec((tm, D), lambda i: (i, 0)),
                  pl.BlockSpec((tm, hw), lambda i: (i, 0)),
                  pl.BlockSpec((tm, hw), lambda i: (i, 0)),
                  pl.BlockSpec((1, D), lambda i: (0, 0)),
                  pl.BlockSpec((D, hw), lambda i: (0, 0)),
                  pl.BlockSpec((D, hw), lambda i: (0, 1))],
        out_specs=[pl.BlockSpec((tm, D), lambda i: (i, 0)),
                   pl.BlockSpec((1, D), lambda i: (0, 0))],
        out_shape=[jax.ShapeDtypeStruct((T, D), F32), jax.ShapeDtypeStruct((1, D), F32)],
        compiler_params=_cp(),
    )(h, dres, dlo, dhi, norm, w, w)


C_FQ, C_FK, C_FV, C_CQ, C_CKV, C_MISC, C_GATE = 0, 512, 1024, 1536, 1792, 1920, 2048
L_KR, L_FL = 64, 96


def _prep_fwd(proj, rc, rs, gfq, gfk, gcq, gckv, gmq, gmk, bfv, wuq, wuk, wuv, name):
    T = proj.shape[0]
    tm = _tile(T, (256, 128))

    def body(p_ref, rc_ref, rs_ref, gfq_ref, gfk_ref, gcq_ref, gckv_ref, gmq_ref, gmk_ref, bf_ref,
             wuq_ref, wuk_ref, wuv_ref, fq_ref, fk_ref, fv_ref, qm_ref, km_ref, vm_ref, lf_ref):
        for blk in range(4):
            for (c0, g_ref, o_ref) in ((C_FQ, gfq_ref, fq_ref), (C_FK, gfk_ref, fk_ref)):
                x = p_ref[:, c0 + 128 * blk:c0 + 128 * (blk + 1)]
                r = lax.rsqrt(_half_sum(x * x) * (1.0 / 64) + EPS)
                o_ref[:, 128 * blk:128 * (blk + 1)] = (x * r * g_ref[...]).astype(BF16)
        fv_ref[...] = p_ref[:, C_FV:C_FV + 512].astype(BF16)

        rcv = rc_ref[...]
        rsv = rs_ref[...]
        cqn, _, _ = _rms(p_ref[:, C_CQ:C_CQ + QR], gcq_ref[...], QR)
        qpre = _dot(cqn.astype(BF16), wuq_ref[...])
        ckvn, _, _ = _rms(p_ref[:, C_CKV:C_CKV + KVR], gckv_ref[...], KVR)
        ckvb = ckvn.astype(BF16)
        kpre = _dot(ckvb, wuk_ref[...])
        vm_ref[...] = _dot(ckvb, wuv_ref[...]).astype(BF16)
        misc = p_ref[:, C_MISC:C_MISC + 128]
        ln = _lane(misc.shape)
        kr = jnp.where(jnp.logical_and(ln >= L_KR, ln < L_KR + ROPE), misc, 0.0)
        for hh in range(NH):
            sl = slice(128 * hh, 128 * (hh + 1))
            qn, _, _ = _rms(qpre[:, sl], gmq_ref[...], MLA_QK)
            qm_ref[:, sl] = (qn * rcv + _rope_swap(qn) * rsv).astype(BF16)
            kn, _, _ = _rms(kpre[:, sl] + kr, gmk_ref[...], MLA_QK)
            km_ref[:, sl] = (kn * rcv + _rope_swap(kn) * rsv).astype(BF16)
        z = misc + bf_ref[...]
        lf_ref[...] = jnp.minimum(z, 0.0) - jnp.log(1.0 + jnp.exp(-jnp.abs(z)))

    row = lambda w: pl.BlockSpec((tm, w), lambda i: (i, 0))
    full = lambda a: pl.BlockSpec(a.shape, lambda i: (0, 0))
    return _call(
        body, name=name, grid=(T // tm,),
        in_specs=[row(PROJW // 2), row(128), row(128)] + [full(a) for a in (gfq, gfk, gcq, gckv, gmq, gmk, bfv, wuq, wuk, wuv)],
        out_specs=[row(512), row(512), row(512), row(1024), row(1024), row(512), row(128)],
        out_shape=[jax.ShapeDtypeStruct((T, 512), BF16), jax.ShapeDtypeStruct((T, 512), BF16),
                   jax.ShapeDtypeStruct((T, 512), BF16), jax.ShapeDtypeStruct((T, 1024), BF16),
                   jax.ShapeDtypeStruct((T, 1024), BF16), jax.ShapeDtypeStruct((T, 512), BF16),
                   jax.ShapeDtypeStruct((T, 128), F32)],
        compiler_params=_cp(),
    )(proj, rc, rs, gfq, gfk, gcq, gckv, gmq, gmk, bfv, wuq, wuk, wuv)


def _prep_bwd(proj, rc, rs, gfq, gfk, gcq, gckv, gmq, gmk, bfv, wuq, wuk, wuv,
              dfq, dfk, dfv, dqm, dkm, dvm, dlf, name):
    T = proj.shape[0]
    tm = _tile(T, (256, 128))

    def body(p_ref, rc_ref, rs_ref, gfq_ref, gfk_ref, gcq_ref, gckv_ref, gmq_ref, gmk_ref, bf_ref,
             wuq_ref, wuk_ref, wuv_ref, dfq_ref, dfk_ref, dfv_ref, dqm_ref, dkm_ref, dvm_ref, dlf_ref,
             dp_ref, dgfq_ref, dgfk_ref, dgcq_ref, dgckv_ref, dgmq_ref, dgmk_ref, dbf_ref,
             dwuq_ref, dwuk_ref, dwuv_ref, dqpre_sc, dkpre_sc):
        accs = (dgfq_ref, dgfk_ref, dgcq_ref, dgckv_ref, dgmq_ref, dgmk_ref, dbf_ref, dwuq_ref, dwuk_ref, dwuv_ref)

        @pl.when(pl.program_id(0) == 0)
        def _():
            for a in accs:
                a[...] = jnp.zeros_like(a)

        for (c0, g_ref, d_ref, dg_ref) in ((C_FQ, gfq_ref, dfq_ref, dgfq_ref), (C_FK, gfk_ref, dfk_ref, dgfk_ref)):
            dg = jnp.zeros((1, 128), F32)
            for blk in range(4):
                x = p_ref[:, c0 + 128 * blk:c0 + 128 * (blk + 1)]
                r = lax.rsqrt(_half_sum(x * x) * (1.0 / 64) + EPS)
                xh = x * r
                dy = d_ref[:, 128 * blk:128 * (blk + 1)]
                dg = dg + _colsum(dy * xh)
                dxh = dy * g_ref[...]
                dx = r * (dxh - xh * (_half_sum(dxh * xh) * (1.0 / 64)))
                dp_ref[:, c0 + 128 * blk:c0 + 128 * (blk + 1)] = dx.astype(BF16)
            dg_ref[...] += dg
        dp_ref[:, C_FV:C_FV + 512] = dfv_ref[...].astype(BF16)

        rcv = rc_ref[...]
        rsv = rs_ref[...]
        cqn, cqh, cqr = _rms(p_ref[:, C_CQ:C_CQ + QR], gcq_ref[...], QR)
        cqb = cqn.astype(BF16)
        qpre = _dot(cqb, wuq_ref[...])
        dgq = jnp.zeros((1, 128), F32)
        for hh in range(NH):
            sl = slice(128 * hh, 128 * (hh + 1))
            _, xh, r = _rms(qpre[:, sl], gmq_ref[...], MLA_QK)
            dout = dqm_ref[:, sl]
            dqn = dout * rcv + _rope_swap(dout * rsv)
            dgq = dgq + _colsum(dqn * xh)
            dqpre_sc[:, sl] = _rms_bwd(dqn, xh, r, gmq_ref[...], MLA_QK).astype(BF16)
        dgmq_ref[...] += dgq
        dqpre = dqpre_sc[...]
        dwuq_ref[...] += _dot(cqb, dqpre, TN)
        dcqn = _dot(dqpre, wuq_ref[...], NT)
        dgcq_ref[...] += _colsum(dcqn * cqh)
        dp_ref[:, C_CQ:C_CQ + QR] = _rms_bwd(dcqn, cqh, cqr, gcq_ref[...], QR).astype(BF16)

        ckvn, ckvh, ckvr = _rms(p_ref[:, C_CKV:C_CKV + KVR], gckv_ref[...], KVR)
        ckvb = ckvn.astype(BF16)
        kpre = _dot(ckvb, wuk_ref[...])
        misc = p_ref[:, C_MISC:C_MISC + 128]
        ln = _lane(misc.shape)
        is_kr = jnp.logical_and(ln >= L_KR, ln < L_KR + ROPE)
        kr = jnp.where(is_kr, misc, 0.0)
        dgk = jnp.zeros((1, 128), F32)
        dkr = jnp.zeros(misc.shape, F32)
        for hh in range(NH):
            sl = slice(128 * hh, 128 * (hh + 1))
            _, xh, r = _rms(kpre[:, sl] + kr, gmk_ref[...], MLA_QK)
            dout = dkm_ref[:, sl]
            dkn = dout * rcv + _rope_swap(dout * rsv)
            dgk = dgk + _colsum(dkn * xh)
            dkx = _rms_bwd(dkn, xh, r, gmk_ref[...], MLA_QK)
            dkr = dkr + jnp.where(is_kr, dkx, 0.0)
            dkpre_sc[:, sl] = jnp.where(ln < 64, dkx, 0.0).astype(BF16)
        dgmk_ref[...] += dgk
        dkpre = dkpre_sc[...]
        dvmb = dvm_ref[...].astype(BF16)
        dwuk_ref[...] += _dot(ckvb, dkpre, TN)
        dwuv_ref[...] += _dot(ckvb, dvmb, TN)
        dckvn = _dot(dkpre, wuk_ref[...], NT) + _dot(dvmb, wuv_ref[...], NT)
        dgckv_ref[...] += _colsum(dckvn * ckvh)
        dp_ref[:, C_CKV:C_CKV + KVR] = _rms_bwd(dckvn, ckvh, ckvr, gckv_ref[...], KVR).astype(BF16)

        z = misc + bf_ref[...]
        dz = dlf_ref[...] * (1.0 - jax.nn.sigmoid(z))
        dbf_ref[...] += _colsum(dz)
        dp_ref[:, C_MISC:C_MISC + 128] = (dkr + dz).astype(BF16)

    row = lambda w: pl.BlockSpec((tm, w), lambda i: (i, 0))
    full = lambda a: pl.BlockSpec(a.shape, lambda i: (0, 0))
    small = (gfq, gfk, gcq, gckv, gmq, gmk, bfv, wuq, wuk, wuv)
    acc_shapes = [(1, 128), (1, 128), (1, QR), (1, KVR), (1, 128), (1, 128), (1, 128),
                  (QR, 1024), (KVR, 1024), (KVR, 512)]
    return _call(
        body, name=name, grid=(T // tm,),
        in_specs=[row(PROJW // 2), row(128), row(128)] + [full(a) for a in small]
                 + [row(512), row(512), row(512), row(1024), row(1024), row(512), row(128)],
        out_specs=[row(PROJW // 2)] + [pl.BlockSpec(s, lambda i: (0, 0)) for s in acc_shapes],
        out_shape=[jax.ShapeDtypeStruct((T, PROJW // 2), BF16)] + [jax.ShapeDtypeStruct(s, F32) for s in acc_shapes],
        scratch_shapes=[pltpu.VMEM((tm, 1024), BF16), pltpu.VMEM((tm, 1024), BF16)],
        compiler_params=_cp(),
    )(proj, rc, rs, *small, dfq, dfk, dfv, dqm, dkm, dvm, dlf)


def _scan_lanes(x, reverse):
    n = x.shape[-1]
    ln = _lane(x.shape)
    k = 1
    while k < n:
        if reverse:
            x = x + jnp.where(ln < n - k, pltpu.roll(x, n - k, x.ndim - 1), 0.0)
        else:
            x = x + jnp.where(ln >= k, pltpu.roll(x, k, x.ndim - 1), 0.0)
        k *= 2
    return x


def _forget_scan(lf, reverse, name):
    def body(x_ref, o_ref):
        x = x_ref[...]
        ln = _lane(x.shape)
        pad = jnp.logical_and(ln >= NMETA, ln < MPAD)
        o_ref[...] = jnp.where(pad, 0.0, _scan_lanes(jnp.where(pad, 0.0, x), reverse))

    return _call(body, name=name, out_shape=jax.ShapeDtypeStruct(lf.shape, F32), compiler_params=_cp())(lf)


def _attn_blocks(LP, tq):
    return [(0, MPAD, MPAD)] + [(MPAD + i * tq, tq, MPAD + (i + 1) * tq) for i in range((LP - MPAD) // tq)]


def _attn_scores(q_ref, k_ref, e, r0, rn, kend, wide, scale, bias):
    if wide:
        qe = q_ref[r0:r0 + rn, 128 * e:128 * (e + 1)]
        ke = k_ref[0:kend, 128 * e:128 * (e + 1)]
    else:
        qb = q_ref[r0:r0 + rn, :]
        mine = (_lane(qb.shape) < 64) if e == 0 else (_lane(qb.shape) >= 64)
        qe = jnp.where(mine, qb, jnp.zeros_like(qb))
        ke = k_ref[0:kend, :]
    s = _dot(qe, ke, NT) * scale
    if bias is not None:
        ct_ref, cr_ref = bias
        s = s + ct_ref[0, r0:r0 + rn, e:e + 1] - cr_ref[0, :, 0:kend]
    neg = -1e30
    if r0 == 0:
        qi = lax.broadcasted_iota(jnp.int32, (rn, kend), 0)
        ki = lax.broadcasted_iota(jnp.int32, (rn, kend), 1)
        s = jnp.where(jnp.logical_and(ki <= qi, ki < NMETA), s, neg)
    else:
        d0 = kend - rn
        head = jnp.where(_lane((rn, MPAD)) < NMETA, s[:, :MPAD], neg)
        qi = lax.broadcasted_iota(jnp.int32, (rn, rn), 0)
        diag = jnp.where(_lane((rn, rn)) <= qi, s[:, d0:], neg)
        s = jnp.concatenate([head] + ([s[:, MPAD:d0]] if d0 > MPAD else []) + [diag], axis=1)
    m = jnp.max(s, axis=-1, keepdims=True)
    p = jnp.exp(s - m)
    l = jnp.sum(p, axis=-1, keepdims=True)
    return qe, ke, p, l


def _attn_specs(B, LP, wide, has_bias):
    qw = 256 if wide else 128
    specs = [pl.BlockSpec((LP, qw), lambda b, hp: (b, hp)),
             pl.BlockSpec((LP, qw), lambda b, hp: (b, hp)),
             pl.BlockSpec((LP, 128), lambda b, hp: (b, hp))]
    bias_specs = []
    if has_bias:
        bias_specs = [pl.BlockSpec((1, LP, 2), lambda b, hp: (b * 4 + hp, 0, 0)),
                      pl.BlockSpec((1, 1, LP), lambda b, hp: (b * 8 + 2 * hp, 0, 0)),
                      pl.BlockSpec((1, 1, LP), lambda b, hp: (b * 8 + 2 * hp + 1, 0, 0))]
    return qw, specs, bias_specs


def _attn_fwd(q, k, v, bias, B, LP, wide, scale, name, tasks=()):
    T = q.shape[0]
    tq = 256
    blocks = _attn_blocks(LP, tq)
    qw, specs, bias_specs = _attn_specs(B, LP, wide, bias is not None)

    def body(*refs):
        if bias is not None:
            q_ref, k_ref, v_ref, ct_ref, cr0_ref, cr1_ref, o_ref = refs
            crs = (cr0_ref, cr1_ref)
        else:
            q_ref, k_ref, v_ref, o_ref = refs
        for (r0, rn, kend) in blocks:
            outs = []
            for e in (0, 1):
                bs = (ct_ref, crs[e]) if bias is not None else None
                _, _, p, l = _attn_scores(q_ref, k_ref, e, r0, rn, kend, wide, scale, bs)
                outs.append(_dot(p.astype(BF16), v_ref[0:kend, :]) / l)
            o = jnp.where(_lane(outs[0].shape) < 64, outs[0], outs[1])
            o_ref[r0:r0 + rn, :] = o.astype(BF16)

    args = (q, k, v) + ((bias[0], bias[1], bias[1]) if bias is not None else ())
    (out,), touts = _call_tasks(
        body, tasks, name=name, grid=(B, 4),
        in_specs=specs + bias_specs,
        out_specs=[pl.BlockSpec((LP, 128), lambda b, hp: (b, hp))],
        out_shape=[jax.ShapeDtypeStruct((T, 512), BF16)],
        args=args)
    return out, touts


def _attn_bwd(q, k, v, do, bias, B, LP, wide, scale, name, tasks=()):
    T = q.shape[0]
    tq = 256
    blocks = _attn_blocks(LP, tq)
    qw, specs, bias_specs = _attn_specs(B, LP, wide, bias is not None)
    has_bias = bias is not None

    def body(*refs):
        if has_bias:
            (q_ref, k_ref, v_ref, do_ref, ct_ref, cr0_ref, cr1_ref,
             dq_ref, dk_ref, dv_ref, dc0_ref, dc1_ref) = refs
            crs = (cr0_ref, cr1_ref)
            dcs = (dc0_ref, dc1_ref)
            dc0_ref[...] = jnp.zeros_like(dc0_ref)
            dc1_ref[...] = jnp.zeros_like(dc1_ref)
        else:
            q_ref, k_ref, v_ref, do_ref, dq_ref, dk_ref, dv_ref = refs
        dk_ref[...] = jnp.zeros_like(dk_ref)
        dv_ref[...] = jnp.zeros_like(dv_ref)
        for (r0, rn, kend) in blocks:
            dqs = []
            for e in (0, 1):
                bs = (ct_ref, crs[e]) if has_bias else None
                qe, ke, p, l = _attn_scores(q_ref, k_ref, e, r0, rn, kend, wide, scale, bs)
                pn = p * (1.0 / l)
                dob = do_ref[r0:r0 + rn, :]
                mine = (_lane(dob.shape) < 64) if e == 0 else (_lane(dob.shape) >= 64)
                doe = jnp.where(mine, dob, jnp.zeros_like(dob))
                dp = _dot(doe, v_ref[0:kend, :], NT)
                delta = jnp.sum(pn * dp, axis=-1, keepdims=True)
                ds = pn * (dp - delta)
                dsb = ds.astype(BF16)
                dqe = _dot(dsb, ke) * scale
                dke = _dot(dsb, qe, TN) * scale
                if wide:
                    dq_ref[r0:r0 + rn, 128 * e:128 * (e + 1)] = dqe
                    dk_ref[0:kend, 128 * e:128 * (e + 1)] += dke
                else:
                    dqs.append(dqe)
                    dk_ref[0:kend, :] += dke
                dv_ref[0:kend, :] += _dot(pn.astype(BF16), doe, TN)
                if has_bias:
                    dcs[e][0, :, 0:kend] -= _colsum(ds)
            if not wide:
                dq_ref[r0:r0 + rn, :] = jnp.where(_lane(dqs[0].shape) < 64, dqs[0], dqs[1])

    args = (q, k, v, do) + ((bias[0], bias[1], bias[1]) if has_bias else ())
    out_specs = [pl.BlockSpec((LP, qw), lambda b, hp: (b, hp)),
                 pl.BlockSpec((LP, qw), lambda b, hp: (b, hp)),
                 pl.BlockSpec((LP, 128), lambda b, hp: (b, hp))]
    out_shape = [jax.ShapeDtypeStruct(q.shape, F32), jax.ShapeDtypeStruct(q.shape, F32),
                 jax.ShapeDtypeStruct((T, 512), F32)]
    if has_bias:
        out_specs += [pl.BlockSpec((1, 1, LP), lambda b, hp: (b * 4 + hp, 0, 0))] * 2
        out_shape += [jax.ShapeDtypeStruct((B * 4, 1, LP), F32)] * 2
    return _call_tasks(
        body, tasks, name=name, grid=(B, 4),
        in_specs=specs + [pl.BlockSpec((LP, 128), lambda b, hp: (b, hp))] + bias_specs,
        out_specs=out_specs, out_shape=out_shape, args=args)


def _post_fwd(h, of, om, proj, bg, wbf, wbm, wout, name):
    T = h.shape[0]
    tm = _tile(T, (512, 384, 256, 128))

    def body(h_ref, of_ref, om_ref, gl_ref, bg_ref, wbf_ref, wbm_ref, wo_ref, o_ref, mix_ref):
        gate = jax.nn.sigmoid(gl_ref[...] + bg_ref[...])
        mix = gate[:, :D] * _dot(of_ref[...], wbf_ref[...]) + gate[:, D:] * _dot(om_ref[...], wbm_ref[...])
        mb = mix.astype(BF16)
        mix_ref[...] = mb
        o_ref[...] = h_ref[...] + _dot(mb, wo_ref[...])

    row = lambda w: pl.BlockSpec((tm, w), lambda i: (i, 0))
    full = lambda a: pl.BlockSpec(a.shape, lambda i: (0, 0))
    return _call(
        body, name=name, grid=(T // tm,),
        in_specs=[row(D), row(512), row(512), pl.BlockSpec((tm, 2 * D), lambda i: (i, 1)),
                  full(bg), full(wbf), full(wbm), full(wout)],
        out_specs=[row(D), row(D)],
        out_shape=[jax.ShapeDtypeStruct((T, D), F32), jax.ShapeDtypeStruct((T, D), BF16)],
        compiler_params=_cp(),
    )(h, of, om, proj, bg, wbf, wbm, wout)


def _post_bwd(dh, of, om, proj, bg, wbf, wbm, wout, name):
    T = dh.shape[0]
    tm = _tile(T, (512, 384, 256, 128))

    def body(d_ref, of_ref, om_ref, gl_ref, bg_ref, wbf_ref, wbm_ref, wo_ref,
             dgl_ref, dbf_ref, dbm_ref, dof_ref, dom_ref, dbg_ref):
        @pl.when(pl.program_id(0) == 0)
        def _():
            dbg_ref[...] = jnp.zeros_like(dbg_ref)

        gate = jax.nn.sigmoid(gl_ref[...] + bg_ref[...])
        dmix = _dot(d_ref[...].astype(BF16), wo_ref[...], NT)
        ofx = _dot(of_ref[...], wbf_ref[...])
        omx = _dot(om_ref[...], wbm_ref[...])
        gf = gate[:, :D]
        gm = gate[:, D:]
        dof = (dmix * gf).astype(BF16)
        dom = (dmix * gm).astype(BF16)
        dglf = dmix * ofx * gf * (1.0 - gf)
        dglm = dmix * omx * gm * (1.0 - gm)
        dgl_ref[:, :D] = dglf.astype(BF16)
        dgl_ref[:, D:] = dglm.astype(BF16)
        dbg_ref[:, :D] += _colsum(dglf)
        dbg_ref[:, D:] += _colsum(dglm)
        dbf_ref[...] = dof
        dbm_ref[...] = dom
        dof_ref[...] = _dot(dof, wbf_ref[...], NT).astype(BF16)
        dom_ref[...] = _dot(dom, wbm_ref[...], NT).astype(BF16)

    row = lambda w: pl.BlockSpec((tm, w), lambda i: (i, 0))
    full = lambda a: pl.BlockSpec(a.shape, lambda i: (0, 0))
    return _call(
        body, name=name, grid=(T // tm,),
        in_specs=[row(D), row(512), row(512), pl.BlockSpec((tm, 2 * D), lambda i: (i, 1)),
                  full(bg), full(wbf), full(wbm), full(wout)],
        out_specs=[row(2 * D), row(D), row(D), row(512), row(512), pl.BlockSpec((1, 2 * D), lambda i: (0, 0))],
        out_shape=[jax.ShapeDtypeStruct((T, 2 * D), BF16), jax.ShapeDtypeStruct((T, D), BF16),
                   jax.ShapeDtypeStruct((T, D), BF16), jax.ShapeDtypeStruct((T, 512), BF16),
                   jax.ShapeDtypeStruct((T, 512), BF16), jax.ShapeDtypeStruct((1, 2 * D), F32)],
        compiler_params=_cp(),
    )(dh, of, om, proj, bg, wbf, wbm, wout)


def _loss_head(h3, target, B, LP, name):
    S = LP - MPAD
    nb = LP // 128

    def body(h_ref, t_ref, dy_ref, l_ref):
        b = pl.program_id(0)
        p = pl.program_id(1)

        @pl.when(jnp.logical_and(b == 0, p == 0))
        def _():
            l_ref[...] = jnp.zeros_like(l_ref)

        @pl.when(p == 0)
        def _():
            dy_ref[...] = jnp.zeros_like(dy_ref)

        @pl.when(p > 0)
        def _():
            e = h_ref[...] - t_ref[0]
            dy_ref[...] = e * (1.0 / D)
            l_ref[...] += jnp.sum(e * e, axis=0, keepdims=True) * (0.5 / D)

    return _call(
        body, name=name, grid=(B, nb),
        in_specs=[pl.BlockSpec((128, D), lambda b, p: (b * nb + p, 0)),
                  pl.BlockSpec((1, 128, D), lambda b, p: (b, jnp.maximum(p - 1, 0), 0))],
        out_specs=[pl.BlockSpec((128, D), lambda b, p: (b * nb + p, 0)),
                   pl.BlockSpec((1, D), lambda b, p: (0, 0))],
        out_shape=[jax.ShapeDtypeStruct(h3.shape, F32), jax.ShapeDtypeStruct((1, D), F32)],
        compiler_params=_cp(),
    )(h3, target)


def _rope_tables(B, LP):
    pos = jnp.concatenate([jnp.arange(MPAD, dtype=F32), NMETA + jnp.arange(LP - MPAD, dtype=F32)])
    inv_freq = ROPE_THETA ** (-jnp.arange(0, ROPE, 2, dtype=F32) / ROPE)
    ang = pos[:, None] * inv_freq[None, :]
    cos, sin = jnp.cos(ang), jnp.sin(ang)
    z32 = jnp.zeros((LP, 32), F32)
    rc = jnp.concatenate([jnp.ones((LP, 64), F32), cos, cos, z32], axis=1)
    rs = jnp.concatenate([jnp.zeros((LP, 64), F32), -sin, sin, z32], axis=1)
    return jnp.tile(rc, (B, 1)), jnp.tile(rs, (B, 1))


def _pad_lanes(v, start, width=128):
    n = v.shape[1]
    return jnp.concatenate([jnp.zeros((1, start), F32), v, jnp.zeros((1, width - start - n), F32)], axis=1)


G_FFN1 = ["ffn1_w_gu", "ffn1_w_down"]
G_MIX = ["w_in", "mla_w_uq", "mla_w_ukv", "w_branch_fox", "w_branch_mla", "w_out"]
G_OUT = ["w_out", "w_branch_fox", "w_branch_mla"]
G_IN = ["w_in", "mla_w_uq", "mla_w_ukv"]


def _step(x, target, meta, vec, gath, shards):
    dist = shards is not None
    B, S, _ = x.shape
    LP = MPAD + S
    T = B * LP
    gath = dict(gath)

    def gather(names):
        return [_gather_task([shards[n] for n in names])] if dist else []

    def gathered(names, touts):
        if dist:
            gath.update(zip(names, touts[0]))

    g4, sums, red = {}, {}, {}

    def scatter(names):
        return [_a2a_task([_pieces(g4[n]) for n in names])] if dist else []

    def scattered(names, tout, me):
        for n, r in zip(names, tout):
            sums[n] = _sum_pieces(r, _pieces(g4[n]), me, "rs_sum_" + n)

    def join(names):
        return [_join_task([sums[n] for n in names])] if dist else []

    def joined(names, tout):
        for n, r in zip(names, tout):
            red[n] = (sums[n], r)

    me = None
    if dist:
        me = (4 * lax.axis_index("x") + 2 * lax.axis_index("y") + lax.axis_index("c")).reshape(1).astype(jnp.int32)

    h0 = jnp.concatenate([jnp.broadcast_to(meta[None], (B, NMETA, D)),
                          jnp.zeros((B, MPAD - NMETA, D), F32), x], axis=1).reshape(T, D)
    rc, rs = _rope_tables(B, LP)
    gfq = jnp.tile(vec["fox_q_norm"], (1, 2))
    gfk = jnp.tile(vec["fox_k_norm"], (1, 2))
    gmq = _pad_lanes(vec["mla_q_norm"], 0)
    gmk = _pad_lanes(vec["mla_k_norm"], 0)
    bfv = _pad_lanes(vec["b_forget"], L_FL)

    w1gu, w1d = gath["ffn1_w_gu"], gath["ffn1_w_down"].reshape(2, FH, D)
    h1, touts = _ffn_fwd(h0, vec["ffn1_norm"], w1gu, w1d, "ffn1_fwd", gather(G_MIX))
    gathered(G_MIX, touts)
    wm = _mixer_weights(gath)
    small = (gfq, gfk, vec["mla_cq_norm"], vec["mla_ckv_norm"], gmq, gmk, bfv, wm["wuq"], wm["wuk"], wm["wuv"])
    proj, u2 = _inproj_fwd(h1, vec["mix_norm"], wm["w_in"], "inproj_fwd")
    fq, fk, fv, qm, km, vm, lf = _prep_fwd(proj, rc, rs, *small, name="prep_fwd")
    lf_rows = lf[:, L_FL:L_FL + NH].reshape(B, LP, NH).transpose(0, 2, 1).reshape(B * NH, LP)
    crow = _forget_scan(lf_rows, False, "forget_scan")
    ctok = crow.reshape(B, 4, 2, LP).transpose(0, 1, 3, 2).reshape(B * 4, LP, 2)
    bias = (ctok, crow.reshape(B * NH, 1, LP))
    of, touts = _attn_fwd(fq, fk, fv, bias, B, LP, False, 64 ** -0.5, "fox_fwd", gather(["ffn2_w_gu"]))
    gathered(["ffn2_w_gu"], touts)
    om, touts = _attn_fwd(qm, km, vm, None, B, LP, True, MLA_QK ** -0.5, "mla_fwd", gather(["ffn2_w_down"]))
    gathered(["ffn2_w_down"], touts)
    h2, mix = _post_fwd(h1, of, om, proj, vec["b_gate"], wm["wbf"], wm["wbm"], wm["w_out"], "post_fwd")
    w2gu, w2d = gath["ffn2_w_gu"], gath["ffn2_w_down"].reshape(2, FH, D)
    h3, _ = _ffn_fwd(h2, vec["ffn2_norm"], w2gu, w2d, "ffn2_fwd")
    dy, lpart = _loss_head(h3, target, B, LP, "loss_head")

    gv = {}
    (dh2, u3, a2, dgp2, gv["ffn2_norm"]), _ = _ffn_bwd(h2, dy, vec["ffn2_norm"], w2gu, w2d, "ffn2_bwd")
    gu_slot = lambda j: lax.rem(j, 2) * 2 + lax.div(j, 2)
    g4["ffn2_w_gu"] = _wgrad(u3, dgp2, "ffn2_dwgu", bn=FH, shard_major=True, slot=gu_slot)[0]
    g4["ffn2_w_down"] = _wgrad(a2, dy, "ffn2_dwd", scale=0.5, bk=FH)[0].reshape(N_CHIPS, DFF // N_CHIPS, D)

    dgl, dbf, dbm, dof, dom, gv["b_gate"] = _post_bwd(dh2, of, om, proj, vec["b_gate"], wm["wbf"], wm["wbm"], wm["w_out"], "post_bwd")
    g4["w_out"] = _wgrad(mix, dh2, "dw_out")[0].reshape(N_CHIPS, D // N_CHIPS, D)
    g4["w_branch_fox"] = _cols_to_shards(_wgrad(of, dbf, "dw_bf")[0])
    g4["w_branch_mla"] = _cols_to_shards(_wgrad(om, dbm, "dw_bm")[0])
    G_FFN2 = ["ffn2_w_gu", "ffn2_w_down"]
    (dfq, dfk, dfv, dc0, dc1), touts = _attn_bwd(fq, fk, fv, dof, bias, B, LP, False, 64 ** -0.5, "fox_bwd", scatter(G_FFN2))
    if dist:
        scattered(G_FFN2, touts[0], me)
    (dqm, dkm, dvm), touts = _attn_bwd(qm, km, vm, dom, None, B, LP, True, MLA_QK ** -0.5, "mla_bwd",
                                       scatter(G_OUT) + join(G_FFN2))
    if dist:
        scattered(G_OUT, touts[0], me)
        joined(G_FFN2, touts[1])
    dc = jnp.concatenate([dc0, dc1], axis=1).reshape(B * NH, LP)
    dlf_rows = _forget_scan(dc, True, "forget_scan_bwd")
    dlf = dlf_rows.reshape(B, NH, LP).transpose(0, 2, 1).reshape(T, NH)
    dlf = jnp.concatenate([jnp.zeros((T, L_FL), F32), dlf, jnp.zeros((T, 128 - L_FL - NH), F32)], axis=1)
    (dlo, dgfq, dgfk, gv["mla_cq_norm"], gv["mla_ckv_norm"], dgmq, dgmk, dbfv,
     dwuq, dwuk, dwuv) = _prep_bwd(proj, rc, rs, *small, dfq, dfk, dfv, dqm, dkm, dvm, dlf, name="prep_bwd")
    gv["fox_q_norm"] = dgfq[:, :64] + dgfq[:, 64:]
    gv["fox_k_norm"] = dgfk[:, :64] + dgfk[:, 64:]
    gv["mla_q_norm"] = dgmq[:, :MLA_QK]
    gv["mla_k_norm"] = dgmk[:, :MLA_QK]
    gv["b_forget"] = dbfv[:, L_FL:L_FL + NH]
    dwin = jnp.concatenate([_wgrad(u2, dlo, "dw_in_lo")[0], _wgrad(u2, dgl, "dw_in_hi")[0]], axis=1)
    g4["w_in"] = _cols_to_shards(_win_from_kernel(dwin))
    g4["mla_w_uq"] = _cols_to_shards(
        dwuq.astype(GRAD_DTYPE).reshape(QR, NH, 128)[:, :, :MLA_QK].reshape(QR, NH * MLA_QK))
    dukv = jnp.concatenate([dwuk.reshape(KVR, NH, 128)[:, :, :64], dwuv.reshape(KVR, NH, 64)], axis=2)
    g4["mla_w_ukv"] = _cols_to_shards(dukv.astype(GRAD_DTYPE).reshape(KVR, NH * 128))
    dh1, gv["mix_norm"] = _inproj_bwd(h1, dh2, dlo, dgl, vec["mix_norm"], wm["w_in"], "inproj_bwd")

    (dh0, u1, a1, dgp1, gv["ffn1_norm"]), touts = _ffn_bwd(h0, dh1, vec["ffn1_norm"], w1gu, w1d, "ffn1_bwd",
                                                            scatter(G_IN) + join(G_OUT))
    if dist:
        scattered(G_IN, touts[0], me)
        joined(G_OUT, touts[1])
    dh0 = dh0.reshape(B, LP, D)
    grad_x = dh0[:, MPAD:]
    grad_meta = jnp.sum(dh0[:, :NMETA], axis=0)
    share = [_share_task([_stack_vectors([gv[n] for n in VECS]), grad_meta, lpart])] if dist else []
    g4["ffn1_w_gu"], touts = _wgrad(u1, dgp1, "ffn1_dwgu", bn=FH, shard_major=True, slot=gu_slot, tasks=share)
    shared = touts[0] if dist else None
    dwd1, touts = _wgrad(a1, dh1, "ffn1_dwd", scale=0.5, bk=FH, tasks=scatter(["ffn1_w_gu"]) + join(G_IN))
    g4["ffn1_w_down"] = dwd1.reshape(N_CHIPS, DFF // N_CHIPS, D)
    if dist:
        scattered(["ffn1_w_gu"], touts[0], me)
        joined(G_IN, touts[1])
        scattered(["ffn1_w_down"], _run_tasks(scatter(["ffn1_w_down"]), "rs_ffn1_w_down")[0], me)
        joined(G_FFN1, _run_tasks(join(G_FFN1), "rs_join_ffn1")[0])
    return lpart, grad_x, grad_meta, gv, (red if dist else g4), shared


def _cols_from_shards(g4):
    n, r, c = g4.shape
    return g4.transpose(1, 0, 2).reshape(r, n * c)


def _cols_to_shards(full):
    r, c4 = full.shape
    return full.reshape(r, N_CHIPS, c4 // N_CHIPS).transpose(1, 0, 2)


def _win_to_kernel(wfull):
    z = lambda n: jnp.zeros((D, n), wfull.dtype)
    fl, cq, ckv, kr, gate = (wfull[:, 1536:1544], wfull[:, 1544:1800], wfull[:, 1800:1928],
                             wfull[:, 1928:1960], wfull[:, 1960:4008])
    misc = jnp.concatenate([z(L_KR), kr, fl, z(128 - L_FL - NH)], axis=1)
    return jnp.concatenate([wfull[:, :1536], cq, ckv, misc, gate], axis=1)


def _win_from_kernel(gk):
    m = C_MISC
    return jnp.concatenate([gk[:, :1536], gk[:, m + L_FL:m + L_FL + NH], gk[:, C_CQ:C_CQ + QR],
                            gk[:, C_CKV:C_CKV + KVR], gk[:, m + L_KR:m + L_KR + ROPE], gk[:, C_GATE:]], axis=1)


def _pieces(g4):
    n, r, c = g4.shape
    return g4.reshape(2 * n, r // 2, c)


def _mixer_weights(gath):
    w = {}
    w["w_in"] = _win_to_kernel(_cols_from_shards(gath["w_in"]))
    uq = _cols_from_shards(gath["mla_w_uq"]).reshape(QR, NH, MLA_QK)
    w["wuq"] = jnp.pad(uq, ((0, 0), (0, 0), (0, 128 - MLA_QK))).reshape(QR, NH * 128)
    ukv = _cols_from_shards(gath["mla_w_ukv"]).reshape(KVR, NH, 128)
    w["wuk"] = jnp.pad(ukv[:, :, :64], ((0, 0), (0, 0), (0, 64))).reshape(KVR, NH * 128)
    w["wuv"] = ukv[:, :, 64:].reshape(KVR, NH * 64)
    w["wbf"] = _cols_from_shards(gath["w_branch_fox"])
    w["wbm"] = _cols_from_shards(gath["w_branch_mla"])
    w["w_out"] = gath["w_out"].reshape(D, D)
    return w


def _chip_peers(x, y):
    return [(1 - x, y), (x, 1 - y), (1 - x, 1 - y)]


RELS = [(dx, dy, dc) for dx in (0, 1) for dy in (0, 1) for dc in (0, 1)][1:]


def _here():
    return lax.axis_index("x"), lax.axis_index("y"), lax.axis_index("c")


def _flip(a, d):
    return (1 - a) if d else a


def _remote(src, dst, send, recv, i, dev):
    return functools.partial(pltpu.make_async_remote_copy, src_ref=src, dst_ref=dst, send_sem=send.at[i],
                             recv_sem=recv.at[i], device_id=dev, device_id_type=MESH)


def _gather_task(shards):
    n = len(shards)

    def descs(ins, outs, sems):
        send, recv, loc = sems
        x, y, c = _here()
        j = 2 * x + y
        locs, pairs = [], []
        for k in range(n):
            locs.append(functools.partial(pltpu.make_async_copy, ins[k], outs[k].at[j], loc.at[k]))
            for r, (px, py) in enumerate(_chip_peers(x, y)):
                dev = (px, py, c)
                pairs.append((_remote(ins[k], outs[k].at[j], send, recv, 3 * k + r, dev),
                              _remote(ins[k], outs[k].at[2 * px + py], send, recv, 3 * k + r, dev)))
        return locs, pairs

    return _Task(shards, [jax.ShapeDtypeStruct((N_CHIPS,) + s.shape, s.dtype) for s in shards],
                 [pltpu.SemaphoreType.DMA((3 * n,)), pltpu.SemaphoreType.DMA((3 * n,)), pltpu.SemaphoreType.DMA((n,))],
                 descs)


class _SplitGather(_Task):
    PARTS = 2

    def __init__(self, shards):
        n = 3 * len(shards) * self.PARTS
        dma = pltpu.SemaphoreType.DMA
        super().__init__(shards, [jax.ShapeDtypeStruct((N_CHIPS,) + s.shape, s.dtype) for s in shards],
                         [dma((n,)), dma((n,)), dma((n,)), dma((n,)), dma((len(shards),))], None)

    def _plan(self, ins, outs, sems):
        send, recv, fsend, frecv, loc = sems
        x, y, c = _here()
        j = 2 * x + y
        locs, first, passed = [], [], []
        for k in range(len(ins)):
            h = self.ins[k].shape[0] // 2
            parts = self.PARTS if h % (32 * self.PARTS) == 0 else 1
            hp = h // parts
            locs.append(functools.partial(pltpu.make_async_copy, ins[k], outs[k].at[j], loc.at[k]))
            for r, (px, py) in enumerate(_chip_peers(x, y)):
                p = 2 * px + py
                for q in range(parts):
                    i = (3 * k + r) * self.PARTS + q
                    mine = pl.ds(pl.multiple_of(c * h + q * hp, 8), hp)
                    theirs = pl.ds(pl.multiple_of((1 - c) * h + q * hp, 8), hp)
                    first.append((_remote(ins[k].at[mine], outs[k].at[j, mine], send, recv, i, (px, py, c)),
                                  _remote(ins[k].at[mine], outs[k].at[p, mine], send, recv, i, (px, py, c))))
                    passed.append((_remote(outs[k].at[p, mine], outs[k].at[p, mine], fsend, frecv, i, (x, y, 1 - c)),
                                   _remote(outs[k].at[p, mine], outs[k].at[p, theirs], fsend, frecv, i, (x, y, 1 - c))))
        return locs, first, passed

    def start(self, ins, outs, sems):
        locs, first, _ = self._plan(ins, outs, sems)
        for lc in locs:
            lc().start()
        for snd, _ in first:
            snd().start()

    def wait(self, ins, outs, sems):
        locs, first, passed = self._plan(ins, outs, sems)
        for (_, landed), (pass_on, _) in zip(first, passed):
            landed().wait_recv()
            pass_on().start()
        for _, rcv in passed:
            rcv().wait_recv()
        for snd, _ in first + passed:
            snd().wait_send()
        for lc in locs:
            lc().wait()


def _a2a_task(ps):
    n = len(ps)
    nr = len(RELS)

    def descs(ins, outs, sems):
        send, recv = sems
        x, y, c = _here()
        me = 4 * x + 2 * y + c
        pairs = []
        for k in range(n):
            for i, (dx, dy, dc) in enumerate(RELS):
                dev = (_flip(x, dx), _flip(y, dy), _flip(c, dc))
                peer = 4 * dev[0] + 2 * dev[1] + dev[2]
                pairs.append((_remote(ins[k].at[peer], outs[k].at[me], send, recv, nr * k + i, dev),
                              _remote(ins[k].at[peer], outs[k].at[peer], send, recv, nr * k + i, dev)))
        return [], pairs

    return _Task(ps, [jax.ShapeDtypeStruct(p.shape, p.dtype) for p in ps],
                 [pltpu.SemaphoreType.DMA((nr * n,)), pltpu.SemaphoreType.DMA((nr * n,))], descs)


def _share_task(vs):
    n = len(vs)
    nr = len(RELS)

    def descs(ins, outs, sems):
        send, recv, loc = sems
        x, y, c = _here()
        me = 4 * x + 2 * y + c
        locs, pairs = [], []
        for k in range(n):
            locs.append(functools.partial(pltpu.make_async_copy, ins[k], outs[k].at[me], loc.at[k]))
            for i, (dx, dy, dc) in enumerate(RELS):
                dev = (_flip(x, dx), _flip(y, dy), _flip(c, dc))
                peer = 4 * dev[0] + 2 * dev[1] + dev[2]
                pairs.append((_remote(ins[k], outs[k].at[me], send, recv, nr * k + i, dev),
                              _remote(ins[k], outs[k].at[peer], send, recv, nr * k + i, dev)))
        return locs, pairs

    return _Task(vs, [jax.ShapeDtypeStruct((N_DEV,) + v.shape, v.dtype) for v in vs],
                 [pltpu.SemaphoreType.DMA((nr * n,)), pltpu.SemaphoreType.DMA((nr * n,)), pltpu.SemaphoreType.DMA((n,))],
                 descs)


def _join_task(ss):
    n = len(ss)

    def descs(ins, outs, sems):
        send, recv = sems
        x, y, c = _here()
        pairs = []
        for k in range(n):
            cp = _remote(ins[k], outs[k], send, recv, k, (x, y, 1 - c))
            pairs.append((cp, cp))
        return [], pairs

    return _Task(ss, [jax.ShapeDtypeStruct(s.shape, s.dtype) for s in ss],
                 [pltpu.SemaphoreType.DMA((n,)), pltpu.SemaphoreType.DMA((n,))], descs)


def _sum_pieces(recv, own, me, name):
    n, h, c = recv.shape
    tr = _tile(h, (256, 176, 128, 64))

    def body(me_ref, r_ref, o_ref, out_ref):
        s = pl.program_id(1)
        val = jnp.where(s == me_ref[0], o_ref[0], r_ref[0]).astype(F32)

        @pl.when(s == 0)
        def _():
            out_ref[...] = val

        @pl.when(s > 0)
        def _():
            out_ref[...] += val

    def other(s, m):
        return jnp.where(s == m[0], (s + 1) % n, s)

    return _call(
        body, name=name,
        grid_spec=pltpu.PrefetchScalarGridSpec(
            num_scalar_prefetch=1, grid=(h // tr, n),
            in_specs=[pl.BlockSpec((1, tr, c), lambda i, s, m: (other(s, m), i, 0)),
                      pl.BlockSpec((1, tr, c), lambda i, s, m: (m[0], i, 0))],
            out_specs=pl.BlockSpec((tr, c), lambda i, s, m: (i, 0))),
        out_shape=jax.ShapeDtypeStruct((h, c), F32),
        compiler_params=_cp(),
    )(me, recv, own)


def _adamw_update(gg, w, m, v):
    c1 = 1.0 / (1.0 - ADAM_B1 ** ADAM_STEP)
    c2 = 1.0 / (1.0 - ADAM_B2 ** ADAM_STEP)
    nm = ADAM_B1 * m + (1.0 - ADAM_B1) * gg
    nv = ADAM_B2 * v + (1.0 - ADAM_B2) * (gg * gg)
    return -ADAM_LR * ((nm * c1) / (jnp.sqrt(nv * c2) + ADAM_EPS) + ADAM_WD * w), nm, nv


def _adamw_small(gvec8, gmeta8, lp8, chip, ws, ms, vs, name):
    na = len(ws)

    def dev_sum(ref):
        acc = ref[0]
        for s in range(1, N_DEV):
            acc = acc + ref[s]
        return acc

    def body(c_ref, gv_ref, gm_ref, lp_ref, *refs):
        w_refs, m_refs, v_refs = refs[:na], refs[na:2 * na], refs[2 * na:3 * na]
        l_ref = refs[3 * na]
        outs = refs[3 * na + 1:]
        g_refs, d_refs, nm_refs, nv_refs = outs[:na], outs[na:2 * na], outs[2 * na:3 * na], outs[3 * na:]
        l_ref[...] = dev_sum(lp_ref)
        gvec = dev_sum(gv_ref)
        for k in range(na):
            gg = gvec[k:k + 1, 0:ws[k].shape[1]] if k < na - 1 else dev_sum(gm_ref)
            g_refs[k][...] = gg
            d_refs[k][...], nm_refs[k][...], nv_refs[k][...] = _adamw_update(gg, w_refs[k][...], m_refs[k][...], v_refs[k][...])

    whole = lambda a: pl.BlockSpec(a.shape, lambda i, c: (0,) * a.ndim)
    arrs = list(ws) + list(ms) + list(vs)
    res = _call(
        body, name=name,
        grid_spec=pltpu.PrefetchScalarGridSpec(
            num_scalar_prefetch=1, grid=(1,),
            in_specs=[whole(gvec8), pl.BlockSpec((N_DEV, NMETA, D // N_CHIPS), lambda i, c: (0, 0, c[0])), whole(lp8)]
                     + [whole(a) for a in arrs],
            out_specs=[pl.BlockSpec((1, D), lambda i, c: (0, 0))] + [whole(a) for a in ws] * 4),
        out_shape=[jax.ShapeDtypeStruct((1, D), F32)] + [jax.ShapeDtypeStruct(a.shape, F32) for a in ws] * 4,
        compiler_params=_cp(),
    )(chip, gvec8, gmeta8, lp8, *arrs)
    return res[0], [list(res[1 + i * na:1 + (i + 1) * na]) for i in range(4)]


def _adamw_halves(wt, mine, theirs, m, v, core, name):
    r, c = wt.shape
    h = r // 2
    tr = _tile(h, (256, 176, 128, 64))
    nh = h // tr

    def body(c_ref, w_ref, a_ref, b_ref, m_ref, v_ref, g_ref, d_ref, nm_ref, nv_ref):
        gg = jnp.where(pl.program_id(0) // nh == c_ref[0], a_ref[...], b_ref[...])
        g_ref[...] = gg
        d_ref[...], nm_ref[...], nv_ref[...] = _adamw_update(gg, w_ref[...], m_ref[...], v_ref[...])

    full = pl.BlockSpec((tr, c), lambda i, cr: (i, 0))
    half = pl.BlockSpec((tr, c), lambda i, cr: (i % nh, 0))
    return _call(
        body, name=name,
        grid_spec=pltpu.PrefetchScalarGridSpec(
            num_scalar_prefetch=1, grid=(2 * nh,),
            in_specs=[full, half, half, full, full], out_specs=[full] * 4),
        out_shape=[jax.ShapeDtypeStruct((r, c), F32)] * 4,
        compiler_params=_cp(),
    )(core, wt, mine, theirs, m, v)


MATS = ["ffn1_w_gu", "ffn1_w_down", "w_in", "mla_w_uq", "mla_w_ukv", "w_branch_fox", "w_branch_mla",
        "w_out", "ffn2_w_gu", "ffn2_w_down"]
VECS = ["ffn1_norm", "mix_norm", "b_forget", "b_gate", "fox_q_norm", "fox_k_norm", "mla_cq_norm",
        "mla_ckv_norm", "mla_q_norm", "mla_k_norm", "ffn2_norm"]
WEIGHTS = ["meta_tokens", "ffn1_norm", "ffn1_w_gu", "ffn1_w_down", "mix_norm", "w_in", "b_forget", "b_gate",
           "fox_q_norm", "fox_k_norm", "mla_cq_norm", "mla_w_uq", "mla_ckv_norm", "mla_w_ukv", "mla_q_norm",
           "mla_k_norm", "w_branch_fox", "w_branch_mla", "w_out", "ffn2_norm", "ffn2_w_gu", "ffn2_w_down"]


VEC_LANES = 2048


def _stack_vectors(parts):
    rows = [_pad_lanes(p, 0, VEC_LANES) for p in parts]
    rows.append(jnp.zeros((-len(parts) % 8, VEC_LANES), F32))
    return jnp.concatenate(rows, axis=0)


def kernel(x, meta_tokens, ffn1_norm, ffn1_w_gu, ffn1_w_down, mix_norm, w_in, b_forget, b_gate, fox_q_norm, fox_k_norm, mla_cq_norm, mla_w_uq, mla_ckv_norm, mla_w_ukv, mla_q_norm, mla_k_norm, w_branch_fox, w_branch_mla, w_out, ffn2_norm, ffn2_w_gu, ffn2_w_down, loss_target, m_meta_tokens, m_ffn1_norm, m_ffn1_w_gu, m_ffn1_w_down, m_mix_norm, m_w_in, m_b_forget, m_b_gate, m_fox_q_norm, m_fox_k_norm, m_mla_cq_norm, m_mla_w_uq, m_mla_ckv_norm, m_mla_w_ukv, m_mla_q_norm, m_mla_k_norm, m_w_branch_fox, m_w_branch_mla, m_w_out, m_ffn2_norm, m_ffn2_w_gu, m_ffn2_w_down, v_meta_tokens, v_ffn1_norm, v_ffn1_w_gu, v_ffn1_w_down, v_mix_norm, v_w_in, v_b_forget, v_b_gate, v_fox_q_norm, v_fox_k_norm, v_mla_cq_norm, v_mla_w_uq, v_mla_ckv_norm, v_mla_w_ukv, v_mla_q_norm, v_mla_k_norm, v_w_branch_fox, v_w_branch_mla, v_w_out, v_ffn2_norm, v_ffn2_w_gu, v_ffn2_w_down):
    a = dict(locals())
    wts = {n: a[n] for n in WEIGHTS}
    ms = {n: a["m_" + n] for n in WEIGHTS}
    vs = {n: a["v_" + n] for n in WEIGHTS}
    cx, cy, cc = lax.axis_index("x"), lax.axis_index("y"), lax.axis_index("c")
    chip = 2 * cx + cy

    shards = {n: wts[n][0].astype(BF16) for n in MATS}
    first = _run_tasks([_SplitGather([shards[n] for n in G_FFN1] + [meta_tokens])], "gather_ffn1")[0]
    gath = dict(zip(G_FFN1, first[:-1]))
    meta_full = _cols_from_shards(first[-1])

    _, grad_x, _, _, gred, (gvec8, gmeta8, lp8) = _step(x, loss_target, meta_full, {n: wts[n] for n in VECS}, gath, shards)

    sm_names = VECS + ["meta_tokens"]
    lsum, sm = _adamw_small(gvec8, gmeta8, lp8, chip.reshape(1).astype(jnp.int32), [wts[n] for n in sm_names],
                            [ms[n] for n in sm_names], [vs[n] for n in sm_names], "adamw_small")
    loss = jnp.sum(lsum)

    grads, delta, new_m, new_v = {}, {}, {}, {}
    core = cc.reshape(1).astype(jnp.int32)
    for n in MATS:
        shp = wts[n].shape
        mine, theirs = gred[n]
        res = _adamw_halves(wts[n][0], mine, theirs, ms[n][0], vs[n][0], core, "adamw_" + n)
        grads[n], delta[n], new_m[n], new_v[n] = (t.reshape(shp) for t in res)
    for k, n in enumerate(sm_names):
        grads[n], delta[n], new_m[n], new_v[n] = (sm[i][k] for i in range(4))

    return (loss, grad_x, *[grads[n] for n in WEIGHTS], *[delta[n] for n in WEIGHTS],
            *[new_m[n] for n in WEIGHTS], *[new_v[n] for n in WEIGHTS])
```

```python
import functools

import jax
import jax.numpy as jnp
from jax import lax
from jax.experimental import pallas as pl
from jax.experimental.pallas import tpu as pltpu

F32 = jnp.float32
BF16 = jnp.bfloat16
MESH = pl.DeviceIdType.MESH

D = 1024
DFF = 2816
FH = DFF // 2
NMETA = 16
MPAD = 128
EPS = 1e-6
NH = 8
FOXW = 512
QR = 256
KVR = 128
ROPE = 32
MLA_QK = 96
PROJW = 4096
ROPE_THETA = 10000.0
N_CHIPS = 4
N_DEV = 8

ADAM_LR = 0.001
ADAM_B1 = 0.9
ADAM_B2 = 0.999
ADAM_EPS = 1e-08
ADAM_WD = 0.01
ADAM_STEP = 10

VMEM_LIMIT = 56 * 2**20
GRAD_DTYPE = BF16

NT = (((1,), (1,)), ((), ()))
TN = (((0,), (0,)), ((), ()))


def _call(body, **kw):
    return pl.pallas_call(body, **kw)


def _cp(**kw):
    return pltpu.CompilerParams(vmem_limit_bytes=VMEM_LIMIT, **kw)


HBM = pl.BlockSpec(memory_space=pltpu.HBM)


class _Task:
    def __init__(self, ins, out_shapes, sems, descs):
        self.ins, self.out_shapes, self.sems, self.descs = list(ins), list(out_shapes), list(sems), descs

    def start(self, ins, outs, sems):
        locs, pairs = self.descs(ins, outs, sems)
        for lc in locs:
            lc().start()
        for snd, _ in pairs:
            snd().start()

    def wait(self, ins, outs, sems):
        locs, pairs = self.descs(ins, outs, sems)
        for _, rcv in pairs:
            rcv().wait_recv()
        for snd, _ in pairs:
            snd().wait_send()
        for lc in locs:
            lc().wait()


def _call_tasks(body, tasks, *, name, grid, in_specs, out_specs, out_shape, args, scratch_shapes=()):
    in_specs, out_specs, out_shape, scratch_shapes = map(list, (in_specs, out_specs, out_shape, scratch_shapes))
    n_in, n_out, n_sc = len(in_specs), len(out_specs), len(scratch_shapes)
    t_in = [len(t.ins) for t in tasks]
    t_out = [len(t.out_shapes) for t in tasks]
    t_sem = [len(t.sems) for t in tasks]

    def wrapped(*refs):
        pos = [0]

        def take(n):
            pos[0] += n
            return refs[pos[0] - n:pos[0]]

        ins, tins = take(n_in), [take(n) for n in t_in]
        outs, touts = take(n_out), [take(n) for n in t_out]
        sc, tsems = take(n_sc), [take(n) for n in t_sem]
        if tasks:
            first = functools.reduce(jnp.logical_and, [pl.program_id(a) == 0 for a in range(len(grid))])
            last = functools.reduce(jnp.logical_and, [pl.program_id(a) == grid[a] - 1 for a in range(len(grid))])

            @pl.when(first)
            def _():
                for t, a, b, s in zip(tasks, tins, touts, tsems):
                    t.start(a, b, s)

        body(*ins, *outs, *sc)
        if tasks:
            @pl.when(last)
            def _():
                for t, a, b, s in zip(tasks, tins, touts, tsems):
                    t.wait(a, b, s)

    res = _call(
        wrapped, name=name, grid=grid,
        in_specs=in_specs + [HBM] * sum(t_in), out_specs=out_specs + [HBM] * sum(t_out),
        out_shape=out_shape + [s for t in tasks for s in t.out_shapes],
        scratch_shapes=scratch_shapes + [s for t in tasks for s in t.sems],
        compiler_params=_cp(),
    )(*args, *[a for t in tasks for a in t.ins])
    res = list(res)
    touts, pos = [], n_out
    for n in t_out:
        touts.append(res[pos:pos + n])
        pos += n
    return res[:n_out], touts


def _run_tasks(tasks, name):
    t_in = [len(t.ins) for t in tasks]
    t_out = [len(t.out_shapes) for t in tasks]
    t_sem = [len(t.sems) for t in tasks]

    def body(*refs):
        pos = [0]

        def take(n):
            pos[0] += n
            return refs[pos[0] - n:pos[0]]

        tins, touts, tsems = [take(n) for n in t_in], [take(n) for n in t_out], [take(n) for n in t_sem]
        for t, a, b, s in zip(tasks, tins, touts, tsems):
            t.start(a, b, s)
        for t, a, b, s in zip(tasks, tins, touts, tsems):
            t.wait(a, b, s)

    res = list(_call(
        body, name=name, in_specs=[HBM] * sum(t_in), out_specs=[HBM] * sum(t_out),
        out_shape=[s for t in tasks for s in t.out_shapes],
        scratch_shapes=[s for t in tasks for s in t.sems],
    )(*[a for t in tasks for a in t.ins]))
    touts, pos = [], 0
    for n in t_out:
        touts.append(res[pos:pos + n])
        pos += n
    return touts


def _tile(n, cands):
    for c in cands:
        if n % c == 0:
            return c
    raise ValueError(f"no tile for {n} among {cands}")


def _dot(a, b, dims=None):
    if dims is None:
        return jnp.dot(a, b, preferred_element_type=F32)
    return lax.dot_general(a, b, dims, preferred_element_type=F32)


def _rms(x, gain, n):
    r = lax.rsqrt(jnp.sum(x * x, axis=-1, keepdims=True) * (1.0 / n) + EPS)
    xh = x * r
    return xh * gain, xh, r


def _rms_bwd(dy, xh, r, gain, n):
    dxh = dy * gain
    return r * (dxh - xh * (jnp.sum(dxh * xh, axis=-1, keepdims=True) * (1.0 / n)))


def _lane(shape):
    return lax.broadcasted_iota(jnp.int32, shape, len(shape) - 1)


def _half_sum(x):
    lo = _lane(x.shape) < 64
    s_lo = jnp.sum(jnp.where(lo, x, 0.0), axis=-1, keepdims=True)
    s_hi = jnp.sum(jnp.where(lo, 0.0, x), axis=-1, keepdims=True)
    return jnp.where(lo, s_lo, s_hi)


def _rope_swap(x):
    ln = _lane(x.shape)
    sw = jnp.where(ln < 80, pltpu.roll(x, 112, 1), pltpu.roll(x, 16, 1))
    return jnp.where(jnp.logical_and(ln >= 64, ln < 96), sw, 0.0)


def _colsum(x):
    return jnp.sum(x, axis=0, keepdims=True)


def _ffn_weight_specs():
    once = pl.Buffered(1)
    return [pl.BlockSpec((D, DFF), lambda i: (0, 0), pipeline_mode=once),
            pl.BlockSpec((D, DFF), lambda i: (0, 1), pipeline_mode=once),
            pl.BlockSpec((DFF, D), lambda i: (0, 0), pipeline_mode=once)]


def _ffn_fwd(h, norm, wgu, wd, name, tasks=()):
    T = h.shape[0]
    tm = _tile(T, (512, 384, 256, 128))

    def body(h_ref, n_ref, wg_ref, wu_ref, wd_ref, o_ref):
        x = h_ref[...]
        u, _, _ = _rms(x, n_ref[...], D)
        ub = u.astype(BF16)
        g = _dot(ub, wg_ref[...])
        p = _dot(ub, wu_ref[...])
        a = (g * jax.nn.sigmoid(g)) * p
        o_ref[...] = x + 0.5 * _dot(a.astype(BF16), wd_ref[...])

    (out,), touts = _call_tasks(
        body, tasks, name=name, grid=(T // tm,),
        in_specs=[pl.BlockSpec((tm, D), lambda i: (i, 0)), pl.BlockSpec((1, D), lambda i: (0, 0))] + _ffn_weight_specs(),
        out_specs=[pl.BlockSpec((tm, D), lambda i: (i, 0))],
        out_shape=[jax.ShapeDtypeStruct((T, D), F32)],
        args=(h, norm, wgu, wgu, wd))
    return out, touts


def _ffn_bwd(h, dout, norm, wgu, wd, name, tasks=()):
    T = h.shape[0]
    tm = _tile(T, (256, 128))

    def body(h_ref, d_ref, n_ref, wg_ref, wu_ref, wd_ref, dh_ref, u_ref, a_ref, dgp_ref, dn_ref):
        @pl.when(pl.program_id(0) == 0)
        def _():
            dn_ref[...] = jnp.zeros_like(dn_ref)

        u, xh, r = _rms(h_ref[...], n_ref[...], D)
        ub = u.astype(BF16)
        u_ref[...] = ub
        g = _dot(ub, wg_ref[...])
        p = _dot(ub, wu_ref[...])
        s = jax.nn.sigmoid(g)
        sl = g * s
        dz = (0.5 * d_ref[...]).astype(BF16)
        da = _dot(dz, wd_ref[...], NT)
        dp = da * sl
        dg = (da * p) * (s * (1.0 + g * (1.0 - s)))
        a_ref[...] = (sl * p).astype(BF16)
        dgb = dg.astype(BF16)
        dpb = dp.astype(BF16)
        dgp_ref[:, :DFF] = dgb
        dgp_ref[:, DFF:] = dpb
        du = _dot(dgb, wg_ref[...], NT) + _dot(dpb, wu_ref[...], NT)
        dn_ref[...] += _colsum(du * xh)
        dh_ref[...] = d_ref[...] + _rms_bwd(du, xh, r, n_ref[...], D)

    row = lambda w: pl.BlockSpec((tm, w), lambda i: (i, 0))
    return _call_tasks(
        body, tasks, name=name, grid=(T // tm,),
        in_specs=[row(D), row(D), pl.BlockSpec((1, D), lambda i: (0, 0))] + _ffn_weight_specs(),
        out_specs=[row(D), row(D), row(DFF), row(2 * DFF), pl.BlockSpec((1, D), lambda i: (0, 0))],
        out_shape=[jax.ShapeDtypeStruct((T, D), F32),
                   jax.ShapeDtypeStruct((T, D), BF16),
                   jax.ShapeDtypeStruct((T, DFF), BF16),
                   jax.ShapeDtypeStruct((T, 2 * DFF), BF16),
                   jax.ShapeDtypeStruct((1, D), F32)],
        args=(h, dout, norm, wgu, wgu, wd))


def _wgrad(x, y, name, scale=1.0, bk=None, bn=None, shard_major=False, tasks=()):
    T, K = x.shape
    N = y.shape[1]
    bk = bk or K
    bn = bn or N
    bt = _tile(T, (1088, 512, 384, 256, 128))
    nt = T // bt

    def body(x_ref, y_ref, o_ref, acc_ref):
        t = pl.program_id(2)

        @pl.when(t == 0)
        def _():
            acc_ref[...] = jnp.zeros_like(acc_ref)

        acc_ref[...] += _dot(x_ref[...].astype(BF16), y_ref[...].astype(BF16), TN)

        @pl.when(t == nt - 1)
        def _():
            o_ref[...] = (acc_ref[...] * scale).astype(o_ref.dtype).reshape(o_ref.shape)

    if shard_major:
        assert bk == K
        out_spec = pl.BlockSpec((1, K, bn), lambda i, j, t: (j, 0, 0))
        out_shape = jax.ShapeDtypeStruct((N // bn, K, bn), GRAD_DTYPE)
    else:
        out_spec = pl.BlockSpec((bk, bn), lambda i, j, t: (i, j))
        out_shape = jax.ShapeDtypeStruct((K, N), GRAD_DTYPE)
    (out,), touts = _call_tasks(
        body, tasks, name=name, grid=(K // bk, N // bn, nt),
        in_specs=[pl.BlockSpec((bt, bk), lambda i, j, t: (t, i)),
                  pl.BlockSpec((bt, bn), lambda i, j, t: (t, j))],
        out_specs=[out_spec], out_shape=[out_shape],
        scratch_shapes=[pltpu.VMEM((bk, bn), F32)],
        args=(x, y))
    return out, touts


def _inproj_fwd(h, norm, w, name):
    T = h.shape[0]
    tm = _tile(T, (1088, 512, 384, 256, 128))
    tn = 1024

    def body(h_ref, n_ref, w_ref, o_ref, u_ref):
        @pl.when(pl.program_id(1) == 0)
        def _():
            u, _, _ = _rms(h_ref[...], n_ref[...], D)
            u_ref[...] = u.astype(BF16)

        o_ref[...] = _dot(u_ref[...], w_ref[...])

    return _call(
        body, name=name, grid=(T // tm, PROJW // tn),
        in_specs=[pl.BlockSpec((tm, D), lambda i, j: (i, 0)),
                  pl.BlockSpec((1, D), lambda i, j: (0, 0)),
                  pl.BlockSpec((D, tn), lambda i, j: (0, j))],
        out_specs=[pl.BlockSpec((tm, tn), lambda i, j: (i, j)),
                   pl.BlockSpec((tm, D), lambda i, j: (i, 0))],
        out_shape=[jax.ShapeDtypeStruct((T, PROJW), F32), jax.ShapeDtypeStruct((T, D), BF16)],
        compiler_params=_cp(),
    )(h, norm, w)


def _inproj_bwd(h, dres, dlo, dhi, norm, w, name):
    T = h.shape[0]
    tm = _tile(T, (512, 384, 256, 128))
    hw = PROJW // 2

    def body(h_ref, d_ref, lo_ref, hi_ref, n_ref, wlo_ref, whi_ref, dh_ref, dn_ref):
        @pl.when(pl.program_id(0) == 0)
        def _():
            dn_ref[...] = jnp.zeros_like(dn_ref)

        _, xh, r = _rms(h_ref[...], n_ref[...], D)
        du = _dot(lo_ref[...], wlo_ref[...], NT) + _dot(hi_ref[...], whi_ref[...], NT)
        dn_ref[...] += _colsum(du * xh)
        dh_ref[...] = d_ref[...] + _rms_bwd(du, xh, r, n_ref[...], D)

    return _call(
        body, name=name, grid=(T // tm,),
        in_specs=[pl.BlockSpec((tm, D), lambda i: (i, 0)),
                  pl.BlockSpec((tm, D), lambda i: (i, 0)),
                  pl.BlockSpec((tm, hw), lambda i: (i, 0)),
                  pl.BlockSpec((tm, hw), lambda i: (i, 0)),
                  pl.BlockSpec((1, D), lambda i: (0, 0)),
                  pl.BlockSpec((D, hw), lambda i: (0, 0)),
                  pl.BlockSpec((D, hw), lambda i: (0, 1))],
        out_specs=[pl.BlockSpec((tm, D), lambda i: (i, 0)),
                   pl.BlockSpec((1, D), lambda i: (0, 0))],
        out_shape=[jax.ShapeDtypeStruct((T, D), F32), jax.ShapeDtypeStruct((1, D), F32)],
        compiler_params=_cp(),
    )(h, dres, dlo, dhi, norm, w, w)


C_FQ, C_FK, C_FV, C_CQ, C_CKV, C_MISC, C_GATE = 0, 512, 1024, 1536, 1792, 1920, 2048
L_KR, L_FL = 64, 96


def _prep_fwd(proj, rc, rs, gfq, gfk, gcq, gckv, gmq, gmk, bfv, wuq, wuk, wuv, name):
    T = proj.shape[0]
    tm = _tile(T, (256, 128))

    def body(p_ref, rc_ref, rs_ref, gfq_ref, gfk_ref, gcq_ref, gckv_ref, gmq_ref, gmk_ref, bf_ref,
             wuq_ref, wuk_ref, wuv_ref, fq_ref, fk_ref, fv_ref, qm_ref, km_ref, vm_ref, lf_ref):
        for blk in range(4):
            for (c0, g_ref, o_ref) in ((C_FQ, gfq_ref, fq_ref), (C_FK, gfk_ref, fk_ref)):
                x = p_ref[:, c0 + 128 * blk:c0 + 128 * (blk + 1)]
                r = lax.rsqrt(_half_sum(x * x) * (1.0 / 64) + EPS)
                o_ref[:, 128 * blk:128 * (blk + 1)] = (x * r * g_ref[...]).astype(BF16)
        fv_ref[...] = p_ref[:, C_FV:C_FV + 512].astype(BF16)

        rcv = rc_ref[...]
        rsv = rs_ref[...]
        cqn, _, _ = _rms(p_ref[:, C_CQ:C_CQ + QR], gcq_ref[...], QR)
        qpre = _dot(cqn.astype(BF16), wuq_ref[...])
        ckvn, _, _ = _rms(p_ref[:, C_CKV:C_CKV + KVR], gckv_ref[...], KVR)
        ckvb = ckvn.astype(BF16)
        kpre = _dot(ckvb, wuk_ref[...])
        vm_ref[...] = _dot(ckvb, wuv_ref[...]).astype(BF16)
        misc = p_ref[:, C_MISC:C_MISC + 128]
        ln = _lane(misc.shape)
        kr = jnp.where(jnp.logical_and(ln >= L_KR, ln < L_KR + ROPE), misc, 0.0)
        for hh in range(NH):
            sl = slice(128 * hh, 128 * (hh + 1))
            qn, _, _ = _rms(qpre[:, sl], gmq_ref[...], MLA_QK)
            qm_ref[:, sl] = (qn * rcv + _rope_swap(qn) * rsv).astype(BF16)
            kn, _, _ = _rms(kpre[:, sl] + kr, gmk_ref[...], MLA_QK)
            km_ref[:, sl] = (kn * rcv + _rope_swap(kn) * rsv).astype(BF16)
        z = misc + bf_ref[...]
        lf_ref[...] = jnp.minimum(z, 0.0) - jnp.log(1.0 + jnp.exp(-jnp.abs(z)))

    row = lambda w: pl.BlockSpec((tm, w), lambda i: (i, 0))
    full = lambda a: pl.BlockSpec(a.shape, lambda i: (0, 0))
    return _call(
        body, name=name, grid=(T // tm,),
        in_specs=[row(PROJW // 2), row(128), row(128)] + [full(a) for a in (gfq, gfk, gcq, gckv, gmq, gmk, bfv, wuq, wuk, wuv)],
        out_specs=[row(512), row(512), row(512), row(1024), row(1024), row(512), row(128)],
        out_shape=[jax.ShapeDtypeStruct((T, 512), BF16), jax.ShapeDtypeStruct((T, 512), BF16),
                   jax.ShapeDtypeStruct((T, 512), BF16), jax.ShapeDtypeStruct((T, 1024), BF16),
                   jax.ShapeDtypeStruct((T, 1024), BF16), jax.ShapeDtypeStruct((T, 512), BF16),
                   jax.ShapeDtypeStruct((T, 128), F32)],
        compiler_params=_cp(),
    )(proj, rc, rs, gfq, gfk, gcq, gckv, gmq, gmk, bfv, wuq, wuk, wuv)


def _prep_bwd(proj, rc, rs, gfq, gfk, gcq, gckv, gmq, gmk, bfv, wuq, wuk, wuv,
              dfq, dfk, dfv, dqm, dkm, dvm, dlf, name):
    T = proj.shape[0]
    tm = _tile(T, (256, 128))

    def body(p_ref, rc_ref, rs_ref, gfq_ref, gfk_ref, gcq_ref, gckv_ref, gmq_ref, gmk_ref, bf_ref,
             wuq_ref, wuk_ref, wuv_ref, dfq_ref, dfk_ref, dfv_ref, dqm_ref, dkm_ref, dvm_ref, dlf_ref,
             dp_ref, dgfq_ref, dgfk_ref, dgcq_ref, dgckv_ref, dgmq_ref, dgmk_ref, dbf_ref,
             dwuq_ref, dwuk_ref, dwuv_ref, dqpre_sc, dkpre_sc):
        accs = (dgfq_ref, dgfk_ref, dgcq_ref, dgckv_ref, dgmq_ref, dgmk_ref, dbf_ref, dwuq_ref, dwuk_ref, dwuv_ref)

        @pl.when(pl.program_id(0) == 0)
        def _():
            for a in accs:
                a[...] = jnp.zeros_like(a)

        for (c0, g_ref, d_ref, dg_ref) in ((C_FQ, gfq_ref, dfq_ref, dgfq_ref), (C_FK, gfk_ref, dfk_ref, dgfk_ref)):
            dg = jnp.zeros((1, 128), F32)
            for blk in range(4):
                x = p_ref[:, c0 + 128 * blk:c0 + 128 * (blk + 1)]
                r = lax.rsqrt(_half_sum(x * x) * (1.0 / 64) + EPS)
                xh = x * r
                dy = d_ref[:, 128 * blk:128 * (blk + 1)]
                dg = dg + _colsum(dy * xh)
                dxh = dy * g_ref[...]
                dx = r * (dxh - xh * (_half_sum(dxh * xh) * (1.0 / 64)))
                dp_ref[:, c0 + 128 * blk:c0 + 128 * (blk + 1)] = dx.astype(BF16)
            dg_ref[...] += dg
        dp_ref[:, C_FV:C_FV + 512] = dfv_ref[...].astype(BF16)

        rcv = rc_ref[...]
        rsv = rs_ref[...]
        cqn, cqh, cqr = _rms(p_ref[:, C_CQ:C_CQ + QR], gcq_ref[...], QR)
        cqb = cqn.astype(BF16)
        qpre = _dot(cqb, wuq_ref[...])
        dgq = jnp.zeros((1, 128), F32)
        for hh in range(NH):
            sl = slice(128 * hh, 128 * (hh + 1))
            _, xh, r = _rms(qpre[:, sl], gmq_ref[...], MLA_QK)
            dout = dqm_ref[:, sl]
            dqn = dout * rcv + _rope_swap(dout * rsv)
            dgq = dgq + _colsum(dqn * xh)
            dqpre_sc[:, sl] = _rms_bwd(dqn, xh, r, gmq_ref[...], MLA_QK).astype(BF16)
        dgmq_ref[...] += dgq
        dqpre = dqpre_sc[...]
        dwuq_ref[...] += _dot(cqb, dqpre, TN)
        dcqn = _dot(dqpre, wuq_ref[...], NT)
        dgcq_ref[...] += _colsum(dcqn * cqh)
        dp_ref[:, C_CQ:C_CQ + QR] = _rms_bwd(dcqn, cqh, cqr, gcq_ref[...], QR).astype(BF16)

        ckvn, ckvh, ckvr = _rms(p_ref[:, C_CKV:C_CKV + KVR], gckv_ref[...], KVR)
        ckvb = ckvn.astype(BF16)
        kpre = _dot(ckvb, wuk_ref[...])
        misc = p_ref[:, C_MISC:C_MISC + 128]
        ln = _lane(misc.shape)
        is_kr = jnp.logical_and(ln >= L_KR, ln < L_KR + ROPE)
        kr = jnp.where(is_kr, misc, 0.0)
        dgk = jnp.zeros((1, 128), F32)
        dkr = jnp.zeros(misc.shape, F32)
        for hh in range(NH):
            sl = slice(128 * hh, 128 * (hh + 1))
            _, xh, r = _rms(kpre[:, sl] + kr, gmk_ref[...], MLA_QK)
            dout = dkm_ref[:, sl]
            dkn = dout * rcv + _rope_swap(dout * rsv)
            dgk = dgk + _colsum(dkn * xh)
            dkx = _rms_bwd(dkn, xh, r, gmk_ref[...], MLA_QK)
            dkr = dkr + jnp.where(is_kr, dkx, 0.0)
            dkpre_sc[:, sl] = jnp.where(ln < 64, dkx, 0.0).astype(BF16)
        dgmk_ref[...] += dgk
        dkpre = dkpre_sc[...]
        dvmb = dvm_ref[...].astype(BF16)
        dwuk_ref[...] += _dot(ckvb, dkpre, TN)
        dwuv_ref[...] += _dot(ckvb, dvmb, TN)
        dckvn = _dot(dkpre, wuk_ref[...], NT) + _dot(dvmb, wuv_ref[...], NT)
        dgckv_ref[...] += _colsum(dckvn * ckvh)
        dp_ref[:, C_CKV:C_CKV + KVR] = _rms_bwd(dckvn, ckvh, ckvr, gckv_ref[...], KVR).astype(BF16)

        z = misc + bf_ref[...]
        dz = dlf_ref[...] * (1.0 - jax.nn.sigmoid(z))
        dbf_ref[...] += _colsum(dz)
        dp_ref[:, C_MISC:C_MISC + 128] = (dkr + dz).astype(BF16)

    row = lambda w: pl.BlockSpec((tm, w), lambda i: (i, 0))
    full = lambda a: pl.BlockSpec(a.shape, lambda i: (0, 0))
    small = (gfq, gfk, gcq, gckv, gmq, gmk, bfv, wuq, wuk, wuv)
    acc_shapes = [(1, 128), (1, 128), (1, QR), (1, KVR), (1, 128), (1, 128), (1, 128),
                  (QR, 1024), (KVR, 1024), (KVR, 512)]
    return _call(
        body, name=name, grid=(T // tm,),
        in_specs=[row(PROJW // 2), row(128), row(128)] + [full(a) for a in small]
                 + [row(512), row(512), row(512), row(1024), row(1024), row(512), row(128)],
        out_specs=[row(PROJW // 2)] + [pl.BlockSpec(s, lambda i: (0, 0)) for s in acc_shapes],
        out_shape=[jax.ShapeDtypeStruct((T, PROJW // 2), BF16)] + [jax.ShapeDtypeStruct(s, F32) for s in acc_shapes],
        scratch_shapes=[pltpu.VMEM((tm, 1024), BF16), pltpu.VMEM((tm, 1024), BF16)],
        compiler_params=_cp(),
    )(proj, rc, rs, *small, dfq, dfk, dfv, dqm, dkm, dvm, dlf)


def _scan_lanes(x, reverse):
    n = x.shape[-1]
    ln = _lane(x.shape)
    k = 1
    while k < n:
        if reverse:
            x = x + jnp.where(ln < n - k, pltpu.roll(x, n - k, x.ndim - 1), 0.0)
        else:
            x = x + jnp.where(ln >= k, pltpu.roll(x, k, x.ndim - 1), 0.0)
        k *= 2
    return x


def _forget_scan(lf, reverse, name):
    def body(x_ref, o_ref):
        x = x_ref[...]
        ln = _lane(x.shape)
        pad = jnp.logical_and(ln >= NMETA, ln < MPAD)
        o_ref[...] = jnp.where(pad, 0.0, _scan_lanes(jnp.where(pad, 0.0, x), reverse))

    return _call(body, name=name, out_shape=jax.ShapeDtypeStruct(lf.shape, F32), compiler_params=_cp())(lf)


def _attn_blocks(LP, tq):
    return [(0, MPAD, MPAD)] + [(MPAD + i * tq, tq, MPAD + (i + 1) * tq) for i in range((LP - MPAD) // tq)]


def _attn_scores(q_ref, k_ref, e, r0, rn, kend, wide, scale, bias):
    if wide:
        qe = q_ref[r0:r0 + rn, 128 * e:128 * (e + 1)]
        ke = k_ref[0:kend, 128 * e:128 * (e + 1)]
    else:
        qb = q_ref[r0:r0 + rn, :]
        mine = (_lane(qb.shape) < 64) if e == 0 else (_lane(qb.shape) >= 64)
        qe = jnp.where(mine, qb, jnp.zeros_like(qb))
        ke = k_ref[0:kend, :]
    s = _dot(qe, ke, NT) * scale
    if bias is not None:
        ct_ref, cr_ref = bias
        s = s + ct_ref[0, r0:r0 + rn, e:e + 1] - cr_ref[0, :, 0:kend]
    neg = -1e30
    if r0 == 0:
        qi = lax.broadcasted_iota(jnp.int32, (rn, kend), 0)
        ki = lax.broadcasted_iota(jnp.int32, (rn, kend), 1)
        s = jnp.where(jnp.logical_and(ki <= qi, ki < NMETA), s, neg)
    else:
        d0 = kend - rn
        head = jnp.where(_lane((rn, MPAD)) < NMETA, s[:, :MPAD], neg)
        qi = lax.broadcasted_iota(jnp.int32, (rn, rn), 0)
        diag = jnp.where(_lane((rn, rn)) <= qi, s[:, d0:], neg)
        s = jnp.concatenate([head] + ([s[:, MPAD:d0]] if d0 > MPAD else []) + [diag], axis=1)
    m = jnp.max(s, axis=-1, keepdims=True)
    p = jnp.exp(s - m)
    l = jnp.sum(p, axis=-1, keepdims=True)
    return qe, ke, p, l


def _attn_specs(B, LP, wide, has_bias):
    qw = 256 if wide else 128
    specs = [pl.BlockSpec((LP, qw), lambda b, hp: (b, hp)),
             pl.BlockSpec((LP, qw), lambda b, hp: (b, hp)),
             pl.BlockSpec((LP, 128), lambda b, hp: (b, hp))]
    bias_specs = []
    if has_bias:
        bias_specs = [pl.BlockSpec((1, LP, 2), lambda b, hp: (b * 4 + hp, 0, 0)),
                      pl.BlockSpec((1, 1, LP), lambda b, hp: (b * 8 + 2 * hp, 0, 0)),
                      pl.BlockSpec((1, 1, LP), lambda b, hp: (b * 8 + 2 * hp + 1, 0, 0))]
    return qw, specs, bias_specs


def _attn_fwd(q, k, v, bias, B, LP, wide, scale, name, tasks=()):
    T = q.shape[0]
    tq = 256
    blocks = _attn_blocks(LP, tq)
    qw, specs, bias_specs = _attn_specs(B, LP, wide, bias is not None)

    def body(*refs):
        if bias is not None:
            q_ref, k_ref, v_ref, ct_ref, cr0_ref, cr1_ref, o_ref = refs
            crs = (cr0_ref, cr1_ref)
        else:
            q_ref, k_ref, v_ref, o_ref = refs
        for (r0, rn, kend) in blocks:
            outs = []
            for e in (0, 1):
                bs = (ct_ref, crs[e]) if bias is not None else None
                _, _, p, l = _attn_scores(q_ref, k_ref, e, r0, rn, kend, wide, scale, bs)
                outs.append(_dot(p.astype(BF16), v_ref[0:kend, :]) / l)
            o = jnp.where(_lane(outs[0].shape) < 64, outs[0], outs[1])
            o_ref[r0:r0 + rn, :] = o.astype(BF16)

    args = (q, k, v) + ((bias[0], bias[1], bias[1]) if bias is not None else ())
    (out,), touts = _call_tasks(
        body, tasks, name=name, grid=(B, 4),
        in_specs=specs + bias_specs,
        out_specs=[pl.BlockSpec((LP, 128), lambda b, hp: (b, hp))],
        out_shape=[jax.ShapeDtypeStruct((T, 512), BF16)],
        args=args)
    return out, touts


def _attn_bwd(q, k, v, do, bias, B, LP, wide, scale, name, tasks=()):
    T = q.shape[0]
    tq = 256
    blocks = _attn_blocks(LP, tq)
    qw, specs, bias_specs = _attn_specs(B, LP, wide, bias is not None)
    has_bias = bias is not None

    def body(*refs):
        if has_bias:
            (q_ref, k_ref, v_ref, do_ref, ct_ref, cr0_ref, cr1_ref,
             dq_ref, dk_ref, dv_ref, dc0_ref, dc1_ref) = refs
            crs = (cr0_ref, cr1_ref)
            dcs = (dc0_ref, dc1_ref)
            dc0_ref[...] = jnp.zeros_like(dc0_ref)
            dc1_ref[...] = jnp.zeros_like(dc1_ref)
        else:
            q_ref, k_ref, v_ref, do_ref, dq_ref, dk_ref, dv_ref = refs
        dk_ref[...] = jnp.zeros_like(dk_ref)
        dv_ref[...] = jnp.zeros_like(dv_ref)
        for (r0, rn, kend) in blocks:
            dqs = []
            for e in (0, 1):
                bs = (ct_ref, crs[e]) if has_bias else None
                qe, ke, p, l = _attn_scores(q_ref, k_ref, e, r0, rn, kend, wide, scale, bs)
                pn = p * (1.0 / l)
                dob = do_ref[r0:r0 + rn, :]
                mine = (_lane(dob.shape) < 64) if e == 0 else (_lane(dob.shape) >= 64)
                doe = jnp.where(mine, dob, jnp.zeros_like(dob))
                dp = _dot(doe, v_ref[0:kend, :], NT)
                delta = jnp.sum(pn * dp, axis=-1, keepdims=True)
                ds = pn * (dp - delta)
                dsb = ds.astype(BF16)
                dqe = _dot(dsb, ke) * scale
                dke = _dot(dsb, qe, TN) * scale
                if wide:
                    dq_ref[r0:r0 + rn, 128 * e:128 * (e + 1)] = dqe
                    dk_ref[0:kend, 128 * e:128 * (e + 1)] += dke
                else:
                    dqs.append(dqe)
                    dk_ref[0:kend, :] += dke
                dv_ref[0:kend, :] += _dot(pn.astype(BF16), doe, TN)
                if has_bias:
                    dcs[e][0, :, 0:kend] -= _colsum(ds)
            if not wide:
                dq_ref[r0:r0 + rn, :] = jnp.where(_lane(dqs[0].shape) < 64, dqs[0], dqs[1])

    args = (q, k, v, do) + ((bias[0], bias[1], bias[1]) if has_bias else ())
    out_specs = [pl.BlockSpec((LP, qw), lambda b, hp: (b, hp)),
                 pl.BlockSpec((LP, qw), lambda b, hp: (b, hp)),
                 pl.BlockSpec((LP, 128), lambda b, hp: (b, hp))]
    out_shape = [jax.ShapeDtypeStruct(q.shape, F32), jax.ShapeDtypeStruct(q.shape, F32),
                 jax.ShapeDtypeStruct((T, 512), F32)]
    if has_bias:
        out_specs += [pl.BlockSpec((1, 1, LP), lambda b, hp: (b * 4 + hp, 0, 0))] * 2
        out_shape += [jax.ShapeDtypeStruct((B * 4, 1, LP), F32)] * 2
    return _call_tasks(
        body, tasks, name=name, grid=(B, 4),
        in_specs=specs + [pl.BlockSpec((LP, 128), lambda b, hp: (b, hp))] + bias_specs,
        out_specs=out_specs, out_shape=out_shape, args=args)


def _post_fwd(h, of, om, proj, bg, wbf, wbm, wout, name):
    T = h.shape[0]
    tm = _tile(T, (512, 384, 256, 128))

    def body(h_ref, of_ref, om_ref, gl_ref, bg_ref, wbf_ref, wbm_ref, wo_ref, o_ref, mix_ref):
        gate = jax.nn.sigmoid(gl_ref[...] + bg_ref[...])
        mix = gate[:, :D] * _dot(of_ref[...], wbf_ref[...]) + gate[:, D:] * _dot(om_ref[...], wbm_ref[...])
        mb = mix.astype(BF16)
        mix_ref[...] = mb
        o_ref[...] = h_ref[...] + _dot(mb, wo_ref[...])

    row = lambda w: pl.BlockSpec((tm, w), lambda i: (i, 0))
    full = lambda a: pl.BlockSpec(a.shape, lambda i: (0, 0))
    return _call(
        body, name=name, grid=(T // tm,),
        in_specs=[row(D), row(512), row(512), pl.BlockSpec((tm, 2 * D), lambda i: (i, 1)),
                  full(bg), full(wbf), full(wbm), full(wout)],
        out_specs=[row(D), row(D)],
        out_shape=[jax.ShapeDtypeStruct((T, D), F32), jax.ShapeDtypeStruct((T, D), BF16)],
        compiler_params=_cp(),
    )(h, of, om, proj, bg, wbf, wbm, wout)


def _post_bwd(dh, of, om, proj, bg, wbf, wbm, wout, name):
    T = dh.shape[0]
    tm = _tile(T, (512, 384, 256, 128))

    def body(d_ref, of_ref, om_ref, gl_ref, bg_ref, wbf_ref, wbm_ref, wo_ref,
             dgl_ref, dbf_ref, dbm_ref, dof_ref, dom_ref, dbg_ref):
        @pl.when(pl.program_id(0) == 0)
        def _():
            dbg_ref[...] = jnp.zeros_like(dbg_ref)

        gate = jax.nn.sigmoid(gl_ref[...] + bg_ref[...])
        dmix = _dot(d_ref[...].astype(BF16), wo_ref[...], NT)
        ofx = _dot(of_ref[...], wbf_ref[...])
        omx = _dot(om_ref[...], wbm_ref[...])
        gf = gate[:, :D]
        gm = gate[:, D:]
        dof = (dmix * gf).astype(BF16)
        dom = (dmix * gm).astype(BF16)
        dglf = dmix * ofx * gf * (1.0 - gf)
        dglm = dmix * omx * gm * (1.0 - gm)
        dgl_ref[:, :D] = dglf.astype(BF16)
        dgl_ref[:, D:] = dglm.astype(BF16)
        dbg_ref[:, :D] += _colsum(dglf)
        dbg_ref[:, D:] += _colsum(dglm)
        dbf_ref[...] = dof
        dbm_ref[...] = dom
        dof_ref[...] = _dot(dof, wbf_ref[...], NT).astype(BF16)
        dom_ref[...] = _dot(dom, wbm_ref[...], NT).astype(BF16)

    row = lambda w: pl.BlockSpec((tm, w), lambda i: (i, 0))
    full = lambda a: pl.BlockSpec(a.shape, lambda i: (0, 0))
    return _call(
        body, name=name, grid=(T // tm,),
        in_specs=[row(D), row(512), row(512), pl.BlockSpec((tm, 2 * D), lambda i: (i, 1)),
                  full(bg), full(wbf), full(wbm), full(wout)],
        out_specs=[row(2 * D), row(D), row(D), row(512), row(512), pl.BlockSpec((1, 2 * D), lambda i: (0, 0))],
        out_shape=[jax.ShapeDtypeStruct((T, 2 * D), BF16), jax.ShapeDtypeStruct((T, D), BF16),
                   jax.ShapeDtypeStruct((T, D), BF16), jax.ShapeDtypeStruct((T, 512), BF16),
                   jax.ShapeDtypeStruct((T, 512), BF16), jax.ShapeDtypeStruct((1, 2 * D), F32)],
        compiler_params=_cp(),
    )(dh, of, om, proj, bg, wbf, wbm, wout)


def _loss_head(h3, target, B, LP, name):
    S = LP - MPAD
    nb = LP // 128

    def body(h_ref, t_ref, dy_ref, l_ref):
        b = pl.program_id(0)
        p = pl.program_id(1)

        @pl.when(jnp.logical_and(b == 0, p == 0))
        def _():
            l_ref[...] = jnp.zeros_like(l_ref)

        @pl.when(p == 0)
        def _():
            dy_ref[...] = jnp.zeros_like(dy_ref)

        @pl.when(p > 0)
        def _():
            e = h_ref[...] - t_ref[0]
            dy_ref[...] = e * (1.0 / D)
            l_ref[...] += jnp.sum(e * e, axis=0, keepdims=True) * (0.5 / D)

    return _call(
        body, name=name, grid=(B, nb),
        in_specs=[pl.BlockSpec((128, D), lambda b, p: (b * nb + p, 0)),
                  pl.BlockSpec((1, 128, D), lambda b, p: (b, jnp.maximum(p - 1, 0), 0))],
        out_specs=[pl.BlockSpec((128, D), lambda b, p: (b * nb + p, 0)),
                   pl.BlockSpec((1, D), lambda b, p: (0, 0))],
        out_shape=[jax.ShapeDtypeStruct(h3.shape, F32), jax.ShapeDtypeStruct((1, D), F32)],
        compiler_params=_cp(),
    )(h3, target)


def _rope_tables(B, LP):
    pos = jnp.concatenate([jnp.arange(MPAD, dtype=F32), NMETA + jnp.arange(LP - MPAD, dtype=F32)])
    inv_freq = ROPE_THETA ** (-jnp.arange(0, ROPE, 2, dtype=F32) / ROPE)
    ang = pos[:, None] * inv_freq[None, :]
    cos, sin = jnp.cos(ang), jnp.sin(ang)
    z32 = jnp.zeros((LP, 32), F32)
    rc = jnp.concatenate([jnp.ones((LP, 64), F32), cos, cos, z32], axis=1)
    rs = jnp.concatenate([jnp.zeros((LP, 64), F32), -sin, sin, z32], axis=1)
    return jnp.tile(rc, (B, 1)), jnp.tile(rs, (B, 1))


def _pad_lanes(v, start, width=128):
    n = v.shape[1]
    return jnp.concatenate([jnp.zeros((1, start), F32), v, jnp.zeros((1, width - start - n), F32)], axis=1)


G_FFN1 = ["ffn1_w_gu", "ffn1_w_down"]
G_MIX = ["w_in", "mla_w_uq", "mla_w_ukv", "w_branch_fox", "w_branch_mla", "w_out"]
G_OUT = ["w_out", "w_branch_fox", "w_branch_mla"]
G_IN = ["w_in", "mla_w_uq", "mla_w_ukv"]


def _step(x, target, meta, vec, gath, shards):
    dist = shards is not None
    B, S, _ = x.shape
    LP = MPAD + S
    T = B * LP
    gath = dict(gath)

    def gather(names):
        return [_gather_task([shards[n] for n in names])] if dist else []

    def gathered(names, touts):
        if dist:
            gath.update(zip(names, touts[0]))

    g4, sums, red = {}, {}, {}

    def scatter(names):
        return [_a2a_task([_pieces(g4[n]) for n in names])] if dist else []

    def scattered(names, tout, me):
        for n, r in zip(names, tout):
            sums[n] = _sum_pieces(r, _pieces(g4[n]), me, "rs_sum_" + n)

    def join(names):
        return [_join_task([sums[n] for n in names])] if dist else []

    def joined(names, tout):
        for n, r in zip(names, tout):
            red[n] = (sums[n], r)

    me = None
    if dist:
        me = (4 * lax.axis_index("x") + 2 * lax.axis_index("y") + lax.axis_index("c")).reshape(1).astype(jnp.int32)

    h0 = jnp.concatenate([jnp.broadcast_to(meta[None], (B, NMETA, D)),
                          jnp.zeros((B, MPAD - NMETA, D), F32), x], axis=1).reshape(T, D)
    rc, rs = _rope_tables(B, LP)
    gfq = jnp.tile(vec["fox_q_norm"], (1, 2))
    gfk = jnp.tile(vec["fox_k_norm"], (1, 2))
    gmq = _pad_lanes(vec["mla_q_norm"], 0)
    gmk = _pad_lanes(vec["mla_k_norm"], 0)
    bfv = _pad_lanes(vec["b_forget"], L_FL)

    w1gu, w1d = _cols_from_shards(gath["ffn1_w_gu"]), gath["ffn1_w_down"].reshape(DFF, D)
    h1, touts = _ffn_fwd(h0, vec["ffn1_norm"], w1gu, w1d, "ffn1_fwd", gather(G_MIX))
    gathered(G_MIX, touts)
    wm = _mixer_weights(gath)
    small = (gfq, gfk, vec["mla_cq_norm"], vec["mla_ckv_norm"], gmq, gmk, bfv, wm["wuq"], wm["wuk"], wm["wuv"])
    proj, u2 = _inproj_fwd(h1, vec["mix_norm"], wm["w_in"], "inproj_fwd")
    fq, fk, fv, qm, km, vm, lf = _prep_fwd(proj, rc, rs, *small, name="prep_fwd")
    lf_rows = lf[:, L_FL:L_FL + NH].reshape(B, LP, NH).transpose(0, 2, 1).reshape(B * NH, LP)
    crow = _forget_scan(lf_rows, False, "forget_scan")
    ctok = crow.reshape(B, 4, 2, LP).transpose(0, 1, 3, 2).reshape(B * 4, LP, 2)
    bias = (ctok, crow.reshape(B * NH, 1, LP))
    of, touts = _attn_fwd(fq, fk, fv, bias, B, LP, False, 64 ** -0.5, "fox_fwd", gather(["ffn2_w_gu"]))
    gathered(["ffn2_w_gu"], touts)
    om, touts = _attn_fwd(qm, km, vm, None, B, LP, True, MLA_QK ** -0.5, "mla_fwd", gather(["ffn2_w_down"]))
    gathered(["ffn2_w_down"], touts)
    h2, mix = _post_fwd(h1, of, om, proj, vec["b_gate"], wm["wbf"], wm["wbm"], wm["w_out"], "post_fwd")
    w2gu, w2d = _cols_from_shards(gath["ffn2_w_gu"]), gath["ffn2_w_down"].reshape(DFF, D)
    h3, _ = _ffn_fwd(h2, vec["ffn2_norm"], w2gu, w2d, "ffn2_fwd")
    dy, lpart = _loss_head(h3, target, B, LP, "loss_head")

    gv = {}
    (dh2, u3, a2, dgp2, gv["ffn2_norm"]), _ = _ffn_bwd(h2, dy, vec["ffn2_norm"], w2gu, w2d, "ffn2_bwd")
    g4["ffn2_w_gu"] = _wgrad(u3, dgp2, "ffn2_dwgu", bn=FH, shard_major=True)[0]
    g4["ffn2_w_down"] = _wgrad(a2, dy, "ffn2_dwd", scale=0.5, bk=FH)[0].reshape(N_CHIPS, DFF // N_CHIPS, D)

    dgl, dbf, dbm, dof, dom, gv["b_gate"] = _post_bwd(dh2, of, om, proj, vec["b_gate"], wm["wbf"], wm["wbm"], wm["w_out"], "post_bwd")
    g4["w_out"] = _wgrad(mix, dh2, "dw_out")[0].reshape(N_CHIPS, D // N_CHIPS, D)
    g4["w_branch_fox"] = _cols_to_shards(_wgrad(of, dbf, "dw_bf")[0])
    g4["w_branch_mla"] = _cols_to_shards(_wgrad(om, dbm, "dw_bm")[0])
    G_FFN2 = ["ffn2_w_gu", "ffn2_w_down"]
    (dfq, dfk, dfv, dc0, dc1), touts = _attn_bwd(fq, fk, fv, dof, bias, B, LP, False, 64 ** -0.5, "fox_bwd", scatter(G_FFN2))
    if dist:
        scattered(G_FFN2, touts[0], me)
    (dqm, dkm, dvm), touts = _attn_bwd(qm, km, vm, dom, None, B, LP, True, MLA_QK ** -0.5, "mla_bwd",
                                       scatter(G_OUT) + join(G_FFN2))
    if dist:
        scattered(G_OUT, touts[0], me)
        joined(G_FFN2, touts[1])
    dc = jnp.concatenate([dc0, dc1], axis=1).reshape(B * NH, LP)
    dlf_rows = _forget_scan(dc, True, "forget_scan_bwd")
    dlf = dlf_rows.reshape(B, NH, LP).transpose(0, 2, 1).reshape(T, NH)
    dlf = jnp.concatenate([jnp.zeros((T, L_FL), F32), dlf, jnp.zeros((T, 128 - L_FL - NH), F32)], axis=1)
    (dlo, dgfq, dgfk, gv["mla_cq_norm"], gv["mla_ckv_norm"], dgmq, dgmk, dbfv,
     dwuq, dwuk, dwuv) = _prep_bwd(proj, rc, rs, *small, dfq, dfk, dfv, dqm, dkm, dvm, dlf, name="prep_bwd")
    gv["fox_q_norm"] = dgfq[:, :64] + dgfq[:, 64:]
    gv["fox_k_norm"] = dgfk[:, :64] + dgfk[:, 64:]
    gv["mla_q_norm"] = dgmq[:, :MLA_QK]
    gv["mla_k_norm"] = dgmk[:, :MLA_QK]
    gv["b_forget"] = dbfv[:, L_FL:L_FL + NH]
    dwin = jnp.concatenate([_wgrad(u2, dlo, "dw_in_lo")[0], _wgrad(u2, dgl, "dw_in_hi")[0]], axis=1)
    g4["w_in"] = _cols_to_shards(_win_from_kernel(dwin))
    g4["mla_w_uq"] = _cols_to_shards(
        dwuq.astype(GRAD_DTYPE).reshape(QR, NH, 128)[:, :, :MLA_QK].reshape(QR, NH * MLA_QK))
    dukv = jnp.concatenate([dwuk.reshape(KVR, NH, 128)[:, :, :64], dwuv.reshape(KVR, NH, 64)], axis=2)
    g4["mla_w_ukv"] = _cols_to_shards(dukv.astype(GRAD_DTYPE).reshape(KVR, NH * 128))
    dh1, gv["mix_norm"] = _inproj_bwd(h1, dh2, dlo, dgl, vec["mix_norm"], wm["w_in"], "inproj_bwd")

    (dh0, u1, a1, dgp1, gv["ffn1_norm"]), touts = _ffn_bwd(h0, dh1, vec["ffn1_norm"], w1gu, w1d, "ffn1_bwd",
                                                            scatter(G_IN) + join(G_OUT))
    if dist:
        scattered(G_IN, touts[0], me)
        joined(G_OUT, touts[1])
    dh0 = dh0.reshape(B, LP, D)
    grad_x = dh0[:, MPAD:]
    grad_meta = jnp.sum(dh0[:, :NMETA], axis=0)
    share = [_share_task([_stack_vectors([gv[n] for n in VECS]), grad_meta, lpart])] if dist else []
    g4["ffn1_w_gu"], touts = _wgrad(u1, dgp1, "ffn1_dwgu", bn=FH, shard_major=True, tasks=share)
    shared = touts[0] if dist else None
    dwd1, touts = _wgrad(a1, dh1, "ffn1_dwd", scale=0.5, bk=FH, tasks=scatter(["ffn1_w_gu"]) + join(G_IN))
    g4["ffn1_w_down"] = dwd1.reshape(N_CHIPS, DFF // N_CHIPS, D)
    if dist:
        scattered(["ffn1_w_gu"], touts[0], me)
        joined(G_IN, touts[1])
        scattered(["ffn1_w_down"], _run_tasks(scatter(["ffn1_w_down"]), "rs_ffn1_w_down")[0], me)
        joined(G_FFN1, _run_tasks(join(G_FFN1), "rs_join_ffn1")[0])
    return lpart, grad_x, grad_meta, gv, (red if dist else g4), shared


def _cols_from_shards(g4):
    n, r, c = g4.shape
    return g4.transpose(1, 0, 2).reshape(r, n * c)


def _cols_to_shards(full):
    r, c4 = full.shape
    return full.reshape(r, N_CHIPS, c4 // N_CHIPS).transpose(1, 0, 2)


def _win_to_kernel(wfull):
    z = lambda n: jnp.zeros((D, n), wfull.dtype)
    fl, cq, ckv, kr, gate = (wfull[:, 1536:1544], wfull[:, 1544:1800], wfull[:, 1800:1928],
                             wfull[:, 1928:1960], wfull[:, 1960:4008])
    misc = jnp.concatenate([z(L_KR), kr, fl, z(128 - L_FL - NH)], axis=1)
    return jnp.concatenate([wfull[:, :1536], cq, ckv, misc, gate], axis=1)


def _win_from_kernel(gk):
    m = C_MISC
    return jnp.concatenate([gk[:, :1536], gk[:, m + L_FL:m + L_FL + NH], gk[:, C_CQ:C_CQ + QR],
                            gk[:, C_CKV:C_CKV + KVR], gk[:, m + L_KR:m + L_KR + ROPE], gk[:, C_GATE:]], axis=1)


def _pieces(g4):
    n, r, c = g4.shape
    return g4.reshape(2 * n, r // 2, c)


def _mixer_weights(gath):
    w = {}
    w["w_in"] = _win_to_kernel(_cols_from_shards(gath["w_in"]))
    uq = _cols_from_shards(gath["mla_w_uq"]).reshape(QR, NH, MLA_QK)
    w["wuq"] = jnp.pad(uq, ((0, 0), (0, 0), (0, 128 - MLA_QK))).reshape(QR, NH * 128)
    ukv = _cols_from_shards(gath["mla_w_ukv"]).reshape(KVR, NH, 128)
    w["wuk"] = jnp.pad(ukv[:, :, :64], ((0, 0), (0, 0), (0, 64))).reshape(KVR, NH * 128)
    w["wuv"] = ukv[:, :, 64:].reshape(KVR, NH * 64)
    w["wbf"] = _cols_from_shards(gath["w_branch_fox"])
    w["wbm"] = _cols_from_shards(gath["w_branch_mla"])
    w["w_out"] = gath["w_out"].reshape(D, D)
    return w


def _chip_peers(x, y):
    return [(1 - x, y), (x, 1 - y), (1 - x, 1 - y)]


RELS = [(dx, dy, dc) for dx in (0, 1) for dy in (0, 1) for dc in (0, 1)][1:]


def _here():
    return lax.axis_index("x"), lax.axis_index("y"), lax.axis_index("c")


def _flip(a, d):
    return (1 - a) if d else a


def _remote(src, dst, send, recv, i, dev):
    return functools.partial(pltpu.make_async_remote_copy, src_ref=src, dst_ref=dst, send_sem=send.at[i],
                             recv_sem=recv.at[i], device_id=dev, device_id_type=MESH)


def _gather_task(shards):
    n = len(shards)

    def descs(ins, outs, sems):
        send, recv, loc = sems
        x, y, c = _here()
        j = 2 * x + y
        locs, pairs = [], []
        for k in range(n):
            locs.append(functools.partial(pltpu.make_async_copy, ins[k], outs[k].at[j], loc.at[k]))
            for r, (px, py) in enumerate(_chip_peers(x, y)):
                dev = (px, py, c)
                pairs.append((_remote(ins[k], outs[k].at[j], send, recv, 3 * k + r, dev),
                              _remote(ins[k], outs[k].at[2 * px + py], send, recv, 3 * k + r, dev)))
        return locs, pairs

    return _Task(shards, [jax.ShapeDtypeStruct((N_CHIPS,) + s.shape, s.dtype) for s in shards],
                 [pltpu.SemaphoreType.DMA((3 * n,)), pltpu.SemaphoreType.DMA((3 * n,)), pltpu.SemaphoreType.DMA((n,))],
                 descs)


class _SplitGather(_Task):
    PARTS = 2

    def __init__(self, shards):
        n = 3 * len(shards) * self.PARTS
        dma = pltpu.SemaphoreType.DMA
        super().__init__(shards, [jax.ShapeDtypeStruct((N_CHIPS,) + s.shape, s.dtype) for s in shards],
                         [dma((n,)), dma((n,)), dma((n,)), dma((n,)), dma((len(shards),))], None)

    def _plan(self, ins, outs, sems):
        send, recv, fsend, frecv, loc = sems
        x, y, c = _here()
        j = 2 * x + y
        locs, first, passed = [], [], []
        for k in range(len(ins)):
            h = self.ins[k].shape[0] // 2
            parts = self.PARTS if h % (32 * self.PARTS) == 0 else 1
            hp = h // parts
            locs.append(functools.partial(pltpu.make_async_copy, ins[k], outs[k].at[j], loc.at[k]))
            for r, (px, py) in enumerate(_chip_peers(x, y)):
                p = 2 * px + py
                for q in range(parts):
                    i = (3 * k + r) * self.PARTS + q
                    mine = pl.ds(pl.multiple_of(c * h + q * hp, 8), hp)
                    theirs = pl.ds(pl.multiple_of((1 - c) * h + q * hp, 8), hp)
                    first.append((_remote(ins[k].at[mine], outs[k].at[j, mine], send, recv, i, (px, py, c)),
                                  _remote(ins[k].at[mine], outs[k].at[p, mine], send, recv, i, (px, py, c))))
                    passed.append((_remote(outs[k].at[p, mine], outs[k].at[p, mine], fsend, frecv, i, (x, y, 1 - c)),
                                   _remote(outs[k].at[p, mine], outs[k].at[p, theirs], fsend, frecv, i, (x, y, 1 - c))))
        return locs, first, passed

    def start(self, ins, outs, sems):
        locs, first, _ = self._plan(ins, outs, sems)
        for lc in locs:
            lc().start()
        for snd, _ in first:
            snd().start()

    def wait(self, ins, outs, sems):
        locs, first, passed = self._plan(ins, outs, sems)
        for (_, landed), (pass_on, _) in zip(first, passed):
            landed().wait_recv()
            pass_on().start()
        for _, rcv in passed:
            rcv().wait_recv()
        for snd, _ in first + passed:
            snd().wait_send()
        for lc in locs:
            lc().wait()


def _a2a_task(ps):
    n = len(ps)
    nr = len(RELS)

    def descs(ins, outs, sems):
        send, recv = sems
        x, y, c = _here()
        me = 4 * x + 2 * y + c
        pairs = []
        for k in range(n):
            for i, (dx, dy, dc) in enumerate(RELS):
                dev = (_flip(x, dx), _flip(y, dy), _flip(c, dc))
                peer = 4 * dev[0] + 2 * dev[1] + dev[2]
                pairs.append((_remote(ins[k].at[peer], outs[k].at[me], send, recv, nr * k + i, dev),
                              _remote(ins[k].at[peer], outs[k].at[peer], send, recv, nr * k + i, dev)))
        return [], pairs

    return _Task(ps, [jax.ShapeDtypeStruct(p.shape, p.dtype) for p in ps],
                 [pltpu.SemaphoreType.DMA((nr * n,)), pltpu.SemaphoreType.DMA((nr * n,))], descs)


def _share_task(vs):
    n = len(vs)
    nr = len(RELS)

    def descs(ins, outs, sems):
        send, recv, loc = sems
        x, y, c = _here()
        me = 4 * x + 2 * y + c
        locs, pairs = [], []
        for k in range(n):
            locs.append(functools.partial(pltpu.make_async_copy, ins[k], outs[k].at[me], loc.at[k]))
            for i, (dx, dy, dc) in enumerate(RELS):
                dev = (_flip(x, dx), _flip(y, dy), _flip(c, dc))
                peer = 4 * dev[0] + 2 * dev[1] + dev[2]
                pairs.append((_remote(ins[k], outs[k].at[me], send, recv, nr * k + i, dev),
                              _remote(ins[k], outs[k].at[peer], send, recv, nr * k + i, dev)))
        return locs, pairs

    return _Task(vs, [jax.ShapeDtypeStruct((N_DEV,) + v.shape, v.dtype) for v in vs],
                 [pltpu.SemaphoreType.DMA((nr * n,)), pltpu.SemaphoreType.DMA((nr * n,)), pltpu.SemaphoreType.DMA((n,))],
                 descs)


def _join_task(ss):
    n = len(ss)

    def descs(ins, outs, sems):
        send, recv = sems
        x, y, c = _here()
        pairs = []
        for k in range(n):
            cp = _remote(ins[k], outs[k], send, recv, k, (x, y, 1 - c))
            pairs.append((cp, cp))
        return [], pairs

    return _Task(ss, [jax.ShapeDtypeStruct(s.shape, s.dtype) for s in ss],
                 [pltpu.SemaphoreType.DMA((n,)), pltpu.SemaphoreType.DMA((n,))], descs)


def _sum_pieces(recv, own, me, name):
    n, h, c = recv.shape
    tr = h

    def body(me_ref, r_ref, o_ref, out_ref):
        s = pl.program_id(1)
        val = jnp.where(s == me_ref[0], o_ref[0], r_ref[0]).astype(F32)

        @pl.when(s == 0)
        def _():
            out_ref[...] = val

        @pl.when(s > 0)
        def _():
            out_ref[...] += val

    def other(s, m):
        return jnp.where(s == m[0], (s + 1) % n, s)

    return _call(
        body, name=name,
        grid_spec=pltpu.PrefetchScalarGridSpec(
            num_scalar_prefetch=1, grid=(h // tr, n),
            in_specs=[pl.BlockSpec((1, tr, c), lambda i, s, m: (other(s, m), i, 0)),
                      pl.BlockSpec((1, tr, c), lambda i, s, m: (m[0], i, 0))],
            out_specs=pl.BlockSpec((tr, c), lambda i, s, m: (i, 0))),
        out_shape=jax.ShapeDtypeStruct((h, c), F32),
        compiler_params=_cp(),
    )(me, recv, own)


def _adamw_update(gg, w, m, v):
    c1 = 1.0 / (1.0 - ADAM_B1 ** ADAM_STEP)
    c2 = 1.0 / (1.0 - ADAM_B2 ** ADAM_STEP)
    nm = ADAM_B1 * m + (1.0 - ADAM_B1) * gg
    nv = ADAM_B2 * v + (1.0 - ADAM_B2) * (gg * gg)
    return -ADAM_LR * ((nm * c1) / (jnp.sqrt(nv * c2) + ADAM_EPS) + ADAM_WD * w), nm, nv


def _adamw_small(gvec8, gmeta8, lp8, chip, ws, ms, vs, name):
    na = len(ws)

    def dev_sum(ref):
        acc = ref[0]
        for s in range(1, N_DEV):
            acc = acc + ref[s]
        return acc

    def body(c_ref, gv_ref, gm_ref, lp_ref, *refs):
        w_refs, m_refs, v_refs = refs[:na], refs[na:2 * na], refs[2 * na:3 * na]
        l_ref = refs[3 * na]
        outs = refs[3 * na + 1:]
        g_refs, d_refs, nm_refs, nv_refs = outs[:na], outs[na:2 * na], outs[2 * na:3 * na], outs[3 * na:]
        l_ref[...] = dev_sum(lp_ref)
        gvec = dev_sum(gv_ref)
        for k in range(na):
            gg = gvec[k:k + 1, 0:ws[k].shape[1]] if k < na - 1 else dev_sum(gm_ref)
            g_refs[k][...] = gg
            d_refs[k][...], nm_refs[k][...], nv_refs[k][...] = _adamw_update(gg, w_refs[k][...], m_refs[k][...], v_refs[k][...])

    whole = lambda a: pl.BlockSpec(a.shape, lambda i, c: (0,) * a.ndim)
    arrs = list(ws) + list(ms) + list(vs)
    res = _call(
        body, name=name,
        grid_spec=pltpu.PrefetchScalarGridSpec(
            num_scalar_prefetch=1, grid=(1,),
            in_specs=[whole(gvec8), pl.BlockSpec((N_DEV, NMETA, D // N_CHIPS), lambda i, c: (0, 0, c[0])), whole(lp8)]
                     + [whole(a) for a in arrs],
            out_specs=[pl.BlockSpec((1, D), lambda i, c: (0, 0))] + [whole(a) for a in ws] * 4),
        out_shape=[jax.ShapeDtypeStruct((1, D), F32)] + [jax.ShapeDtypeStruct(a.shape, F32) for a in ws] * 4,
        compiler_params=_cp(),
    )(chip, gvec8, gmeta8, lp8, *arrs)
    return res[0], [list(res[1 + i * na:1 + (i + 1) * na]) for i in range(4)]


def _adamw_halves(wt, mine, theirs, m, v, core, name):
    r, c = wt.shape
    h = r // 2
    tr = _tile(h, (256, 176, 128, 64))
    nh = h // tr

    def body(c_ref, w_ref, a_ref, b_ref, m_ref, v_ref, g_ref, d_ref, nm_ref, nv_ref):
        gg = jnp.where(pl.program_id(0) // nh == c_ref[0], a_ref[...], b_ref[...])
        g_ref[...] = gg
        d_ref[...], nm_ref[...], nv_ref[...] = _adamw_update(gg, w_ref[...], m_ref[...], v_ref[...])

    full = pl.BlockSpec((tr, c), lambda i, cr: (i, 0))
    half = pl.BlockSpec((tr, c), lambda i, cr: (i % nh, 0))
    return _call(
        body, name=name,
        grid_spec=pltpu.PrefetchScalarGridSpec(
            num_scalar_prefetch=1, grid=(2 * nh,),
            in_specs=[full, half, half, full, full], out_specs=[full] * 4),
        out_shape=[jax.ShapeDtypeStruct((r, c), F32)] * 4,
        compiler_params=_cp(),
    )(core, wt, mine, theirs, m, v)


MATS = ["ffn1_w_gu", "ffn1_w_down", "w_in", "mla_w_uq", "mla_w_ukv", "w_branch_fox", "w_branch_mla",
        "w_out", "ffn2_w_gu", "ffn2_w_down"]
VECS = ["ffn1_norm", "mix_norm", "b_forget", "b_gate", "fox_q_norm", "fox_k_norm", "mla_cq_norm",
        "mla_ckv_norm", "mla_q_norm", "mla_k_norm", "ffn2_norm"]
WEIGHTS = ["meta_tokens", "ffn1_norm", "ffn1_w_gu", "ffn1_w_down", "mix_norm", "w_in", "b_forget", "b_gate",
           "fox_q_norm", "fox_k_norm", "mla_cq_norm", "mla_w_uq", "mla_ckv_norm", "mla_w_ukv", "mla_q_norm",
           "mla_k_norm", "w_branch_fox", "w_branch_mla", "w_out", "ffn2_norm", "ffn2_w_gu", "ffn2_w_down"]


VEC_LANES = 2048


def _stack_vectors(parts):
    rows = [_pad_lanes(p, 0, VEC_LANES) for p in parts]
    rows.append(jnp.zeros((-len(parts) % 8, VEC_LANES), F32))
    return jnp.concatenate(rows, axis=0)


def kernel(x, meta_tokens, ffn1_norm, ffn1_w_gu, ffn1_w_down, mix_norm, w_in, b_forget, b_gate, fox_q_norm, fox_k_norm, mla_cq_norm, mla_w_uq, mla_ckv_norm, mla_w_ukv, mla_q_norm, mla_k_norm, w_branch_fox, w_branch_mla, w_out, ffn2_norm, ffn2_w_gu, ffn2_w_down, loss_target, m_meta_tokens, m_ffn1_norm, m_ffn1_w_gu, m_ffn1_w_down, m_mix_norm, m_w_in, m_b_forget, m_b_gate, m_fox_q_norm, m_fox_k_norm, m_mla_cq_norm, m_mla_w_uq, m_mla_ckv_norm, m_mla_w_ukv, m_mla_q_norm, m_mla_k_norm, m_w_branch_fox, m_w_branch_mla, m_w_out, m_ffn2_norm, m_ffn2_w_gu, m_ffn2_w_down, v_meta_tokens, v_ffn1_norm, v_ffn1_w_gu, v_ffn1_w_down, v_mix_norm, v_w_in, v_b_forget, v_b_gate, v_fox_q_norm, v_fox_k_norm, v_mla_cq_norm, v_mla_w_uq, v_mla_ckv_norm, v_mla_w_ukv, v_mla_q_norm, v_mla_k_norm, v_w_branch_fox, v_w_branch_mla, v_w_out, v_ffn2_norm, v_ffn2_w_gu, v_ffn2_w_down):
    a = dict(locals())
    wts = {n: a[n] for n in WEIGHTS}
    ms = {n: a["m_" + n] for n in WEIGHTS}
    vs = {n: a["v_" + n] for n in WEIGHTS}
    cx, cy, cc = lax.axis_index("x"), lax.axis_index("y"), lax.axis_index("c")
    chip = 2 * cx + cy

    shards = {n: wts[n][0].astype(BF16) for n in MATS}
    first = _run_tasks([_SplitGather([shards[n] for n in G_FFN1] + [meta_tokens])], "gather_ffn1")[0]
    gath = dict(zip(G_FFN1, first[:-1]))
    meta_full = _cols_from_shards(first[-1])

    _, grad_x, _, _, gred, (gvec8, gmeta8, lp8) = _step(x, loss_target, meta_full, {n: wts[n] for n in VECS}, gath, shards)

    sm_names = VECS + ["meta_tokens"]
    lsum, sm = _adamw_small(gvec8, gmeta8, lp8, chip.reshape(1).astype(jnp.int32), [wts[n] for n in sm_names],
                            [ms[n] for n in sm_names], [vs[n] for n in sm_names], "adamw_small")
    loss = jnp.sum(lsum)

    grads, delta, new_m, new_v = {}, {}, {}, {}
    core = cc.reshape(1).astype(jnp.int32)
    for n in MATS:
        shp = wts[n].shape
        mine, theirs = gred[n]
        res = _adamw_halves(wts[n][0], mine, theirs, ms[n][0], vs[n][0], core, "adamw_" + n)
        grads[n], delta[n], new_m[n], new_v[n] = (t.reshape(shp) for t in res)
    for k, n in enumerate(sm_names):
        grads[n], delta[n], new_m[n], new_v[n] = (sm[i][k] for i in range(4))

    return (loss, grad_x, *[grads[n] for n in WEIGHTS], *[delta[n] for n in WEIGHTS],
            *[new_m[n] for n in WEIGHTS], *[new_v[n] for n in WEIGHTS])
```

```python
import functools

import jax
import jax.numpy as jnp
from jax import lax
from jax.experimental import pallas as pl
from jax.experimental.pallas import tpu as pltpu

F32 = jnp.float32
BF16 = jnp.bfloat16
MESH = pl.DeviceIdType.MESH

D = 1024
DFF = 2816
FH = DFF // 2
NMETA = 16
MPAD = 128
EPS = 1e-6
NH = 8
FOXW = 512
QR = 256
KVR = 128
ROPE = 32
MLA_QK = 96
PROJW = 4096
ROPE_THETA = 10000.0
N_CHIPS = 4
N_DEV = 8

ADAM_LR = 0.001
ADAM_B1 = 0.9
ADAM_B2 = 0.999
ADAM_EPS = 1e-08
ADAM_WD = 0.01
ADAM_STEP = 10

VMEM_LIMIT = 56 * 2**20
GRAD_DTYPE = BF16

NT = (((1,), (1,)), ((), ()))
TN = (((0,), (0,)), ((), ()))


def _call(body, **kw):
    return pl.pallas_call(body, **kw)


def _cp(**kw):
    return pltpu.CompilerParams(vmem_limit_bytes=VMEM_LIMIT, **kw)


HBM = pl.BlockSpec(memory_space=pltpu.HBM)


class _Task:
    def __init__(self, ins, out_shapes, sems, descs):
        self.ins, self.out_shapes, self.sems, self.descs = list(ins), list(out_shapes), list(sems), descs

    def start(self, ins, outs, sems):
        locs, pairs = self.descs(ins, outs, sems)
        for lc in locs:
            lc().start()
        for snd, _ in pairs:
            snd().start()

    def wait(self, ins, outs, sems):
        locs, pairs = self.descs(ins, outs, sems)
        for _, rcv in pairs:
            rcv().wait_recv()
        for snd, _ in pairs:
            snd().wait_send()
        for lc in locs:
            lc().wait()


def _call_tasks(body, tasks, *, name, grid, in_specs, out_specs, out_shape, args, scratch_shapes=()):
    in_specs, out_specs, out_shape, scratch_shapes = map(list, (in_specs, out_specs, out_shape, scratch_shapes))
    n_in, n_out, n_sc = len(in_specs), len(out_specs), len(scratch_shapes)
    t_in = [len(t.ins) for t in tasks]
    t_out = [len(t.out_shapes) for t in tasks]
    t_sem = [len(t.sems) for t in tasks]

    def wrapped(*refs):
        pos = [0]

        def take(n):
            pos[0] += n
            return refs[pos[0] - n:pos[0]]

        ins, tins = take(n_in), [take(n) for n in t_in]
        outs, touts = take(n_out), [take(n) for n in t_out]
        sc, tsems = take(n_sc), [take(n) for n in t_sem]
        if tasks:
            first = functools.reduce(jnp.logical_and, [pl.program_id(a) == 0 for a in range(len(grid))])
            last = functools.reduce(jnp.logical_and, [pl.program_id(a) == grid[a] - 1 for a in range(len(grid))])

            @pl.when(first)
            def _():
                for t, a, b, s in zip(tasks, tins, touts, tsems):
                    t.start(a, b, s)

        body(*ins, *outs, *sc)
        if tasks:
            @pl.when(last)
            def _():
                for t, a, b, s in zip(tasks, tins, touts, tsems):
                    t.wait(a, b, s)

    res = _call(
        wrapped, name=name, grid=grid,
        in_specs=in_specs + [HBM] * sum(t_in), out_specs=out_specs + [HBM] * sum(t_out),
        out_shape=out_shape + [s for t in tasks for s in t.out_shapes],
        scratch_shapes=scratch_shapes + [s for t in tasks for s in t.sems],
        compiler_params=_cp(),
    )(*args, *[a for t in tasks for a in t.ins])
    res = list(res)
    touts, pos = [], n_out
    for n in t_out:
        touts.append(res[pos:pos + n])
        pos += n
    return res[:n_out], touts


def _run_tasks(tasks, name):
    t_in = [len(t.ins) for t in tasks]
    t_out = [len(t.out_shapes) for t in tasks]
    t_sem = [len(t.sems) for t in tasks]

    def body(*refs):
        pos = [0]

        def take(n):
            pos[0] += n
            return refs[pos[0] - n:pos[0]]

        tins, touts, tsems = [take(n) for n in t_in], [take(n) for n in t_out], [take(n) for n in t_sem]
        for t, a, b, s in zip(tasks, tins, touts, tsems):
            t.start(a, b, s)
        for t, a, b, s in zip(tasks, tins, touts, tsems):
            t.wait(a, b, s)

    res = list(_call(
        body, name=name, in_specs=[HBM] * sum(t_in), out_specs=[HBM] * sum(t_out),
        out_shape=[s for t in tasks for s in t.out_shapes],
        scratch_shapes=[s for t in tasks for s in t.sems],
    )(*[a for t in tasks for a in t.ins]))
    touts, pos = [], 0
    for n in t_out:
        touts.append(res[pos:pos + n])
        pos += n
    return touts


def _tile(n, cands):
    for c in cands:
        if n % c == 0:
            return c
    raise ValueError(f"no tile for {n} among {cands}")


def _dot(a, b, dims=None):
    if dims is None:
        return jnp.dot(a, b, preferred_element_type=F32)
    return lax.dot_general(a, b, dims, preferred_element_type=F32)


def _rms(x, gain, n):
    r = lax.rsqrt(jnp.sum(x * x, axis=-1, keepdims=True) * (1.0 / n) + EPS)
    xh = x * r
    return xh * gain, xh, r


def _rms_bwd(dy, xh, r, gain, n):
    dxh = dy * gain
    return r * (dxh - xh * (jnp.sum(dxh * xh, axis=-1, keepdims=True) * (1.0 / n)))


def _lane(shape):
    return lax.broadcasted_iota(jnp.int32, shape, len(shape) - 1)


def _half_sum(x):
    lo = _lane(x.shape) < 64
    s_lo = jnp.sum(jnp.where(lo, x, 0.0), axis=-1, keepdims=True)
    s_hi = jnp.sum(jnp.where(lo, 0.0, x), axis=-1, keepdims=True)
    return jnp.where(lo, s_lo, s_hi)


def _rope_swap(x):
    ln = _lane(x.shape)
    sw = jnp.where(ln < 80, pltpu.roll(x, 112, 1), pltpu.roll(x, 16, 1))
    return jnp.where(jnp.logical_and(ln >= 64, ln < 96), sw, 0.0)


def _colsum(x):
    return jnp.sum(x, axis=0, keepdims=True)


def _ffn_weight_specs():
    once = pl.Buffered(1)
    return [pl.BlockSpec((D, DFF), lambda i: (0, 0), pipeline_mode=once),
            pl.BlockSpec((D, DFF), lambda i: (0, 1), pipeline_mode=once),
            pl.BlockSpec((DFF, D), lambda i: (0, 0), pipeline_mode=once)]


def _ffn_fwd(h, norm, wgu, wd, name, tasks=()):
    T = h.shape[0]
    tm = _tile(T, (512, 384, 256, 128))

    def body(h_ref, n_ref, wg_ref, wu_ref, wd_ref, o_ref):
        x = h_ref[...]
        u, _, _ = _rms(x, n_ref[...], D)
        ub = u.astype(BF16)
        g = _dot(ub, wg_ref[...])
        p = _dot(ub, wu_ref[...])
        a = (g * jax.nn.sigmoid(g)) * p
        o_ref[...] = x + 0.5 * _dot(a.astype(BF16), wd_ref[...])

    (out,), touts = _call_tasks(
        body, tasks, name=name, grid=(T // tm,),
        in_specs=[pl.BlockSpec((tm, D), lambda i: (i, 0)), pl.BlockSpec((1, D), lambda i: (0, 0))] + _ffn_weight_specs(),
        out_specs=[pl.BlockSpec((tm, D), lambda i: (i, 0))],
        out_shape=[jax.ShapeDtypeStruct((T, D), F32)],
        args=(h, norm, wgu, wgu, wd))
    return out, touts


def _ffn_bwd(h, dout, norm, wgu, wd, name, tasks=()):
    T = h.shape[0]
    tm = _tile(T, (256, 128))

    def body(h_ref, d_ref, n_ref, wg_ref, wu_ref, wd_ref, dh_ref, u_ref, a_ref, dgp_ref, dn_ref):
        @pl.when(pl.program_id(0) == 0)
        def _():
            dn_ref[...] = jnp.zeros_like(dn_ref)

        u, xh, r = _rms(h_ref[...], n_ref[...], D)
        ub = u.astype(BF16)
        u_ref[...] = ub
        g = _dot(ub, wg_ref[...])
        p = _dot(ub, wu_ref[...])
        s = jax.nn.sigmoid(g)
        sl = g * s
        dz = (0.5 * d_ref[...]).astype(BF16)
        da = _dot(dz, wd_ref[...], NT)
        dp = da * sl
        dg = (da * p) * (s * (1.0 + g * (1.0 - s)))
        a_ref[...] = (sl * p).astype(BF16)
        dgb = dg.astype(BF16)
        dpb = dp.astype(BF16)
        dgp_ref[:, :DFF] = dgb
        dgp_ref[:, DFF:] = dpb
        du = _dot(dgb, wg_ref[...], NT) + _dot(dpb, wu_ref[...], NT)
        dn_ref[...] += _colsum(du * xh)
        dh_ref[...] = d_ref[...] + _rms_bwd(du, xh, r, n_ref[...], D)

    row = lambda w: pl.BlockSpec((tm, w), lambda i: (i, 0))
    return _call_tasks(
        body, tasks, name=name, grid=(T // tm,),
        in_specs=[row(D), row(D), pl.BlockSpec((1, D), lambda i: (0, 0))] + _ffn_weight_specs(),
        out_specs=[row(D), row(D), row(DFF), row(2 * DFF), pl.BlockSpec((1, D), lambda i: (0, 0))],
        out_shape=[jax.ShapeDtypeStruct((T, D), F32),
                   jax.ShapeDtypeStruct((T, D), BF16),
                   jax.ShapeDtypeStruct((T, DFF), BF16),
                   jax.ShapeDtypeStruct((T, 2 * DFF), BF16),
                   jax.ShapeDtypeStruct((1, D), F32)],
        args=(h, dout, norm, wgu, wgu, wd))


def _wgrad(x, y, name, scale=1.0, bk=None, bn=None, shards=0, bt=512, tasks=()):
    T, K = x.shape
    N = y.shape[1]
    bk = bk or K
    bn = bn or N
    bt = _tile(T, (bt, 512, 384, 256, 128))
    nt = T // bt

    def body(x_ref, y_ref, o_ref, acc_ref):
        t = pl.program_id(2)

        @pl.when(t == 0)
        def _():
            acc_ref[...] = jnp.zeros_like(acc_ref)

        acc_ref[...] += _dot(x_ref[...].astype(BF16), y_ref[...].astype(BF16), TN)

        @pl.when(t == nt - 1)
        def _():
            res = (acc_ref[...] * scale).astype(o_ref.dtype)
            if shards:
                w = bn // shards
                for s in range(shards):
                    o_ref[s] = res[:, s * w:(s + 1) * w]
            else:
                o_ref[...] = res

    if shards:
        assert bk == K
        out_spec = pl.BlockSpec((shards, K, bn // shards), lambda i, j, t: (j, 0, 0))
        out_shape = jax.ShapeDtypeStruct((N * shards // bn, K, bn // shards), GRAD_DTYPE)
    else:
        out_spec = pl.BlockSpec((bk, bn), lambda i, j, t: (i, j))
        out_shape = jax.ShapeDtypeStruct((K, N), GRAD_DTYPE)
    (out,), touts = _call_tasks(
        body, tasks, name=name, grid=(K // bk, N // bn, nt),
        in_specs=[pl.BlockSpec((bt, bk), lambda i, j, t: (t, i)),
                  pl.BlockSpec((bt, bn), lambda i, j, t: (t, j))],
        out_specs=[out_spec], out_shape=[out_shape],
        scratch_shapes=[pltpu.VMEM((bk, bn), F32)],
        args=(x, y))
    return out, touts


def _inproj_fwd(h, norm, w, name):
    T = h.shape[0]
    tm = _tile(T, (1088, 512, 384, 256, 128))
    tn = 1024

    def body(h_ref, n_ref, w_ref, o_ref, u_ref):
        @pl.when(pl.program_id(1) == 0)
        def _():
            u, _, _ = _rms(h_ref[...], n_ref[...], D)
            u_ref[...] = u.astype(BF16)

        o_ref[...] = _dot(u_ref[...], w_ref[...])

    return _call(
        body, name=name, grid=(T // tm, PROJW // tn),
        in_specs=[pl.BlockSpec((tm, D), lambda i, j: (i, 0)),
                  pl.BlockSpec((1, D), lambda i, j: (0, 0)),
                  pl.BlockSpec((D, tn), lambda i, j: (0, j))],
        out_specs=[pl.BlockSpec((tm, tn), lambda i, j: (i, j)),
                   pl.BlockSpec((tm, D), lambda i, j: (i, 0))],
        out_shape=[jax.ShapeDtypeStruct((T, PROJW), F32), jax.ShapeDtypeStruct((T, D), BF16)],
        compiler_params=_cp(),
    )(h, norm, w)


def _inproj_bwd(h, dres, dlo, dhi, norm, w, name):
    T = h.shape[0]
    tm = _tile(T, (512, 384, 256, 128))
    hw = PROJW // 2

    def body(h_ref, d_ref, lo_ref, hi_ref, n_ref, wlo_ref, whi_ref, dh_ref, dn_ref):
        @pl.when(pl.program_id(0) == 0)
        def _():
            dn_ref[...] = jnp.zeros_like(dn_ref)

        _, xh, r = _rms(h_ref[...], n_ref[...], D)
        du = _dot(lo_ref[...], wlo_ref[...], NT) + _dot(hi_ref[...], whi_ref[...], NT)
        dn_ref[...] += _colsum(du * xh)
        dh_ref[...] = d_ref[...] + _rms_bwd(du, xh, r, n_ref[...], D)

    return _call(
        body, name=name, grid=(T // tm,),
        in_specs=[pl.BlockSpec((tm, D), lambda i: (i, 0)),
                  pl.BlockSpec((tm, D), lambda i: (i, 0)),
                  pl.BlockSpec((tm, hw), lambda i: (i, 0)),
                  pl.BlockSpec((tm, hw), lambda i: (i, 0)),
                  pl.BlockSpec((1, D), lambda i: (0, 0)),
                  pl.BlockSpec((D, hw), lambda i: (0, 0)),
                  pl.BlockSpec((D, hw), lambda i: (0, 1))],
        out_specs=[pl.BlockSpec((tm, D), lambda i: (i, 0)),
                   pl.BlockSpec((1, D), lambda i: (0, 0))],
        out_shape=[jax.ShapeDtypeStruct((T, D), F32), jax.ShapeDtypeStruct((1, D), F32)],
        compiler_params=_cp(),
    )(h, dres, dlo, dhi, norm, w, w)


C_FQ, C_FK, C_FV, C_CQ, C_CKV, C_MISC, C_GATE = 0, 512, 1024, 1536, 1792, 1920, 2048
L_KR, L_FL = 64, 96


def _prep_fwd(proj, rc, rs, gfq, gfk, gcq, gckv, gmq, gmk, bfv, wuq, wuk, wuv, name):
    T = proj.shape[0]
    tm = _tile(T, (256, 128))

    def body(p_ref, rc_ref, rs_ref, gfq_ref, gfk_ref, gcq_ref, gckv_ref, gmq_ref, gmk_ref, bf_ref,
             wuq_ref, wuk_ref, wuv_ref, fq_ref, fk_ref, fv_ref, qm_ref, km_ref, vm_ref, lf_ref):
        for blk in range(4):
            for (c0, g_ref, o_ref) in ((C_FQ, gfq_ref, fq_ref), (C_FK, gfk_ref, fk_ref)):
                x = p_ref[:, c0 + 128 * blk:c0 + 128 * (blk + 1)]
                r = lax.rsqrt(_half_sum(x * x) * (1.0 / 64) + EPS)
                o_ref[:, 128 * blk:128 * (blk + 1)] = (x * r * g_ref[...]).astype(BF16)
        fv_ref[...] = p_ref[:, C_FV:C_FV + 512].astype(BF16)

        rcv = rc_ref[...]
        rsv = rs_ref[...]
        cqn, _, _ = _rms(p_ref[:, C_CQ:C_CQ + QR], gcq_ref[...], QR)
        qpre = _dot(cqn.astype(BF16), wuq_ref[...])
        ckvn, _, _ = _rms(p_ref[:, C_CKV:C_CKV + KVR], gckv_ref[...], KVR)
        ckvb = ckvn.astype(BF16)
        kpre = _dot(ckvb, wuk_ref[...])
        vm_ref[...] = _dot(ckvb, wuv_ref[...]).astype(BF16)
        misc = p_ref[:, C_MISC:C_MISC + 128]
        ln = _lane(misc.shape)
        kr = jnp.where(jnp.logical_and(ln >= L_KR, ln < L_KR + ROPE), misc, 0.0)
        for hh in range(NH):
            sl = slice(128 * hh, 128 * (hh + 1))
            qn, _, _ = _rms(qpre[:, sl], gmq_ref[...], MLA_QK)
            qm_ref[:, sl] = (qn * rcv + _rope_swap(qn) * rsv).astype(BF16)
            kn, _, _ = _rms(kpre[:, sl] + kr, gmk_ref[...], MLA_QK)
            km_ref[:, sl] = (kn * rcv + _rope_swap(kn) * rsv).astype(BF16)
        z = misc + bf_ref[...]
        lf_ref[...] = jnp.minimum(z, 0.0) - jnp.log(1.0 + jnp.exp(-jnp.abs(z)))

    row = lambda w: pl.BlockSpec((tm, w), lambda i: (i, 0))
    full = lambda a: pl.BlockSpec(a.shape, lambda i: (0, 0))
    return _call(
        body, name=name, grid=(T // tm,),
        in_specs=[row(PROJW // 2), row(128), row(128)] + [full(a) for a in (gfq, gfk, gcq, gckv, gmq, gmk, bfv, wuq, wuk, wuv)],
        out_specs=[row(512), row(512), row(512), row(1024), row(1024), row(512), row(128)],
        out_shape=[jax.ShapeDtypeStruct((T, 512), BF16), jax.ShapeDtypeStruct((T, 512), BF16),
                   jax.ShapeDtypeStruct((T, 512), BF16), jax.ShapeDtypeStruct((T, 1024), BF16),
                   jax.ShapeDtypeStruct((T, 1024), BF16), jax.ShapeDtypeStruct((T, 512), BF16),
                   jax.ShapeDtypeStruct((T, 128), F32)],
        compiler_params=_cp(),
    )(proj, rc, rs, gfq, gfk, gcq, gckv, gmq, gmk, bfv, wuq, wuk, wuv)


def _prep_bwd(proj, rc, rs, gfq, gfk, gcq, gckv, gmq, gmk, bfv, wuq, wuk, wuv,
              dfq, dfk, dfv, dqm, dkm, dvm, dlf, name):
    T = proj.shape[0]
    tm = _tile(T, (256, 128))

    def body(p_ref, rc_ref, rs_ref, gfq_ref, gfk_ref, gcq_ref, gckv_ref, gmq_ref, gmk_ref, bf_ref,
             wuq_ref, wuk_ref, wuv_ref, dfq_ref, dfk_ref, dfv_ref, dqm_ref, dkm_ref, dvm_ref, dlf_ref,
             dp_ref, dgfq_ref, dgfk_ref, dgcq_ref, dgckv_ref, dgmq_ref, dgmk_ref, dbf_ref,
             dwuq_ref, dwuk_ref, dwuv_ref, dqpre_sc, dkpre_sc):
        accs = (dgfq_ref, dgfk_ref, dgcq_ref, dgckv_ref, dgmq_ref, dgmk_ref, dbf_ref, dwuq_ref, dwuk_ref, dwuv_ref)

        @pl.when(pl.program_id(0) == 0)
        def _():
            for a in accs:
                a[...] = jnp.zeros_like(a)

        for (c0, g_ref, d_ref, dg_ref) in ((C_FQ, gfq_ref, dfq_ref, dgfq_ref), (C_FK, gfk_ref, dfk_ref, dgfk_ref)):
            dg = jnp.zeros((1, 128), F32)
            for blk in range(4):
                x = p_ref[:, c0 + 128 * blk:c0 + 128 * (blk + 1)]
                r = lax.rsqrt(_half_sum(x * x) * (1.0 / 64) + EPS)
                xh = x * r
                dy = d_ref[:, 128 * blk:128 * (blk + 1)]
                dg = dg + _colsum(dy * xh)
                dxh = dy * g_ref[...]
                dx = r * (dxh - xh * (_half_sum(dxh * xh) * (1.0 / 64)))
                dp_ref[:, c0 + 128 * blk:c0 + 128 * (blk + 1)] = dx.astype(BF16)
            dg_ref[...] += dg
        dp_ref[:, C_FV:C_FV + 512] = dfv_ref[...].astype(BF16)

        rcv = rc_ref[...]
        rsv = rs_ref[...]
        cqn, cqh, cqr = _rms(p_ref[:, C_CQ:C_CQ + QR], gcq_ref[...], QR)
        cqb = cqn.astype(BF16)
        qpre = _dot(cqb, wuq_ref[...])
        dgq = jnp.zeros((1, 128), F32)
        for hh in range(NH):
            sl = slice(128 * hh, 128 * (hh + 1))
            _, xh, r = _rms(qpre[:, sl], gmq_ref[...], MLA_QK)
            dout = dqm_ref[:, sl]
            dqn = dout * rcv + _rope_swap(dout * rsv)
            dgq = dgq + _colsum(dqn * xh)
            dqpre_sc[:, sl] = _rms_bwd(dqn, xh, r, gmq_ref[...], MLA_QK).astype(BF16)
        dgmq_ref[...] += dgq
        dqpre = dqpre_sc[...]
        dwuq_ref[...] += _dot(cqb, dqpre, TN)
        dcqn = _dot(dqpre, wuq_ref[...], NT)
        dgcq_ref[...] += _colsum(dcqn * cqh)
        dp_ref[:, C_CQ:C_CQ + QR] = _rms_bwd(dcqn, cqh, cqr, gcq_ref[...], QR).astype(BF16)

        ckvn, ckvh, ckvr = _rms(p_ref[:, C_CKV:C_CKV + KVR], gckv_ref[...], KVR)
        ckvb = ckvn.astype(BF16)
        kpre = _dot(ckvb, wuk_ref[...])
        misc = p_ref[:, C_MISC:C_MISC + 128]
        ln = _lane(misc.shape)
        is_kr = jnp.logical_and(ln >= L_KR, ln < L_KR + ROPE)
        kr = jnp.where(is_kr, misc, 0.0)
        dgk = jnp.zeros((1, 128), F32)
        dkr = jnp.zeros(misc.shape, F32)
        for hh in range(NH):
            sl = slice(128 * hh, 128 * (hh + 1))
            _, xh, r = _rms(kpre[:, sl] + kr, gmk_ref[...], MLA_QK)
            dout = dkm_ref[:, sl]
            dkn = dout * rcv + _rope_swap(dout * rsv)
            dgk = dgk + _colsum(dkn * xh)
            dkx = _rms_bwd(dkn, xh, r, gmk_ref[...], MLA_QK)
            dkr = dkr + jnp.where(is_kr, dkx, 0.0)
            dkpre_sc[:, sl] = jnp.where(ln < 64, dkx, 0.0).astype(BF16)
        dgmk_ref[...] += dgk
        dkpre = dkpre_sc[...]
        dvmb = dvm_ref[...].astype(BF16)
        dwuk_ref[...] += _dot(ckvb, dkpre, TN)
        dwuv_ref[...] += _dot(ckvb, dvmb, TN)
        dckvn = _dot(dkpre, wuk_ref[...], NT) + _dot(dvmb, wuv_ref[...], NT)
        dgckv_ref[...] += _colsum(dckvn * ckvh)
        dp_ref[:, C_CKV:C_CKV + KVR] = _rms_bwd(dckvn, ckvh, ckvr, gckv_ref[...], KVR).astype(BF16)

        z = misc + bf_ref[...]
        dz = dlf_ref[...] * (1.0 - jax.nn.sigmoid(z))
        dbf_ref[...] += _colsum(dz)
        dp_ref[:, C_MISC:C_MISC + 128] = (dkr + dz).astype(BF16)

    row = lambda w: pl.BlockSpec((tm, w), lambda i: (i, 0))
    full = lambda a: pl.BlockSpec(a.shape, lambda i: (0, 0))
    small = (gfq, gfk, gcq, gckv, gmq, gmk, bfv, wuq, wuk, wuv)
    acc_shapes = [(1, 128), (1, 128), (1, QR), (1, KVR), (1, 128), (1, 128), (1, 128),
                  (QR, 1024), (KVR, 1024), (KVR, 512)]
    return _call(
        body, name=name, grid=(T // tm,),
        in_specs=[row(PROJW // 2), row(128), row(128)] + [full(a) for a in small]
                 + [row(512), row(512), row(512), row(1024), row(1024), row(512), row(128)],
        out_specs=[row(PROJW // 2)] + [pl.BlockSpec(s, lambda i: (0, 0)) for s in acc_shapes],
        out_shape=[jax.ShapeDtypeStruct((T, PROJW // 2), BF16)] + [jax.ShapeDtypeStruct(s, F32) for s in acc_shapes],
        scratch_shapes=[pltpu.VMEM((tm, 1024), BF16), pltpu.VMEM((tm, 1024), BF16)],
        compiler_params=_cp(),
    )(proj, rc, rs, *small, dfq, dfk, dfv, dqm, dkm, dvm, dlf)


def _scan_lanes(x, reverse):
    n = x.shape[-1]
    ln = _lane(x.shape)
    k = 1
    while k < n:
        if reverse:
            x = x + jnp.where(ln < n - k, pltpu.roll(x, n - k, x.ndim - 1), 0.0)
        else:
            x = x + jnp.where(ln >= k, pltpu.roll(x, k, x.ndim - 1), 0.0)
        k *= 2
    return x


def _forget_scan(lf, reverse, name):
    def body(x_ref, o_ref):
        x = x_ref[...]
        ln = _lane(x.shape)
        pad = jnp.logical_and(ln >= NMETA, ln < MPAD)
        o_ref[...] = jnp.where(pad, 0.0, _scan_lanes(jnp.where(pad, 0.0, x), reverse))

    return _call(body, name=name, out_shape=jax.ShapeDtypeStruct(lf.shape, F32), compiler_params=_cp())(lf)


def _attn_blocks(LP, tq):
    edges = sorted({MPAD, LP} | set(range(tq, LP, tq)))
    return [(0, MPAD, MPAD)] + [(a, b - a, b) for a, b in zip(edges[:-1], edges[1:])]


def _attn_scores(q_ref, k_ref, e, r0, rn, kend, wide, scale, bias):
    if wide:
        qe = q_ref[r0:r0 + rn, 128 * e:128 * (e + 1)]
        ke = k_ref[0:kend, 128 * e:128 * (e + 1)]
    else:
        qb = q_ref[r0:r0 + rn, :]
        mine = (_lane(qb.shape) < 64) if e == 0 else (_lane(qb.shape) >= 64)
        qe = jnp.where(mine, qb, jnp.zeros_like(qb))
        ke = k_ref[0:kend, :]
    s = _dot(qe, ke, NT) * scale
    if bias is not None:
        ct_ref, cr_ref = bias
        s = s + ct_ref[0, r0:r0 + rn, e:e + 1] - cr_ref[0, :, 0:kend]
    neg = -1e30
    if r0 == 0:
        qi = lax.broadcasted_iota(jnp.int32, (rn, kend), 0)
        ki = lax.broadcasted_iota(jnp.int32, (rn, kend), 1)
        s = jnp.where(jnp.logical_and(ki <= qi, ki < NMETA), s, neg)
    else:
        d0 = kend - rn
        head = jnp.where(_lane((rn, MPAD)) < NMETA, s[:, :MPAD], neg)
        qi = lax.broadcasted_iota(jnp.int32, (rn, rn), 0)
        diag = jnp.where(_lane((rn, rn)) <= qi, s[:, d0:], neg)
        s = jnp.concatenate([head] + ([s[:, MPAD:d0]] if d0 > MPAD else []) + [diag], axis=1)
    m = jnp.max(s, axis=-1, keepdims=True)
    p = jnp.exp(s - m)
    l = jnp.sum(p, axis=-1, keepdims=True)
    return qe, ke, p, l


def _attn_specs(B, LP, wide, has_bias):
    qw = 256 if wide else 128
    specs = [pl.BlockSpec((LP, qw), lambda b, hp: (b, hp)),
             pl.BlockSpec((LP, qw), lambda b, hp: (b, hp)),
             pl.BlockSpec((LP, 128), lambda b, hp: (b, hp))]
    bias_specs = []
    if has_bias:
        bias_specs = [pl.BlockSpec((1, LP, 2), lambda b, hp: (b * 4 + hp, 0, 0)),
                      pl.BlockSpec((1, 1, LP), lambda b, hp: (b * 8 + 2 * hp, 0, 0)),
                      pl.BlockSpec((1, 1, LP), lambda b, hp: (b * 8 + 2 * hp + 1, 0, 0))]
    return qw, specs, bias_specs


def _attn_fwd(q, k, v, bias, B, LP, wide, scale, name, tasks=()):
    T = q.shape[0]
    tq = 256
    blocks = _attn_blocks(LP, tq)
    qw, specs, bias_specs = _attn_specs(B, LP, wide, bias is not None)

    def body(*refs):
        if bias is not None:
            q_ref, k_ref, v_ref, ct_ref, cr0_ref, cr1_ref, o_ref = refs
            crs = (cr0_ref, cr1_ref)
        else:
            q_ref, k_ref, v_ref, o_ref = refs
        for (r0, rn, kend) in blocks:
            outs = []
            for e in (0, 1):
                bs = (ct_ref, crs[e]) if bias is not None else None
                _, _, p, l = _attn_scores(q_ref, k_ref, e, r0, rn, kend, wide, scale, bs)
                outs.append(_dot(p.astype(BF16), v_ref[0:kend, :]) / l)
            o = jnp.where(_lane(outs[0].shape) < 64, outs[0], outs[1])
            o_ref[r0:r0 + rn, :] = o.astype(BF16)

    args = (q, k, v) + ((bias[0], bias[1], bias[1]) if bias is not None else ())
    (out,), touts = _call_tasks(
        body, tasks, name=name, grid=(B, 4),
        in_specs=specs + bias_specs,
        out_specs=[pl.BlockSpec((LP, 128), lambda b, hp: (b, hp))],
        out_shape=[jax.ShapeDtypeStruct((T, 512), BF16)],
        args=args)
    return out, touts


def _attn_bwd(q, k, v, do, bias, B, LP, wide, scale, name, tasks=()):
    T = q.shape[0]
    tq = 256
    blocks = _attn_blocks(LP, tq)
    qw, specs, bias_specs = _attn_specs(B, LP, wide, bias is not None)
    has_bias = bias is not None

    def body(*refs):
        if has_bias:
            (q_ref, k_ref, v_ref, do_ref, ct_ref, cr0_ref, cr1_ref,
             dq_ref, dk_ref, dv_ref, dc0_ref, dc1_ref) = refs
            crs = (cr0_ref, cr1_ref)
            dcs = (dc0_ref, dc1_ref)
            dc0_ref[...] = jnp.zeros_like(dc0_ref)
            dc1_ref[...] = jnp.zeros_like(dc1_ref)
        else:
            q_ref, k_ref, v_ref, do_ref, dq_ref, dk_ref, dv_ref = refs
        dk_ref[...] = jnp.zeros_like(dk_ref)
        dv_ref[...] = jnp.zeros_like(dv_ref)
        for (r0, rn, kend) in blocks:
            dqs = []
            for e in (0, 1):
                bs = (ct_ref, crs[e]) if has_bias else None
                qe, ke, p, l = _attn_scores(q_ref, k_ref, e, r0, rn, kend, wide, scale, bs)
                pn = p * (1.0 / l)
                dob = do_ref[r0:r0 + rn, :]
                mine = (_lane(dob.shape) < 64) if e == 0 else (_lane(dob.shape) >= 64)
                doe = jnp.where(mine, dob, jnp.zeros_like(dob))
                dp = _dot(doe, v_ref[0:kend, :], NT)
                delta = jnp.sum(pn * dp, axis=-1, keepdims=True)
                ds = pn * (dp - delta)
                dsb = ds.astype(BF16)
                dqe = _dot(dsb, ke) * scale
                dke = _dot(dsb, qe, TN) * scale
                if wide:
                    dq_ref[r0:r0 + rn, 128 * e:128 * (e + 1)] = dqe
                    dk_ref[0:kend, 128 * e:128 * (e + 1)] += dke
                else:
                    dqs.append(dqe)
                    dk_ref[0:kend, :] += dke
                dv_ref[0:kend, :] += _dot(pn.astype(BF16), doe, TN)
                if has_bias:
                    dcs[e][0, :, 0:kend] -= _colsum(ds)
            if not wide:
                dq_ref[r0:r0 + rn, :] = jnp.where(_lane(dqs[0].shape) < 64, dqs[0], dqs[1])

    args = (q, k, v, do) + ((bias[0], bias[1], bias[1]) if has_bias else ())
    out_specs = [pl.BlockSpec((LP, qw), lambda b, hp: (b, hp)),
                 pl.BlockSpec((LP, qw), lambda b, hp: (b, hp)),
                 pl.BlockSpec((LP, 128), lambda b, hp: (b, hp))]
    out_shape = [jax.ShapeDtypeStruct(q.shape, F32), jax.ShapeDtypeStruct(q.shape, F32),
                 jax.ShapeDtypeStruct((T, 512), F32)]
    if has_bias:
        out_specs += [pl.BlockSpec((1, 1, LP), lambda b, hp: (b * 4 + hp, 0, 0))] * 2
        out_shape += [jax.ShapeDtypeStruct((B * 4, 1, LP), F32)] * 2
    return _call_tasks(
        body, tasks, name=name, grid=(B, 4),
        in_specs=specs + [pl.BlockSpec((LP, 128), lambda b, hp: (b, hp))] + bias_specs,
        out_specs=out_specs, out_shape=out_shape, args=args)


def _post_fwd(h, of, om, proj, bg, wbf, wbm, wout, name):
    T = h.shape[0]
    tm = _tile(T, (512, 384, 256, 128))

    def body(h_ref, of_ref, om_ref, gl_ref, bg_ref, wbf_ref, wbm_ref, wo_ref, o_ref, mix_ref):
        gate = jax.nn.sigmoid(gl_ref[...] + bg_ref[...])
        mix = gate[:, :D] * _dot(of_ref[...], wbf_ref[...]) + gate[:, D:] * _dot(om_ref[...], wbm_ref[...])
        mb = mix.astype(BF16)
        mix_ref[...] = mb
        o_ref[...] = h_ref[...] + _dot(mb, wo_ref[...])

    row = lambda w: pl.BlockSpec((tm, w), lambda i: (i, 0))
    full = lambda a: pl.BlockSpec(a.shape, lambda i: (0, 0))
    return _call(
        body, name=name, grid=(T // tm,),
        in_specs=[row(D), row(512), row(512), pl.BlockSpec((tm, 2 * D), lambda i: (i, 1)),
                  full(bg), full(wbf), full(wbm), full(wout)],
        out_specs=[row(D), row(D)],
        out_shape=[jax.ShapeDtypeStruct((T, D), F32), jax.ShapeDtypeStruct((T, D), BF16)],
        compiler_params=_cp(),
    )(h, of, om, proj, bg, wbf, wbm, wout)


def _post_bwd(dh, of, om, proj, bg, wbf, wbm, wout, name):
    T = dh.shape[0]
    tm = _tile(T, (512, 384, 256, 128))

    def body(d_ref, of_ref, om_ref, gl_ref, bg_ref, wbf_ref, wbm_ref, wo_ref,
             dgl_ref, dbf_ref, dbm_ref, dof_ref, dom_ref, dbg_ref):
        @pl.when(pl.program_id(0) == 0)
        def _():
            dbg_ref[...] = jnp.zeros_like(dbg_ref)

        gate = jax.nn.sigmoid(gl_ref[...] + bg_ref[...])
        dmix = _dot(d_ref[...].astype(BF16), wo_ref[...], NT)
        ofx = _dot(of_ref[...], wbf_ref[...])
        omx = _dot(om_ref[...], wbm_ref[...])
        gf = gate[:, :D]
        gm = gate[:, D:]
        dof = (dmix * gf).astype(BF16)
        dom = (dmix * gm).astype(BF16)
        dglf = dmix * ofx * gf * (1.0 - gf)
        dglm = dmix * omx * gm * (1.0 - gm)
        dgl_ref[:, :D] = dglf.astype(BF16)
        dgl_ref[:, D:] = dglm.astype(BF16)
        dbg_ref[:, :D] += _colsum(dglf)
        dbg_ref[:, D:] += _colsum(dglm)
        dbf_ref[...] = dof
        dbm_ref[...] = dom
        dof_ref[...] = _dot(dof, wbf_ref[...], NT).astype(BF16)
        dom_ref[...] = _dot(dom, wbm_ref[...], NT).astype(BF16)

    row = lambda w: pl.BlockSpec((tm, w), lambda i: (i, 0))
    full = lambda a: pl.BlockSpec(a.shape, lambda i: (0, 0))
    return _call(
        body, name=name, grid=(T // tm,),
        in_specs=[row(D), row(512), row(512), pl.BlockSpec((tm, 2 * D), lambda i: (i, 1)),
                  full(bg), full(wbf), full(wbm), full(wout)],
        out_specs=[row(2 * D), row(D), row(D), row(512), row(512), pl.BlockSpec((1, 2 * D), lambda i: (0, 0))],
        out_shape=[jax.ShapeDtypeStruct((T, 2 * D), BF16), jax.ShapeDtypeStruct((T, D), BF16),
                   jax.ShapeDtypeStruct((T, D), BF16), jax.ShapeDtypeStruct((T, 512), BF16),
                   jax.ShapeDtypeStruct((T, 512), BF16), jax.ShapeDtypeStruct((1, 2 * D), F32)],
        compiler_params=_cp(),
    )(dh, of, om, proj, bg, wbf, wbm, wout)


def _loss_head(h3, target, B, LP, name):
    S = LP - MPAD
    nb = LP // 128

    def body(h_ref, t_ref, dy_ref, l_ref):
        b = pl.program_id(0)
        p = pl.program_id(1)

        @pl.when(jnp.logical_and(b == 0, p == 0))
        def _():
            l_ref[...] = jnp.zeros_like(l_ref)

        @pl.when(p == 0)
        def _():
            dy_ref[...] = jnp.zeros_like(dy_ref)

        @pl.when(p > 0)
        def _():
            e = h_ref[...] - t_ref[0]
            dy_ref[...] = e * (1.0 / D)
            l_ref[...] += jnp.sum(e * e, axis=0, keepdims=True) * (0.5 / D)

    return _call(
        body, name=name, grid=(B, nb),
        in_specs=[pl.BlockSpec((128, D), lambda b, p: (b * nb + p, 0)),
                  pl.BlockSpec((1, 128, D), lambda b, p: (b, jnp.maximum(p - 1, 0), 0))],
        out_specs=[pl.BlockSpec((128, D), lambda b, p: (b * nb + p, 0)),
                   pl.BlockSpec((1, D), lambda b, p: (0, 0))],
        out_shape=[jax.ShapeDtypeStruct(h3.shape, F32), jax.ShapeDtypeStruct((1, D), F32)],
        compiler_params=_cp(),
    )(h3, target)


def _rope_tables(B, LP):
    pos = jnp.concatenate([jnp.arange(MPAD, dtype=F32), NMETA + jnp.arange(LP - MPAD, dtype=F32)])
    inv_freq = ROPE_THETA ** (-jnp.arange(0, ROPE, 2, dtype=F32) / ROPE)
    ang = pos[:, None] * inv_freq[None, :]
    cos, sin = jnp.cos(ang), jnp.sin(ang)
    z32 = jnp.zeros((LP, 32), F32)
    rc = jnp.concatenate([jnp.ones((LP, 64), F32), cos, cos, z32], axis=1)
    rs = jnp.concatenate([jnp.zeros((LP, 64), F32), -sin, sin, z32], axis=1)
    return jnp.tile(rc, (B, 1)), jnp.tile(rs, (B, 1))


def _pad_lanes(v, start, width=128):
    n = v.shape[1]
    return jnp.concatenate([jnp.zeros((1, start), F32), v, jnp.zeros((1, width - start - n), F32)], axis=1)


G_FFN1 = ["ffn1_w_gu", "ffn1_w_down"]
G_MIX = ["w_in", "mla_w_uq", "mla_w_ukv", "w_branch_fox", "w_branch_mla", "w_out"]
G_OUT = ["w_out", "w_branch_fox", "w_branch_mla"]
G_IN = ["w_in", "mla_w_uq", "mla_w_ukv"]


def _step(x, target, meta, vec, gath, shards):
    dist = shards is not None
    B, S, _ = x.shape
    LP = MPAD + S
    T = B * LP
    gath = dict(gath)

    def gather(names):
        return [_gather_task([shards[n] for n in names])] if dist else []

    def gathered(names, touts):
        if dist:
            gath.update(zip(names, touts[0]))

    g4, sums, red = {}, {}, {}

    def scatter(names):
        return [_a2a_task([_pieces(g4[n]) for n in names])] if dist else []

    def scattered(names, tout, me):
        for n, r in zip(names, tout):
            sums[n] = _sum_pieces(r, _pieces(g4[n]), me, "rs_sum_" + n)

    def join(names):
        return [_join_task([sums[n] for n in names])] if dist else []

    def joined(names, tout):
        for n, r in zip(names, tout):
            red[n] = (sums[n], r)

    me = None
    if dist:
        me = (4 * lax.axis_index("x") + 2 * lax.axis_index("y") + lax.axis_index("c")).reshape(1).astype(jnp.int32)

    h0 = jnp.concatenate([jnp.broadcast_to(meta[None], (B, NMETA, D)),
                          jnp.zeros((B, MPAD - NMETA, D), F32), x], axis=1).reshape(T, D)
    rc, rs = _rope_tables(B, LP)
    gfq = jnp.tile(vec["fox_q_norm"], (1, 2))
    gfk = jnp.tile(vec["fox_k_norm"], (1, 2))
    gmq = _pad_lanes(vec["mla_q_norm"], 0)
    gmk = _pad_lanes(vec["mla_k_norm"], 0)
    bfv = _pad_lanes(vec["b_forget"], L_FL)

    w1gu, w1d = _cols_from_shards(gath["ffn1_w_gu"]), gath["ffn1_w_down"].reshape(DFF, D)
    h1, touts = _ffn_fwd(h0, vec["ffn1_norm"], w1gu, w1d, "ffn1_fwd", gather(G_MIX))
    gathered(G_MIX, touts)
    wm = _mixer_weights(gath)
    small = (gfq, gfk, vec["mla_cq_norm"], vec["mla_ckv_norm"], gmq, gmk, bfv, wm["wuq"], wm["wuk"], wm["wuv"])
    proj, u2 = _inproj_fwd(h1, vec["mix_norm"], wm["w_in"], "inproj_fwd")
    fq, fk, fv, qm, km, vm, lf = _prep_fwd(proj, rc, rs, *small, name="prep_fwd")
    lf_rows = lf[:, L_FL:L_FL + NH].reshape(B, LP, NH).transpose(0, 2, 1).reshape(B * NH, LP)
    crow = _forget_scan(lf_rows, False, "forget_scan")
    ctok = crow.reshape(B, 4, 2, LP).transpose(0, 1, 3, 2).reshape(B * 4, LP, 2)
    bias = (ctok, crow.reshape(B * NH, 1, LP))
    of, touts = _attn_fwd(fq, fk, fv, bias, B, LP, False, 64 ** -0.5, "fox_fwd", gather(["ffn2_w_gu"]))
    gathered(["ffn2_w_gu"], touts)
    om, touts = _attn_fwd(qm, km, vm, None, B, LP, True, MLA_QK ** -0.5, "mla_fwd", gather(["ffn2_w_down"]))
    gathered(["ffn2_w_down"], touts)
    h2, mix = _post_fwd(h1, of, om, proj, vec["b_gate"], wm["wbf"], wm["wbm"], wm["w_out"], "post_fwd")
    w2gu, w2d = _cols_from_shards(gath["ffn2_w_gu"]), gath["ffn2_w_down"].reshape(DFF, D)
    h3, _ = _ffn_fwd(h2, vec["ffn2_norm"], w2gu, w2d, "ffn2_fwd")
    dy, lpart = _loss_head(h3, target, B, LP, "loss_head")

    gv = {}
    (dh2, u3, a2, dgp2, gv["ffn2_norm"]), _ = _ffn_bwd(h2, dy, vec["ffn2_norm"], w2gu, w2d, "ffn2_bwd")
    g4["ffn2_w_gu"] = _wgrad(u3, dgp2, "ffn2_dwgu", bn=DFF, shards=2)[0]
    g4["ffn2_w_down"] = _wgrad(a2, dy, "ffn2_dwd", scale=0.5, bk=FH)[0].reshape(N_CHIPS, DFF // N_CHIPS, D)

    dgl, dbf, dbm, dof, dom, gv["b_gate"] = _post_bwd(dh2, of, om, proj, vec["b_gate"], wm["wbf"], wm["wbm"], wm["w_out"], "post_bwd")
    g4["w_out"] = _wgrad(mix, dh2, "dw_out", bt=1088)[0].reshape(N_CHIPS, D // N_CHIPS, D)
    g4["w_branch_fox"] = _cols_to_shards(_wgrad(of, dbf, "dw_bf", bt=1088)[0])
    g4["w_branch_mla"] = _cols_to_shards(_wgrad(om, dbm, "dw_bm", bt=1088)[0])
    G_FFN2 = ["ffn2_w_gu", "ffn2_w_down"]
    (dfq, dfk, dfv, dc0, dc1), touts = _attn_bwd(fq, fk, fv, dof, bias, B, LP, False, 64 ** -0.5, "fox_bwd", scatter(G_FFN2))
    if dist:
        scattered(G_FFN2, touts[0], me)
    (dqm, dkm, dvm), touts = _attn_bwd(qm, km, vm, dom, None, B, LP, True, MLA_QK ** -0.5, "mla_bwd",
                                       scatter(G_OUT) + join(G_FFN2))
    if dist:
        scattered(G_OUT, touts[0], me)
        joined(G_FFN2, touts[1])
    dc = jnp.concatenate([dc0, dc1], axis=1).reshape(B * NH, LP)
    dlf_rows = _forget_scan(dc, True, "forget_scan_bwd")
    dlf = dlf_rows.reshape(B, NH, LP).transpose(0, 2, 1).reshape(T, NH)
    dlf = jnp.concatenate([jnp.zeros((T, L_FL), F32), dlf, jnp.zeros((T, 128 - L_FL - NH), F32)], axis=1)
    (dlo, dgfq, dgfk, gv["mla_cq_norm"], gv["mla_ckv_norm"], dgmq, dgmk, dbfv,
     dwuq, dwuk, dwuv) = _prep_bwd(proj, rc, rs, *small, dfq, dfk, dfv, dqm, dkm, dvm, dlf, name="prep_bwd")
    gv["fox_q_norm"] = dgfq[:, :64] + dgfq[:, 64:]
    gv["fox_k_norm"] = dgfk[:, :64] + dgfk[:, 64:]
    gv["mla_q_norm"] = dgmq[:, :MLA_QK]
    gv["mla_k_norm"] = dgmk[:, :MLA_QK]
    gv["b_forget"] = dbfv[:, L_FL:L_FL + NH]
    dwin = jnp.concatenate([_wgrad(u2, dlo, "dw_in_lo")[0], _wgrad(u2, dgl, "dw_in_hi")[0]], axis=1)
    g4["w_in"] = _cols_to_shards(_win_from_kernel(dwin))
    g4["mla_w_uq"] = _cols_to_shards(
        dwuq.astype(GRAD_DTYPE).reshape(QR, NH, 128)[:, :, :MLA_QK].reshape(QR, NH * MLA_QK))
    dukv = jnp.concatenate([dwuk.reshape(KVR, NH, 128)[:, :, :64], dwuv.reshape(KVR, NH, 64)], axis=2)
    g4["mla_w_ukv"] = _cols_to_shards(dukv.astype(GRAD_DTYPE).reshape(KVR, NH * 128))
    dh1, gv["mix_norm"] = _inproj_bwd(h1, dh2, dlo, dgl, vec["mix_norm"], wm["w_in"], "inproj_bwd")

    (dh0, u1, a1, dgp1, gv["ffn1_norm"]), touts = _ffn_bwd(h0, dh1, vec["ffn1_norm"], w1gu, w1d, "ffn1_bwd",
                                                            scatter(G_IN) + join(G_OUT))
    if dist:
        scattered(G_IN, touts[0], me)
        joined(G_OUT, touts[1])
    dh0 = dh0.reshape(B, LP, D)
    grad_x = dh0[:, MPAD:]
    grad_meta = jnp.sum(dh0[:, :NMETA], axis=0)
    share = [_share_task([_stack_vectors([gv[n] for n in VECS]), grad_meta, lpart])] if dist else []
    g4["ffn1_w_gu"], touts = _wgrad(u1, dgp1, "ffn1_dwgu", bn=DFF, shards=2, tasks=share)
    shared = touts[0] if dist else None
    dwd1, touts = _wgrad(a1, dh1, "ffn1_dwd", scale=0.5, bk=FH, tasks=scatter(["ffn1_w_gu"]) + join(G_IN))
    g4["ffn1_w_down"] = dwd1.reshape(N_CHIPS, DFF // N_CHIPS, D)
    if dist:
        scattered(["ffn1_w_gu"], touts[0], me)
        joined(G_IN, touts[1])
        scattered(["ffn1_w_down"], _run_tasks(scatter(["ffn1_w_down"]), "rs_ffn1_w_down")[0], me)
        joined(G_FFN1, _run_tasks(join(G_FFN1), "rs_join_ffn1")[0])
    return lpart, grad_x, grad_meta, gv, (red if dist else g4), shared


def _cols_from_shards(g4):
    n, r, c = g4.shape
    return g4.transpose(1, 0, 2).reshape(r, n * c)


def _cols_to_shards(full):
    r, c4 = full.shape
    return full.reshape(r, N_CHIPS, c4 // N_CHIPS).transpose(1, 0, 2)


def _win_to_kernel(wfull):
    z = lambda n: jnp.zeros((D, n), wfull.dtype)
    fl, cq, ckv, kr, gate = (wfull[:, 1536:1544], wfull[:, 1544:1800], wfull[:, 1800:1928],
                             wfull[:, 1928:1960], wfull[:, 1960:4008])
    misc = jnp.concatenate([z(L_KR), kr, fl, z(128 - L_FL - NH)], axis=1)
    return jnp.concatenate([wfull[:, :1536], cq, ckv, misc, gate], axis=1)


def _win_from_kernel(gk):
    m = C_MISC
    return jnp.concatenate([gk[:, :1536], gk[:, m + L_FL:m + L_FL + NH], gk[:, C_CQ:C_CQ + QR],
                            gk[:, C_CKV:C_CKV + KVR], gk[:, m + L_KR:m + L_KR + ROPE], gk[:, C_GATE:]], axis=1)


def _pieces(g4):
    n, r, c = g4.shape
    return g4.reshape(2 * n, r // 2, c)


def _mixer_weights(gath):
    w = {}
    w["w_in"] = _win_to_kernel(_cols_from_shards(gath["w_in"]))
    uq = _cols_from_shards(gath["mla_w_uq"]).reshape(QR, NH, MLA_QK)
    w["wuq"] = jnp.pad(uq, ((0, 0), (0, 0), (0, 128 - MLA_QK))).reshape(QR, NH * 128)
    ukv = _cols_from_shards(gath["mla_w_ukv"]).reshape(KVR, NH, 128)
    w["wuk"] = jnp.pad(ukv[:, :, :64], ((0, 0), (0, 0), (0, 64))).reshape(KVR, NH * 128)
    w["wuv"] = ukv[:, :, 64:].reshape(KVR, NH * 64)
    w["wbf"] = _cols_from_shards(gath["w_branch_fox"])
    w["wbm"] = _cols_from_shards(gath["w_branch_mla"])
    w["w_out"] = gath["w_out"].reshape(D, D)
    return w


def _chip_peers(x, y):
    return [(1 - x, y), (x, 1 - y), (1 - x, 1 - y)]


RELS = [(dx, dy, dc) for dx in (0, 1) for dy in (0, 1) for dc in (0, 1)][1:]


def _here():
    return lax.axis_index("x"), lax.axis_index("y"), lax.axis_index("c")


def _flip(a, d):
    return (1 - a) if d else a


def _remote(src, dst, send, recv, i, dev):
    return functools.partial(pltpu.make_async_remote_copy, src_ref=src, dst_ref=dst, send_sem=send.at[i],
                             recv_sem=recv.at[i], device_id=dev, device_id_type=MESH)


def _gather_task(shards):
    n = len(shards)

    def descs(ins, outs, sems):
        send, recv, loc = sems
        x, y, c = _here()
        j = 2 * x + y
        locs, pairs = [], []
        for k in range(n):
            locs.append(functools.partial(pltpu.make_async_copy, ins[k], outs[k].at[j], loc.at[k]))
            for r, (px, py) in enumerate(_chip_peers(x, y)):
                dev = (px, py, c)
                pairs.append((_remote(ins[k], outs[k].at[j], send, recv, 3 * k + r, dev),
                              _remote(ins[k], outs[k].at[2 * px + py], send, recv, 3 * k + r, dev)))
        return locs, pairs

    return _Task(shards, [jax.ShapeDtypeStruct((N_CHIPS,) + s.shape, s.dtype) for s in shards],
                 [pltpu.SemaphoreType.DMA((3 * n,)), pltpu.SemaphoreType.DMA((3 * n,)), pltpu.SemaphoreType.DMA((n,))],
                 descs)


class _SplitGather(_Task):
    PARTS = 2

    def __init__(self, shards):
        n = 3 * len(shards) * self.PARTS
        dma = pltpu.SemaphoreType.DMA
        super().__init__(shards, [jax.ShapeDtypeStruct((N_CHIPS,) + s.shape, s.dtype) for s in shards],
                         [dma((n,)), dma((n,)), dma((n,)), dma((n,)), dma((len(shards),))], None)

    def _plan(self, ins, outs, sems):
        send, recv, fsend, frecv, loc = sems
        x, y, c = _here()
        j = 2 * x + y
        locs, first, passed = [], [], []
        for k in range(len(ins)):
            h = self.ins[k].shape[0] // 2
            parts = self.PARTS if h % (32 * self.PARTS) == 0 else 1
            hp = h // parts
            locs.append(functools.partial(pltpu.make_async_copy, ins[k], outs[k].at[j], loc.at[k]))
            for r, (px, py) in enumerate(_chip_peers(x, y)):
                p = 2 * px + py
                for q in range(parts):
                    i = (3 * k + r) * self.PARTS + q
                    mine = pl.ds(pl.multiple_of(c * h + q * hp, 8), hp)
                    theirs = pl.ds(pl.multiple_of((1 - c) * h + q * hp, 8), hp)
                    first.append((_remote(ins[k].at[mine], outs[k].at[j, mine], send, recv, i, (px, py, c)),
                                  _remote(ins[k].at[mine], outs[k].at[p, mine], send, recv, i, (px, py, c))))
                    passed.append((_remote(outs[k].at[p, mine], outs[k].at[p, mine], fsend, frecv, i, (x, y, 1 - c)),
                                   _remote(outs[k].at[p, mine], outs[k].at[p, theirs], fsend, frecv, i, (x, y, 1 - c))))
        return locs, first, passed

    def start(self, ins, outs, sems):
        locs, first, _ = self._plan(ins, outs, sems)
        for lc in locs:
            lc().start()
        for snd, _ in first:
            snd().start()

    def wait(self, ins, outs, sems):
        locs, first, passed = self._plan(ins, outs, sems)
        for (_, landed), (pass_on, _) in zip(first, passed):
            landed().wait_recv()
            pass_on().start()
        for _, rcv in passed:
            rcv().wait_recv()
        for snd, _ in first + passed:
            snd().wait_send()
        for lc in locs:
            lc().wait()


def _a2a_task(ps):
    n = len(ps)
    nr = len(RELS)

    def descs(ins, outs, sems):
        send, recv = sems
        x, y, c = _here()
        me = 4 * x + 2 * y + c
        pairs = []
        for k in range(n):
            for i, (dx, dy, dc) in enumerate(RELS):
                dev = (_flip(x, dx), _flip(y, dy), _flip(c, dc))
                peer = 4 * dev[0] + 2 * dev[1] + dev[2]
                pairs.append((_remote(ins[k].at[peer], outs[k].at[me], send, recv, nr * k + i, dev),
                              _remote(ins[k].at[peer], outs[k].at[peer], send, recv, nr * k + i, dev)))
        return [], pairs

    return _Task(ps, [jax.ShapeDtypeStruct(p.shape, p.dtype) for p in ps],
                 [pltpu.SemaphoreType.DMA((nr * n,)), pltpu.SemaphoreType.DMA((nr * n,))], descs)


def _share_task(vs):
    n = len(vs)
    nr = len(RELS)

    def descs(ins, outs, sems):
        send, recv, loc = sems
        x, y, c = _here()
        me = 4 * x + 2 * y + c
        locs, pairs = [], []
        for k in range(n):
            locs.append(functools.partial(pltpu.make_async_copy, ins[k], outs[k].at[me], loc.at[k]))
            for i, (dx, dy, dc) in enumerate(RELS):
                dev = (_flip(x, dx), _flip(y, dy), _flip(c, dc))
                peer = 4 * dev[0] + 2 * dev[1] + dev[2]
                pairs.append((_remote(ins[k], outs[k].at[me], send, recv, nr * k + i, dev),
                              _remote(ins[k], outs[k].at[peer], send, recv, nr * k + i, dev)))
        return locs, pairs

    return _Task(vs, [jax.ShapeDtypeStruct((N_DEV,) + v.shape, v.dtype) for v in vs],
                 [pltpu.SemaphoreType.DMA((nr * n,)), pltpu.SemaphoreType.DMA((nr * n,)), pltpu.SemaphoreType.DMA((n,))],
                 descs)


def _join_task(ss):
    n = len(ss)

    def descs(ins, outs, sems):
        send, recv = sems
        x, y, c = _here()
        pairs = []
        for k in range(n):
            cp = _remote(ins[k], outs[k], send, recv, k, (x, y, 1 - c))
            pairs.append((cp, cp))
        return [], pairs

    return _Task(ss, [jax.ShapeDtypeStruct(s.shape, s.dtype) for s in ss],
                 [pltpu.SemaphoreType.DMA((n,)), pltpu.SemaphoreType.DMA((n,))], descs)


def _sum_pieces(recv, own, me, name):
    n, h, c = recv.shape
    tr = h

    def body(me_ref, r_ref, o_ref, out_ref):
        s = pl.program_id(1)
        val = jnp.where(s == me_ref[0], o_ref[0], r_ref[0]).astype(F32)

        @pl.when(s == 0)
        def _():
            out_ref[...] = val

        @pl.when(s > 0)
        def _():
            out_ref[...] += val

    def other(s, m):
        return jnp.where(s == m[0], (s + 1) % n, s)

    return _call(
        body, name=name,
        grid_spec=pltpu.PrefetchScalarGridSpec(
            num_scalar_prefetch=1, grid=(h // tr, n),
            in_specs=[pl.BlockSpec((1, tr, c), lambda i, s, m: (other(s, m), i, 0)),
                      pl.BlockSpec((1, tr, c), lambda i, s, m: (m[0], i, 0))],
            out_specs=pl.BlockSpec((tr, c), lambda i, s, m: (i, 0))),
        out_shape=jax.ShapeDtypeStruct((h, c), F32),
        compiler_params=_cp(),
    )(me, recv, own)


def _adamw_update(gg, w, m, v):
    c1 = 1.0 / (1.0 - ADAM_B1 ** ADAM_STEP)
    c2 = 1.0 / (1.0 - ADAM_B2 ** ADAM_STEP)
    nm = ADAM_B1 * m + (1.0 - ADAM_B1) * gg
    nv = ADAM_B2 * v + (1.0 - ADAM_B2) * (gg * gg)
    return -ADAM_LR * ((nm * c1) / (jnp.sqrt(nv * c2) + ADAM_EPS) + ADAM_WD * w), nm, nv


def _adamw_small(gvec8, gmeta8, lp8, chip, ws, ms, vs, name):
    na = len(ws)

    def dev_sum(ref):
        acc = ref[0]
        for s in range(1, N_DEV):
            acc = acc + ref[s]
        return acc

    def body(c_ref, gv_ref, gm_ref, lp_ref, *refs):
        w_refs, m_refs, v_refs = refs[:na], refs[na:2 * na], refs[2 * na:3 * na]
        l_ref = refs[3 * na]
        outs = refs[3 * na + 1:]
        g_refs, d_refs, nm_refs, nv_refs = outs[:na], outs[na:2 * na], outs[2 * na:3 * na], outs[3 * na:]
        l_ref[...] = dev_sum(lp_ref)
        gvec = dev_sum(gv_ref)
        for k in range(na):
            gg = gvec[k:k + 1, 0:ws[k].shape[1]] if k < na - 1 else dev_sum(gm_ref)
            g_refs[k][...] = gg
            d_refs[k][...], nm_refs[k][...], nv_refs[k][...] = _adamw_update(gg, w_refs[k][...], m_refs[k][...], v_refs[k][...])

    whole = lambda a: pl.BlockSpec(a.shape, lambda i, c: (0,) * a.ndim)
    arrs = list(ws) + list(ms) + list(vs)
    res = _call(
        body, name=name,
        grid_spec=pltpu.PrefetchScalarGridSpec(
            num_scalar_prefetch=1, grid=(1,),
            in_specs=[whole(gvec8), pl.BlockSpec((N_DEV, NMETA, D // N_CHIPS), lambda i, c: (0, 0, c[0])), whole(lp8)]
                     + [whole(a) for a in arrs],
            out_specs=[pl.BlockSpec((1, D), lambda i, c: (0, 0))] + [whole(a) for a in ws] * 4),
        out_shape=[jax.ShapeDtypeStruct((1, D), F32)] + [jax.ShapeDtypeStruct(a.shape, F32) for a in ws] * 4,
        compiler_params=_cp(),
    )(chip, gvec8, gmeta8, lp8, *arrs)
    return res[0], [list(res[1 + i * na:1 + (i + 1) * na]) for i in range(4)]


def _adamw_halves(wt, mine, theirs, m, v, core, name):
    r, c = wt.shape
    h = r // 2
    tr = _tile(h, (256, 176, 128, 64))
    nh = h // tr

    def body(c_ref, w_ref, a_ref, b_ref, m_ref, v_ref, g_ref, d_ref, nm_ref, nv_ref):
        gg = jnp.where(pl.program_id(0) // nh == c_ref[0], a_ref[...], b_ref[...])
        g_ref[...] = gg
        d_ref[...], nm_ref[...], nv_ref[...] = _adamw_update(gg, w_ref[...], m_ref[...], v_ref[...])

    full = pl.BlockSpec((tr, c), lambda i, cr: (i, 0))
    half = pl.BlockSpec((tr, c), lambda i, cr: (i % nh, 0))
    return _call(
        body, name=name,
        grid_spec=pltpu.PrefetchScalarGridSpec(
            num_scalar_prefetch=1, grid=(2 * nh,),
            in_specs=[full, half, half, full, full], out_specs=[full] * 4),
        out_shape=[jax.ShapeDtypeStruct((r, c), F32)] * 4,
        compiler_params=_cp(),
    )(core, wt, mine, theirs, m, v)


MATS = ["ffn1_w_gu", "ffn1_w_down", "w_in", "mla_w_uq", "mla_w_ukv", "w_branch_fox", "w_branch_mla",
        "w_out", "ffn2_w_gu", "ffn2_w_down"]
VECS = ["ffn1_norm", "mix_norm", "b_forget", "b_gate", "fox_q_norm", "fox_k_norm", "mla_cq_norm",
        "mla_ckv_norm", "mla_q_norm", "mla_k_norm", "ffn2_norm"]
WEIGHTS = ["meta_tokens", "ffn1_norm", "ffn1_w_gu", "ffn1_w_down", "mix_norm", "w_in", "b_forget", "b_gate",
           "fox_q_norm", "fox_k_norm", "mla_cq_norm", "mla_w_uq", "mla_ckv_norm", "mla_w_ukv", "mla_q_norm",
           "mla_k_norm", "w_branch_fox", "w_branch_mla", "w_out", "ffn2_norm", "ffn2_w_gu", "ffn2_w_down"]


VEC_LANES = 2048


def _stack_vectors(parts):
    rows = [_pad_lanes(p, 0, VEC_LANES) for p in parts]
    rows.append(jnp.zeros((-len(parts) % 8, VEC_LANES), F32))
    return jnp.concatenate(rows, axis=0)


def kernel(x, meta_tokens, ffn1_norm, ffn1_w_gu, ffn1_w_down, mix_norm, w_in, b_forget, b_gate, fox_q_norm, fox_k_norm, mla_cq_norm, mla_w_uq, mla_ckv_norm, mla_w_ukv, mla_q_norm, mla_k_norm, w_branch_fox, w_branch_mla, w_out, ffn2_norm, ffn2_w_gu, ffn2_w_down, loss_target, m_meta_tokens, m_ffn1_norm, m_ffn1_w_gu, m_ffn1_w_down, m_mix_norm, m_w_in, m_b_forget, m_b_gate, m_fox_q_norm, m_fox_k_norm, m_mla_cq_norm, m_mla_w_uq, m_mla_ckv_norm, m_mla_w_ukv, m_mla_q_norm, m_mla_k_norm, m_w_branch_fox, m_w_branch_mla, m_w_out, m_ffn2_norm, m_ffn2_w_gu, m_ffn2_w_down, v_meta_tokens, v_ffn1_norm, v_ffn1_w_gu, v_ffn1_w_down, v_mix_norm, v_w_in, v_b_forget, v_b_gate, v_fox_q_norm, v_fox_k_norm, v_mla_cq_norm, v_mla_w_uq, v_mla_ckv_norm, v_mla_w_ukv, v_mla_q_norm, v_mla_k_norm, v_w_branch_fox, v_w_branch_mla, v_w_out, v_ffn2_norm, v_ffn2_w_gu, v_ffn2_w_down):
    a = dict(locals())
    wts = {n: a[n] for n in WEIGHTS}
    ms = {n: a["m_" + n] for n in WEIGHTS}
    vs = {n: a["v_" + n] for n in WEIGHTS}
    cx, cy, cc = lax.axis_index("x"), lax.axis_index("y"), lax.axis_index("c")
    chip = 2 * cx + cy

    shards = {n: wts[n][0].astype(BF16) for n in MATS}
    first = _run_tasks([_SplitGather([shards[n] for n in G_FFN1] + [meta_tokens])], "gather_ffn1")[0]
    gath = dict(zip(G_FFN1, first[:-1]))
    meta_full = _cols_from_shards(first[-1])

    _, grad_x, _, _, gred, (gvec8, gmeta8, lp8) = _step(x, loss_target, meta_full, {n: wts[n] for n in VECS}, gath, shards)

    sm_names = VECS + ["meta_tokens"]
    lsum, sm = _adamw_small(gvec8, gmeta8, lp8, chip.reshape(1).astype(jnp.int32), [wts[n] for n in sm_names],
                            [ms[n] for n in sm_names], [vs[n] for n in sm_names], "adamw_small")
    loss = jnp.sum(lsum)

    grads, delta, new_m, new_v = {}, {}, {}, {}
    core = cc.reshape(1).astype(jnp.int32)
    for n in MATS:
        shp = wts[n].shape
        mine, theirs = gred[n]
        res = _adamw_halves(wts[n][0], mine, theirs, ms[n][0], vs[n][0], core, "adamw_" + n)
        grads[n], delta[n], new_m[n], new_v[n] = (t.reshape(shp) for t in res)
    for k, n in enumerate(sm_names):
        grads[n], delta[n], new_m[n], new_v[n] = (sm[i][k] for i in range(4))

    return (loss, grad_x, *[grads[n] for n in WEIGHTS], *[delta[n] for n in WEIGHTS],
            *[new_m[n] for n in WEIGHTS], *[new_v[n] for n in WEIGHTS])
```

```python
import functools

import jax
import jax.numpy as jnp
from jax import lax
from jax.experimental import pallas as pl
from jax.experimental.pallas import tpu as pltpu

F32 = jnp.float32
BF16 = jnp.bfloat16
MESH = pl.DeviceIdType.MESH

D = 1024
DFF = 2816
FH = DFF // 2
NMETA = 16
MPAD = 128
EPS = 1e-6
NH = 8
FOXW = 512
QR = 256
KVR = 128
ROPE = 32
MLA_QK = 96
PROJW = 4096
ROPE_THETA = 10000.0
N_CHIPS = 4
N_DEV = 8

ADAM_LR = 0.001
ADAM_B1 = 0.9
ADAM_B2 = 0.999
ADAM_EPS = 1e-08
ADAM_WD = 0.01
ADAM_STEP = 10

VMEM_LIMIT = 56 * 2**20
ATTN_TQ = 256
GRAD_DTYPE = BF16

NT = (((1,), (1,)), ((), ()))
TN = (((0,), (0,)), ((), ()))


def _call(body, **kw):
    return pl.pallas_call(body, **kw)


def _cp(**kw):
    return pltpu.CompilerParams(vmem_limit_bytes=VMEM_LIMIT, **kw)


HBM = pl.BlockSpec(memory_space=pltpu.HBM)


class _Task:
    def __init__(self, ins, out_shapes, sems, descs):
        self.ins, self.out_shapes, self.sems, self.descs = list(ins), list(out_shapes), list(sems), descs

    def start(self, ins, outs, sems):
        locs, pairs = self.descs(ins, outs, sems)
        for lc in locs:
            lc().start()
        for snd, _ in pairs:
            snd().start()

    def wait(self, ins, outs, sems):
        locs, pairs = self.descs(ins, outs, sems)
        for _, rcv in pairs:
            rcv().wait_recv()
        for snd, _ in pairs:
            snd().wait_send()
        for lc in locs:
            lc().wait()


def _call_tasks(body, tasks, *, name, grid, in_specs, out_specs, out_shape, args, scratch_shapes=()):
    in_specs, out_specs, out_shape, scratch_shapes = map(list, (in_specs, out_specs, out_shape, scratch_shapes))
    n_in, n_out, n_sc = len(in_specs), len(out_specs), len(scratch_shapes)
    t_in = [len(t.ins) for t in tasks]
    t_out = [len(t.out_shapes) for t in tasks]
    t_sem = [len(t.sems) for t in tasks]

    def wrapped(*refs):
        pos = [0]

        def take(n):
            pos[0] += n
            return refs[pos[0] - n:pos[0]]

        ins, tins = take(n_in), [take(n) for n in t_in]
        outs, touts = take(n_out), [take(n) for n in t_out]
        sc, tsems = take(n_sc), [take(n) for n in t_sem]
        if tasks:
            first = functools.reduce(jnp.logical_and, [pl.program_id(a) == 0 for a in range(len(grid))])
            last = functools.reduce(jnp.logical_and, [pl.program_id(a) == grid[a] - 1 for a in range(len(grid))])

            @pl.when(first)
            def _():
                for t, a, b, s in zip(tasks, tins, touts, tsems):
                    t.start(a, b, s)

        body(*ins, *outs, *sc)
        if tasks:
            @pl.when(last)
            def _():
                for t, a, b, s in zip(tasks, tins, touts, tsems):
                    t.wait(a, b, s)

    res = _call(
        wrapped, name=name, grid=grid,
        in_specs=in_specs + [HBM] * sum(t_in), out_specs=out_specs + [HBM] * sum(t_out),
        out_shape=out_shape + [s for t in tasks for s in t.out_shapes],
        scratch_shapes=scratch_shapes + [s for t in tasks for s in t.sems],
        compiler_params=_cp(),
    )(*args, *[a for t in tasks for a in t.ins])
    res = list(res)
    touts, pos = [], n_out
    for n in t_out:
        touts.append(res[pos:pos + n])
        pos += n
    return res[:n_out], touts


def _run_tasks(tasks, name):
    t_in = [len(t.ins) for t in tasks]
    t_out = [len(t.out_shapes) for t in tasks]
    t_sem = [len(t.sems) for t in tasks]

    def body(*refs):
        pos = [0]

        def take(n):
            pos[0] += n
            return refs[pos[0] - n:pos[0]]

        tins, touts, tsems = [take(n) for n in t_in], [take(n) for n in t_out], [take(n) for n in t_sem]
        for t, a, b, s in zip(tasks, tins, touts, tsems):
            t.start(a, b, s)
        for t, a, b, s in zip(tasks, tins, touts, tsems):
            t.wait(a, b, s)

    res = list(_call(
        body, name=name, in_specs=[HBM] * sum(t_in), out_specs=[HBM] * sum(t_out),
        out_shape=[s for t in tasks for s in t.out_shapes],
        scratch_shapes=[s for t in tasks for s in t.sems],
    )(*[a for t in tasks for a in t.ins]))
    touts, pos = [], 0
    for n in t_out:
        touts.append(res[pos:pos + n])
        pos += n
    return touts


def _tile(n, cands):
    for c in cands:
        if n % c == 0:
            return c
    raise ValueError(f"no tile for {n} among {cands}")


def _dot(a, b, dims=None):
    if dims is None:
        return jnp.dot(a, b, preferred_element_type=F32)
    return lax.dot_general(a, b, dims, preferred_element_type=F32)


def _lane(shape):
    return lax.broadcasted_iota(jnp.int32, shape, len(shape) - 1)


def _seg_ones(w, log2_seg):
    r = lax.shift_right_logical(lax.broadcasted_iota(jnp.int32, (w, w), 0), log2_seg)
    c = lax.shift_right_logical(lax.broadcasted_iota(jnp.int32, (w, w), 1), log2_seg)
    return jnp.where(r == c, 1.0, 0.0).astype(BF16)


def _seg_sum(x, ones):
    hi = x.astype(BF16)
    r1 = x - hi.astype(F32)
    mid = r1.astype(BF16)
    lo = (r1 - mid.astype(F32)).astype(BF16)
    return _dot(hi, ones) + _dot(mid, ones) + _dot(lo, ones)


def _lane_sum(x, ones):
    return jnp.sum(x, axis=-1, keepdims=True) if ones is None else _seg_sum(x, ones)


def _rms(x, gain, n, ones=None):
    r = lax.rsqrt(_lane_sum(x * x, ones) * (1.0 / n) + EPS)
    xh = x * r
    return xh * gain, xh, r


def _rms_bwd(dy, xh, r, gain, n, ones=None):
    dxh = dy * gain
    return r * (dxh - xh * (_lane_sum(dxh * xh, ones) * (1.0 / n)))


def _rope_swap(x):
    ln = _lane(x.shape)
    sw = jnp.where(ln < 80, pltpu.roll(x, 112, 1), pltpu.roll(x, 16, 1))
    return jnp.where(jnp.logical_and(ln >= 64, ln < 96), sw, 0.0)


def _colsum(x):
    return jnp.sum(x, axis=0, keepdims=True)


def _ffn_weight_specs():
    once = pl.Buffered(1)
    return [pl.BlockSpec((D, DFF), lambda i: (0, 0), pipeline_mode=once),
            pl.BlockSpec((D, DFF), lambda i: (0, 1), pipeline_mode=once),
            pl.BlockSpec((DFF, D), lambda i: (0, 0), pipeline_mode=once)]


def _ffn_fwd(h, norm, wgu, wd, name, tasks=()):
    T = h.shape[0]
    tm = _tile(T, (512, 384, 256, 128))

    def body(h_ref, n_ref, wg_ref, wu_ref, wd_ref, o_ref):
        x = h_ref[...]
        u, _, _ = _rms(x, n_ref[...], D)
        ub = u.astype(BF16)
        g = _dot(ub, wg_ref[...])
        p = _dot(ub, wu_ref[...])
        a = (g * jax.nn.sigmoid(g)) * p
        o_ref[...] = x + 0.5 * _dot(a.astype(BF16), wd_ref[...])

    (out,), touts = _call_tasks(
        body, tasks, name=name, grid=(T // tm,),
        in_specs=[pl.BlockSpec((tm, D), lambda i: (i, 0)), pl.BlockSpec((1, D), lambda i: (0, 0))] + _ffn_weight_specs(),
        out_specs=[pl.BlockSpec((tm, D), lambda i: (i, 0))],
        out_shape=[jax.ShapeDtypeStruct((T, D), F32)],
        args=(h, norm, wgu, wgu, wd))
    return out, touts


def _ffn_bwd(h, dout, norm, wgu, wd, name, tasks=()):
    T = h.shape[0]
    tm = _tile(T, (256, 128))

    def body(h_ref, d_ref, n_ref, wg_ref, wu_ref, wd_ref, dh_ref, u_ref, a_ref, dgp_ref, dn_ref):
        @pl.when(pl.program_id(0) == 0)
        def _():
            dn_ref[...] = jnp.zeros_like(dn_ref)

        u, xh, r = _rms(h_ref[...], n_ref[...], D)
        ub = u.astype(BF16)
        u_ref[...] = ub
        g = _dot(ub, wg_ref[...])
        p = _dot(ub, wu_ref[...])
        s = jax.nn.sigmoid(g)
        sl = g * s
        dz = (0.5 * d_ref[...]).astype(BF16)
        da = _dot(dz, wd_ref[...], NT)
        dp = da * sl
        dg = (da * p) * (s * (1.0 + g * (1.0 - s)))
        a_ref[...] = (sl * p).astype(BF16)
        dgb = dg.astype(BF16)
        dpb = dp.astype(BF16)
        dgp_ref[:, :DFF] = dgb
        dgp_ref[:, DFF:] = dpb
        du = _dot(dgb, wg_ref[...], NT) + _dot(dpb, wu_ref[...], NT)
        dn_ref[...] += _colsum(du * xh)
        dh_ref[...] = d_ref[...] + _rms_bwd(du, xh, r, n_ref[...], D)

    row = lambda w: pl.BlockSpec((tm, w), lambda i: (i, 0))
    return _call_tasks(
        body, tasks, name=name, grid=(T // tm,),
        in_specs=[row(D), row(D), pl.BlockSpec((1, D), lambda i: (0, 0))] + _ffn_weight_specs(),
        out_specs=[row(D), row(D), row(DFF), row(2 * DFF), pl.BlockSpec((1, D), lambda i: (0, 0))],
        out_shape=[jax.ShapeDtypeStruct((T, D), F32),
                   jax.ShapeDtypeStruct((T, D), BF16),
                   jax.ShapeDtypeStruct((T, DFF), BF16),
                   jax.ShapeDtypeStruct((T, 2 * DFF), BF16),
                   jax.ShapeDtypeStruct((1, D), F32)],
        args=(h, dout, norm, wgu, wgu, wd))


def _wgrad(x, y, name, scale=1.0, bk=None, bn=None, shards=0, bt=512, tasks=()):
    T, K = x.shape
    N = y.shape[1]
    bk = bk or K
    bn = bn or N
    bt = _tile(T, (bt, 512, 384, 256, 128))
    nt = T // bt

    def body(x_ref, y_ref, o_ref, acc_ref):
        t = pl.program_id(2)

        @pl.when(t == 0)
        def _():
            acc_ref[...] = jnp.zeros_like(acc_ref)

        acc_ref[...] += _dot(x_ref[...].astype(BF16), y_ref[...].astype(BF16), TN)

        @pl.when(t == nt - 1)
        def _():
            res = (acc_ref[...] * scale).astype(o_ref.dtype)
            if shards:
                w = bn // shards
                for s in range(shards):
                    o_ref[s] = res[:, s * w:(s + 1) * w]
            else:
                o_ref[...] = res

    if shards:
        assert bk == K
        out_spec = pl.BlockSpec((shards, K, bn // shards), lambda i, j, t: (j, 0, 0))
        out_shape = jax.ShapeDtypeStruct((N * shards // bn, K, bn // shards), GRAD_DTYPE)
    else:
        out_spec = pl.BlockSpec((bk, bn), lambda i, j, t: (i, j))
        out_shape = jax.ShapeDtypeStruct((K, N), GRAD_DTYPE)
    (out,), touts = _call_tasks(
        body, tasks, name=name, grid=(K // bk, N // bn, nt),
        in_specs=[pl.BlockSpec((bt, bk), lambda i, j, t: (t, i)),
                  pl.BlockSpec((bt, bn), lambda i, j, t: (t, j))],
        out_specs=[out_spec], out_shape=[out_shape],
        scratch_shapes=[pltpu.VMEM((bk, bn), F32)],
        args=(x, y))
    return out, touts


def _inproj_fwd(h, norm, w, name):
    T = h.shape[0]
    tm = _tile(T, (1088, 512, 384, 256, 128))
    tn = 1024

    def body(h_ref, n_ref, w_ref, o_ref, u_ref):
        @pl.when(pl.program_id(1) == 0)
        def _():
            u, _, _ = _rms(h_ref[...], n_ref[...], D)
            u_ref[...] = u.astype(BF16)

        o_ref[...] = _dot(u_ref[...], w_ref[...])

    return _call(
        body, name=name, grid=(T // tm, PROJW // tn),
        in_specs=[pl.BlockSpec((tm, D), lambda i, j: (i, 0)),
                  pl.BlockSpec((1, D), lambda i, j: (0, 0)),
                  pl.BlockSpec((D, tn), lambda i, j: (0, j))],
        out_specs=[pl.BlockSpec((tm, tn), lambda i, j: (i, j)),
                   pl.BlockSpec((tm, D), lambda i, j: (i, 0))],
        out_shape=[jax.ShapeDtypeStruct((T, PROJW), F32), jax.ShapeDtypeStruct((T, D), BF16)],
        compiler_params=_cp(),
    )(h, norm, w)


def _inproj_bwd(h, dres, dlo, dhi, norm, w, name):
    T = h.shape[0]
    tm = _tile(T, (512, 384, 256, 128))
    hw = PROJW // 2

    def body(h_ref, d_ref, lo_ref, hi_ref, n_ref, wlo_ref, whi_ref, dh_ref, dn_ref):
        @pl.when(pl.program_id(0) == 0)
        def _():
            dn_ref[...] = jnp.zeros_like(dn_ref)

        _, xh, r = _rms(h_ref[...], n_ref[...], D)
        du = _dot(lo_ref[...], wlo_ref[...], NT) + _dot(hi_ref[...], whi_ref[...], NT)
        dn_ref[...] += _colsum(du * xh)
        dh_ref[...] = d_ref[...] + _rms_bwd(du, xh, r, n_ref[...], D)

    return _call(
        body, name=name, grid=(T // tm,),
        in_specs=[pl.BlockSpec((tm, D), lambda i: (i, 0)),
                  pl.BlockSpec((tm, D), lambda i: (i, 0)),
                  pl.BlockSpec((tm, hw), lambda i: (i, 0)),
                  pl.BlockSpec((tm, hw), lambda i: (i, 0)),
                  pl.BlockSpec((1, D), lambda i: (0, 0)),
                  pl.BlockSpec((D, hw), lambda i: (0, 0)),
                  pl.BlockSpec((D, hw), lambda i: (0, 1))],
        out_specs=[pl.BlockSpec((tm, D), lambda i: (i, 0)),
                   pl.BlockSpec((1, D), lambda i: (0, 0))],
        out_shape=[jax.ShapeDtypeStruct((T, D), F32), jax.ShapeDtypeStruct((1, D), F32)],
        compiler_params=_cp(),
    )(h, dres, dlo, dhi, norm, w, w)


C_FQ, C_FK, C_FV, C_CQ, C_CKV, C_MISC, C_GATE = 0, 512, 1024, 1536, 1792, 1920, 2048
L_KR, L_FL = 64, 96


def _prep_fwd(proj, rc, rs, gfq, gfk, gcq, gckv, gmq, gmk, bfv, wuq, wuk, wuv, name):
    T = proj.shape[0]
    tm = _tile(T, (256, 128))

    def body(p_ref, rc_ref, rs_ref, gfq_ref, gfk_ref, gcq_ref, gckv_ref, gmq_ref, gmk_ref, bf_ref,
             wuq_ref, wuk_ref, wuv_ref, fq_ref, fk_ref, fv_ref, qm_ref, km_ref, vm_ref, lf_ref):
        o64, o128, o256 = _seg_ones(128, 6), _seg_ones(128, 7), _seg_ones(256, 8)
        for blk in range(4):
            for (c0, g_ref, o_ref) in ((C_FQ, gfq_ref, fq_ref), (C_FK, gfk_ref, fk_ref)):
                x = p_ref[:, c0 + 128 * blk:c0 + 128 * (blk + 1)]
                fn, _, _ = _rms(x, g_ref[...], 64, o64)
                o_ref[:, 128 * blk:128 * (blk + 1)] = fn.astype(BF16)
        fv_ref[...] = p_ref[:, C_FV:C_FV + 512].astype(BF16)

        rcv = rc_ref[...]
        rsv = rs_ref[...]
        cqn, _, _ = _rms(p_ref[:, C_CQ:C_CQ + QR], gcq_ref[...], QR, o256)
        qpre = _dot(cqn.astype(BF16), wuq_ref[...])
        ckvn, _, _ = _rms(p_ref[:, C_CKV:C_CKV + KVR], gckv_ref[...], KVR, o128)
        ckvb = ckvn.astype(BF16)
        kpre = _dot(ckvb, wuk_ref[...])
        vm_ref[...] = _dot(ckvb, wuv_ref[...]).astype(BF16)
        misc = p_ref[:, C_MISC:C_MISC + 128]
        ln = _lane(misc.shape)
        kr = jnp.where(jnp.logical_and(ln >= L_KR, ln < L_KR + ROPE), misc, 0.0)
        for hh in range(NH):
            sl = slice(128 * hh, 128 * (hh + 1))
            qn, _, _ = _rms(qpre[:, sl], gmq_ref[...], MLA_QK, o128)
            qm_ref[:, sl] = (qn * rcv + _rope_swap(qn) * rsv).astype(BF16)
            kn, _, _ = _rms(kpre[:, sl] + kr, gmk_ref[...], MLA_QK, o128)
            km_ref[:, sl] = (kn * rcv + _rope_swap(kn) * rsv).astype(BF16)
        z = misc + bf_ref[...]
        lf_ref[...] = jnp.minimum(z, 0.0) - jnp.log(1.0 + jnp.exp(-jnp.abs(z)))

    row = lambda w: pl.BlockSpec((tm, w), lambda i: (i, 0))
    full = lambda a: pl.BlockSpec(a.shape, lambda i: (0, 0))
    return _call(
        body, name=name, grid=(T // tm,),
        in_specs=[row(PROJW // 2), row(128), row(128)] + [full(a) for a in (gfq, gfk, gcq, gckv, gmq, gmk, bfv, wuq, wuk, wuv)],
        out_specs=[row(512), row(512), row(512), row(1024), row(1024), row(512), row(128)],
        out_shape=[jax.ShapeDtypeStruct((T, 512), BF16), jax.ShapeDtypeStruct((T, 512), BF16),
                   jax.ShapeDtypeStruct((T, 512), BF16), jax.ShapeDtypeStruct((T, 1024), BF16),
                   jax.ShapeDtypeStruct((T, 1024), BF16), jax.ShapeDtypeStruct((T, 512), BF16),
                   jax.ShapeDtypeStruct((T, 128), F32)],
        compiler_params=_cp(),
    )(proj, rc, rs, gfq, gfk, gcq, gckv, gmq, gmk, bfv, wuq, wuk, wuv)


def _prep_bwd(proj, rc, rs, gfq, gfk, gcq, gckv, gmq, gmk, bfv, wuq, wuk, wuv,
              dfq, dfk, dfv, dqm, dkm, dvm, dlf, name):
    T = proj.shape[0]
    tm = _tile(T, (256, 128))

    def body(p_ref, rc_ref, rs_ref, gfq_ref, gfk_ref, gcq_ref, gckv_ref, gmq_ref, gmk_ref, bf_ref,
             wuq_ref, wuk_ref, wuv_ref, dfq_ref, dfk_ref, dfv_ref, dqm_ref, dkm_ref, dvm_ref, dlf_ref,
             dp_ref, dgfq_ref, dgfk_ref, dgcq_ref, dgckv_ref, dgmq_ref, dgmk_ref, dbf_ref,
             dwuq_ref, dwuk_ref, dwuv_ref, dqpre_sc, dkpre_sc):
        accs = (dgfq_ref, dgfk_ref, dgcq_ref, dgckv_ref, dgmq_ref, dgmk_ref, dbf_ref, dwuq_ref, dwuk_ref, dwuv_ref)

        @pl.when(pl.program_id(0) == 0)
        def _():
            for a in accs:
                a[...] = jnp.zeros_like(a)

        o64, o128, o256 = _seg_ones(128, 6), _seg_ones(128, 7), _seg_ones(256, 8)
        for (c0, g_ref, d_ref, dg_ref) in ((C_FQ, gfq_ref, dfq_ref, dgfq_ref), (C_FK, gfk_ref, dfk_ref, dgfk_ref)):
            dg = jnp.zeros((1, 128), F32)
            for blk in range(4):
                x = p_ref[:, c0 + 128 * blk:c0 + 128 * (blk + 1)]
                _, xh, r = _rms(x, g_ref[...], 64, o64)
                dy = d_ref[:, 128 * blk:128 * (blk + 1)]
                dg = dg + _colsum(dy * xh)
                dp_ref[:, c0 + 128 * blk:c0 + 128 * (blk + 1)] = _rms_bwd(dy, xh, r, g_ref[...], 64, o64).astype(BF16)
            dg_ref[...] += dg
        dp_ref[:, C_FV:C_FV + 512] = dfv_ref[...].astype(BF16)

        rcv = rc_ref[...]
        rsv = rs_ref[...]
        cqn, cqh, cqr = _rms(p_ref[:, C_CQ:C_CQ + QR], gcq_ref[...], QR, o256)
        cqb = cqn.astype(BF16)
        qpre = _dot(cqb, wuq_ref[...])
        dgq = jnp.zeros((1, 128), F32)
        for hh in range(NH):
            sl = slice(128 * hh, 128 * (hh + 1))
            _, xh, r = _rms(qpre[:, sl], gmq_ref[...], MLA_QK, o128)
            dout = dqm_ref[:, sl]
            dqn = dout * rcv + _rope_swap(dout * rsv)
            dgq = dgq + _colsum(dqn * xh)
            dqpre_sc[:, sl] = _rms_bwd(dqn, xh, r, gmq_ref[...], MLA_QK, o128).astype(BF16)
        dgmq_ref[...] += dgq
        dqpre = dqpre_sc[...]
        dwuq_ref[...] += _dot(cqb, dqpre, TN)
        dcqn = _dot(dqpre, wuq_ref[...], NT)
        dgcq_ref[...] += _colsum(dcqn * cqh)
        dp_ref[:, C_CQ:C_CQ + QR] = _rms_bwd(dcqn, cqh, cqr, gcq_ref[...], QR, o256).astype(BF16)

        ckvn, ckvh, ckvr = _rms(p_ref[:, C_CKV:C_CKV + KVR], gckv_ref[...], KVR, o128)
        ckvb = ckvn.astype(BF16)
        kpre = _dot(ckvb, wuk_ref[...])
        misc = p_ref[:, C_MISC:C_MISC + 128]
        ln = _lane(misc.shape)
        is_kr = jnp.logical_and(ln >= L_KR, ln < L_KR + ROPE)
        kr = jnp.where(is_kr, misc, 0.0)
        dgk = jnp.zeros((1, 128), F32)
        dkr = jnp.zeros(misc.shape, F32)
        for hh in range(NH):
            sl = slice(128 * hh, 128 * (hh + 1))
            _, xh, r = _rms(kpre[:, sl] + kr, gmk_ref[...], MLA_QK, o128)
            dout = dkm_ref[:, sl]
            dkn = dout * rcv + _rope_swap(dout * rsv)
            dgk = dgk + _colsum(dkn * xh)
            dkx = _rms_bwd(dkn, xh, r, gmk_ref[...], MLA_QK, o128)
            dkr = dkr + jnp.where(is_kr, dkx, 0.0)
            dkpre_sc[:, sl] = jnp.where(ln < 64, dkx, 0.0).astype(BF16)
        dgmk_ref[...] += dgk
        dkpre = dkpre_sc[...]
        dvmb = dvm_ref[...].astype(BF16)
        dwuk_ref[...] += _dot(ckvb, dkpre, TN)
        dwuv_ref[...] += _dot(ckvb, dvmb, TN)
        dckvn = _dot(dkpre, wuk_ref[...], NT) + _dot(dvmb, wuv_ref[...], NT)
        dgckv_ref[...] += _colsum(dckvn * ckvh)
        dp_ref[:, C_CKV:C_CKV + KVR] = _rms_bwd(dckvn, ckvh, ckvr, gckv_ref[...], KVR, o128).astype(BF16)

        z = misc + bf_ref[...]
        dz = dlf_ref[...] * (1.0 - jax.nn.sigmoid(z))
        dbf_ref[...] += _colsum(dz)
        dp_ref[:, C_MISC:C_MISC + 128] = (dkr + dz).astype(BF16)

    row = lambda w: pl.BlockSpec((tm, w), lambda i: (i, 0))
    full = lambda a: pl.BlockSpec(a.shape, lambda i: (0, 0))
    small = (gfq, gfk, gcq, gckv, gmq, gmk, bfv, wuq, wuk, wuv)
    acc_shapes = [(1, 128), (1, 128), (1, QR), (1, KVR), (1, 128), (1, 128), (1, 128),
                  (QR, 1024), (KVR, 1024), (KVR, 512)]
    return _call(
        body, name=name, grid=(T // tm,),
        in_specs=[row(PROJW // 2), row(128), row(128)] + [full(a) for a in small]
                 + [row(512), row(512), row(512), row(1024), row(1024), row(512), row(128)],
        out_specs=[row(PROJW // 2)] + [pl.BlockSpec(s, lambda i: (0, 0)) for s in acc_shapes],
        out_shape=[jax.ShapeDtypeStruct((T, PROJW // 2), BF16)] + [jax.ShapeDtypeStruct(s, F32) for s in acc_shapes],
        scratch_shapes=[pltpu.VMEM((tm, 1024), BF16), pltpu.VMEM((tm, 1024), BF16)],
        compiler_params=_cp(),
    )(proj, rc, rs, *small, dfq, dfk, dfv, dqm, dkm, dvm, dlf)


def _scan_lanes(x, reverse):
    n = x.shape[-1]
    ln = _lane(x.shape)
    k = 1
    while k < n:
        if reverse:
            x = x + jnp.where(ln < n - k, pltpu.roll(x, n - k, x.ndim - 1), 0.0)
        else:
            x = x + jnp.where(ln >= k, pltpu.roll(x, k, x.ndim - 1), 0.0)
        k *= 2
    return x


def _forget_scan(lf, reverse, name):
    def body(x_ref, o_ref):
        x = x_ref[...]
        ln = _lane(x.shape)
        pad = jnp.logical_and(ln >= NMETA, ln < MPAD)
        o_ref[...] = jnp.where(pad, 0.0, _scan_lanes(jnp.where(pad, 0.0, x), reverse))

    return _call(body, name=name, out_shape=jax.ShapeDtypeStruct(lf.shape, F32), compiler_params=_cp())(lf)


def _attn_blocks(LP, tq):
    return [(0, MPAD, MPAD)] + [(MPAD + i * tq, tq, MPAD + (i + 1) * tq) for i in range((LP - MPAD) // tq)]


def _attn_scores(q_ref, k_ref, e, r0, rn, kend, wide, scale, bias):
    if wide:
        qe = q_ref[r0:r0 + rn, 128 * e:128 * (e + 1)]
        ke = k_ref[0:kend, 128 * e:128 * (e + 1)]
    else:
        qb = q_ref[r0:r0 + rn, :]
        mine = (_lane(qb.shape) < 64) if e == 0 else (_lane(qb.shape) >= 64)
        qe = jnp.where(mine, qb, jnp.zeros_like(qb))
        ke = k_ref[0:kend, :]
    s = _dot(qe, ke, NT) * scale
    if bias is not None:
        ct_ref, cr_ref = bias
        s = s + ct_ref[0, r0:r0 + rn, e:e + 1] - cr_ref[0, :, 0:kend]
    neg = -1e30
    if r0 == 0:
        qi = lax.broadcasted_iota(jnp.int32, (rn, kend), 0)
        ki = lax.broadcasted_iota(jnp.int32, (rn, kend), 1)
        s = jnp.where(jnp.logical_and(ki <= qi, ki < NMETA), s, neg)
    else:
        d0 = kend - rn
        head = jnp.where(_lane((rn, MPAD)) < NMETA, s[:, :MPAD], neg)
        qi = lax.broadcasted_iota(jnp.int32, (rn, rn), 0)
        diag = jnp.where(_lane((rn, rn)) <= qi, s[:, d0:], neg)
        s = jnp.concatenate([head] + ([s[:, MPAD:d0]] if d0 > MPAD else []) + [diag], axis=1)
    m = jnp.max(s, axis=-1, keepdims=True)
    p = jnp.exp(s - m)
    l = jnp.sum(p, axis=-1, keepdims=True)
    return qe, ke, p, l


def _attn_specs(B, LP, wide, has_bias):
    qw = 256 if wide else 128
    specs = [pl.BlockSpec((LP, qw), lambda b, hp: (b, hp)),
             pl.BlockSpec((LP, qw), lambda b, hp: (b, hp)),
             pl.BlockSpec((LP, 128), lambda b, hp: (b, hp))]
    bias_specs = []
    if has_bias:
        bias_specs = [pl.BlockSpec((1, LP, 2), lambda b, hp: (b * 4 + hp, 0, 0)),
                      pl.BlockSpec((1, 1, LP), lambda b, hp: (b * 8 + 2 * hp, 0, 0)),
                      pl.BlockSpec((1, 1, LP), lambda b, hp: (b * 8 + 2 * hp + 1, 0, 0))]
    return qw, specs, bias_specs


def _attn_fwd(q, k, v, bias, B, LP, wide, scale, name, tasks=()):
    T = q.shape[0]
    blocks = _attn_blocks(LP, ATTN_TQ)
    qw, specs, bias_specs = _attn_specs(B, LP, wide, bias is not None)

    def body(*refs):
        if bias is not None:
            q_ref, k_ref, v_ref, ct_ref, cr0_ref, cr1_ref, o_ref = refs
            crs = (cr0_ref, cr1_ref)
        else:
            q_ref, k_ref, v_ref, o_ref = refs
        for (r0, rn, kend) in blocks:
            outs = []
            for e in (0, 1):
                bs = (ct_ref, crs[e]) if bias is not None else None
                _, _, p, l = _attn_scores(q_ref, k_ref, e, r0, rn, kend, wide, scale, bs)
                outs.append(_dot(p.astype(BF16), v_ref[0:kend, :]) / l)
            o = jnp.where(_lane(outs[0].shape) < 64, outs[0], outs[1])
            o_ref[r0:r0 + rn, :] = o.astype(BF16)

    args = (q, k, v) + ((bias[0], bias[1], bias[1]) if bias is not None else ())
    (out,), touts = _call_tasks(
        body, tasks, name=name, grid=(B, 4),
        in_specs=specs + bias_specs,
        out_specs=[pl.BlockSpec((LP, 128), lambda b, hp: (b, hp))],
        out_shape=[jax.ShapeDtypeStruct((T, 512), BF16)],
        args=args)
    return out, touts


def _attn_bwd(q, k, v, do, bias, B, LP, wide, scale, name, tasks=()):
    T = q.shape[0]
    blocks = _attn_blocks(LP, ATTN_TQ)
    qw, specs, bias_specs = _attn_specs(B, LP, wide, bias is not None)
    has_bias = bias is not None

    def body(*refs):
        if has_bias:
            (q_ref, k_ref, v_ref, do_ref, ct_ref, cr0_ref, cr1_ref,
             dq_ref, dk_ref, dv_ref, dc0_ref, dc1_ref) = refs
            crs = (cr0_ref, cr1_ref)
            dcs = (dc0_ref, dc1_ref)
            dc0_ref[...] = jnp.zeros_like(dc0_ref)
            dc1_ref[...] = jnp.zeros_like(dc1_ref)
        else:
            q_ref, k_ref, v_ref, do_ref, dq_ref, dk_ref, dv_ref = refs
        dk_ref[...] = jnp.zeros_like(dk_ref)
        dv_ref[...] = jnp.zeros_like(dv_ref)
        for (r0, rn, kend) in blocks:
            dqs = []
            for e in (0, 1):
                bs = (ct_ref, crs[e]) if has_bias else None
                qe, ke, p, l = _attn_scores(q_ref, k_ref, e, r0, rn, kend, wide, scale, bs)
                pn = p * (1.0 / l)
                dob = do_ref[r0:r0 + rn, :]
                mine = (_lane(dob.shape) < 64) if e == 0 else (_lane(dob.shape) >= 64)
                doe = jnp.where(mine, dob, jnp.zeros_like(dob))
                dp = _dot(doe, v_ref[0:kend, :], NT)
                delta = jnp.sum(pn * dp, axis=-1, keepdims=True)
                ds = pn * (dp - delta)
                dsb = ds.astype(BF16)
                dqe = _dot(dsb, ke) * scale
                dke = _dot(dsb, qe, TN) * scale
                if wide:
                    dq_ref[r0:r0 + rn, 128 * e:128 * (e + 1)] = dqe
                    dk_ref[0:kend, 128 * e:128 * (e + 1)] += dke
                else:
                    dqs.append(dqe)
                    dk_ref[0:kend, :] += dke
                dv_ref[0:kend, :] += _dot(pn.astype(BF16), doe, TN)
                if has_bias:
                    dcs[e][0, :, 0:kend] -= _colsum(ds)
            if not wide:
                dq_ref[r0:r0 + rn, :] = jnp.where(_lane(dqs[0].shape) < 64, dqs[0], dqs[1])

    args = (q, k, v, do) + ((bias[0], bias[1], bias[1]) if has_bias else ())
    out_specs = [pl.BlockSpec((LP, qw), lambda b, hp: (b, hp)),
                 pl.BlockSpec((LP, qw), lambda b, hp: (b, hp)),
                 pl.BlockSpec((LP, 128), lambda b, hp: (b, hp))]
    out_shape = [jax.ShapeDtypeStruct(q.shape, F32), jax.ShapeDtypeStruct(q.shape, F32),
                 jax.ShapeDtypeStruct((T, 512), F32)]
    if has_bias:
        out_specs += [pl.BlockSpec((1, 1, LP), lambda b, hp: (b * 4 + hp, 0, 0))] * 2
        out_shape += [jax.ShapeDtypeStruct((B * 4, 1, LP), F32)] * 2
    return _call_tasks(
        body, tasks, name=name, grid=(B, 4),
        in_specs=specs + [pl.BlockSpec((LP, 128), lambda b, hp: (b, hp))] + bias_specs,
        out_specs=out_specs, out_shape=out_shape, args=args)


def _post_fwd(h, of, om, proj, bg, wbf, wbm, wout, name):
    T = h.shape[0]
    tm = _tile(T, (512, 384, 256, 128))

    def body(h_ref, of_ref, om_ref, gl_ref, bg_ref, wbf_ref, wbm_ref, wo_ref, o_ref, mix_ref):
        gate = jax.nn.sigmoid(gl_ref[...] + bg_ref[...])
        mix = gate[:, :D] * _dot(of_ref[...], wbf_ref[...]) + gate[:, D:] * _dot(om_ref[...], wbm_ref[...])
        mb = mix.astype(BF16)
        mix_ref[...] = mb
        o_ref[...] = h_ref[...] + _dot(mb, wo_ref[...])

    row = lambda w: pl.BlockSpec((tm, w), lambda i: (i, 0))
    full = lambda a: pl.BlockSpec(a.shape, lambda i: (0, 0))
    return _call(
        body, name=name, grid=(T // tm,),
        in_specs=[row(D), row(512), row(512), pl.BlockSpec((tm, 2 * D), lambda i: (i, 1)),
                  full(bg), full(wbf), full(wbm), full(wout)],
        out_specs=[row(D), row(D)],
        out_shape=[jax.ShapeDtypeStruct((T, D), F32), jax.ShapeDtypeStruct((T, D), BF16)],
        compiler_params=_cp(),
    )(h, of, om, proj, bg, wbf, wbm, wout)


def _post_bwd(dh, of, om, proj, bg, wbf, wbm, wout, name):
    T = dh.shape[0]
    tm = _tile(T, (512, 384, 256, 128))

    def body(d_ref, of_ref, om_ref, gl_ref, bg_ref, wbf_ref, wbm_ref, wo_ref,
             dgl_ref, dbf_ref, dbm_ref, dof_ref, dom_ref, dbg_ref):
        @pl.when(pl.program_id(0) == 0)
        def _():
            dbg_ref[...] = jnp.zeros_like(dbg_ref)

        gate = jax.nn.sigmoid(gl_ref[...] + bg_ref[...])
        dmix = _dot(d_ref[...].astype(BF16), wo_ref[...], NT)
        ofx = _dot(of_ref[...], wbf_ref[...])
        omx = _dot(om_ref[...], wbm_ref[...])
        gf = gate[:, :D]
        gm = gate[:, D:]
        dof = (dmix * gf).astype(BF16)
        dom = (dmix * gm).astype(BF16)
        dglf = dmix * ofx * gf * (1.0 - gf)
        dglm = dmix * omx * gm * (1.0 - gm)
        dgl_ref[:, :D] = dglf.astype(BF16)
        dgl_ref[:, D:] = dglm.astype(BF16)
        dbg_ref[:, :D] += _colsum(dglf)
        dbg_ref[:, D:] += _colsum(dglm)
        dbf_ref[...] = dof
        dbm_ref[...] = dom
        dof_ref[...] = _dot(dof, wbf_ref[...], NT).astype(BF16)
        dom_ref[...] = _dot(dom, wbm_ref[...], NT).astype(BF16)

    row = lambda w: pl.BlockSpec((tm, w), lambda i: (i, 0))
    full = lambda a: pl.BlockSpec(a.shape, lambda i: (0, 0))
    return _call(
        body, name=name, grid=(T // tm,),
        in_specs=[row(D), row(512), row(512), pl.BlockSpec((tm, 2 * D), lambda i: (i, 1)),
                  full(bg), full(wbf), full(wbm), full(wout)],
        out_specs=[row(2 * D), row(D), row(D), row(512), row(512), pl.BlockSpec((1, 2 * D), lambda i: (0, 0))],
        out_shape=[jax.ShapeDtypeStruct((T, 2 * D), BF16), jax.ShapeDtypeStruct((T, D), BF16),
                   jax.ShapeDtypeStruct((T, D), BF16), jax.ShapeDtypeStruct((T, 512), BF16),
                   jax.ShapeDtypeStruct((T, 512), BF16), jax.ShapeDtypeStruct((1, 2 * D), F32)],
        compiler_params=_cp(),
    )(dh, of, om, proj, bg, wbf, wbm, wout)


def _loss_head(h3, target, B, LP, name):
    S = LP - MPAD
    nb = LP // 128

    def body(h_ref, t_ref, dy_ref, l_ref):
        b = pl.program_id(0)
        p = pl.program_id(1)

        @pl.when(jnp.logical_and(b == 0, p == 0))
        def _():
            l_ref[...] = jnp.zeros_like(l_ref)

        @pl.when(p == 0)
        def _():
            dy_ref[...] = jnp.zeros_like(dy_ref)

        @pl.when(p > 0)
        def _():
            e = h_ref[...] - t_ref[0]
            dy_ref[...] = e * (1.0 / D)
            l_ref[...] += jnp.sum(e * e, axis=0, keepdims=True) * (0.5 / D)

    return _call(
        body, name=name, grid=(B, nb),
        in_specs=[pl.BlockSpec((128, D), lambda b, p: (b * nb + p, 0)),
                  pl.BlockSpec((1, 128, D), lambda b, p: (b, jnp.maximum(p - 1, 0), 0))],
        out_specs=[pl.BlockSpec((128, D), lambda b, p: (b * nb + p, 0)),
                   pl.BlockSpec((1, D), lambda b, p: (0, 0))],
        out_shape=[jax.ShapeDtypeStruct(h3.shape, F32), jax.ShapeDtypeStruct((1, D), F32)],
        compiler_params=_cp(),
    )(h3, target)


def _rope_tables(B, LP):
    pos = jnp.concatenate([jnp.arange(MPAD, dtype=F32), NMETA + jnp.arange(LP - MPAD, dtype=F32)])
    inv_freq = ROPE_THETA ** (-jnp.arange(0, ROPE, 2, dtype=F32) / ROPE)
    ang = pos[:, None] * inv_freq[None, :]
    cos, sin = jnp.cos(ang), jnp.sin(ang)
    z32 = jnp.zeros((LP, 32), F32)
    rc = jnp.concatenate([jnp.ones((LP, 64), F32), cos, cos, z32], axis=1)
    rs = jnp.concatenate([jnp.zeros((LP, 64), F32), -sin, sin, z32], axis=1)
    return jnp.tile(rc, (B, 1)), jnp.tile(rs, (B, 1))


def _pad_lanes(v, start, width=128):
    n = v.shape[1]
    return jnp.concatenate([jnp.zeros((1, start), F32), v, jnp.zeros((1, width - start - n), F32)], axis=1)


G_FFN1 = ["ffn1_w_gu", "ffn1_w_down"]
G_MIX = ["w_in", "mla_w_uq", "mla_w_ukv", "w_branch_fox", "w_branch_mla", "w_out"]
G_OUT = ["w_out", "w_branch_fox", "w_branch_mla"]
G_IN = ["w_in", "mla_w_uq", "mla_w_ukv"]


def _step(x, target, meta, vec, gath, shards):
    dist = shards is not None
    B, S, _ = x.shape
    LP = MPAD + S
    T = B * LP
    gath = dict(gath)

    def gather(names):
        return [_gather_task([shards[n] for n in names])] if dist else []

    def gathered(names, touts):
        if dist:
            gath.update(zip(names, touts[0]))

    g4, sums, red = {}, {}, {}

    def scatter(names):
        return [_a2a_task([_pieces(g4[n]) for n in names])] if dist else []

    def scattered(names, tout, me):
        for n, r in zip(names, tout):
            sums[n] = _sum_pieces(r, _pieces(g4[n]), me, "rs_sum_" + n)

    def join(names):
        return [_join_task([sums[n] for n in names])] if dist else []

    def joined(names, tout):
        for n, r in zip(names, tout):
            red[n] = (sums[n], r)

    me = None
    if dist:
        me = (4 * lax.axis_index("x") + 2 * lax.axis_index("y") + lax.axis_index("c")).reshape(1).astype(jnp.int32)

    h0 = jnp.concatenate([jnp.broadcast_to(meta[None], (B, NMETA, D)),
                          jnp.zeros((B, MPAD - NMETA, D), F32), x], axis=1).reshape(T, D)
    rc, rs = _rope_tables(B, LP)
    gfq = jnp.tile(vec["fox_q_norm"], (1, 2))
    gfk = jnp.tile(vec["fox_k_norm"], (1, 2))
    gmq = _pad_lanes(vec["mla_q_norm"], 0)
    gmk = _pad_lanes(vec["mla_k_norm"], 0)
    bfv = _pad_lanes(vec["b_forget"], L_FL)

    w1gu, w1d = _cols_from_shards(gath["ffn1_w_gu"]), gath["ffn1_w_down"].reshape(DFF, D)
    h1, touts = _ffn_fwd(h0, vec["ffn1_norm"], w1gu, w1d, "ffn1_fwd", gather(G_MIX))
    gathered(G_MIX, touts)
    wm = _mixer_weights(gath)
    small = (gfq, gfk, vec["mla_cq_norm"], vec["mla_ckv_norm"], gmq, gmk, bfv, wm["wuq"], wm["wuk"], wm["wuv"])
    proj, u2 = _inproj_fwd(h1, vec["mix_norm"], wm["w_in"], "inproj_fwd")
    fq, fk, fv, qm, km, vm, lf = _prep_fwd(proj, rc, rs, *small, name="prep_fwd")
    lf_rows = lf[:, L_FL:L_FL + NH].reshape(B, LP, NH).transpose(0, 2, 1).reshape(B * NH, LP)
    crow = _forget_scan(lf_rows, False, "forget_scan")
    ctok = crow.reshape(B, 4, 2, LP).transpose(0, 1, 3, 2).reshape(B * 4, LP, 2)
    bias = (ctok, crow.reshape(B * NH, 1, LP))
    of, touts = _attn_fwd(fq, fk, fv, bias, B, LP, False, 64 ** -0.5, "fox_fwd", gather(["ffn2_w_gu"]))
    gathered(["ffn2_w_gu"], touts)
    om, touts = _attn_fwd(qm, km, vm, None, B, LP, True, MLA_QK ** -0.5, "mla_fwd", gather(["ffn2_w_down"]))
    gathered(["ffn2_w_down"], touts)
    h2, mix = _post_fwd(h1, of, om, proj, vec["b_gate"], wm["wbf"], wm["wbm"], wm["w_out"], "post_fwd")
    w2gu, w2d = _cols_from_shards(gath["ffn2_w_gu"]), gath["ffn2_w_down"].reshape(DFF, D)
    h3, _ = _ffn_fwd(h2, vec["ffn2_norm"], w2gu, w2d, "ffn2_fwd")
    dy, lpart = _loss_head(h3, target, B, LP, "loss_head")

    gv = {}
    (dh2, u3, a2, dgp2, gv["ffn2_norm"]), _ = _ffn_bwd(h2, dy, vec["ffn2_norm"], w2gu, w2d, "ffn2_bwd")
    g4["ffn2_w_gu"] = _wgrad(u3, dgp2, "ffn2_dwgu", bn=DFF, shards=2)[0]
    g4["ffn2_w_down"] = _wgrad(a2, dy, "ffn2_dwd", scale=0.5, bk=FH)[0].reshape(N_CHIPS, DFF // N_CHIPS, D)

    dgl, dbf, dbm, dof, dom, gv["b_gate"] = _post_bwd(dh2, of, om, proj, vec["b_gate"], wm["wbf"], wm["wbm"], wm["w_out"], "post_bwd")
    g4["w_out"] = _wgrad(mix, dh2, "dw_out", bt=1088)[0].reshape(N_CHIPS, D // N_CHIPS, D)
    g4["w_branch_fox"] = _cols_to_shards(_wgrad(of, dbf, "dw_bf", bt=1088)[0])
    g4["w_branch_mla"] = _cols_to_shards(_wgrad(om, dbm, "dw_bm", bt=1088)[0])
    G_FFN2 = ["ffn2_w_gu", "ffn2_w_down"]
    (dfq, dfk, dfv, dc0, dc1), touts = _attn_bwd(fq, fk, fv, dof, bias, B, LP, False, 64 ** -0.5, "fox_bwd", scatter(G_FFN2))
    if dist:
        scattered(G_FFN2, touts[0], me)
    (dqm, dkm, dvm), touts = _attn_bwd(qm, km, vm, dom, None, B, LP, True, MLA_QK ** -0.5, "mla_bwd",
                                       scatter(G_OUT) + join(G_FFN2))
    if dist:
        scattered(G_OUT, touts[0], me)
        joined(G_FFN2, touts[1])
    dc = jnp.concatenate([dc0, dc1], axis=1).reshape(B * NH, LP)
    dlf_rows = _forget_scan(dc, True, "forget_scan_bwd")
    dlf = dlf_rows.reshape(B, NH, LP).transpose(0, 2, 1).reshape(T, NH)
    dlf = jnp.concatenate([jnp.zeros((T, L_FL), F32), dlf, jnp.zeros((T, 128 - L_FL - NH), F32)], axis=1)
    (dlo, dgfq, dgfk, gv["mla_cq_norm"], gv["mla_ckv_norm"], dgmq, dgmk, dbfv,
     dwuq, dwuk, dwuv) = _prep_bwd(proj, rc, rs, *small, dfq, dfk, dfv, dqm, dkm, dvm, dlf, name="prep_bwd")
    gv["fox_q_norm"] = dgfq[:, :64] + dgfq[:, 64:]
    gv["fox_k_norm"] = dgfk[:, :64] + dgfk[:, 64:]
    gv["mla_q_norm"] = dgmq[:, :MLA_QK]
    gv["mla_k_norm"] = dgmk[:, :MLA_QK]
    gv["b_forget"] = dbfv[:, L_FL:L_FL + NH]
    dwin = jnp.concatenate([_wgrad(u2, dlo, "dw_in_lo")[0], _wgrad(u2, dgl, "dw_in_hi")[0]], axis=1)
    g4["w_in"] = _cols_to_shards(_win_from_kernel(dwin))
    g4["mla_w_uq"] = _cols_to_shards(
        dwuq.astype(GRAD_DTYPE).reshape(QR, NH, 128)[:, :, :MLA_QK].reshape(QR, NH * MLA_QK))
    dukv = jnp.concatenate([dwuk.reshape(KVR, NH, 128)[:, :, :64], dwuv.reshape(KVR, NH, 64)], axis=2)
    g4["mla_w_ukv"] = _cols_to_shards(dukv.astype(GRAD_DTYPE).reshape(KVR, NH * 128))
    dh1, gv["mix_norm"] = _inproj_bwd(h1, dh2, dlo, dgl, vec["mix_norm"], wm["w_in"], "inproj_bwd")

    (dh0, u1, a1, dgp1, gv["ffn1_norm"]), touts = _ffn_bwd(h0, dh1, vec["ffn1_norm"], w1gu, w1d, "ffn1_bwd",
                                                            scatter(G_IN) + join(G_OUT))
    if dist:
        scattered(G_IN, touts[0], me)
        joined(G_OUT, touts[1])
    dh0 = dh0.reshape(B, LP, D)
    grad_x = dh0[:, MPAD:]
    grad_meta = jnp.sum(dh0[:, :NMETA], axis=0)
    share = [_share_task([_stack_vectors([gv[n] for n in VECS]), grad_meta, lpart])] if dist else []
    g4["ffn1_w_gu"], touts = _wgrad(u1, dgp1, "ffn1_dwgu", bn=DFF, shards=2, tasks=share)
    shared = touts[0] if dist else None
    dwd1, touts = _wgrad(a1, dh1, "ffn1_dwd", scale=0.5, bk=FH, tasks=scatter(["ffn1_w_gu"]) + join(G_IN))
    g4["ffn1_w_down"] = dwd1.reshape(N_CHIPS, DFF // N_CHIPS, D)
    if dist:
        scattered(["ffn1_w_gu"], touts[0], me)
        joined(G_IN, touts[1])
        scattered(["ffn1_w_down"], _run_tasks(scatter(["ffn1_w_down"]), "rs_ffn1_w_down")[0], me)
        joined(G_FFN1, _run_tasks(join(G_FFN1), "rs_join_ffn1")[0])
    return lpart, grad_x, grad_meta, gv, (red if dist else g4), shared


def _cols_from_shards(g4):
    n, r, c = g4.shape
    return g4.transpose(1, 0, 2).reshape(r, n * c)


def _cols_to_shards(full):
    r, c4 = full.shape
    return full.reshape(r, N_CHIPS, c4 // N_CHIPS).transpose(1, 0, 2)


def _win_to_kernel(wfull):
    z = lambda n: jnp.zeros((D, n), wfull.dtype)
    fl, cq, ckv, kr, gate = (wfull[:, 1536:1544], wfull[:, 1544:1800], wfull[:, 1800:1928],
                             wfull[:, 1928:1960], wfull[:, 1960:4008])
    misc = jnp.concatenate([z(L_KR), kr, fl, z(128 - L_FL - NH)], axis=1)
    return jnp.concatenate([wfull[:, :1536], cq, ckv, misc, gate], axis=1)


def _win_from_kernel(gk):
    m = C_MISC
    return jnp.concatenate([gk[:, :1536], gk[:, m + L_FL:m + L_FL + NH], gk[:, C_CQ:C_CQ + QR],
                            gk[:, C_CKV:C_CKV + KVR], gk[:, m + L_KR:m + L_KR + ROPE], gk[:, C_GATE:]], axis=1)


def _pieces(g4):
    n, r, c = g4.shape
    return g4.reshape(2 * n, r // 2, c)


def _mixer_weights(gath):
    w = {}
    w["w_in"] = _win_to_kernel(_cols_from_shards(gath["w_in"]))
    uq = _cols_from_shards(gath["mla_w_uq"]).reshape(QR, NH, MLA_QK)
    w["wuq"] = jnp.pad(uq, ((0, 0), (0, 0), (0, 128 - MLA_QK))).reshape(QR, NH * 128)
    ukv = _cols_from_shards(gath["mla_w_ukv"]).reshape(KVR, NH, 128)
    w["wuk"] = jnp.pad(ukv[:, :, :64], ((0, 0), (0, 0), (0, 64))).reshape(KVR, NH * 128)
    w["wuv"] = ukv[:, :, 64:].reshape(KVR, NH * 64)
    w["wbf"] = _cols_from_shards(gath["w_branch_fox"])
    w["wbm"] = _cols_from_shards(gath["w_branch_mla"])
    w["w_out"] = gath["w_out"].reshape(D, D)
    return w


def _chip_peers(x, y):
    return [(1 - x, y), (x, 1 - y), (1 - x, 1 - y)]


RELS = [(dx, dy, dc) for dx in (0, 1) for dy in (0, 1) for dc in (0, 1)][1:]


def _here():
    return lax.axis_index("x"), lax.axis_index("y"), lax.axis_index("c")


def _flip(a, d):
    return (1 - a) if d else a


def _remote(src, dst, send, recv, i, dev):
    return functools.partial(pltpu.make_async_remote_copy, src_ref=src, dst_ref=dst, send_sem=send.at[i],
                             recv_sem=recv.at[i], device_id=dev, device_id_type=MESH)


def _gather_task(shards):
    n = len(shards)

    def descs(ins, outs, sems):
        send, recv, loc = sems
        x, y, c = _here()
        j = 2 * x + y
        locs, pairs = [], []
        for k in range(n):
            locs.append(functools.partial(pltpu.make_async_copy, ins[k], outs[k].at[j], loc.at[k]))
            for r, (px, py) in enumerate(_chip_peers(x, y)):
                dev = (px, py, c)
                pairs.append((_remote(ins[k], outs[k].at[j], send, recv, 3 * k + r, dev),
                              _remote(ins[k], outs[k].at[2 * px + py], send, recv, 3 * k + r, dev)))
        return locs, pairs

    return _Task(shards, [jax.ShapeDtypeStruct((N_CHIPS,) + s.shape, s.dtype) for s in shards],
                 [pltpu.SemaphoreType.DMA((3 * n,)), pltpu.SemaphoreType.DMA((3 * n,)), pltpu.SemaphoreType.DMA((n,))],
                 descs)


class _SplitGather(_Task):
    PARTS = 2

    def __init__(self, shards):
        n = 3 * len(shards) * self.PARTS
        dma = pltpu.SemaphoreType.DMA
        super().__init__(shards, [jax.ShapeDtypeStruct((N_CHIPS,) + s.shape, s.dtype) for s in shards],
                         [dma((n,)), dma((n,)), dma((n,)), dma((n,)), dma((len(shards),))], None)

    def _plan(self, ins, outs, sems):
        send, recv, fsend, frecv, loc = sems
        x, y, c = _here()
        j = 2 * x + y
        locs, first, passed = [], [], []
        for k in range(len(ins)):
            h = self.ins[k].shape[0] // 2
            parts = self.PARTS if h % (32 * self.PARTS) == 0 else 1
            hp = h // parts
            locs.append(functools.partial(pltpu.make_async_copy, ins[k], outs[k].at[j], loc.at[k]))
            for r, (px, py) in enumerate(_chip_peers(x, y)):
                p = 2 * px + py
                for q in range(parts):
                    i = (3 * k + r) * self.PARTS + q
                    mine = pl.ds(pl.multiple_of(c * h + q * hp, 8), hp)
                    theirs = pl.ds(pl.multiple_of((1 - c) * h + q * hp, 8), hp)
                    first.append((_remote(ins[k].at[mine], outs[k].at[j, mine], send, recv, i, (px, py, c)),
                                  _remote(ins[k].at[mine], outs[k].at[p, mine], send, recv, i, (px, py, c))))
                    passed.append((_remote(outs[k].at[p, mine], outs[k].at[p, mine], fsend, frecv, i, (x, y, 1 - c)),
                                   _remote(outs[k].at[p, mine], outs[k].at[p, theirs], fsend, frecv, i, (x, y, 1 - c))))
        return locs, first, passed

    def start(self, ins, outs, sems):
        locs, first, _ = self._plan(ins, outs, sems)
        for lc in locs:
            lc().start()
        for snd, _ in first:
            snd().start()

    def wait(self, ins, outs, sems):
        locs, first, passed = self._plan(ins, outs, sems)
        for (_, landed), (pass_on, _) in zip(first, passed):
            landed().wait_recv()
            pass_on().start()
        for _, rcv in passed:
            rcv().wait_recv()
        for snd, _ in first + passed:
            snd().wait_send()
        for lc in locs:
            lc().wait()


def _a2a_task(ps):
    n = len(ps)
    nr = len(RELS)

    def descs(ins, outs, sems):
        send, recv = sems
        x, y, c = _here()
        me = 4 * x + 2 * y + c
        pairs = []
        for k in range(n):
            for i, (dx, dy, dc) in enumerate(RELS):
                dev = (_flip(x, dx), _flip(y, dy), _flip(c, dc))
                peer = 4 * dev[0] + 2 * dev[1] + dev[2]
                pairs.append((_remote(ins[k].at[peer], outs[k].at[me], send, recv, nr * k + i, dev),
                              _remote(ins[k].at[peer], outs[k].at[peer], send, recv, nr * k + i, dev)))
        return [], pairs

    return _Task(ps, [jax.ShapeDtypeStruct(p.shape, p.dtype) for p in ps],
                 [pltpu.SemaphoreType.DMA((nr * n,)), pltpu.SemaphoreType.DMA((nr * n,))], descs)


def _share_task(vs):
    n = len(vs)
    nr = len(RELS)

    def descs(ins, outs, sems):
        send, recv, loc = sems
        x, y, c = _here()
        me = 4 * x + 2 * y + c
        locs, pairs = [], []
        for k in range(n):
            locs.append(functools.partial(pltpu.make_async_copy, ins[k], outs[k].at[me], loc.at[k]))
            for i, (dx, dy, dc) in enumerate(RELS):
                dev = (_flip(x, dx), _flip(y, dy), _flip(c, dc))
                peer = 4 * dev[0] + 2 * dev[1] + dev[2]
                pairs.append((_remote(ins[k], outs[k].at[me], send, recv, nr * k + i, dev),
                              _remote(ins[k], outs[k].at[peer], send, recv, nr * k + i, dev)))
        return locs, pairs

    return _Task(vs, [jax.ShapeDtypeStruct((N_DEV,) + v.shape, v.dtype) for v in vs],
                 [pltpu.SemaphoreType.DMA((nr * n,)), pltpu.SemaphoreType.DMA((nr * n,)), pltpu.SemaphoreType.DMA((n,))],
                 descs)


def _join_task(ss):
    n = len(ss)

    def descs(ins, outs, sems):
        send, recv = sems
        x, y, c = _here()
        pairs = []
        for k in range(n):
            cp = _remote(ins[k], outs[k], send, recv, k, (x, y, 1 - c))
            pairs.append((cp, cp))
        return [], pairs

    return _Task(ss, [jax.ShapeDtypeStruct(s.shape, s.dtype) for s in ss],
                 [pltpu.SemaphoreType.DMA((n,)), pltpu.SemaphoreType.DMA((n,))], descs)


def _sum_pieces(recv, own, me, name):
    n, h, c = recv.shape
    tr = h

    def body(me_ref, r_ref, o_ref, out_ref):
        s = pl.program_id(1)
        val = jnp.where(s == me_ref[0], o_ref[0], r_ref[0]).astype(F32)

        @pl.when(s == 0)
        def _():
            out_ref[...] = val

        @pl.when(s > 0)
        def _():
            out_ref[...] += val

    def other(s, m):
        return jnp.where(s == m[0], (s + 1) % n, s)

    return _call(
        body, name=name,
        grid_spec=pltpu.PrefetchScalarGridSpec(
            num_scalar_prefetch=1, grid=(h // tr, n),
            in_specs=[pl.BlockSpec((1, tr, c), lambda i, s, m: (other(s, m), i, 0)),
                      pl.BlockSpec((1, tr, c), lambda i, s, m: (m[0], i, 0))],
            out_specs=pl.BlockSpec((tr, c), lambda i, s, m: (i, 0))),
        out_shape=jax.ShapeDtypeStruct((h, c), F32),
        compiler_params=_cp(),
    )(me, recv, own)


def _adamw_update(gg, w, m, v):
    c1 = 1.0 / (1.0 - ADAM_B1 ** ADAM_STEP)
    c2 = 1.0 / (1.0 - ADAM_B2 ** ADAM_STEP)
    nm = ADAM_B1 * m + (1.0 - ADAM_B1) * gg
    nv = ADAM_B2 * v + (1.0 - ADAM_B2) * (gg * gg)
    return -ADAM_LR * ((nm * c1) / (jnp.sqrt(nv * c2) + ADAM_EPS) + ADAM_WD * w), nm, nv


def _adamw_small(gvec8, gmeta8, lp8, chip, ws, ms, vs, name):
    na = len(ws)

    def dev_sum(ref):
        acc = ref[0]
        for s in range(1, N_DEV):
            acc = acc + ref[s]
        return acc

    def body(c_ref, gv_ref, gm_ref, lp_ref, *refs):
        w_refs, m_refs, v_refs = refs[:na], refs[na:2 * na], refs[2 * na:3 * na]
        l_ref = refs[3 * na]
        outs = refs[3 * na + 1:]
        g_refs, d_refs, nm_refs, nv_refs = outs[:na], outs[na:2 * na], outs[2 * na:3 * na], outs[3 * na:]
        l_ref[...] = dev_sum(lp_ref)
        gvec = dev_sum(gv_ref)
        for k in range(na):
            gg = gvec[k:k + 1, 0:ws[k].shape[1]] if k < na - 1 else dev_sum(gm_ref)
            g_refs[k][...] = gg
            d_refs[k][...], nm_refs[k][...], nv_refs[k][...] = _adamw_update(gg, w_refs[k][...], m_refs[k][...], v_refs[k][...])

    whole = lambda a: pl.BlockSpec(a.shape, lambda i, c: (0,) * a.ndim)
    arrs = list(ws) + list(ms) + list(vs)
    res = _call(
        body, name=name,
        grid_spec=pltpu.PrefetchScalarGridSpec(
            num_scalar_prefetch=1, grid=(1,),
            in_specs=[whole(gvec8), pl.BlockSpec((N_DEV, NMETA, D // N_CHIPS), lambda i, c: (0, 0, c[0])), whole(lp8)]
                     + [whole(a) for a in arrs],
            out_specs=[pl.BlockSpec((1, D), lambda i, c: (0, 0))] + [whole(a) for a in ws] * 4),
        out_shape=[jax.ShapeDtypeStruct((1, D), F32)] + [jax.ShapeDtypeStruct(a.shape, F32) for a in ws] * 4,
        compiler_params=_cp(),
    )(chip, gvec8, gmeta8, lp8, *arrs)
    return res[0], [list(res[1 + i * na:1 + (i + 1) * na]) for i in range(4)]


def _adamw_halves(wt, mine, theirs, m, v, core, name):
    r, c = wt.shape
    h = r // 2
    tr = _tile(h, (256, 176, 128, 64))
    nh = h // tr

    def body(c_ref, w_ref, a_ref, b_ref, m_ref, v_ref, g_ref, d_ref, nm_ref, nv_ref):
        gg = jnp.where(pl.program_id(0) // nh == c_ref[0], a_ref[...], b_ref[...])
        g_ref[...] = gg
        d_ref[...], nm_ref[...], nv_ref[...] = _adamw_update(gg, w_ref[...], m_ref[...], v_ref[...])

    full = pl.BlockSpec((tr, c), lambda i, cr: (i, 0))
    half = pl.BlockSpec((tr, c), lambda i, cr: (i % nh, 0))
    return _call(
        body, name=name,
        grid_spec=pltpu.PrefetchScalarGridSpec(
            num_scalar_prefetch=1, grid=(2 * nh,),
            in_specs=[full, half, half, full, full], out_specs=[full] * 4),
        out_shape=[jax.ShapeDtypeStruct((r, c), F32)] * 4,
        compiler_params=_cp(),
    )(core, wt, mine, theirs, m, v)


MATS = ["ffn1_w_gu", "ffn1_w_down", "w_in", "mla_w_uq", "mla_w_ukv", "w_branch_fox", "w_branch_mla",
        "w_out", "ffn2_w_gu", "ffn2_w_down"]
VECS = ["ffn1_norm", "mix_norm", "b_forget", "b_gate", "fox_q_norm", "fox_k_norm", "mla_cq_norm",
        "mla_ckv_norm", "mla_q_norm", "mla_k_norm", "ffn2_norm"]
WEIGHTS = ["meta_tokens", "ffn1_norm", "ffn1_w_gu", "ffn1_w_down", "mix_norm", "w_in", "b_forget", "b_gate",
           "fox_q_norm", "fox_k_norm", "mla_cq_norm", "mla_w_uq", "mla_ckv_norm", "mla_w_ukv", "mla_q_norm",
           "mla_k_norm", "w_branch_fox", "w_branch_mla", "w_out", "ffn2_norm", "ffn2_w_gu", "ffn2_w_down"]


VEC_LANES = 2048


def _stack_vectors(parts):
    rows = [_pad_lanes(p, 0, VEC_LANES) for p in parts]
    rows.append(jnp.zeros((-len(parts) % 8, VEC_LANES), F32))
    return jnp.concatenate(rows, axis=0)


def kernel(x, meta_tokens, ffn1_norm, ffn1_w_gu, ffn1_w_down, mix_norm, w_in, b_forget, b_gate, fox_q_norm, fox_k_norm, mla_cq_norm, mla_w_uq, mla_ckv_norm, mla_w_ukv, mla_q_norm, mla_k_norm, w_branch_fox, w_branch_mla, w_out, ffn2_norm, ffn2_w_gu, ffn2_w_down, loss_target, m_meta_tokens, m_ffn1_norm, m_ffn1_w_gu, m_ffn1_w_down, m_mix_norm, m_w_in, m_b_forget, m_b_gate, m_fox_q_norm, m_fox_k_norm, m_mla_cq_norm, m_mla_w_uq, m_mla_ckv_norm, m_mla_w_ukv, m_mla_q_norm, m_mla_k_norm, m_w_branch_fox, m_w_branch_mla, m_w_out, m_ffn2_norm, m_ffn2_w_gu, m_ffn2_w_down, v_meta_tokens, v_ffn1_norm, v_ffn1_w_gu, v_ffn1_w_down, v_mix_norm, v_w_in, v_b_forget, v_b_gate, v_fox_q_norm, v_fox_k_norm, v_mla_cq_norm, v_mla_w_uq, v_mla_ckv_norm, v_mla_w_ukv, v_mla_q_norm, v_mla_k_norm, v_w_branch_fox, v_w_branch_mla, v_w_out, v_ffn2_norm, v_ffn2_w_gu, v_ffn2_w_down):
    a = dict(locals())
    wts = {n: a[n] for n in WEIGHTS}
    ms = {n: a["m_" + n] for n in WEIGHTS}
    vs = {n: a["v_" + n] for n in WEIGHTS}
    cx, cy, cc = lax.axis_index("x"), lax.axis_index("y"), lax.axis_index("c")
    chip = 2 * cx + cy

    shards = {n: wts[n][0].astype(BF16) for n in MATS}
    first = _run_tasks([_SplitGather([shards[n] for n in G_FFN1] + [meta_tokens])], "gather_ffn1")[0]
    gath = dict(zip(G_FFN1, first[:-1]))
    meta_full = _cols_from_shards(first[-1])

    _, grad_x, _, _, gred, (gvec8, gmeta8, lp8) = _step(x, loss_target, meta_full, {n: wts[n] for n in VECS}, gath, shards)

    sm_names = VECS + ["meta_tokens"]
    lsum, sm = _adamw_small(gvec8, gmeta8, lp8, chip.reshape(1).astype(jnp.int32), [wts[n] for n in sm_names],
                            [ms[n] for n in sm_names], [vs[n] for n in sm_names], "adamw_small")
    loss = jnp.sum(lsum)

    grads, delta, new_m, new_v = {}, {}, {}, {}
    core = cc.reshape(1).astype(jnp.int32)
    for n in MATS:
        shp = wts[n].shape
        mine, theirs = gred[n]
        res = _adamw_halves(wts[n][0], mine, theirs, ms[n][0], vs[n][0], core, "adamw_" + n)
        grads[n], delta[n], new_m[n], new_v[n] = (t.reshape(shp) for t in res)
    for k, n in enumerate(sm_names):
        grads[n], delta[n], new_m[n], new_v[n] = (sm[i][k] for i in range(4))

    return (loss, grad_x, *[grads[n] for n in WEIGHTS], *[delta[n] for n in WEIGHTS],
            *[new_m[n] for n in WEIGHTS], *[new_v[n] for n in WEIGHTS])
```

```python
import functools

import jax
import jax.numpy as jnp
from jax import lax
from jax.experimental import pallas as pl
from jax.experimental.pallas import tpu as pltpu

F32 = jnp.float32
BF16 = jnp.bfloat16
MESH = pl.DeviceIdType.MESH

D = 1024
DFF = 2816
FH = DFF // 2
NMETA = 16
MPAD = 128
EPS = 1e-6
NH = 8
FOXW = 512
QR = 256
KVR = 128
ROPE = 32
MLA_QK = 96
PROJW = 4096
ROPE_THETA = 10000.0
N_CHIPS = 4
N_DEV = 8

ADAM_LR = 0.001
ADAM_B1 = 0.9
ADAM_B2 = 0.999
ADAM_EPS = 1e-08
ADAM_WD = 0.01
ADAM_STEP = 10

VMEM_LIMIT = 56 * 2**20
ATTN_TQ = 256
GRAD_DTYPE = BF16

NT = (((1,), (1,)), ((), ()))
TN = (((0,), (0,)), ((), ()))


def _call(body, **kw):
    return pl.pallas_call(body, **kw)


def _cp(**kw):
    return pltpu.CompilerParams(vmem_limit_bytes=VMEM_LIMIT, **kw)


HBM = pl.BlockSpec(memory_space=pltpu.HBM)


class _Task:
    def __init__(self, ins, out_shapes, sems, descs):
        self.ins, self.out_shapes, self.sems, self.descs = list(ins), list(out_shapes), list(sems), descs

    def start(self, ins, outs, sems):
        locs, pairs = self.descs(ins, outs, sems)
        for lc in locs:
            lc().start()
        for snd, _ in pairs:
            snd().start()

    def wait(self, ins, outs, sems):
        locs, pairs = self.descs(ins, outs, sems)
        for _, rcv in pairs:
            rcv().wait_recv()
        for snd, _ in pairs:
            snd().wait_send()
        for lc in locs:
            lc().wait()


def _call_tasks(body, tasks, *, name, grid, in_specs, out_specs, out_shape, args, scratch_shapes=()):
    in_specs, out_specs, out_shape, scratch_shapes = map(list, (in_specs, out_specs, out_shape, scratch_shapes))
    n_in, n_out, n_sc = len(in_specs), len(out_specs), len(scratch_shapes)
    t_in = [len(t.ins) for t in tasks]
    t_out = [len(t.out_shapes) for t in tasks]
    t_sem = [len(t.sems) for t in tasks]

    def wrapped(*refs):
        pos = [0]

        def take(n):
            pos[0] += n
            return refs[pos[0] - n:pos[0]]

        ins, tins = take(n_in), [take(n) for n in t_in]
        outs, touts = take(n_out), [take(n) for n in t_out]
        sc, tsems = take(n_sc), [take(n) for n in t_sem]
        if tasks:
            first = functools.reduce(jnp.logical_and, [pl.program_id(a) == 0 for a in range(len(grid))])
            last = functools.reduce(jnp.logical_and, [pl.program_id(a) == grid[a] - 1 for a in range(len(grid))])

            @pl.when(first)
            def _():
                for t, a, b, s in zip(tasks, tins, touts, tsems):
                    t.start(a, b, s)

        body(*ins, *outs, *sc)
        if tasks:
            @pl.when(last)
            def _():
                for t, a, b, s in zip(tasks, tins, touts, tsems):
                    t.wait(a, b, s)

    res = _call(
        wrapped, name=name, grid=grid,
        in_specs=in_specs + [HBM] * sum(t_in), out_specs=out_specs + [HBM] * sum(t_out),
        out_shape=out_shape + [s for t in tasks for s in t.out_shapes],
        scratch_shapes=scratch_shapes + [s for t in tasks for s in t.sems],
        compiler_params=_cp(),
    )(*args, *[a for t in tasks for a in t.ins])
    res = list(res)
    touts, pos = [], n_out
    for n in t_out:
        touts.append(res[pos:pos + n])
        pos += n
    return res[:n_out], touts


def _run_tasks(tasks, name):
    t_in = [len(t.ins) for t in tasks]
    t_out = [len(t.out_shapes) for t in tasks]
    t_sem = [len(t.sems) for t in tasks]

    def body(*refs):
        pos = [0]

        def take(n):
            pos[0] += n
            return refs[pos[0] - n:pos[0]]

        tins, touts, tsems = [take(n) for n in t_in], [take(n) for n in t_out], [take(n) for n in t_sem]
        for t, a, b, s in zip(tasks, tins, touts, tsems):
            t.start(a, b, s)
        for t, a, b, s in zip(tasks, tins, touts, tsems):
            t.wait(a, b, s)

    res = list(_call(
        body, name=name, in_specs=[HBM] * sum(t_in), out_specs=[HBM] * sum(t_out),
        out_shape=[s for t in tasks for s in t.out_shapes],
        scratch_shapes=[s for t in tasks for s in t.sems],
    )(*[a for t in tasks for a in t.ins]))
    touts, pos = [], 0
    for n in t_out:
        touts.append(res[pos:pos + n])
        pos += n
    return touts


def _tile(n, cands):
    for c in cands:
        if n % c == 0:
            return c
    raise ValueError(f"no tile for {n} among {cands}")


def _dot(a, b, dims=None):
    if dims is None:
        return jnp.dot(a, b, preferred_element_type=F32)
    return lax.dot_general(a, b, dims, preferred_element_type=F32)


def _lane(shape):
    return lax.broadcasted_iota(jnp.int32, shape, len(shape) - 1)


def _seg_ones(w, log2_seg):
    r = lax.shift_right_logical(lax.broadcasted_iota(jnp.int32, (w, w), 0), log2_seg)
    c = lax.shift_right_logical(lax.broadcasted_iota(jnp.int32, (w, w), 1), log2_seg)
    return jnp.where(r == c, 1.0, 0.0).astype(BF16)


def _seg_sum(x, ones):
    hi = x.astype(BF16)
    r1 = x - hi.astype(F32)
    mid = r1.astype(BF16)
    lo = (r1 - mid.astype(F32)).astype(BF16)
    return _dot(hi, ones) + _dot(mid, ones) + _dot(lo, ones)


def _lane_sum(x, ones):
    return jnp.sum(x, axis=-1, keepdims=True) if ones is None else _seg_sum(x, ones)


def _rms(x, gain, n, ones=None):
    r = lax.rsqrt(_lane_sum(x * x, ones) * (1.0 / n) + EPS)
    xh = x * r
    return xh * gain, xh, r


def _rms_bwd(dy, xh, r, gain, n, ones=None):
    dxh = dy * gain
    return r * (dxh - xh * (_lane_sum(dxh * xh, ones) * (1.0 / n)))


def _rope_swap(x):
    ln = _lane(x.shape)
    sw = jnp.where(ln < 80, pltpu.roll(x, 112, 1), pltpu.roll(x, 16, 1))
    return jnp.where(jnp.logical_and(ln >= 64, ln < 96), sw, 0.0)


def _colsum(x):
    return jnp.sum(x, axis=0, keepdims=True)


def _ffn_weight_specs():
    once = pl.Buffered(1)
    return [pl.BlockSpec((D, DFF), lambda i: (0, 0), pipeline_mode=once),
            pl.BlockSpec((D, DFF), lambda i: (0, 1), pipeline_mode=once),
            pl.BlockSpec((DFF, D), lambda i: (0, 0), pipeline_mode=once)]


def _ffn_fwd(h, norm, wgu, wd, name, tasks=()):
    T = h.shape[0]
    tm = _tile(T, (512, 384, 256, 128))

    def body(h_ref, n_ref, wg_ref, wu_ref, wd_ref, o_ref):
        x = h_ref[...]
        u, _, _ = _rms(x, n_ref[...], D)
        ub = u.astype(BF16)
        g = _dot(ub, wg_ref[...])
        p = _dot(ub, wu_ref[...])
        a = (g * jax.nn.sigmoid(g)) * p
        o_ref[...] = x + 0.5 * _dot(a.astype(BF16), wd_ref[...])

    (out,), touts = _call_tasks(
        body, tasks, name=name, grid=(T // tm,),
        in_specs=[pl.BlockSpec((tm, D), lambda i: (i, 0)), pl.BlockSpec((1, D), lambda i: (0, 0))] + _ffn_weight_specs(),
        out_specs=[pl.BlockSpec((tm, D), lambda i: (i, 0))],
        out_shape=[jax.ShapeDtypeStruct((T, D), F32)],
        args=(h, norm, wgu, wgu, wd))
    return out, touts


def _ffn_bwd(h, dout, norm, wgu, wd, name, tasks=()):
    T = h.shape[0]
    tm = _tile(T, (256, 128))

    def body(h_ref, d_ref, n_ref, wg_ref, wu_ref, wd_ref, dh_ref, u_ref, a_ref, dgp_ref, dn_ref):
        @pl.when(pl.program_id(0) == 0)
        def _():
            dn_ref[...] = jnp.zeros_like(dn_ref)

        u, xh, r = _rms(h_ref[...], n_ref[...], D)
        ub = u.astype(BF16)
        u_ref[...] = ub
        g = _dot(ub, wg_ref[...])
        p = _dot(ub, wu_ref[...])
        s = jax.nn.sigmoid(g)
        sl = g * s
        dz = (0.5 * d_ref[...]).astype(BF16)
        da = _dot(dz, wd_ref[...], NT)
        dp = da * sl
        dg = (da * p) * (s * (1.0 + g * (1.0 - s)))
        a_ref[...] = (sl * p).astype(BF16)
        dgb = dg.astype(BF16)
        dpb = dp.astype(BF16)
        dgp_ref[:, :DFF] = dgb
        dgp_ref[:, DFF:] = dpb
        du = _dot(dgb, wg_ref[...], NT) + _dot(dpb, wu_ref[...], NT)
        dn_ref[...] += _colsum(du * xh)
        dh_ref[...] = d_ref[...] + _rms_bwd(du, xh, r, n_ref[...], D)

    row = lambda w: pl.BlockSpec((tm, w), lambda i: (i, 0))
    return _call_tasks(
        body, tasks, name=name, grid=(T // tm,),
        in_specs=[row(D), row(D), pl.BlockSpec((1, D), lambda i: (0, 0))] + _ffn_weight_specs(),
        out_specs=[row(D), row(D), row(DFF), row(2 * DFF), pl.BlockSpec((1, D), lambda i: (0, 0))],
        out_shape=[jax.ShapeDtypeStruct((T, D), F32),
                   jax.ShapeDtypeStruct((T, D), BF16),
                   jax.ShapeDtypeStruct((T, DFF), BF16),
                   jax.ShapeDtypeStruct((T, 2 * DFF), BF16),
                   jax.ShapeDtypeStruct((1, D), F32)],
        args=(h, dout, norm, wgu, wgu, wd))


def _wgrad(x, y, name, scale=1.0, bk=None, bn=None, shards=0, bt=512, tasks=()):
    T, K = x.shape
    N = y.shape[1]
    bk = bk or K
    bn = bn or N
    bt = _tile(T, (bt, 512, 384, 256, 128))
    nt = T // bt

    def body(x_ref, y_ref, o_ref, acc_ref):
        t = pl.program_id(2)

        @pl.when(t == 0)
        def _():
            acc_ref[...] = jnp.zeros_like(acc_ref)

        acc_ref[...] += _dot(x_ref[...].astype(BF16), y_ref[...].astype(BF16), TN)

        @pl.when(t == nt - 1)
        def _():
            res = (acc_ref[...] * scale).astype(o_ref.dtype)
            if shards:
                w = bn // shards
                for s in range(shards):
                    o_ref[s] = res[:, s * w:(s + 1) * w]
            else:
                o_ref[...] = res

    if shards:
        assert bk == K
        out_spec = pl.BlockSpec((shards, K, bn // shards), lambda i, j, t: (j, 0, 0))
        out_shape = jax.ShapeDtypeStruct((N * shards // bn, K, bn // shards), GRAD_DTYPE)
    else:
        out_spec = pl.BlockSpec((bk, bn), lambda i, j, t: (i, j))
        out_shape = jax.ShapeDtypeStruct((K, N), GRAD_DTYPE)
    (out,), touts = _call_tasks(
        body, tasks, name=name, grid=(K // bk, N // bn, nt),
        in_specs=[pl.BlockSpec((bt, bk), lambda i, j, t: (t, i)),
                  pl.BlockSpec((bt, bn), lambda i, j, t: (t, j))],
        out_specs=[out_spec], out_shape=[out_shape],
        scratch_shapes=[pltpu.VMEM((bk, bn), F32)],
        args=(x, y))
    return out, touts


def _inproj_fwd(h, norm, w, name):
    T = h.shape[0]
    tm = _tile(T, (512, 384, 256, 128))

    def body(h_ref, n_ref, w_ref, o_ref, u_ref):
        u, _, _ = _rms(h_ref[...], n_ref[...], D)
        ub = u.astype(BF16)
        u_ref[...] = ub
        o_ref[...] = _dot(ub, w_ref[...])

    return _call(
        body, name=name, grid=(T // tm,),
        in_specs=[pl.BlockSpec((tm, D), lambda i: (i, 0)),
                  pl.BlockSpec((1, D), lambda i: (0, 0)),
                  pl.BlockSpec((D, PROJW), lambda i: (0, 0), pipeline_mode=pl.Buffered(1))],
        out_specs=[pl.BlockSpec((tm, PROJW), lambda i: (i, 0)),
                   pl.BlockSpec((tm, D), lambda i: (i, 0))],
        out_shape=[jax.ShapeDtypeStruct((T, PROJW), F32), jax.ShapeDtypeStruct((T, D), BF16)],
        compiler_params=_cp(),
    )(h, norm, w)


def _inproj_bwd(h, dres, dlo, dhi, norm, w, name):
    T = h.shape[0]
    tm = _tile(T, (512, 384, 256, 128))
    hw = PROJW // 2

    def body(h_ref, d_ref, lo_ref, hi_ref, n_ref, wlo_ref, whi_ref, dh_ref, dn_ref):
        @pl.when(pl.program_id(0) == 0)
        def _():
            dn_ref[...] = jnp.zeros_like(dn_ref)

        _, xh, r = _rms(h_ref[...], n_ref[...], D)
        du = _dot(lo_ref[...], wlo_ref[...], NT) + _dot(hi_ref[...], whi_ref[...], NT)
        dn_ref[...] += _colsum(du * xh)
        dh_ref[...] = d_ref[...] + _rms_bwd(du, xh, r, n_ref[...], D)

    return _call(
        body, name=name, grid=(T // tm,),
        in_specs=[pl.BlockSpec((tm, D), lambda i: (i, 0)),
                  pl.BlockSpec((tm, D), lambda i: (i, 0)),
                  pl.BlockSpec((tm, hw), lambda i: (i, 0)),
                  pl.BlockSpec((tm, hw), lambda i: (i, 0)),
                  pl.BlockSpec((1, D), lambda i: (0, 0)),
                  pl.BlockSpec((D, hw), lambda i: (0, 0)),
                  pl.BlockSpec((D, hw), lambda i: (0, 1))],
        out_specs=[pl.BlockSpec((tm, D), lambda i: (i, 0)),
                   pl.BlockSpec((1, D), lambda i: (0, 0))],
        out_shape=[jax.ShapeDtypeStruct((T, D), F32), jax.ShapeDtypeStruct((1, D), F32)],
        compiler_params=_cp(),
    )(h, dres, dlo, dhi, norm, w, w)


C_FQ, C_FK, C_FV, C_CQ, C_CKV, C_MISC, C_GATE = 0, 512, 1024, 1536, 1792, 1920, 2048
L_KR, L_FL = 64, 96


def _prep_fwd(proj, rc, rs, gfq, gfk, gcq, gckv, gmq, gmk, bfv, wuq, wuk, wuv, name):
    T = proj.shape[0]
    tm = _tile(T, (256, 128))

    def body(p_ref, rc_ref, rs_ref, gfq_ref, gfk_ref, gcq_ref, gckv_ref, gmq_ref, gmk_ref, bf_ref,
             wuq_ref, wuk_ref, wuv_ref, fq_ref, fk_ref, fv_ref, qm_ref, km_ref, vm_ref, lf_ref):
        o64, o128, o256 = _seg_ones(128, 6), _seg_ones(128, 7), _seg_ones(256, 8)
        for blk in range(4):
            for (c0, g_ref, o_ref) in ((C_FQ, gfq_ref, fq_ref), (C_FK, gfk_ref, fk_ref)):
                x = p_ref[:, c0 + 128 * blk:c0 + 128 * (blk + 1)]
                fn, _, _ = _rms(x, g_ref[...], 64, o64)
                o_ref[:, 128 * blk:128 * (blk + 1)] = fn.astype(BF16)
        fv_ref[...] = p_ref[:, C_FV:C_FV + 512].astype(BF16)

        rcv = rc_ref[...]
        rsv = rs_ref[...]
        cqn, _, _ = _rms(p_ref[:, C_CQ:C_CQ + QR], gcq_ref[...], QR, o256)
        qpre = _dot(cqn.astype(BF16), wuq_ref[...])
        ckvn, _, _ = _rms(p_ref[:, C_CKV:C_CKV + KVR], gckv_ref[...], KVR, o128)
        ckvb = ckvn.astype(BF16)
        kpre = _dot(ckvb, wuk_ref[...])
        vm_ref[...] = _dot(ckvb, wuv_ref[...]).astype(BF16)
        misc = p_ref[:, C_MISC:C_MISC + 128]
        ln = _lane(misc.shape)
        kr = jnp.where(jnp.logical_and(ln >= L_KR, ln < L_KR + ROPE), misc, 0.0)
        for hh in range(NH):
            sl = slice(128 * hh, 128 * (hh + 1))
            qn, _, _ = _rms(qpre[:, sl], gmq_ref[...], MLA_QK, o128)
            qm_ref[:, sl] = (qn * rcv + _rope_swap(qn) * rsv).astype(BF16)
            kn, _, _ = _rms(kpre[:, sl] + kr, gmk_ref[...], MLA_QK, o128)
            km_ref[:, sl] = (kn * rcv + _rope_swap(kn) * rsv).astype(BF16)
        z = misc + bf_ref[...]
        lf_ref[...] = jnp.minimum(z, 0.0) - jnp.log(1.0 + jnp.exp(-jnp.abs(z)))

    row = lambda w: pl.BlockSpec((tm, w), lambda i: (i, 0))
    full = lambda a: pl.BlockSpec(a.shape, lambda i: (0, 0))
    return _call(
        body, name=name, grid=(T // tm,),
        in_specs=[row(PROJW // 2), row(128), row(128)] + [full(a) for a in (gfq, gfk, gcq, gckv, gmq, gmk, bfv, wuq, wuk, wuv)],
        out_specs=[row(512), row(512), row(512), row(1024), row(1024), row(512), row(128)],
        out_shape=[jax.ShapeDtypeStruct((T, 512), BF16), jax.ShapeDtypeStruct((T, 512), BF16),
                   jax.ShapeDtypeStruct((T, 512), BF16), jax.ShapeDtypeStruct((T, 1024), BF16),
                   jax.ShapeDtypeStruct((T, 1024), BF16), jax.ShapeDtypeStruct((T, 512), BF16),
                   jax.ShapeDtypeStruct((T, 128), F32)],
        compiler_params=_cp(),
    )(proj, rc, rs, gfq, gfk, gcq, gckv, gmq, gmk, bfv, wuq, wuk, wuv)


def _prep_bwd(proj, rc, rs, gfq, gfk, gcq, gckv, gmq, gmk, bfv, wuq, wuk, wuv,
              dfq, dfk, dfv, dqm, dkm, dvm, dlf, name):
    T = proj.shape[0]
    tm = _tile(T, (256, 128))

    def body(p_ref, rc_ref, rs_ref, gfq_ref, gfk_ref, gcq_ref, gckv_ref, gmq_ref, gmk_ref, bf_ref,
             wuq_ref, wuk_ref, wuv_ref, dfq_ref, dfk_ref, dfv_ref, dqm_ref, dkm_ref, dvm_ref, dlf_ref,
             dp_ref, dgfq_ref, dgfk_ref, dgcq_ref, dgckv_ref, dgmq_ref, dgmk_ref, dbf_ref,
             dwuq_ref, dwuk_ref, dwuv_ref, dqpre_sc, dkpre_sc):
        accs = (dgfq_ref, dgfk_ref, dgcq_ref, dgckv_ref, dgmq_ref, dgmk_ref, dbf_ref, dwuq_ref, dwuk_ref, dwuv_ref)

        @pl.when(pl.program_id(0) == 0)
        def _():
            for a in accs:
                a[...] = jnp.zeros_like(a)

        o64, o128, o256 = _seg_ones(128, 6), _seg_ones(128, 7), _seg_ones(256, 8)
        for (c0, g_ref, d_ref, dg_ref) in ((C_FQ, gfq_ref, dfq_ref, dgfq_ref), (C_FK, gfk_ref, dfk_ref, dgfk_ref)):
            dg = jnp.zeros((1, 128), F32)
            for blk in range(4):
                x = p_ref[:, c0 + 128 * blk:c0 + 128 * (blk + 1)]
                _, xh, r = _rms(x, g_ref[...], 64, o64)
                dy = d_ref[:, 128 * blk:128 * (blk + 1)]
                dg = dg + _colsum(dy * xh)
                dp_ref[:, c0 + 128 * blk:c0 + 128 * (blk + 1)] = _rms_bwd(dy, xh, r, g_ref[...], 64, o64).astype(BF16)
            dg_ref[...] += dg
        dp_ref[:, C_FV:C_FV + 512] = dfv_ref[...].astype(BF16)

        rcv = rc_ref[...]
        rsv = rs_ref[...]
        cqn, cqh, cqr = _rms(p_ref[:, C_CQ:C_CQ + QR], gcq_ref[...], QR, o256)
        cqb = cqn.astype(BF16)
        qpre = _dot(cqb, wuq_ref[...])
        dgq = jnp.zeros((1, 128), F32)
        for hh in range(NH):
            sl = slice(128 * hh, 128 * (hh + 1))
            _, xh, r = _rms(qpre[:, sl], gmq_ref[...], MLA_QK, o128)
            dout = dqm_ref[:, sl]
            dqn = dout * rcv + _rope_swap(dout * rsv)
            dgq = dgq + _colsum(dqn * xh)
            dqpre_sc[:, sl] = _rms_bwd(dqn, xh, r, gmq_ref[...], MLA_QK, o128).astype(BF16)
        dgmq_ref[...] += dgq
        dqpre = dqpre_sc[...]
        dwuq_ref[...] += _dot(cqb, dqpre, TN)
        dcqn = _dot(dqpre, wuq_ref[...], NT)
        dgcq_ref[...] += _colsum(dcqn * cqh)
        dp_ref[:, C_CQ:C_CQ + QR] = _rms_bwd(dcqn, cqh, cqr, gcq_ref[...], QR, o256).astype(BF16)

        ckvn, ckvh, ckvr = _rms(p_ref[:, C_CKV:C_CKV + KVR], gckv_ref[...], KVR, o128)
        ckvb = ckvn.astype(BF16)
        kpre = _dot(ckvb, wuk_ref[...])
        misc = p_ref[:, C_MISC:C_MISC + 128]
        ln = _lane(misc.shape)
        is_kr = jnp.logical_and(ln >= L_KR, ln < L_KR + ROPE)
        kr = jnp.where(is_kr, misc, 0.0)
        dgk = jnp.zeros((1, 128), F32)
        dkr = jnp.zeros(misc.shape, F32)
        for hh in range(NH):
            sl = slice(128 * hh, 128 * (hh + 1))
            _, xh, r = _rms(kpre[:, sl] + kr, gmk_ref[...], MLA_QK, o128)
            dout = dkm_ref[:, sl]
            dkn = dout * rcv + _rope_swap(dout * rsv)
            dgk = dgk + _colsum(dkn * xh)
            dkx = _rms_bwd(dkn, xh, r, gmk_ref[...], MLA_QK, o128)
            dkr = dkr + jnp.where(is_kr, dkx, 0.0)
            dkpre_sc[:, sl] = jnp.where(ln < 64, dkx, 0.0).astype(BF16)
        dgmk_ref[...] += dgk
        dkpre = dkpre_sc[...]
        dvmb = dvm_ref[...].astype(BF16)
        dwuk_ref[...] += _dot(ckvb, dkpre, TN)
        dwuv_ref[...] += _dot(ckvb, dvmb, TN)
        dckvn = _dot(dkpre, wuk_ref[...], NT) + _dot(dvmb, wuv_ref[...], NT)
        dgckv_ref[...] += _colsum(dckvn * ckvh)
        dp_ref[:, C_CKV:C_CKV + KVR] = _rms_bwd(dckvn, ckvh, ckvr, gckv_ref[...], KVR, o128).astype(BF16)

        z = misc + bf_ref[...]
        dz = dlf_ref[...] * (1.0 - jax.nn.sigmoid(z))
        dbf_ref[...] += _colsum(dz)
        dp_ref[:, C_MISC:C_MISC + 128] = (dkr + dz).astype(BF16)

    row = lambda w: pl.BlockSpec((tm, w), lambda i: (i, 0))
    full = lambda a: pl.BlockSpec(a.shape, lambda i: (0, 0))
    small = (gfq, gfk, gcq, gckv, gmq, gmk, bfv, wuq, wuk, wuv)
    acc_shapes = [(1, 128), (1, 128), (1, QR), (1, KVR), (1, 128), (1, 128), (1, 128),
                  (QR, 1024), (KVR, 1024), (KVR, 512)]
    return _call(
        body, name=name, grid=(T // tm,),
        in_specs=[row(PROJW // 2), row(128), row(128)] + [full(a) for a in small]
                 + [row(512), row(512), row(512), row(1024), row(1024), row(512), row(128)],
        out_specs=[row(PROJW // 2)] + [pl.BlockSpec(s, lambda i: (0, 0)) for s in acc_shapes],
        out_shape=[jax.ShapeDtypeStruct((T, PROJW // 2), BF16)] + [jax.ShapeDtypeStruct(s, F32) for s in acc_shapes],
        scratch_shapes=[pltpu.VMEM((tm, 1024), BF16), pltpu.VMEM((tm, 1024), BF16)],
        compiler_params=_cp(),
    )(proj, rc, rs, *small, dfq, dfk, dfv, dqm, dkm, dvm, dlf)


def _scan_lanes(x, reverse):
    n = x.shape[-1]
    ln = _lane(x.shape)
    k = 1
    while k < n:
        if reverse:
            x = x + jnp.where(ln < n - k, pltpu.roll(x, n - k, x.ndim - 1), 0.0)
        else:
            x = x + jnp.where(ln >= k, pltpu.roll(x, k, x.ndim - 1), 0.0)
        k *= 2
    return x


def _forget_scan(lf, reverse, name):
    def body(x_ref, o_ref):
        x = x_ref[...]
        ln = _lane(x.shape)
        pad = jnp.logical_and(ln >= NMETA, ln < MPAD)
        o_ref[...] = jnp.where(pad, 0.0, _scan_lanes(jnp.where(pad, 0.0, x), reverse))

    return _call(body, name=name, out_shape=jax.ShapeDtypeStruct(lf.shape, F32), compiler_params=_cp())(lf)


def _attn_blocks(LP, tq):
    return [(0, MPAD, MPAD)] + [(MPAD + i * tq, tq, MPAD + (i + 1) * tq) for i in range((LP - MPAD) // tq)]


def _attn_scores(q_ref, k_ref, e, r0, rn, kend, wide, scale, bias):
    if wide:
        qe = q_ref[r0:r0 + rn, 128 * e:128 * (e + 1)]
        ke = k_ref[0:kend, 128 * e:128 * (e + 1)]
    else:
        qb = q_ref[r0:r0 + rn, :]
        mine = (_lane(qb.shape) < 64) if e == 0 else (_lane(qb.shape) >= 64)
        qe = jnp.where(mine, qb, jnp.zeros_like(qb))
        ke = k_ref[0:kend, :]
    s = _dot(qe, ke, NT) * scale
    if bias is not None:
        ct_ref, cr_ref = bias
        s = s + ct_ref[0, r0:r0 + rn, e:e + 1] - cr_ref[0, :, 0:kend]
    neg = -1e30
    if r0 == 0:
        qi = lax.broadcasted_iota(jnp.int32, (rn, kend), 0)
        ki = lax.broadcasted_iota(jnp.int32, (rn, kend), 1)
        s = jnp.where(jnp.logical_and(ki <= qi, ki < NMETA), s, neg)
    else:
        d0 = kend - rn
        head = jnp.where(_lane((rn, MPAD)) < NMETA, s[:, :MPAD], neg)
        qi = lax.broadcasted_iota(jnp.int32, (rn, rn), 0)
        diag = jnp.where(_lane((rn, rn)) <= qi, s[:, d0:], neg)
        s = jnp.concatenate([head] + ([s[:, MPAD:d0]] if d0 > MPAD else []) + [diag], axis=1)
    m = jnp.max(s, axis=-1, keepdims=True)
    p = jnp.exp(s - m)
    l = jnp.sum(p, axis=-1, keepdims=True)
    return qe, ke, p, l


def _attn_specs(B, LP, wide, has_bias):
    qw = 256 if wide else 128
    specs = [pl.BlockSpec((LP, qw), lambda b, hp: (b, hp)),
             pl.BlockSpec((LP, qw), lambda b, hp: (b, hp)),
             pl.BlockSpec((LP, 128), lambda b, hp: (b, hp))]
    bias_specs = []
    if has_bias:
        bias_specs = [pl.BlockSpec((1, LP, 2), lambda b, hp: (b * 4 + hp, 0, 0)),
                      pl.BlockSpec((1, 1, LP), lambda b, hp: (b * 8 + 2 * hp, 0, 0)),
                      pl.BlockSpec((1, 1, LP), lambda b, hp: (b * 8 + 2 * hp + 1, 0, 0))]
    return qw, specs, bias_specs


def _attn_fwd(q, k, v, bias, B, LP, wide, scale, name, tasks=()):
    T = q.shape[0]
    blocks = _attn_blocks(LP, ATTN_TQ)
    qw, specs, bias_specs = _attn_specs(B, LP, wide, bias is not None)

    def body(*refs):
        if bias is not None:
            q_ref, k_ref, v_ref, ct_ref, cr0_ref, cr1_ref, o_ref = refs
            crs = (cr0_ref, cr1_ref)
        else:
            q_ref, k_ref, v_ref, o_ref = refs
        for (r0, rn, kend) in blocks:
            outs = []
            for e in (0, 1):
                bs = (ct_ref, crs[e]) if bias is not None else None
                _, _, p, l = _attn_scores(q_ref, k_ref, e, r0, rn, kend, wide, scale, bs)
                outs.append(_dot(p.astype(BF16), v_ref[0:kend, :]) / l)
            o = jnp.where(_lane(outs[0].shape) < 64, outs[0], outs[1])
            o_ref[r0:r0 + rn, :] = o.astype(BF16)

    args = (q, k, v) + ((bias[0], bias[1], bias[1]) if bias is not None else ())
    (out,), touts = _call_tasks(
        body, tasks, name=name, grid=(B, 4),
        in_specs=specs + bias_specs,
        out_specs=[pl.BlockSpec((LP, 128), lambda b, hp: (b, hp))],
        out_shape=[jax.ShapeDtypeStruct((T, 512), BF16)],
        args=args)
    return out, touts


def _attn_bwd(q, k, v, do, bias, B, LP, wide, scale, name, tasks=()):
    T = q.shape[0]
    blocks = _attn_blocks(LP, ATTN_TQ)
    qw, specs, bias_specs = _attn_specs(B, LP, wide, bias is not None)
    has_bias = bias is not None

    def body(*refs):
        if has_bias:
            (q_ref, k_ref, v_ref, do_ref, ct_ref, cr0_ref, cr1_ref,
             dq_ref, dk_ref, dv_ref, dc0_ref, dc1_ref) = refs
            crs = (cr0_ref, cr1_ref)
            dcs = (dc0_ref, dc1_ref)
            dc0_ref[...] = jnp.zeros_like(dc0_ref)
            dc1_ref[...] = jnp.zeros_like(dc1_ref)
        else:
            q_ref, k_ref, v_ref, do_ref, dq_ref, dk_ref, dv_ref = refs
        dk_ref[...] = jnp.zeros_like(dk_ref)
        dv_ref[...] = jnp.zeros_like(dv_ref)
        for (r0, rn, kend) in blocks:
            dqs = []
            for e in (0, 1):
                bs = (ct_ref, crs[e]) if has_bias else None
                qe, ke, p, l = _attn_scores(q_ref, k_ref, e, r0, rn, kend, wide, scale, bs)
                pn = p * (1.0 / l)
                dob = do_ref[r0:r0 + rn, :]
                mine = (_lane(dob.shape) < 64) if e == 0 else (_lane(dob.shape) >= 64)
                doe = jnp.where(mine, dob, jnp.zeros_like(dob))
                dp = _dot(doe, v_ref[0:kend, :], NT)
                delta = jnp.sum(pn * dp, axis=-1, keepdims=True)
                ds = pn * (dp - delta)
                dsb = ds.astype(BF16)
                dqe = _dot(dsb, ke) * scale
                dke = _dot(dsb, qe, TN) * scale
                if wide:
                    dq_ref[r0:r0 + rn, 128 * e:128 * (e + 1)] = dqe
                    dk_ref[0:kend, 128 * e:128 * (e + 1)] += dke
                else:
                    dqs.append(dqe)
                    dk_ref[0:kend, :] += dke
                dv_ref[0:kend, :] += _dot(pn.astype(BF16), doe, TN)
                if has_bias:
                    dcs[e][0, :, 0:kend] -= _colsum(ds)
            if not wide:
                dq_ref[r0:r0 + rn, :] = jnp.where(_lane(dqs[0].shape) < 64, dqs[0], dqs[1])

    args = (q, k, v, do) + ((bias[0], bias[1], bias[1]) if has_bias else ())
    out_specs = [pl.BlockSpec((LP, qw), lambda b, hp: (b, hp)),
                 pl.BlockSpec((LP, qw), lambda b, hp: (b, hp)),
                 pl.BlockSpec((LP, 128), lambda b, hp: (b, hp))]
    out_shape = [jax.ShapeDtypeStruct(q.shape, F32), jax.ShapeDtypeStruct(q.shape, F32),
                 jax.ShapeDtypeStruct((T, 512), F32)]
    if has_bias:
        out_specs += [pl.BlockSpec((1, 1, LP), lambda b, hp: (b * 4 + hp, 0, 0))] * 2
        out_shape += [jax.ShapeDtypeStruct((B * 4, 1, LP), F32)] * 2
    return _call_tasks(
        body, tasks, name=name, grid=(B, 4),
        in_specs=specs + [pl.BlockSpec((LP, 128), lambda b, hp: (b, hp))] + bias_specs,
        out_specs=out_specs, out_shape=out_shape, args=args)


def _post_fwd(h, of, om, proj, bg, wbf, wbm, wout, name):
    T = h.shape[0]
    tm = _tile(T, (512, 384, 256, 128))

    def body(h_ref, of_ref, om_ref, gl_ref, bg_ref, wbf_ref, wbm_ref, wo_ref, o_ref, mix_ref):
        gate = jax.nn.sigmoid(gl_ref[...] + bg_ref[...])
        mix = gate[:, :D] * _dot(of_ref[...], wbf_ref[...]) + gate[:, D:] * _dot(om_ref[...], wbm_ref[...])
        mb = mix.astype(BF16)
        mix_ref[...] = mb
        o_ref[...] = h_ref[...] + _dot(mb, wo_ref[...])

    row = lambda w: pl.BlockSpec((tm, w), lambda i: (i, 0))
    full = lambda a: pl.BlockSpec(a.shape, lambda i: (0, 0))
    return _call(
        body, name=name, grid=(T // tm,),
        in_specs=[row(D), row(512), row(512), pl.BlockSpec((tm, 2 * D), lambda i: (i, 1)),
                  full(bg), full(wbf), full(wbm), full(wout)],
        out_specs=[row(D), row(D)],
        out_shape=[jax.ShapeDtypeStruct((T, D), F32), jax.ShapeDtypeStruct((T, D), BF16)],
        compiler_params=_cp(),
    )(h, of, om, proj, bg, wbf, wbm, wout)


def _post_bwd(dh, of, om, proj, bg, wbf, wbm, wout, name):
    T = dh.shape[0]
    tm = _tile(T, (512, 384, 256, 128))

    def body(d_ref, of_ref, om_ref, gl_ref, bg_ref, wbf_ref, wbm_ref, wo_ref,
             dgl_ref, dbf_ref, dbm_ref, dof_ref, dom_ref, dbg_ref):
        @pl.when(pl.program_id(0) == 0)
        def _():
            dbg_ref[...] = jnp.zeros_like(dbg_ref)

        gate = jax.nn.sigmoid(gl_ref[...] + bg_ref[...])
        dmix = _dot(d_ref[...].astype(BF16), wo_ref[...], NT)
        ofx = _dot(of_ref[...], wbf_ref[...])
        omx = _dot(om_ref[...], wbm_ref[...])
        gf = gate[:, :D]
        gm = gate[:, D:]
        dof = (dmix * gf).astype(BF16)
        dom = (dmix * gm).astype(BF16)
        dglf = dmix * ofx * gf * (1.0 - gf)
        dglm = dmix * omx * gm * (1.0 - gm)
        dgl_ref[:, :D] = dglf.astype(BF16)
        dgl_ref[:, D:] = dglm.astype(BF16)
        dbg_ref[:, :D] += _colsum(dglf)
        dbg_ref[:, D:] += _colsum(dglm)
        dbf_ref[...] = dof
        dbm_ref[...] = dom
        dof_ref[...] = _dot(dof, wbf_ref[...], NT).astype(BF16)
        dom_ref[...] = _dot(dom, wbm_ref[...], NT).astype(BF16)

    row = lambda w: pl.BlockSpec((tm, w), lambda i: (i, 0))
    full = lambda a: pl.BlockSpec(a.shape, lambda i: (0, 0))
    return _call(
        body, name=name, grid=(T // tm,),
        in_specs=[row(D), row(512), row(512), pl.BlockSpec((tm, 2 * D), lambda i: (i, 1)),
                  full(bg), full(wbf), full(wbm), full(wout)],
        out_specs=[row(2 * D), row(D), row(D), row(512), row(512), pl.BlockSpec((1, 2 * D), lambda i: (0, 0))],
        out_shape=[jax.ShapeDtypeStruct((T, 2 * D), BF16), jax.ShapeDtypeStruct((T, D), BF16),
                   jax.ShapeDtypeStruct((T, D), BF16), jax.ShapeDtypeStruct((T, 512), BF16),
                   jax.ShapeDtypeStruct((T, 512), BF16), jax.ShapeDtypeStruct((1, 2 * D), F32)],
        compiler_params=_cp(),
    )(dh, of, om, proj, bg, wbf, wbm, wout)


def _loss_head(h3, target, B, LP, name):
    S = LP - MPAD
    nb = LP // 128

    def body(h_ref, t_ref, dy_ref, l_ref):
        b = pl.program_id(0)
        p = pl.program_id(1)

        @pl.when(jnp.logical_and(b == 0, p == 0))
        def _():
            l_ref[...] = jnp.zeros_like(l_ref)

        @pl.when(p == 0)
        def _():
            dy_ref[...] = jnp.zeros_like(dy_ref)

        @pl.when(p > 0)
        def _():
            e = h_ref[...] - t_ref[0]
            dy_ref[...] = e * (1.0 / D)
            l_ref[...] += jnp.sum(e * e, axis=0, keepdims=True) * (0.5 / D)

    return _call(
        body, name=name, grid=(B, nb),
        in_specs=[pl.BlockSpec((128, D), lambda b, p: (b * nb + p, 0)),
                  pl.BlockSpec((1, 128, D), lambda b, p: (b, jnp.maximum(p - 1, 0), 0))],
        out_specs=[pl.BlockSpec((128, D), lambda b, p: (b * nb + p, 0)),
                   pl.BlockSpec((1, D), lambda b, p: (0, 0))],
        out_shape=[jax.ShapeDtypeStruct(h3.shape, F32), jax.ShapeDtypeStruct((1, D), F32)],
        compiler_params=_cp(),
    )(h3, target)


def _rope_tables(B, LP):
    pos = jnp.concatenate([jnp.arange(MPAD, dtype=F32), NMETA + jnp.arange(LP - MPAD, dtype=F32)])
    inv_freq = ROPE_THETA ** (-jnp.arange(0, ROPE, 2, dtype=F32) / ROPE)
    ang = pos[:, None] * inv_freq[None, :]
    cos, sin = jnp.cos(ang), jnp.sin(ang)
    z32 = jnp.zeros((LP, 32), F32)
    rc = jnp.concatenate([jnp.ones((LP, 64), F32), cos, cos, z32], axis=1)
    rs = jnp.concatenate([jnp.zeros((LP, 64), F32), -sin, sin, z32], axis=1)
    return jnp.tile(rc, (B, 1)), jnp.tile(rs, (B, 1))


def _pad_lanes(v, start, width=128):
    n = v.shape[1]
    return jnp.concatenate([jnp.zeros((1, start), F32), v, jnp.zeros((1, width - start - n), F32)], axis=1)


G_FFN1 = ["ffn1_w_gu", "ffn1_w_down"]
G_MIX = ["w_in", "mla_w_uq", "mla_w_ukv", "w_branch_fox", "w_branch_mla", "w_out"]
G_OUT = ["w_out", "w_branch_fox", "w_branch_mla"]
G_IN = ["w_in", "mla_w_uq", "mla_w_ukv"]


def _step(x, target, meta, vec, gath, shards):
    dist = shards is not None
    B, S, _ = x.shape
    LP = MPAD + S
    T = B * LP
    gath = dict(gath)

    def gather(names, wide=()):
        return [_gather_task([shards[n] for n in names], wide)] if dist else []

    def flat_gu(w):
        return w if w.ndim == 2 else _cols_from_shards(w)

    def gathered(names, touts):
        if dist:
            gath.update(zip(names, touts[0]))

    g4, sums, red = {}, {}, {}

    def scatter(names):
        return [_a2a_task([_pieces(g4[n]) for n in names])] if dist else []

    def scattered(names, tout, me):
        for n, r in zip(names, tout):
            sums[n] = _sum_pieces(r, _pieces(g4[n]), me, "rs_sum_" + n)

    def join(names):
        return [_join_task([sums[n] for n in names])] if dist else []

    def joined(names, tout):
        for n, r in zip(names, tout):
            red[n] = (sums[n], r)

    me = None
    if dist:
        me = (4 * lax.axis_index("x") + 2 * lax.axis_index("y") + lax.axis_index("c")).reshape(1).astype(jnp.int32)

    h0 = jnp.concatenate([jnp.broadcast_to(meta[None], (B, NMETA, D)),
                          jnp.zeros((B, MPAD - NMETA, D), F32), x], axis=1).reshape(T, D)
    rc, rs = _rope_tables(B, LP)
    gfq = jnp.tile(vec["fox_q_norm"], (1, 2))
    gfk = jnp.tile(vec["fox_k_norm"], (1, 2))
    gmq = _pad_lanes(vec["mla_q_norm"], 0)
    gmk = _pad_lanes(vec["mla_k_norm"], 0)
    bfv = _pad_lanes(vec["b_forget"], L_FL)

    w1gu, w1d = flat_gu(gath["ffn1_w_gu"]), gath["ffn1_w_down"].reshape(DFF, D)
    h1, touts = _ffn_fwd(h0, vec["ffn1_norm"], w1gu, w1d, "ffn1_fwd", gather(G_MIX))
    gathered(G_MIX, touts)
    wm = _mixer_weights(gath)
    small = (gfq, gfk, vec["mla_cq_norm"], vec["mla_ckv_norm"], gmq, gmk, bfv, wm["wuq"], wm["wuk"], wm["wuv"])
    proj, u2 = _inproj_fwd(h1, vec["mix_norm"], wm["w_in"], "inproj_fwd")
    fq, fk, fv, qm, km, vm, lf = _prep_fwd(proj, rc, rs, *small, name="prep_fwd")
    lf_rows = lf[:, L_FL:L_FL + NH].reshape(B, LP, NH).transpose(0, 2, 1).reshape(B * NH, LP)
    crow = _forget_scan(lf_rows, False, "forget_scan")
    ctok = crow.reshape(B, 4, 2, LP).transpose(0, 1, 3, 2).reshape(B * 4, LP, 2)
    bias = (ctok, crow.reshape(B * NH, 1, LP))
    of, touts = _attn_fwd(fq, fk, fv, bias, B, LP, False, 64 ** -0.5, "fox_fwd", gather(["ffn2_w_gu"], wide=(0,)))
    gathered(["ffn2_w_gu"], touts)
    om, touts = _attn_fwd(qm, km, vm, None, B, LP, True, MLA_QK ** -0.5, "mla_fwd", gather(["ffn2_w_down"]))
    gathered(["ffn2_w_down"], touts)
    h2, mix = _post_fwd(h1, of, om, proj, vec["b_gate"], wm["wbf"], wm["wbm"], wm["w_out"], "post_fwd")
    w2gu, w2d = flat_gu(gath["ffn2_w_gu"]), gath["ffn2_w_down"].reshape(DFF, D)
    h3, _ = _ffn_fwd(h2, vec["ffn2_norm"], w2gu, w2d, "ffn2_fwd")
    dy, lpart = _loss_head(h3, target, B, LP, "loss_head")

    gv = {}
    (dh2, u3, a2, dgp2, gv["ffn2_norm"]), _ = _ffn_bwd(h2, dy, vec["ffn2_norm"], w2gu, w2d, "ffn2_bwd")
    g4["ffn2_w_gu"] = _wgrad(u3, dgp2, "ffn2_dwgu", bn=DFF, shards=2)[0]
    g4["ffn2_w_down"] = _wgrad(a2, dy, "ffn2_dwd", scale=0.5, bk=FH)[0].reshape(N_CHIPS, DFF // N_CHIPS, D)

    dgl, dbf, dbm, dof, dom, gv["b_gate"] = _post_bwd(dh2, of, om, proj, vec["b_gate"], wm["wbf"], wm["wbm"], wm["w_out"], "post_bwd")
    g4["w_out"] = _wgrad(mix, dh2, "dw_out", bt=1088)[0].reshape(N_CHIPS, D // N_CHIPS, D)
    g4["w_branch_fox"] = _cols_to_shards(_wgrad(of, dbf, "dw_bf", bt=1088)[0])
    g4["w_branch_mla"] = _cols_to_shards(_wgrad(om, dbm, "dw_bm", bt=1088)[0])
    G_FFN2 = ["ffn2_w_gu", "ffn2_w_down"]
    (dfq, dfk, dfv, dc0, dc1), touts = _attn_bwd(fq, fk, fv, dof, bias, B, LP, False, 64 ** -0.5, "fox_bwd", scatter(G_FFN2))
    if dist:
        scattered(G_FFN2, touts[0], me)
    (dqm, dkm, dvm), touts = _attn_bwd(qm, km, vm, dom, None, B, LP, True, MLA_QK ** -0.5, "mla_bwd",
                                       scatter(G_OUT) + join(G_FFN2))
    if dist:
        scattered(G_OUT, touts[0], me)
        joined(G_FFN2, touts[1])
    dc = jnp.concatenate([dc0, dc1], axis=1).reshape(B * NH, LP)
    dlf_rows = _forget_scan(dc, True, "forget_scan_bwd")
    dlf = dlf_rows.reshape(B, NH, LP).transpose(0, 2, 1).reshape(T, NH)
    dlf = jnp.concatenate([jnp.zeros((T, L_FL), F32), dlf, jnp.zeros((T, 128 - L_FL - NH), F32)], axis=1)
    (dlo, dgfq, dgfk, gv["mla_cq_norm"], gv["mla_ckv_norm"], dgmq, dgmk, dbfv,
     dwuq, dwuk, dwuv) = _prep_bwd(proj, rc, rs, *small, dfq, dfk, dfv, dqm, dkm, dvm, dlf, name="prep_bwd")
    gv["fox_q_norm"] = dgfq[:, :64] + dgfq[:, 64:]
    gv["fox_k_norm"] = dgfk[:, :64] + dgfk[:, 64:]
    gv["mla_q_norm"] = dgmq[:, :MLA_QK]
    gv["mla_k_norm"] = dgmk[:, :MLA_QK]
    gv["b_forget"] = dbfv[:, L_FL:L_FL + NH]
    dwin = jnp.concatenate([_wgrad(u2, dlo, "dw_in_lo")[0], _wgrad(u2, dgl, "dw_in_hi")[0]], axis=1)
    g4["w_in"] = _cols_to_shards(_win_from_kernel(dwin))
    g4["mla_w_uq"] = _cols_to_shards(
        dwuq.astype(GRAD_DTYPE).reshape(QR, NH, 128)[:, :, :MLA_QK].reshape(QR, NH * MLA_QK))
    dukv = jnp.concatenate([dwuk.reshape(KVR, NH, 128)[:, :, :64], dwuv.reshape(KVR, NH, 64)], axis=2)
    g4["mla_w_ukv"] = _cols_to_shards(dukv.astype(GRAD_DTYPE).reshape(KVR, NH * 128))
    dh1, gv["mix_norm"] = _inproj_bwd(h1, dh2, dlo, dgl, vec["mix_norm"], wm["w_in"], "inproj_bwd")

    (dh0, u1, a1, dgp1, gv["ffn1_norm"]), touts = _ffn_bwd(h0, dh1, vec["ffn1_norm"], w1gu, w1d, "ffn1_bwd",
                                                            scatter(G_IN) + join(G_OUT))
    if dist:
        scattered(G_IN, touts[0], me)
        joined(G_OUT, touts[1])
    dh0 = dh0.reshape(B, LP, D)
    grad_x = dh0[:, MPAD:]
    grad_meta = jnp.sum(dh0[:, :NMETA], axis=0)
    share = [_share_task([_stack_vectors([gv[n] for n in VECS]), grad_meta, lpart])] if dist else []
    g4["ffn1_w_gu"], touts = _wgrad(u1, dgp1, "ffn1_dwgu", bn=DFF, shards=2, tasks=share)
    shared = touts[0] if dist else None
    dwd1, touts = _wgrad(a1, dh1, "ffn1_dwd", scale=0.5, bk=FH, tasks=scatter(["ffn1_w_gu"]) + join(G_IN))
    g4["ffn1_w_down"] = dwd1.reshape(N_CHIPS, DFF // N_CHIPS, D)
    if dist:
        scattered(["ffn1_w_gu"], touts[0], me)
        joined(G_IN, touts[1])
        scattered(["ffn1_w_down"], _run_tasks(scatter(["ffn1_w_down"]), "rs_ffn1_w_down")[0], me)
        joined(G_FFN1, _run_tasks(join(G_FFN1), "rs_join_ffn1")[0])
    return lpart, grad_x, grad_meta, gv, (red if dist else g4), shared


def _cols_from_shards(g4):
    n, r, c = g4.shape
    return g4.transpose(1, 0, 2).reshape(r, n * c)


def _cols_to_shards(full):
    r, c4 = full.shape
    return full.reshape(r, N_CHIPS, c4 // N_CHIPS).transpose(1, 0, 2)


def _win_to_kernel(wfull):
    z = lambda n: jnp.zeros((D, n), wfull.dtype)
    fl, cq, ckv, kr, gate = (wfull[:, 1536:1544], wfull[:, 1544:1800], wfull[:, 1800:1928],
                             wfull[:, 1928:1960], wfull[:, 1960:4008])
    misc = jnp.concatenate([z(L_KR), kr, fl, z(128 - L_FL - NH)], axis=1)
    return jnp.concatenate([wfull[:, :1536], cq, ckv, misc, gate], axis=1)


def _win_from_kernel(gk):
    m = C_MISC
    return jnp.concatenate([gk[:, :1536], gk[:, m + L_FL:m + L_FL + NH], gk[:, C_CQ:C_CQ + QR],
                            gk[:, C_CKV:C_CKV + KVR], gk[:, m + L_KR:m + L_KR + ROPE], gk[:, C_GATE:]], axis=1)


def _pieces(g4):
    n, r, c = g4.shape
    return g4.reshape(2 * n, r // 2, c)


def _mixer_weights(gath):
    w = {}
    w["w_in"] = _win_to_kernel(_cols_from_shards(gath["w_in"]))
    uq = _cols_from_shards(gath["mla_w_uq"]).reshape(QR, NH, MLA_QK)
    w["wuq"] = jnp.pad(uq, ((0, 0), (0, 0), (0, 128 - MLA_QK))).reshape(QR, NH * 128)
    ukv = _cols_from_shards(gath["mla_w_ukv"]).reshape(KVR, NH, 128)
    w["wuk"] = jnp.pad(ukv[:, :, :64], ((0, 0), (0, 0), (0, 64))).reshape(KVR, NH * 128)
    w["wuv"] = ukv[:, :, 64:].reshape(KVR, NH * 64)
    w["wbf"] = _cols_from_shards(gath["w_branch_fox"])
    w["wbm"] = _cols_from_shards(gath["w_branch_mla"])
    w["w_out"] = gath["w_out"].reshape(D, D)
    return w


def _chip_peers(x, y):
    return [(1 - x, y), (x, 1 - y), (1 - x, 1 - y)]


RELS = [(dx, dy, dc) for dx in (0, 1) for dy in (0, 1) for dc in (0, 1)][1:]


def _here():
    return lax.axis_index("x"), lax.axis_index("y"), lax.axis_index("c")


def _flip(a, d):
    return (1 - a) if d else a


def _remote(src, dst, send, recv, i, dev):
    return functools.partial(pltpu.make_async_remote_copy, src_ref=src, dst_ref=dst, send_sem=send.at[i],
                             recv_sem=recv.at[i], device_id=dev, device_id_type=MESH)


def _gathered_shape(s, wide):
    return jax.ShapeDtypeStruct((s.shape[0], N_CHIPS * s.shape[1]) if wide else (N_CHIPS,) + s.shape, s.dtype)


def _slot(ref, j, shard, wide, rows=None):
    if wide:
        lanes = pl.ds(pl.multiple_of(j * shard.shape[1], 128), shard.shape[1])
        return ref.at[slice(None) if rows is None else rows, lanes]
    return ref.at[j] if rows is None else ref.at[j, rows]


def _gather_task(shards, wide=()):
    n = len(shards)

    def descs(ins, outs, sems):
        send, recv, loc = sems
        x, y, c = _here()
        j = 2 * x + y
        locs, pairs = [], []
        for k in range(n):
            at = functools.partial(_slot, outs[k], shard=shards[k], wide=k in wide)
            locs.append(functools.partial(pltpu.make_async_copy, ins[k], at(j), loc.at[k]))
            for r, (px, py) in enumerate(_chip_peers(x, y)):
                dev = (px, py, c)
                pairs.append((_remote(ins[k], at(j), send, recv, 3 * k + r, dev),
                              _remote(ins[k], at(2 * px + py), send, recv, 3 * k + r, dev)))
        return locs, pairs

    return _Task(shards, [_gathered_shape(s, k in wide) for k, s in enumerate(shards)],
                 [pltpu.SemaphoreType.DMA((3 * n,)), pltpu.SemaphoreType.DMA((3 * n,)), pltpu.SemaphoreType.DMA((n,))],
                 descs)


class _SplitGather(_Task):
    PARTS = 2

    def __init__(self, shards, wide=()):
        n = 3 * len(shards) * self.PARTS
        dma = pltpu.SemaphoreType.DMA
        self.wide = wide
        super().__init__(shards, [_gathered_shape(s, k in wide) for k, s in enumerate(shards)],
                         [dma((n,)), dma((n,)), dma((n,)), dma((n,)), dma((len(shards),))], None)

    def _plan(self, ins, outs, sems):
        send, recv, fsend, frecv, loc = sems
        x, y, c = _here()
        j = 2 * x + y
        locs, first, passed = [], [], []
        for k in range(len(ins)):
            h = self.ins[k].shape[0] // 2
            parts = self.PARTS if h % (32 * self.PARTS) == 0 else 1
            hp = h // parts
            at = functools.partial(_slot, outs[k], shard=self.ins[k], wide=k in self.wide)
            locs.append(functools.partial(pltpu.make_async_copy, ins[k], at(j), loc.at[k]))
            for r, (px, py) in enumerate(_chip_peers(x, y)):
                p = 2 * px + py
                for q in range(parts):
                    i = (3 * k + r) * self.PARTS + q
                    mine = pl.ds(pl.multiple_of(c * h + q * hp, 8), hp)
                    theirs = pl.ds(pl.multiple_of((1 - c) * h + q * hp, 8), hp)
                    first.append((_remote(ins[k].at[mine], at(j, rows=mine), send, recv, i, (px, py, c)),
                                  _remote(ins[k].at[mine], at(p, rows=mine), send, recv, i, (px, py, c))))
                    passed.append((_remote(at(p, rows=mine), at(p, rows=mine), fsend, frecv, i, (x, y, 1 - c)),
                                   _remote(at(p, rows=mine), at(p, rows=theirs), fsend, frecv, i, (x, y, 1 - c))))
        return locs, first, passed

    def start(self, ins, outs, sems):
        locs, first, _ = self._plan(ins, outs, sems)
        for lc in locs:
            lc().start()
        for snd, _ in first:
            snd().start()

    def wait(self, ins, outs, sems):
        locs, first, passed = self._plan(ins, outs, sems)
        for (_, landed), (pass_on, _) in zip(first, passed):
            landed().wait_recv()
            pass_on().start()
        for _, rcv in passed:
            rcv().wait_recv()
        for snd, _ in first + passed:
            snd().wait_send()
        for lc in locs:
            lc().wait()


def _a2a_task(ps):
    n = len(ps)
    nr = len(RELS)

    def descs(ins, outs, sems):
        send, recv = sems
        x, y, c = _here()
        me = 4 * x + 2 * y + c
        pairs = []
        for k in range(n):
            for i, (dx, dy, dc) in enumerate(RELS):
                dev = (_flip(x, dx), _flip(y, dy), _flip(c, dc))
                peer = 4 * dev[0] + 2 * dev[1] + dev[2]
                pairs.append((_remote(ins[k].at[peer], outs[k].at[me], send, recv, nr * k + i, dev),
                              _remote(ins[k].at[peer], outs[k].at[peer], send, recv, nr * k + i, dev)))
        return [], pairs

    return _Task(ps, [jax.ShapeDtypeStruct(p.shape, p.dtype) for p in ps],
                 [pltpu.SemaphoreType.DMA((nr * n,)), pltpu.SemaphoreType.DMA((nr * n,))], descs)


def _share_task(vs):
    n = len(vs)
    nr = len(RELS)

    def descs(ins, outs, sems):
        send, recv, loc = sems
        x, y, c = _here()
        me = 4 * x + 2 * y + c
        locs, pairs = [], []
        for k in range(n):
            locs.append(functools.partial(pltpu.make_async_copy, ins[k], outs[k].at[me], loc.at[k]))
            for i, (dx, dy, dc) in enumerate(RELS):
                dev = (_flip(x, dx), _flip(y, dy), _flip(c, dc))
                peer = 4 * dev[0] + 2 * dev[1] + dev[2]
                pairs.append((_remote(ins[k], outs[k].at[me], send, recv, nr * k + i, dev),
                              _remote(ins[k], outs[k].at[peer], send, recv, nr * k + i, dev)))
        return locs, pairs

    return _Task(vs, [jax.ShapeDtypeStruct((N_DEV,) + v.shape, v.dtype) for v in vs],
                 [pltpu.SemaphoreType.DMA((nr * n,)), pltpu.SemaphoreType.DMA((nr * n,)), pltpu.SemaphoreType.DMA((n,))],
                 descs)


def _join_task(ss):
    n = len(ss)

    def descs(ins, outs, sems):
        send, recv = sems
        x, y, c = _here()
        pairs = []
        for k in range(n):
            cp = _remote(ins[k], outs[k], send, recv, k, (x, y, 1 - c))
            pairs.append((cp, cp))
        return [], pairs

    return _Task(ss, [jax.ShapeDtypeStruct(s.shape, s.dtype) for s in ss],
                 [pltpu.SemaphoreType.DMA((n,)), pltpu.SemaphoreType.DMA((n,))], descs)


def _sum_pieces(recv, own, me, name):
    n, h, c = recv.shape
    tr = h

    def body(me_ref, r_ref, o_ref, out_ref):
        s = pl.program_id(1)
        val = jnp.where(s == me_ref[0], o_ref[0], r_ref[0]).astype(F32)

        @pl.when(s == 0)
        def _():
            out_ref[...] = val

        @pl.when(s > 0)
        def _():
            out_ref[...] += val

    def other(s, m):
        return jnp.where(s == m[0], (s + 1) % n, s)

    return _call(
        body, name=name,
        grid_spec=pltpu.PrefetchScalarGridSpec(
            num_scalar_prefetch=1, grid=(h // tr, n),
            in_specs=[pl.BlockSpec((1, tr, c), lambda i, s, m: (other(s, m), i, 0)),
                      pl.BlockSpec((1, tr, c), lambda i, s, m: (m[0], i, 0))],
            out_specs=pl.BlockSpec((tr, c), lambda i, s, m: (i, 0))),
        out_shape=jax.ShapeDtypeStruct((h, c), F32),
        compiler_params=_cp(),
    )(me, recv, own)


def _adamw_update(gg, w, m, v):
    c1 = 1.0 / (1.0 - ADAM_B1 ** ADAM_STEP)
    c2 = 1.0 / (1.0 - ADAM_B2 ** ADAM_STEP)
    nm = ADAM_B1 * m + (1.0 - ADAM_B1) * gg
    nv = ADAM_B2 * v + (1.0 - ADAM_B2) * (gg * gg)
    return -ADAM_LR * ((nm * c1) / (jnp.sqrt(nv * c2) + ADAM_EPS) + ADAM_WD * w), nm, nv


def _adamw_small(gvec8, gmeta8, lp8, chip, ws, ms, vs, name):
    na = len(ws)

    def dev_sum(ref):
        acc = ref[0]
        for s in range(1, N_DEV):
            acc = acc + ref[s]
        return acc

    def body(c_ref, gv_ref, gm_ref, lp_ref, *refs):
        w_refs, m_refs, v_refs = refs[:na], refs[na:2 * na], refs[2 * na:3 * na]
        l_ref = refs[3 * na]
        outs = refs[3 * na + 1:]
        g_refs, d_refs, nm_refs, nv_refs = outs[:na], outs[na:2 * na], outs[2 * na:3 * na], outs[3 * na:]
        l_ref[...] = dev_sum(lp_ref)
        gvec = dev_sum(gv_ref)
        for k in range(na):
            gg = gvec[k:k + 1, 0:ws[k].shape[1]] if k < na - 1 else dev_sum(gm_ref)
            g_refs[k][...] = gg
            d_refs[k][...], nm_refs[k][...], nv_refs[k][...] = _adamw_update(gg, w_refs[k][...], m_refs[k][...], v_refs[k][...])

    whole = lambda a: pl.BlockSpec(a.shape, lambda i, c: (0,) * a.ndim)
    arrs = list(ws) + list(ms) + list(vs)
    res = _call(
        body, name=name,
        grid_spec=pltpu.PrefetchScalarGridSpec(
            num_scalar_prefetch=1, grid=(1,),
            in_specs=[whole(gvec8), pl.BlockSpec((N_DEV, NMETA, D // N_CHIPS), lambda i, c: (0, 0, c[0])), whole(lp8)]
                     + [whole(a) for a in arrs],
            out_specs=[pl.BlockSpec((1, D), lambda i, c: (0, 0))] + [whole(a) for a in ws] * 4),
        out_shape=[jax.ShapeDtypeStruct((1, D), F32)] + [jax.ShapeDtypeStruct(a.shape, F32) for a in ws] * 4,
        compiler_params=_cp(),
    )(chip, gvec8, gmeta8, lp8, *arrs)
    return res[0], [list(res[1 + i * na:1 + (i + 1) * na]) for i in range(4)]


def _adamw_halves(wt, mine, theirs, m, v, core, name):
    r, c = wt.shape
    h = r // 2
    tr = _tile(h, (256, 176, 128, 64))
    nh = h // tr

    def body(c_ref, w_ref, a_ref, b_ref, m_ref, v_ref, g_ref, d_ref, nm_ref, nv_ref):
        gg = jnp.where(pl.program_id(0) // nh == c_ref[0], a_ref[...], b_ref[...])
        g_ref[...] = gg
        d_ref[...], nm_ref[...], nv_ref[...] = _adamw_update(gg, w_ref[...], m_ref[...], v_ref[...])

    full = pl.BlockSpec((tr, c), lambda i, cr: (i, 0))
    half = pl.BlockSpec((tr, c), lambda i, cr: (i % nh, 0))
    return _call(
        body, name=name,
        grid_spec=pltpu.PrefetchScalarGridSpec(
            num_scalar_prefetch=1, grid=(2 * nh,),
            in_specs=[full, half, half, full, full], out_specs=[full] * 4),
        out_shape=[jax.ShapeDtypeStruct((r, c), F32)] * 4,
        compiler_params=_cp(),
    )(core, wt, mine, theirs, m, v)


MATS = ["ffn1_w_gu", "ffn1_w_down", "w_in", "mla_w_uq", "mla_w_ukv", "w_branch_fox", "w_branch_mla",
        "w_out", "ffn2_w_gu", "ffn2_w_down"]
VECS = ["ffn1_norm", "mix_norm", "b_forget", "b_gate", "fox_q_norm", "fox_k_norm", "mla_cq_norm",
        "mla_ckv_norm", "mla_q_norm", "mla_k_norm", "ffn2_norm"]
WEIGHTS = ["meta_tokens", "ffn1_norm", "ffn1_w_gu", "ffn1_w_down", "mix_norm", "w_in", "b_forget", "b_gate",
           "fox_q_norm", "fox_k_norm", "mla_cq_norm", "mla_w_uq", "mla_ckv_norm", "mla_w_ukv", "mla_q_norm",
           "mla_k_norm", "w_branch_fox", "w_branch_mla", "w_out", "ffn2_norm", "ffn2_w_gu", "ffn2_w_down"]


VEC_LANES = 2048


def _stack_vectors(parts):
    rows = [_pad_lanes(p, 0, VEC_LANES) for p in parts]
    rows.append(jnp.zeros((-len(parts) % 8, VEC_LANES), F32))
    return jnp.concatenate(rows, axis=0)


def kernel(x, meta_tokens, ffn1_norm, ffn1_w_gu, ffn1_w_down, mix_norm, w_in, b_forget, b_gate, fox_q_norm, fox_k_norm, mla_cq_norm, mla_w_uq, mla_ckv_norm, mla_w_ukv, mla_q_norm, mla_k_norm, w_branch_fox, w_branch_mla, w_out, ffn2_norm, ffn2_w_gu, ffn2_w_down, loss_target, m_meta_tokens, m_ffn1_norm, m_ffn1_w_gu, m_ffn1_w_down, m_mix_norm, m_w_in, m_b_forget, m_b_gate, m_fox_q_norm, m_fox_k_norm, m_mla_cq_norm, m_mla_w_uq, m_mla_ckv_norm, m_mla_w_ukv, m_mla_q_norm, m_mla_k_norm, m_w_branch_fox, m_w_branch_mla, m_w_out, m_ffn2_norm, m_ffn2_w_gu, m_ffn2_w_down, v_meta_tokens, v_ffn1_norm, v_ffn1_w_gu, v_ffn1_w_down, v_mix_norm, v_w_in, v_b_forget, v_b_gate, v_fox_q_norm, v_fox_k_norm, v_mla_cq_norm, v_mla_w_uq, v_mla_ckv_norm, v_mla_w_ukv, v_mla_q_norm, v_mla_k_norm, v_w_branch_fox, v_w_branch_mla, v_w_out, v_ffn2_norm, v_ffn2_w_gu, v_ffn2_w_down):
    a = dict(locals())
    wts = {n: a[n] for n in WEIGHTS}
    ms = {n: a["m_" + n] for n in WEIGHTS}
    vs = {n: a["v_" + n] for n in WEIGHTS}
    cx, cy, cc = lax.axis_index("x"), lax.axis_index("y"), lax.axis_index("c")
    chip = 2 * cx + cy

    shards = {n: wts[n][0].astype(BF16) for n in MATS}
    first = _run_tasks([_SplitGather([shards[n] for n in G_FFN1] + [meta_tokens], wide=(0,))], "gather_ffn1")[0]
    gath = dict(zip(G_FFN1, first[:-1]))
    meta_full = _cols_from_shards(first[-1])

    _, grad_x, _, _, gred, (gvec8, gmeta8, lp8) = _step(x, loss_target, meta_full, {n: wts[n] for n in VECS}, gath, shards)

    sm_names = VECS + ["meta_tokens"]
    lsum, sm = _adamw_small(gvec8, gmeta8, lp8, chip.reshape(1).astype(jnp.int32), [wts[n] for n in sm_names],
                            [ms[n] for n in sm_names], [vs[n] for n in sm_names], "adamw_small")
    loss = jnp.sum(lsum)

    grads, delta, new_m, new_v = {}, {}, {}, {}
    core = cc.reshape(1).astype(jnp.int32)
    for n in MATS:
        shp = wts[n].shape
        mine, theirs = gred[n]
        res = _adamw_halves(wts[n][0], mine, theirs, ms[n][0], vs[n][0], core, "adamw_" + n)
        grads[n], delta[n], new_m[n], new_v[n] = (t.reshape(shp) for t in res)
    for k, n in enumerate(sm_names):
        grads[n], delta[n], new_m[n], new_v[n] = (sm[i][k] for i in range(4))

    return (loss, grad_x, *[grads[n] for n in WEIGHTS], *[delta[n] for n in WEIGHTS],
            *[new_m[n] for n in WEIGHTS], *[new_v[n] for n in WEIGHTS])
```

```python
import functools

import jax
import jax.numpy as jnp
from jax import lax
from jax.experimental import pallas as pl
from jax.experimental.pallas import tpu as pltpu

F32 = jnp.float32
BF16 = jnp.bfloat16
MESH = pl.DeviceIdType.MESH

D = 1024
DFF = 2816
FH = DFF // 2
NMETA = 16
MPAD = 128
EPS = 1e-6
NH = 8
FOXW = 512
QR = 256
KVR = 128
ROPE = 32
MLA_QK = 96
PROJW = 4096
ROPE_THETA = 10000.0
N_CHIPS = 4
N_DEV = 8

ADAM_LR = 0.001
ADAM_B1 = 0.9
ADAM_B2 = 0.999
ADAM_EPS = 1e-08
ADAM_WD = 0.01
ADAM_STEP = 10

VMEM_LIMIT = 56 * 2**20
ATTN_TQ = 256
GRAD_DTYPE = BF16

NT = (((1,), (1,)), ((), ()))
TN = (((0,), (0,)), ((), ()))


def _call(body, **kw):
    return pl.pallas_call(body, **kw)


def _cp(**kw):
    return pltpu.CompilerParams(vmem_limit_bytes=VMEM_LIMIT, **kw)


HBM = pl.BlockSpec(memory_space=pltpu.HBM)


class _Task:
    def __init__(self, ins, out_shapes, sems, descs):
        self.ins, self.out_shapes, self.sems, self.descs = list(ins), list(out_shapes), list(sems), descs

    def start(self, ins, outs, sems):
        locs, pairs = self.descs(ins, outs, sems)
        for lc in locs:
            lc().start()
        for snd, _ in pairs:
            snd().start()

    def wait(self, ins, outs, sems):
        locs, pairs = self.descs(ins, outs, sems)
        for _, rcv in pairs:
            rcv().wait_recv()
        for snd, _ in pairs:
            snd().wait_send()
        for lc in locs:
            lc().wait()


def _call_tasks(body, tasks, *, name, grid, in_specs, out_specs, out_shape, args, scratch_shapes=()):
    in_specs, out_specs, out_shape, scratch_shapes = map(list, (in_specs, out_specs, out_shape, scratch_shapes))
    n_in, n_out, n_sc = len(in_specs), len(out_specs), len(scratch_shapes)
    t_in = [len(t.ins) for t in tasks]
    t_out = [len(t.out_shapes) for t in tasks]
    t_sem = [len(t.sems) for t in tasks]

    def wrapped(*refs):
        pos = [0]

        def take(n):
            pos[0] += n
            return refs[pos[0] - n:pos[0]]

        ins, tins = take(n_in), [take(n) for n in t_in]
        outs, touts = take(n_out), [take(n) for n in t_out]
        sc, tsems = take(n_sc), [take(n) for n in t_sem]
        if tasks:
            first = functools.reduce(jnp.logical_and, [pl.program_id(a) == 0 for a in range(len(grid))])
            last = functools.reduce(jnp.logical_and, [pl.program_id(a) == grid[a] - 1 for a in range(len(grid))])

            @pl.when(first)
            def _():
                for t, a, b, s in zip(tasks, tins, touts, tsems):
                    t.start(a, b, s)

        body(*ins, *outs, *sc)
        if tasks:
            @pl.when(last)
            def _():
                for t, a, b, s in zip(tasks, tins, touts, tsems):
                    t.wait(a, b, s)

    res = _call(
        wrapped, name=name, grid=grid,
        in_specs=in_specs + [HBM] * sum(t_in), out_specs=out_specs + [HBM] * sum(t_out),
        out_shape=out_shape + [s for t in tasks for s in t.out_shapes],
        scratch_shapes=scratch_shapes + [s for t in tasks for s in t.sems],
        compiler_params=_cp(),
    )(*args, *[a for t in tasks for a in t.ins])
    res = list(res)
    touts, pos = [], n_out
    for n in t_out:
        touts.append(res[pos:pos + n])
        pos += n
    return res[:n_out], touts


def _run_tasks(tasks, name):
    t_in = [len(t.ins) for t in tasks]
    t_out = [len(t.out_shapes) for t in tasks]
    t_sem = [len(t.sems) for t in tasks]

    def body(*refs):
        pos = [0]

        def take(n):
            pos[0] += n
            return refs[pos[0] - n:pos[0]]

        tins, touts, tsems = [take(n) for n in t_in], [take(n) for n in t_out], [take(n) for n in t_sem]
        for t, a, b, s in zip(tasks, tins, touts, tsems):
            t.start(a, b, s)
        for t, a, b, s in zip(tasks, tins, touts, tsems):
            t.wait(a, b, s)

    res = list(_call(
        body, name=name, in_specs=[HBM] * sum(t_in), out_specs=[HBM] * sum(t_out),
        out_shape=[s for t in tasks for s in t.out_shapes],
        scratch_shapes=[s for t in tasks for s in t.sems],
    )(*[a for t in tasks for a in t.ins]))
    touts, pos = [], 0
    for n in t_out:
        touts.append(res[pos:pos + n])
        pos += n
    return touts


def _tile(n, cands):
    for c in cands:
        if n % c == 0:
            return c
    raise ValueError(f"no tile for {n} among {cands}")


def _dot(a, b, dims=None):
    if dims is None:
        return jnp.dot(a, b, preferred_element_type=F32)
    return lax.dot_general(a, b, dims, preferred_element_type=F32)


def _lane(shape):
    return lax.broadcasted_iota(jnp.int32, shape, len(shape) - 1)


def _seg_ones(w, log2_seg):
    r = lax.shift_right_logical(lax.broadcasted_iota(jnp.int32, (w, w), 0), log2_seg)
    c = lax.shift_right_logical(lax.broadcasted_iota(jnp.int32, (w, w), 1), log2_seg)
    return jnp.where(r == c, 1.0, 0.0).astype(BF16)


def _seg_sum(x, ones):
    hi = x.astype(BF16)
    r1 = x - hi.astype(F32)
    mid = r1.astype(BF16)
    lo = (r1 - mid.astype(F32)).astype(BF16)
    return _dot(hi, ones) + _dot(mid, ones) + _dot(lo, ones)


def _lane_sum(x, ones):
    return jnp.sum(x, axis=-1, keepdims=True) if ones is None else _seg_sum(x, ones)


def _rms(x, gain, n, ones=None):
    r = lax.rsqrt(_lane_sum(x * x, ones) * (1.0 / n) + EPS)
    xh = x * r
    return xh * gain, xh, r


def _rms_bwd(dy, xh, r, gain, n, ones=None):
    dxh = dy * gain
    return r * (dxh - xh * (_lane_sum(dxh * xh, ones) * (1.0 / n)))


def _rope_swap(x):
    ln = _lane(x.shape)
    sw = jnp.where(ln < 80, pltpu.roll(x, 112, 1), pltpu.roll(x, 16, 1))
    return jnp.where(jnp.logical_and(ln >= 64, ln < 96), sw, 0.0)


def _colsum(x):
    return jnp.sum(x, axis=0, keepdims=True)


def _ffn_weight_specs():
    once = pl.Buffered(1)
    return [pl.BlockSpec((D, DFF), lambda i: (0, 0), pipeline_mode=once),
            pl.BlockSpec((D, DFF), lambda i: (0, 1), pipeline_mode=once),
            pl.BlockSpec((DFF, D), lambda i: (0, 0), pipeline_mode=once)]


def _ffn_fwd(h, norm, wgu, wd, name, tasks=()):
    T = h.shape[0]
    tm = _tile(T, (512, 384, 256, 128))

    def body(h_ref, n_ref, wg_ref, wu_ref, wd_ref, o_ref):
        x = h_ref[...]
        u, _, _ = _rms(x, n_ref[...], D)
        ub = u.astype(BF16)
        g = _dot(ub, wg_ref[...])
        p = _dot(ub, wu_ref[...])
        a = (g * jax.nn.sigmoid(g)) * p
        o_ref[...] = x + 0.5 * _dot(a.astype(BF16), wd_ref[...])

    (out,), touts = _call_tasks(
        body, tasks, name=name, grid=(T // tm,),
        in_specs=[pl.BlockSpec((tm, D), lambda i: (i, 0)), pl.BlockSpec((1, D), lambda i: (0, 0))] + _ffn_weight_specs(),
        out_specs=[pl.BlockSpec((tm, D), lambda i: (i, 0))],
        out_shape=[jax.ShapeDtypeStruct((T, D), F32)],
        args=(h, norm, wgu, wgu, wd))
    return out, touts


def _ffn_bwd(h, dout, norm, wgu, wd, name, tasks=()):
    T = h.shape[0]
    tm = _tile(T, (256, 128))

    def body(h_ref, d_ref, n_ref, wg_ref, wu_ref, wd_ref, dh_ref, u_ref, a_ref, dgp_ref, dn_ref):
        @pl.when(pl.program_id(0) == 0)
        def _():
            dn_ref[...] = jnp.zeros_like(dn_ref)

        u, xh, r = _rms(h_ref[...], n_ref[...], D)
        ub = u.astype(BF16)
        u_ref[...] = ub
        g = _dot(ub, wg_ref[...])
        p = _dot(ub, wu_ref[...])
        s = jax.nn.sigmoid(g)
        sl = g * s
        dz = (0.5 * d_ref[...]).astype(BF16)
        da = _dot(dz, wd_ref[...], NT)
        dp = da * sl
        dg = (da * p) * (s * (1.0 + g * (1.0 - s)))
        a_ref[...] = (sl * p).astype(BF16)
        dgb = dg.astype(BF16)
        dpb = dp.astype(BF16)
        dgp_ref[:, :DFF] = dgb
        dgp_ref[:, DFF:] = dpb
        du = _dot(dgb, wg_ref[...], NT) + _dot(dpb, wu_ref[...], NT)
        dn_ref[...] += _colsum(du * xh)
        dh_ref[...] = d_ref[...] + _rms_bwd(du, xh, r, n_ref[...], D)

    row = lambda w: pl.BlockSpec((tm, w), lambda i: (i, 0))
    return _call_tasks(
        body, tasks, name=name, grid=(T // tm,),
        in_specs=[row(D), row(D), pl.BlockSpec((1, D), lambda i: (0, 0))] + _ffn_weight_specs(),
        out_specs=[row(D), row(D), row(DFF), row(2 * DFF), pl.BlockSpec((1, D), lambda i: (0, 0))],
        out_shape=[jax.ShapeDtypeStruct((T, D), F32),
                   jax.ShapeDtypeStruct((T, D), BF16),
                   jax.ShapeDtypeStruct((T, DFF), BF16),
                   jax.ShapeDtypeStruct((T, 2 * DFF), BF16),
                   jax.ShapeDtypeStruct((1, D), F32)],
        args=(h, dout, norm, wgu, wgu, wd))


def _wgrad(x, y, name, scale=1.0, bk=None, bn=None, shards=0, bt=2176, tasks=()):
    T, K = x.shape
    N = y.shape[1]
    bk = bk or K
    bn = bn or N
    bt = _tile(T, (bt, 512, 384, 256, 128))
    nt = T // bt

    def body(x_ref, y_ref, o_ref, acc_ref):
        t = pl.program_id(2)

        @pl.when(t == 0)
        def _():
            acc_ref[...] = jnp.zeros_like(acc_ref)

        acc_ref[...] += _dot(x_ref[...].astype(BF16), y_ref[...].astype(BF16), TN)

        @pl.when(t == nt - 1)
        def _():
            res = (acc_ref[...] * scale).astype(o_ref.dtype)
            if shards:
                w = bn // shards
                for s in range(shards):
                    o_ref[s] = res[:, s * w:(s + 1) * w]
            else:
                o_ref[...] = res

    if shards:
        assert bk == K
        out_spec = pl.BlockSpec((shards, K, bn // shards), lambda i, j, t: (j, 0, 0))
        out_shape = jax.ShapeDtypeStruct((N * shards // bn, K, bn // shards), GRAD_DTYPE)
    else:
        out_spec = pl.BlockSpec((bk, bn), lambda i, j, t: (i, j))
        out_shape = jax.ShapeDtypeStruct((K, N), GRAD_DTYPE)
    (out,), touts = _call_tasks(
        body, tasks, name=name, grid=(K // bk, N // bn, nt),
        in_specs=[pl.BlockSpec((bt, bk), lambda i, j, t: (t, i)),
                  pl.BlockSpec((bt, bn), lambda i, j, t: (t, j))],
        out_specs=[out_spec], out_shape=[out_shape],
        scratch_shapes=[pltpu.VMEM((bk, bn), F32)],
        args=(x, y))
    return out, touts


def _inproj_fwd(h, norm, w, name):
    T = h.shape[0]
    tm = _tile(T, (512, 384, 256, 128))

    def body(h_ref, n_ref, w_ref, o_ref, u_ref):
        u, _, _ = _rms(h_ref[...], n_ref[...], D)
        ub = u.astype(BF16)
        u_ref[...] = ub
        o_ref[...] = _dot(ub, w_ref[...])

    return _call(
        body, name=name, grid=(T // tm,),
        in_specs=[pl.BlockSpec((tm, D), lambda i: (i, 0)),
                  pl.BlockSpec((1, D), lambda i: (0, 0)),
                  pl.BlockSpec((D, PROJW), lambda i: (0, 0), pipeline_mode=pl.Buffered(1))],
        out_specs=[pl.BlockSpec((tm, PROJW), lambda i: (i, 0)),
                   pl.BlockSpec((tm, D), lambda i: (i, 0))],
        out_shape=[jax.ShapeDtypeStruct((T, PROJW), F32), jax.ShapeDtypeStruct((T, D), BF16)],
        compiler_params=_cp(),
    )(h, norm, w)


def _inproj_bwd(h, dres, dlo, dhi, norm, w, name):
    T = h.shape[0]
    tm = _tile(T, (512, 384, 256, 128))
    hw = PROJW // 2

    def body(h_ref, d_ref, lo_ref, hi_ref, n_ref, wlo_ref, whi_ref, dh_ref, dn_ref):
        @pl.when(pl.program_id(0) == 0)
        def _():
            dn_ref[...] = jnp.zeros_like(dn_ref)

        _, xh, r = _rms(h_ref[...], n_ref[...], D)
        du = _dot(lo_ref[...], wlo_ref[...], NT) + _dot(hi_ref[...], whi_ref[...], NT)
        dn_ref[...] += _colsum(du * xh)
        dh_ref[...] = d_ref[...] + _rms_bwd(du, xh, r, n_ref[...], D)

    return _call(
        body, name=name, grid=(T // tm,),
        in_specs=[pl.BlockSpec((tm, D), lambda i: (i, 0)),
                  pl.BlockSpec((tm, D), lambda i: (i, 0)),
                  pl.BlockSpec((tm, hw), lambda i: (i, 0)),
                  pl.BlockSpec((tm, hw), lambda i: (i, 0)),
                  pl.BlockSpec((1, D), lambda i: (0, 0)),
                  pl.BlockSpec((D, hw), lambda i: (0, 0)),
                  pl.BlockSpec((D, hw), lambda i: (0, 1))],
        out_specs=[pl.BlockSpec((tm, D), lambda i: (i, 0)),
                   pl.BlockSpec((1, D), lambda i: (0, 0))],
        out_shape=[jax.ShapeDtypeStruct((T, D), F32), jax.ShapeDtypeStruct((1, D), F32)],
        compiler_params=_cp(),
    )(h, dres, dlo, dhi, norm, w, w)


C_FQ, C_FK, C_FV, C_CQ, C_CKV, C_MISC, C_GATE = 0, 512, 1024, 1536, 1792, 1920, 2048
L_KR, L_FL = 64, 96


def _prep_fwd(proj, rc, rs, gfq, gfk, gcq, gckv, gmq, gmk, bfv, wuq, wuk, wuv, name):
    T = proj.shape[0]
    tm = _tile(T, (256, 128))

    def body(p_ref, rc_ref, rs_ref, gfq_ref, gfk_ref, gcq_ref, gckv_ref, gmq_ref, gmk_ref, bf_ref,
             wuq_ref, wuk_ref, wuv_ref, fq_ref, fk_ref, fv_ref, qm_ref, km_ref, vm_ref, lf_ref):
        o64, o128, o256 = _seg_ones(128, 6), _seg_ones(128, 7), _seg_ones(256, 8)
        for blk in range(4):
            for (c0, g_ref, o_ref) in ((C_FQ, gfq_ref, fq_ref), (C_FK, gfk_ref, fk_ref)):
                x = p_ref[:, c0 + 128 * blk:c0 + 128 * (blk + 1)]
                fn, _, _ = _rms(x, g_ref[...], 64, o64)
                o_ref[:, 128 * blk:128 * (blk + 1)] = fn.astype(BF16)
        fv_ref[...] = p_ref[:, C_FV:C_FV + 512].astype(BF16)

        rcv = rc_ref[...]
        rsv = rs_ref[...]
        cqn, _, _ = _rms(p_ref[:, C_CQ:C_CQ + QR], gcq_ref[...], QR, o256)
        qpre = _dot(cqn.astype(BF16), wuq_ref[...])
        ckvn, _, _ = _rms(p_ref[:, C_CKV:C_CKV + KVR], gckv_ref[...], KVR, o128)
        ckvb = ckvn.astype(BF16)
        kpre = _dot(ckvb, wuk_ref[...])
        vm_ref[...] = _dot(ckvb, wuv_ref[...]).astype(BF16)
        misc = p_ref[:, C_MISC:C_MISC + 128]
        ln = _lane(misc.shape)
        kr = jnp.where(jnp.logical_and(ln >= L_KR, ln < L_KR + ROPE), misc, 0.0)
        for hh in range(NH):
            sl = slice(128 * hh, 128 * (hh + 1))
            qn, _, _ = _rms(qpre[:, sl], gmq_ref[...], MLA_QK, o128)
            qm_ref[:, sl] = (qn * rcv + _rope_swap(qn) * rsv).astype(BF16)
            kn, _, _ = _rms(kpre[:, sl] + kr, gmk_ref[...], MLA_QK, o128)
            km_ref[:, sl] = (kn * rcv + _rope_swap(kn) * rsv).astype(BF16)
        z = misc + bf_ref[...]
        lf_ref[...] = jnp.minimum(z, 0.0) - jnp.log(1.0 + jnp.exp(-jnp.abs(z)))

    row = lambda w: pl.BlockSpec((tm, w), lambda i: (i, 0))
    full = lambda a: pl.BlockSpec(a.shape, lambda i: (0, 0))
    return _call(
        body, name=name, grid=(T // tm,),
        in_specs=[row(PROJW // 2), row(128), row(128)] + [full(a) for a in (gfq, gfk, gcq, gckv, gmq, gmk, bfv, wuq, wuk, wuv)],
        out_specs=[row(512), row(512), row(512), row(1024), row(1024), row(512), row(128)],
        out_shape=[jax.ShapeDtypeStruct((T, 512), BF16), jax.ShapeDtypeStruct((T, 512), BF16),
                   jax.ShapeDtypeStruct((T, 512), BF16), jax.ShapeDtypeStruct((T, 1024), BF16),
                   jax.ShapeDtypeStruct((T, 1024), BF16), jax.ShapeDtypeStruct((T, 512), BF16),
                   jax.ShapeDtypeStruct((T, 128), F32)],
        compiler_params=_cp(),
    )(proj, rc, rs, gfq, gfk, gcq, gckv, gmq, gmk, bfv, wuq, wuk, wuv)


def _prep_bwd(proj, rc, rs, gfq, gfk, gcq, gckv, gmq, gmk, bfv, wuq, wuk, wuv,
              dfq, dfk, dfv, dqm, dkm, dvm, dlf, name):
    T = proj.shape[0]
    tm = _tile(T, (256, 128))

    def body(p_ref, rc_ref, rs_ref, gfq_ref, gfk_ref, gcq_ref, gckv_ref, gmq_ref, gmk_ref, bf_ref,
             wuq_ref, wuk_ref, wuv_ref, dfq_ref, dfk_ref, dfv_ref, dqm_ref, dkm_ref, dvm_ref, dlf_ref,
             dp_ref, dgfq_ref, dgfk_ref, dgcq_ref, dgckv_ref, dgmq_ref, dgmk_ref, dbf_ref,
             dwuq_ref, dwuk_ref, dwuv_ref, dqpre_sc, dkpre_sc):
        accs = (dgfq_ref, dgfk_ref, dgcq_ref, dgckv_ref, dgmq_ref, dgmk_ref, dbf_ref, dwuq_ref, dwuk_ref, dwuv_ref)

        @pl.when(pl.program_id(0) == 0)
        def _():
            for a in accs:
                a[...] = jnp.zeros_like(a)

        o64, o128, o256 = _seg_ones(128, 6), _seg_ones(128, 7), _seg_ones(256, 8)
        for (c0, g_ref, d_ref, dg_ref) in ((C_FQ, gfq_ref, dfq_ref, dgfq_ref), (C_FK, gfk_ref, dfk_ref, dgfk_ref)):
            dg = jnp.zeros((1, 128), F32)
            for blk in range(4):
                x = p_ref[:, c0 + 128 * blk:c0 + 128 * (blk + 1)]
                _, xh, r = _rms(x, g_ref[...], 64, o64)
                dy = d_ref[:, 128 * blk:128 * (blk + 1)]
                dg = dg + _colsum(dy * xh)
                dp_ref[:, c0 + 128 * blk:c0 + 128 * (blk + 1)] = _rms_bwd(dy, xh, r, g_ref[...], 64, o64).astype(BF16)
            dg_ref[...] += dg
        dp_ref[:, C_FV:C_FV + 512] = dfv_ref[...].astype(BF16)

        rcv = rc_ref[...]
        rsv = rs_ref[...]
        cqn, cqh, cqr = _rms(p_ref[:, C_CQ:C_CQ + QR], gcq_ref[...], QR, o256)
        cqb = cqn.astype(BF16)
        qpre = _dot(cqb, wuq_ref[...])
        dgq = jnp.zeros((1, 128), F32)
        for hh in range(NH):
            sl = slice(128 * hh, 128 * (hh + 1))
            _, xh, r = _rms(qpre[:, sl], gmq_ref[...], MLA_QK, o128)
            dout = dqm_ref[:, sl]
            dqn = dout * rcv + _rope_swap(dout * rsv)
            dgq = dgq + _colsum(dqn * xh)
            dqpre_sc[:, sl] = _rms_bwd(dqn, xh, r, gmq_ref[...], MLA_QK, o128).astype(BF16)
        dgmq_ref[...] += dgq
        dqpre = dqpre_sc[...]
        dwuq_ref[...] += _dot(cqb, dqpre, TN)
        dcqn = _dot(dqpre, wuq_ref[...], NT)
        dgcq_ref[...] += _colsum(dcqn * cqh)
        dp_ref[:, C_CQ:C_CQ + QR] = _rms_bwd(dcqn, cqh, cqr, gcq_ref[...], QR, o256).astype(BF16)

        ckvn, ckvh, ckvr = _rms(p_ref[:, C_CKV:C_CKV + KVR], gckv_ref[...], KVR, o128)
        ckvb = ckvn.astype(BF16)
        kpre = _dot(ckvb, wuk_ref[...])
        misc = p_ref[:, C_MISC:C_MISC + 128]
        ln = _lane(misc.shape)
        is_kr = jnp.logical_and(ln >= L_KR, ln < L_KR + ROPE)
        kr = jnp.where(is_kr, misc, 0.0)
        dgk = jnp.zeros((1, 128), F32)
        dkr = jnp.zeros(misc.shape, F32)
        for hh in range(NH):
            sl = slice(128 * hh, 128 * (hh + 1))
            _, xh, r = _rms(kpre[:, sl] + kr, gmk_ref[...], MLA_QK, o128)
            dout = dkm_ref[:, sl]
            dkn = dout * rcv + _rope_swap(dout * rsv)
            dgk = dgk + _colsum(dkn * xh)
            dkx = _rms_bwd(dkn, xh, r, gmk_ref[...], MLA_QK, o128)
            dkr = dkr + jnp.where(is_kr, dkx, 0.0)
            dkpre_sc[:, sl] = jnp.where(ln < 64, dkx, 0.0).astype(BF16)
        dgmk_ref[...] += dgk
        dkpre = dkpre_sc[...]
        dvmb = dvm_ref[...].astype(BF16)
        dwuk_ref[...] += _dot(ckvb, dkpre, TN)
        dwuv_ref[...] += _dot(ckvb, dvmb, TN)
        dckvn = _dot(dkpre, wuk_ref[...], NT) + _dot(dvmb, wuv_ref[...], NT)
        dgckv_ref[...] += _colsum(dckvn * ckvh)
        dp_ref[:, C_CKV:C_CKV + KVR] = _rms_bwd(dckvn, ckvh, ckvr, gckv_ref[...], KVR, o128).astype(BF16)

        z = misc + bf_ref[...]
        dz = dlf_ref[...] * (1.0 - jax.nn.sigmoid(z))
        dbf_ref[...] += _colsum(dz)
        dp_ref[:, C_MISC:C_MISC + 128] = (dkr + dz).astype(BF16)

    row = lambda w: pl.BlockSpec((tm, w), lambda i: (i, 0))
    full = lambda a: pl.BlockSpec(a.shape, lambda i: (0, 0))
    small = (gfq, gfk, gcq, gckv, gmq, gmk, bfv, wuq, wuk, wuv)
    acc_shapes = [(1, 128), (1, 128), (1, QR), (1, KVR), (1, 128), (1, 128), (1, 128),
                  (QR, 1024), (KVR, 1024), (KVR, 512)]
    return _call(
        body, name=name, grid=(T // tm,),
        in_specs=[row(PROJW // 2), row(128), row(128)] + [full(a) for a in small]
                 + [row(512), row(512), row(512), row(1024), row(1024), row(512), row(128)],
        out_specs=[row(PROJW // 2)] + [pl.BlockSpec(s, lambda i: (0, 0)) for s in acc_shapes],
        out_shape=[jax.ShapeDtypeStruct((T, PROJW // 2), BF16)] + [jax.ShapeDtypeStruct(s, F32) for s in acc_shapes],
        scratch_shapes=[pltpu.VMEM((tm, 1024), BF16), pltpu.VMEM((tm, 1024), BF16)],
        compiler_params=_cp(),
    )(proj, rc, rs, *small, dfq, dfk, dfv, dqm, dkm, dvm, dlf)


def _scan_lanes(x, reverse):
    n = x.shape[-1]
    ln = _lane(x.shape)
    k = 1
    while k < n:
        if reverse:
            x = x + jnp.where(ln < n - k, pltpu.roll(x, n - k, x.ndim - 1), 0.0)
        else:
            x = x + jnp.where(ln >= k, pltpu.roll(x, k, x.ndim - 1), 0.0)
        k *= 2
    return x


def _forget_scan(lf, reverse, name):
    def body(x_ref, o_ref):
        x = x_ref[...]
        ln = _lane(x.shape)
        pad = jnp.logical_and(ln >= NMETA, ln < MPAD)
        o_ref[...] = jnp.where(pad, 0.0, _scan_lanes(jnp.where(pad, 0.0, x), reverse))

    return _call(body, name=name, out_shape=jax.ShapeDtypeStruct(lf.shape, F32), compiler_params=_cp())(lf)


def _attn_blocks(LP, tq):
    return [(0, MPAD, MPAD)] + [(MPAD + i * tq, tq, MPAD + (i + 1) * tq) for i in range((LP - MPAD) // tq)]


def _attn_scores(q_ref, k_ref, e, r0, rn, kend, wide, scale, bias):
    if wide:
        qe = q_ref[r0:r0 + rn, 128 * e:128 * (e + 1)]
        ke = k_ref[0:kend, 128 * e:128 * (e + 1)]
    else:
        qb = q_ref[r0:r0 + rn, :]
        mine = (_lane(qb.shape) < 64) if e == 0 else (_lane(qb.shape) >= 64)
        qe = jnp.where(mine, qb, jnp.zeros_like(qb))
        ke = k_ref[0:kend, :]
    s = _dot(qe, ke, NT) * scale
    if bias is not None:
        ct_ref, cr_ref = bias
        s = s + ct_ref[0, r0:r0 + rn, e:e + 1] - cr_ref[0, :, 0:kend]
    neg = -1e30
    if r0 == 0:
        qi = lax.broadcasted_iota(jnp.int32, (rn, kend), 0)
        ki = lax.broadcasted_iota(jnp.int32, (rn, kend), 1)
        s = jnp.where(jnp.logical_and(ki <= qi, ki < NMETA), s, neg)
    else:
        d0 = kend - rn
        head = jnp.where(_lane((rn, MPAD)) < NMETA, s[:, :MPAD], neg)
        qi = lax.broadcasted_iota(jnp.int32, (rn, rn), 0)
        diag = jnp.where(_lane((rn, rn)) <= qi, s[:, d0:], neg)
        s = jnp.concatenate([head] + ([s[:, MPAD:d0]] if d0 > MPAD else []) + [diag], axis=1)
    m = jnp.max(s, axis=-1, keepdims=True)
    p = jnp.exp(s - m)
    l = jnp.sum(p, axis=-1, keepdims=True)
    return qe, ke, p, l


def _attn_specs(B, LP, wide, has_bias):
    qw = 256 if wide else 128
    specs = [pl.BlockSpec((LP, qw), lambda b, hp: (b, hp)),
             pl.BlockSpec((LP, qw), lambda b, hp: (b, hp)),
             pl.BlockSpec((LP, 128), lambda b, hp: (b, hp))]
    bias_specs = []
    if has_bias:
        bias_specs = [pl.BlockSpec((1, LP, 2), lambda b, hp: (b * 4 + hp, 0, 0)),
                      pl.BlockSpec((1, 1, LP), lambda b, hp: (b * 8 + 2 * hp, 0, 0)),
                      pl.BlockSpec((1, 1, LP), lambda b, hp: (b * 8 + 2 * hp + 1, 0, 0))]
    return qw, specs, bias_specs


def _attn_fwd(q, k, v, bias, B, LP, wide, scale, name, tasks=()):
    T = q.shape[0]
    blocks = _attn_blocks(LP, ATTN_TQ)
    qw, specs, bias_specs = _attn_specs(B, LP, wide, bias is not None)

    def body(*refs):
        if bias is not None:
            q_ref, k_ref, v_ref, ct_ref, cr0_ref, cr1_ref, o_ref = refs
            crs = (cr0_ref, cr1_ref)
        else:
            q_ref, k_ref, v_ref, o_ref = refs
        for (r0, rn, kend) in blocks:
            outs = []
            for e in (0, 1):
                bs = (ct_ref, crs[e]) if bias is not None else None
                _, _, p, l = _attn_scores(q_ref, k_ref, e, r0, rn, kend, wide, scale, bs)
                outs.append(_dot(p.astype(BF16), v_ref[0:kend, :]) / l)
            o = jnp.where(_lane(outs[0].shape) < 64, outs[0], outs[1])
            o_ref[r0:r0 + rn, :] = o.astype(BF16)

    args = (q, k, v) + ((bias[0], bias[1], bias[1]) if bias is not None else ())
    (out,), touts = _call_tasks(
        body, tasks, name=name, grid=(B, 4),
        in_specs=specs + bias_specs,
        out_specs=[pl.BlockSpec((LP, 128), lambda b, hp: (b, hp))],
        out_shape=[jax.ShapeDtypeStruct((T, 512), BF16)],
        args=args)
    return out, touts


def _attn_bwd(q, k, v, do, bias, B, LP, wide, scale, name, tasks=()):
    T = q.shape[0]
    blocks = _attn_blocks(LP, ATTN_TQ)
    qw, specs, bias_specs = _attn_specs(B, LP, wide, bias is not None)
    has_bias = bias is not None

    def body(*refs):
        if has_bias:
            (q_ref, k_ref, v_ref, do_ref, ct_ref, cr0_ref, cr1_ref,
             dq_ref, dk_ref, dv_ref, dc0_ref, dc1_ref) = refs
            crs = (cr0_ref, cr1_ref)
            dcs = (dc0_ref, dc1_ref)
            dc0_ref[...] = jnp.zeros_like(dc0_ref)
            dc1_ref[...] = jnp.zeros_like(dc1_ref)
        else:
            q_ref, k_ref, v_ref, do_ref, dq_ref, dk_ref, dv_ref = refs
        dk_ref[...] = jnp.zeros_like(dk_ref)
        dv_ref[...] = jnp.zeros_like(dv_ref)
        for (r0, rn, kend) in blocks:
            dqs = []
            for e in (0, 1):
                bs = (ct_ref, crs[e]) if has_bias else None
                qe, ke, p, l = _attn_scores(q_ref, k_ref, e, r0, rn, kend, wide, scale, bs)
                pn = p * (1.0 / l)
                dob = do_ref[r0:r0 + rn, :]
                mine = (_lane(dob.shape) < 64) if e == 0 else (_lane(dob.shape) >= 64)
                doe = jnp.where(mine, dob, jnp.zeros_like(dob))
                dp = _dot(doe, v_ref[0:kend, :], NT)
                delta = jnp.sum(pn * dp, axis=-1, keepdims=True)
                ds = pn * (dp - delta)
                dsb = ds.astype(BF16)
                dqe = _dot(dsb, ke) * scale
                dke = _dot(dsb, qe, TN) * scale
                if wide:
                    dq_ref[r0:r0 + rn, 128 * e:128 * (e + 1)] = dqe
                    dk_ref[0:kend, 128 * e:128 * (e + 1)] += dke
                else:
                    dqs.append(dqe)
                    dk_ref[0:kend, :] += dke
                dv_ref[0:kend, :] += _dot(pn.astype(BF16), doe, TN)
                if has_bias:
                    dcs[e][0, :, 0:kend] -= _colsum(ds)
            if not wide:
                dq_ref[r0:r0 + rn, :] = jnp.where(_lane(dqs[0].shape) < 64, dqs[0], dqs[1])

    args = (q, k, v, do) + ((bias[0], bias[1], bias[1]) if has_bias else ())
    out_specs = [pl.BlockSpec((LP, qw), lambda b, hp: (b, hp)),
                 pl.BlockSpec((LP, qw), lambda b, hp: (b, hp)),
                 pl.BlockSpec((LP, 128), lambda b, hp: (b, hp))]
    out_shape = [jax.ShapeDtypeStruct(q.shape, F32), jax.ShapeDtypeStruct(q.shape, F32),
                 jax.ShapeDtypeStruct((T, 512), F32)]
    if has_bias:
        out_specs += [pl.BlockSpec((1, 1, LP), lambda b, hp: (b * 4 + hp, 0, 0))] * 2
        out_shape += [jax.ShapeDtypeStruct((B * 4, 1, LP), F32)] * 2
    return _call_tasks(
        body, tasks, name=name, grid=(B, 4),
        in_specs=specs + [pl.BlockSpec((LP, 128), lambda b, hp: (b, hp))] + bias_specs,
        out_specs=out_specs, out_shape=out_shape, args=args)


def _post_fwd(h, of, om, proj, bg, wbf, wbm, wout, name):
    T = h.shape[0]
    tm = _tile(T, (512, 384, 256, 128))

    def body(h_ref, of_ref, om_ref, gl_ref, bg_ref, wbf_ref, wbm_ref, wo_ref, o_ref, mix_ref):
        gate = jax.nn.sigmoid(gl_ref[...] + bg_ref[...])
        mix = gate[:, :D] * _dot(of_ref[...], wbf_ref[...]) + gate[:, D:] * _dot(om_ref[...], wbm_ref[...])
        mb = mix.astype(BF16)
        mix_ref[...] = mb
        o_ref[...] = h_ref[...] + _dot(mb, wo_ref[...])

    row = lambda w: pl.BlockSpec((tm, w), lambda i: (i, 0))
    full = lambda a: pl.BlockSpec(a.shape, lambda i: (0, 0))
    return _call(
        body, name=name, grid=(T // tm,),
        in_specs=[row(D), row(512), row(512), pl.BlockSpec((tm, 2 * D), lambda i: (i, 1)),
                  full(bg), full(wbf), full(wbm), full(wout)],
        out_specs=[row(D), row(D)],
        out_shape=[jax.ShapeDtypeStruct((T, D), F32), jax.ShapeDtypeStruct((T, D), BF16)],
        compiler_params=_cp(),
    )(h, of, om, proj, bg, wbf, wbm, wout)


def _post_bwd(dh, of, om, proj, bg, wbf, wbm, wout, name):
    T = dh.shape[0]
    tm = _tile(T, (512, 384, 256, 128))

    def body(d_ref, of_ref, om_ref, gl_ref, bg_ref, wbf_ref, wbm_ref, wo_ref,
             dgl_ref, dbf_ref, dbm_ref, dof_ref, dom_ref, dbg_ref):
        @pl.when(pl.program_id(0) == 0)
        def _():
            dbg_ref[...] = jnp.zeros_like(dbg_ref)

        gate = jax.nn.sigmoid(gl_ref[...] + bg_ref[...])
        dmix = _dot(d_ref[...].astype(BF16), wo_ref[...], NT)
        ofx = _dot(of_ref[...], wbf_ref[...])
        omx = _dot(om_ref[...], wbm_ref[...])
        gf = gate[:, :D]
        gm = gate[:, D:]
        dof = (dmix * gf).astype(BF16)
        dom = (dmix * gm).astype(BF16)
        dglf = dmix * ofx * gf * (1.0 - gf)
        dglm = dmix * omx * gm * (1.0 - gm)
        dgl_ref[:, :D] = dglf.astype(BF16)
        dgl_ref[:, D:] = dglm.astype(BF16)
        dbg_ref[:, :D] += _colsum(dglf)
        dbg_ref[:, D:] += _colsum(dglm)
        dbf_ref[...] = dof
        dbm_ref[...] = dom
        dof_ref[...] = _dot(dof, wbf_ref[...], NT).astype(BF16)
        dom_ref[...] = _dot(dom, wbm_ref[...], NT).astype(BF16)

    row = lambda w: pl.BlockSpec((tm, w), lambda i: (i, 0))
    full = lambda a: pl.BlockSpec(a.shape, lambda i: (0, 0))
    return _call(
        body, name=name, grid=(T // tm,),
        in_specs=[row(D), row(512), row(512), pl.BlockSpec((tm, 2 * D), lambda i: (i, 1)),
                  full(bg), full(wbf), full(wbm), full(wout)],
        out_specs=[row(2 * D), row(D), row(D), row(512), row(512), pl.BlockSpec((1, 2 * D), lambda i: (0, 0))],
        out_shape=[jax.ShapeDtypeStruct((T, 2 * D), BF16), jax.ShapeDtypeStruct((T, D), BF16),
                   jax.ShapeDtypeStruct((T, D), BF16), jax.ShapeDtypeStruct((T, 512), BF16),
                   jax.ShapeDtypeStruct((T, 512), BF16), jax.ShapeDtypeStruct((1, 2 * D), F32)],
        compiler_params=_cp(),
    )(dh, of, om, proj, bg, wbf, wbm, wout)


def _loss_head(h3, target, B, LP, name):
    S = LP - MPAD
    nb = LP // 128

    def body(h_ref, t_ref, dy_ref, l_ref):
        b = pl.program_id(0)
        p = pl.program_id(1)

        @pl.when(jnp.logical_and(b == 0, p == 0))
        def _():
            l_ref[...] = jnp.zeros_like(l_ref)

        @pl.when(p == 0)
        def _():
            dy_ref[...] = jnp.zeros_like(dy_ref)

        @pl.when(p > 0)
        def _():
            e = h_ref[...] - t_ref[0]
            dy_ref[...] = e * (1.0 / D)
            l_ref[...] += jnp.sum(e * e, axis=0, keepdims=True) * (0.5 / D)

    return _call(
        body, name=name, grid=(B, nb),
        in_specs=[pl.BlockSpec((128, D), lambda b, p: (b * nb + p, 0)),
                  pl.BlockSpec((1, 128, D), lambda b, p: (b, jnp.maximum(p - 1, 0), 0))],
        out_specs=[pl.BlockSpec((128, D), lambda b, p: (b * nb + p, 0)),
                   pl.BlockSpec((1, D), lambda b, p: (0, 0))],
        out_shape=[jax.ShapeDtypeStruct(h3.shape, F32), jax.ShapeDtypeStruct((1, D), F32)],
        compiler_params=_cp(),
    )(h3, target)


def _rope_tables(B, LP):
    pos = jnp.concatenate([jnp.arange(MPAD, dtype=F32), NMETA + jnp.arange(LP - MPAD, dtype=F32)])
    inv_freq = ROPE_THETA ** (-jnp.arange(0, ROPE, 2, dtype=F32) / ROPE)
    ang = pos[:, None] * inv_freq[None, :]
    cos, sin = jnp.cos(ang), jnp.sin(ang)
    z32 = jnp.zeros((LP, 32), F32)
    rc = jnp.concatenate([jnp.ones((LP, 64), F32), cos, cos, z32], axis=1)
    rs = jnp.concatenate([jnp.zeros((LP, 64), F32), -sin, sin, z32], axis=1)
    return jnp.tile(rc, (B, 1)), jnp.tile(rs, (B, 1))


def _pad_lanes(v, start, width=128):
    n = v.shape[1]
    return jnp.concatenate([jnp.zeros((1, start), F32), v, jnp.zeros((1, width - start - n), F32)], axis=1)


G_FFN1 = ["ffn1_w_gu", "ffn1_w_down"]
G_MIX = ["w_in", "mla_w_uq", "mla_w_ukv", "w_branch_fox", "w_branch_mla", "w_out"]
G_OUT = ["w_out", "w_branch_fox", "w_branch_mla"]
G_IN = ["w_in", "mla_w_uq", "mla_w_ukv"]


def _step(x, target, meta, vec, gath, shards):
    dist = shards is not None
    B, S, _ = x.shape
    LP = MPAD + S
    T = B * LP
    gath = dict(gath)

    def gather(names, wide=()):
        return [_gather_task([shards[n] for n in names], wide)] if dist else []

    def flat_gu(w):
        return w if w.ndim == 2 else _cols_from_shards(w)

    def gathered(names, touts):
        if dist:
            gath.update(zip(names, touts[0]))

    g4, sums, red = {}, {}, {}

    def scatter(names):
        return [_a2a_task([_pieces(g4[n]) for n in names])] if dist else []

    def scattered(names, tout, me):
        for n, r in zip(names, tout):
            sums[n] = _sum_pieces(r, _pieces(g4[n]), me, "rs_sum_" + n)

    def join(names):
        return [_join_task([sums[n] for n in names])] if dist else []

    def joined(names, tout):
        for n, r in zip(names, tout):
            red[n] = (sums[n], r)

    me = None
    if dist:
        me = (4 * lax.axis_index("x") + 2 * lax.axis_index("y") + lax.axis_index("c")).reshape(1).astype(jnp.int32)

    h0 = jnp.concatenate([jnp.broadcast_to(meta[None], (B, NMETA, D)),
                          jnp.zeros((B, MPAD - NMETA, D), F32), x], axis=1).reshape(T, D)
    rc, rs = _rope_tables(B, LP)
    gfq = jnp.tile(vec["fox_q_norm"], (1, 2))
    gfk = jnp.tile(vec["fox_k_norm"], (1, 2))
    gmq = _pad_lanes(vec["mla_q_norm"], 0)
    gmk = _pad_lanes(vec["mla_k_norm"], 0)
    bfv = _pad_lanes(vec["b_forget"], L_FL)

    w1gu, w1d = flat_gu(gath["ffn1_w_gu"]), gath["ffn1_w_down"].reshape(DFF, D)
    h1, touts = _ffn_fwd(h0, vec["ffn1_norm"], w1gu, w1d, "ffn1_fwd", gather(G_MIX))
    gathered(G_MIX, touts)
    wm = _mixer_weights(gath)
    small = (gfq, gfk, vec["mla_cq_norm"], vec["mla_ckv_norm"], gmq, gmk, bfv, wm["wuq"], wm["wuk"], wm["wuv"])
    proj, u2 = _inproj_fwd(h1, vec["mix_norm"], wm["w_in"], "inproj_fwd")
    fq, fk, fv, qm, km, vm, lf = _prep_fwd(proj, rc, rs, *small, name="prep_fwd")
    lf_rows = lf[:, L_FL:L_FL + NH].reshape(B, LP, NH).transpose(0, 2, 1).reshape(B * NH, LP)
    crow = _forget_scan(lf_rows, False, "forget_scan")
    ctok = crow.reshape(B, 4, 2, LP).transpose(0, 1, 3, 2).reshape(B * 4, LP, 2)
    bias = (ctok, crow.reshape(B * NH, 1, LP))
    of, touts = _attn_fwd(fq, fk, fv, bias, B, LP, False, 64 ** -0.5, "fox_fwd", gather(["ffn2_w_gu"], wide=(0,)))
    gathered(["ffn2_w_gu"], touts)
    om, touts = _attn_fwd(qm, km, vm, None, B, LP, True, MLA_QK ** -0.5, "mla_fwd", gather(["ffn2_w_down"]))
    gathered(["ffn2_w_down"], touts)
    h2, mix = _post_fwd(h1, of, om, proj, vec["b_gate"], wm["wbf"], wm["wbm"], wm["w_out"], "post_fwd")
    w2gu, w2d = flat_gu(gath["ffn2_w_gu"]), gath["ffn2_w_down"].reshape(DFF, D)
    h3, _ = _ffn_fwd(h2, vec["ffn2_norm"], w2gu, w2d, "ffn2_fwd")
    dy, lpart = _loss_head(h3, target, B, LP, "loss_head")

    gv = {}
    (dh2, u3, a2, dgp2, gv["ffn2_norm"]), _ = _ffn_bwd(h2, dy, vec["ffn2_norm"], w2gu, w2d, "ffn2_bwd")
    g4["ffn2_w_gu"] = _wgrad(u3, dgp2, "ffn2_dwgu", bn=DFF, shards=2, bt=1088)[0]
    g4["ffn2_w_down"] = _wgrad(a2, dy, "ffn2_dwd", scale=0.5, bk=FH)[0].reshape(N_CHIPS, DFF // N_CHIPS, D)

    dgl, dbf, dbm, dof, dom, gv["b_gate"] = _post_bwd(dh2, of, om, proj, vec["b_gate"], wm["wbf"], wm["wbm"], wm["w_out"], "post_bwd")
    g4["w_out"] = _wgrad(mix, dh2, "dw_out")[0].reshape(N_CHIPS, D // N_CHIPS, D)
    g4["w_branch_fox"] = _cols_to_shards(_wgrad(of, dbf, "dw_bf")[0])
    g4["w_branch_mla"] = _cols_to_shards(_wgrad(om, dbm, "dw_bm")[0])
    G_FFN2 = ["ffn2_w_gu", "ffn2_w_down"]
    (dfq, dfk, dfv, dc0, dc1), touts = _attn_bwd(fq, fk, fv, dof, bias, B, LP, False, 64 ** -0.5, "fox_bwd", scatter(G_FFN2))
    if dist:
        scattered(G_FFN2, touts[0], me)
    (dqm, dkm, dvm), touts = _attn_bwd(qm, km, vm, dom, None, B, LP, True, MLA_QK ** -0.5, "mla_bwd",
                                       scatter(G_OUT) + join(G_FFN2))
    if dist:
        scattered(G_OUT, touts[0], me)
        joined(G_FFN2, touts[1])
    dc = jnp.concatenate([dc0, dc1], axis=1).reshape(B * NH, LP)
    dlf_rows = _forget_scan(dc, True, "forget_scan_bwd")
    dlf = dlf_rows.reshape(B, NH, LP).transpose(0, 2, 1).reshape(T, NH)
    dlf = jnp.concatenate([jnp.zeros((T, L_FL), F32), dlf, jnp.zeros((T, 128 - L_FL - NH), F32)], axis=1)
    (dlo, dgfq, dgfk, gv["mla_cq_norm"], gv["mla_ckv_norm"], dgmq, dgmk, dbfv,
     dwuq, dwuk, dwuv) = _prep_bwd(proj, rc, rs, *small, dfq, dfk, dfv, dqm, dkm, dvm, dlf, name="prep_bwd")
    gv["fox_q_norm"] = dgfq[:, :64] + dgfq[:, 64:]
    gv["fox_k_norm"] = dgfk[:, :64] + dgfk[:, 64:]
    gv["mla_q_norm"] = dgmq[:, :MLA_QK]
    gv["mla_k_norm"] = dgmk[:, :MLA_QK]
    gv["b_forget"] = dbfv[:, L_FL:L_FL + NH]
    dwin = jnp.concatenate([_wgrad(u2, dlo, "dw_in_lo")[0], _wgrad(u2, dgl, "dw_in_hi")[0]], axis=1)
    g4["w_in"] = _cols_to_shards(_win_from_kernel(dwin))
    g4["mla_w_uq"] = _cols_to_shards(
        dwuq.astype(GRAD_DTYPE).reshape(QR, NH, 128)[:, :, :MLA_QK].reshape(QR, NH * MLA_QK))
    dukv = jnp.concatenate([dwuk.reshape(KVR, NH, 128)[:, :, :64], dwuv.reshape(KVR, NH, 64)], axis=2)
    g4["mla_w_ukv"] = _cols_to_shards(dukv.astype(GRAD_DTYPE).reshape(KVR, NH * 128))
    dh1, gv["mix_norm"] = _inproj_bwd(h1, dh2, dlo, dgl, vec["mix_norm"], wm["w_in"], "inproj_bwd")

    (dh0, u1, a1, dgp1, gv["ffn1_norm"]), touts = _ffn_bwd(h0, dh1, vec["ffn1_norm"], w1gu, w1d, "ffn1_bwd",
                                                            scatter(G_IN) + join(G_OUT))
    if dist:
        scattered(G_IN, touts[0], me)
        joined(G_OUT, touts[1])
    dh0 = dh0.reshape(B, LP, D)
    grad_x = dh0[:, MPAD:]
    grad_meta = jnp.sum(dh0[:, :NMETA], axis=0)
    share = [_share_task([_stack_vectors([gv[n] for n in VECS]), grad_meta, lpart])] if dist else []
    g4["ffn1_w_gu"], touts = _wgrad(u1, dgp1, "ffn1_dwgu", bn=DFF, shards=2, bt=1088, tasks=share)
    shared = touts[0] if dist else None
    dwd1, touts = _wgrad(a1, dh1, "ffn1_dwd", scale=0.5, bk=FH, tasks=scatter(["ffn1_w_gu"]) + join(G_IN))
    g4["ffn1_w_down"] = dwd1.reshape(N_CHIPS, DFF // N_CHIPS, D)
    if dist:
        scattered(["ffn1_w_gu"], touts[0], me)
        joined(G_IN, touts[1])
        scattered(["ffn1_w_down"], _run_tasks(scatter(["ffn1_w_down"]), "rs_ffn1_w_down")[0], me)
        joined(G_FFN1, _run_tasks(join(G_FFN1), "rs_join_ffn1")[0])
    return lpart, grad_x, grad_meta, gv, (red if dist else g4), shared


def _cols_from_shards(g4):
    n, r, c = g4.shape
    return g4.transpose(1, 0, 2).reshape(r, n * c)


def _cols_to_shards(full):
    r, c4 = full.shape
    return full.reshape(r, N_CHIPS, c4 // N_CHIPS).transpose(1, 0, 2)


def _win_to_kernel(wfull):
    z = lambda n: jnp.zeros((D, n), wfull.dtype)
    fl, cq, ckv, kr, gate = (wfull[:, 1536:1544], wfull[:, 1544:1800], wfull[:, 1800:1928],
                             wfull[:, 1928:1960], wfull[:, 1960:4008])
    misc = jnp.concatenate([z(L_KR), kr, fl, z(128 - L_FL - NH)], axis=1)
    return jnp.concatenate([wfull[:, :1536], cq, ckv, misc, gate], axis=1)


def _win_from_kernel(gk):
    m = C_MISC
    return jnp.concatenate([gk[:, :1536], gk[:, m + L_FL:m + L_FL + NH], gk[:, C_CQ:C_CQ + QR],
                            gk[:, C_CKV:C_CKV + KVR], gk[:, m + L_KR:m + L_KR + ROPE], gk[:, C_GATE:]], axis=1)


def _pieces(g4):
    n, r, c = g4.shape
    return g4.reshape(2 * n, r // 2, c)


def _mixer_weights(gath):
    w = {}
    w["w_in"] = _win_to_kernel(_cols_from_shards(gath["w_in"]))
    uq = _cols_from_shards(gath["mla_w_uq"]).reshape(QR, NH, MLA_QK)
    w["wuq"] = jnp.pad(uq, ((0, 0), (0, 0), (0, 128 - MLA_QK))).reshape(QR, NH * 128)
    ukv = _cols_from_shards(gath["mla_w_ukv"]).reshape(KVR, NH, 128)
    w["wuk"] = jnp.pad(ukv[:, :, :64], ((0, 0), (0, 0), (0, 64))).reshape(KVR, NH * 128)
    w["wuv"] = ukv[:, :, 64:].reshape(KVR, NH * 64)
    w["wbf"] = _cols_from_shards(gath["w_branch_fox"])
    w["wbm"] = _cols_from_shards(gath["w_branch_mla"])
    w["w_out"] = gath["w_out"].reshape(D, D)
    return w


def _chip_peers(x, y):
    return [(1 - x, y), (x, 1 - y), (1 - x, 1 - y)]


RELS = [(dx, dy, dc) for dx in (0, 1) for dy in (0, 1) for dc in (0, 1)][1:]


def _here():
    return lax.axis_index("x"), lax.axis_index("y"), lax.axis_index("c")


def _flip(a, d):
    return (1 - a) if d else a


def _remote(src, dst, send, recv, i, dev):
    return functools.partial(pltpu.make_async_remote_copy, src_ref=src, dst_ref=dst, send_sem=send.at[i],
                             recv_sem=recv.at[i], device_id=dev, device_id_type=MESH)


def _gathered_shape(s, wide):
    return jax.ShapeDtypeStruct((s.shape[0], N_CHIPS * s.shape[1]) if wide else (N_CHIPS,) + s.shape, s.dtype)


def _slot(ref, j, shard, wide, rows=None):
    if wide:
        lanes = pl.ds(pl.multiple_of(j * shard.shape[1], 128), shard.shape[1])
        return ref.at[slice(None) if rows is None else rows, lanes]
    return ref.at[j] if rows is None else ref.at[j, rows]


def _gather_task(shards, wide=()):
    n = len(shards)

    def descs(ins, outs, sems):
        send, recv, loc = sems
        x, y, c = _here()
        j = 2 * x + y
        locs, pairs = [], []
        for k in range(n):
            at = functools.partial(_slot, outs[k], shard=shards[k], wide=k in wide)
            locs.append(functools.partial(pltpu.make_async_copy, ins[k], at(j), loc.at[k]))
            for r, (px, py) in enumerate(_chip_peers(x, y)):
                dev = (px, py, c)
                pairs.append((_remote(ins[k], at(j), send, recv, 3 * k + r, dev),
                              _remote(ins[k], at(2 * px + py), send, recv, 3 * k + r, dev)))
        return locs, pairs

    return _Task(shards, [_gathered_shape(s, k in wide) for k, s in enumerate(shards)],
                 [pltpu.SemaphoreType.DMA((3 * n,)), pltpu.SemaphoreType.DMA((3 * n,)), pltpu.SemaphoreType.DMA((n,))],
                 descs)


class _SplitGather(_Task):
    PARTS = 2

    def __init__(self, shards, wide=()):
        n = 3 * len(shards) * self.PARTS
        dma = pltpu.SemaphoreType.DMA
        self.wide = wide
        super().__init__(shards, [_gathered_shape(s, k in wide) for k, s in enumerate(shards)],
                         [dma((n,)), dma((n,)), dma((n,)), dma((n,)), dma((len(shards),))], None)

    def _plan(self, ins, outs, sems):
        send, recv, fsend, frecv, loc = sems
        x, y, c = _here()
        j = 2 * x + y
        locs, first, passed = [], [], []
        for k in range(len(ins)):
            h = self.ins[k].shape[0] // 2
            parts = self.PARTS if h % (32 * self.PARTS) == 0 else 1
            hp = h // parts
            at = functools.partial(_slot, outs[k], shard=self.ins[k], wide=k in self.wide)
            locs.append(functools.partial(pltpu.make_async_copy, ins[k], at(j), loc.at[k]))
            for r, (px, py) in enumerate(_chip_peers(x, y)):
                p = 2 * px + py
                for q in range(parts):
                    i = (3 * k + r) * self.PARTS + q
                    mine = pl.ds(pl.multiple_of(c * h + q * hp, 8), hp)
                    theirs = pl.ds(pl.multiple_of((1 - c) * h + q * hp, 8), hp)
                    first.append((_remote(ins[k].at[mine], at(j, rows=mine), send, recv, i, (px, py, c)),
                                  _remote(ins[k].at[mine], at(p, rows=mine), send, recv, i, (px, py, c))))
                    passed.append((_remote(at(p, rows=mine), at(p, rows=mine), fsend, frecv, i, (x, y, 1 - c)),
                                   _remote(at(p, rows=mine), at(p, rows=theirs), fsend, frecv, i, (x, y, 1 - c))))
        return locs, first, passed

    def start(self, ins, outs, sems):
        locs, first, _ = self._plan(ins, outs, sems)
        for lc in locs:
            lc().start()
        for snd, _ in first:
            snd().start()

    def wait(self, ins, outs, sems):
        locs, first, passed = self._plan(ins, outs, sems)
        for (_, landed), (pass_on, _) in zip(first, passed):
            landed().wait_recv()
            pass_on().start()
        for _, rcv in passed:
            rcv().wait_recv()
        for snd, _ in first + passed:
            snd().wait_send()
        for lc in locs:
            lc().wait()


def _a2a_task(ps):
    n = len(ps)
    nr = len(RELS)

    def descs(ins, outs, sems):
        send, recv = sems
        x, y, c = _here()
        me = 4 * x + 2 * y + c
        pairs = []
        for k in range(n):
            for i, (dx, dy, dc) in enumerate(RELS):
                dev = (_flip(x, dx), _flip(y, dy), _flip(c, dc))
                peer = 4 * dev[0] + 2 * dev[1] + dev[2]
                pairs.append((_remote(ins[k].at[peer], outs[k].at[me], send, recv, nr * k + i, dev),
                              _remote(ins[k].at[peer], outs[k].at[peer], send, recv, nr * k + i, dev)))
        return [], pairs

    return _Task(ps, [jax.ShapeDtypeStruct(p.shape, p.dtype) for p in ps],
                 [pltpu.SemaphoreType.DMA((nr * n,)), pltpu.SemaphoreType.DMA((nr * n,))], descs)


def _share_task(vs):
    n = len(vs)
    nr = len(RELS)

    def descs(ins, outs, sems):
        send, recv, loc = sems
        x, y, c = _here()
        me = 4 * x + 2 * y + c
        locs, pairs = [], []
        for k in range(n):
            locs.append(functools.partial(pltpu.make_async_copy, ins[k], outs[k].at[me], loc.at[k]))
            for i, (dx, dy, dc) in enumerate(RELS):
                dev = (_flip(x, dx), _flip(y, dy), _flip(c, dc))
                peer = 4 * dev[0] + 2 * dev[1] + dev[2]
                pairs.append((_remote(ins[k], outs[k].at[me], send, recv, nr * k + i, dev),
                              _remote(ins[k], outs[k].at[peer], send, recv, nr * k + i, dev)))
        return locs, pairs

    return _Task(vs, [jax.ShapeDtypeStruct((N_DEV,) + v.shape, v.dtype) for v in vs],
                 [pltpu.SemaphoreType.DMA((nr * n,)), pltpu.SemaphoreType.DMA((nr * n,)), pltpu.SemaphoreType.DMA((n,))],
                 descs)


def _join_task(ss):
    n = len(ss)

    def descs(ins, outs, sems):
        send, recv = sems
        x, y, c = _here()
        pairs = []
        for k in range(n):
            cp = _remote(ins[k], outs[k], send, recv, k, (x, y, 1 - c))
            pairs.append((cp, cp))
        return [], pairs

    return _Task(ss, [jax.ShapeDtypeStruct(s.shape, s.dtype) for s in ss],
                 [pltpu.SemaphoreType.DMA((n,)), pltpu.SemaphoreType.DMA((n,))], descs)


def _sum_pieces(recv, own, me, name):
    n, h, c = recv.shape
    tr = h

    def body(me_ref, r_ref, o_ref, out_ref):
        s = pl.program_id(1)
        val = jnp.where(s == me_ref[0], o_ref[0], r_ref[0]).astype(F32)

        @pl.when(s == 0)
        def _():
            out_ref[...] = val

        @pl.when(s > 0)
        def _():
            out_ref[...] += val

    def other(s, m):
        return jnp.where(s == m[0], (s + 1) % n, s)

    return _call(
        body, name=name,
        grid_spec=pltpu.PrefetchScalarGridSpec(
            num_scalar_prefetch=1, grid=(h // tr, n),
            in_specs=[pl.BlockSpec((1, tr, c), lambda i, s, m: (other(s, m), i, 0)),
                      pl.BlockSpec((1, tr, c), lambda i, s, m: (m[0], i, 0))],
            out_specs=pl.BlockSpec((tr, c), lambda i, s, m: (i, 0))),
        out_shape=jax.ShapeDtypeStruct((h, c), F32),
        compiler_params=_cp(),
    )(me, recv, own)


def _adamw_update(gg, w, m, v):
    c1 = 1.0 / (1.0 - ADAM_B1 ** ADAM_STEP)
    c2 = 1.0 / (1.0 - ADAM_B2 ** ADAM_STEP)
    nm = ADAM_B1 * m + (1.0 - ADAM_B1) * gg
    nv = ADAM_B2 * v + (1.0 - ADAM_B2) * (gg * gg)
    return -ADAM_LR * ((nm * c1) / (jnp.sqrt(nv * c2) + ADAM_EPS) + ADAM_WD * w), nm, nv


def _adamw_small(gvec8, gmeta8, lp8, chip, ws, ms, vs, name):
    na = len(ws)

    def dev_sum(ref):
        acc = ref[0]
        for s in range(1, N_DEV):
            acc = acc + ref[s]
        return acc

    def body(c_ref, gv_ref, gm_ref, lp_ref, *refs):
        w_refs, m_refs, v_refs = refs[:na], refs[na:2 * na], refs[2 * na:3 * na]
        l_ref = refs[3 * na]
        outs = refs[3 * na + 1:]
        g_refs, d_refs, nm_refs, nv_refs = outs[:na], outs[na:2 * na], outs[2 * na:3 * na], outs[3 * na:]
        l_ref[...] = dev_sum(lp_ref)
        gvec = dev_sum(gv_ref)
        for k in range(na):
            gg = gvec[k:k + 1, 0:ws[k].shape[1]] if k < na - 1 else dev_sum(gm_ref)
            g_refs[k][...] = gg
            d_refs[k][...], nm_refs[k][...], nv_refs[k][...] = _adamw_update(gg, w_refs[k][...], m_refs[k][...], v_refs[k][...])

    whole = lambda a: pl.BlockSpec(a.shape, lambda i, c: (0,) * a.ndim)
    arrs = list(ws) + list(ms) + list(vs)
    res = _call(
        body, name=name,
        grid_spec=pltpu.PrefetchScalarGridSpec(
            num_scalar_prefetch=1, grid=(1,),
            in_specs=[whole(gvec8), pl.BlockSpec((N_DEV, NMETA, D // N_CHIPS), lambda i, c: (0, 0, c[0])), whole(lp8)]
                     + [whole(a) for a in arrs],
            out_specs=[pl.BlockSpec((1, D), lambda i, c: (0, 0))] + [whole(a) for a in ws] * 4),
        out_shape=[jax.ShapeDtypeStruct((1, D), F32)] + [jax.ShapeDtypeStruct(a.shape, F32) for a in ws] * 4,
        compiler_params=_cp(),
    )(chip, gvec8, gmeta8, lp8, *arrs)
    return res[0], [list(res[1 + i * na:1 + (i + 1) * na]) for i in range(4)]


def _adamw_halves(wt, mine, theirs, m, v, core, name):
    r, c = wt.shape
    h = r // 2
    tr = _tile(h, (256, 176, 128, 64))
    nh = h // tr

    def body(c_ref, w_ref, a_ref, b_ref, m_ref, v_ref, g_ref, d_ref, nm_ref, nv_ref):
        gg = jnp.where(pl.program_id(0) // nh == c_ref[0], a_ref[...], b_ref[...])
        g_ref[...] = gg
        d_ref[...], nm_ref[...], nv_ref[...] = _adamw_update(gg, w_ref[...], m_ref[...], v_ref[...])

    full = pl.BlockSpec((tr, c), lambda i, cr: (i, 0))
    half = pl.BlockSpec((tr, c), lambda i, cr: (i % nh, 0))
    return _call(
        body, name=name,
        grid_spec=pltpu.PrefetchScalarGridSpec(
            num_scalar_prefetch=1, grid=(2 * nh,),
            in_specs=[full, half, half, full, full], out_specs=[full] * 4),
        out_shape=[jax.ShapeDtypeStruct((r, c), F32)] * 4,
        compiler_params=_cp(),
    )(core, wt, mine, theirs, m, v)


MATS = ["ffn1_w_gu", "ffn1_w_down", "w_in", "mla_w_uq", "mla_w_ukv", "w_branch_fox", "w_branch_mla",
        "w_out", "ffn2_w_gu", "ffn2_w_down"]
VECS = ["ffn1_norm", "mix_norm", "b_forget", "b_gate", "fox_q_norm", "fox_k_norm", "mla_cq_norm",
        "mla_ckv_norm", "mla_q_norm", "mla_k_norm", "ffn2_norm"]
WEIGHTS = ["meta_tokens", "ffn1_norm", "ffn1_w_gu", "ffn1_w_down", "mix_norm", "w_in", "b_forget", "b_gate",
           "fox_q_norm", "fox_k_norm", "mla_cq_norm", "mla_w_uq", "mla_ckv_norm", "mla_w_ukv", "mla_q_norm",
           "mla_k_norm", "w_branch_fox", "w_branch_mla", "w_out", "ffn2_norm", "ffn2_w_gu", "ffn2_w_down"]


VEC_LANES = 2048


def _stack_vectors(parts):
    rows = [_pad_lanes(p, 0, VEC_LANES) for p in parts]
    rows.append(jnp.zeros((-len(parts) % 8, VEC_LANES), F32))
    return jnp.concatenate(rows, axis=0)


def kernel(x, meta_tokens, ffn1_norm, ffn1_w_gu, ffn1_w_down, mix_norm, w_in, b_forget, b_gate, fox_q_norm, fox_k_norm, mla_cq_norm, mla_w_uq, mla_ckv_norm, mla_w_ukv, mla_q_norm, mla_k_norm, w_branch_fox, w_branch_mla, w_out, ffn2_norm, ffn2_w_gu, ffn2_w_down, loss_target, m_meta_tokens, m_ffn1_norm, m_ffn1_w_gu, m_ffn1_w_down, m_mix_norm, m_w_in, m_b_forget, m_b_gate, m_fox_q_norm, m_fox_k_norm, m_mla_cq_norm, m_mla_w_uq, m_mla_ckv_norm, m_mla_w_ukv, m_mla_q_norm, m_mla_k_norm, m_w_branch_fox, m_w_branch_mla, m_w_out, m_ffn2_norm, m_ffn2_w_gu, m_ffn2_w_down, v_meta_tokens, v_ffn1_norm, v_ffn1_w_gu, v_ffn1_w_down, v_mix_norm, v_w_in, v_b_forget, v_b_gate, v_fox_q_norm, v_fox_k_norm, v_mla_cq_norm, v_mla_w_uq, v_mla_ckv_norm, v_mla_w_ukv, v_mla_q_norm, v_mla_k_norm, v_w_branch_fox, v_w_branch_mla, v_w_out, v_ffn2_norm, v_ffn2_w_gu, v_ffn2_w_down):
    a = dict(locals())
    wts = {n: a[n] for n in WEIGHTS}
    ms = {n: a["m_" + n] for n in WEIGHTS}
    vs = {n: a["v_" + n] for n in WEIGHTS}
    cx, cy, cc = lax.axis_index("x"), lax.axis_index("y"), lax.axis_index("c")
    chip = 2 * cx + cy

    shards = {n: wts[n][0].astype(BF16) for n in MATS}
    first = _run_tasks([_SplitGather([shards[n] for n in G_FFN1] + [meta_tokens], wide=(0,))], "gather_ffn1")[0]
    gath = dict(zip(G_FFN1, first[:-1]))
    meta_full = _cols_from_shards(first[-1])

    _, grad_x, _, _, gred, (gvec8, gmeta8, lp8) = _step(x, loss_target, meta_full, {n: wts[n] for n in VECS}, gath, shards)

    sm_names = VECS + ["meta_tokens"]
    lsum, sm = _adamw_small(gvec8, gmeta8, lp8, chip.reshape(1).astype(jnp.int32), [wts[n] for n in sm_names],
                            [ms[n] for n in sm_names], [vs[n] for n in sm_names], "adamw_small")
    loss = jnp.sum(lsum)

    grads, delta, new_m, new_v = {}, {}, {}, {}
    core = cc.reshape(1).astype(jnp.int32)
    for n in MATS:
        shp = wts[n].shape
        mine, theirs = gred[n]
        res = _adamw_halves(wts[n][0], mine, theirs, ms[n][0], vs[n][0], core, "adamw_" + n)
        grads[n], delta[n], new_m[n], new_v[n] = (t.reshape(shp) for t in res)
    for k, n in enumerate(sm_names):
        grads[n], delta[n], new_m[n], new_v[n] = (sm[i][k] for i in range(4))

    return (loss, grad_x, *[grads[n] for n in WEIGHTS], *[delta[n] for n in WEIGHTS],
            *[new_m[n] for n in WEIGHTS], *[new_v[n] for n in WEIGHTS])
```

```python
import functools

import jax
import jax.numpy as jnp
from jax import lax
from jax.experimental import pallas as pl
from jax.experimental.pallas import tpu as pltpu

F32 = jnp.float32
BF16 = jnp.bfloat16
MESH = pl.DeviceIdType.MESH

D = 1024
DFF = 2816
FH = DFF // 2
NMETA = 16
MPAD = 128
EPS = 1e-6
NH = 8
FOXW = 512
QR = 256
KVR = 128
ROPE = 32
MLA_QK = 96
PROJW = 4096
ROPE_THETA = 10000.0
N_CHIPS = 4
N_DEV = 8

ADAM_LR = 0.001
ADAM_B1 = 0.9
ADAM_B2 = 0.999
ADAM_EPS = 1e-08
ADAM_WD = 0.01
ADAM_STEP = 10

VMEM_LIMIT = 56 * 2**20
ATTN_TQ = 256
GRAD_DTYPE = BF16

NT = (((1,), (1,)), ((), ()))
TN = (((0,), (0,)), ((), ()))


def _call(body, **kw):
    return pl.pallas_call(body, **kw)


def _cp(**kw):
    return pltpu.CompilerParams(vmem_limit_bytes=VMEM_LIMIT, **kw)


HBM = pl.BlockSpec(memory_space=pltpu.HBM)


class _Task:
    def __init__(self, ins, out_shapes, sems, descs):
        self.ins, self.out_shapes, self.sems, self.descs = list(ins), list(out_shapes), list(sems), descs

    def start(self, ins, outs, sems):
        locs, pairs = self.descs(ins, outs, sems)
        for lc in locs:
            lc().start()
        for snd, _ in pairs:
            snd().start()

    def wait(self, ins, outs, sems):
        locs, pairs = self.descs(ins, outs, sems)
        for _, rcv in pairs:
            rcv().wait_recv()
        for snd, _ in pairs:
            snd().wait_send()
        for lc in locs:
            lc().wait()


def _call_tasks(body, tasks, *, name, grid, in_specs, out_specs, out_shape, args, scratch_shapes=()):
    in_specs, out_specs, out_shape, scratch_shapes = map(list, (in_specs, out_specs, out_shape, scratch_shapes))
    n_in, n_out, n_sc = len(in_specs), len(out_specs), len(scratch_shapes)
    t_in = [len(t.ins) for t in tasks]
    t_out = [len(t.out_shapes) for t in tasks]
    t_sem = [len(t.sems) for t in tasks]

    def wrapped(*refs):
        pos = [0]

        def take(n):
            pos[0] += n
            return refs[pos[0] - n:pos[0]]

        ins, tins = take(n_in), [take(n) for n in t_in]
        outs, touts = take(n_out), [take(n) for n in t_out]
        sc, tsems = take(n_sc), [take(n) for n in t_sem]
        if tasks:
            first = functools.reduce(jnp.logical_and, [pl.program_id(a) == 0 for a in range(len(grid))])
            last = functools.reduce(jnp.logical_and, [pl.program_id(a) == grid[a] - 1 for a in range(len(grid))])

            @pl.when(first)
            def _():
                for t, a, b, s in zip(tasks, tins, touts, tsems):
                    t.start(a, b, s)

        body(*ins, *outs, *sc)
        if tasks:
            @pl.when(last)
            def _():
                for t, a, b, s in zip(tasks, tins, touts, tsems):
                    t.wait(a, b, s)

    res = _call(
        wrapped, name=name, grid=grid,
        in_specs=in_specs + [HBM] * sum(t_in), out_specs=out_specs + [HBM] * sum(t_out),
        out_shape=out_shape + [s for t in tasks for s in t.out_shapes],
        scratch_shapes=scratch_shapes + [s for t in tasks for s in t.sems],
        compiler_params=_cp(),
    )(*args, *[a for t in tasks for a in t.ins])
    res = list(res)
    touts, pos = [], n_out
    for n in t_out:
        touts.append(res[pos:pos + n])
        pos += n
    return res[:n_out], touts


def _run_tasks(tasks, name):
    t_in = [len(t.ins) for t in tasks]
    t_out = [len(t.out_shapes) for t in tasks]
    t_sem = [len(t.sems) for t in tasks]

    def body(*refs):
        pos = [0]

        def take(n):
            pos[0] += n
            return refs[pos[0] - n:pos[0]]

        tins, touts, tsems = [take(n) for n in t_in], [take(n) for n in t_out], [take(n) for n in t_sem]
        for t, a, b, s in zip(tasks, tins, touts, tsems):
            t.start(a, b, s)
        for t, a, b, s in zip(tasks, tins, touts, tsems):
            t.wait(a, b, s)

    res = list(_call(
        body, name=name, in_specs=[HBM] * sum(t_in), out_specs=[HBM] * sum(t_out),
        out_shape=[s for t in tasks for s in t.out_shapes],
        scratch_shapes=[s for t in tasks for s in t.sems],
    )(*[a for t in tasks for a in t.ins]))
    touts, pos = [], 0
    for n in t_out:
        touts.append(res[pos:pos + n])
        pos += n
    return touts


def _tile(n, cands):
    for c in cands:
        if n % c == 0:
            return c
    raise ValueError(f"no tile for {n} among {cands}")


def _dot(a, b, dims=None):
    if dims is None:
        return jnp.dot(a, b, preferred_element_type=F32)
    return lax.dot_general(a, b, dims, preferred_element_type=F32)


def _lane(shape):
    return lax.broadcasted_iota(jnp.int32, shape, len(shape) - 1)


def _seg_ones(w, log2_seg):
    r = lax.shift_right_logical(lax.broadcasted_iota(jnp.int32, (w, w), 0), log2_seg)
    c = lax.shift_right_logical(lax.broadcasted_iota(jnp.int32, (w, w), 1), log2_seg)
    return jnp.where(r == c, 1.0, 0.0).astype(BF16)


def _seg_sum(x, ones):
    hi = x.astype(BF16)
    r1 = x - hi.astype(F32)
    mid = r1.astype(BF16)
    lo = (r1 - mid.astype(F32)).astype(BF16)
    return _dot(hi, ones) + _dot(mid, ones) + _dot(lo, ones)


def _lane_sum(x, ones):
    return jnp.sum(x, axis=-1, keepdims=True) if ones is None else _seg_sum(x, ones)


def _rms(x, gain, n, ones=None):
    r = lax.rsqrt(_lane_sum(x * x, ones) * (1.0 / n) + EPS)
    xh = x * r
    return xh * gain, xh, r


def _rms_bwd(dy, xh, r, gain, n, ones=None):
    dxh = dy * gain
    return r * (dxh - xh * (_lane_sum(dxh * xh, ones) * (1.0 / n)))


def _rope_swap(x):
    ln = _lane(x.shape)
    sw = jnp.where(ln < 80, pltpu.roll(x, 112, 1), pltpu.roll(x, 16, 1))
    return jnp.where(jnp.logical_and(ln >= 64, ln < 96), sw, 0.0)


def _colsum(x):
    return jnp.sum(x, axis=0, keepdims=True)


def _ffn_weight_specs():
    once = pl.Buffered(1)
    return [pl.BlockSpec((D, DFF), lambda i: (0, 0), pipeline_mode=once),
            pl.BlockSpec((D, DFF), lambda i: (0, 1), pipeline_mode=once),
            pl.BlockSpec((DFF, D), lambda i: (0, 0), pipeline_mode=once)]


def _ffn_fwd(h, norm, wgu, wd, name, tasks=()):
    T = h.shape[0]
    tm = _tile(T, (512, 384, 256, 128))

    def body(h_ref, n_ref, wg_ref, wu_ref, wd_ref, o_ref):
        x = h_ref[...]
        u, _, _ = _rms(x, n_ref[...], D)
        ub = u.astype(BF16)
        g = _dot(ub, wg_ref[...])
        p = _dot(ub, wu_ref[...])
        a = (g * jax.nn.sigmoid(g)) * p
        o_ref[...] = x + 0.5 * _dot(a.astype(BF16), wd_ref[...])

    (out,), touts = _call_tasks(
        body, tasks, name=name, grid=(T // tm,),
        in_specs=[pl.BlockSpec((tm, D), lambda i: (i, 0)), pl.BlockSpec((1, D), lambda i: (0, 0))] + _ffn_weight_specs(),
        out_specs=[pl.BlockSpec((tm, D), lambda i: (i, 0))],
        out_shape=[jax.ShapeDtypeStruct((T, D), F32)],
        args=(h, norm, wgu, wgu, wd))
    return out, touts


def _ffn_bwd(h, dout, norm, wgu, wd, name, tasks=()):
    T = h.shape[0]
    tm = _tile(T, (256, 128))

    def body(h_ref, d_ref, n_ref, wg_ref, wu_ref, wd_ref, dh_ref, u_ref, a_ref, dgp_ref, dn_ref):
        @pl.when(pl.program_id(0) == 0)
        def _():
            dn_ref[...] = jnp.zeros_like(dn_ref)

        u, xh, r = _rms(h_ref[...], n_ref[...], D)
        ub = u.astype(BF16)
        u_ref[...] = ub
        g = _dot(ub, wg_ref[...])
        p = _dot(ub, wu_ref[...])
        s = jax.nn.sigmoid(g)
        sl = g * s
        dz = (0.5 * d_ref[...]).astype(BF16)
        da = _dot(dz, wd_ref[...], NT)
        dp = da * sl
        dg = (da * p) * (s * (1.0 + g * (1.0 - s)))
        a_ref[...] = (sl * p).astype(BF16)
        dgb = dg.astype(BF16)
        dpb = dp.astype(BF16)
        dgp_ref[:, :DFF] = dgb
        dgp_ref[:, DFF:] = dpb
        du = _dot(dgb, wg_ref[...], NT) + _dot(dpb, wu_ref[...], NT)
        dn_ref[...] += _colsum(du * xh)
        dh_ref[...] = d_ref[...] + _rms_bwd(du, xh, r, n_ref[...], D)

    row = lambda w: pl.BlockSpec((tm, w), lambda i: (i, 0))
    return _call_tasks(
        body, tasks, name=name, grid=(T // tm,),
        in_specs=[row(D), row(D), pl.BlockSpec((1, D), lambda i: (0, 0))] + _ffn_weight_specs(),
        out_specs=[row(D), row(D), row(DFF), row(2 * DFF), pl.BlockSpec((1, D), lambda i: (0, 0))],
        out_shape=[jax.ShapeDtypeStruct((T, D), F32),
                   jax.ShapeDtypeStruct((T, D), BF16),
                   jax.ShapeDtypeStruct((T, DFF), BF16),
                   jax.ShapeDtypeStruct((T, 2 * DFF), BF16),
                   jax.ShapeDtypeStruct((1, D), F32)],
        args=(h, dout, norm, wgu, wgu, wd))


def _wgrad(x, y, name, scale=1.0, bk=None, bn=None, shards=0, bt=512, tasks=()):
    T, K = x.shape
    N = y.shape[1]
    bk = bk or K
    bn = bn or N
    bt = _tile(T, (bt, 512, 384, 256, 128))
    nt = T // bt

    def body(x_ref, y_ref, o_ref, acc_ref):
        t = pl.program_id(2)

        @pl.when(t == 0)
        def _():
            acc_ref[...] = jnp.zeros_like(acc_ref)

        acc_ref[...] += _dot(x_ref[...].astype(BF16), y_ref[...].astype(BF16), TN)

        @pl.when(t == nt - 1)
        def _():
            res = (acc_ref[...] * scale).astype(o_ref.dtype)
            if shards:
                w = bn // shards
                for s in range(shards):
                    o_ref[s] = res[:, s * w:(s + 1) * w]
            else:
                o_ref[...] = res

    if shards:
        assert bk == K
        out_spec = pl.BlockSpec((shards, K, bn // shards), lambda i, j, t: (j, 0, 0))
        out_shape = jax.ShapeDtypeStruct((N * shards // bn, K, bn // shards), GRAD_DTYPE)
    else:
        out_spec = pl.BlockSpec((bk, bn), lambda i, j, t: (i, j))
        out_shape = jax.ShapeDtypeStruct((K, N), GRAD_DTYPE)
    (out,), touts = _call_tasks(
        body, tasks, name=name, grid=(K // bk, N // bn, nt),
        in_specs=[pl.BlockSpec((bt, bk), lambda i, j, t: (t, i)),
                  pl.BlockSpec((bt, bn), lambda i, j, t: (t, j))],
        out_specs=[out_spec], out_shape=[out_shape],
        scratch_shapes=[pltpu.VMEM((bk, bn), F32)],
        args=(x, y))
    return out, touts


def _inproj_fwd(h, norm, w, name):
    T = h.shape[0]
    tm = _tile(T, (512, 384, 256, 128))

    def body(h_ref, n_ref, w_ref, o_ref, u_ref):
        u, _, _ = _rms(h_ref[...], n_ref[...], D)
        ub = u.astype(BF16)
        u_ref[...] = ub
        o_ref[...] = _dot(ub, w_ref[...])

    return _call(
        body, name=name, grid=(T // tm,),
        in_specs=[pl.BlockSpec((tm, D), lambda i: (i, 0)),
                  pl.BlockSpec((1, D), lambda i: (0, 0)),
                  pl.BlockSpec((D, PROJW), lambda i: (0, 0), pipeline_mode=pl.Buffered(1))],
        out_specs=[pl.BlockSpec((tm, PROJW), lambda i: (i, 0)),
                   pl.BlockSpec((tm, D), lambda i: (i, 0))],
        out_shape=[jax.ShapeDtypeStruct((T, PROJW), F32), jax.ShapeDtypeStruct((T, D), BF16)],
        compiler_params=_cp(),
    )(h, norm, w)


def _inproj_bwd(h, dres, dlo, dhi, norm, w, name):
    T = h.shape[0]
    tm = _tile(T, (512, 384, 256, 128))
    hw = PROJW // 2

    def body(h_ref, d_ref, lo_ref, hi_ref, n_ref, wlo_ref, whi_ref, dh_ref, dn_ref):
        @pl.when(pl.program_id(0) == 0)
        def _():
            dn_ref[...] = jnp.zeros_like(dn_ref)

        _, xh, r = _rms(h_ref[...], n_ref[...], D)
        du = _dot(lo_ref[...], wlo_ref[...], NT) + _dot(hi_ref[...], whi_ref[...], NT)
        dn_ref[...] += _colsum(du * xh)
        dh_ref[...] = d_ref[...] + _rms_bwd(du, xh, r, n_ref[...], D)

    return _call(
        body, name=name, grid=(T // tm,),
        in_specs=[pl.BlockSpec((tm, D), lambda i: (i, 0)),
                  pl.BlockSpec((tm, D), lambda i: (i, 0)),
                  pl.BlockSpec((tm, hw), lambda i: (i, 0)),
                  pl.BlockSpec((tm, hw), lambda i: (i, 0)),
                  pl.BlockSpec((1, D), lambda i: (0, 0)),
                  pl.BlockSpec((D, hw), lambda i: (0, 0)),
                  pl.BlockSpec((D, hw), lambda i: (0, 1))],
        out_specs=[pl.BlockSpec((tm, D), lambda i: (i, 0)),
                   pl.BlockSpec((1, D), lambda i: (0, 0))],
        out_shape=[jax.ShapeDtypeStruct((T, D), F32), jax.ShapeDtypeStruct((1, D), F32)],
        compiler_params=_cp(),
    )(h, dres, dlo, dhi, norm, w, w)


C_FQ, C_FK, C_FV, C_CQ, C_CKV, C_MISC, C_GATE = 0, 512, 1024, 1536, 1792, 1920, 2048
L_KR, L_FL = 64, 96


def _prep_fwd(proj, rc, rs, gfq, gfk, gcq, gckv, gmq, gmk, bfv, wuq, wuk, wuv, name):
    T = proj.shape[0]
    tm = _tile(T, (256, 128))

    def body(p_ref, rc_ref, rs_ref, gfq_ref, gfk_ref, gcq_ref, gckv_ref, gmq_ref, gmk_ref, bf_ref,
             wuq_ref, wuk_ref, wuv_ref, fq_ref, fk_ref, fv_ref, qm_ref, km_ref, vm_ref, lf_ref):
        o64, o128, o256 = _seg_ones(128, 6), _seg_ones(128, 7), _seg_ones(256, 8)
        for blk in range(4):
            for (c0, g_ref, o_ref) in ((C_FQ, gfq_ref, fq_ref), (C_FK, gfk_ref, fk_ref)):
                x = p_ref[:, c0 + 128 * blk:c0 + 128 * (blk + 1)]
                fn, _, _ = _rms(x, g_ref[...], 64, o64)
                o_ref[:, 128 * blk:128 * (blk + 1)] = fn.astype(BF16)
        fv_ref[...] = p_ref[:, C_FV:C_FV + 512].astype(BF16)

        rcv = rc_ref[...]
        rsv = rs_ref[...]
        cqn, _, _ = _rms(p_ref[:, C_CQ:C_CQ + QR], gcq_ref[...], QR, o256)
        qpre = _dot(cqn.astype(BF16), wuq_ref[...])
        ckvn, _, _ = _rms(p_ref[:, C_CKV:C_CKV + KVR], gckv_ref[...], KVR, o128)
        ckvb = ckvn.astype(BF16)
        kpre = _dot(ckvb, wuk_ref[...])
        vm_ref[...] = _dot(ckvb, wuv_ref[...]).astype(BF16)
        misc = p_ref[:, C_MISC:C_MISC + 128]
        ln = _lane(misc.shape)
        kr = jnp.where(jnp.logical_and(ln >= L_KR, ln < L_KR + ROPE), misc, 0.0)
        for hh in range(NH):
            sl = slice(128 * hh, 128 * (hh + 1))
            qn, _, _ = _rms(qpre[:, sl], gmq_ref[...], MLA_QK, o128)
            qm_ref[:, sl] = (qn * rcv + _rope_swap(qn) * rsv).astype(BF16)
            kn, _, _ = _rms(kpre[:, sl] + kr, gmk_ref[...], MLA_QK, o128)
            km_ref[:, sl] = (kn * rcv + _rope_swap(kn) * rsv).astype(BF16)
        z = misc + bf_ref[...]
        lf_ref[...] = jnp.minimum(z, 0.0) - jnp.log(1.0 + jnp.exp(-jnp.abs(z)))

    row = lambda w: pl.BlockSpec((tm, w), lambda i: (i, 0))
    full = lambda a: pl.BlockSpec(a.shape, lambda i: (0, 0))
    return _call(
        body, name=name, grid=(T // tm,),
        in_specs=[row(PROJW // 2), row(128), row(128)] + [full(a) for a in (gfq, gfk, gcq, gckv, gmq, gmk, bfv, wuq, wuk, wuv)],
        out_specs=[row(512), row(512), row(512), row(1024), row(1024), row(512), row(128)],
        out_shape=[jax.ShapeDtypeStruct((T, 512), BF16), jax.ShapeDtypeStruct((T, 512), BF16),
                   jax.ShapeDtypeStruct((T, 512), BF16), jax.ShapeDtypeStruct((T, 1024), BF16),
                   jax.ShapeDtypeStruct((T, 1024), BF16), jax.ShapeDtypeStruct((T, 512), BF16),
                   jax.ShapeDtypeStruct((T, 128), F32)],
        compiler_params=_cp(),
    )(proj, rc, rs, gfq, gfk, gcq, gckv, gmq, gmk, bfv, wuq, wuk, wuv)


def _prep_bwd(proj, rc, rs, gfq, gfk, gcq, gckv, gmq, gmk, bfv, wuq, wuk, wuv,
              dfq, dfk, dfv, dqm, dkm, dvm, dlf, name):
    T = proj.shape[0]
    tm = _tile(T, (256, 128))

    def body(p_ref, rc_ref, rs_ref, gfq_ref, gfk_ref, gcq_ref, gckv_ref, gmq_ref, gmk_ref, bf_ref,
             wuq_ref, wuk_ref, wuv_ref, dfq_ref, dfk_ref, dfv_ref, dqm_ref, dkm_ref, dvm_ref, dlf_ref,
             dp_ref, dgfq_ref, dgfk_ref, dgcq_ref, dgckv_ref, dgmq_ref, dgmk_ref, dbf_ref,
             dwuq_ref, dwuk_ref, dwuv_ref, dqpre_sc, dkpre_sc):
        accs = (dgfq_ref, dgfk_ref, dgcq_ref, dgckv_ref, dgmq_ref, dgmk_ref, dbf_ref, dwuq_ref, dwuk_ref, dwuv_ref)

        @pl.when(pl.program_id(0) == 0)
        def _():
            for a in accs:
                a[...] = jnp.zeros_like(a)

        o64, o128, o256 = _seg_ones(128, 6), _seg_ones(128, 7), _seg_ones(256, 8)
        for (c0, g_ref, d_ref, dg_ref) in ((C_FQ, gfq_ref, dfq_ref, dgfq_ref), (C_FK, gfk_ref, dfk_ref, dgfk_ref)):
            dg = jnp.zeros((1, 128), F32)
            for blk in range(4):
                x = p_ref[:, c0 + 128 * blk:c0 + 128 * (blk + 1)]
                _, xh, r = _rms(x, g_ref[...], 64, o64)
                dy = d_ref[:, 128 * blk:128 * (blk + 1)]
                dg = dg + _colsum(dy * xh)
                dp_ref[:, c0 + 128 * blk:c0 + 128 * (blk + 1)] = _rms_bwd(dy, xh, r, g_ref[...], 64, o64).astype(BF16)
            dg_ref[...] += dg
        dp_ref[:, C_FV:C_FV + 512] = dfv_ref[...].astype(BF16)

        rcv = rc_ref[...]
        rsv = rs_ref[...]
        cqn, cqh, cqr = _rms(p_ref[:, C_CQ:C_CQ + QR], gcq_ref[...], QR, o256)
        cqb = cqn.astype(BF16)
        qpre = _dot(cqb, wuq_ref[...])
        dgq = jnp.zeros((1, 128), F32)
        for hh in range(NH):
            sl = slice(128 * hh, 128 * (hh + 1))
            _, xh, r = _rms(qpre[:, sl], gmq_ref[...], MLA_QK, o128)
            dout = dqm_ref[:, sl]
            dqn = dout * rcv + _rope_swap(dout * rsv)
            dgq = dgq + _colsum(dqn * xh)
            dqpre_sc[:, sl] = _rms_bwd(dqn, xh, r, gmq_ref[...], MLA_QK, o128).astype(BF16)
        dgmq_ref[...] += dgq
        dqpre = dqpre_sc[...]
        dwuq_ref[...] += _dot(cqb, dqpre, TN)
        dcqn = _dot(dqpre, wuq_ref[...], NT)
        dgcq_ref[...] += _colsum(dcqn * cqh)
        dp_ref[:, C_CQ:C_CQ + QR] = _rms_bwd(dcqn, cqh, cqr, gcq_ref[...], QR, o256).astype(BF16)

        ckvn, ckvh, ckvr = _rms(p_ref[:, C_CKV:C_CKV + KVR], gckv_ref[...], KVR, o128)
        ckvb = ckvn.astype(BF16)
        kpre = _dot(ckvb, wuk_ref[...])
        misc = p_ref[:, C_MISC:C_MISC + 128]
        ln = _lane(misc.shape)
        is_kr = jnp.logical_and(ln >= L_KR, ln < L_KR + ROPE)
        kr = jnp.where(is_kr, misc, 0.0)
        dgk = jnp.zeros((1, 128), F32)
        dkr = jnp.zeros(misc.shape, F32)
        for hh in range(NH):
            sl = slice(128 * hh, 128 * (hh + 1))
            _, xh, r = _rms(kpre[:, sl] + kr, gmk_ref[...], MLA_QK, o128)
            dout = dkm_ref[:, sl]
            dkn = dout * rcv + _rope_swap(dout * rsv)
            dgk = dgk + _colsum(dkn * xh)
            dkx = _rms_bwd(dkn, xh, r, gmk_ref[...], MLA_QK, o128)
            dkr = dkr + jnp.where(is_kr, dkx, 0.0)
            dkpre_sc[:, sl] = jnp.where(ln < 64, dkx, 0.0).astype(BF16)
        dgmk_ref[...] += dgk
        dkpre = dkpre_sc[...]
        dvmb = dvm_ref[...].astype(BF16)
        dwuk_ref[...] += _dot(ckvb, dkpre, TN)
        dwuv_ref[...] += _dot(ckvb, dvmb, TN)
        dckvn = _dot(dkpre, wuk_ref[...], NT) + _dot(dvmb, wuv_ref[...], NT)
        dgckv_ref[...] += _colsum(dckvn * ckvh)
        dp_ref[:, C_CKV:C_CKV + KVR] = _rms_bwd(dckvn, ckvh, ckvr, gckv_ref[...], KVR, o128).astype(BF16)

        z = misc + bf_ref[...]
        dz = dlf_ref[...] * (1.0 - jax.nn.sigmoid(z))
        dbf_ref[...] += _colsum(dz)
        dp_ref[:, C_MISC:C_MISC + 128] = (dkr + dz).astype(BF16)

    row = lambda w: pl.BlockSpec((tm, w), lambda i: (i, 0))
    full = lambda a: pl.BlockSpec(a.shape, lambda i: (0, 0))
    small = (gfq, gfk, gcq, gckv, gmq, gmk, bfv, wuq, wuk, wuv)
    acc_shapes = [(1, 128), (1, 128), (1, QR), (1, KVR), (1, 128), (1, 128), (1, 128),
                  (QR, 1024), (KVR, 1024), (KVR, 512)]
    return _call(
        body, name=name, grid=(T // tm,),
        in_specs=[row(PROJW // 2), row(128), row(128)] + [full(a) for a in small]
                 + [row(512), row(512), row(512), row(1024), row(1024), row(512), row(128)],
        out_specs=[row(PROJW // 2)] + [pl.BlockSpec(s, lambda i: (0, 0)) for s in acc_shapes],
        out_shape=[jax.ShapeDtypeStruct((T, PROJW // 2), BF16)] + [jax.ShapeDtypeStruct(s, F32) for s in acc_shapes],
        scratch_shapes=[pltpu.VMEM((tm, 1024), BF16), pltpu.VMEM((tm, 1024), BF16)],
        compiler_params=_cp(),
    )(proj, rc, rs, *small, dfq, dfk, dfv, dqm, dkm, dvm, dlf)


def _scan_lanes(x, reverse):
    n = x.shape[-1]
    ln = _lane(x.shape)
    k = 1
    while k < n:
        if reverse:
            x = x + jnp.where(ln < n - k, pltpu.roll(x, n - k, x.ndim - 1), 0.0)
        else:
            x = x + jnp.where(ln >= k, pltpu.roll(x, k, x.ndim - 1), 0.0)
        k *= 2
    return x


def _forget_scan(lf, reverse, name):
    def body(x_ref, o_ref):
        x = x_ref[...]
        ln = _lane(x.shape)
        pad = jnp.logical_and(ln >= NMETA, ln < MPAD)
        o_ref[...] = jnp.where(pad, 0.0, _scan_lanes(jnp.where(pad, 0.0, x), reverse))

    return _call(body, name=name, out_shape=jax.ShapeDtypeStruct(lf.shape, F32), compiler_params=_cp())(lf)


def _attn_blocks(LP, tq):
    return [(0, MPAD, MPAD)] + [(MPAD + i * tq, tq, MPAD + (i + 1) * tq) for i in range((LP - MPAD) // tq)]


def _attn_scores(q_ref, k_ref, e, r0, rn, kend, wide, scale, bias):
    if wide:
        qe = q_ref[r0:r0 + rn, 128 * e:128 * (e + 1)]
        ke = k_ref[0:kend, 128 * e:128 * (e + 1)]
    else:
        qb = q_ref[r0:r0 + rn, :]
        mine = (_lane(qb.shape) < 64) if e == 0 else (_lane(qb.shape) >= 64)
        qe = jnp.where(mine, qb, jnp.zeros_like(qb))
        ke = k_ref[0:kend, :]
    s = _dot(qe, ke, NT) * scale
    if bias is not None:
        ct_ref, cr_ref = bias
        s = s + ct_ref[0, r0:r0 + rn, e:e + 1] - cr_ref[0, :, 0:kend]
    neg = -1e30
    if r0 == 0:
        qi = lax.broadcasted_iota(jnp.int32, (rn, kend), 0)
        ki = lax.broadcasted_iota(jnp.int32, (rn, kend), 1)
        s = jnp.where(jnp.logical_and(ki <= qi, ki < NMETA), s, neg)
    else:
        d0 = kend - rn
        head = jnp.where(_lane((rn, MPAD)) < NMETA, s[:, :MPAD], neg)
        qi = lax.broadcasted_iota(jnp.int32, (rn, rn), 0)
        diag = jnp.where(_lane((rn, rn)) <= qi, s[:, d0:], neg)
        s = jnp.concatenate([head] + ([s[:, MPAD:d0]] if d0 > MPAD else []) + [diag], axis=1)
    m = jnp.max(s, axis=-1, keepdims=True)
    p = jnp.exp(s - m)
    l = jnp.sum(p, axis=-1, keepdims=True)
    return qe, ke, p, l


def _attn_specs(B, LP, wide, has_bias):
    qw = 256 if wide else 128
    specs = [pl.BlockSpec((LP, qw), lambda b, hp: (b, hp)),
             pl.BlockSpec((LP, qw), lambda b, hp: (b, hp)),
             pl.BlockSpec((LP, 128), lambda b, hp: (b, hp))]
    bias_specs = []
    if has_bias:
        bias_specs = [pl.BlockSpec((1, LP, 2), lambda b, hp: (b * 4 + hp, 0, 0)),
                      pl.BlockSpec((1, 1, LP), lambda b, hp: (b * 8 + 2 * hp, 0, 0)),
                      pl.BlockSpec((1, 1, LP), lambda b, hp: (b * 8 + 2 * hp + 1, 0, 0))]
    return qw, specs, bias_specs


def _attn_fwd(q, k, v, bias, B, LP, wide, scale, name, tasks=()):
    T = q.shape[0]
    blocks = _attn_blocks(LP, ATTN_TQ)
    qw, specs, bias_specs = _attn_specs(B, LP, wide, bias is not None)

    def body(*refs):
        if bias is not None:
            q_ref, k_ref, v_ref, ct_ref, cr0_ref, cr1_ref, o_ref = refs
            crs = (cr0_ref, cr1_ref)
        else:
            q_ref, k_ref, v_ref, o_ref = refs
        for (r0, rn, kend) in blocks:
            outs = []
            for e in (0, 1):
                bs = (ct_ref, crs[e]) if bias is not None else None
                _, _, p, l = _attn_scores(q_ref, k_ref, e, r0, rn, kend, wide, scale, bs)
                outs.append(_dot(p.astype(BF16), v_ref[0:kend, :]) / l)
            o = jnp.where(_lane(outs[0].shape) < 64, outs[0], outs[1])
            o_ref[r0:r0 + rn, :] = o.astype(BF16)

    args = (q, k, v) + ((bias[0], bias[1], bias[1]) if bias is not None else ())
    (out,), touts = _call_tasks(
        body, tasks, name=name, grid=(B, 4),
        in_specs=specs + bias_specs,
        out_specs=[pl.BlockSpec((LP, 128), lambda b, hp: (b, hp))],
        out_shape=[jax.ShapeDtypeStruct((T, 512), BF16)],
        args=args)
    return out, touts


def _attn_bwd(q, k, v, do, bias, B, LP, wide, scale, name, tasks=()):
    T = q.shape[0]
    blocks = _attn_blocks(LP, ATTN_TQ)
    qw, specs, bias_specs = _attn_specs(B, LP, wide, bias is not None)
    has_bias = bias is not None

    def body(*refs):
        if has_bias:
            (q_ref, k_ref, v_ref, do_ref, ct_ref, cr0_ref, cr1_ref,
             dq_ref, dk_ref, dv_ref, dc0_ref, dc1_ref) = refs
            crs = (cr0_ref, cr1_ref)
            dcs = (dc0_ref, dc1_ref)
            dc0_ref[...] = jnp.zeros_like(dc0_ref)
            dc1_ref[...] = jnp.zeros_like(dc1_ref)
        else:
            q_ref, k_ref, v_ref, do_ref, dq_ref, dk_ref, dv_ref = refs
        dk_ref[...] = jnp.zeros_like(dk_ref)
        dv_ref[...] = jnp.zeros_like(dv_ref)
        for (r0, rn, kend) in blocks:
            dqs = []
            for e in (0, 1):
                bs = (ct_ref, crs[e]) if has_bias else None
                qe, ke, p, l = _attn_scores(q_ref, k_ref, e, r0, rn, kend, wide, scale, bs)
                pn = p * (1.0 / l)
                dob = do_ref[r0:r0 + rn, :]
                mine = (_lane(dob.shape) < 64) if e == 0 else (_lane(dob.shape) >= 64)
                doe = jnp.where(mine, dob, jnp.zeros_like(dob))
                dp = _dot(doe, v_ref[0:kend, :], NT)
                delta = jnp.sum(pn * dp, axis=-1, keepdims=True)
                ds = pn * (dp - delta)
                dsb = ds.astype(BF16)
                dqe = _dot(dsb, ke) * scale
                dke = _dot(dsb, qe, TN) * scale
                if wide:
                    dq_ref[r0:r0 + rn, 128 * e:128 * (e + 1)] = dqe
                    dk_ref[0:kend, 128 * e:128 * (e + 1)] += dke
                else:
                    dqs.append(dqe)
                    dk_ref[0:kend, :] += dke
                dv_ref[0:kend, :] += _dot(pn.astype(BF16), doe, TN)
                if has_bias:
                    dcs[e][0, :, 0:kend] -= _colsum(ds)
            if not wide:
                dq_ref[r0:r0 + rn, :] = jnp.where(_lane(dqs[0].shape) < 64, dqs[0], dqs[1])

    args = (q, k, v, do) + ((bias[0], bias[1], bias[1]) if has_bias else ())
    out_specs = [pl.BlockSpec((LP, qw), lambda b, hp: (b, hp)),
                 pl.BlockSpec((LP, qw), lambda b, hp: (b, hp)),
                 pl.BlockSpec((LP, 128), lambda b, hp: (b, hp))]
    out_shape = [jax.ShapeDtypeStruct(q.shape, F32), jax.ShapeDtypeStruct(q.shape, F32),
                 jax.ShapeDtypeStruct((T, 512), F32)]
    if has_bias:
        out_specs += [pl.BlockSpec((1, 1, LP), lambda b, hp: (b * 4 + hp, 0, 0))] * 2
        out_shape += [jax.ShapeDtypeStruct((B * 4, 1, LP), F32)] * 2
    return _call_tasks(
        body, tasks, name=name, grid=(B, 4),
        in_specs=specs + [pl.BlockSpec((LP, 128), lambda b, hp: (b, hp))] + bias_specs,
        out_specs=out_specs, out_shape=out_shape, args=args)


def _post_fwd(h, of, om, proj, bg, wbf, wbm, wout, name):
    T = h.shape[0]
    tm = _tile(T, (512, 384, 256, 128))

    def body(h_ref, of_ref, om_ref, gl_ref, bg_ref, wbf_ref, wbm_ref, wo_ref, o_ref, mix_ref):
        gate = jax.nn.sigmoid(gl_ref[...] + bg_ref[...])
        mix = gate[:, :D] * _dot(of_ref[...], wbf_ref[...]) + gate[:, D:] * _dot(om_ref[...], wbm_ref[...])
        mb = mix.astype(BF16)
        mix_ref[...] = mb
        o_ref[...] = h_ref[...] + _dot(mb, wo_ref[...])

    row = lambda w: pl.BlockSpec((tm, w), lambda i: (i, 0))
    full = lambda a: pl.BlockSpec(a.shape, lambda i: (0, 0))
    return _call(
        body, name=name, grid=(T // tm,),
        in_specs=[row(D), row(512), row(512), pl.BlockSpec((tm, 2 * D), lambda i: (i, 1)),
                  full(bg), full(wbf), full(wbm), full(wout)],
        out_specs=[row(D), row(D)],
        out_shape=[jax.ShapeDtypeStruct((T, D), F32), jax.ShapeDtypeStruct((T, D), BF16)],
        compiler_params=_cp(),
    )(h, of, om, proj, bg, wbf, wbm, wout)


def _post_bwd(dh, of, om, proj, bg, wbf, wbm, wout, name):
    T = dh.shape[0]
    tm = _tile(T, (512, 384, 256, 128))

    def body(d_ref, of_ref, om_ref, gl_ref, bg_ref, wbf_ref, wbm_ref, wo_ref,
             dgl_ref, dbf_ref, dbm_ref, dof_ref, dom_ref, dbg_ref):
        @pl.when(pl.program_id(0) == 0)
        def _():
            dbg_ref[...] = jnp.zeros_like(dbg_ref)

        gate = jax.nn.sigmoid(gl_ref[...] + bg_ref[...])
        dmix = _dot(d_ref[...].astype(BF16), wo_ref[...], NT)
        ofx = _dot(of_ref[...], wbf_ref[...])
        omx = _dot(om_ref[...], wbm_ref[...])
        gf = gate[:, :D]
        gm = gate[:, D:]
        dof = (dmix * gf).astype(BF16)
        dom = (dmix * gm).astype(BF16)
        dglf = dmix * ofx * gf * (1.0 - gf)
        dglm = dmix * omx * gm * (1.0 - gm)
        dgl_ref[:, :D] = dglf.astype(BF16)
        dgl_ref[:, D:] = dglm.astype(BF16)
        dbg_ref[:, :D] += _colsum(dglf)
        dbg_ref[:, D:] += _colsum(dglm)
        dbf_ref[...] = dof
        dbm_ref[...] = dom
        dof_ref[...] = _dot(dof, wbf_ref[...], NT).astype(BF16)
        dom_ref[...] = _dot(dom, wbm_ref[...], NT).astype(BF16)

    row = lambda w: pl.BlockSpec((tm, w), lambda i: (i, 0))
    full = lambda a: pl.BlockSpec(a.shape, lambda i: (0, 0))
    return _call(
        body, name=name, grid=(T // tm,),
        in_specs=[row(D), row(512), row(512), pl.BlockSpec((tm, 2 * D), lambda i: (i, 1)),
                  full(bg), full(wbf), full(wbm), full(wout)],
        out_specs=[row(2 * D), row(D), row(D), row(512), row(512), pl.BlockSpec((1, 2 * D), lambda i: (0, 0))],
        out_shape=[jax.ShapeDtypeStruct((T, 2 * D), BF16), jax.ShapeDtypeStruct((T, D), BF16),
                   jax.ShapeDtypeStruct((T, D), BF16), jax.ShapeDtypeStruct((T, 512), BF16),
                   jax.ShapeDtypeStruct((T, 512), BF16), jax.ShapeDtypeStruct((1, 2 * D), F32)],
        compiler_params=_cp(),
    )(dh, of, om, proj, bg, wbf, wbm, wout)


def _loss_head(h3, target, B, LP, name):
    S = LP - MPAD
    half = LP // 2
    first = half - MPAD

    def body(h_ref, t_ref, dy_ref, l_ref):
        b = pl.program_id(0)
        k = pl.program_id(1)

        @pl.when(jnp.logical_and(b == 0, k == 0))
        def _():
            l_ref[...] = jnp.zeros_like(l_ref)

        @pl.when(k == 0)
        def _():
            e = h_ref[MPAD:, :] - t_ref[0, 0:first, :]
            dy_ref[0:MPAD, :] = jnp.zeros((MPAD, D), F32)
            dy_ref[MPAD:, :] = e * (1.0 / D)
            l_ref[...] += jnp.sum(e * e, axis=0, keepdims=True) * (0.5 / D)

        @pl.when(k == 1)
        def _():
            e = h_ref[...] - t_ref[0, first:S, :]
            dy_ref[...] = e * (1.0 / D)
            l_ref[...] += jnp.sum(e * e, axis=0, keepdims=True) * (0.5 / D)

    return _call(
        body, name=name, grid=(B, 2),
        in_specs=[pl.BlockSpec((half, D), lambda b, k: (b * 2 + k, 0)),
                  pl.BlockSpec((1, S, D), lambda b, k: (b, 0, 0))],
        out_specs=[pl.BlockSpec((half, D), lambda b, k: (b * 2 + k, 0)),
                   pl.BlockSpec((1, D), lambda b, k: (0, 0))],
        out_shape=[jax.ShapeDtypeStruct(h3.shape, F32), jax.ShapeDtypeStruct((1, D), F32)],
        compiler_params=_cp(),
    )(h3, target)


def _rope_tables(B, LP):
    pos = jnp.concatenate([jnp.arange(MPAD, dtype=F32), NMETA + jnp.arange(LP - MPAD, dtype=F32)])
    inv_freq = ROPE_THETA ** (-jnp.arange(0, ROPE, 2, dtype=F32) / ROPE)
    ang = pos[:, None] * inv_freq[None, :]
    cos, sin = jnp.cos(ang), jnp.sin(ang)
    z32 = jnp.zeros((LP, 32), F32)
    rc = jnp.concatenate([jnp.ones((LP, 64), F32), cos, cos, z32], axis=1)
    rs = jnp.concatenate([jnp.zeros((LP, 64), F32), -sin, sin, z32], axis=1)
    return jnp.tile(rc, (B, 1)), jnp.tile(rs, (B, 1))


def _pad_lanes(v, start, width=128):
    n = v.shape[1]
    return jnp.concatenate([jnp.zeros((1, start), F32), v, jnp.zeros((1, width - start - n), F32)], axis=1)


G_FFN1 = ["ffn1_w_gu", "ffn1_w_down"]
G_MIX = ["w_in", "mla_w_uq", "mla_w_ukv", "w_branch_fox", "w_branch_mla", "w_out"]
G_OUT = ["w_out", "w_branch_fox", "w_branch_mla"]
G_IN = ["w_in", "mla_w_uq", "mla_w_ukv"]


def _step(x, target, meta, vec, gath, shards):
    dist = shards is not None
    B, S, _ = x.shape
    LP = MPAD + S
    T = B * LP
    gath = dict(gath)

    def gather(names, wide=()):
        return [_gather_task([shards[n] for n in names], wide)] if dist else []

    def flat_gu(w):
        return w if w.ndim == 2 else _cols_from_shards(w)

    def gathered(names, touts):
        if dist:
            gath.update(zip(names, touts[0]))

    g4, sums, red = {}, {}, {}

    def scatter(names):
        return [_a2a_task([_pieces(g4[n]) for n in names])] if dist else []

    def scattered(names, tout, me):
        for n, r in zip(names, tout):
            sums[n] = _sum_pieces(r, _pieces(g4[n]), me, "rs_sum_" + n)

    def join(names):
        return [_join_task([sums[n] for n in names])] if dist else []

    def joined(names, tout):
        for n, r in zip(names, tout):
            red[n] = (sums[n], r)

    me = None
    if dist:
        me = (4 * lax.axis_index("x") + 2 * lax.axis_index("y") + lax.axis_index("c")).reshape(1).astype(jnp.int32)

    h0 = jnp.concatenate([jnp.broadcast_to(meta[None], (B, NMETA, D)),
                          jnp.zeros((B, MPAD - NMETA, D), F32), x], axis=1).reshape(T, D)
    rc, rs = _rope_tables(B, LP)
    gfq = jnp.tile(vec["fox_q_norm"], (1, 2))
    gfk = jnp.tile(vec["fox_k_norm"], (1, 2))
    gmq = _pad_lanes(vec["mla_q_norm"], 0)
    gmk = _pad_lanes(vec["mla_k_norm"], 0)
    bfv = _pad_lanes(vec["b_forget"], L_FL)

    w1gu, w1d = flat_gu(gath["ffn1_w_gu"]), gath["ffn1_w_down"].reshape(DFF, D)
    h1, touts = _ffn_fwd(h0, vec["ffn1_norm"], w1gu, w1d, "ffn1_fwd", gather(G_MIX))
    gathered(G_MIX, touts)
    wm = _mixer_weights(gath)
    small = (gfq, gfk, vec["mla_cq_norm"], vec["mla_ckv_norm"], gmq, gmk, bfv, wm["wuq"], wm["wuk"], wm["wuv"])
    proj, u2 = _inproj_fwd(h1, vec["mix_norm"], wm["w_in"], "inproj_fwd")
    fq, fk, fv, qm, km, vm, lf = _prep_fwd(proj, rc, rs, *small, name="prep_fwd")
    lf_rows = lf[:, L_FL:L_FL + NH].reshape(B, LP, NH).transpose(0, 2, 1).reshape(B * NH, LP)
    crow = _forget_scan(lf_rows, False, "forget_scan")
    ctok = crow.reshape(B, 4, 2, LP).transpose(0, 1, 3, 2).reshape(B * 4, LP, 2)
    bias = (ctok, crow.reshape(B * NH, 1, LP))
    of, touts = _attn_fwd(fq, fk, fv, bias, B, LP, False, 64 ** -0.5, "fox_fwd", gather(["ffn2_w_gu"], wide=(0,)))
    gathered(["ffn2_w_gu"], touts)
    om, touts = _attn_fwd(qm, km, vm, None, B, LP, True, MLA_QK ** -0.5, "mla_fwd", gather(["ffn2_w_down"]))
    gathered(["ffn2_w_down"], touts)
    h2, mix = _post_fwd(h1, of, om, proj, vec["b_gate"], wm["wbf"], wm["wbm"], wm["w_out"], "post_fwd")
    w2gu, w2d = flat_gu(gath["ffn2_w_gu"]), gath["ffn2_w_down"].reshape(DFF, D)
    h3, _ = _ffn_fwd(h2, vec["ffn2_norm"], w2gu, w2d, "ffn2_fwd")
    dy, lpart = _loss_head(h3, target, B, LP, "loss_head")

    gv = {}
    (dh2, u3, a2, dgp2, gv["ffn2_norm"]), _ = _ffn_bwd(h2, dy, vec["ffn2_norm"], w2gu, w2d, "ffn2_bwd")
    g4["ffn2_w_gu"] = _wgrad(u3, dgp2, "ffn2_dwgu", bn=DFF, shards=2)[0]
    g4["ffn2_w_down"] = _wgrad(a2, dy, "ffn2_dwd", scale=0.5, bk=FH, bt=2176)[0].reshape(N_CHIPS, DFF // N_CHIPS, D)

    dgl, dbf, dbm, dof, dom, gv["b_gate"] = _post_bwd(dh2, of, om, proj, vec["b_gate"], wm["wbf"], wm["wbm"], wm["w_out"], "post_bwd")
    g4["w_out"] = _wgrad(mix, dh2, "dw_out", bt=2176)[0].reshape(N_CHIPS, D // N_CHIPS, D)
    g4["w_branch_fox"] = _cols_to_shards(_wgrad(of, dbf, "dw_bf", bt=2176)[0])
    g4["w_branch_mla"] = _cols_to_shards(_wgrad(om, dbm, "dw_bm", bt=2176)[0])
    G_FFN2 = ["ffn2_w_gu", "ffn2_w_down"]
    (dfq, dfk, dfv, dc0, dc1), touts = _attn_bwd(fq, fk, fv, dof, bias, B, LP, False, 64 ** -0.5, "fox_bwd", scatter(G_FFN2))
    if dist:
        scattered(G_FFN2, touts[0], me)
    (dqm, dkm, dvm), touts = _attn_bwd(qm, km, vm, dom, None, B, LP, True, MLA_QK ** -0.5, "mla_bwd",
                                       scatter(G_OUT) + join(G_FFN2))
    if dist:
        scattered(G_OUT, touts[0], me)
        joined(G_FFN2, touts[1])
    dc = jnp.concatenate([dc0, dc1], axis=1).reshape(B * NH, LP)
    dlf_rows = _forget_scan(dc, True, "forget_scan_bwd")
    dlf = dlf_rows.reshape(B, NH, LP).transpose(0, 2, 1).reshape(T, NH)
    dlf = jnp.concatenate([jnp.zeros((T, L_FL), F32), dlf, jnp.zeros((T, 128 - L_FL - NH), F32)], axis=1)
    (dlo, dgfq, dgfk, gv["mla_cq_norm"], gv["mla_ckv_norm"], dgmq, dgmk, dbfv,
     dwuq, dwuk, dwuv) = _prep_bwd(proj, rc, rs, *small, dfq, dfk, dfv, dqm, dkm, dvm, dlf, name="prep_bwd")
    gv["fox_q_norm"] = dgfq[:, :64] + dgfq[:, 64:]
    gv["fox_k_norm"] = dgfk[:, :64] + dgfk[:, 64:]
    gv["mla_q_norm"] = dgmq[:, :MLA_QK]
    gv["mla_k_norm"] = dgmk[:, :MLA_QK]
    gv["b_forget"] = dbfv[:, L_FL:L_FL + NH]
    dwin = jnp.concatenate([_wgrad(u2, dlo, "dw_in_lo")[0], _wgrad(u2, dgl, "dw_in_hi")[0]], axis=1)
    g4["w_in"] = _cols_to_shards(_win_from_kernel(dwin))
    g4["mla_w_uq"] = _cols_to_shards(
        dwuq.astype(GRAD_DTYPE).reshape(QR, NH, 128)[:, :, :MLA_QK].reshape(QR, NH * MLA_QK))
    dukv = jnp.concatenate([dwuk.reshape(KVR, NH, 128)[:, :, :64], dwuv.reshape(KVR, NH, 64)], axis=2)
    g4["mla_w_ukv"] = _cols_to_shards(dukv.astype(GRAD_DTYPE).reshape(KVR, NH * 128))
    dh1, gv["mix_norm"] = _inproj_bwd(h1, dh2, dlo, dgl, vec["mix_norm"], wm["w_in"], "inproj_bwd")

    (dh0, u1, a1, dgp1, gv["ffn1_norm"]), touts = _ffn_bwd(h0, dh1, vec["ffn1_norm"], w1gu, w1d, "ffn1_bwd",
                                                            scatter(G_IN) + join(G_OUT))
    if dist:
        scattered(G_IN, touts[0], me)
        joined(G_OUT, touts[1])
    dh0 = dh0.reshape(B, LP, D)
    grad_x = dh0[:, MPAD:]
    grad_meta = jnp.sum(dh0[:, :NMETA], axis=0)
    share = [_share_task([_stack_vectors([gv[n] for n in VECS]), grad_meta, lpart])] if dist else []
    g4["ffn1_w_gu"], touts = _wgrad(u1, dgp1, "ffn1_dwgu", bn=DFF, shards=2, tasks=share)
    shared = touts[0] if dist else None
    dwd1, touts = _wgrad(a1, dh1, "ffn1_dwd", scale=0.5, bk=FH, tasks=scatter(["ffn1_w_gu"]) + join(G_IN))
    g4["ffn1_w_down"] = dwd1.reshape(N_CHIPS, DFF // N_CHIPS, D)
    if dist:
        scattered(["ffn1_w_gu"], touts[0], me)
        joined(G_IN, touts[1])
        scattered(["ffn1_w_down"], _run_tasks(scatter(["ffn1_w_down"]), "rs_ffn1_w_down")[0], me)
        joined(G_FFN1, _run_tasks(join(G_FFN1), "rs_join_ffn1")[0])
    return lpart, grad_x, grad_meta, gv, (red if dist else g4), shared


def _cols_from_shards(g4):
    n, r, c = g4.shape
    return g4.transpose(1, 0, 2).reshape(r, n * c)


def _cols_to_shards(full):
    r, c4 = full.shape
    return full.reshape(r, N_CHIPS, c4 // N_CHIPS).transpose(1, 0, 2)


def _win_to_kernel(wfull):
    z = lambda n: jnp.zeros((D, n), wfull.dtype)
    fl, cq, ckv, kr, gate = (wfull[:, 1536:1544], wfull[:, 1544:1800], wfull[:, 1800:1928],
                             wfull[:, 1928:1960], wfull[:, 1960:4008])
    misc = jnp.concatenate([z(L_KR), kr, fl, z(128 - L_FL - NH)], axis=1)
    return jnp.concatenate([wfull[:, :1536], cq, ckv, misc, gate], axis=1)


def _win_from_kernel(gk):
    m = C_MISC
    return jnp.concatenate([gk[:, :1536], gk[:, m + L_FL:m + L_FL + NH], gk[:, C_CQ:C_CQ + QR],
                            gk[:, C_CKV:C_CKV + KVR], gk[:, m + L_KR:m + L_KR + ROPE], gk[:, C_GATE:]], axis=1)


def _pieces(g4):
    n, r, c = g4.shape
    return g4.reshape(2 * n, r // 2, c)


def _mixer_weights(gath):
    w = {}
    w["w_in"] = _win_to_kernel(_cols_from_shards(gath["w_in"]))
    uq = _cols_from_shards(gath["mla_w_uq"]).reshape(QR, NH, MLA_QK)
    w["wuq"] = jnp.pad(uq, ((0, 0), (0, 0), (0, 128 - MLA_QK))).reshape(QR, NH * 128)
    ukv = _cols_from_shards(gath["mla_w_ukv"]).reshape(KVR, NH, 128)
    w["wuk"] = jnp.pad(ukv[:, :, :64], ((0, 0), (0, 0), (0, 64))).reshape(KVR, NH * 128)
    w["wuv"] = ukv[:, :, 64:].reshape(KVR, NH * 64)
    w["wbf"] = _cols_from_shards(gath["w_branch_fox"])
    w["wbm"] = _cols_from_shards(gath["w_branch_mla"])
    w["w_out"] = gath["w_out"].reshape(D, D)
    return w


def _chip_peers(x, y):
    return [(1 - x, y), (x, 1 - y), (1 - x, 1 - y)]


RELS = [(dx, dy, dc) for dx in (0, 1) for dy in (0, 1) for dc in (0, 1)][1:]


def _here():
    return lax.axis_index("x"), lax.axis_index("y"), lax.axis_index("c")


def _flip(a, d):
    return (1 - a) if d else a


def _remote(src, dst, send, recv, i, dev):
    return functools.partial(pltpu.make_async_remote_copy, src_ref=src, dst_ref=dst, send_sem=send.at[i],
                             recv_sem=recv.at[i], device_id=dev, device_id_type=MESH)


def _gathered_shape(s, wide):
    return jax.ShapeDtypeStruct((s.shape[0], N_CHIPS * s.shape[1]) if wide else (N_CHIPS,) + s.shape, s.dtype)


def _slot(ref, j, shard, wide, rows=None):
    if wide:
        lanes = pl.ds(pl.multiple_of(j * shard.shape[1], 128), shard.shape[1])
        return ref.at[slice(None) if rows is None else rows, lanes]
    return ref.at[j] if rows is None else ref.at[j, rows]


def _gather_task(shards, wide=()):
    n = len(shards)

    def descs(ins, outs, sems):
        send, recv, loc = sems
        x, y, c = _here()
        j = 2 * x + y
        locs, pairs = [], []
        for k in range(n):
            at = functools.partial(_slot, outs[k], shard=shards[k], wide=k in wide)
            locs.append(functools.partial(pltpu.make_async_copy, ins[k], at(j), loc.at[k]))
            for r, (px, py) in enumerate(_chip_peers(x, y)):
                dev = (px, py, c)
                pairs.append((_remote(ins[k], at(j), send, recv, 3 * k + r, dev),
                              _remote(ins[k], at(2 * px + py), send, recv, 3 * k + r, dev)))
        return locs, pairs

    return _Task(shards, [_gathered_shape(s, k in wide) for k, s in enumerate(shards)],
                 [pltpu.SemaphoreType.DMA((3 * n,)), pltpu.SemaphoreType.DMA((3 * n,)), pltpu.SemaphoreType.DMA((n,))],
                 descs)


class _SplitGather(_Task):
    PARTS = 2

    def __init__(self, shards, wide=()):
        n = 3 * len(shards) * self.PARTS
        dma = pltpu.SemaphoreType.DMA
        self.wide = wide
        super().__init__(shards, [_gathered_shape(s, k in wide) for k, s in enumerate(shards)],
                         [dma((n,)), dma((n,)), dma((n,)), dma((n,)), dma((len(shards),))], None)

    def _plan(self, ins, outs, sems):
        send, recv, fsend, frecv, loc = sems
        x, y, c = _here()
        j = 2 * x + y
        locs, first, passed = [], [], []
        for k in range(len(ins)):
            h = self.ins[k].shape[0] // 2
            parts = self.PARTS if h % (32 * self.PARTS) == 0 else 1
            hp = h // parts
            at = functools.partial(_slot, outs[k], shard=self.ins[k], wide=k in self.wide)
            locs.append(functools.partial(pltpu.make_async_copy, ins[k], at(j), loc.at[k]))
            for r, (px, py) in enumerate(_chip_peers(x, y)):
                p = 2 * px + py
                for q in range(parts):
                    i = (3 * k + r) * self.PARTS + q
                    mine = pl.ds(pl.multiple_of(c * h + q * hp, 8), hp)
                    theirs = pl.ds(pl.multiple_of((1 - c) * h + q * hp, 8), hp)
                    first.append((_remote(ins[k].at[mine], at(j, rows=mine), send, recv, i, (px, py, c)),
                                  _remote(ins[k].at[mine], at(p, rows=mine), send, recv, i, (px, py, c))))
                    passed.append((_remote(at(p, rows=mine), at(p, rows=mine), fsend, frecv, i, (x, y, 1 - c)),
                                   _remote(at(p, rows=mine), at(p, rows=theirs), fsend, frecv, i, (x, y, 1 - c))))
        return locs, first, passed

    def start(self, ins, outs, sems):
        locs, first, _ = self._plan(ins, outs, sems)
        for lc in locs:
            lc().start()
        for snd, _ in first:
            snd().start()

    def wait(self, ins, outs, sems):
        locs, first, passed = self._plan(ins, outs, sems)
        for (_, landed), (pass_on, _) in zip(first, passed):
            landed().wait_recv()
            pass_on().start()
        for _, rcv in passed:
            rcv().wait_recv()
        for snd, _ in first + passed:
            snd().wait_send()
        for lc in locs:
            lc().wait()


def _a2a_task(ps):
    n = len(ps)
    nr = len(RELS)

    def descs(ins, outs, sems):
        send, recv = sems
        x, y, c = _here()
        me = 4 * x + 2 * y + c
        pairs = []
        for k in range(n):
            for i, (dx, dy, dc) in enumerate(RELS):
                dev = (_flip(x, dx), _flip(y, dy), _flip(c, dc))
                peer = 4 * dev[0] + 2 * dev[1] + dev[2]
                pairs.append((_remote(ins[k].at[peer], outs[k].at[me], send, recv, nr * k + i, dev),
                              _remote(ins[k].at[peer], outs[k].at[peer], send, recv, nr * k + i, dev)))
        return [], pairs

    return _Task(ps, [jax.ShapeDtypeStruct(p.shape, p.dtype) for p in ps],
                 [pltpu.SemaphoreType.DMA((nr * n,)), pltpu.SemaphoreType.DMA((nr * n,))], descs)


def _share_task(vs):
    n = len(vs)
    nr = len(RELS)

    def descs(ins, outs, sems):
        send, recv, loc = sems
        x, y, c = _here()
        me = 4 * x + 2 * y + c
        locs, pairs = [], []
        for k in range(n):
            locs.append(functools.partial(pltpu.make_async_copy, ins[k], outs[k].at[me], loc.at[k]))
            for i, (dx, dy, dc) in enumerate(RELS):
                dev = (_flip(x, dx), _flip(y, dy), _flip(c, dc))
                peer = 4 * dev[0] + 2 * dev[1] + dev[2]
                pairs.append((_remote(ins[k], outs[k].at[me], send, recv, nr * k + i, dev),
                              _remote(ins[k], outs[k].at[peer], send, recv, nr * k + i, dev)))
        return locs, pairs

    return _Task(vs, [jax.ShapeDtypeStruct((N_DEV,) + v.shape, v.dtype) for v in vs],
                 [pltpu.SemaphoreType.DMA((nr * n,)), pltpu.SemaphoreType.DMA((nr * n,)), pltpu.SemaphoreType.DMA((n,))],
                 descs)


def _join_task(ss):
    n = len(ss)

    def descs(ins, outs, sems):
        send, recv = sems
        x, y, c = _here()
        pairs = []
        for k in range(n):
            cp = _remote(ins[k], outs[k], send, recv, k, (x, y, 1 - c))
            pairs.append((cp, cp))
        return [], pairs

    return _Task(ss, [jax.ShapeDtypeStruct(s.shape, s.dtype) for s in ss],
                 [pltpu.SemaphoreType.DMA((n,)), pltpu.SemaphoreType.DMA((n,))], descs)


def _sum_pieces(recv, own, me, name):
    n, h, c = recv.shape
    tr = h

    def body(me_ref, r_ref, o_ref, out_ref):
        s = pl.program_id(1)
        val = jnp.where(s == me_ref[0], o_ref[0], r_ref[0]).astype(F32)

        @pl.when(s == 0)
        def _():
            out_ref[...] = val

        @pl.when(s > 0)
        def _():
            out_ref[...] += val

    def other(s, m):
        return jnp.where(s == m[0], (s + 1) % n, s)

    return _call(
        body, name=name,
        grid_spec=pltpu.PrefetchScalarGridSpec(
            num_scalar_prefetch=1, grid=(h // tr, n),
            in_specs=[pl.BlockSpec((1, tr, c), lambda i, s, m: (other(s, m), i, 0)),
                      pl.BlockSpec((1, tr, c), lambda i, s, m: (m[0], i, 0))],
            out_specs=pl.BlockSpec((tr, c), lambda i, s, m: (i, 0))),
        out_shape=jax.ShapeDtypeStruct((h, c), F32),
        compiler_params=_cp(),
    )(me, recv, own)


def _adamw_update(gg, w, m, v):
    c1 = 1.0 / (1.0 - ADAM_B1 ** ADAM_STEP)
    c2 = 1.0 / (1.0 - ADAM_B2 ** ADAM_STEP)
    nm = ADAM_B1 * m + (1.0 - ADAM_B1) * gg
    nv = ADAM_B2 * v + (1.0 - ADAM_B2) * (gg * gg)
    return -ADAM_LR * ((nm * c1) / (jnp.sqrt(nv * c2) + ADAM_EPS) + ADAM_WD * w), nm, nv


def _adamw_small(gvec8, gmeta8, lp8, chip, ws, ms, vs, name):
    na = len(ws)

    def dev_sum(ref):
        acc = ref[0]
        for s in range(1, N_DEV):
            acc = acc + ref[s]
        return acc

    def body(c_ref, gv_ref, gm_ref, lp_ref, *refs):
        w_refs, m_refs, v_refs = refs[:na], refs[na:2 * na], refs[2 * na:3 * na]
        l_ref = refs[3 * na]
        outs = refs[3 * na + 1:]
        g_refs, d_refs, nm_refs, nv_refs = outs[:na], outs[na:2 * na], outs[2 * na:3 * na], outs[3 * na:]
        l_ref[...] = dev_sum(lp_ref)
        gvec = dev_sum(gv_ref)
        for k in range(na):
            gg = gvec[k:k + 1, 0:ws[k].shape[1]] if k < na - 1 else dev_sum(gm_ref)
            g_refs[k][...] = gg
            d_refs[k][...], nm_refs[k][...], nv_refs[k][...] = _adamw_update(gg, w_refs[k][...], m_refs[k][...], v_refs[k][...])

    whole = lambda a: pl.BlockSpec(a.shape, lambda i, c: (0,) * a.ndim)
    arrs = list(ws) + list(ms) + list(vs)
    res = _call(
        body, name=name,
        grid_spec=pltpu.PrefetchScalarGridSpec(
            num_scalar_prefetch=1, grid=(1,),
            in_specs=[whole(gvec8), pl.BlockSpec((N_DEV, NMETA, D // N_CHIPS), lambda i, c: (0, 0, c[0])), whole(lp8)]
                     + [whole(a) for a in arrs],
            out_specs=[pl.BlockSpec((1, D), lambda i, c: (0, 0))] + [whole(a) for a in ws] * 4),
        out_shape=[jax.ShapeDtypeStruct((1, D), F32)] + [jax.ShapeDtypeStruct(a.shape, F32) for a in ws] * 4,
        compiler_params=_cp(),
    )(chip, gvec8, gmeta8, lp8, *arrs)
    return res[0], [list(res[1 + i * na:1 + (i + 1) * na]) for i in range(4)]


def _adamw_halves(wt, mine, theirs, m, v, core, name):
    r, c = wt.shape
    h = r // 2
    tr = _tile(h, (256, 176, 128, 64))
    nh = h // tr

    def body(c_ref, w_ref, a_ref, b_ref, m_ref, v_ref, g_ref, d_ref, nm_ref, nv_ref):
        gg = jnp.where(pl.program_id(0) // nh == c_ref[0], a_ref[...], b_ref[...])
        g_ref[...] = gg
        d_ref[...], nm_ref[...], nv_ref[...] = _adamw_update(gg, w_ref[...], m_ref[...], v_ref[...])

    full = pl.BlockSpec((tr, c), lambda i, cr: (i, 0))
    half = pl.BlockSpec((tr, c), lambda i, cr: (i % nh, 0))
    return _call(
        body, name=name,
        grid_spec=pltpu.PrefetchScalarGridSpec(
            num_scalar_prefetch=1, grid=(2 * nh,),
            in_specs=[full, half, half, full, full], out_specs=[full] * 4),
        out_shape=[jax.ShapeDtypeStruct((r, c), F32)] * 4,
        compiler_params=_cp(),
    )(core, wt, mine, theirs, m, v)


MATS = ["ffn1_w_gu", "ffn1_w_down", "w_in", "mla_w_uq", "mla_w_ukv", "w_branch_fox", "w_branch_mla",
        "w_out", "ffn2_w_gu", "ffn2_w_down"]
VECS = ["ffn1_norm", "mix_norm", "b_forget", "b_gate", "fox_q_norm", "fox_k_norm", "mla_cq_norm",
        "mla_ckv_norm", "mla_q_norm", "mla_k_norm", "ffn2_norm"]
WEIGHTS = ["meta_tokens", "ffn1_norm", "ffn1_w_gu", "ffn1_w_down", "mix_norm", "w_in", "b_forget", "b_gate",
           "fox_q_norm", "fox_k_norm", "mla_cq_norm", "mla_w_uq", "mla_ckv_norm", "mla_w_ukv", "mla_q_norm",
           "mla_k_norm", "w_branch_fox", "w_branch_mla", "w_out", "ffn2_norm", "ffn2_w_gu", "ffn2_w_down"]


VEC_LANES = 2048


def _stack_vectors(parts):
    rows = [_pad_lanes(p, 0, VEC_LANES) for p in parts]
    rows.append(jnp.zeros((-len(parts) % 8, VEC_LANES), F32))
    return jnp.concatenate(rows, axis=0)


def kernel(x, meta_tokens, ffn1_norm, ffn1_w_gu, ffn1_w_down, mix_norm, w_in, b_forget, b_gate, fox_q_norm, fox_k_norm, mla_cq_norm, mla_w_uq, mla_ckv_norm, mla_w_ukv, mla_q_norm, mla_k_norm, w_branch_fox, w_branch_mla, w_out, ffn2_norm, ffn2_w_gu, ffn2_w_down, loss_target, m_meta_tokens, m_ffn1_norm, m_ffn1_w_gu, m_ffn1_w_down, m_mix_norm, m_w_in, m_b_forget, m_b_gate, m_fox_q_norm, m_fox_k_norm, m_mla_cq_norm, m_mla_w_uq, m_mla_ckv_norm, m_mla_w_ukv, m_mla_q_norm, m_mla_k_norm, m_w_branch_fox, m_w_branch_mla, m_w_out, m_ffn2_norm, m_ffn2_w_gu, m_ffn2_w_down, v_meta_tokens, v_ffn1_norm, v_ffn1_w_gu, v_ffn1_w_down, v_mix_norm, v_w_in, v_b_forget, v_b_gate, v_fox_q_norm, v_fox_k_norm, v_mla_cq_norm, v_mla_w_uq, v_mla_ckv_norm, v_mla_w_ukv, v_mla_q_norm, v_mla_k_norm, v_w_branch_fox, v_w_branch_mla, v_w_out, v_ffn2_norm, v_ffn2_w_gu, v_ffn2_w_down):
    a = dict(locals())
    wts = {n: a[n] for n in WEIGHTS}
    ms = {n: a["m_" + n] for n in WEIGHTS}
    vs = {n: a["v_" + n] for n in WEIGHTS}
    cx, cy, cc = lax.axis_index("x"), lax.axis_index("y"), lax.axis_index("c")
    chip = 2 * cx + cy

    shards = {n: wts[n][0].astype(BF16) for n in MATS}
    first = _run_tasks([_SplitGather([shards[n] for n in G_FFN1] + [meta_tokens], wide=(0,))], "gather_ffn1")[0]
    gath = dict(zip(G_FFN1, first[:-1]))
    meta_full = _cols_from_shards(first[-1])

    _, grad_x, _, _, gred, (gvec8, gmeta8, lp8) = _step(x, loss_target, meta_full, {n: wts[n] for n in VECS}, gath, shards)

    sm_names = VECS + ["meta_tokens"]
    lsum, sm = _adamw_small(gvec8, gmeta8, lp8, chip.reshape(1).astype(jnp.int32), [wts[n] for n in sm_names],
                            [ms[n] for n in sm_names], [vs[n] for n in sm_names], "adamw_small")
    loss = jnp.sum(lsum)

    grads, delta, new_m, new_v = {}, {}, {}, {}
    core = cc.reshape(1).astype(jnp.int32)
    for n in MATS:
        shp = wts[n].shape
        mine, theirs = gred[n]
        res = _adamw_halves(wts[n][0], mine, theirs, ms[n][0], vs[n][0], core, "adamw_" + n)
        grads[n], delta[n], new_m[n], new_v[n] = (t.reshape(shp) for t in res)
    for k, n in enumerate(sm_names):
        grads[n], delta[n], new_m[n], new_v[n] = (sm[i][k] for i in range(4))

    return (loss, grad_x, *[grads[n] for n in WEIGHTS], *[delta[n] for n in WEIGHTS],
            *[new_m[n] for n in WEIGHTS], *[new_v[n] for n in WEIGHTS])
```

```python
import functools

import jax
import jax.numpy as jnp
from jax import lax
from jax.experimental import pallas as pl
from jax.experimental.pallas import tpu as pltpu

F32 = jnp.float32
BF16 = jnp.bfloat16
MESH = pl.DeviceIdType.MESH

D = 1024
DFF = 2816
FH = DFF // 2
NMETA = 16
MPAD = 128
EPS = 1e-6
NH = 8
FOXW = 512
QR = 256
KVR = 128
ROPE = 32
MLA_QK = 96
PROJW = 4096
ROPE_THETA = 10000.0
N_CHIPS = 4
N_DEV = 8

ADAM_LR = 0.001
ADAM_B1 = 0.9
ADAM_B2 = 0.999
ADAM_EPS = 1e-08
ADAM_WD = 0.01
ADAM_STEP = 10

VMEM_LIMIT = 56 * 2**20
ATTN_TQ = 256
GRAD_DTYPE = BF16

NT = (((1,), (1,)), ((), ()))
TN = (((0,), (0,)), ((), ()))


def _call(body, **kw):
    return pl.pallas_call(body, **kw)


def _cp(**kw):
    return pltpu.CompilerParams(vmem_limit_bytes=VMEM_LIMIT, **kw)


HBM = pl.BlockSpec(memory_space=pltpu.HBM)


class _Task:
    def __init__(self, ins, out_shapes, sems, descs):
        self.ins, self.out_shapes, self.sems, self.descs = list(ins), list(out_shapes), list(sems), descs

    @staticmethod
    def _if(cond, action):
        if cond is None:
            action()
        else:
            pl.when(cond)(action)

    def start(self, ins, outs, sems):
        locs, pairs = self.descs(ins, outs, sems)
        for lc in locs:
            lc().start()
        for snd, _, *ok in pairs:
            self._if(ok[0] if ok else None, lambda snd=snd: snd().start())

    def wait(self, ins, outs, sems):
        locs, pairs = self.descs(ins, outs, sems)
        for _, rcv, *ok in pairs:
            self._if(ok[1] if ok else None, lambda rcv=rcv: rcv().wait_recv())
        for snd, _, *ok in pairs:
            self._if(ok[0] if ok else None, lambda snd=snd: snd().wait_send())
        for lc in locs:
            lc().wait()


def _call_tasks(body, tasks, *, name, grid, in_specs, out_specs, out_shape, args, scratch_shapes=()):
    in_specs, out_specs, out_shape, scratch_shapes = map(list, (in_specs, out_specs, out_shape, scratch_shapes))
    n_in, n_out, n_sc = len(in_specs), len(out_specs), len(scratch_shapes)
    t_in = [len(t.ins) for t in tasks]
    t_out = [len(t.out_shapes) for t in tasks]
    t_sem = [len(t.sems) for t in tasks]

    def wrapped(*refs):
        pos = [0]

        def take(n):
            pos[0] += n
            return refs[pos[0] - n:pos[0]]

        ins, tins = take(n_in), [take(n) for n in t_in]
        outs, touts = take(n_out), [take(n) for n in t_out]
        sc, tsems = take(n_sc), [take(n) for n in t_sem]
        if tasks:
            first = functools.reduce(jnp.logical_and, [pl.program_id(a) == 0 for a in range(len(grid))])
            last = functools.reduce(jnp.logical_and, [pl.program_id(a) == grid[a] - 1 for a in range(len(grid))])

            @pl.when(first)
            def _():
                for t, a, b, s in zip(tasks, tins, touts, tsems):
                    t.start(a, b, s)

        body(*ins, *outs, *sc)
        if tasks:
            @pl.when(last)
            def _():
                for t, a, b, s in zip(tasks, tins, touts, tsems):
                    t.wait(a, b, s)

    res = _call(
        wrapped, name=name, grid=grid,
        in_specs=in_specs + [HBM] * sum(t_in), out_specs=out_specs + [HBM] * sum(t_out),
        out_shape=out_shape + [s for t in tasks for s in t.out_shapes],
        scratch_shapes=scratch_shapes + [s for t in tasks for s in t.sems],
        compiler_params=_cp(),
    )(*args, *[a for t in tasks for a in t.ins])
    res = list(res)
    touts, pos = [], n_out
    for n in t_out:
        touts.append(res[pos:pos + n])
        pos += n
    return res[:n_out], touts


def _run_tasks(tasks, name):
    t_in = [len(t.ins) for t in tasks]
    t_out = [len(t.out_shapes) for t in tasks]
    t_sem = [len(t.sems) for t in tasks]

    def body(*refs):
        pos = [0]

        def take(n):
            pos[0] += n
            return refs[pos[0] - n:pos[0]]

        tins, touts, tsems = [take(n) for n in t_in], [take(n) for n in t_out], [take(n) for n in t_sem]
        for t, a, b, s in zip(tasks, tins, touts, tsems):
            t.start(a, b, s)
        for t, a, b, s in zip(tasks, tins, touts, tsems):
            t.wait(a, b, s)

    res = list(_call(
        body, name=name, in_specs=[HBM] * sum(t_in), out_specs=[HBM] * sum(t_out),
        out_shape=[s for t in tasks for s in t.out_shapes],
        scratch_shapes=[s for t in tasks for s in t.sems],
    )(*[a for t in tasks for a in t.ins]))
    touts, pos = [], 0
    for n in t_out:
        touts.append(res[pos:pos + n])
        pos += n
    return touts


def _tile(n, cands):
    for c in cands:
        if n % c == 0:
            return c
    raise ValueError(f"no tile for {n} among {cands}")


def _dot(a, b, dims=None):
    if dims is None:
        return jnp.dot(a, b, preferred_element_type=F32)
    return lax.dot_general(a, b, dims, preferred_element_type=F32)


def _lane(shape):
    return lax.broadcasted_iota(jnp.int32, shape, len(shape) - 1)


def _seg_ones(w, log2_seg):
    r = lax.shift_right_logical(lax.broadcasted_iota(jnp.int32, (w, w), 0), log2_seg)
    c = lax.shift_right_logical(lax.broadcasted_iota(jnp.int32, (w, w), 1), log2_seg)
    return jnp.where(r == c, 1.0, 0.0).astype(BF16)


def _seg_sum(x, ones):
    hi = x.astype(BF16)
    r1 = x - hi.astype(F32)
    mid = r1.astype(BF16)
    lo = (r1 - mid.astype(F32)).astype(BF16)
    return _dot(hi, ones) + _dot(mid, ones) + _dot(lo, ones)


def _lane_sum(x, ones):
    return jnp.sum(x, axis=-1, keepdims=True) if ones is None else _seg_sum(x, ones)


def _rms(x, gain, n, ones=None):
    r = lax.rsqrt(_lane_sum(x * x, ones) * (1.0 / n) + EPS)
    xh = x * r
    return xh * gain, xh, r


def _rms_bwd(dy, xh, r, gain, n, ones=None):
    dxh = dy * gain
    return r * (dxh - xh * (_lane_sum(dxh * xh, ones) * (1.0 / n)))


def _rope_swap(x):
    ln = _lane(x.shape)
    sw = jnp.where(ln < 80, pltpu.roll(x, 112, 1), pltpu.roll(x, 16, 1))
    return jnp.where(jnp.logical_and(ln >= 64, ln < 96), sw, 0.0)


def _colsum(x):
    return jnp.sum(x, axis=0, keepdims=True)


def _ffn_weight_specs():
    once = pl.Buffered(1)
    return [pl.BlockSpec((D, DFF), lambda i: (0, 0), pipeline_mode=once),
            pl.BlockSpec((D, DFF), lambda i: (0, 1), pipeline_mode=once),
            pl.BlockSpec((DFF, D), lambda i: (0, 0), pipeline_mode=once)]


def _ffn_fwd(h, norm, wgu, wd, name, tasks=()):
    T = h.shape[0]
    tm = _tile(T, (512, 384, 256, 128))

    def body(h_ref, n_ref, wg_ref, wu_ref, wd_ref, o_ref):
        x = h_ref[...]
        u, _, _ = _rms(x, n_ref[...], D)
        ub = u.astype(BF16)
        g = _dot(ub, wg_ref[...])
        p = _dot(ub, wu_ref[...])
        a = (g * jax.nn.sigmoid(g)) * p
        o_ref[...] = x + 0.5 * _dot(a.astype(BF16), wd_ref[...])

    (out,), touts = _call_tasks(
        body, tasks, name=name, grid=(T // tm,),
        in_specs=[pl.BlockSpec((tm, D), lambda i: (i, 0)), pl.BlockSpec((1, D), lambda i: (0, 0))] + _ffn_weight_specs(),
        out_specs=[pl.BlockSpec((tm, D), lambda i: (i, 0))],
        out_shape=[jax.ShapeDtypeStruct((T, D), F32)],
        args=(h, norm, wgu, wgu, wd))
    return out, touts


def _ffn_bwd(h, dout, norm, wgu, wd, name, tasks=()):
    T = h.shape[0]
    tm = _tile(T, (256, 128))

    def body(h_ref, d_ref, n_ref, wg_ref, wu_ref, wd_ref, dh_ref, u_ref, a_ref, dgp_ref, dn_ref):
        @pl.when(pl.program_id(0) == 0)
        def _():
            dn_ref[...] = jnp.zeros_like(dn_ref)

        u, xh, r = _rms(h_ref[...], n_ref[...], D)
        ub = u.astype(BF16)
        u_ref[...] = ub
        g = _dot(ub, wg_ref[...])
        p = _dot(ub, wu_ref[...])
        s = jax.nn.sigmoid(g)
        sl = g * s
        dz = (0.5 * d_ref[...]).astype(BF16)
        da = _dot(dz, wd_ref[...], NT)
        dp = da * sl
        dg = (da * p) * (s * (1.0 + g * (1.0 - s)))
        a_ref[...] = (sl * p).astype(BF16)
        dgb = dg.astype(BF16)
        dpb = dp.astype(BF16)
        dgp_ref[:, :DFF] = dgb
        dgp_ref[:, DFF:] = dpb
        du = _dot(dgb, wg_ref[...], NT) + _dot(dpb, wu_ref[...], NT)
        dn_ref[...] += _colsum(du * xh)
        dh_ref[...] = d_ref[...] + _rms_bwd(du, xh, r, n_ref[...], D)

    row = lambda w: pl.BlockSpec((tm, w), lambda i: (i, 0))
    return _call_tasks(
        body, tasks, name=name, grid=(T // tm,),
        in_specs=[row(D), row(D), pl.BlockSpec((1, D), lambda i: (0, 0))] + _ffn_weight_specs(),
        out_specs=[row(D), row(D), row(DFF), row(2 * DFF), pl.BlockSpec((1, D), lambda i: (0, 0))],
        out_shape=[jax.ShapeDtypeStruct((T, D), F32),
                   jax.ShapeDtypeStruct((T, D), BF16),
                   jax.ShapeDtypeStruct((T, DFF), BF16),
                   jax.ShapeDtypeStruct((T, 2 * DFF), BF16),
                   jax.ShapeDtypeStruct((1, D), F32)],
        args=(h, dout, norm, wgu, wgu, wd))


def _wgrad(x, y, name, scale=1.0, bk=None, bn=None, shards=0, bt=512, rows=None, tasks=()):
    T = x.shape[0]
    N = y.shape[1]
    bk = bk or x.shape[1]
    bn = bn or N
    k0, nk = rows or (0, x.shape[1] // bk)
    K = nk * bk
    bt = _tile(T, (bt, 512, 384, 256, 128))
    nt = T // bt

    def body(x_ref, y_ref, o_ref, acc_ref):
        t = pl.program_id(2)

        @pl.when(t == 0)
        def _():
            acc_ref[...] = jnp.zeros_like(acc_ref)

        acc_ref[...] += _dot(x_ref[...].astype(BF16), y_ref[...].astype(BF16), TN)

        @pl.when(t == nt - 1)
        def _():
            res = (acc_ref[...] * scale).astype(o_ref.dtype)
            if shards:
                w = bn // shards
                for s in range(shards):
                    o_ref[s] = res[:, s * w:(s + 1) * w]
            else:
                o_ref[...] = res

    if shards:
        out_spec = pl.BlockSpec((shards, bk, bn // shards), lambda i, j, t: (j, i, 0))
        out_shape = jax.ShapeDtypeStruct((N * shards // bn, K, bn // shards), GRAD_DTYPE)
    else:
        out_spec = pl.BlockSpec((bk, bn), lambda i, j, t: (i, j))
        out_shape = jax.ShapeDtypeStruct((K, N), GRAD_DTYPE)
    (out,), touts = _call_tasks(
        body, tasks, name=name, grid=(nk, N // bn, nt),
        in_specs=[pl.BlockSpec((bt, bk), lambda i, j, t: (t, i + k0)),
                  pl.BlockSpec((bt, bn), lambda i, j, t: (t, j))],
        out_specs=[out_spec], out_shape=[out_shape],
        scratch_shapes=[pltpu.VMEM((bk, bn), F32)],
        args=(x, y))
    return out, touts


def _inproj_fwd(h, norm, w, name):
    T = h.shape[0]
    tm = _tile(T, (512, 384, 256, 128))

    def body(h_ref, n_ref, w_ref, o_ref, u_ref):
        u, _, _ = _rms(h_ref[...], n_ref[...], D)
        ub = u.astype(BF16)
        u_ref[...] = ub
        o_ref[...] = _dot(ub, w_ref[...])

    return _call(
        body, name=name, grid=(T // tm,),
        in_specs=[pl.BlockSpec((tm, D), lambda i: (i, 0)),
                  pl.BlockSpec((1, D), lambda i: (0, 0)),
                  pl.BlockSpec((D, PROJW), lambda i: (0, 0), pipeline_mode=pl.Buffered(1))],
        out_specs=[pl.BlockSpec((tm, PROJW), lambda i: (i, 0)),
                   pl.BlockSpec((tm, D), lambda i: (i, 0))],
        out_shape=[jax.ShapeDtypeStruct((T, PROJW), F32), jax.ShapeDtypeStruct((T, D), BF16)],
        compiler_params=_cp(),
    )(h, norm, w)


def _inproj_bwd(h, dres, dlo, dhi, norm, w, name):
    T = h.shape[0]
    tm = _tile(T, (512, 384, 256, 128))
    hw = PROJW // 2

    def body(h_ref, d_ref, lo_ref, hi_ref, n_ref, wlo_ref, whi_ref, dh_ref, dn_ref):
        @pl.when(pl.program_id(0) == 0)
        def _():
            dn_ref[...] = jnp.zeros_like(dn_ref)

        _, xh, r = _rms(h_ref[...], n_ref[...], D)
        du = _dot(lo_ref[...], wlo_ref[...], NT) + _dot(hi_ref[...], whi_ref[...], NT)
        dn_ref[...] += _colsum(du * xh)
        dh_ref[...] = d_ref[...] + _rms_bwd(du, xh, r, n_ref[...], D)

    return _call(
        body, name=name, grid=(T // tm,),
        in_specs=[pl.BlockSpec((tm, D), lambda i: (i, 0)),
                  pl.BlockSpec((tm, D), lambda i: (i, 0)),
                  pl.BlockSpec((tm, hw), lambda i: (i, 0)),
                  pl.BlockSpec((tm, hw), lambda i: (i, 0)),
                  pl.BlockSpec((1, D), lambda i: (0, 0)),
                  pl.BlockSpec((D, hw), lambda i: (0, 0)),
                  pl.BlockSpec((D, hw), lambda i: (0, 1))],
        out_specs=[pl.BlockSpec((tm, D), lambda i: (i, 0)),
                   pl.BlockSpec((1, D), lambda i: (0, 0))],
        out_shape=[jax.ShapeDtypeStruct((T, D), F32), jax.ShapeDtypeStruct((1, D), F32)],
        compiler_params=_cp(),
    )(h, dres, dlo, dhi, norm, w, w)


C_FQ, C_FK, C_FV, C_CQ, C_CKV, C_MISC, C_GATE = 0, 512, 1024, 1536, 1792, 1920, 2048
L_KR, L_FL = 64, 96


def _prep_fwd(proj, rc, rs, gfq, gfk, gcq, gckv, gmq, gmk, bfv, wuq, wuk, wuv, name):
    T = proj.shape[0]
    tm = _tile(T, (256, 128))

    def body(p_ref, rc_ref, rs_ref, gfq_ref, gfk_ref, gcq_ref, gckv_ref, gmq_ref, gmk_ref, bf_ref,
             wuq_ref, wuk_ref, wuv_ref, fq_ref, fk_ref, fv_ref, qm_ref, km_ref, vm_ref, lf_ref):
        o64, o128, o256 = _seg_ones(128, 6), _seg_ones(128, 7), _seg_ones(256, 8)
        for blk in range(4):
            for (c0, g_ref, o_ref) in ((C_FQ, gfq_ref, fq_ref), (C_FK, gfk_ref, fk_ref)):
                x = p_ref[:, c0 + 128 * blk:c0 + 128 * (blk + 1)]
                fn, _, _ = _rms(x, g_ref[...], 64, o64)
                o_ref[:, 128 * blk:128 * (blk + 1)] = fn.astype(BF16)
        fv_ref[...] = p_ref[:, C_FV:C_FV + 512].astype(BF16)

        rcv = rc_ref[...]
        rsv = rs_ref[...]
        cqn, _, _ = _rms(p_ref[:, C_CQ:C_CQ + QR], gcq_ref[...], QR, o256)
        qpre = _dot(cqn.astype(BF16), wuq_ref[...])
        ckvn, _, _ = _rms(p_ref[:, C_CKV:C_CKV + KVR], gckv_ref[...], KVR, o128)
        ckvb = ckvn.astype(BF16)
        kpre = _dot(ckvb, wuk_ref[...])
        vm_ref[...] = _dot(ckvb, wuv_ref[...]).astype(BF16)
        misc = p_ref[:, C_MISC:C_MISC + 128]
        ln = _lane(misc.shape)
        kr = jnp.where(jnp.logical_and(ln >= L_KR, ln < L_KR + ROPE), misc, 0.0)
        for hh in range(NH):
            sl = slice(128 * hh, 128 * (hh + 1))
            qn, _, _ = _rms(qpre[:, sl], gmq_ref[...], MLA_QK, o128)
            qm_ref[:, sl] = (qn * rcv + _rope_swap(qn) * rsv).astype(BF16)
            kn, _, _ = _rms(kpre[:, sl] + kr, gmk_ref[...], MLA_QK, o128)
            km_ref[:, sl] = (kn * rcv + _rope_swap(kn) * rsv).astype(BF16)
        z = misc + bf_ref[...]
        lf_ref[...] = jnp.minimum(z, 0.0) - jnp.log(1.0 + jnp.exp(-jnp.abs(z)))

    row = lambda w: pl.BlockSpec((tm, w), lambda i: (i, 0))
    full = lambda a: pl.BlockSpec(a.shape, lambda i: (0, 0))
    return _call(
        body, name=name, grid=(T // tm,),
        in_specs=[row(PROJW // 2), row(128), row(128)] + [full(a) for a in (gfq, gfk, gcq, gckv, gmq, gmk, bfv, wuq, wuk, wuv)],
        out_specs=[row(512), row(512), row(512), row(1024), row(1024), row(512), row(128)],
        out_shape=[jax.ShapeDtypeStruct((T, 512), BF16), jax.ShapeDtypeStruct((T, 512), BF16),
                   jax.ShapeDtypeStruct((T, 512), BF16), jax.ShapeDtypeStruct((T, 1024), BF16),
                   jax.ShapeDtypeStruct((T, 1024), BF16), jax.ShapeDtypeStruct((T, 512), BF16),
                   jax.ShapeDtypeStruct((T, 128), F32)],
        compiler_params=_cp(),
    )(proj, rc, rs, gfq, gfk, gcq, gckv, gmq, gmk, bfv, wuq, wuk, wuv)


def _prep_bwd(proj, rc, rs, gfq, gfk, gcq, gckv, gmq, gmk, bfv, wuq, wuk, wuv,
              dfq, dfk, dfv, dqm, dkm, dvm, dlf, name):
    T = proj.shape[0]
    tm = _tile(T, (256, 128))

    def body(p_ref, rc_ref, rs_ref, gfq_ref, gfk_ref, gcq_ref, gckv_ref, gmq_ref, gmk_ref, bf_ref,
             wuq_ref, wuk_ref, wuv_ref, dfq_ref, dfk_ref, dfv_ref, dqm_ref, dkm_ref, dvm_ref, dlf_ref,
             dp_ref, dgfq_ref, dgfk_ref, dgcq_ref, dgckv_ref, dgmq_ref, dgmk_ref, dbf_ref,
             dwuq_ref, dwuk_ref, dwuv_ref, dqpre_sc, dkpre_sc):
        accs = (dgfq_ref, dgfk_ref, dgcq_ref, dgckv_ref, dgmq_ref, dgmk_ref, dbf_ref, dwuq_ref, dwuk_ref, dwuv_ref)

        @pl.when(pl.program_id(0) == 0)
        def _():
            for a in accs:
                a[...] = jnp.zeros_like(a)

        o64, o128, o256 = _seg_ones(128, 6), _seg_ones(128, 7), _seg_ones(256, 8)
        for (c0, g_ref, d_ref, dg_ref) in ((C_FQ, gfq_ref, dfq_ref, dgfq_ref), (C_FK, gfk_ref, dfk_ref, dgfk_ref)):
            dg = jnp.zeros((1, 128), F32)
            for blk in range(4):
                x = p_ref[:, c0 + 128 * blk:c0 + 128 * (blk + 1)]
                _, xh, r = _rms(x, g_ref[...], 64, o64)
                dy = d_ref[:, 128 * blk:128 * (blk + 1)]
                dg = dg + _colsum(dy * xh)
                dp_ref[:, c0 + 128 * blk:c0 + 128 * (blk + 1)] = _rms_bwd(dy, xh, r, g_ref[...], 64, o64).astype(BF16)
            dg_ref[...] += dg
        dp_ref[:, C_FV:C_FV + 512] = dfv_ref[...].astype(BF16)

        rcv = rc_ref[...]
        rsv = rs_ref[...]
        cqn, cqh, cqr = _rms(p_ref[:, C_CQ:C_CQ + QR], gcq_ref[...], QR, o256)
        cqb = cqn.astype(BF16)
        qpre = _dot(cqb, wuq_ref[...])
        dgq = jnp.zeros((1, 128), F32)
        for hh in range(NH):
            sl = slice(128 * hh, 128 * (hh + 1))
            _, xh, r = _rms(qpre[:, sl], gmq_ref[...], MLA_QK, o128)
            dout = dqm_ref[:, sl]
            dqn = dout * rcv + _rope_swap(dout * rsv)
            dgq = dgq + _colsum(dqn * xh)
            dqpre_sc[:, sl] = _rms_bwd(dqn, xh, r, gmq_ref[...], MLA_QK, o128).astype(BF16)
        dgmq_ref[...] += dgq
        dqpre = dqpre_sc[...]
        dwuq_ref[...] += _dot(cqb, dqpre, TN)
        dcqn = _dot(dqpre, wuq_ref[...], NT)
        dgcq_ref[...] += _colsum(dcqn * cqh)
        dp_ref[:, C_CQ:C_CQ + QR] = _rms_bwd(dcqn, cqh, cqr, gcq_ref[...], QR, o256).astype(BF16)

        ckvn, ckvh, ckvr = _rms(p_ref[:, C_CKV:C_CKV + KVR], gckv_ref[...], KVR, o128)
        ckvb = ckvn.astype(BF16)
        kpre = _dot(ckvb, wuk_ref[...])
        misc = p_ref[:, C_MISC:C_MISC + 128]
        ln = _lane(misc.shape)
        is_kr = jnp.logical_and(ln >= L_KR, ln < L_KR + ROPE)
        kr = jnp.where(is_kr, misc, 0.0)
        dgk = jnp.zeros((1, 128), F32)
        dkr = jnp.zeros(misc.shape, F32)
        for hh in range(NH):
            sl = slice(128 * hh, 128 * (hh + 1))
            _, xh, r = _rms(kpre[:, sl] + kr, gmk_ref[...], MLA_QK, o128)
            dout = dkm_ref[:, sl]
            dkn = dout * rcv + _rope_swap(dout * rsv)
            dgk = dgk + _colsum(dkn * xh)
            dkx = _rms_bwd(dkn, xh, r, gmk_ref[...], MLA_QK, o128)
            dkr = dkr + jnp.where(is_kr, dkx, 0.0)
            dkpre_sc[:, sl] = jnp.where(ln < 64, dkx, 0.0).astype(BF16)
        dgmk_ref[...] += dgk
        dkpre = dkpre_sc[...]
        dvmb = dvm_ref[...].astype(BF16)
        dwuk_ref[...] += _dot(ckvb, dkpre, TN)
        dwuv_ref[...] += _dot(ckvb, dvmb, TN)
        dckvn = _dot(dkpre, wuk_ref[...], NT) + _dot(dvmb, wuv_ref[...], NT)
        dgckv_ref[...] += _colsum(dckvn * ckvh)
        dp_ref[:, C_CKV:C_CKV + KVR] = _rms_bwd(dckvn, ckvh, ckvr, gckv_ref[...], KVR, o128).astype(BF16)

        z = misc + bf_ref[...]
        dz = dlf_ref[...] * (1.0 - jax.nn.sigmoid(z))
        dbf_ref[...] += _colsum(dz)
        dp_ref[:, C_MISC:C_MISC + 128] = (dkr + dz).astype(BF16)

    row = lambda w: pl.BlockSpec((tm, w), lambda i: (i, 0))
    full = lambda a: pl.BlockSpec(a.shape, lambda i: (0, 0))
    small = (gfq, gfk, gcq, gckv, gmq, gmk, bfv, wuq, wuk, wuv)
    acc_shapes = [(1, 128), (1, 128), (1, QR), (1, KVR), (1, 128), (1, 128), (1, 128),
                  (QR, 1024), (KVR, 1024), (KVR, 512)]
    return _call(
        body, name=name, grid=(T // tm,),
        in_specs=[row(PROJW // 2), row(128), row(128)] + [full(a) for a in small]
                 + [row(512), row(512), row(512), row(1024), row(1024), row(512), row(128)],
        out_specs=[row(PROJW // 2)] + [pl.BlockSpec(s, lambda i: (0, 0)) for s in acc_shapes],
        out_shape=[jax.ShapeDtypeStruct((T, PROJW // 2), BF16)] + [jax.ShapeDtypeStruct(s, F32) for s in acc_shapes],
        scratch_shapes=[pltpu.VMEM((tm, 1024), BF16), pltpu.VMEM((tm, 1024), BF16)],
        compiler_params=_cp(),
    )(proj, rc, rs, *small, dfq, dfk, dfv, dqm, dkm, dvm, dlf)


def _scan_lanes(x, reverse):
    n = x.shape[-1]
    ln = _lane(x.shape)
    k = 1
    while k < n:
        if reverse:
            x = x + jnp.where(ln < n - k, pltpu.roll(x, n - k, x.ndim - 1), 0.0)
        else:
            x = x + jnp.where(ln >= k, pltpu.roll(x, k, x.ndim - 1), 0.0)
        k *= 2
    return x


def _forget_scan(lf, reverse, name):
    def body(x_ref, o_ref):
        x = x_ref[...]
        ln = _lane(x.shape)
        pad = jnp.logical_and(ln >= NMETA, ln < MPAD)
        o_ref[...] = jnp.where(pad, 0.0, _scan_lanes(jnp.where(pad, 0.0, x), reverse))

    return _call(body, name=name, out_shape=jax.ShapeDtypeStruct(lf.shape, F32), compiler_params=_cp())(lf)


def _attn_blocks(LP, tq):
    return [(0, MPAD, MPAD)] + [(MPAD + i * tq, tq, MPAD + (i + 1) * tq) for i in range((LP - MPAD) // tq)]


def _attn_scores(q_ref, k_ref, e, r0, rn, kend, wide, scale, bias):
    if wide:
        qe = q_ref[r0:r0 + rn, 128 * e:128 * (e + 1)]
        ke = k_ref[0:kend, 128 * e:128 * (e + 1)]
    else:
        qb = q_ref[r0:r0 + rn, :]
        mine = (_lane(qb.shape) < 64) if e == 0 else (_lane(qb.shape) >= 64)
        qe = jnp.where(mine, qb, jnp.zeros_like(qb))
        ke = k_ref[0:kend, :]
    s = _dot(qe, ke, NT) * scale
    if bias is not None:
        ct_ref, cr_ref = bias
        s = s + ct_ref[0, r0:r0 + rn, e:e + 1] - cr_ref[0, :, 0:kend]
    neg = -1e30
    if r0 == 0:
        qi = lax.broadcasted_iota(jnp.int32, (rn, kend), 0)
        ki = lax.broadcasted_iota(jnp.int32, (rn, kend), 1)
        s = jnp.where(jnp.logical_and(ki <= qi, ki < NMETA), s, neg)
    else:
        d0 = kend - rn
        head = jnp.where(_lane((rn, MPAD)) < NMETA, s[:, :MPAD], neg)
        qi = lax.broadcasted_iota(jnp.int32, (rn, rn), 0)
        diag = jnp.where(_lane((rn, rn)) <= qi, s[:, d0:], neg)
        s = jnp.concatenate([head] + ([s[:, MPAD:d0]] if d0 > MPAD else []) + [diag], axis=1)
    m = jnp.max(s, axis=-1, keepdims=True)
    p = jnp.exp(s - m)
    l = jnp.sum(p, axis=-1, keepdims=True)
    return qe, ke, p, l


def _attn_specs(B, LP, wide, has_bias):
    qw = 256 if wide else 128
    specs = [pl.BlockSpec((LP, qw), lambda b, hp: (b, hp)),
             pl.BlockSpec((LP, qw), lambda b, hp: (b, hp)),
             pl.BlockSpec((LP, 128), lambda b, hp: (b, hp))]
    bias_specs = []
    if has_bias:
        bias_specs = [pl.BlockSpec((1, LP, 2), lambda b, hp: (b * 4 + hp, 0, 0)),
                      pl.BlockSpec((1, 1, LP), lambda b, hp: (b * 8 + 2 * hp, 0, 0)),
                      pl.BlockSpec((1, 1, LP), lambda b, hp: (b * 8 + 2 * hp + 1, 0, 0))]
    return qw, specs, bias_specs


def _attn_fwd(q, k, v, bias, B, LP, wide, scale, name, tasks=()):
    T = q.shape[0]
    blocks = _attn_blocks(LP, ATTN_TQ)
    qw, specs, bias_specs = _attn_specs(B, LP, wide, bias is not None)

    def body(*refs):
        if bias is not None:
            q_ref, k_ref, v_ref, ct_ref, cr0_ref, cr1_ref, o_ref = refs
            crs = (cr0_ref, cr1_ref)
        else:
            q_ref, k_ref, v_ref, o_ref = refs
        for (r0, rn, kend) in blocks:
            outs = []
            for e in (0, 1):
                bs = (ct_ref, crs[e]) if bias is not None else None
                _, _, p, l = _attn_scores(q_ref, k_ref, e, r0, rn, kend, wide, scale, bs)
                outs.append(_dot(p.astype(BF16), v_ref[0:kend, :]) / l)
            o = jnp.where(_lane(outs[0].shape) < 64, outs[0], outs[1])
            o_ref[r0:r0 + rn, :] = o.astype(BF16)

    args = (q, k, v) + ((bias[0], bias[1], bias[1]) if bias is not None else ())
    (out,), touts = _call_tasks(
        body, tasks, name=name, grid=(B, 4),
        in_specs=specs + bias_specs,
        out_specs=[pl.BlockSpec((LP, 128), lambda b, hp: (b, hp))],
        out_shape=[jax.ShapeDtypeStruct((T, 512), BF16)],
        args=args)
    return out, touts


def _attn_bwd(q, k, v, do, bias, B, LP, wide, scale, name, tasks=()):
    T = q.shape[0]
    blocks = _attn_blocks(LP, ATTN_TQ)
    qw, specs, bias_specs = _attn_specs(B, LP, wide, bias is not None)
    has_bias = bias is not None

    def body(*refs):
        if has_bias:
            (q_ref, k_ref, v_ref, do_ref, ct_ref, cr0_ref, cr1_ref,
             dq_ref, dk_ref, dv_ref, dc0_ref, dc1_ref) = refs
            crs = (cr0_ref, cr1_ref)
            dcs = (dc0_ref, dc1_ref)
            dc0_ref[...] = jnp.zeros_like(dc0_ref)
            dc1_ref[...] = jnp.zeros_like(dc1_ref)
        else:
            q_ref, k_ref, v_ref, do_ref, dq_ref, dk_ref, dv_ref = refs
        dk_ref[...] = jnp.zeros_like(dk_ref)
        dv_ref[...] = jnp.zeros_like(dv_ref)
        for (r0, rn, kend) in blocks:
            dqs = []
            for e in (0, 1):
                bs = (ct_ref, crs[e]) if has_bias else None
                qe, ke, p, l = _attn_scores(q_ref, k_ref, e, r0, rn, kend, wide, scale, bs)
                pn = p * (1.0 / l)
                dob = do_ref[r0:r0 + rn, :]
                mine = (_lane(dob.shape) < 64) if e == 0 else (_lane(dob.shape) >= 64)
                doe = jnp.where(mine, dob, jnp.zeros_like(dob))
                dp = _dot(doe, v_ref[0:kend, :], NT)
                delta = jnp.sum(pn * dp, axis=-1, keepdims=True)
                ds = pn * (dp - delta)
                dsb = ds.astype(BF16)
                dqe = _dot(dsb, ke) * scale
                dke = _dot(dsb, qe, TN) * scale
                if wide:
                    dq_ref[r0:r0 + rn, 128 * e:128 * (e + 1)] = dqe
                    dk_ref[0:kend, 128 * e:128 * (e + 1)] += dke
                else:
                    dqs.append(dqe)
                    dk_ref[0:kend, :] += dke
                dv_ref[0:kend, :] += _dot(pn.astype(BF16), doe, TN)
                if has_bias:
                    dcs[e][0, :, 0:kend] -= _colsum(ds)
            if not wide:
                dq_ref[r0:r0 + rn, :] = jnp.where(_lane(dqs[0].shape) < 64, dqs[0], dqs[1])

    args = (q, k, v, do) + ((bias[0], bias[1], bias[1]) if has_bias else ())
    out_specs = [pl.BlockSpec((LP, qw), lambda b, hp: (b, hp)),
                 pl.BlockSpec((LP, qw), lambda b, hp: (b, hp)),
                 pl.BlockSpec((LP, 128), lambda b, hp: (b, hp))]
    out_shape = [jax.ShapeDtypeStruct(q.shape, F32), jax.ShapeDtypeStruct(q.shape, F32),
                 jax.ShapeDtypeStruct((T, 512), F32)]
    if has_bias:
        out_specs += [pl.BlockSpec((1, 1, LP), lambda b, hp: (b * 4 + hp, 0, 0))] * 2
        out_shape += [jax.ShapeDtypeStruct((B * 4, 1, LP), F32)] * 2
    return _call_tasks(
        body, tasks, name=name, grid=(B, 4),
        in_specs=specs + [pl.BlockSpec((LP, 128), lambda b, hp: (b, hp))] + bias_specs,
        out_specs=out_specs, out_shape=out_shape, args=args)


def _post_fwd(h, of, om, proj, bg, wbf, wbm, wout, name):
    T = h.shape[0]
    tm = _tile(T, (512, 384, 256, 128))

    def body(h_ref, of_ref, om_ref, gl_ref, bg_ref, wbf_ref, wbm_ref, wo_ref, o_ref, mix_ref):
        gate = jax.nn.sigmoid(gl_ref[...] + bg_ref[...])
        mix = gate[:, :D] * _dot(of_ref[...], wbf_ref[...]) + gate[:, D:] * _dot(om_ref[...], wbm_ref[...])
        mb = mix.astype(BF16)
        mix_ref[...] = mb
        o_ref[...] = h_ref[...] + _dot(mb, wo_ref[...])

    row = lambda w: pl.BlockSpec((tm, w), lambda i: (i, 0))
    full = lambda a: pl.BlockSpec(a.shape, lambda i: (0, 0))
    return _call(
        body, name=name, grid=(T // tm,),
        in_specs=[row(D), row(512), row(512), pl.BlockSpec((tm, 2 * D), lambda i: (i, 1)),
                  full(bg), full(wbf), full(wbm), full(wout)],
        out_specs=[row(D), row(D)],
        out_shape=[jax.ShapeDtypeStruct((T, D), F32), jax.ShapeDtypeStruct((T, D), BF16)],
        compiler_params=_cp(),
    )(h, of, om, proj, bg, wbf, wbm, wout)


def _post_bwd(dh, of, om, proj, bg, wbf, wbm, wout, name):
    T = dh.shape[0]
    tm = _tile(T, (512, 384, 256, 128))

    def body(d_ref, of_ref, om_ref, gl_ref, bg_ref, wbf_ref, wbm_ref, wo_ref,
             dgl_ref, dbf_ref, dbm_ref, dof_ref, dom_ref, dbg_ref):
        @pl.when(pl.program_id(0) == 0)
        def _():
            dbg_ref[...] = jnp.zeros_like(dbg_ref)

        gate = jax.nn.sigmoid(gl_ref[...] + bg_ref[...])
        dmix = _dot(d_ref[...].astype(BF16), wo_ref[...], NT)
        ofx = _dot(of_ref[...], wbf_ref[...])
        omx = _dot(om_ref[...], wbm_ref[...])
        gf = gate[:, :D]
        gm = gate[:, D:]
        dof = (dmix * gf).astype(BF16)
        dom = (dmix * gm).astype(BF16)
        dglf = dmix * ofx * gf * (1.0 - gf)
        dglm = dmix * omx * gm * (1.0 - gm)
        dgl_ref[:, :D] = dglf.astype(BF16)
        dgl_ref[:, D:] = dglm.astype(BF16)
        dbg_ref[:, :D] += _colsum(dglf)
        dbg_ref[:, D:] += _colsum(dglm)
        dbf_ref[...] = dof
        dbm_ref[...] = dom
        dof_ref[...] = _dot(dof, wbf_ref[...], NT).astype(BF16)
        dom_ref[...] = _dot(dom, wbm_ref[...], NT).astype(BF16)

    row = lambda w: pl.BlockSpec((tm, w), lambda i: (i, 0))
    full = lambda a: pl.BlockSpec(a.shape, lambda i: (0, 0))
    return _call(
        body, name=name, grid=(T // tm,),
        in_specs=[row(D), row(512), row(512), pl.BlockSpec((tm, 2 * D), lambda i: (i, 1)),
                  full(bg), full(wbf), full(wbm), full(wout)],
        out_specs=[row(2 * D), row(D), row(D), row(512), row(512), pl.BlockSpec((1, 2 * D), lambda i: (0, 0))],
        out_shape=[jax.ShapeDtypeStruct((T, 2 * D), BF16), jax.ShapeDtypeStruct((T, D), BF16),
                   jax.ShapeDtypeStruct((T, D), BF16), jax.ShapeDtypeStruct((T, 512), BF16),
                   jax.ShapeDtypeStruct((T, 512), BF16), jax.ShapeDtypeStruct((1, 2 * D), F32)],
        compiler_params=_cp(),
    )(dh, of, om, proj, bg, wbf, wbm, wout)


def _loss_head(h3, target, B, LP, name):
    S = LP - MPAD
    half = LP // 2
    first = half - MPAD

    def body(h_ref, t_ref, dy_ref, l_ref):
        b = pl.program_id(0)
        k = pl.program_id(1)

        @pl.when(jnp.logical_and(b == 0, k == 0))
        def _():
            l_ref[...] = jnp.zeros_like(l_ref)

        @pl.when(k == 0)
        def _():
            e = h_ref[MPAD:, :] - t_ref[0, 0:first, :]
            dy_ref[0:MPAD, :] = jnp.zeros((MPAD, D), F32)
            dy_ref[MPAD:, :] = e * (1.0 / D)
            l_ref[...] += jnp.sum(e * e, axis=0, keepdims=True) * (0.5 / D)

        @pl.when(k == 1)
        def _():
            e = h_ref[...] - t_ref[0, first:S, :]
            dy_ref[...] = e * (1.0 / D)
            l_ref[...] += jnp.sum(e * e, axis=0, keepdims=True) * (0.5 / D)

    return _call(
        body, name=name, grid=(B, 2),
        in_specs=[pl.BlockSpec((half, D), lambda b, k: (b * 2 + k, 0)),
                  pl.BlockSpec((1, S, D), lambda b, k: (b, 0, 0))],
        out_specs=[pl.BlockSpec((half, D), lambda b, k: (b * 2 + k, 0)),
                   pl.BlockSpec((1, D), lambda b, k: (0, 0))],
        out_shape=[jax.ShapeDtypeStruct(h3.shape, F32), jax.ShapeDtypeStruct((1, D), F32)],
        compiler_params=_cp(),
    )(h3, target)


def _rope_tables(B, LP):
    pos = jnp.concatenate([jnp.arange(MPAD, dtype=F32), NMETA + jnp.arange(LP - MPAD, dtype=F32)])
    inv_freq = ROPE_THETA ** (-jnp.arange(0, ROPE, 2, dtype=F32) / ROPE)
    ang = pos[:, None] * inv_freq[None, :]
    cos, sin = jnp.cos(ang), jnp.sin(ang)
    z32 = jnp.zeros((LP, 32), F32)
    rc = jnp.concatenate([jnp.ones((LP, 64), F32), cos, cos, z32], axis=1)
    rs = jnp.concatenate([jnp.zeros((LP, 64), F32), -sin, sin, z32], axis=1)
    return jnp.tile(rc, (B, 1)), jnp.tile(rs, (B, 1))


def _pad_lanes(v, start, width=128):
    n = v.shape[1]
    return jnp.concatenate([jnp.zeros((1, start), F32), v, jnp.zeros((1, width - start - n), F32)], axis=1)


G_FFN1 = ["ffn1_w_gu", "ffn1_w_down"]
G_MIX = ["w_in", "mla_w_uq", "mla_w_ukv", "w_branch_fox", "w_branch_mla", "w_out"]
G_OUT = ["w_out", "w_branch_fox", "w_branch_mla"]
G_IN = ["w_in", "mla_w_uq", "mla_w_ukv"]


def _step(x, target, meta, vec, gath, shards):
    dist = shards is not None
    B, S, _ = x.shape
    LP = MPAD + S
    T = B * LP
    gath = dict(gath)

    def gather(names, wide=()):
        return [_gather_task([shards[n] for n in names], wide)] if dist else []

    def flat_gu(w):
        return w if w.ndim == 2 else _cols_from_shards(w)

    def gathered(names, touts):
        if dist:
            gath.update(zip(names, touts[0]))

    g4, sums, red = {}, {}, {}

    def scatter(names):
        return [_a2a_task([_pieces(g4[n]) for n in names])] if dist else []

    def scattered(names, tout, me):
        for n, r in zip(names, tout):
            sums[n] = _sum_pieces(r, _pieces(g4[n]), me, "rs_sum_" + n)

    def join(names):
        return [_join_task([sums[n] for n in names])] if dist else []

    def joined(names, tout):
        for n, r in zip(names, tout):
            red[n] = (sums[n], r)

    me = None
    if dist:
        me = jnp.tile((4 * lax.axis_index("x") + 2 * lax.axis_index("y") + lax.axis_index("c")).reshape(1), 2).astype(jnp.int32)

    h0 = jnp.concatenate([jnp.broadcast_to(meta[None], (B, NMETA, D)),
                          jnp.zeros((B, MPAD - NMETA, D), F32), x], axis=1).reshape(T, D)
    rc, rs = _rope_tables(B, LP)
    gfq = jnp.tile(vec["fox_q_norm"], (1, 2))
    gfk = jnp.tile(vec["fox_k_norm"], (1, 2))
    gmq = _pad_lanes(vec["mla_q_norm"], 0)
    gmk = _pad_lanes(vec["mla_k_norm"], 0)
    bfv = _pad_lanes(vec["b_forget"], L_FL)

    w1gu, w1d = flat_gu(gath["ffn1_w_gu"]), gath["ffn1_w_down"].reshape(DFF, D)
    h1, touts = _ffn_fwd(h0, vec["ffn1_norm"], w1gu, w1d, "ffn1_fwd", gather(G_MIX))
    gathered(G_MIX, touts)
    wm = _mixer_weights(gath)
    small = (gfq, gfk, vec["mla_cq_norm"], vec["mla_ckv_norm"], gmq, gmk, bfv, wm["wuq"], wm["wuk"], wm["wuv"])
    proj, u2 = _inproj_fwd(h1, vec["mix_norm"], wm["w_in"], "inproj_fwd")
    fq, fk, fv, qm, km, vm, lf = _prep_fwd(proj, rc, rs, *small, name="prep_fwd")
    lf_rows = lf[:, L_FL:L_FL + NH].reshape(B, LP, NH).transpose(0, 2, 1).reshape(B * NH, LP)
    crow = _forget_scan(lf_rows, False, "forget_scan")
    ctok = crow.reshape(B, 4, 2, LP).transpose(0, 1, 3, 2).reshape(B * 4, LP, 2)
    bias = (ctok, crow.reshape(B * NH, 1, LP))
    of, touts = _attn_fwd(fq, fk, fv, bias, B, LP, False, 64 ** -0.5, "fox_fwd", gather(["ffn2_w_gu"], wide=(0,)))
    gathered(["ffn2_w_gu"], touts)
    om, touts = _attn_fwd(qm, km, vm, None, B, LP, True, MLA_QK ** -0.5, "mla_fwd", gather(["ffn2_w_down"]))
    gathered(["ffn2_w_down"], touts)
    h2, mix = _post_fwd(h1, of, om, proj, vec["b_gate"], wm["wbf"], wm["wbm"], wm["w_out"], "post_fwd")
    w2gu, w2d = flat_gu(gath["ffn2_w_gu"]), gath["ffn2_w_down"].reshape(DFF, D)
    h3, _ = _ffn_fwd(h2, vec["ffn2_norm"], w2gu, w2d, "ffn2_fwd")
    dy, lpart = _loss_head(h3, target, B, LP, "loss_head")

    gv = {}
    (dh2, u3, a2, dgp2, gv["ffn2_norm"]), _ = _ffn_bwd(h2, dy, vec["ffn2_norm"], w2gu, w2d, "ffn2_bwd")
    g4["ffn2_w_gu"] = _wgrad(u3, dgp2, "ffn2_dwgu", bn=DFF, shards=2)[0]
    g4["ffn2_w_down"] = _wgrad(a2, dy, "ffn2_dwd", scale=0.5, bk=FH, bt=2176)[0].reshape(N_CHIPS, DFF // N_CHIPS, D)

    dgl, dbf, dbm, dof, dom, gv["b_gate"] = _post_bwd(dh2, of, om, proj, vec["b_gate"], wm["wbf"], wm["wbm"], wm["w_out"], "post_bwd")
    g4["w_out"] = _wgrad(mix, dh2, "dw_out", bt=2176)[0].reshape(N_CHIPS, D // N_CHIPS, D)
    g4["w_branch_fox"] = _cols_to_shards(_wgrad(of, dbf, "dw_bf", bt=2176)[0])
    g4["w_branch_mla"] = _cols_to_shards(_wgrad(om, dbm, "dw_bm", bt=2176)[0])
    G_FFN2 = ["ffn2_w_gu", "ffn2_w_down"]
    (dfq, dfk, dfv, dc0, dc1), touts = _attn_bwd(fq, fk, fv, dof, bias, B, LP, False, 64 ** -0.5, "fox_bwd", scatter(G_FFN2))
    if dist:
        scattered(G_FFN2, touts[0], me)
    (dqm, dkm, dvm), touts = _attn_bwd(qm, km, vm, dom, None, B, LP, True, MLA_QK ** -0.5, "mla_bwd",
                                       scatter(G_OUT) + join(G_FFN2))
    if dist:
        scattered(G_OUT, touts[0], me)
        joined(G_FFN2, touts[1])
    dc = jnp.concatenate([dc0, dc1], axis=1).reshape(B * NH, LP)
    dlf_rows = _forget_scan(dc, True, "forget_scan_bwd")
    dlf = dlf_rows.reshape(B, NH, LP).transpose(0, 2, 1).reshape(T, NH)
    dlf = jnp.concatenate([jnp.zeros((T, L_FL), F32), dlf, jnp.zeros((T, 128 - L_FL - NH), F32)], axis=1)
    (dlo, dgfq, dgfk, gv["mla_cq_norm"], gv["mla_ckv_norm"], dgmq, dgmk, dbfv,
     dwuq, dwuk, dwuv) = _prep_bwd(proj, rc, rs, *small, dfq, dfk, dfv, dqm, dkm, dvm, dlf, name="prep_bwd")
    gv["fox_q_norm"] = dgfq[:, :64] + dgfq[:, 64:]
    gv["fox_k_norm"] = dgfk[:, :64] + dgfk[:, 64:]
    gv["mla_q_norm"] = dgmq[:, :MLA_QK]
    gv["mla_k_norm"] = dgmk[:, :MLA_QK]
    gv["b_forget"] = dbfv[:, L_FL:L_FL + NH]
    dwin = jnp.concatenate([_wgrad(u2, dlo, "dw_in_lo")[0], _wgrad(u2, dgl, "dw_in_hi")[0]], axis=1)
    g4["w_in"] = _cols_to_shards(_win_from_kernel(dwin))
    g4["mla_w_uq"] = _cols_to_shards(
        dwuq.astype(GRAD_DTYPE).reshape(QR, NH, 128)[:, :, :MLA_QK].reshape(QR, NH * MLA_QK))
    dukv = jnp.concatenate([dwuk.reshape(KVR, NH, 128)[:, :, :64], dwuv.reshape(KVR, NH, 64)], axis=2)
    g4["mla_w_ukv"] = _cols_to_shards(dukv.astype(GRAD_DTYPE).reshape(KVR, NH * 128))
    dh1, gv["mix_norm"] = _inproj_bwd(h1, dh2, dlo, dgl, vec["mix_norm"], wm["w_in"], "inproj_bwd")

    (dh0, u1, a1, dgp1, gv["ffn1_norm"]), touts = _ffn_bwd(h0, dh1, vec["ffn1_norm"], w1gu, w1d, "ffn1_bwd",
                                                            scatter(G_IN) + join(G_OUT))
    if dist:
        scattered(G_IN, touts[0], me)
        joined(G_OUT, touts[1])
    dh0 = dh0.reshape(B, LP, D)
    grad_x = dh0[:, MPAD:]
    grad_meta = jnp.sum(dh0[:, :NMETA], axis=0)
    share = [_share_task([_stack_vectors([gv[n] for n in VECS]), grad_meta, lpart])] if dist else []
    dwd1, touts = _wgrad(a1, dh1, "ffn1_dwd", scale=0.5, bk=FH, tasks=share + join(G_IN))
    g4["ffn1_w_down"] = dwd1.reshape(N_CHIPS, DFF // N_CHIPS, D)
    shared = touts[0] if dist else None
    if dist:
        joined(G_IN, touts[1])
    half = D // 2
    gu = []
    for cr in (0, 1):
        tasks = (scatter(["ffn1_w_down"]) if cr == 0 else [_a2a_core_task([gu[0]], 0)]) if dist else []
        part, touts = _wgrad(u1, dgp1, "ffn1_dwgu_%d" % cr, bk=half, bn=DFF, shards=2, rows=(cr, 1), tasks=tasks)
        gu.append(part)
        if dist and cr == 0:
            scattered(["ffn1_w_down"], touts[0], me)
        elif dist:
            got0 = touts[0][0]
    g4["ffn1_w_gu"] = jnp.concatenate(gu, axis=1)
    if dist:
        got1 = _run_tasks([_a2a_core_task([gu[1]], 1)], "rs_ffn1_w_gu")[0][0]
        core = lax.axis_index("c")
        mine = me.at[1].set(2 * lax.axis_index("x") + lax.axis_index("y"))
        sums["ffn1_w_gu"] = _sum_pieces(jnp.where(core == 0, got0, got1), jnp.where(core == 0, gu[0], gu[1]),
                                        mine, "rs_sum_ffn1_w_gu")
        joined(G_FFN1, _run_tasks(join(G_FFN1), "rs_join_ffn1")[0])
    return lpart, grad_x, grad_meta, gv, (red if dist else g4), shared


def _cols_from_shards(g4):
    n, r, c = g4.shape
    return g4.transpose(1, 0, 2).reshape(r, n * c)


def _cols_to_shards(full):
    r, c4 = full.shape
    return full.reshape(r, N_CHIPS, c4 // N_CHIPS).transpose(1, 0, 2)


def _win_to_kernel(wfull):
    z = lambda n: jnp.zeros((D, n), wfull.dtype)
    fl, cq, ckv, kr, gate = (wfull[:, 1536:1544], wfull[:, 1544:1800], wfull[:, 1800:1928],
                             wfull[:, 1928:1960], wfull[:, 1960:4008])
    misc = jnp.concatenate([z(L_KR), kr, fl, z(128 - L_FL - NH)], axis=1)
    return jnp.concatenate([wfull[:, :1536], cq, ckv, misc, gate], axis=1)


def _win_from_kernel(gk):
    m = C_MISC
    return jnp.concatenate([gk[:, :1536], gk[:, m + L_FL:m + L_FL + NH], gk[:, C_CQ:C_CQ + QR],
                            gk[:, C_CKV:C_CKV + KVR], gk[:, m + L_KR:m + L_KR + ROPE], gk[:, C_GATE:]], axis=1)


def _pieces(g4):
    n, r, c = g4.shape
    return g4.reshape(2 * n, r // 2, c)


def _mixer_weights(gath):
    w = {}
    w["w_in"] = _win_to_kernel(_cols_from_shards(gath["w_in"]))
    uq = _cols_from_shards(gath["mla_w_uq"]).reshape(QR, NH, MLA_QK)
    w["wuq"] = jnp.pad(uq, ((0, 0), (0, 0), (0, 128 - MLA_QK))).reshape(QR, NH * 128)
    ukv = _cols_from_shards(gath["mla_w_ukv"]).reshape(KVR, NH, 128)
    w["wuk"] = jnp.pad(ukv[:, :, :64], ((0, 0), (0, 0), (0, 64))).reshape(KVR, NH * 128)
    w["wuv"] = ukv[:, :, 64:].reshape(KVR, NH * 64)
    w["wbf"] = _cols_from_shards(gath["w_branch_fox"])
    w["wbm"] = _cols_from_shards(gath["w_branch_mla"])
    w["w_out"] = gath["w_out"].reshape(D, D)
    return w


def _chip_peers(x, y):
    return [(1 - x, y), (x, 1 - y), (1 - x, 1 - y)]


RELS = [(dx, dy, dc) for dx in (0, 1) for dy in (0, 1) for dc in (0, 1)][1:]


def _here():
    return lax.axis_index("x"), lax.axis_index("y"), lax.axis_index("c")


def _flip(a, d):
    return (1 - a) if d else a


def _remote(src, dst, send, recv, i, dev):
    return functools.partial(pltpu.make_async_remote_copy, src_ref=src, dst_ref=dst, send_sem=send.at[i],
                             recv_sem=recv.at[i], device_id=dev, device_id_type=MESH)


def _gathered_shape(s, wide):
    return jax.ShapeDtypeStruct((s.shape[0], N_CHIPS * s.shape[1]) if wide else (N_CHIPS,) + s.shape, s.dtype)


def _slot(ref, j, shard, wide, rows=None):
    if wide:
        lanes = pl.ds(pl.multiple_of(j * shard.shape[1], 128), shard.shape[1])
        return ref.at[slice(None) if rows is None else rows, lanes]
    return ref.at[j] if rows is None else ref.at[j, rows]


def _gather_task(shards, wide=()):
    n = len(shards)

    def descs(ins, outs, sems):
        send, recv, loc = sems
        x, y, c = _here()
        j = 2 * x + y
        locs, pairs = [], []
        for k in range(n):
            at = functools.partial(_slot, outs[k], shard=shards[k], wide=k in wide)
            locs.append(functools.partial(pltpu.make_async_copy, ins[k], at(j), loc.at[k]))
            for r, (px, py) in enumerate(_chip_peers(x, y)):
                dev = (px, py, c)
                pairs.append((_remote(ins[k], at(j), send, recv, 3 * k + r, dev),
                              _remote(ins[k], at(2 * px + py), send, recv, 3 * k + r, dev)))
        return locs, pairs

    return _Task(shards, [_gathered_shape(s, k in wide) for k, s in enumerate(shards)],
                 [pltpu.SemaphoreType.DMA((3 * n,)), pltpu.SemaphoreType.DMA((3 * n,)), pltpu.SemaphoreType.DMA((n,))],
                 descs)


class _SplitGather(_Task):
    PARTS = 2

    def __init__(self, shards, wide=()):
        n = 3 * len(shards) * self.PARTS
        dma = pltpu.SemaphoreType.DMA
        self.wide = wide
        super().__init__(shards, [_gathered_shape(s, k in wide) for k, s in enumerate(shards)],
                         [dma((n,)), dma((n,)), dma((n,)), dma((n,)), dma((len(shards),))], None)

    def _plan(self, ins, outs, sems):
        send, recv, fsend, frecv, loc = sems
        x, y, c = _here()
        j = 2 * x + y
        locs, first, passed = [], [], []
        for k in range(len(ins)):
            h = self.ins[k].shape[0] // 2
            parts = self.PARTS if h % (32 * self.PARTS) == 0 else 1
            hp = h // parts
            at = functools.partial(_slot, outs[k], shard=self.ins[k], wide=k in self.wide)
            locs.append(functools.partial(pltpu.make_async_copy, ins[k], at(j), loc.at[k]))
            for r, (px, py) in enumerate(_chip_peers(x, y)):
                p = 2 * px + py
                for q in range(parts):
                    i = (3 * k + r) * self.PARTS + q
                    mine = pl.ds(pl.multiple_of(c * h + q * hp, 8), hp)
                    theirs = pl.ds(pl.multiple_of((1 - c) * h + q * hp, 8), hp)
                    first.append((_remote(ins[k].at[mine], at(j, rows=mine), send, recv, i, (px, py, c)),
                                  _remote(ins[k].at[mine], at(p, rows=mine), send, recv, i, (px, py, c))))
                    passed.append((_remote(at(p, rows=mine), at(p, rows=mine), fsend, frecv, i, (x, y, 1 - c)),
                                   _remote(at(p, rows=mine), at(p, rows=theirs), fsend, frecv, i, (x, y, 1 - c))))
        return locs, first, passed

    def start(self, ins, outs, sems):
        locs, first, _ = self._plan(ins, outs, sems)
        for lc in locs:
            lc().start()
        for snd, _ in first:
            snd().start()

    def wait(self, ins, outs, sems):
        locs, first, passed = self._plan(ins, outs, sems)
        for (_, landed), (pass_on, _) in zip(first, passed):
            landed().wait_recv()
            pass_on().start()
        for _, rcv in passed:
            rcv().wait_recv()
        for snd, _ in first + passed:
            snd().wait_send()
        for lc in locs:
            lc().wait()


def _a2a_task(ps):
    n = len(ps)
    nr = len(RELS)

    def descs(ins, outs, sems):
        send, recv = sems
        x, y, c = _here()
        me = 4 * x + 2 * y + c
        pairs = []
        for k in range(n):
            for i, (dx, dy, dc) in enumerate(RELS):
                dev = (_flip(x, dx), _flip(y, dy), _flip(c, dc))
                peer = 4 * dev[0] + 2 * dev[1] + dev[2]
                pairs.append((_remote(ins[k].at[peer], outs[k].at[me], send, recv, nr * k + i, dev),
                              _remote(ins[k].at[peer], outs[k].at[peer], send, recv, nr * k + i, dev)))
        return [], pairs

    return _Task(ps, [jax.ShapeDtypeStruct(p.shape, p.dtype) for p in ps],
                 [pltpu.SemaphoreType.DMA((nr * n,)), pltpu.SemaphoreType.DMA((nr * n,))], descs)


def _a2a_core_task(ps, core):
    n = len(ps)
    nr = len(RELS)

    def descs(ins, outs, sems):
        send, recv = sems
        x, y, c = _here()
        me = 4 * x + 2 * y + c
        pairs = []
        for k in range(n):
            for i, (dx, dy, dc) in enumerate(RELS):
                dev = (_flip(x, dx), _flip(y, dy), _flip(c, dc))
                peer = 4 * dev[0] + 2 * dev[1] + dev[2]
                pairs.append((_remote(ins[k].at[2 * dev[0] + dev[1]], outs[k].at[me], send, recv, nr * k + i, dev),
                              _remote(ins[k].at[2 * dev[0] + dev[1]], outs[k].at[peer], send, recv, nr * k + i, dev),
                              dev[2] == core, c == core))
        return [], pairs

    return _Task(ps, [jax.ShapeDtypeStruct((N_DEV,) + p.shape[1:], p.dtype) for p in ps],
                 [pltpu.SemaphoreType.DMA((nr * n,)), pltpu.SemaphoreType.DMA((nr * n,))], descs)


def _share_task(vs):
    n = len(vs)
    nr = len(RELS)

    def descs(ins, outs, sems):
        send, recv, loc = sems
        x, y, c = _here()
        me = 4 * x + 2 * y + c
        locs, pairs = [], []
        for k in range(n):
            locs.append(functools.partial(pltpu.make_async_copy, ins[k], outs[k].at[me], loc.at[k]))
            for i, (dx, dy, dc) in enumerate(RELS):
                dev = (_flip(x, dx), _flip(y, dy), _flip(c, dc))
                peer = 4 * dev[0] + 2 * dev[1] + dev[2]
                pairs.append((_remote(ins[k], outs[k].at[me], send, recv, nr * k + i, dev),
                              _remote(ins[k], outs[k].at[peer], send, recv, nr * k + i, dev)))
        return locs, pairs

    return _Task(vs, [jax.ShapeDtypeStruct((N_DEV,) + v.shape, v.dtype) for v in vs],
                 [pltpu.SemaphoreType.DMA((nr * n,)), pltpu.SemaphoreType.DMA((nr * n,)), pltpu.SemaphoreType.DMA((n,))],
                 descs)


def _join_task(ss):
    n = len(ss)

    def descs(ins, outs, sems):
        send, recv = sems
        x, y, c = _here()
        pairs = []
        for k in range(n):
            cp = _remote(ins[k], outs[k], send, recv, k, (x, y, 1 - c))
            pairs.append((cp, cp))
        return [], pairs

    return _Task(ss, [jax.ShapeDtypeStruct(s.shape, s.dtype) for s in ss],
                 [pltpu.SemaphoreType.DMA((n,)), pltpu.SemaphoreType.DMA((n,))], descs)


def _sum_pieces(recv, own, me, name):
    n, h, c = recv.shape
    tr = h

    def body(me_ref, r_ref, o_ref, out_ref):
        s = pl.program_id(1)
        val = jnp.where(s == me_ref[0], o_ref[0], r_ref[0]).astype(F32)

        @pl.when(s == 0)
        def _():
            out_ref[...] = val

        @pl.when(s > 0)
        def _():
            out_ref[...] += val

    def other(s, m):
        return jnp.where(s == m[0], (s + 1) % n, s)

    return _call(
        body, name=name,
        grid_spec=pltpu.PrefetchScalarGridSpec(
            num_scalar_prefetch=1, grid=(h // tr, n),
            in_specs=[pl.BlockSpec((1, tr, c), lambda i, s, m: (other(s, m), i, 0)),
                      pl.BlockSpec((1, tr, c), lambda i, s, m: (m[1], i, 0))],
            out_specs=pl.BlockSpec((tr, c), lambda i, s, m: (i, 0))),
        out_shape=jax.ShapeDtypeStruct((h, c), F32),
        compiler_params=_cp(),
    )(me, recv, own)


def _adamw_update(gg, w, m, v):
    c1 = 1.0 / (1.0 - ADAM_B1 ** ADAM_STEP)
    c2 = 1.0 / (1.0 - ADAM_B2 ** ADAM_STEP)
    nm = ADAM_B1 * m + (1.0 - ADAM_B1) * gg
    nv = ADAM_B2 * v + (1.0 - ADAM_B2) * (gg * gg)
    return -ADAM_LR * ((nm * c1) / (jnp.sqrt(nv * c2) + ADAM_EPS) + ADAM_WD * w), nm, nv


def _adamw_small(gvec8, gmeta8, lp8, chip, ws, ms, vs, name):
    na = len(ws)

    def dev_sum(ref):
        acc = ref[0]
        for s in range(1, N_DEV):
            acc = acc + ref[s]
        return acc

    def body(c_ref, gv_ref, gm_ref, lp_ref, *refs):
        w_refs, m_refs, v_refs = refs[:na], refs[na:2 * na], refs[2 * na:3 * na]
        l_ref = refs[3 * na]
        outs = refs[3 * na + 1:]
        g_refs, d_refs, nm_refs, nv_refs = outs[:na], outs[na:2 * na], outs[2 * na:3 * na], outs[3 * na:]
        l_ref[...] = dev_sum(lp_ref)
        gvec = dev_sum(gv_ref)
        for k in range(na):
            gg = gvec[k:k + 1, 0:ws[k].shape[1]] if k < na - 1 else dev_sum(gm_ref)
            g_refs[k][...] = gg
            d_refs[k][...], nm_refs[k][...], nv_refs[k][...] = _adamw_update(gg, w_refs[k][...], m_refs[k][...], v_refs[k][...])

    whole = lambda a: pl.BlockSpec(a.shape, lambda i, c: (0,) * a.ndim)
    arrs = list(ws) + list(ms) + list(vs)
    res = _call(
        body, name=name,
        grid_spec=pltpu.PrefetchScalarGridSpec(
            num_scalar_prefetch=1, grid=(1,),
            in_specs=[whole(gvec8), pl.BlockSpec((N_DEV, NMETA, D // N_CHIPS), lambda i, c: (0, 0, c[0])), whole(lp8)]
                     + [whole(a) for a in arrs],
            out_specs=[pl.BlockSpec((1, D), lambda i, c: (0, 0))] + [whole(a) for a in ws] * 4),
        out_shape=[jax.ShapeDtypeStruct((1, D), F32)] + [jax.ShapeDtypeStruct(a.shape, F32) for a in ws] * 4,
        compiler_params=_cp(),
    )(chip, gvec8, gmeta8, lp8, *arrs)
    return res[0], [list(res[1 + i * na:1 + (i + 1) * na]) for i in range(4)]


def _adamw_halves(wt, mine, theirs, m, v, core, name):
    r, c = wt.shape
    h = r // 2
    tr = _tile(h, (256, 176, 128, 64))
    nh = h // tr

    def body(c_ref, w_ref, a_ref, b_ref, m_ref, v_ref, g_ref, d_ref, nm_ref, nv_ref):
        gg = jnp.where(pl.program_id(0) // nh == c_ref[0], a_ref[...], b_ref[...])
        g_ref[...] = gg
        d_ref[...], nm_ref[...], nv_ref[...] = _adamw_update(gg, w_ref[...], m_ref[...], v_ref[...])

    full = pl.BlockSpec((tr, c), lambda i, cr: (i, 0))
    half = pl.BlockSpec((tr, c), lambda i, cr: (i % nh, 0))
    return _call(
        body, name=name,
        grid_spec=pltpu.PrefetchScalarGridSpec(
            num_scalar_prefetch=1, grid=(2 * nh,),
            in_specs=[full, half, half, full, full], out_specs=[full] * 4),
        out_shape=[jax.ShapeDtypeStruct((r, c), F32)] * 4,
        compiler_params=_cp(),
    )(core, wt, mine, theirs, m, v)


MATS = ["ffn1_w_gu", "ffn1_w_down", "w_in", "mla_w_uq", "mla_w_ukv", "w_branch_fox", "w_branch_mla",
        "w_out", "ffn2_w_gu", "ffn2_w_down"]
VECS = ["ffn1_norm", "mix_norm", "b_forget", "b_gate", "fox_q_norm", "fox_k_norm", "mla_cq_norm",
        "mla_ckv_norm", "mla_q_norm", "mla_k_norm", "ffn2_norm"]
WEIGHTS = ["meta_tokens", "ffn1_norm", "ffn1_w_gu", "ffn1_w_down", "mix_norm", "w_in", "b_forget", "b_gate",
           "fox_q_norm", "fox_k_norm", "mla_cq_norm", "mla_w_uq", "mla_ckv_norm", "mla_w_ukv", "mla_q_norm",
           "mla_k_norm", "w_branch_fox", "w_branch_mla", "w_out", "ffn2_norm", "ffn2_w_gu", "ffn2_w_down"]


VEC_LANES = 2048


def _stack_vectors(parts):
    rows = [_pad_lanes(p, 0, VEC_LANES) for p in parts]
    rows.append(jnp.zeros((-len(parts) % 8, VEC_LANES), F32))
    return jnp.concatenate(rows, axis=0)


def kernel(x, meta_tokens, ffn1_norm, ffn1_w_gu, ffn1_w_down, mix_norm, w_in, b_forget, b_gate, fox_q_norm, fox_k_norm, mla_cq_norm, mla_w_uq, mla_ckv_norm, mla_w_ukv, mla_q_norm, mla_k_norm, w_branch_fox, w_branch_mla, w_out, ffn2_norm, ffn2_w_gu, ffn2_w_down, loss_target, m_meta_tokens, m_ffn1_norm, m_ffn1_w_gu, m_ffn1_w_down, m_mix_norm, m_w_in, m_b_forget, m_b_gate, m_fox_q_norm, m_fox_k_norm, m_mla_cq_norm, m_mla_w_uq, m_mla_ckv_norm, m_mla_w_ukv, m_mla_q_norm, m_mla_k_norm, m_w_branch_fox, m_w_branch_mla, m_w_out, m_ffn2_norm, m_ffn2_w_gu, m_ffn2_w_down, v_meta_tokens, v_ffn1_norm, v_ffn1_w_gu, v_ffn1_w_down, v_mix_norm, v_w_in, v_b_forget, v_b_gate, v_fox_q_norm, v_fox_k_norm, v_mla_cq_norm, v_mla_w_uq, v_mla_ckv_norm, v_mla_w_ukv, v_mla_q_norm, v_mla_k_norm, v_w_branch_fox, v_w_branch_mla, v_w_out, v_ffn2_norm, v_ffn2_w_gu, v_ffn2_w_down):
    a = dict(locals())
    wts = {n: a[n] for n in WEIGHTS}
    ms = {n: a["m_" + n] for n in WEIGHTS}
    vs = {n: a["v_" + n] for n in WEIGHTS}
    cx, cy, cc = lax.axis_index("x"), lax.axis_index("y"), lax.axis_index("c")
    chip = 2 * cx + cy

    shards = {n: wts[n][0].astype(BF16) for n in MATS}
    first = _run_tasks([_SplitGather([shards[n] for n in G_FFN1] + [meta_tokens], wide=(0,))], "gather_ffn1")[0]
    gath = dict(zip(G_FFN1, first[:-1]))
    meta_full = _cols_from_shards(first[-1])

    _, grad_x, _, _, gred, (gvec8, gmeta8, lp8) = _step(x, loss_target, meta_full, {n: wts[n] for n in VECS}, gath, shards)

    sm_names = VECS + ["meta_tokens"]
    lsum, sm = _adamw_small(gvec8, gmeta8, lp8, chip.reshape(1).astype(jnp.int32), [wts[n] for n in sm_names],
                            [ms[n] for n in sm_names], [vs[n] for n in sm_names], "adamw_small")
    loss = jnp.sum(lsum)

    grads, delta, new_m, new_v = {}, {}, {}, {}
    core = cc.reshape(1).astype(jnp.int32)
    for n in MATS:
        shp = wts[n].shape
        mine, theirs = gred[n]
        res = _adamw_halves(wts[n][0], mine, theirs, ms[n][0], vs[n][0], core, "adamw_" + n)
        grads[n], delta[n], new_m[n], new_v[n] = (t.reshape(shp) for t in res)
    for k, n in enumerate(sm_names):
        grads[n], delta[n], new_m[n], new_v[n] = (sm[i][k] for i in range(4))

    return (loss, grad_x, *[grads[n] for n in WEIGHTS], *[delta[n] for n in WEIGHTS],
            *[new_m[n] for n in WEIGHTS], *[new_v[n] for n in WEIGHTS])
```

```python
import functools

import jax
import jax.numpy as jnp
from jax import lax
from jax.experimental import pallas as pl
from jax.experimental.pallas import tpu as pltpu

F32 = jnp.float32
BF16 = jnp.bfloat16
MESH = pl.DeviceIdType.MESH

D = 1024
DFF = 2816
FH = DFF // 2
NMETA = 16
MPAD = 128
EPS = 1e-6
NH = 8
FOXW = 512
QR = 256
KVR = 128
ROPE = 32
MLA_QK = 96
PROJW = 4096
ROPE_THETA = 10000.0
N_CHIPS = 4
N_DEV = 8

ADAM_LR = 0.001
ADAM_B1 = 0.9
ADAM_B2 = 0.999
ADAM_EPS = 1e-08
ADAM_WD = 0.01
ADAM_STEP = 10

VMEM_LIMIT = 56 * 2**20
ATTN_TQ = 256
GRAD_DTYPE = BF16

NT = (((1,), (1,)), ((), ()))
TN = (((0,), (0,)), ((), ()))


def _call(body, **kw):
    return pl.pallas_call(body, **kw)


def _cp(**kw):
    return pltpu.CompilerParams(vmem_limit_bytes=VMEM_LIMIT, **kw)


HBM = pl.BlockSpec(memory_space=pltpu.HBM)


class _Task:
    def __init__(self, ins, out_shapes, sems, descs):
        self.ins, self.out_shapes, self.sems, self.descs = list(ins), list(out_shapes), list(sems), descs

    @staticmethod
    def _if(cond, action):
        if cond is None:
            action()
        else:
            pl.when(cond)(action)

    def start(self, ins, outs, sems):
        locs, pairs = self.descs(ins, outs, sems)
        for lc in locs:
            lc().start()
        for snd, _, *ok in pairs:
            self._if(ok[0] if ok else None, lambda snd=snd: snd().start())

    def wait(self, ins, outs, sems):
        locs, pairs = self.descs(ins, outs, sems)
        for _, rcv, *ok in pairs:
            self._if(ok[1] if ok else None, lambda rcv=rcv: rcv().wait_recv())
        for snd, _, *ok in pairs:
            self._if(ok[0] if ok else None, lambda snd=snd: snd().wait_send())
        for lc in locs:
            lc().wait()


def _call_tasks(body, tasks, *, name, grid, in_specs, out_specs, out_shape, args, scratch_shapes=()):
    in_specs, out_specs, out_shape, scratch_shapes = map(list, (in_specs, out_specs, out_shape, scratch_shapes))
    n_in, n_out, n_sc = len(in_specs), len(out_specs), len(scratch_shapes)
    t_in = [len(t.ins) for t in tasks]
    t_out = [len(t.out_shapes) for t in tasks]
    t_sem = [len(t.sems) for t in tasks]

    def wrapped(*refs):
        pos = [0]

        def take(n):
            pos[0] += n
            return refs[pos[0] - n:pos[0]]

        ins, tins = take(n_in), [take(n) for n in t_in]
        outs, touts = take(n_out), [take(n) for n in t_out]
        sc, tsems = take(n_sc), [take(n) for n in t_sem]
        if tasks:
            first = functools.reduce(jnp.logical_and, [pl.program_id(a) == 0 for a in range(len(grid))])
            last = functools.reduce(jnp.logical_and, [pl.program_id(a) == grid[a] - 1 for a in range(len(grid))])

            @pl.when(first)
            def _():
                for t, a, b, s in zip(tasks, tins, touts, tsems):
                    t.start(a, b, s)

        body(*ins, *outs, *sc)
        if tasks:
            @pl.when(last)
            def _():
                for t, a, b, s in zip(tasks, tins, touts, tsems):
                    t.wait(a, b, s)

    res = _call(
        wrapped, name=name, grid=grid,
        in_specs=in_specs + [HBM] * sum(t_in), out_specs=out_specs + [HBM] * sum(t_out),
        out_shape=out_shape + [s for t in tasks for s in t.out_shapes],
        scratch_shapes=scratch_shapes + [s for t in tasks for s in t.sems],
        compiler_params=_cp(),
    )(*args, *[a for t in tasks for a in t.ins])
    res = list(res)
    touts, pos = [], n_out
    for n in t_out:
        touts.append(res[pos:pos + n])
        pos += n
    return res[:n_out], touts


def _run_tasks(tasks, name):
    t_in = [len(t.ins) for t in tasks]
    t_out = [len(t.out_shapes) for t in tasks]
    t_sem = [len(t.sems) for t in tasks]

    def body(*refs):
        pos = [0]

        def take(n):
            pos[0] += n
            return refs[pos[0] - n:pos[0]]

        tins, touts, tsems = [take(n) for n in t_in], [take(n) for n in t_out], [take(n) for n in t_sem]
        for t, a, b, s in zip(tasks, tins, touts, tsems):
            t.start(a, b, s)
        for t, a, b, s in zip(tasks, tins, touts, tsems):
            t.wait(a, b, s)

    res = list(_call(
        body, name=name, in_specs=[HBM] * sum(t_in), out_specs=[HBM] * sum(t_out),
        out_shape=[s for t in tasks for s in t.out_shapes],
        scratch_shapes=[s for t in tasks for s in t.sems],
    )(*[a for t in tasks for a in t.ins]))
    touts, pos = [], 0
    for n in t_out:
        touts.append(res[pos:pos + n])
        pos += n
    return touts


def _tile(n, cands):
    for c in cands:
        if n % c == 0:
            return c
    raise ValueError(f"no tile for {n} among {cands}")


def _dot(a, b, dims=None):
    if dims is None:
        return jnp.dot(a, b, preferred_element_type=F32)
    return lax.dot_general(a, b, dims, preferred_element_type=F32)


def _lane(shape):
    return lax.broadcasted_iota(jnp.int32, shape, len(shape) - 1)


def _seg_ones(w, log2_seg):
    r = lax.shift_right_logical(lax.broadcasted_iota(jnp.int32, (w, w), 0), log2_seg)
    c = lax.shift_right_logical(lax.broadcasted_iota(jnp.int32, (w, w), 1), log2_seg)
    return jnp.where(r == c, 1.0, 0.0).astype(BF16)


def _seg_sum(x, ones):
    hi = x.astype(BF16)
    r1 = x - hi.astype(F32)
    mid = r1.astype(BF16)
    lo = (r1 - mid.astype(F32)).astype(BF16)
    return _dot(hi, ones) + _dot(mid, ones) + _dot(lo, ones)


def _lane_sum(x, ones):
    return jnp.sum(x, axis=-1, keepdims=True) if ones is None else _seg_sum(x, ones)


def _rms(x, gain, n, ones=None):
    r = lax.rsqrt(_lane_sum(x * x, ones) * (1.0 / n) + EPS)
    xh = x * r
    return xh * gain, xh, r


def _rms_bwd(dy, xh, r, gain, n, ones=None):
    dxh = dy * gain
    return r * (dxh - xh * (_lane_sum(dxh * xh, ones) * (1.0 / n)))


def _rope_swap(x):
    ln = _lane(x.shape)
    sw = jnp.where(ln < 80, pltpu.roll(x, 112, 1), pltpu.roll(x, 16, 1))
    return jnp.where(jnp.logical_and(ln >= 64, ln < 96), sw, 0.0)


def _colsum(x):
    return jnp.sum(x, axis=0, keepdims=True)


def _ffn_weight_specs():
    once = pl.Buffered(1)
    return [pl.BlockSpec((D, DFF), lambda i: (0, 0), pipeline_mode=once),
            pl.BlockSpec((D, DFF), lambda i: (0, 1), pipeline_mode=once),
            pl.BlockSpec((DFF, D), lambda i: (0, 0), pipeline_mode=once)]


def _ffn_fwd(h, norm, wgu, wd, name, tasks=()):
    T = h.shape[0]
    tm = _tile(T, (512, 384, 256, 128))

    def body(h_ref, n_ref, wg_ref, wu_ref, wd_ref, o_ref):
        x = h_ref[...]
        u, _, _ = _rms(x, n_ref[...], D)
        ub = u.astype(BF16)
        g = _dot(ub, wg_ref[...])
        p = _dot(ub, wu_ref[...])
        a = (g * jax.nn.sigmoid(g)) * p
        o_ref[...] = x + 0.5 * _dot(a.astype(BF16), wd_ref[...])

    (out,), touts = _call_tasks(
        body, tasks, name=name, grid=(T // tm,),
        in_specs=[pl.BlockSpec((tm, D), lambda i: (i, 0)), pl.BlockSpec((1, D), lambda i: (0, 0))] + _ffn_weight_specs(),
        out_specs=[pl.BlockSpec((tm, D), lambda i: (i, 0))],
        out_shape=[jax.ShapeDtypeStruct((T, D), F32)],
        args=(h, norm, wgu, wgu, wd))
    return out, touts


def _ffn_bwd(h, dout, norm, wgu, wd, name, tasks=()):
    T = h.shape[0]
    tm = _tile(T, (256, 128))

    def body(h_ref, d_ref, n_ref, wg_ref, wu_ref, wd_ref, dh_ref, u_ref, a_ref, dgp_ref, dn_ref):
        @pl.when(pl.program_id(0) == 0)
        def _():
            dn_ref[...] = jnp.zeros_like(dn_ref)

        u, xh, r = _rms(h_ref[...], n_ref[...], D)
        ub = u.astype(BF16)
        u_ref[...] = ub
        g = _dot(ub, wg_ref[...])
        p = _dot(ub, wu_ref[...])
        s = jax.nn.sigmoid(g)
        sl = g * s
        dz = (0.5 * d_ref[...]).astype(BF16)
        da = _dot(dz, wd_ref[...], NT)
        dp = da * sl
        dg = (da * p) * (s * (1.0 + g * (1.0 - s)))
        a_ref[...] = (sl * p).astype(BF16)
        dgb = dg.astype(BF16)
        dpb = dp.astype(BF16)
        dgp_ref[:, :DFF] = dgb
        dgp_ref[:, DFF:] = dpb
        du = _dot(dgb, wg_ref[...], NT) + _dot(dpb, wu_ref[...], NT)
        dn_ref[...] += _colsum(du * xh)
        dh_ref[...] = d_ref[...] + _rms_bwd(du, xh, r, n_ref[...], D)

    row = lambda w: pl.BlockSpec((tm, w), lambda i: (i, 0))
    return _call_tasks(
        body, tasks, name=name, grid=(T // tm,),
        in_specs=[row(D), row(D), pl.BlockSpec((1, D), lambda i: (0, 0))] + _ffn_weight_specs(),
        out_specs=[row(D), row(D), row(DFF), row(2 * DFF), pl.BlockSpec((1, D), lambda i: (0, 0))],
        out_shape=[jax.ShapeDtypeStruct((T, D), F32),
                   jax.ShapeDtypeStruct((T, D), BF16),
                   jax.ShapeDtypeStruct((T, DFF), BF16),
                   jax.ShapeDtypeStruct((T, 2 * DFF), BF16),
                   jax.ShapeDtypeStruct((1, D), F32)],
        args=(h, dout, norm, wgu, wgu, wd))


def _wgrad(x, y, name, scale=1.0, bk=None, bn=None, shards=0, bt=512, cols=None, tasks=()):
    T, K = x.shape
    bk = bk or K
    bn = bn or y.shape[1]
    j0, nj = cols or (0, y.shape[1] // bn)
    N = nj * bn
    bt = _tile(T, (bt, 512, 384, 256, 128))
    nt = T // bt

    def body(x_ref, y_ref, o_ref, acc_ref):
        t = pl.program_id(2)

        @pl.when(t == 0)
        def _():
            acc_ref[...] = jnp.zeros_like(acc_ref)

        acc_ref[...] += _dot(x_ref[...].astype(BF16), y_ref[...].astype(BF16), TN)

        @pl.when(t == nt - 1)
        def _():
            res = (acc_ref[...] * scale).astype(o_ref.dtype)
            if shards:
                w = bn // shards
                for s in range(shards):
                    o_ref[s] = res[:, s * w:(s + 1) * w]
            else:
                o_ref[...] = res

    if shards:
        out_spec = pl.BlockSpec((shards, bk, bn // shards), lambda i, j, t: (j, i, 0))
        out_shape = jax.ShapeDtypeStruct((N * shards // bn, K, bn // shards), GRAD_DTYPE)
    else:
        out_spec = pl.BlockSpec((bk, bn), lambda i, j, t: (i, j))
        out_shape = jax.ShapeDtypeStruct((K, N), GRAD_DTYPE)
    (out,), touts = _call_tasks(
        body, tasks, name=name, grid=(K // bk, nj, nt),
        in_specs=[pl.BlockSpec((bt, bk), lambda i, j, t: (t, i)),
                  pl.BlockSpec((bt, bn), lambda i, j, t: (t, j + j0))],
        out_specs=[out_spec], out_shape=[out_shape],
        scratch_shapes=[pltpu.VMEM((bk, bn), F32)],
        args=(x, y))
    return out, touts


def _inproj_fwd(h, norm, w, name):
    T = h.shape[0]
    tm = _tile(T, (512, 384, 256, 128))

    def body(h_ref, n_ref, w_ref, o_ref, u_ref):
        u, _, _ = _rms(h_ref[...], n_ref[...], D)
        ub = u.astype(BF16)
        u_ref[...] = ub
        o_ref[...] = _dot(ub, w_ref[...])

    return _call(
        body, name=name, grid=(T // tm,),
        in_specs=[pl.BlockSpec((tm, D), lambda i: (i, 0)),
                  pl.BlockSpec((1, D), lambda i: (0, 0)),
                  pl.BlockSpec((D, PROJW), lambda i: (0, 0), pipeline_mode=pl.Buffered(1))],
        out_specs=[pl.BlockSpec((tm, PROJW), lambda i: (i, 0)),
                   pl.BlockSpec((tm, D), lambda i: (i, 0))],
        out_shape=[jax.ShapeDtypeStruct((T, PROJW), F32), jax.ShapeDtypeStruct((T, D), BF16)],
        compiler_params=_cp(),
    )(h, norm, w)


def _inproj_bwd(h, dres, dlo, dhi, norm, w, name):
    T = h.shape[0]
    tm = _tile(T, (512, 384, 256, 128))
    hw = PROJW // 2

    def body(h_ref, d_ref, lo_ref, hi_ref, n_ref, wlo_ref, whi_ref, dh_ref, dn_ref):
        @pl.when(pl.program_id(0) == 0)
        def _():
            dn_ref[...] = jnp.zeros_like(dn_ref)

        _, xh, r = _rms(h_ref[...], n_ref[...], D)
        du = _dot(lo_ref[...], wlo_ref[...], NT) + _dot(hi_ref[...], whi_ref[...], NT)
        dn_ref[...] += _colsum(du * xh)
        dh_ref[...] = d_ref[...] + _rms_bwd(du, xh, r, n_ref[...], D)

    return _call(
        body, name=name, grid=(T // tm,),
        in_specs=[pl.BlockSpec((tm, D), lambda i: (i, 0)),
                  pl.BlockSpec((tm, D), lambda i: (i, 0)),
                  pl.BlockSpec((tm, hw), lambda i: (i, 0)),
                  pl.BlockSpec((tm, hw), lambda i: (i, 0)),
                  pl.BlockSpec((1, D), lambda i: (0, 0)),
                  pl.BlockSpec((D, hw), lambda i: (0, 0)),
                  pl.BlockSpec((D, hw), lambda i: (0, 1))],
        out_specs=[pl.BlockSpec((tm, D), lambda i: (i, 0)),
                   pl.BlockSpec((1, D), lambda i: (0, 0))],
        out_shape=[jax.ShapeDtypeStruct((T, D), F32), jax.ShapeDtypeStruct((1, D), F32)],
        compiler_params=_cp(),
    )(h, dres, dlo, dhi, norm, w, w)


C_FQ, C_FK, C_FV, C_CQ, C_CKV, C_MISC, C_GATE = 0, 512, 1024, 1536, 1792, 1920, 2048
L_KR, L_FL = 64, 96


def _prep_fwd(proj, rc, rs, gfq, gfk, gcq, gckv, gmq, gmk, bfv, wuq, wuk, wuv, name):
    T = proj.shape[0]
    tm = _tile(T, (256, 128))

    def body(p_ref, rc_ref, rs_ref, gfq_ref, gfk_ref, gcq_ref, gckv_ref, gmq_ref, gmk_ref, bf_ref,
             wuq_ref, wuk_ref, wuv_ref, fq_ref, fk_ref, fv_ref, qm_ref, km_ref, vm_ref, lf_ref):
        o64, o128, o256 = _seg_ones(128, 6), _seg_ones(128, 7), _seg_ones(256, 8)
        for blk in range(4):
            for (c0, g_ref, o_ref) in ((C_FQ, gfq_ref, fq_ref), (C_FK, gfk_ref, fk_ref)):
                x = p_ref[:, c0 + 128 * blk:c0 + 128 * (blk + 1)]
                fn, _, _ = _rms(x, g_ref[...], 64, o64)
                o_ref[:, 128 * blk:128 * (blk + 1)] = fn.astype(BF16)
        fv_ref[...] = p_ref[:, C_FV:C_FV + 512].astype(BF16)

        rcv = rc_ref[...]
        rsv = rs_ref[...]
        cqn, _, _ = _rms(p_ref[:, C_CQ:C_CQ + QR], gcq_ref[...], QR, o256)
        qpre = _dot(cqn.astype(BF16), wuq_ref[...])
        ckvn, _, _ = _rms(p_ref[:, C_CKV:C_CKV + KVR], gckv_ref[...], KVR, o128)
        ckvb = ckvn.astype(BF16)
        kpre = _dot(ckvb, wuk_ref[...])
        vm_ref[...] = _dot(ckvb, wuv_ref[...]).astype(BF16)
        misc = p_ref[:, C_MISC:C_MISC + 128]
        ln = _lane(misc.shape)
        kr = jnp.where(jnp.logical_and(ln >= L_KR, ln < L_KR + ROPE), misc, 0.0)
        for hh in range(NH):
            sl = slice(128 * hh, 128 * (hh + 1))
            qn, _, _ = _rms(qpre[:, sl], gmq_ref[...], MLA_QK, o128)
            qm_ref[:, sl] = (qn * rcv + _rope_swap(qn) * rsv).astype(BF16)
            kn, _, _ = _rms(kpre[:, sl] + kr, gmk_ref[...], MLA_QK, o128)
            km_ref[:, sl] = (kn * rcv + _rope_swap(kn) * rsv).astype(BF16)
        z = misc + bf_ref[...]
        lf_ref[...] = jnp.minimum(z, 0.0) - jnp.log(1.0 + jnp.exp(-jnp.abs(z)))

    row = lambda w: pl.BlockSpec((tm, w), lambda i: (i, 0))
    full = lambda a: pl.BlockSpec(a.shape, lambda i: (0, 0))
    return _call(
        body, name=name, grid=(T // tm,),
        in_specs=[row(PROJW // 2), row(128), row(128)] + [full(a) for a in (gfq, gfk, gcq, gckv, gmq, gmk, bfv, wuq, wuk, wuv)],
        out_specs=[row(512), row(512), row(512), row(1024), row(1024), row(512), row(128)],
        out_shape=[jax.ShapeDtypeStruct((T, 512), BF16), jax.ShapeDtypeStruct((T, 512), BF16),
                   jax.ShapeDtypeStruct((T, 512), BF16), jax.ShapeDtypeStruct((T, 1024), BF16),
                   jax.ShapeDtypeStruct((T, 1024), BF16), jax.ShapeDtypeStruct((T, 512), BF16),
                   jax.ShapeDtypeStruct((T, 128), F32)],
        compiler_params=_cp(),
    )(proj, rc, rs, gfq, gfk, gcq, gckv, gmq, gmk, bfv, wuq, wuk, wuv)


def _prep_bwd(proj, rc, rs, gfq, gfk, gcq, gckv, gmq, gmk, bfv, wuq, wuk, wuv,
              dfq, dfk, dfv, dqm, dkm, dvm, dlf, name):
    T = proj.shape[0]
    tm = _tile(T, (256, 128))

    def body(p_ref, rc_ref, rs_ref, gfq_ref, gfk_ref, gcq_ref, gckv_ref, gmq_ref, gmk_ref, bf_ref,
             wuq_ref, wuk_ref, wuv_ref, dfq_ref, dfk_ref, dfv_ref, dqm_ref, dkm_ref, dvm_ref, dlf_ref,
             dp_ref, dgfq_ref, dgfk_ref, dgcq_ref, dgckv_ref, dgmq_ref, dgmk_ref, dbf_ref,
             dwuq_ref, dwuk_ref, dwuv_ref, dqpre_sc, dkpre_sc):
        accs = (dgfq_ref, dgfk_ref, dgcq_ref, dgckv_ref, dgmq_ref, dgmk_ref, dbf_ref, dwuq_ref, dwuk_ref, dwuv_ref)

        @pl.when(pl.program_id(0) == 0)
        def _():
            for a in accs:
                a[...] = jnp.zeros_like(a)

        o64, o128, o256 = _seg_ones(128, 6), _seg_ones(128, 7), _seg_ones(256, 8)
        for (c0, g_ref, d_ref, dg_ref) in ((C_FQ, gfq_ref, dfq_ref, dgfq_ref), (C_FK, gfk_ref, dfk_ref, dgfk_ref)):
            dg = jnp.zeros((1, 128), F32)
            for blk in range(4):
                x = p_ref[:, c0 + 128 * blk:c0 + 128 * (blk + 1)]
                _, xh, r = _rms(x, g_ref[...], 64, o64)
                dy = d_ref[:, 128 * blk:128 * (blk + 1)]
                dg = dg + _colsum(dy * xh)
                dp_ref[:, c0 + 128 * blk:c0 + 128 * (blk + 1)] = _rms_bwd(dy, xh, r, g_ref[...], 64, o64).astype(BF16)
            dg_ref[...] += dg
        dp_ref[:, C_FV:C_FV + 512] = dfv_ref[...].astype(BF16)

        rcv = rc_ref[...]
        rsv = rs_ref[...]
        cqn, cqh, cqr = _rms(p_ref[:, C_CQ:C_CQ + QR], gcq_ref[...], QR, o256)
        cqb = cqn.astype(BF16)
        qpre = _dot(cqb, wuq_ref[...])
        dgq = jnp.zeros((1, 128), F32)
        for hh in range(NH):
            sl = slice(128 * hh, 128 * (hh + 1))
            _, xh, r = _rms(qpre[:, sl], gmq_ref[...], MLA_QK, o128)
            dout = dqm_ref[:, sl]
            dqn = dout * rcv + _rope_swap(dout * rsv)
            dgq = dgq + _colsum(dqn * xh)
            dqpre_sc[:, sl] = _rms_bwd(dqn, xh, r, gmq_ref[...], MLA_QK, o128).astype(BF16)
        dgmq_ref[...] += dgq
        dqpre = dqpre_sc[...]
        dwuq_ref[...] += _dot(cqb, dqpre, TN)
        dcqn = _dot(dqpre, wuq_ref[...], NT)
        dgcq_ref[...] += _colsum(dcqn * cqh)
        dp_ref[:, C_CQ:C_CQ + QR] = _rms_bwd(dcqn, cqh, cqr, gcq_ref[...], QR, o256).astype(BF16)

        ckvn, ckvh, ckvr = _rms(p_ref[:, C_CKV:C_CKV + KVR], gckv_ref[...], KVR, o128)
        ckvb = ckvn.astype(BF16)
        kpre = _dot(ckvb, wuk_ref[...])
        misc = p_ref[:, C_MISC:C_MISC + 128]
        ln = _lane(misc.shape)
        is_kr = jnp.logical_and(ln >= L_KR, ln < L_KR + ROPE)
        kr = jnp.where(is_kr, misc, 0.0)
        dgk = jnp.zeros((1, 128), F32)
        dkr = jnp.zeros(misc.shape, F32)
        for hh in range(NH):
            sl = slice(128 * hh, 128 * (hh + 1))
            _, xh, r = _rms(kpre[:, sl] + kr, gmk_ref[...], MLA_QK, o128)
            dout = dkm_ref[:, sl]
            dkn = dout * rcv + _rope_swap(dout * rsv)
            dgk = dgk + _colsum(dkn * xh)
            dkx = _rms_bwd(dkn, xh, r, gmk_ref[...], MLA_QK, o128)
            dkr = dkr + jnp.where(is_kr, dkx, 0.0)
            dkpre_sc[:, sl] = jnp.where(ln < 64, dkx, 0.0).astype(BF16)
        dgmk_ref[...] += dgk
        dkpre = dkpre_sc[...]
        dvmb = dvm_ref[...].astype(BF16)
        dwuk_ref[...] += _dot(ckvb, dkpre, TN)
        dwuv_ref[...] += _dot(ckvb, dvmb, TN)
        dckvn = _dot(dkpre, wuk_ref[...], NT) + _dot(dvmb, wuv_ref[...], NT)
        dgckv_ref[...] += _colsum(dckvn * ckvh)
        dp_ref[:, C_CKV:C_CKV + KVR] = _rms_bwd(dckvn, ckvh, ckvr, gckv_ref[...], KVR, o128).astype(BF16)

        z = misc + bf_ref[...]
        dz = dlf_ref[...] * (1.0 - jax.nn.sigmoid(z))
        dbf_ref[...] += _colsum(dz)
        dp_ref[:, C_MISC:C_MISC + 128] = (dkr + dz).astype(BF16)

    row = lambda w: pl.BlockSpec((tm, w), lambda i: (i, 0))
    full = lambda a: pl.BlockSpec(a.shape, lambda i: (0, 0))
    small = (gfq, gfk, gcq, gckv, gmq, gmk, bfv, wuq, wuk, wuv)
    acc_shapes = [(1, 128), (1, 128), (1, QR), (1, KVR), (1, 128), (1, 128), (1, 128),
                  (QR, 1024), (KVR, 1024), (KVR, 512)]
    return _call(
        body, name=name, grid=(T // tm,),
        in_specs=[row(PROJW // 2), row(128), row(128)] + [full(a) for a in small]
                 + [row(512), row(512), row(512), row(1024), row(1024), row(512), row(128)],
        out_specs=[row(PROJW // 2)] + [pl.BlockSpec(s, lambda i: (0, 0)) for s in acc_shapes],
        out_shape=[jax.ShapeDtypeStruct((T, PROJW // 2), BF16)] + [jax.ShapeDtypeStruct(s, F32) for s in acc_shapes],
        scratch_shapes=[pltpu.VMEM((tm, 1024), BF16), pltpu.VMEM((tm, 1024), BF16)],
        compiler_params=_cp(),
    )(proj, rc, rs, *small, dfq, dfk, dfv, dqm, dkm, dvm, dlf)


def _scan_lanes(x, reverse):
    n = x.shape[-1]
    ln = _lane(x.shape)
    k = 1
    while k < n:
        if reverse:
            x = x + jnp.where(ln < n - k, pltpu.roll(x, n - k, x.ndim - 1), 0.0)
        else:
            x = x + jnp.where(ln >= k, pltpu.roll(x, k, x.ndim - 1), 0.0)
        k *= 2
    return x


def _forget_scan(lf, reverse, name):
    def body(x_ref, o_ref):
        x = x_ref[...]
        ln = _lane(x.shape)
        pad = jnp.logical_and(ln >= NMETA, ln < MPAD)
        o_ref[...] = jnp.where(pad, 0.0, _scan_lanes(jnp.where(pad, 0.0, x), reverse))

    return _call(body, name=name, out_shape=jax.ShapeDtypeStruct(lf.shape, F32), compiler_params=_cp())(lf)


def _attn_blocks(LP, tq):
    return [(0, MPAD, MPAD)] + [(MPAD + i * tq, tq, MPAD + (i + 1) * tq) for i in range((LP - MPAD) // tq)]


def _attn_scores(q_ref, k_ref, e, r0, rn, kend, wide, scale, bias):
    if wide:
        qe = q_ref[r0:r0 + rn, 128 * e:128 * (e + 1)]
        ke = k_ref[0:kend, 128 * e:128 * (e + 1)]
    else:
        qb = q_ref[r0:r0 + rn, :]
        mine = (_lane(qb.shape) < 64) if e == 0 else (_lane(qb.shape) >= 64)
        qe = jnp.where(mine, qb, jnp.zeros_like(qb))
        ke = k_ref[0:kend, :]
    s = _dot(qe, ke, NT) * scale
    if bias is not None:
        ct_ref, cr_ref = bias
        s = s + ct_ref[0, r0:r0 + rn, e:e + 1] - cr_ref[0, :, 0:kend]
    neg = -1e30
    if r0 == 0:
        qi = lax.broadcasted_iota(jnp.int32, (rn, kend), 0)
        ki = lax.broadcasted_iota(jnp.int32, (rn, kend), 1)
        s = jnp.where(jnp.logical_and(ki <= qi, ki < NMETA), s, neg)
    else:
        d0 = kend - rn
        head = jnp.where(_lane((rn, MPAD)) < NMETA, s[:, :MPAD], neg)
        qi = lax.broadcasted_iota(jnp.int32, (rn, rn), 0)
        diag = jnp.where(_lane((rn, rn)) <= qi, s[:, d0:], neg)
        s = jnp.concatenate([head] + ([s[:, MPAD:d0]] if d0 > MPAD else []) + [diag], axis=1)
    m = jnp.max(s, axis=-1, keepdims=True)
    p = jnp.exp(s - m)
    l = jnp.sum(p, axis=-1, keepdims=True)
    return qe, ke, p, l


def _attn_specs(B, LP, wide, has_bias):
    qw = 256 if wide else 128
    specs = [pl.BlockSpec((LP, qw), lambda b, hp: (b, hp)),
             pl.BlockSpec((LP, qw), lambda b, hp: (b, hp)),
             pl.BlockSpec((LP, 128), lambda b, hp: (b, hp))]
    bias_specs = []
    if has_bias:
        bias_specs = [pl.BlockSpec((1, LP, 2), lambda b, hp: (b * 4 + hp, 0, 0)),
                      pl.BlockSpec((1, 1, LP), lambda b, hp: (b * 8 + 2 * hp, 0, 0)),
                      pl.BlockSpec((1, 1, LP), lambda b, hp: (b * 8 + 2 * hp + 1, 0, 0))]
    return qw, specs, bias_specs


def _attn_fwd(q, k, v, bias, B, LP, wide, scale, name, tasks=()):
    T = q.shape[0]
    blocks = _attn_blocks(LP, ATTN_TQ)
    qw, specs, bias_specs = _attn_specs(B, LP, wide, bias is not None)

    def body(*refs):
        if bias is not None:
            q_ref, k_ref, v_ref, ct_ref, cr0_ref, cr1_ref, o_ref = refs
            crs = (cr0_ref, cr1_ref)
        else:
            q_ref, k_ref, v_ref, o_ref = refs
        for (r0, rn, kend) in blocks:
            outs = []
            for e in (0, 1):
                bs = (ct_ref, crs[e]) if bias is not None else None
                _, _, p, l = _attn_scores(q_ref, k_ref, e, r0, rn, kend, wide, scale, bs)
                outs.append(_dot(p.astype(BF16), v_ref[0:kend, :]) / l)
            o = jnp.where(_lane(outs[0].shape) < 64, outs[0], outs[1])
            o_ref[r0:r0 + rn, :] = o.astype(BF16)

    args = (q, k, v) + ((bias[0], bias[1], bias[1]) if bias is not None else ())
    (out,), touts = _call_tasks(
        body, tasks, name=name, grid=(B, 4),
        in_specs=specs + bias_specs,
        out_specs=[pl.BlockSpec((LP, 128), lambda b, hp: (b, hp))],
        out_shape=[jax.ShapeDtypeStruct((T, 512), BF16)],
        args=args)
    return out, touts


def _attn_bwd(q, k, v, do, bias, B, LP, wide, scale, name, tasks=()):
    T = q.shape[0]
    blocks = _attn_blocks(LP, ATTN_TQ)
    qw, specs, bias_specs = _attn_specs(B, LP, wide, bias is not None)
    has_bias = bias is not None

    def body(*refs):
        if has_bias:
            (q_ref, k_ref, v_ref, do_ref, ct_ref, cr0_ref, cr1_ref,
             dq_ref, dk_ref, dv_ref, dc0_ref, dc1_ref) = refs
            crs = (cr0_ref, cr1_ref)
            dcs = (dc0_ref, dc1_ref)
            dc0_ref[...] = jnp.zeros_like(dc0_ref)
            dc1_ref[...] = jnp.zeros_like(dc1_ref)
        else:
            q_ref, k_ref, v_ref, do_ref, dq_ref, dk_ref, dv_ref = refs
        dk_ref[...] = jnp.zeros_like(dk_ref)
        dv_ref[...] = jnp.zeros_like(dv_ref)
        for (r0, rn, kend) in blocks:
            dqs = []
            for e in (0, 1):
                bs = (ct_ref, crs[e]) if has_bias else None
                qe, ke, p, l = _attn_scores(q_ref, k_ref, e, r0, rn, kend, wide, scale, bs)
                pn = p * (1.0 / l)
                dob = do_ref[r0:r0 + rn, :]
                mine = (_lane(dob.shape) < 64) if e == 0 else (_lane(dob.shape) >= 64)
                doe = jnp.where(mine, dob, jnp.zeros_like(dob))
                dp = _dot(doe, v_ref[0:kend, :], NT)
                delta = jnp.sum(pn * dp, axis=-1, keepdims=True)
                ds = pn * (dp - delta)
                dsb = ds.astype(BF16)
                dqe = _dot(dsb, ke) * scale
                dke = _dot(dsb, qe, TN) * scale
                if wide:
                    dq_ref[r0:r0 + rn, 128 * e:128 * (e + 1)] = dqe
                    dk_ref[0:kend, 128 * e:128 * (e + 1)] += dke
                else:
                    dqs.append(dqe)
                    dk_ref[0:kend, :] += dke
                dv_ref[0:kend, :] += _dot(pn.astype(BF16), doe, TN)
                if has_bias:
                    dcs[e][0, :, 0:kend] -= _colsum(ds)
            if not wide:
                dq_ref[r0:r0 + rn, :] = jnp.where(_lane(dqs[0].shape) < 64, dqs[0], dqs[1])

    args = (q, k, v, do) + ((bias[0], bias[1], bias[1]) if has_bias else ())
    out_specs = [pl.BlockSpec((LP, qw), lambda b, hp: (b, hp)),
                 pl.BlockSpec((LP, qw), lambda b, hp: (b, hp)),
                 pl.BlockSpec((LP, 128), lambda b, hp: (b, hp))]
    out_shape = [jax.ShapeDtypeStruct(q.shape, F32), jax.ShapeDtypeStruct(q.shape, F32),
                 jax.ShapeDtypeStruct((T, 512), F32)]
    if has_bias:
        out_specs += [pl.BlockSpec((1, 1, LP), lambda b, hp: (b * 4 + hp, 0, 0))] * 2
        out_shape += [jax.ShapeDtypeStruct((B * 4, 1, LP), F32)] * 2
    return _call_tasks(
        body, tasks, name=name, grid=(B, 4),
        in_specs=specs + [pl.BlockSpec((LP, 128), lambda b, hp: (b, hp))] + bias_specs,
        out_specs=out_specs, out_shape=out_shape, args=args)


def _post_fwd(h, of, om, proj, bg, wbf, wbm, wout, name):
    T = h.shape[0]
    tm = _tile(T, (512, 384, 256, 128))

    def body(h_ref, of_ref, om_ref, gl_ref, bg_ref, wbf_ref, wbm_ref, wo_ref, o_ref, mix_ref):
        gate = jax.nn.sigmoid(gl_ref[...] + bg_ref[...])
        mix = gate[:, :D] * _dot(of_ref[...], wbf_ref[...]) + gate[:, D:] * _dot(om_ref[...], wbm_ref[...])
        mb = mix.astype(BF16)
        mix_ref[...] = mb
        o_ref[...] = h_ref[...] + _dot(mb, wo_ref[...])

    row = lambda w: pl.BlockSpec((tm, w), lambda i: (i, 0))
    full = lambda a: pl.BlockSpec(a.shape, lambda i: (0, 0))
    return _call(
        body, name=name, grid=(T // tm,),
        in_specs=[row(D), row(512), row(512), pl.BlockSpec((tm, 2 * D), lambda i: (i, 1)),
                  full(bg), full(wbf), full(wbm), full(wout)],
        out_specs=[row(D), row(D)],
        out_shape=[jax.ShapeDtypeStruct((T, D), F32), jax.ShapeDtypeStruct((T, D), BF16)],
        compiler_params=_cp(),
    )(h, of, om, proj, bg, wbf, wbm, wout)


def _post_bwd(dh, of, om, proj, bg, wbf, wbm, wout, name):
    T = dh.shape[0]
    tm = _tile(T, (512, 384, 256, 128))

    def body(d_ref, of_ref, om_ref, gl_ref, bg_ref, wbf_ref, wbm_ref, wo_ref,
             dgl_ref, dbf_ref, dbm_ref, dof_ref, dom_ref, dbg_ref):
        @pl.when(pl.program_id(0) == 0)
        def _():
            dbg_ref[...] = jnp.zeros_like(dbg_ref)

        gate = jax.nn.sigmoid(gl_ref[...] + bg_ref[...])
        dmix = _dot(d_ref[...].astype(BF16), wo_ref[...], NT)
        ofx = _dot(of_ref[...], wbf_ref[...])
        omx = _dot(om_ref[...], wbm_ref[...])
        gf = gate[:, :D]
        gm = gate[:, D:]
        dof = (dmix * gf).astype(BF16)
        dom = (dmix * gm).astype(BF16)
        dglf = dmix * ofx * gf * (1.0 - gf)
        dglm = dmix * omx * gm * (1.0 - gm)
        dgl_ref[:, :D] = dglf.astype(BF16)
        dgl_ref[:, D:] = dglm.astype(BF16)
        dbg_ref[:, :D] += _colsum(dglf)
        dbg_ref[:, D:] += _colsum(dglm)
        dbf_ref[...] = dof
        dbm_ref[...] = dom
        dof_ref[...] = _dot(dof, wbf_ref[...], NT).astype(BF16)
        dom_ref[...] = _dot(dom, wbm_ref[...], NT).astype(BF16)

    row = lambda w: pl.BlockSpec((tm, w), lambda i: (i, 0))
    full = lambda a: pl.BlockSpec(a.shape, lambda i: (0, 0))
    return _call(
        body, name=name, grid=(T // tm,),
        in_specs=[row(D), row(512), row(512), pl.BlockSpec((tm, 2 * D), lambda i: (i, 1)),
                  full(bg), full(wbf), full(wbm), full(wout)],
        out_specs=[row(2 * D), row(D), row(D), row(512), row(512), pl.BlockSpec((1, 2 * D), lambda i: (0, 0))],
        out_shape=[jax.ShapeDtypeStruct((T, 2 * D), BF16), jax.ShapeDtypeStruct((T, D), BF16),
                   jax.ShapeDtypeStruct((T, D), BF16), jax.ShapeDtypeStruct((T, 512), BF16),
                   jax.ShapeDtypeStruct((T, 512), BF16), jax.ShapeDtypeStruct((1, 2 * D), F32)],
        compiler_params=_cp(),
    )(dh, of, om, proj, bg, wbf, wbm, wout)


def _loss_head(h3, target, B, LP, name):
    S = LP - MPAD
    half = LP // 2
    first = half - MPAD

    def body(h_ref, t_ref, dy_ref, l_ref):
        b = pl.program_id(0)
        k = pl.program_id(1)

        @pl.when(jnp.logical_and(b == 0, k == 0))
        def _():
            l_ref[...] = jnp.zeros_like(l_ref)

        @pl.when(k == 0)
        def _():
            e = h_ref[MPAD:, :] - t_ref[0, 0:first, :]
            dy_ref[0:MPAD, :] = jnp.zeros((MPAD, D), F32)
            dy_ref[MPAD:, :] = e * (1.0 / D)
            l_ref[...] += jnp.sum(e * e, axis=0, keepdims=True) * (0.5 / D)

        @pl.when(k == 1)
        def _():
            e = h_ref[...] - t_ref[0, first:S, :]
            dy_ref[...] = e * (1.0 / D)
            l_ref[...] += jnp.sum(e * e, axis=0, keepdims=True) * (0.5 / D)

    return _call(
        body, name=name, grid=(B, 2),
        in_specs=[pl.BlockSpec((half, D), lambda b, k: (b * 2 + k, 0)),
                  pl.BlockSpec((1, S, D), lambda b, k: (b, 0, 0))],
        out_specs=[pl.BlockSpec((half, D), lambda b, k: (b * 2 + k, 0)),
                   pl.BlockSpec((1, D), lambda b, k: (0, 0))],
        out_shape=[jax.ShapeDtypeStruct(h3.shape, F32), jax.ShapeDtypeStruct((1, D), F32)],
        compiler_params=_cp(),
    )(h3, target)


def _rope_tables(B, LP):
    pos = jnp.concatenate([jnp.arange(MPAD, dtype=F32), NMETA + jnp.arange(LP - MPAD, dtype=F32)])
    inv_freq = ROPE_THETA ** (-jnp.arange(0, ROPE, 2, dtype=F32) / ROPE)
    ang = pos[:, None] * inv_freq[None, :]
    cos, sin = jnp.cos(ang), jnp.sin(ang)
    z32 = jnp.zeros((LP, 32), F32)
    rc = jnp.concatenate([jnp.ones((LP, 64), F32), cos, cos, z32], axis=1)
    rs = jnp.concatenate([jnp.zeros((LP, 64), F32), -sin, sin, z32], axis=1)
    return jnp.tile(rc, (B, 1)), jnp.tile(rs, (B, 1))


def _pad_lanes(v, start, width=128):
    n = v.shape[1]
    return jnp.concatenate([jnp.zeros((1, start), F32), v, jnp.zeros((1, width - start - n), F32)], axis=1)


G_FFN1 = ["ffn1_w_gu", "ffn1_w_down"]
G_MIX = ["w_in", "mla_w_uq", "mla_w_ukv", "w_branch_fox", "w_branch_mla", "w_out"]
G_OUT = ["w_out", "w_branch_fox", "w_branch_mla"]
G_IN = ["w_in", "mla_w_uq", "mla_w_ukv"]


def _step(x, target, meta, vec, gath, shards):
    dist = shards is not None
    B, S, _ = x.shape
    LP = MPAD + S
    T = B * LP
    gath = dict(gath)

    def gather(names, wide=()):
        return [_gather_task([shards[n] for n in names], wide)] if dist else []

    def flat_gu(w):
        return w if w.ndim == 2 else _cols_from_shards(w)

    def gathered(names, touts):
        if dist:
            gath.update(zip(names, touts[0]))

    g4, sums, red = {}, {}, {}

    def scatter(names):
        return [_a2a_task([_pieces(g4[n]) for n in names])] if dist else []

    def scattered(names, tout, me):
        for n, r in zip(names, tout):
            sums[n] = _sum_pieces(r, _pieces(g4[n]), me, "rs_sum_" + n)

    def join(names):
        return [_join_task([sums[n] for n in names])] if dist else []

    def joined(names, tout):
        for n, r in zip(names, tout):
            red[n] = (sums[n], r)

    me = None
    if dist:
        me = jnp.tile((4 * lax.axis_index("x") + 2 * lax.axis_index("y") + lax.axis_index("c")).reshape(1), 2).astype(jnp.int32)

    h0 = jnp.concatenate([jnp.broadcast_to(meta[None], (B, NMETA, D)),
                          jnp.zeros((B, MPAD - NMETA, D), F32), x], axis=1).reshape(T, D)
    rc, rs = _rope_tables(B, LP)
    gfq = jnp.tile(vec["fox_q_norm"], (1, 2))
    gfk = jnp.tile(vec["fox_k_norm"], (1, 2))
    gmq = _pad_lanes(vec["mla_q_norm"], 0)
    gmk = _pad_lanes(vec["mla_k_norm"], 0)
    bfv = _pad_lanes(vec["b_forget"], L_FL)

    w1gu, w1d = flat_gu(gath["ffn1_w_gu"]), gath["ffn1_w_down"].reshape(DFF, D)
    h1, touts = _ffn_fwd(h0, vec["ffn1_norm"], w1gu, w1d, "ffn1_fwd", gather(G_MIX))
    gathered(G_MIX, touts)
    wm = _mixer_weights(gath)
    small = (gfq, gfk, vec["mla_cq_norm"], vec["mla_ckv_norm"], gmq, gmk, bfv, wm["wuq"], wm["wuk"], wm["wuv"])
    proj, u2 = _inproj_fwd(h1, vec["mix_norm"], wm["w_in"], "inproj_fwd")
    fq, fk, fv, qm, km, vm, lf = _prep_fwd(proj, rc, rs, *small, name="prep_fwd")
    lf_rows = lf[:, L_FL:L_FL + NH].reshape(B, LP, NH).transpose(0, 2, 1).reshape(B * NH, LP)
    crow = _forget_scan(lf_rows, False, "forget_scan")
    ctok = crow.reshape(B, 4, 2, LP).transpose(0, 1, 3, 2).reshape(B * 4, LP, 2)
    bias = (ctok, crow.reshape(B * NH, 1, LP))
    of, touts = _attn_fwd(fq, fk, fv, bias, B, LP, False, 64 ** -0.5, "fox_fwd", gather(["ffn2_w_gu"], wide=(0,)))
    gathered(["ffn2_w_gu"], touts)
    om, touts = _attn_fwd(qm, km, vm, None, B, LP, True, MLA_QK ** -0.5, "mla_fwd", gather(["ffn2_w_down"]))
    gathered(["ffn2_w_down"], touts)
    h2, mix = _post_fwd(h1, of, om, proj, vec["b_gate"], wm["wbf"], wm["wbm"], wm["w_out"], "post_fwd")
    w2gu, w2d = flat_gu(gath["ffn2_w_gu"]), gath["ffn2_w_down"].reshape(DFF, D)
    h3, _ = _ffn_fwd(h2, vec["ffn2_norm"], w2gu, w2d, "ffn2_fwd")
    dy, lpart = _loss_head(h3, target, B, LP, "loss_head")

    gv = {}
    (dh2, u3, a2, dgp2, gv["ffn2_norm"]), _ = _ffn_bwd(h2, dy, vec["ffn2_norm"], w2gu, w2d, "ffn2_bwd")
    g4["ffn2_w_gu"] = _wgrad(u3, dgp2, "ffn2_dwgu", bn=DFF, shards=2)[0]
    g4["ffn2_w_down"] = _wgrad(a2, dy, "ffn2_dwd", scale=0.5, bk=FH, bt=2176)[0].reshape(N_CHIPS, DFF // N_CHIPS, D)

    dgl, dbf, dbm, dof, dom, gv["b_gate"] = _post_bwd(dh2, of, om, proj, vec["b_gate"], wm["wbf"], wm["wbm"], wm["w_out"], "post_bwd")
    g4["w_out"] = _wgrad(mix, dh2, "dw_out", bt=2176)[0].reshape(N_CHIPS, D // N_CHIPS, D)
    g4["w_branch_fox"] = _cols_to_shards(_wgrad(of, dbf, "dw_bf", bt=2176)[0])
    g4["w_branch_mla"] = _cols_to_shards(_wgrad(om, dbm, "dw_bm", bt=2176)[0])
    G_FFN2 = ["ffn2_w_gu", "ffn2_w_down"]
    (dfq, dfk, dfv, dc0, dc1), touts = _attn_bwd(fq, fk, fv, dof, bias, B, LP, False, 64 ** -0.5, "fox_bwd", scatter(G_FFN2))
    if dist:
        scattered(G_FFN2, touts[0], me)
    (dqm, dkm, dvm), touts = _attn_bwd(qm, km, vm, dom, None, B, LP, True, MLA_QK ** -0.5, "mla_bwd",
                                       scatter(G_OUT) + join(G_FFN2))
    if dist:
        scattered(G_OUT, touts[0], me)
        joined(G_FFN2, touts[1])
    dc = jnp.concatenate([dc0, dc1], axis=1).reshape(B * NH, LP)
    dlf_rows = _forget_scan(dc, True, "forget_scan_bwd")
    dlf = dlf_rows.reshape(B, NH, LP).transpose(0, 2, 1).reshape(T, NH)
    dlf = jnp.concatenate([jnp.zeros((T, L_FL), F32), dlf, jnp.zeros((T, 128 - L_FL - NH), F32)], axis=1)
    (dlo, dgfq, dgfk, gv["mla_cq_norm"], gv["mla_ckv_norm"], dgmq, dgmk, dbfv,
     dwuq, dwuk, dwuv) = _prep_bwd(proj, rc, rs, *small, dfq, dfk, dfv, dqm, dkm, dvm, dlf, name="prep_bwd")
    gv["fox_q_norm"] = dgfq[:, :64] + dgfq[:, 64:]
    gv["fox_k_norm"] = dgfk[:, :64] + dgfk[:, 64:]
    gv["mla_q_norm"] = dgmq[:, :MLA_QK]
    gv["mla_k_norm"] = dgmk[:, :MLA_QK]
    gv["b_forget"] = dbfv[:, L_FL:L_FL + NH]
    dwin = jnp.concatenate([_wgrad(u2, dlo, "dw_in_lo")[0], _wgrad(u2, dgl, "dw_in_hi")[0]], axis=1)
    g4["w_in"] = _cols_to_shards(_win_from_kernel(dwin))
    g4["mla_w_uq"] = _cols_to_shards(
        dwuq.astype(GRAD_DTYPE).reshape(QR, NH, 128)[:, :, :MLA_QK].reshape(QR, NH * MLA_QK))
    dukv = jnp.concatenate([dwuk.reshape(KVR, NH, 128)[:, :, :64], dwuv.reshape(KVR, NH, 64)], axis=2)
    g4["mla_w_ukv"] = _cols_to_shards(dukv.astype(GRAD_DTYPE).reshape(KVR, NH * 128))
    dh1, gv["mix_norm"] = _inproj_bwd(h1, dh2, dlo, dgl, vec["mix_norm"], wm["w_in"], "inproj_bwd")

    (dh0, u1, a1, dgp1, gv["ffn1_norm"]), touts = _ffn_bwd(h0, dh1, vec["ffn1_norm"], w1gu, w1d, "ffn1_bwd",
                                                            scatter(G_IN) + join(G_OUT))
    if dist:
        scattered(G_IN, touts[0], me)
        joined(G_OUT, touts[1])
    dh0 = dh0.reshape(B, LP, D)
    grad_x = dh0[:, MPAD:]
    grad_meta = jnp.sum(dh0[:, :NMETA], axis=0)
    share = [_share_task([_stack_vectors([gv[n] for n in VECS]), grad_meta, lpart])] if dist else []
    dwd1, touts = _wgrad(a1, dh1, "ffn1_dwd", scale=0.5, bk=FH, tasks=share + join(G_IN))
    g4["ffn1_w_down"] = dwd1.reshape(N_CHIPS, DFF // N_CHIPS, D)
    shared = touts[0] if dist else None
    if dist:
        joined(G_IN, touts[1])
    gu = []
    for xv in (0, 1):
        tasks = (scatter(["ffn1_w_down"]) if xv == 0 else [_a2a_x_task([gu[0]], 0)]) if dist else []
        part, touts = _wgrad(u1, dgp1, "ffn1_dwgu_%d" % xv, bn=DFF, shards=2, cols=(xv, 1), tasks=tasks)
        gu.append(part.reshape(N_CHIPS, D // 2, FH))
        if dist and xv == 0:
            scattered(["ffn1_w_down"], touts[0], me)
        elif dist:
            got0 = touts[0][0]
    g4["ffn1_w_gu"] = jnp.concatenate(gu, axis=0).reshape(N_CHIPS, D, FH)
    if dist:
        got1 = _run_tasks([_a2a_x_task([gu[1]], 1)], "rs_ffn1_w_gu")[0][0]
        at0 = lax.axis_index("x") == 0
        mine = me.at[1].set(2 * lax.axis_index("y") + lax.axis_index("c"))
        sums["ffn1_w_gu"] = _sum_pieces(jnp.where(at0, got0, got1), jnp.where(at0, gu[0], gu[1]),
                                        mine, "rs_sum_ffn1_w_gu")
        joined(G_FFN1, _run_tasks(join(G_FFN1), "rs_join_ffn1")[0])
    return lpart, grad_x, grad_meta, gv, (red if dist else g4), shared


def _cols_from_shards(g4):
    n, r, c = g4.shape
    return g4.transpose(1, 0, 2).reshape(r, n * c)


def _cols_to_shards(full):
    r, c4 = full.shape
    return full.reshape(r, N_CHIPS, c4 // N_CHIPS).transpose(1, 0, 2)


def _win_to_kernel(wfull):
    z = lambda n: jnp.zeros((D, n), wfull.dtype)
    fl, cq, ckv, kr, gate = (wfull[:, 1536:1544], wfull[:, 1544:1800], wfull[:, 1800:1928],
                             wfull[:, 1928:1960], wfull[:, 1960:4008])
    misc = jnp.concatenate([z(L_KR), kr, fl, z(128 - L_FL - NH)], axis=1)
    return jnp.concatenate([wfull[:, :1536], cq, ckv, misc, gate], axis=1)


def _win_from_kernel(gk):
    m = C_MISC
    return jnp.concatenate([gk[:, :1536], gk[:, m + L_FL:m + L_FL + NH], gk[:, C_CQ:C_CQ + QR],
                            gk[:, C_CKV:C_CKV + KVR], gk[:, m + L_KR:m + L_KR + ROPE], gk[:, C_GATE:]], axis=1)


def _pieces(g4):
    n, r, c = g4.shape
    return g4.reshape(2 * n, r // 2, c)


def _mixer_weights(gath):
    w = {}
    w["w_in"] = _win_to_kernel(_cols_from_shards(gath["w_in"]))
    uq = _cols_from_shards(gath["mla_w_uq"]).reshape(QR, NH, MLA_QK)
    w["wuq"] = jnp.pad(uq, ((0, 0), (0, 0), (0, 128 - MLA_QK))).reshape(QR, NH * 128)
    ukv = _cols_from_shards(gath["mla_w_ukv"]).reshape(KVR, NH, 128)
    w["wuk"] = jnp.pad(ukv[:, :, :64], ((0, 0), (0, 0), (0, 64))).reshape(KVR, NH * 128)
    w["wuv"] = ukv[:, :, 64:].reshape(KVR, NH * 64)
    w["wbf"] = _cols_from_shards(gath["w_branch_fox"])
    w["wbm"] = _cols_from_shards(gath["w_branch_mla"])
    w["w_out"] = gath["w_out"].reshape(D, D)
    return w


def _chip_peers(x, y):
    return [(1 - x, y), (x, 1 - y), (1 - x, 1 - y)]


RELS = [(dx, dy, dc) for dx in (0, 1) for dy in (0, 1) for dc in (0, 1)][1:]


def _here():
    return lax.axis_index("x"), lax.axis_index("y"), lax.axis_index("c")


def _flip(a, d):
    return (1 - a) if d else a


def _remote(src, dst, send, recv, i, dev):
    return functools.partial(pltpu.make_async_remote_copy, src_ref=src, dst_ref=dst, send_sem=send.at[i],
                             recv_sem=recv.at[i], device_id=dev, device_id_type=MESH)


def _gathered_shape(s, wide):
    return jax.ShapeDtypeStruct((s.shape[0], N_CHIPS * s.shape[1]) if wide else (N_CHIPS,) + s.shape, s.dtype)


def _slot(ref, j, shard, wide, rows=None):
    if wide:
        lanes = pl.ds(pl.multiple_of(j * shard.shape[1], 128), shard.shape[1])
        return ref.at[slice(None) if rows is None else rows, lanes]
    return ref.at[j] if rows is None else ref.at[j, rows]


def _gather_task(shards, wide=()):
    n = len(shards)

    def descs(ins, outs, sems):
        send, recv, loc = sems
        x, y, c = _here()
        j = 2 * x + y
        locs, pairs = [], []
        for k in range(n):
            at = functools.partial(_slot, outs[k], shard=shards[k], wide=k in wide)
            locs.append(functools.partial(pltpu.make_async_copy, ins[k], at(j), loc.at[k]))
            for r, (px, py) in enumerate(_chip_peers(x, y)):
                dev = (px, py, c)
                pairs.append((_remote(ins[k], at(j), send, recv, 3 * k + r, dev),
                              _remote(ins[k], at(2 * px + py), send, recv, 3 * k + r, dev)))
        return locs, pairs

    return _Task(shards, [_gathered_shape(s, k in wide) for k, s in enumerate(shards)],
                 [pltpu.SemaphoreType.DMA((3 * n,)), pltpu.SemaphoreType.DMA((3 * n,)), pltpu.SemaphoreType.DMA((n,))],
                 descs)


class _SplitGather(_Task):
    PARTS = 2

    def __init__(self, shards, wide=()):
        n = 3 * len(shards) * self.PARTS
        dma = pltpu.SemaphoreType.DMA
        self.wide = wide
        super().__init__(shards, [_gathered_shape(s, k in wide) for k, s in enumerate(shards)],
                         [dma((n,)), dma((n,)), dma((n,)), dma((n,)), dma((len(shards),))], None)

    def _plan(self, ins, outs, sems):
        send, recv, fsend, frecv, loc = sems
        x, y, c = _here()
        j = 2 * x + y
        locs, first, passed = [], [], []
        for k in range(len(ins)):
            h = self.ins[k].shape[0] // 2
            parts = self.PARTS if h % (32 * self.PARTS) == 0 else 1
            hp = h // parts
            at = functools.partial(_slot, outs[k], shard=self.ins[k], wide=k in self.wide)
            locs.append(functools.partial(pltpu.make_async_copy, ins[k], at(j), loc.at[k]))
            for r, (px, py) in enumerate(_chip_peers(x, y)):
                p = 2 * px + py
                for q in range(parts):
                    i = (3 * k + r) * self.PARTS + q
                    mine = pl.ds(pl.multiple_of(c * h + q * hp, 8), hp)
                    theirs = pl.ds(pl.multiple_of((1 - c) * h + q * hp, 8), hp)
                    first.append((_remote(ins[k].at[mine], at(j, rows=mine), send, recv, i, (px, py, c)),
                                  _remote(ins[k].at[mine], at(p, rows=mine), send, recv, i, (px, py, c))))
                    passed.append((_remote(at(p, rows=mine), at(p, rows=mine), fsend, frecv, i, (x, y, 1 - c)),
                                   _remote(at(p, rows=mine), at(p, rows=theirs), fsend, frecv, i, (x, y, 1 - c))))
        return locs, first, passed

    def start(self, ins, outs, sems):
        locs, first, _ = self._plan(ins, outs, sems)
        for lc in locs:
            lc().start()
        for snd, _ in first:
            snd().start()

    def wait(self, ins, outs, sems):
        locs, first, passed = self._plan(ins, outs, sems)
        for (_, landed), (pass_on, _) in zip(first, passed):
            landed().wait_recv()
            pass_on().start()
        for _, rcv in passed:
            rcv().wait_recv()
        for snd, _ in first + passed:
            snd().wait_send()
        for lc in locs:
            lc().wait()


def _a2a_task(ps):
    n = len(ps)
    nr = len(RELS)

    def descs(ins, outs, sems):
        send, recv = sems
        x, y, c = _here()
        me = 4 * x + 2 * y + c
        pairs = []
        for k in range(n):
            for i, (dx, dy, dc) in enumerate(RELS):
                dev = (_flip(x, dx), _flip(y, dy), _flip(c, dc))
                peer = 4 * dev[0] + 2 * dev[1] + dev[2]
                pairs.append((_remote(ins[k].at[peer], outs[k].at[me], send, recv, nr * k + i, dev),
                              _remote(ins[k].at[peer], outs[k].at[peer], send, recv, nr * k + i, dev)))
        return [], pairs

    return _Task(ps, [jax.ShapeDtypeStruct(p.shape, p.dtype) for p in ps],
                 [pltpu.SemaphoreType.DMA((nr * n,)), pltpu.SemaphoreType.DMA((nr * n,))], descs)


def _a2a_x_task(ps, xv):
    n = len(ps)
    nr = len(RELS)

    def descs(ins, outs, sems):
        send, recv = sems
        x, y, c = _here()
        me = 4 * x + 2 * y + c
        pairs = []
        for k in range(n):
            for i, (dx, dy, dc) in enumerate(RELS):
                dev = (_flip(x, dx), _flip(y, dy), _flip(c, dc))
                peer = 4 * dev[0] + 2 * dev[1] + dev[2]
                piece = ins[k].at[2 * dev[1] + dev[2]]
                pairs.append((_remote(piece, outs[k].at[me], send, recv, nr * k + i, dev),
                              _remote(piece, outs[k].at[peer], send, recv, nr * k + i, dev),
                              dev[0] == xv, x == xv))
        return [], pairs

    return _Task(ps, [jax.ShapeDtypeStruct((N_DEV,) + p.shape[1:], p.dtype) for p in ps],
                 [pltpu.SemaphoreType.DMA((nr * n,)), pltpu.SemaphoreType.DMA((nr * n,))], descs)


def _share_task(vs):
    n = len(vs)
    nr = len(RELS)

    def descs(ins, outs, sems):
        send, recv, loc = sems
        x, y, c = _here()
        me = 4 * x + 2 * y + c
        locs, pairs = [], []
        for k in range(n):
            locs.append(functools.partial(pltpu.make_async_copy, ins[k], outs[k].at[me], loc.at[k]))
            for i, (dx, dy, dc) in enumerate(RELS):
                dev = (_flip(x, dx), _flip(y, dy), _flip(c, dc))
                peer = 4 * dev[0] + 2 * dev[1] + dev[2]
                pairs.append((_remote(ins[k], outs[k].at[me], send, recv, nr * k + i, dev),
                              _remote(ins[k], outs[k].at[peer], send, recv, nr * k + i, dev)))
        return locs, pairs

    return _Task(vs, [jax.ShapeDtypeStruct((N_DEV,) + v.shape, v.dtype) for v in vs],
                 [pltpu.SemaphoreType.DMA((nr * n,)), pltpu.SemaphoreType.DMA((nr * n,)), pltpu.SemaphoreType.DMA((n,))],
                 descs)


def _join_task(ss):
    n = len(ss)

    def descs(ins, outs, sems):
        send, recv = sems
        x, y, c = _here()
        pairs = []
        for k in range(n):
            cp = _remote(ins[k], outs[k], send, recv, k, (x, y, 1 - c))
            pairs.append((cp, cp))
        return [], pairs

    return _Task(ss, [jax.ShapeDtypeStruct(s.shape, s.dtype) for s in ss],
                 [pltpu.SemaphoreType.DMA((n,)), pltpu.SemaphoreType.DMA((n,))], descs)


def _sum_pieces(recv, own, me, name):
    n, h, c = recv.shape
    tr = h

    def body(me_ref, r_ref, o_ref, out_ref):
        s = pl.program_id(1)
        val = jnp.where(s == me_ref[0], o_ref[0], r_ref[0]).astype(F32)

        @pl.when(s == 0)
        def _():
            out_ref[...] = val

        @pl.when(s > 0)
        def _():
            out_ref[...] += val

    def other(s, m):
        return jnp.where(s == m[0], (s + 1) % n, s)

    return _call(
        body, name=name,
        grid_spec=pltpu.PrefetchScalarGridSpec(
            num_scalar_prefetch=1, grid=(h // tr, n),
            in_specs=[pl.BlockSpec((1, tr, c), lambda i, s, m: (other(s, m), i, 0)),
                      pl.BlockSpec((1, tr, c), lambda i, s, m: (m[1], i, 0))],
            out_specs=pl.BlockSpec((tr, c), lambda i, s, m: (i, 0))),
        out_shape=jax.ShapeDtypeStruct((h, c), F32),
        compiler_params=_cp(),
    )(me, recv, own)


def _adamw_update(gg, w, m, v):
    c1 = 1.0 / (1.0 - ADAM_B1 ** ADAM_STEP)
    c2 = 1.0 / (1.0 - ADAM_B2 ** ADAM_STEP)
    nm = ADAM_B1 * m + (1.0 - ADAM_B1) * gg
    nv = ADAM_B2 * v + (1.0 - ADAM_B2) * (gg * gg)
    return -ADAM_LR * ((nm * c1) / (jnp.sqrt(nv * c2) + ADAM_EPS) + ADAM_WD * w), nm, nv


def _adamw_small(gvec8, gmeta8, lp8, chip, ws, ms, vs, name):
    na = len(ws)

    def dev_sum(ref):
        acc = ref[0]
        for s in range(1, N_DEV):
            acc = acc + ref[s]
        return acc

    def body(c_ref, gv_ref, gm_ref, lp_ref, *refs):
        w_refs, m_refs, v_refs = refs[:na], refs[na:2 * na], refs[2 * na:3 * na]
        l_ref = refs[3 * na]
        outs = refs[3 * na + 1:]
        g_refs, d_refs, nm_refs, nv_refs = outs[:na], outs[na:2 * na], outs[2 * na:3 * na], outs[3 * na:]
        l_ref[...] = dev_sum(lp_ref)
        gvec = dev_sum(gv_ref)
        for k in range(na):
            gg = gvec[k:k + 1, 0:ws[k].shape[1]] if k < na - 1 else dev_sum(gm_ref)
            g_refs[k][...] = gg
            d_refs[k][...], nm_refs[k][...], nv_refs[k][...] = _adamw_update(gg, w_refs[k][...], m_refs[k][...], v_refs[k][...])

    whole = lambda a: pl.BlockSpec(a.shape, lambda i, c: (0,) * a.ndim)
    arrs = list(ws) + list(ms) + list(vs)
    res = _call(
        body, name=name,
        grid_spec=pltpu.PrefetchScalarGridSpec(
            num_scalar_prefetch=1, grid=(1,),
            in_specs=[whole(gvec8), pl.BlockSpec((N_DEV, NMETA, D // N_CHIPS), lambda i, c: (0, 0, c[0])), whole(lp8)]
                     + [whole(a) for a in arrs],
            out_specs=[pl.BlockSpec((1, D), lambda i, c: (0, 0))] + [whole(a) for a in ws] * 4),
        out_shape=[jax.ShapeDtypeStruct((1, D), F32)] + [jax.ShapeDtypeStruct(a.shape, F32) for a in ws] * 4,
        compiler_params=_cp(),
    )(chip, gvec8, gmeta8, lp8, *arrs)
    return res[0], [list(res[1 + i * na:1 + (i + 1) * na]) for i in range(4)]


def _adamw_halves(wt, mine, theirs, m, v, core, name):
    r, c = wt.shape
    h = r // 2
    tr = _tile(h, (256, 176, 128, 64))
    nh = h // tr

    def body(c_ref, w_ref, a_ref, b_ref, m_ref, v_ref, g_ref, d_ref, nm_ref, nv_ref):
        gg = jnp.where(pl.program_id(0) // nh == c_ref[0], a_ref[...], b_ref[...])
        g_ref[...] = gg
        d_ref[...], nm_ref[...], nv_ref[...] = _adamw_update(gg, w_ref[...], m_ref[...], v_ref[...])

    full = pl.BlockSpec((tr, c), lambda i, cr: (i, 0))
    half = pl.BlockSpec((tr, c), lambda i, cr: (i % nh, 0))
    return _call(
        body, name=name,
        grid_spec=pltpu.PrefetchScalarGridSpec(
            num_scalar_prefetch=1, grid=(2 * nh,),
            in_specs=[full, half, half, full, full], out_specs=[full] * 4),
        out_shape=[jax.ShapeDtypeStruct((r, c), F32)] * 4,
        compiler_params=_cp(),
    )(core, wt, mine, theirs, m, v)


MATS = ["ffn1_w_gu", "ffn1_w_down", "w_in", "mla_w_uq", "mla_w_ukv", "w_branch_fox", "w_branch_mla",
        "w_out", "ffn2_w_gu", "ffn2_w_down"]
VECS = ["ffn1_norm", "mix_norm", "b_forget", "b_gate", "fox_q_norm", "fox_k_norm", "mla_cq_norm",
        "mla_ckv_norm", "mla_q_norm", "mla_k_norm", "ffn2_norm"]
WEIGHTS = ["meta_tokens", "ffn1_norm", "ffn1_w_gu", "ffn1_w_down", "mix_norm", "w_in", "b_forget", "b_gate",
           "fox_q_norm", "fox_k_norm", "mla_cq_norm", "mla_w_uq", "mla_ckv_norm", "mla_w_ukv", "mla_q_norm",
           "mla_k_norm", "w_branch_fox", "w_branch_mla", "w_out", "ffn2_norm", "ffn2_w_gu", "ffn2_w_down"]


VEC_LANES = 2048


def _stack_vectors(parts):
    rows = [_pad_lanes(p, 0, VEC_LANES) for p in parts]
    rows.append(jnp.zeros((-len(parts) % 8, VEC_LANES), F32))
    return jnp.concatenate(rows, axis=0)


def kernel(x, meta_tokens, ffn1_norm, ffn1_w_gu, ffn1_w_down, mix_norm, w_in, b_forget, b_gate, fox_q_norm, fox_k_norm, mla_cq_norm, mla_w_uq, mla_ckv_norm, mla_w_ukv, mla_q_norm, mla_k_norm, w_branch_fox, w_branch_mla, w_out, ffn2_norm, ffn2_w_gu, ffn2_w_down, loss_target, m_meta_tokens, m_ffn1_norm, m_ffn1_w_gu, m_ffn1_w_down, m_mix_norm, m_w_in, m_b_forget, m_b_gate, m_fox_q_norm, m_fox_k_norm, m_mla_cq_norm, m_mla_w_uq, m_mla_ckv_norm, m_mla_w_ukv, m_mla_q_norm, m_mla_k_norm, m_w_branch_fox, m_w_branch_mla, m_w_out, m_ffn2_norm, m_ffn2_w_gu, m_ffn2_w_down, v_meta_tokens, v_ffn1_norm, v_ffn1_w_gu, v_ffn1_w_down, v_mix_norm, v_w_in, v_b_forget, v_b_gate, v_fox_q_norm, v_fox_k_norm, v_mla_cq_norm, v_mla_w_uq, v_mla_ckv_norm, v_mla_w_ukv, v_mla_q_norm, v_mla_k_norm, v_w_branch_fox, v_w_branch_mla, v_w_out, v_ffn2_norm, v_ffn2_w_gu, v_ffn2_w_down):
    a = dict(locals())
    wts = {n: a[n] for n in WEIGHTS}
    ms = {n: a["m_" + n] for n in WEIGHTS}
    vs = {n: a["v_" + n] for n in WEIGHTS}
    cx, cy, cc = lax.axis_index("x"), lax.axis_index("y"), lax.axis_index("c")
    chip = 2 * cx + cy

    shards = {n: wts[n][0].astype(BF16) for n in MATS}
    first = _run_tasks([_SplitGather([shards[n] for n in G_FFN1] + [meta_tokens], wide=(0,))], "gather_ffn1")[0]
    gath = dict(zip(G_FFN1, first[:-1]))
    meta_full = _cols_from_shards(first[-1])

    _, grad_x, _, _, gred, (gvec8, gmeta8, lp8) = _step(x, loss_target, meta_full, {n: wts[n] for n in VECS}, gath, shards)

    sm_names = VECS + ["meta_tokens"]
    lsum, sm = _adamw_small(gvec8, gmeta8, lp8, chip.reshape(1).astype(jnp.int32), [wts[n] for n in sm_names],
                            [ms[n] for n in sm_names], [vs[n] for n in sm_names], "adamw_small")
    loss = jnp.sum(lsum)

    grads, delta, new_m, new_v = {}, {}, {}, {}
    core = cc.reshape(1).astype(jnp.int32)
    for n in MATS:
        shp = wts[n].shape
        mine, theirs = gred[n]
        res = _adamw_halves(wts[n][0], mine, theirs, ms[n][0], vs[n][0], core, "adamw_" + n)
        grads[n], delta[n], new_m[n], new_v[n] = (t.reshape(shp) for t in res)
    for k, n in enumerate(sm_names):
        grads[n], delta[n], new_m[n], new_v[n] = (sm[i][k] for i in range(4))

    return (loss, grad_x, *[grads[n] for n in WEIGHTS], *[delta[n] for n in WEIGHTS],
            *[new_m[n] for n in WEIGHTS], *[new_v[n] for n in WEIGHTS])
```

```python
import functools

import jax
import jax.numpy as jnp
from jax import lax
from jax.experimental import pallas as pl
from jax.experimental.pallas import tpu as pltpu

F32 = jnp.float32
BF16 = jnp.bfloat16
MESH = pl.DeviceIdType.MESH

D = 1024
DFF = 2816
FH = DFF // 2
NMETA = 16
MPAD = 128
EPS = 1e-6
NH = 8
FOXW = 512
QR = 256
KVR = 128
ROPE = 32
MLA_QK = 96
PROJW = 4096
ROPE_THETA = 10000.0
N_CHIPS = 4
N_DEV = 8

ADAM_LR = 0.001
ADAM_B1 = 0.9
ADAM_B2 = 0.999
ADAM_EPS = 1e-08
ADAM_WD = 0.01
ADAM_STEP = 10

VMEM_LIMIT = 56 * 2**20
ATTN_TQ = 256
GRAD_DTYPE = BF16

NT = (((1,), (1,)), ((), ()))
TN = (((0,), (0,)), ((), ()))


def _call(body, **kw):
    return pl.pallas_call(body, **kw)


def _cp(**kw):
    return pltpu.CompilerParams(vmem_limit_bytes=VMEM_LIMIT, **kw)


HBM = pl.BlockSpec(memory_space=pltpu.HBM)


class _Task:
    def __init__(self, ins, out_shapes, sems, descs):
        self.ins, self.out_shapes, self.sems, self.descs = list(ins), list(out_shapes), list(sems), descs

    @staticmethod
    def _if(cond, action):
        if cond is None:
            action()
        else:
            pl.when(cond)(action)

    def start(self, ins, outs, sems):
        locs, pairs = self.descs(ins, outs, sems)
        for lc in locs:
            lc().start()
        for snd, _, *ok in pairs:
            self._if(ok[0] if ok else None, lambda snd=snd: snd().start())

    def wait(self, ins, outs, sems):
        locs, pairs = self.descs(ins, outs, sems)
        for _, rcv, *ok in pairs:
            self._if(ok[1] if ok else None, lambda rcv=rcv: rcv().wait_recv())
        for snd, _, *ok in pairs:
            self._if(ok[0] if ok else None, lambda snd=snd: snd().wait_send())
        for lc in locs:
            lc().wait()


def _call_tasks(body, tasks, *, name, grid, in_specs, out_specs, out_shape, args, scratch_shapes=()):
    in_specs, out_specs, out_shape, scratch_shapes = map(list, (in_specs, out_specs, out_shape, scratch_shapes))
    n_in, n_out, n_sc = len(in_specs), len(out_specs), len(scratch_shapes)
    t_in = [len(t.ins) for t in tasks]
    t_out = [len(t.out_shapes) for t in tasks]
    t_sem = [len(t.sems) for t in tasks]

    def wrapped(*refs):
        pos = [0]

        def take(n):
            pos[0] += n
            return refs[pos[0] - n:pos[0]]

        ins, tins = take(n_in), [take(n) for n in t_in]
        outs, touts = take(n_out), [take(n) for n in t_out]
        sc, tsems = take(n_sc), [take(n) for n in t_sem]
        if tasks:
            first = functools.reduce(jnp.logical_and, [pl.program_id(a) == 0 for a in range(len(grid))])
            last = functools.reduce(jnp.logical_and, [pl.program_id(a) == grid[a] - 1 for a in range(len(grid))])

            @pl.when(first)
            def _():
                for t, a, b, s in zip(tasks, tins, touts, tsems):
                    t.start(a, b, s)

        body(*ins, *outs, *sc)
        if tasks:
            @pl.when(last)
            def _():
                for t, a, b, s in zip(tasks, tins, touts, tsems):
                    t.wait(a, b, s)

    res = _call(
        wrapped, name=name, grid=grid,
        in_specs=in_specs + [HBM] * sum(t_in), out_specs=out_specs + [HBM] * sum(t_out),
        out_shape=out_shape + [s for t in tasks for s in t.out_shapes],
        scratch_shapes=scratch_shapes + [s for t in tasks for s in t.sems],
        compiler_params=_cp(),
    )(*args, *[a for t in tasks for a in t.ins])
    res = list(res)
    touts, pos = [], n_out
    for n in t_out:
        touts.append(res[pos:pos + n])
        pos += n
    return res[:n_out], touts


def _run_tasks(tasks, name):
    t_in = [len(t.ins) for t in tasks]
    t_out = [len(t.out_shapes) for t in tasks]
    t_sem = [len(t.sems) for t in tasks]

    def body(*refs):
        pos = [0]

        def take(n):
            pos[0] += n
            return refs[pos[0] - n:pos[0]]

        tins, touts, tsems = [take(n) for n in t_in], [take(n) for n in t_out], [take(n) for n in t_sem]
        for t, a, b, s in zip(tasks, tins, touts, tsems):
            t.start(a, b, s)
        for t, a, b, s in zip(tasks, tins, touts, tsems):
            t.wait(a, b, s)

    res = list(_call(
        body, name=name, in_specs=[HBM] * sum(t_in), out_specs=[HBM] * sum(t_out),
        out_shape=[s for t in tasks for s in t.out_shapes],
        scratch_shapes=[s for t in tasks for s in t.sems],
    )(*[a for t in tasks for a in t.ins]))
    touts, pos = [], 0
    for n in t_out:
        touts.append(res[pos:pos + n])
        pos += n
    return touts


def _tile(n, cands):
    for c in cands:
        if n % c == 0:
            return c
    raise ValueError(f"no tile for {n} among {cands}")


def _dot(a, b, dims=None):
    if dims is None:
        return jnp.dot(a, b, preferred_element_type=F32)
    return lax.dot_general(a, b, dims, preferred_element_type=F32)


def _lane(shape):
    return lax.broadcasted_iota(jnp.int32, shape, len(shape) - 1)


def _seg_ones(w, log2_seg):
    r = lax.shift_right_logical(lax.broadcasted_iota(jnp.int32, (w, w), 0), log2_seg)
    c = lax.shift_right_logical(lax.broadcasted_iota(jnp.int32, (w, w), 1), log2_seg)
    return jnp.where(r == c, 1.0, 0.0).astype(BF16)


def _seg_sum(x, ones):
    hi = x.astype(BF16)
    r1 = x - hi.astype(F32)
    mid = r1.astype(BF16)
    lo = (r1 - mid.astype(F32)).astype(BF16)
    return _dot(hi, ones) + _dot(mid, ones) + _dot(lo, ones)


def _lane_sum(x, ones):
    return jnp.sum(x, axis=-1, keepdims=True) if ones is None else _seg_sum(x, ones)


def _rms(x, gain, n, ones=None):
    r = lax.rsqrt(_lane_sum(x * x, ones) * (1.0 / n) + EPS)
    xh = x * r
    return xh * gain, xh, r


def _rms_bwd(dy, xh, r, gain, n, ones=None):
    dxh = dy * gain
    return r * (dxh - xh * (_lane_sum(dxh * xh, ones) * (1.0 / n)))


def _rope_swap(x):
    ln = _lane(x.shape)
    sw = jnp.where(ln < 80, pltpu.roll(x, 112, 1), pltpu.roll(x, 16, 1))
    return jnp.where(jnp.logical_and(ln >= 64, ln < 96), sw, 0.0)


def _colsum(x):
    return jnp.sum(x, axis=0, keepdims=True)


def _ffn_weight_specs():
    once = pl.Buffered(1)
    return [pl.BlockSpec((D, DFF), lambda i: (0, 0), pipeline_mode=once),
            pl.BlockSpec((D, DFF), lambda i: (0, 1), pipeline_mode=once),
            pl.BlockSpec((DFF, D), lambda i: (0, 0), pipeline_mode=once)]


def _ffn_fwd(h, norm, wgu, wd, name, tasks=()):
    T = h.shape[0]
    tm = _tile(T, (512, 384, 256, 128))

    def body(h_ref, n_ref, wg_ref, wu_ref, wd_ref, o_ref):
        x = h_ref[...]
        u, _, _ = _rms(x, n_ref[...], D)
        ub = u.astype(BF16)
        g = _dot(ub, wg_ref[...])
        p = _dot(ub, wu_ref[...])
        a = (g * jax.nn.sigmoid(g)) * p
        o_ref[...] = x + 0.5 * _dot(a.astype(BF16), wd_ref[...])

    (out,), touts = _call_tasks(
        body, tasks, name=name, grid=(T // tm,),
        in_specs=[pl.BlockSpec((tm, D), lambda i: (i, 0)), pl.BlockSpec((1, D), lambda i: (0, 0))] + _ffn_weight_specs(),
        out_specs=[pl.BlockSpec((tm, D), lambda i: (i, 0))],
        out_shape=[jax.ShapeDtypeStruct((T, D), F32)],
        args=(h, norm, wgu, wgu, wd))
    return out, touts


def _ffn_bwd(h, dout, norm, wgu, wd, name, tasks=()):
    T = h.shape[0]
    tm = _tile(T, (256, 128))

    def body(h_ref, d_ref, n_ref, wg_ref, wu_ref, wd_ref, dh_ref, u_ref, a_ref, dgp_ref, dn_ref):
        @pl.when(pl.program_id(0) == 0)
        def _():
            dn_ref[...] = jnp.zeros_like(dn_ref)

        u, xh, r = _rms(h_ref[...], n_ref[...], D)
        ub = u.astype(BF16)
        u_ref[...] = ub
        g = _dot(ub, wg_ref[...])
        p = _dot(ub, wu_ref[...])
        s = jax.nn.sigmoid(g)
        sl = g * s
        dz = (0.5 * d_ref[...]).astype(BF16)
        da = _dot(dz, wd_ref[...], NT)
        dp = da * sl
        dg = (da * p) * (s * (1.0 + g * (1.0 - s)))
        a_ref[...] = (sl * p).astype(BF16)
        dgb = dg.astype(BF16)
        dpb = dp.astype(BF16)
        dgp_ref[:, :DFF] = dgb
        dgp_ref[:, DFF:] = dpb
        du = _dot(dgb, wg_ref[...], NT) + _dot(dpb, wu_ref[...], NT)
        dn_ref[...] += _colsum(du * xh)
        dh_ref[...] = d_ref[...] + _rms_bwd(du, xh, r, n_ref[...], D)

    row = lambda w: pl.BlockSpec((tm, w), lambda i: (i, 0))
    return _call_tasks(
        body, tasks, name=name, grid=(T // tm,),
        in_specs=[row(D), row(D), pl.BlockSpec((1, D), lambda i: (0, 0))] + _ffn_weight_specs(),
        out_specs=[row(D), row(D), row(DFF), row(2 * DFF), pl.BlockSpec((1, D), lambda i: (0, 0))],
        out_shape=[jax.ShapeDtypeStruct((T, D), F32),
                   jax.ShapeDtypeStruct((T, D), BF16),
                   jax.ShapeDtypeStruct((T, DFF), BF16),
                   jax.ShapeDtypeStruct((T, 2 * DFF), BF16),
                   jax.ShapeDtypeStruct((1, D), F32)],
        args=(h, dout, norm, wgu, wgu, wd))


def _wgrad(x, y, name, scale=1.0, bk=None, bn=None, shards=0, bt=512, rows=None, tasks=()):
    T = x.shape[0]
    N = y.shape[1]
    bk = bk or x.shape[1]
    bn = bn or N
    k0, nk = rows or (0, x.shape[1] // bk)
    K = nk * bk
    bt = _tile(T, (bt, 512, 384, 256, 128))
    nt = T // bt

    def body(x_ref, y_ref, o_ref, acc_ref):
        t = pl.program_id(2)

        @pl.when(t == 0)
        def _():
            acc_ref[...] = jnp.zeros_like(acc_ref)

        acc_ref[...] += _dot(x_ref[...].astype(BF16), y_ref[...].astype(BF16), TN)

        @pl.when(t == nt - 1)
        def _():
            res = (acc_ref[...] * scale).astype(o_ref.dtype)
            if shards:
                w = bn // shards
                for s in range(shards):
                    o_ref[s] = res[:, s * w:(s + 1) * w]
            else:
                o_ref[...] = res

    if shards:
        out_spec = pl.BlockSpec((shards, bk, bn // shards), lambda i, j, t: (j, i, 0))
        out_shape = jax.ShapeDtypeStruct((N * shards // bn, K, bn // shards), GRAD_DTYPE)
    else:
        out_spec = pl.BlockSpec((bk, bn), lambda i, j, t: (i, j))
        out_shape = jax.ShapeDtypeStruct((K, N), GRAD_DTYPE)
    (out,), touts = _call_tasks(
        body, tasks, name=name, grid=(nk, N // bn, nt),
        in_specs=[pl.BlockSpec((bt, bk), lambda i, j, t: (t, i + k0)),
                  pl.BlockSpec((bt, bn), lambda i, j, t: (t, j))],
        out_specs=[out_spec], out_shape=[out_shape],
        scratch_shapes=[pltpu.VMEM((bk, bn), F32)],
        args=(x, y))
    return out, touts


def _inproj_fwd(h, norm, w, name):
    T = h.shape[0]
    tm = _tile(T, (512, 384, 256, 128))

    def body(h_ref, n_ref, w_ref, o_ref, u_ref):
        u, _, _ = _rms(h_ref[...], n_ref[...], D)
        ub = u.astype(BF16)
        u_ref[...] = ub
        o_ref[...] = _dot(ub, w_ref[...])

    return _call(
        body, name=name, grid=(T // tm,),
        in_specs=[pl.BlockSpec((tm, D), lambda i: (i, 0)),
                  pl.BlockSpec((1, D), lambda i: (0, 0)),
                  pl.BlockSpec((D, PROJW), lambda i: (0, 0), pipeline_mode=pl.Buffered(1))],
        out_specs=[pl.BlockSpec((tm, PROJW), lambda i: (i, 0)),
                   pl.BlockSpec((tm, D), lambda i: (i, 0))],
        out_shape=[jax.ShapeDtypeStruct((T, PROJW), F32), jax.ShapeDtypeStruct((T, D), BF16)],
        compiler_params=_cp(),
    )(h, norm, w)


def _inproj_bwd(h, dres, dlo, dhi, norm, w, name):
    T = h.shape[0]
    tm = _tile(T, (512, 384, 256, 128))
    hw = PROJW // 2

    def body(h_ref, d_ref, lo_ref, hi_ref, n_ref, wlo_ref, whi_ref, dh_ref, dn_ref):
        @pl.when(pl.program_id(0) == 0)
        def _():
            dn_ref[...] = jnp.zeros_like(dn_ref)

        _, xh, r = _rms(h_ref[...], n_ref[...], D)
        du = _dot(lo_ref[...], wlo_ref[...], NT) + _dot(hi_ref[...], whi_ref[...], NT)
        dn_ref[...] += _colsum(du * xh)
        dh_ref[...] = d_ref[...] + _rms_bwd(du, xh, r, n_ref[...], D)

    return _call(
        body, name=name, grid=(T // tm,),
        in_specs=[pl.BlockSpec((tm, D), lambda i: (i, 0)),
                  pl.BlockSpec((tm, D), lambda i: (i, 0)),
                  pl.BlockSpec((tm, hw), lambda i: (i, 0)),
                  pl.BlockSpec((tm, hw), lambda i: (i, 0)),
                  pl.BlockSpec((1, D), lambda i: (0, 0)),
                  pl.BlockSpec((D, hw), lambda i: (0, 0)),
                  pl.BlockSpec((D, hw), lambda i: (0, 1))],
        out_specs=[pl.BlockSpec((tm, D), lambda i: (i, 0)),
                   pl.BlockSpec((1, D), lambda i: (0, 0))],
        out_shape=[jax.ShapeDtypeStruct((T, D), F32), jax.ShapeDtypeStruct((1, D), F32)],
        compiler_params=_cp(),
    )(h, dres, dlo, dhi, norm, w, w)


C_FQ, C_FK, C_FV, C_CQ, C_CKV, C_MISC, C_GATE = 0, 512, 1024, 1536, 1792, 1920, 2048
L_KR, L_FL = 64, 96


def _prep_fwd(proj, rc, rs, gfq, gfk, gcq, gckv, gmq, gmk, bfv, wuq, wuk, wuv, name):
    T = proj.shape[0]
    tm = _tile(T, (256, 128))

    def body(p_ref, rc_ref, rs_ref, gfq_ref, gfk_ref, gcq_ref, gckv_ref, gmq_ref, gmk_ref, bf_ref,
             wuq_ref, wuk_ref, wuv_ref, fq_ref, fk_ref, fv_ref, qm_ref, km_ref, vm_ref, lf_ref):
        o64, o128, o256 = _seg_ones(128, 6), _seg_ones(128, 7), _seg_ones(256, 8)
        for blk in range(4):
            for (c0, g_ref, o_ref) in ((C_FQ, gfq_ref, fq_ref), (C_FK, gfk_ref, fk_ref)):
                x = p_ref[:, c0 + 128 * blk:c0 + 128 * (blk + 1)]
                fn, _, _ = _rms(x, g_ref[...], 64, o64)
                o_ref[:, 128 * blk:128 * (blk + 1)] = fn.astype(BF16)
        fv_ref[...] = p_ref[:, C_FV:C_FV + 512].astype(BF16)

        rcv = rc_ref[...]
        rsv = rs_ref[...]
        cqn, _, _ = _rms(p_ref[:, C_CQ:C_CQ + QR], gcq_ref[...], QR, o256)
        qpre = _dot(cqn.astype(BF16), wuq_ref[...])
        ckvn, _, _ = _rms(p_ref[:, C_CKV:C_CKV + KVR], gckv_ref[...], KVR, o128)
        ckvb = ckvn.astype(BF16)
        kpre = _dot(ckvb, wuk_ref[...])
        vm_ref[...] = _dot(ckvb, wuv_ref[...]).astype(BF16)
        misc = p_ref[:, C_MISC:C_MISC + 128]
        ln = _lane(misc.shape)
        kr = jnp.where(jnp.logical_and(ln >= L_KR, ln < L_KR + ROPE), misc, 0.0)
        for hh in range(NH):
            sl = slice(128 * hh, 128 * (hh + 1))
            qn, _, _ = _rms(qpre[:, sl], gmq_ref[...], MLA_QK, o128)
            qm_ref[:, sl] = (qn * rcv + _rope_swap(qn) * rsv).astype(BF16)
            kn, _, _ = _rms(kpre[:, sl] + kr, gmk_ref[...], MLA_QK, o128)
            km_ref[:, sl] = (kn * rcv + _rope_swap(kn) * rsv).astype(BF16)
        z = misc + bf_ref[...]
        lf_ref[...] = jnp.minimum(z, 0.0) - jnp.log(1.0 + jnp.exp(-jnp.abs(z)))

    row = lambda w: pl.BlockSpec((tm, w), lambda i: (i, 0))
    full = lambda a: pl.BlockSpec(a.shape, lambda i: (0, 0))
    return _call(
        body, name=name, grid=(T // tm,),
        in_specs=[row(PROJW // 2), row(128), row(128)] + [full(a) for a in (gfq, gfk, gcq, gckv, gmq, gmk, bfv, wuq, wuk, wuv)],
        out_specs=[row(512), row(512), row(512), row(1024), row(1024), row(512), row(128)],
        out_shape=[jax.ShapeDtypeStruct((T, 512), BF16), jax.ShapeDtypeStruct((T, 512), BF16),
                   jax.ShapeDtypeStruct((T, 512), BF16), jax.ShapeDtypeStruct((T, 1024), BF16),
                   jax.ShapeDtypeStruct((T, 1024), BF16), jax.ShapeDtypeStruct((T, 512), BF16),
                   jax.ShapeDtypeStruct((T, 128), F32)],
        compiler_params=_cp(),
    )(proj, rc, rs, gfq, gfk, gcq, gckv, gmq, gmk, bfv, wuq, wuk, wuv)


def _prep_bwd(proj, rc, rs, gfq, gfk, gcq, gckv, gmq, gmk, bfv, wuq, wuk, wuv,
              dfq, dfk, dfv, dqm, dkm, dvm, dlf, name):
    T = proj.shape[0]
    tm = _tile(T, (256, 128))

    def body(p_ref, rc_ref, rs_ref, gfq_ref, gfk_ref, gcq_ref, gckv_ref, gmq_ref, gmk_ref, bf_ref,
             wuq_ref, wuk_ref, wuv_ref, dfq_ref, dfk_ref, dfv_ref, dqm_ref, dkm_ref, dvm_ref, dlf_ref,
             dp_ref, dgfq_ref, dgfk_ref, dgcq_ref, dgckv_ref, dgmq_ref, dgmk_ref, dbf_ref,
             dwuq_ref, dwuk_ref, dwuv_ref, dqpre_sc, dkpre_sc):
        accs = (dgfq_ref, dgfk_ref, dgcq_ref, dgckv_ref, dgmq_ref, dgmk_ref, dbf_ref, dwuq_ref, dwuk_ref, dwuv_ref)

        @pl.when(pl.program_id(0) == 0)
        def _():
            for a in accs:
                a[...] = jnp.zeros_like(a)

        o64, o128, o256 = _seg_ones(128, 6), _seg_ones(128, 7), _seg_ones(256, 8)
        for (c0, g_ref, d_ref, dg_ref) in ((C_FQ, gfq_ref, dfq_ref, dgfq_ref), (C_FK, gfk_ref, dfk_ref, dgfk_ref)):
            dg = jnp.zeros((1, 128), F32)
            for blk in range(4):
                x = p_ref[:, c0 + 128 * blk:c0 + 128 * (blk + 1)]
                _, xh, r = _rms(x, g_ref[...], 64, o64)
                dy = d_ref[:, 128 * blk:128 * (blk + 1)]
                dg = dg + _colsum(dy * xh)
                dp_ref[:, c0 + 128 * blk:c0 + 128 * (blk + 1)] = _rms_bwd(dy, xh, r, g_ref[...], 64, o64).astype(BF16)
            dg_ref[...] += dg
        dp_ref[:, C_FV:C_FV + 512] = dfv_ref[...].astype(BF16)

        rcv = rc_ref[...]
        rsv = rs_ref[...]
        cqn, cqh, cqr = _rms(p_ref[:, C_CQ:C_CQ + QR], gcq_ref[...], QR, o256)
        cqb = cqn.astype(BF16)
        qpre = _dot(cqb, wuq_ref[...])
        dgq = jnp.zeros((1, 128), F32)
        for hh in range(NH):
            sl = slice(128 * hh, 128 * (hh + 1))
            _, xh, r = _rms(qpre[:, sl], gmq_ref[...], MLA_QK, o128)
            dout = dqm_ref[:, sl]
            dqn = dout * rcv + _rope_swap(dout * rsv)
            dgq = dgq + _colsum(dqn * xh)
            dqpre_sc[:, sl] = _rms_bwd(dqn, xh, r, gmq_ref[...], MLA_QK, o128).astype(BF16)
        dgmq_ref[...] += dgq
        dqpre = dqpre_sc[...]
        dwuq_ref[...] += _dot(cqb, dqpre, TN)
        dcqn = _dot(dqpre, wuq_ref[...], NT)
        dgcq_ref[...] += _colsum(dcqn * cqh)
        dp_ref[:, C_CQ:C_CQ + QR] = _rms_bwd(dcqn, cqh, cqr, gcq_ref[...], QR, o256).astype(BF16)

        ckvn, ckvh, ckvr = _rms(p_ref[:, C_CKV:C_CKV + KVR], gckv_ref[...], KVR, o128)
        ckvb = ckvn.astype(BF16)
        kpre = _dot(ckvb, wuk_ref[...])
        misc = p_ref[:, C_MISC:C_MISC + 128]
        ln = _lane(misc.shape)
        is_kr = jnp.logical_and(ln >= L_KR, ln < L_KR + ROPE)
        kr = jnp.where(is_kr, misc, 0.0)
        dgk = jnp.zeros((1, 128), F32)
        dkr = jnp.zeros(misc.shape, F32)
        for hh in range(NH):
            sl = slice(128 * hh, 128 * (hh + 1))
            _, xh, r = _rms(kpre[:, sl] + kr, gmk_ref[...], MLA_QK, o128)
            dout = dkm_ref[:, sl]
            dkn = dout * rcv + _rope_swap(dout * rsv)
            dgk = dgk + _colsum(dkn * xh)
            dkx = _rms_bwd(dkn, xh, r, gmk_ref[...], MLA_QK, o128)
            dkr = dkr + jnp.where(is_kr, dkx, 0.0)
            dkpre_sc[:, sl] = jnp.where(ln < 64, dkx, 0.0).astype(BF16)
        dgmk_ref[...] += dgk
        dkpre = dkpre_sc[...]
        dvmb = dvm_ref[...].astype(BF16)
        dwuk_ref[...] += _dot(ckvb, dkpre, TN)
        dwuv_ref[...] += _dot(ckvb, dvmb, TN)
        dckvn = _dot(dkpre, wuk_ref[...], NT) + _dot(dvmb, wuv_ref[...], NT)
        dgckv_ref[...] += _colsum(dckvn * ckvh)
        dp_ref[:, C_CKV:C_CKV + KVR] = _rms_bwd(dckvn, ckvh, ckvr, gckv_ref[...], KVR, o128).astype(BF16)

        z = misc + bf_ref[...]
        dz = dlf_ref[...] * (1.0 - jax.nn.sigmoid(z))
        dbf_ref[...] += _colsum(dz)
        dp_ref[:, C_MISC:C_MISC + 128] = (dkr + dz).astype(BF16)

    row = lambda w: pl.BlockSpec((tm, w), lambda i: (i, 0))
    full = lambda a: pl.BlockSpec(a.shape, lambda i: (0, 0))
    small = (gfq, gfk, gcq, gckv, gmq, gmk, bfv, wuq, wuk, wuv)
    acc_shapes = [(1, 128), (1, 128), (1, QR), (1, KVR), (1, 128), (1, 128), (1, 128),
                  (QR, 1024), (KVR, 1024), (KVR, 512)]
    return _call(
        body, name=name, grid=(T // tm,),
        in_specs=[row(PROJW // 2), row(128), row(128)] + [full(a) for a in small]
                 + [row(512), row(512), row(512), row(1024), row(1024), row(512), row(128)],
        out_specs=[row(PROJW // 2)] + [pl.BlockSpec(s, lambda i: (0, 0)) for s in acc_shapes],
        out_shape=[jax.ShapeDtypeStruct((T, PROJW // 2), BF16)] + [jax.ShapeDtypeStruct(s, F32) for s in acc_shapes],
        scratch_shapes=[pltpu.VMEM((tm, 1024), BF16), pltpu.VMEM((tm, 1024), BF16)],
        compiler_params=_cp(),
    )(proj, rc, rs, *small, dfq, dfk, dfv, dqm, dkm, dvm, dlf)


def _scan_lanes(x, reverse):
    n = x.shape[-1]
    ln = _lane(x.shape)
    k = 1
    while k < n:
        if reverse:
            x = x + jnp.where(ln < n - k, pltpu.roll(x, n - k, x.ndim - 1), 0.0)
        else:
            x = x + jnp.where(ln >= k, pltpu.roll(x, k, x.ndim - 1), 0.0)
        k *= 2
    return x


def _forget_scan(lf, reverse, name):
    def body(x_ref, o_ref):
        x = x_ref[...]
        ln = _lane(x.shape)
        pad = jnp.logical_and(ln >= NMETA, ln < MPAD)
        o_ref[...] = jnp.where(pad, 0.0, _scan_lanes(jnp.where(pad, 0.0, x), reverse))

    return _call(body, name=name, out_shape=jax.ShapeDtypeStruct(lf.shape, F32), compiler_params=_cp())(lf)


def _attn_blocks(LP, tq):
    return [(0, MPAD, MPAD)] + [(MPAD + i * tq, tq, MPAD + (i + 1) * tq) for i in range((LP - MPAD) // tq)]


def _attn_scores(q_ref, k_ref, e, r0, rn, kend, wide, scale, bias):
    if wide:
        qe = q_ref[r0:r0 + rn, 128 * e:128 * (e + 1)]
        ke = k_ref[0:kend, 128 * e:128 * (e + 1)]
    else:
        qb = q_ref[r0:r0 + rn, :]
        mine = (_lane(qb.shape) < 64) if e == 0 else (_lane(qb.shape) >= 64)
        qe = jnp.where(mine, qb, jnp.zeros_like(qb))
        ke = k_ref[0:kend, :]
    s = _dot(qe, ke, NT) * scale
    if bias is not None:
        ct_ref, cr_ref = bias
        s = s + ct_ref[0, r0:r0 + rn, e:e + 1] - cr_ref[0, :, 0:kend]
    neg = -1e30
    if r0 == 0:
        qi = lax.broadcasted_iota(jnp.int32, (rn, kend), 0)
        ki = lax.broadcasted_iota(jnp.int32, (rn, kend), 1)
        s = jnp.where(jnp.logical_and(ki <= qi, ki < NMETA), s, neg)
    else:
        d0 = kend - rn
        head = jnp.where(_lane((rn, MPAD)) < NMETA, s[:, :MPAD], neg)
        qi = lax.broadcasted_iota(jnp.int32, (rn, rn), 0)
        diag = jnp.where(_lane((rn, rn)) <= qi, s[:, d0:], neg)
        s = jnp.concatenate([head] + ([s[:, MPAD:d0]] if d0 > MPAD else []) + [diag], axis=1)
    m = jnp.max(s, axis=-1, keepdims=True)
    p = jnp.exp(s - m)
    l = jnp.sum(p, axis=-1, keepdims=True)
    return qe, ke, p, l


def _attn_specs(B, LP, wide, has_bias):
    qw = 256 if wide else 128
    specs = [pl.BlockSpec((LP, qw), lambda b, hp: (b, hp)),
             pl.BlockSpec((LP, qw), lambda b, hp: (b, hp)),
             pl.BlockSpec((LP, 128), lambda b, hp: (b, hp))]
    bias_specs = []
    if has_bias:
        bias_specs = [pl.BlockSpec((1, LP, 2), lambda b, hp: (b * 4 + hp, 0, 0)),
                      pl.BlockSpec((1, 1, LP), lambda b, hp: (b * 8 + 2 * hp, 0, 0)),
                      pl.BlockSpec((1, 1, LP), lambda b, hp: (b * 8 + 2 * hp + 1, 0, 0))]
    return qw, specs, bias_specs


def _attn_fwd(q, k, v, bias, B, LP, wide, scale, name, tasks=()):
    T = q.shape[0]
    blocks = _attn_blocks(LP, ATTN_TQ)
    qw, specs, bias_specs = _attn_specs(B, LP, wide, bias is not None)

    def body(*refs):
        if bias is not None:
            q_ref, k_ref, v_ref, ct_ref, cr0_ref, cr1_ref, o_ref = refs
            crs = (cr0_ref, cr1_ref)
        else:
            q_ref, k_ref, v_ref, o_ref = refs
        for (r0, rn, kend) in blocks:
            outs = []
            for e in (0, 1):
                bs = (ct_ref, crs[e]) if bias is not None else None
                _, _, p, l = _attn_scores(q_ref, k_ref, e, r0, rn, kend, wide, scale, bs)
                outs.append(_dot(p.astype(BF16), v_ref[0:kend, :]) / l)
            o = jnp.where(_lane(outs[0].shape) < 64, outs[0], outs[1])
            o_ref[r0:r0 + rn, :] = o.astype(BF16)

    args = (q, k, v) + ((bias[0], bias[1], bias[1]) if bias is not None else ())
    (out,), touts = _call_tasks(
        body, tasks, name=name, grid=(B, 4),
        in_specs=specs + bias_specs,
        out_specs=[pl.BlockSpec((LP, 128), lambda b, hp: (b, hp))],
        out_shape=[jax.ShapeDtypeStruct((T, 512), BF16)],
        args=args)
    return out, touts


def _attn_bwd(q, k, v, do, bias, B, LP, wide, scale, name, tasks=()):
    T = q.shape[0]
    blocks = _attn_blocks(LP, ATTN_TQ)
    qw, specs, bias_specs = _attn_specs(B, LP, wide, bias is not None)
    has_bias = bias is not None

    def body(*refs):
        if has_bias:
            (q_ref, k_ref, v_ref, do_ref, ct_ref, cr0_ref, cr1_ref,
             dq_ref, dk_ref, dv_ref, dc0_ref, dc1_ref) = refs
            crs = (cr0_ref, cr1_ref)
            dcs = (dc0_ref, dc1_ref)
            dc0_ref[...] = jnp.zeros_like(dc0_ref)
            dc1_ref[...] = jnp.zeros_like(dc1_ref)
        else:
            q_ref, k_ref, v_ref, do_ref, dq_ref, dk_ref, dv_ref = refs
        dk_ref[...] = jnp.zeros_like(dk_ref)
        dv_ref[...] = jnp.zeros_like(dv_ref)
        for (r0, rn, kend) in blocks:
            dqs = []
            for e in (0, 1):
                bs = (ct_ref, crs[e]) if has_bias else None
                qe, ke, p, l = _attn_scores(q_ref, k_ref, e, r0, rn, kend, wide, scale, bs)
                pn = p * (1.0 / l)
                dob = do_ref[r0:r0 + rn, :]
                mine = (_lane(dob.shape) < 64) if e == 0 else (_lane(dob.shape) >= 64)
                doe = jnp.where(mine, dob, jnp.zeros_like(dob))
                dp = _dot(doe, v_ref[0:kend, :], NT)
                delta = jnp.sum(pn * dp, axis=-1, keepdims=True)
                ds = pn * (dp - delta)
                dsb = ds.astype(BF16)
                dqe = _dot(dsb, ke) * scale
                dke = _dot(dsb, qe, TN) * scale
                if wide:
                    dq_ref[r0:r0 + rn, 128 * e:128 * (e + 1)] = dqe
                    dk_ref[0:kend, 128 * e:128 * (e + 1)] += dke
                else:
                    dqs.append(dqe)
                    dk_ref[0:kend, :] += dke
                dv_ref[0:kend, :] += _dot(pn.astype(BF16), doe, TN)
                if has_bias:
                    dcs[e][0, :, 0:kend] -= _colsum(ds)
            if not wide:
                dq_ref[r0:r0 + rn, :] = jnp.where(_lane(dqs[0].shape) < 64, dqs[0], dqs[1])

    args = (q, k, v, do) + ((bias[0], bias[1], bias[1]) if has_bias else ())
    out_specs = [pl.BlockSpec((LP, qw), lambda b, hp: (b, hp)),
                 pl.BlockSpec((LP, qw), lambda b, hp: (b, hp)),
                 pl.BlockSpec((LP, 128), lambda b, hp: (b, hp))]
    out_shape = [jax.ShapeDtypeStruct(q.shape, F32), jax.ShapeDtypeStruct(q.shape, F32),
                 jax.ShapeDtypeStruct((T, 512), F32)]
    if has_bias:
        out_specs += [pl.BlockSpec((1, 1, LP), lambda b, hp: (b * 4 + hp, 0, 0))] * 2
        out_shape += [jax.ShapeDtypeStruct((B * 4, 1, LP), F32)] * 2
    return _call_tasks(
        body, tasks, name=name, grid=(B, 4),
        in_specs=specs + [pl.BlockSpec((LP, 128), lambda b, hp: (b, hp))] + bias_specs,
        out_specs=out_specs, out_shape=out_shape, args=args)


def _post_fwd(h, of, om, proj, bg, wbf, wbm, wout, name):
    T = h.shape[0]
    tm = _tile(T, (512, 384, 256, 128))

    def body(h_ref, of_ref, om_ref, gl_ref, bg_ref, wbf_ref, wbm_ref, wo_ref, o_ref, mix_ref):
        gate = jax.nn.sigmoid(gl_ref[...] + bg_ref[...])
        mix = gate[:, :D] * _dot(of_ref[...], wbf_ref[...]) + gate[:, D:] * _dot(om_ref[...], wbm_ref[...])
        mb = mix.astype(BF16)
        mix_ref[...] = mb
        o_ref[...] = h_ref[...] + _dot(mb, wo_ref[...])

    row = lambda w: pl.BlockSpec((tm, w), lambda i: (i, 0))
    full = lambda a: pl.BlockSpec(a.shape, lambda i: (0, 0))
    return _call(
        body, name=name, grid=(T // tm,),
        in_specs=[row(D), row(512), row(512), pl.BlockSpec((tm, 2 * D), lambda i: (i, 1)),
                  full(bg), full(wbf), full(wbm), full(wout)],
        out_specs=[row(D), row(D)],
        out_shape=[jax.ShapeDtypeStruct((T, D), F32), jax.ShapeDtypeStruct((T, D), BF16)],
        compiler_params=_cp(),
    )(h, of, om, proj, bg, wbf, wbm, wout)


def _post_bwd(dh, of, om, proj, bg, wbf, wbm, wout, name):
    T = dh.shape[0]
    tm = _tile(T, (512, 384, 256, 128))

    def body(d_ref, of_ref, om_ref, gl_ref, bg_ref, wbf_ref, wbm_ref, wo_ref,
             dgl_ref, dbf_ref, dbm_ref, dof_ref, dom_ref, dbg_ref):
        @pl.when(pl.program_id(0) == 0)
        def _():
            dbg_ref[...] = jnp.zeros_like(dbg_ref)

        gate = jax.nn.sigmoid(gl_ref[...] + bg_ref[...])
        dmix = _dot(d_ref[...].astype(BF16), wo_ref[...], NT)
        ofx = _dot(of_ref[...], wbf_ref[...])
        omx = _dot(om_ref[...], wbm_ref[...])
        gf = gate[:, :D]
        gm = gate[:, D:]
        dof = (dmix * gf).astype(BF16)
        dom = (dmix * gm).astype(BF16)
        dglf = dmix * ofx * gf * (1.0 - gf)
        dglm = dmix * omx * gm * (1.0 - gm)
        dgl_ref[:, :D] = dglf.astype(BF16)
        dgl_ref[:, D:] = dglm.astype(BF16)
        dbg_ref[:, :D] += _colsum(dglf)
        dbg_ref[:, D:] += _colsum(dglm)
        dbf_ref[...] = dof
        dbm_ref[...] = dom
        dof_ref[...] = _dot(dof, wbf_ref[...], NT).astype(BF16)
        dom_ref[...] = _dot(dom, wbm_ref[...], NT).astype(BF16)

    row = lambda w: pl.BlockSpec((tm, w), lambda i: (i, 0))
    full = lambda a: pl.BlockSpec(a.shape, lambda i: (0, 0))
    return _call(
        body, name=name, grid=(T // tm,),
        in_specs=[row(D), row(512), row(512), pl.BlockSpec((tm, 2 * D), lambda i: (i, 1)),
                  full(bg), full(wbf), full(wbm), full(wout)],
        out_specs=[row(2 * D), row(D), row(D), row(512), row(512), pl.BlockSpec((1, 2 * D), lambda i: (0, 0))],
        out_shape=[jax.ShapeDtypeStruct((T, 2 * D), BF16), jax.ShapeDtypeStruct((T, D), BF16),
                   jax.ShapeDtypeStruct((T, D), BF16), jax.ShapeDtypeStruct((T, 512), BF16),
                   jax.ShapeDtypeStruct((T, 512), BF16), jax.ShapeDtypeStruct((1, 2 * D), F32)],
        compiler_params=_cp(),
    )(dh, of, om, proj, bg, wbf, wbm, wout)


def _loss_head(h3, target, B, LP, name):
    S = LP - MPAD
    half = LP // 2
    first = half - MPAD

    def body(h_ref, t_ref, dy_ref, l_ref):
        b = pl.program_id(0)
        k = pl.program_id(1)

        @pl.when(jnp.logical_and(b == 0, k == 0))
        def _():
            l_ref[...] = jnp.zeros_like(l_ref)

        @pl.when(k == 0)
        def _():
            e = h_ref[MPAD:, :] - t_ref[0, 0:first, :]
            dy_ref[0:MPAD, :] = jnp.zeros((MPAD, D), F32)
            dy_ref[MPAD:, :] = e * (1.0 / D)
            l_ref[...] += jnp.sum(e * e, axis=0, keepdims=True) * (0.5 / D)

        @pl.when(k == 1)
        def _():
            e = h_ref[...] - t_ref[0, first:S, :]
            dy_ref[...] = e * (1.0 / D)
            l_ref[...] += jnp.sum(e * e, axis=0, keepdims=True) * (0.5 / D)

    return _call(
        body, name=name, grid=(B, 2),
        in_specs=[pl.BlockSpec((half, D), lambda b, k: (b * 2 + k, 0)),
                  pl.BlockSpec((1, S, D), lambda b, k: (b, 0, 0))],
        out_specs=[pl.BlockSpec((half, D), lambda b, k: (b * 2 + k, 0)),
                   pl.BlockSpec((1, D), lambda b, k: (0, 0))],
        out_shape=[jax.ShapeDtypeStruct(h3.shape, F32), jax.ShapeDtypeStruct((1, D), F32)],
        compiler_params=_cp(),
    )(h3, target)


def _rope_tables(B, LP):
    pos = jnp.concatenate([jnp.arange(MPAD, dtype=F32), NMETA + jnp.arange(LP - MPAD, dtype=F32)])
    inv_freq = ROPE_THETA ** (-jnp.arange(0, ROPE, 2, dtype=F32) / ROPE)
    ang = pos[:, None] * inv_freq[None, :]
    cos, sin = jnp.cos(ang), jnp.sin(ang)
    z32 = jnp.zeros((LP, 32), F32)
    rc = jnp.concatenate([jnp.ones((LP, 64), F32), cos, cos, z32], axis=1)
    rs = jnp.concatenate([jnp.zeros((LP, 64), F32), -sin, sin, z32], axis=1)
    return jnp.tile(rc, (B, 1)), jnp.tile(rs, (B, 1))


def _pad_lanes(v, start, width=128):
    n = v.shape[1]
    return jnp.concatenate([jnp.zeros((1, start), F32), v, jnp.zeros((1, width - start - n), F32)], axis=1)


G_FFN1 = ["ffn1_w_gu", "ffn1_w_down"]
G_MIX = ["w_in", "mla_w_uq", "mla_w_ukv", "w_branch_fox", "w_branch_mla", "w_out"]
G_OUT = ["w_out", "w_branch_fox", "w_branch_mla"]
G_IN = ["w_in", "mla_w_uq", "mla_w_ukv"]


def _step(x, target, meta, vec, gath, shards):
    dist = shards is not None
    B, S, _ = x.shape
    LP = MPAD + S
    T = B * LP
    gath = dict(gath)

    def gather(names, wide=()):
        return [_gather_task([shards[n] for n in names], wide)] if dist else []

    def flat_gu(w):
        return w if w.ndim == 2 else _cols_from_shards(w)

    def gathered(names, touts):
        if dist:
            gath.update(zip(names, touts[0]))

    g4, sums, red = {}, {}, {}

    def scatter(names):
        return [_a2a_task([_pieces(g4[n]) for n in names])] if dist else []

    def scattered(names, tout, me):
        for n, r in zip(names, tout):
            sums[n] = _sum_pieces(r, _pieces(g4[n]), me, "rs_sum_" + n)

    def join(names):
        return [_join_task([sums[n] for n in names])] if dist else []

    def joined(names, tout):
        for n, r in zip(names, tout):
            red[n] = (sums[n], r)

    me = None
    if dist:
        me = jnp.tile((4 * lax.axis_index("x") + 2 * lax.axis_index("y") + lax.axis_index("c")).reshape(1), 2).astype(jnp.int32)

    h0 = jnp.concatenate([jnp.broadcast_to(meta[None], (B, NMETA, D)),
                          jnp.zeros((B, MPAD - NMETA, D), F32), x], axis=1).reshape(T, D)
    rc, rs = _rope_tables(B, LP)
    gfq = jnp.tile(vec["fox_q_norm"], (1, 2))
    gfk = jnp.tile(vec["fox_k_norm"], (1, 2))
    gmq = _pad_lanes(vec["mla_q_norm"], 0)
    gmk = _pad_lanes(vec["mla_k_norm"], 0)
    bfv = _pad_lanes(vec["b_forget"], L_FL)

    w1gu, w1d = flat_gu(gath["ffn1_w_gu"]), gath["ffn1_w_down"].reshape(DFF, D)
    h1, touts = _ffn_fwd(h0, vec["ffn1_norm"], w1gu, w1d, "ffn1_fwd", gather(G_MIX))
    gathered(G_MIX, touts)
    wm = _mixer_weights(gath)
    small = (gfq, gfk, vec["mla_cq_norm"], vec["mla_ckv_norm"], gmq, gmk, bfv, wm["wuq"], wm["wuk"], wm["wuv"])
    proj, u2 = _inproj_fwd(h1, vec["mix_norm"], wm["w_in"], "inproj_fwd")
    fq, fk, fv, qm, km, vm, lf = _prep_fwd(proj, rc, rs, *small, name="prep_fwd")
    lf_rows = lf[:, L_FL:L_FL + NH].reshape(B, LP, NH).transpose(0, 2, 1).reshape(B * NH, LP)
    crow = _forget_scan(lf_rows, False, "forget_scan")
    ctok = crow.reshape(B, 4, 2, LP).transpose(0, 1, 3, 2).reshape(B * 4, LP, 2)
    bias = (ctok, crow.reshape(B * NH, 1, LP))
    of, touts = _attn_fwd(fq, fk, fv, bias, B, LP, False, 64 ** -0.5, "fox_fwd", gather(["ffn2_w_gu"], wide=(0,)))
    gathered(["ffn2_w_gu"], touts)
    om, touts = _attn_fwd(qm, km, vm, None, B, LP, True, MLA_QK ** -0.5, "mla_fwd", gather(["ffn2_w_down"]))
    gathered(["ffn2_w_down"], touts)
    h2, mix = _post_fwd(h1, of, om, proj, vec["b_gate"], wm["wbf"], wm["wbm"], wm["w_out"], "post_fwd")
    w2gu, w2d = flat_gu(gath["ffn2_w_gu"]), gath["ffn2_w_down"].reshape(DFF, D)
    h3, _ = _ffn_fwd(h2, vec["ffn2_norm"], w2gu, w2d, "ffn2_fwd")
    dy, lpart = _loss_head(h3, target, B, LP, "loss_head")

    gv = {}
    (dh2, u3, a2, dgp2, gv["ffn2_norm"]), _ = _ffn_bwd(h2, dy, vec["ffn2_norm"], w2gu, w2d, "ffn2_bwd")
    g4["ffn2_w_gu"] = _wgrad(u3, dgp2, "ffn2_dwgu", bn=DFF, shards=2)[0]
    g4["ffn2_w_down"] = _wgrad(a2, dy, "ffn2_dwd", scale=0.5, bk=FH, bt=2176)[0].reshape(N_CHIPS, DFF // N_CHIPS, D)

    dgl, dbf, dbm, dof, dom, gv["b_gate"] = _post_bwd(dh2, of, om, proj, vec["b_gate"], wm["wbf"], wm["wbm"], wm["w_out"], "post_bwd")
    g4["w_out"] = _wgrad(mix, dh2, "dw_out", bt=2176)[0].reshape(N_CHIPS, D // N_CHIPS, D)
    g4["w_branch_fox"] = _cols_to_shards(_wgrad(of, dbf, "dw_bf", bt=2176)[0])
    g4["w_branch_mla"] = _cols_to_shards(_wgrad(om, dbm, "dw_bm", bt=2176)[0])
    G_FFN2 = ["ffn2_w_gu", "ffn2_w_down"]
    (dfq, dfk, dfv, dc0, dc1), touts = _attn_bwd(fq, fk, fv, dof, bias, B, LP, False, 64 ** -0.5, "fox_bwd", scatter(G_FFN2))
    if dist:
        scattered(G_FFN2, touts[0], me)
    (dqm, dkm, dvm), touts = _attn_bwd(qm, km, vm, dom, None, B, LP, True, MLA_QK ** -0.5, "mla_bwd",
                                       scatter(G_OUT) + join(G_FFN2))
    if dist:
        scattered(G_OUT, touts[0], me)
        joined(G_FFN2, touts[1])
    dc = jnp.concatenate([dc0, dc1], axis=1).reshape(B * NH, LP)
    dlf_rows = _forget_scan(dc, True, "forget_scan_bwd")
    dlf = dlf_rows.reshape(B, NH, LP).transpose(0, 2, 1).reshape(T, NH)
    dlf = jnp.concatenate([jnp.zeros((T, L_FL), F32), dlf, jnp.zeros((T, 128 - L_FL - NH), F32)], axis=1)
    (dlo, dgfq, dgfk, gv["mla_cq_norm"], gv["mla_ckv_norm"], dgmq, dgmk, dbfv,
     dwuq, dwuk, dwuv) = _prep_bwd(proj, rc, rs, *small, dfq, dfk, dfv, dqm, dkm, dvm, dlf, name="prep_bwd")
    gv["fox_q_norm"] = dgfq[:, :64] + dgfq[:, 64:]
    gv["fox_k_norm"] = dgfk[:, :64] + dgfk[:, 64:]
    gv["mla_q_norm"] = dgmq[:, :MLA_QK]
    gv["mla_k_norm"] = dgmk[:, :MLA_QK]
    gv["b_forget"] = dbfv[:, L_FL:L_FL + NH]
    dwin = jnp.concatenate([_wgrad(u2, dlo, "dw_in_lo")[0], _wgrad(u2, dgl, "dw_in_hi")[0]], axis=1)
    g4["w_in"] = _cols_to_shards(_win_from_kernel(dwin))
    g4["mla_w_uq"] = _cols_to_shards(
        dwuq.astype(GRAD_DTYPE).reshape(QR, NH, 128)[:, :, :MLA_QK].reshape(QR, NH * MLA_QK))
    dukv = jnp.concatenate([dwuk.reshape(KVR, NH, 128)[:, :, :64], dwuv.reshape(KVR, NH, 64)], axis=2)
    g4["mla_w_ukv"] = _cols_to_shards(dukv.astype(GRAD_DTYPE).reshape(KVR, NH * 128))
    dh1, gv["mix_norm"] = _inproj_bwd(h1, dh2, dlo, dgl, vec["mix_norm"], wm["w_in"], "inproj_bwd")

    (dh0, u1, a1, dgp1, gv["ffn1_norm"]), touts = _ffn_bwd(h0, dh1, vec["ffn1_norm"], w1gu, w1d, "ffn1_bwd",
                                                            scatter(G_IN) + join(G_OUT))
    if dist:
        scattered(G_IN, touts[0], me)
        joined(G_OUT, touts[1])
    dh0 = dh0.reshape(B, LP, D)
    grad_x = dh0[:, MPAD:]
    grad_meta = jnp.sum(dh0[:, :NMETA], axis=0)
    share = [_share_task([_stack_vectors([gv[n] for n in VECS]), grad_meta, lpart])] if dist else []
    dwd1, touts = _wgrad(a1, dh1, "ffn1_dwd", scale=0.5, bk=FH, bt=2176, tasks=share + join(G_IN))
    g4["ffn1_w_down"] = dwd1.reshape(N_CHIPS, DFF // N_CHIPS, D)
    shared = touts[0] if dist else None
    if dist:
        joined(G_IN, touts[1])
    half = D // 2
    gu = []
    for cr in (0, 1):
        tasks = (scatter(["ffn1_w_down"]) if cr == 0 else [_a2a_core_task([gu[0]], 0)]) if dist else []
        part, touts = _wgrad(u1, dgp1, "ffn1_dwgu_%d" % cr, bk=half, bn=DFF, shards=2, rows=(cr, 1), tasks=tasks)
        gu.append(part)
        if dist and cr == 0:
            scattered(["ffn1_w_down"], touts[0], me)
        elif dist:
            got0 = touts[0][0]
    g4["ffn1_w_gu"] = jnp.concatenate(gu, axis=1)
    if dist:
        got1 = _run_tasks([_a2a_core_task([gu[1]], 1)], "rs_ffn1_w_gu")[0][0]
        core = lax.axis_index("c")
        mine = me.at[1].set(2 * lax.axis_index("x") + lax.axis_index("y"))
        sums["ffn1_w_gu"] = _sum_pieces(jnp.where(core == 0, got0, got1), jnp.where(core == 0, gu[0], gu[1]),
                                        mine, "rs_sum_ffn1_w_gu")
        joined(G_FFN1, _run_tasks(join(G_FFN1), "rs_join_ffn1")[0])
    return lpart, grad_x, grad_meta, gv, (red if dist else g4), shared


def _cols_from_shards(g4):
    n, r, c = g4.shape
    return g4.transpose(1, 0, 2).reshape(r, n * c)


def _cols_to_shards(full):
    r, c4 = full.shape
    return full.reshape(r, N_CHIPS, c4 // N_CHIPS).transpose(1, 0, 2)


def _win_to_kernel(wfull):
    z = lambda n: jnp.zeros((D, n), wfull.dtype)
    fl, cq, ckv, kr, gate = (wfull[:, 1536:1544], wfull[:, 1544:1800], wfull[:, 1800:1928],
                             wfull[:, 1928:1960], wfull[:, 1960:4008])
    misc = jnp.concatenate([z(L_KR), kr, fl, z(128 - L_FL - NH)], axis=1)
    return jnp.concatenate([wfull[:, :1536], cq, ckv, misc, gate], axis=1)


def _win_from_kernel(gk):
    m = C_MISC
    return jnp.concatenate([gk[:, :1536], gk[:, m + L_FL:m + L_FL + NH], gk[:, C_CQ:C_CQ + QR],
                            gk[:, C_CKV:C_CKV + KVR], gk[:, m + L_KR:m + L_KR + ROPE], gk[:, C_GATE:]], axis=1)


def _pieces(g4):
    n, r, c = g4.shape
    return g4.reshape(2 * n, r // 2, c)


def _mixer_weights(gath):
    w = {}
    w["w_in"] = _win_to_kernel(_cols_from_shards(gath["w_in"]))
    uq = _cols_from_shards(gath["mla_w_uq"]).reshape(QR, NH, MLA_QK)
    w["wuq"] = jnp.pad(uq, ((0, 0), (0, 0), (0, 128 - MLA_QK))).reshape(QR, NH * 128)
    ukv = _cols_from_shards(gath["mla_w_ukv"]).reshape(KVR, NH, 128)
    w["wuk"] = jnp.pad(ukv[:, :, :64], ((0, 0), (0, 0), (0, 64))).reshape(KVR, NH * 128)
    w["wuv"] = ukv[:, :, 64:].reshape(KVR, NH * 64)
    w["wbf"] = _cols_from_shards(gath["w_branch_fox"])
    w["wbm"] = _cols_from_shards(gath["w_branch_mla"])
    w["w_out"] = gath["w_out"].reshape(D, D)
    return w


def _chip_peers(x, y):
    return [(1 - x, y), (x, 1 - y), (1 - x, 1 - y)]


RELS = [(dx, dy, dc) for dx in (0, 1) for dy in (0, 1) for dc in (0, 1)][1:]


def _here():
    return lax.axis_index("x"), lax.axis_index("y"), lax.axis_index("c")


def _flip(a, d):
    return (1 - a) if d else a


def _remote(src, dst, send, recv, i, dev):
    return functools.partial(pltpu.make_async_remote_copy, src_ref=src, dst_ref=dst, send_sem=send.at[i],
                             recv_sem=recv.at[i], device_id=dev, device_id_type=MESH)


def _gathered_shape(s, wide):
    return jax.ShapeDtypeStruct((s.shape[0], N_CHIPS * s.shape[1]) if wide else (N_CHIPS,) + s.shape, s.dtype)


def _slot(ref, j, shard, wide, rows=None):
    if wide:
        lanes = pl.ds(pl.multiple_of(j * shard.shape[1], 128), shard.shape[1])
        return ref.at[slice(None) if rows is None else rows, lanes]
    return ref.at[j] if rows is None else ref.at[j, rows]


def _gather_task(shards, wide=()):
    n = len(shards)

    def descs(ins, outs, sems):
        send, recv, loc = sems
        x, y, c = _here()
        j = 2 * x + y
        locs, pairs = [], []
        for k in range(n):
            at = functools.partial(_slot, outs[k], shard=shards[k], wide=k in wide)
            locs.append(functools.partial(pltpu.make_async_copy, ins[k], at(j), loc.at[k]))
            for r, (px, py) in enumerate(_chip_peers(x, y)):
                dev = (px, py, c)
                pairs.append((_remote(ins[k], at(j), send, recv, 3 * k + r, dev),
                              _remote(ins[k], at(2 * px + py), send, recv, 3 * k + r, dev)))
        return locs, pairs

    return _Task(shards, [_gathered_shape(s, k in wide) for k, s in enumerate(shards)],
                 [pltpu.SemaphoreType.DMA((3 * n,)), pltpu.SemaphoreType.DMA((3 * n,)), pltpu.SemaphoreType.DMA((n,))],
                 descs)


class _SplitGather(_Task):
    PARTS = 2

    def __init__(self, shards, wide=()):
        n = 3 * len(shards) * self.PARTS
        dma = pltpu.SemaphoreType.DMA
        self.wide = wide
        super().__init__(shards, [_gathered_shape(s, k in wide) for k, s in enumerate(shards)],
                         [dma((n,)), dma((n,)), dma((n,)), dma((n,)), dma((len(shards),))], None)

    def _plan(self, ins, outs, sems):
        send, recv, fsend, frecv, loc = sems
        x, y, c = _here()
        j = 2 * x + y
        locs, first, passed = [], [], []
        for k in range(len(ins)):
            h = self.ins[k].shape[0] // 2
            parts = self.PARTS if h % (32 * self.PARTS) == 0 else 1
            hp = h // parts
            at = functools.partial(_slot, outs[k], shard=self.ins[k], wide=k in self.wide)
            locs.append(functools.partial(pltpu.make_async_copy, ins[k], at(j), loc.at[k]))
            for r, (px, py) in enumerate(_chip_peers(x, y)):
                p = 2 * px + py
                for q in range(parts):
                    i = (3 * k + r) * self.PARTS + q
                    mine = pl.ds(pl.multiple_of(c * h + q * hp, 8), hp)
                    theirs = pl.ds(pl.multiple_of((1 - c) * h + q * hp, 8), hp)
                    first.append((_remote(ins[k].at[mine], at(j, rows=mine), send, recv, i, (px, py, c)),
                                  _remote(ins[k].at[mine], at(p, rows=mine), send, recv, i, (px, py, c))))
                    passed.append((_remote(at(p, rows=mine), at(p, rows=mine), fsend, frecv, i, (x, y, 1 - c)),
                                   _remote(at(p, rows=mine), at(p, rows=theirs), fsend, frecv, i, (x, y, 1 - c))))
        return locs, first, passed

    def start(self, ins, outs, sems):
        locs, first, _ = self._plan(ins, outs, sems)
        for lc in locs:
            lc().start()
        for snd, _ in first:
            snd().start()

    def wait(self, ins, outs, sems):
        locs, first, passed = self._plan(ins, outs, sems)
        for (_, landed), (pass_on, _) in zip(first, passed):
            landed().wait_recv()
            pass_on().start()
        for _, rcv in passed:
            rcv().wait_recv()
        for snd, _ in first + passed:
            snd().wait_send()
        for lc in locs:
            lc().wait()


def _a2a_task(ps):
    n = len(ps)
    nr = len(RELS)

    def descs(ins, outs, sems):
        send, recv = sems
        x, y, c = _here()
        me = 4 * x + 2 * y + c
        pairs = []
        for k in range(n):
            for i, (dx, dy, dc) in enumerate(RELS):
                dev = (_flip(x, dx), _flip(y, dy), _flip(c, dc))
                peer = 4 * dev[0] + 2 * dev[1] + dev[2]
                pairs.append((_remote(ins[k].at[peer], outs[k].at[me], send, recv, nr * k + i, dev),
                              _remote(ins[k].at[peer], outs[k].at[peer], send, recv, nr * k + i, dev)))
        return [], pairs

    return _Task(ps, [jax.ShapeDtypeStruct(p.shape, p.dtype) for p in ps],
                 [pltpu.SemaphoreType.DMA((nr * n,)), pltpu.SemaphoreType.DMA((nr * n,))], descs)


def _a2a_core_task(ps, core):
    n = len(ps)
    nr = len(RELS)

    def descs(ins, outs, sems):
        send, recv = sems
        x, y, c = _here()
        me = 4 * x + 2 * y + c
        pairs = []
        for k in range(n):
            for i, (dx, dy, dc) in enumerate(RELS):
                dev = (_flip(x, dx), _flip(y, dy), _flip(c, dc))
                peer = 4 * dev[0] + 2 * dev[1] + dev[2]
                pairs.append((_remote(ins[k].at[2 * dev[0] + dev[1]], outs[k].at[me], send, recv, nr * k + i, dev),
                              _remote(ins[k].at[2 * dev[0] + dev[1]], outs[k].at[peer], send, recv, nr * k + i, dev),
                              dev[2] == core, c == core))
        return [], pairs

    return _Task(ps, [jax.ShapeDtypeStruct((N_DEV,) + p.shape[1:], p.dtype) for p in ps],
                 [pltpu.SemaphoreType.DMA((nr * n,)), pltpu.SemaphoreType.DMA((nr * n,))], descs)


def _share_task(vs):
    n = len(vs)
    nr = len(RELS)

    def descs(ins, outs, sems):
        send, recv, loc = sems
        x, y, c = _here()
        me = 4 * x + 2 * y + c
        locs, pairs = [], []
        for k in range(n):
            locs.append(functools.partial(pltpu.make_async_copy, ins[k], outs[k].at[me], loc.at[k]))
            for i, (dx, dy, dc) in enumerate(RELS):
                dev = (_flip(x, dx), _flip(y, dy), _flip(c, dc))
                peer = 4 * dev[0] + 2 * dev[1] + dev[2]
                pairs.append((_remote(ins[k], outs[k].at[me], send, recv, nr * k + i, dev),
                              _remote(ins[k], outs[k].at[peer], send, recv, nr * k + i, dev)))
        return locs, pairs

    return _Task(vs, [jax.ShapeDtypeStruct((N_DEV,) + v.shape, v.dtype) for v in vs],
                 [pltpu.SemaphoreType.DMA((nr * n,)), pltpu.SemaphoreType.DMA((nr * n,)), pltpu.SemaphoreType.DMA((n,))],
                 descs)


def _join_task(ss):
    n = len(ss)

    def descs(ins, outs, sems):
        send, recv = sems
        x, y, c = _here()
        pairs = []
        for k in range(n):
            cp = _remote(ins[k], outs[k], send, recv, k, (x, y, 1 - c))
            pairs.append((cp, cp))
        return [], pairs

    return _Task(ss, [jax.ShapeDtypeStruct(s.shape, s.dtype) for s in ss],
                 [pltpu.SemaphoreType.DMA((n,)), pltpu.SemaphoreType.DMA((n,))], descs)


def _sum_pieces(recv, own, me, name):
    n, h, c = recv.shape
    tr = h

    def body(me_ref, r_ref, o_ref, out_ref):
        s = pl.program_id(1)
        val = jnp.where(s == me_ref[0], o_ref[0], r_ref[0]).astype(F32)

        @pl.when(s == 0)
        def _():
            out_ref[...] = val

        @pl.when(s > 0)
        def _():
            out_ref[...] += val

    def other(s, m):
        return jnp.where(s == m[0], (s + 1) % n, s)

    return _call(
        body, name=name,
        grid_spec=pltpu.PrefetchScalarGridSpec(
            num_scalar_prefetch=1, grid=(h // tr, n),
            in_specs=[pl.BlockSpec((1, tr, c), lambda i, s, m: (other(s, m), i, 0)),
                      pl.BlockSpec((1, tr, c), lambda i, s, m: (m[1], i, 0))],
            out_specs=pl.BlockSpec((tr, c), lambda i, s, m: (i, 0))),
        out_shape=jax.ShapeDtypeStruct((h, c), F32),
        compiler_params=_cp(),
    )(me, recv, own)


def _adamw_update(gg, w, m, v):
    c1 = 1.0 / (1.0 - ADAM_B1 ** ADAM_STEP)
    c2 = 1.0 / (1.0 - ADAM_B2 ** ADAM_STEP)
    nm = ADAM_B1 * m + (1.0 - ADAM_B1) * gg
    nv = ADAM_B2 * v + (1.0 - ADAM_B2) * (gg * gg)
    return -ADAM_LR * ((nm * c1) / (jnp.sqrt(nv * c2) + ADAM_EPS) + ADAM_WD * w), nm, nv


def _adamw_small(gvec8, gmeta8, lp8, chip, ws, ms, vs, name):
    na = len(ws)

    def dev_sum(ref):
        acc = ref[0]
        for s in range(1, N_DEV):
            acc = acc + ref[s]
        return acc

    def body(c_ref, gv_ref, gm_ref, lp_ref, *refs):
        w_refs, m_refs, v_refs = refs[:na], refs[na:2 * na], refs[2 * na:3 * na]
        l_ref = refs[3 * na]
        outs = refs[3 * na + 1:]
        g_refs, d_refs, nm_refs, nv_refs = outs[:na], outs[na:2 * na], outs[2 * na:3 * na], outs[3 * na:]
        l_ref[...] = dev_sum(lp_ref)
        gvec = dev_sum(gv_ref)
        for k in range(na):
            gg = gvec[k:k + 1, 0:ws[k].shape[1]] if k < na - 1 else dev_sum(gm_ref)
            g_refs[k][...] = gg
            d_refs[k][...], nm_refs[k][...], nv_refs[k][...] = _adamw_update(gg, w_refs[k][...], m_refs[k][...], v_refs[k][...])

    whole = lambda a: pl.BlockSpec(a.shape, lambda i, c: (0,) * a.ndim)
    arrs = list(ws) + list(ms) + list(vs)
    res = _call(
        body, name=name,
        grid_spec=pltpu.PrefetchScalarGridSpec(
            num_scalar_prefetch=1, grid=(1,),
            in_specs=[whole(gvec8), pl.BlockSpec((N_DEV, NMETA, D // N_CHIPS), lambda i, c: (0, 0, c[0])), whole(lp8)]
                     + [whole(a) for a in arrs],
            out_specs=[pl.BlockSpec((1, D), lambda i, c: (0, 0))] + [whole(a) for a in ws] * 4),
        out_shape=[jax.ShapeDtypeStruct((1, D), F32)] + [jax.ShapeDtypeStruct(a.shape, F32) for a in ws] * 4,
        compiler_params=_cp(),
    )(chip, gvec8, gmeta8, lp8, *arrs)
    return res[0], [list(res[1 + i * na:1 + (i + 1) * na]) for i in range(4)]


def _adamw_halves(wt, mine, theirs, m, v, core, name):
    r, c = wt.shape
    h = r // 2
    tr = _tile(h, (256, 176, 128, 64))
    nh = h // tr

    def body(c_ref, w_ref, a_ref, b_ref, m_ref, v_ref, g_ref, d_ref, nm_ref, nv_ref):
        gg = jnp.where(pl.program_id(0) // nh == c_ref[0], a_ref[...], b_ref[...])
        g_ref[...] = gg
        d_ref[...], nm_ref[...], nv_ref[...] = _adamw_update(gg, w_ref[...], m_ref[...], v_ref[...])

    full = pl.BlockSpec((tr, c), lambda i, cr: (i, 0))
    half = pl.BlockSpec((tr, c), lambda i, cr: (i % nh, 0))
    return _call(
        body, name=name,
        grid_spec=pltpu.PrefetchScalarGridSpec(
            num_scalar_prefetch=1, grid=(2 * nh,),
            in_specs=[full, half, half, full, full], out_specs=[full] * 4),
        out_shape=[jax.ShapeDtypeStruct((r, c), F32)] * 4,
        compiler_params=_cp(),
    )(core, wt, mine, theirs, m, v)


MATS = ["ffn1_w_gu", "ffn1_w_down", "w_in", "mla_w_uq", "mla_w_ukv", "w_branch_fox", "w_branch_mla",
        "w_out", "ffn2_w_gu", "ffn2_w_down"]
VECS = ["ffn1_norm", "mix_norm", "b_forget", "b_gate", "fox_q_norm", "fox_k_norm", "mla_cq_norm",
        "mla_ckv_norm", "mla_q_norm", "mla_k_norm", "ffn2_norm"]
WEIGHTS = ["meta_tokens", "ffn1_norm", "ffn1_w_gu", "ffn1_w_down", "mix_norm", "w_in", "b_forget", "b_gate",
           "fox_q_norm", "fox_k_norm", "mla_cq_norm", "mla_w_uq", "mla_ckv_norm", "mla_w_ukv", "mla_q_norm",
           "mla_k_norm", "w_branch_fox", "w_branch_mla", "w_out", "ffn2_norm", "ffn2_w_gu", "ffn2_w_down"]


VEC_LANES = 2048


def _stack_vectors(parts):
    rows = [_pad_lanes(p, 0, VEC_LANES) for p in parts]
    rows.append(jnp.zeros((-len(parts) % 8, VEC_LANES), F32))
    return jnp.concatenate(rows, axis=0)


def kernel(x, meta_tokens, ffn1_norm, ffn1_w_gu, ffn1_w_down, mix_norm, w_in, b_forget, b_gate, fox_q_norm, fox_k_norm, mla_cq_norm, mla_w_uq, mla_ckv_norm, mla_w_ukv, mla_q_norm, mla_k_norm, w_branch_fox, w_branch_mla, w_out, ffn2_norm, ffn2_w_gu, ffn2_w_down, loss_target, m_meta_tokens, m_ffn1_norm, m_ffn1_w_gu, m_ffn1_w_down, m_mix_norm, m_w_in, m_b_forget, m_b_gate, m_fox_q_norm, m_fox_k_norm, m_mla_cq_norm, m_mla_w_uq, m_mla_ckv_norm, m_mla_w_ukv, m_mla_q_norm, m_mla_k_norm, m_w_branch_fox, m_w_branch_mla, m_w_out, m_ffn2_norm, m_ffn2_w_gu, m_ffn2_w_down, v_meta_tokens, v_ffn1_norm, v_ffn1_w_gu, v_ffn1_w_down, v_mix_norm, v_w_in, v_b_forget, v_b_gate, v_fox_q_norm, v_fox_k_norm, v_mla_cq_norm, v_mla_w_uq, v_mla_ckv_norm, v_mla_w_ukv, v_mla_q_norm, v_mla_k_norm, v_w_branch_fox, v_w_branch_mla, v_w_out, v_ffn2_norm, v_ffn2_w_gu, v_ffn2_w_down):
    a = dict(locals())
    wts = {n: a[n] for n in WEIGHTS}
    ms = {n: a["m_" + n] for n in WEIGHTS}
    vs = {n: a["v_" + n] for n in WEIGHTS}
    cx, cy, cc = lax.axis_index("x"), lax.axis_index("y"), lax.axis_index("c")
    chip = 2 * cx + cy

    shards = {n: wts[n][0].astype(BF16) for n in MATS}
    first = _run_tasks([_SplitGather([shards[n] for n in G_FFN1] + [meta_tokens], wide=(0,))], "gather_ffn1")[0]
    gath = dict(zip(G_FFN1, first[:-1]))
    meta_full = _cols_from_shards(first[-1])

    _, grad_x, _, _, gred, (gvec8, gmeta8, lp8) = _step(x, loss_target, meta_full, {n: wts[n] for n in VECS}, gath, shards)

    sm_names = VECS + ["meta_tokens"]
    lsum, sm = _adamw_small(gvec8, gmeta8, lp8, chip.reshape(1).astype(jnp.int32), [wts[n] for n in sm_names],
                            [ms[n] for n in sm_names], [vs[n] for n in sm_names], "adamw_small")
    loss = jnp.sum(lsum)

    grads, delta, new_m, new_v = {}, {}, {}, {}
    core = cc.reshape(1).astype(jnp.int32)
    for n in MATS:
        shp = wts[n].shape
        mine, theirs = gred[n]
        res = _adamw_halves(wts[n][0], mine, theirs, ms[n][0], vs[n][0], core, "adamw_" + n)
        grads[n], delta[n], new_m[n], new_v[n] = (t.reshape(shp) for t in res)
    for k, n in enumerate(sm_names):
        grads[n], delta[n], new_m[n], new_v[n] = (sm[i][k] for i in range(4))

    return (loss, grad_x, *[grads[n] for n in WEIGHTS], *[delta[n] for n in WEIGHTS],
            *[new_m[n] for n in WEIGHTS], *[new_v[n] for n in WEIGHTS])
```

```python
import functools

import jax
import jax.numpy as jnp
from jax import lax
from jax.experimental import pallas as pl
from jax.experimental.pallas import tpu as pltpu

F32 = jnp.float32
BF16 = jnp.bfloat16
MESH = pl.DeviceIdType.MESH

D = 1024
DFF = 2816
FH = DFF // 2
NMETA = 16
MPAD = 128
EPS = 1e-6
NH = 8
FOXW = 512
QR = 256
KVR = 128
ROPE = 32
MLA_QK = 96
PROJW = 4096
ROPE_THETA = 10000.0
N_CHIPS = 4
N_DEV = 8

ADAM_LR = 0.001
ADAM_B1 = 0.9
ADAM_B2 = 0.999
ADAM_EPS = 1e-08
ADAM_WD = 0.01
ADAM_STEP = 10

VMEM_LIMIT = 56 * 2**20
ATTN_TQ = 256
GRAD_DTYPE = BF16

NT = (((1,), (1,)), ((), ()))
TN = (((0,), (0,)), ((), ()))


def _call(body, **kw):
    return pl.pallas_call(body, **kw)


def _cp(**kw):
    return pltpu.CompilerParams(vmem_limit_bytes=VMEM_LIMIT, **kw)


HBM = pl.BlockSpec(memory_space=pltpu.HBM)


class _Task:
    def __init__(self, ins, out_shapes, sems, descs):
        self.ins, self.out_shapes, self.sems, self.descs = list(ins), list(out_shapes), list(sems), descs

    @staticmethod
    def _if(cond, action):
        if cond is None:
            action()
        else:
            pl.when(cond)(action)

    def start(self, ins, outs, sems):
        locs, pairs = self.descs(ins, outs, sems)
        for lc in locs:
            lc().start()
        for snd, _, *ok in pairs:
            self._if(ok[0] if ok else None, lambda snd=snd: snd().start())

    def wait(self, ins, outs, sems):
        locs, pairs = self.descs(ins, outs, sems)
        for _, rcv, *ok in pairs:
            self._if(ok[1] if ok else None, lambda rcv=rcv: rcv().wait_recv())
        for snd, _, *ok in pairs:
            self._if(ok[0] if ok else None, lambda snd=snd: snd().wait_send())
        for lc in locs:
            lc().wait()


def _call_tasks(body, tasks, *, name, grid, in_specs, out_specs, out_shape, args, scratch_shapes=()):
    in_specs, out_specs, out_shape, scratch_shapes = map(list, (in_specs, out_specs, out_shape, scratch_shapes))
    n_in, n_out, n_sc = len(in_specs), len(out_specs), len(scratch_shapes)
    t_in = [len(t.ins) for t in tasks]
    t_out = [len(t.out_shapes) for t in tasks]
    t_sem = [len(t.sems) for t in tasks]

    def wrapped(*refs):
        pos = [0]

        def take(n):
            pos[0] += n
            return refs[pos[0] - n:pos[0]]

        ins, tins = take(n_in), [take(n) for n in t_in]
        outs, touts = take(n_out), [take(n) for n in t_out]
        sc, tsems = take(n_sc), [take(n) for n in t_sem]
        if tasks:
            first = functools.reduce(jnp.logical_and, [pl.program_id(a) == 0 for a in range(len(grid))])
            last = functools.reduce(jnp.logical_and, [pl.program_id(a) == grid[a] - 1 for a in range(len(grid))])

            @pl.when(first)
            def _():
                for t, a, b, s in zip(tasks, tins, touts, tsems):
                    t.start(a, b, s)

        body(*ins, *outs, *sc)
        if tasks:
            @pl.when(last)
            def _():
                for t, a, b, s in zip(tasks, tins, touts, tsems):
                    t.wait(a, b, s)

    res = _call(
        wrapped, name=name, grid=grid,
        in_specs=in_specs + [HBM] * sum(t_in), out_specs=out_specs + [HBM] * sum(t_out),
        out_shape=out_shape + [s for t in tasks for s in t.out_shapes],
        scratch_shapes=scratch_shapes + [s for t in tasks for s in t.sems],
        compiler_params=_cp(),
    )(*args, *[a for t in tasks for a in t.ins])
    res = list(res)
    touts, pos = [], n_out
    for n in t_out:
        touts.append(res[pos:pos + n])
        pos += n
    return res[:n_out], touts


def _run_tasks(tasks, name):
    t_in = [len(t.ins) for t in tasks]
    t_out = [len(t.out_shapes) for t in tasks]
    t_sem = [len(t.sems) for t in tasks]

    def body(*refs):
        pos = [0]

        def take(n):
            pos[0] += n
            return refs[pos[0] - n:pos[0]]

        tins, touts, tsems = [take(n) for n in t_in], [take(n) for n in t_out], [take(n) for n in t_sem]
        for t, a, b, s in zip(tasks, tins, touts, tsems):
            t.start(a, b, s)
        for t, a, b, s in zip(tasks, tins, touts, tsems):
            t.wait(a, b, s)

    res = list(_call(
        body, name=name, in_specs=[HBM] * sum(t_in), out_specs=[HBM] * sum(t_out),
        out_shape=[s for t in tasks for s in t.out_shapes],
        scratch_shapes=[s for t in tasks for s in t.sems],
    )(*[a for t in tasks for a in t.ins]))
    touts, pos = [], 0
    for n in t_out:
        touts.append(res[pos:pos + n])
        pos += n
    return touts


def _tile(n, cands):
    for c in cands:
        if n % c == 0:
            return c
    raise ValueError(f"no tile for {n} among {cands}")


def _dot(a, b, dims=None):
    if dims is None:
        return jnp.dot(a, b, preferred_element_type=F32)
    return lax.dot_general(a, b, dims, preferred_element_type=F32)


def _lane(shape):
    return lax.broadcasted_iota(jnp.int32, shape, len(shape) - 1)


def _seg_ones(w, log2_seg):
    r = lax.shift_right_logical(lax.broadcasted_iota(jnp.int32, (w, w), 0), log2_seg)
    c = lax.shift_right_logical(lax.broadcasted_iota(jnp.int32, (w, w), 1), log2_seg)
    return jnp.where(r == c, 1.0, 0.0).astype(BF16)


def _seg_sum(x, ones):
    hi = x.astype(BF16)
    r1 = x - hi.astype(F32)
    mid = r1.astype(BF16)
    lo = (r1 - mid.astype(F32)).astype(BF16)
    return _dot(hi, ones) + _dot(mid, ones) + _dot(lo, ones)


def _lane_sum(x, ones):
    return jnp.sum(x, axis=-1, keepdims=True) if ones is None else _seg_sum(x, ones)


def _rms(x, gain, n, ones=None):
    r = lax.rsqrt(_lane_sum(x * x, ones) * (1.0 / n) + EPS)
    xh = x * r
    return xh * gain, xh, r


def _rms_bwd(dy, xh, r, gain, n, ones=None):
    dxh = dy * gain
    return r * (dxh - xh * (_lane_sum(dxh * xh, ones) * (1.0 / n)))


def _rope_swap(x):
    ln = _lane(x.shape)
    sw = jnp.where(ln < 80, pltpu.roll(x, 112, 1), pltpu.roll(x, 16, 1))
    return jnp.where(jnp.logical_and(ln >= 64, ln < 96), sw, 0.0)


def _colsum(x):
    return jnp.sum(x, axis=0, keepdims=True)


def _ffn_weight_specs():
    once = pl.Buffered(1)
    return [pl.BlockSpec((D, DFF), lambda i: (0, 0), pipeline_mode=once),
            pl.BlockSpec((D, DFF), lambda i: (0, 1), pipeline_mode=once),
            pl.BlockSpec((DFF, D), lambda i: (0, 0), pipeline_mode=once)]


def _ffn_fwd(h, norm, wgu, wd, name, tasks=()):
    T = h.shape[0]
    tm = _tile(T, (512, 384, 256, 128))

    def body(h_ref, n_ref, wg_ref, wu_ref, wd_ref, o_ref):
        x = h_ref[...]
        u, _, _ = _rms(x, n_ref[...], D)
        ub = u.astype(BF16)
        g = _dot(ub, wg_ref[...])
        p = _dot(ub, wu_ref[...])
        a = (g * jax.nn.sigmoid(g)) * p
        o_ref[...] = x + 0.5 * _dot(a.astype(BF16), wd_ref[...])

    (out,), touts = _call_tasks(
        body, tasks, name=name, grid=(T // tm,),
        in_specs=[pl.BlockSpec((tm, D), lambda i: (i, 0)), pl.BlockSpec((1, D), lambda i: (0, 0))] + _ffn_weight_specs(),
        out_specs=[pl.BlockSpec((tm, D), lambda i: (i, 0))],
        out_shape=[jax.ShapeDtypeStruct((T, D), F32)],
        args=(h, norm, wgu, wgu, wd))
    return out, touts


def _ffn_bwd(h, dout, norm, wgu, wd, name, tasks=()):
    T = h.shape[0]
    tm = _tile(T, (256, 128))

    def body(h_ref, d_ref, n_ref, wg_ref, wu_ref, wd_ref, dh_ref, u_ref, a_ref, dgp_ref, dn_ref):
        @pl.when(pl.program_id(0) == 0)
        def _():
            dn_ref[...] = jnp.zeros_like(dn_ref)

        u, xh, r = _rms(h_ref[...], n_ref[...], D)
        ub = u.astype(BF16)
        u_ref[...] = ub
        g = _dot(ub, wg_ref[...])
        p = _dot(ub, wu_ref[...])
        s = jax.nn.sigmoid(g)
        sl = g * s
        dz = (0.5 * d_ref[...]).astype(BF16)
        da = _dot(dz, wd_ref[...], NT)
        dp = da * sl
        dg = (da * p) * (s * (1.0 + g * (1.0 - s)))
        a_ref[...] = (sl * p).astype(BF16)
        dgb = dg.astype(BF16)
        dpb = dp.astype(BF16)
        dgp_ref[:, :DFF] = dgb
        dgp_ref[:, DFF:] = dpb
        du = _dot(dgb, wg_ref[...], NT) + _dot(dpb, wu_ref[...], NT)
        dn_ref[...] += _colsum(du * xh)
        dh_ref[...] = d_ref[...] + _rms_bwd(du, xh, r, n_ref[...], D)

    row = lambda w: pl.BlockSpec((tm, w), lambda i: (i, 0))
    return _call_tasks(
        body, tasks, name=name, grid=(T // tm,),
        in_specs=[row(D), row(D), pl.BlockSpec((1, D), lambda i: (0, 0))] + _ffn_weight_specs(),
        out_specs=[row(D), row(D), row(DFF), row(2 * DFF), pl.BlockSpec((1, D), lambda i: (0, 0))],
        out_shape=[jax.ShapeDtypeStruct((T, D), F32),
                   jax.ShapeDtypeStruct((T, D), BF16),
                   jax.ShapeDtypeStruct((T, DFF), BF16),
                   jax.ShapeDtypeStruct((T, 2 * DFF), BF16),
                   jax.ShapeDtypeStruct((1, D), F32)],
        args=(h, dout, norm, wgu, wgu, wd))


def _wgrad(x, y, name, scale=1.0, bk=None, bn=None, shards=0, bt=512, rows=None, tasks=()):
    T = x.shape[0]
    N = y.shape[1]
    bk = bk or x.shape[1]
    bn = bn or N
    k0, nk = rows or (0, x.shape[1] // bk)
    K = nk * bk
    bt = _tile(T, (bt, 512, 384, 256, 128))
    nt = T // bt

    def body(x_ref, y_ref, o_ref, acc_ref):
        t = pl.program_id(2)

        @pl.when(t == 0)
        def _():
            acc_ref[...] = jnp.zeros_like(acc_ref)

        acc_ref[...] += _dot(x_ref[...].astype(BF16), y_ref[...].astype(BF16), TN)

        @pl.when(t == nt - 1)
        def _():
            res = (acc_ref[...] * scale).astype(o_ref.dtype)
            if shards:
                w = bn // shards
                for s in range(shards):
                    o_ref[s] = res[:, s * w:(s + 1) * w]
            else:
                o_ref[...] = res

    if shards:
        out_spec = pl.BlockSpec((shards, bk, bn // shards), lambda i, j, t: (j, i, 0))
        out_shape = jax.ShapeDtypeStruct((N * shards // bn, K, bn // shards), GRAD_DTYPE)
    else:
        out_spec = pl.BlockSpec((bk, bn), lambda i, j, t: (i, j))
        out_shape = jax.ShapeDtypeStruct((K, N), GRAD_DTYPE)
    (out,), touts = _call_tasks(
        body, tasks, name=name, grid=(nk, N // bn, nt),
        in_specs=[pl.BlockSpec((bt, bk), lambda i, j, t: (t, i + k0)),
                  pl.BlockSpec((bt, bn), lambda i, j, t: (t, j))],
        out_specs=[out_spec], out_shape=[out_shape],
        scratch_shapes=[pltpu.VMEM((bk, bn), F32)],
        args=(x, y))
    return out, touts


def _inproj_fwd(h, norm, w, name):
    T = h.shape[0]
    tm = _tile(T, (512, 384, 256, 128))

    def body(h_ref, n_ref, w_ref, o_ref, u_ref):
        u, _, _ = _rms(h_ref[...], n_ref[...], D)
        ub = u.astype(BF16)
        u_ref[...] = ub
        o_ref[...] = _dot(ub, w_ref[...])

    return _call(
        body, name=name, grid=(T // tm,),
        in_specs=[pl.BlockSpec((tm, D), lambda i: (i, 0)),
                  pl.BlockSpec((1, D), lambda i: (0, 0)),
                  pl.BlockSpec((D, PROJW), lambda i: (0, 0), pipeline_mode=pl.Buffered(1))],
        out_specs=[pl.BlockSpec((tm, PROJW), lambda i: (i, 0)),
                   pl.BlockSpec((tm, D), lambda i: (i, 0))],
        out_shape=[jax.ShapeDtypeStruct((T, PROJW), F32), jax.ShapeDtypeStruct((T, D), BF16)],
        compiler_params=_cp(),
    )(h, norm, w)


def _inproj_bwd(h, dres, dlo, dhi, norm, w, name):
    T = h.shape[0]
    tm = _tile(T, (512, 384, 256, 128))
    hw = PROJW // 2

    def body(h_ref, d_ref, lo_ref, hi_ref, n_ref, wlo_ref, whi_ref, dh_ref, dn_ref):
        @pl.when(pl.program_id(0) == 0)
        def _():
            dn_ref[...] = jnp.zeros_like(dn_ref)

        _, xh, r = _rms(h_ref[...], n_ref[...], D)
        du = _dot(lo_ref[...], wlo_ref[...], NT) + _dot(hi_ref[...], whi_ref[...], NT)
        dn_ref[...] += _colsum(du * xh)
        dh_ref[...] = d_ref[...] + _rms_bwd(du, xh, r, n_ref[...], D)

    return _call(
        body, name=name, grid=(T // tm,),
        in_specs=[pl.BlockSpec((tm, D), lambda i: (i, 0)),
                  pl.BlockSpec((tm, D), lambda i: (i, 0)),
                  pl.BlockSpec((tm, hw), lambda i: (i, 0)),
                  pl.BlockSpec((tm, hw), lambda i: (i, 0)),
                  pl.BlockSpec((1, D), lambda i: (0, 0)),
                  pl.BlockSpec((D, hw), lambda i: (0, 0)),
                  pl.BlockSpec((D, hw), lambda i: (0, 1))],
        out_specs=[pl.BlockSpec((tm, D), lambda i: (i, 0)),
                   pl.BlockSpec((1, D), lambda i: (0, 0))],
        out_shape=[jax.ShapeDtypeStruct((T, D), F32), jax.ShapeDtypeStruct((1, D), F32)],
        compiler_params=_cp(),
    )(h, dres, dlo, dhi, norm, w, w)


C_FQ, C_FK, C_FV, C_CQ, C_CKV, C_MISC, C_GATE = 0, 512, 1024, 1536, 1792, 1920, 2048
L_KR, L_FL = 64, 96


def _prep_fwd(proj, rc, rs, gfq, gfk, gcq, gckv, gmq, gmk, bfv, wuq, wuk, wuv, name):
    T = proj.shape[0]
    tm = _tile(T, (256, 128))

    def body(p_ref, rc_ref, rs_ref, gfq_ref, gfk_ref, gcq_ref, gckv_ref, gmq_ref, gmk_ref, bf_ref,
             wuq_ref, wuk_ref, wuv_ref, fq_ref, fk_ref, fv_ref, qm_ref, km_ref, vm_ref, lf_ref):
        o64, o128, o256 = _seg_ones(128, 6), _seg_ones(128, 7), _seg_ones(256, 8)
        for blk in range(4):
            for (c0, g_ref, o_ref) in ((C_FQ, gfq_ref, fq_ref), (C_FK, gfk_ref, fk_ref)):
                x = p_ref[:, c0 + 128 * blk:c0 + 128 * (blk + 1)]
                fn, _, _ = _rms(x, g_ref[...], 64, o64)
                o_ref[:, 128 * blk:128 * (blk + 1)] = fn.astype(BF16)
        fv_ref[...] = p_ref[:, C_FV:C_FV + 512].astype(BF16)

        rcv = rc_ref[...]
        rsv = rs_ref[...]
        cqn, _, _ = _rms(p_ref[:, C_CQ:C_CQ + QR], gcq_ref[...], QR, o256)
        qpre = _dot(cqn.astype(BF16), wuq_ref[...])
        ckvn, _, _ = _rms(p_ref[:, C_CKV:C_CKV + KVR], gckv_ref[...], KVR, o128)
        ckvb = ckvn.astype(BF16)
        kpre = _dot(ckvb, wuk_ref[...])
        vm_ref[...] = _dot(ckvb, wuv_ref[...]).astype(BF16)
        misc = p_ref[:, C_MISC:C_MISC + 128]
        ln = _lane(misc.shape)
        kr = jnp.where(jnp.logical_and(ln >= L_KR, ln < L_KR + ROPE), misc, 0.0)
        for hh in range(NH):
            sl = slice(128 * hh, 128 * (hh + 1))
            qn, _, _ = _rms(qpre[:, sl], gmq_ref[...], MLA_QK, o128)
            qm_ref[:, sl] = (qn * rcv + _rope_swap(qn) * rsv).astype(BF16)
            kn, _, _ = _rms(kpre[:, sl] + kr, gmk_ref[...], MLA_QK, o128)
            km_ref[:, sl] = (kn * rcv + _rope_swap(kn) * rsv).astype(BF16)
        z = misc + bf_ref[...]
        lf_ref[...] = jnp.minimum(z, 0.0) - jnp.log(1.0 + jnp.exp(-jnp.abs(z)))

    row = lambda w: pl.BlockSpec((tm, w), lambda i: (i, 0))
    full = lambda a: pl.BlockSpec(a.shape, lambda i: (0, 0))
    return _call(
        body, name=name, grid=(T // tm,),
        in_specs=[row(PROJW // 2), row(128), row(128)] + [full(a) for a in (gfq, gfk, gcq, gckv, gmq, gmk, bfv, wuq, wuk, wuv)],
        out_specs=[row(512), row(512), row(512), row(1024), row(1024), row(512), row(128)],
        out_shape=[jax.ShapeDtypeStruct((T, 512), BF16), jax.ShapeDtypeStruct((T, 512), BF16),
                   jax.ShapeDtypeStruct((T, 512), BF16), jax.ShapeDtypeStruct((T, 1024), BF16),
                   jax.ShapeDtypeStruct((T, 1024), BF16), jax.ShapeDtypeStruct((T, 512), BF16),
                   jax.ShapeDtypeStruct((T, 128), F32)],
        compiler_params=_cp(),
    )(proj, rc, rs, gfq, gfk, gcq, gckv, gmq, gmk, bfv, wuq, wuk, wuv)


def _prep_bwd(proj, rc, rs, gfq, gfk, gcq, gckv, gmq, gmk, bfv, wuq, wuk, wuv,
              dfq, dfk, dfv, dqm, dkm, dvm, dlf, name):
    T = proj.shape[0]
    tm = _tile(T, (256, 128))

    def body(p_ref, rc_ref, rs_ref, gfq_ref, gfk_ref, gcq_ref, gckv_ref, gmq_ref, gmk_ref, bf_ref,
             wuq_ref, wuk_ref, wuv_ref, dfq_ref, dfk_ref, dfv_ref, dqm_ref, dkm_ref, dvm_ref, dlf_ref,
             dp_ref, dgfq_ref, dgfk_ref, dgcq_ref, dgckv_ref, dgmq_ref, dgmk_ref, dbf_ref,
             dwuq_ref, dwuk_ref, dwuv_ref, dqpre_sc, dkpre_sc):
        accs = (dgfq_ref, dgfk_ref, dgcq_ref, dgckv_ref, dgmq_ref, dgmk_ref, dbf_ref, dwuq_ref, dwuk_ref, dwuv_ref)

        @pl.when(pl.program_id(0) == 0)
        def _():
            for a in accs:
                a[...] = jnp.zeros_like(a)

        o64, o128, o256 = _seg_ones(128, 6), _seg_ones(128, 7), _seg_ones(256, 8)
        for (c0, g_ref, d_ref, dg_ref) in ((C_FQ, gfq_ref, dfq_ref, dgfq_ref), (C_FK, gfk_ref, dfk_ref, dgfk_ref)):
            dg = jnp.zeros((1, 128), F32)
            for blk in range(4):
                x = p_ref[:, c0 + 128 * blk:c0 + 128 * (blk + 1)]
                _, xh, r = _rms(x, g_ref[...], 64, o64)
                dy = d_ref[:, 128 * blk:128 * (blk + 1)]
                dg = dg + _colsum(dy * xh)
                dp_ref[:, c0 + 128 * blk:c0 + 128 * (blk + 1)] = _rms_bwd(dy, xh, r, g_ref[...], 64, o64).astype(BF16)
            dg_ref[...] += dg
        dp_ref[:, C_FV:C_FV + 512] = dfv_ref[...].astype(BF16)

        rcv = rc_ref[...]
        rsv = rs_ref[...]
        cqn, cqh, cqr = _rms(p_ref[:, C_CQ:C_CQ + QR], gcq_ref[...], QR, o256)
        cqb = cqn.astype(BF16)
        qpre = _dot(cqb, wuq_ref[...])
        dgq = jnp.zeros((1, 128), F32)
        for hh in range(NH):
            sl = slice(128 * hh, 128 * (hh + 1))
            _, xh, r = _rms(qpre[:, sl], gmq_ref[...], MLA_QK, o128)
            dout = dqm_ref[:, sl]
            dqn = dout * rcv + _rope_swap(dout * rsv)
            dgq = dgq + _colsum(dqn * xh)
            dqpre_sc[:, sl] = _rms_bwd(dqn, xh, r, gmq_ref[...], MLA_QK, o128).astype(BF16)
        dgmq_ref[...] += dgq
        dqpre = dqpre_sc[...]
        dwuq_ref[...] += _dot(cqb, dqpre, TN)
        dcqn = _dot(dqpre, wuq_ref[...], NT)
        dgcq_ref[...] += _colsum(dcqn * cqh)
        dp_ref[:, C_CQ:C_CQ + QR] = _rms_bwd(dcqn, cqh, cqr, gcq_ref[...], QR, o256).astype(BF16)

        ckvn, ckvh, ckvr = _rms(p_ref[:, C_CKV:C_CKV + KVR], gckv_ref[...], KVR, o128)
        ckvb = ckvn.astype(BF16)
        kpre = _dot(ckvb, wuk_ref[...])
        misc = p_ref[:, C_MISC:C_MISC + 128]
        ln = _lane(misc.shape)
        is_kr = jnp.logical_and(ln >= L_KR, ln < L_KR + ROPE)
        kr = jnp.where(is_kr, misc, 0.0)
        dgk = jnp.zeros((1, 128), F32)
        dkr = jnp.zeros(misc.shape, F32)
        for hh in range(NH):
            sl = slice(128 * hh, 128 * (hh + 1))
            _, xh, r = _rms(kpre[:, sl] + kr, gmk_ref[...], MLA_QK, o128)
            dout = dkm_ref[:, sl]
            dkn = dout * rcv + _rope_swap(dout * rsv)
            dgk = dgk + _colsum(dkn * xh)
            dkx = _rms_bwd(dkn, xh, r, gmk_ref[...], MLA_QK, o128)
            dkr = dkr + jnp.where(is_kr, dkx, 0.0)
            dkpre_sc[:, sl] = jnp.where(ln < 64, dkx, 0.0).astype(BF16)
        dgmk_ref[...] += dgk
        dkpre = dkpre_sc[...]
        dvmb = dvm_ref[...].astype(BF16)
        dwuk_ref[...] += _dot(ckvb, dkpre, TN)
        dwuv_ref[...] += _dot(ckvb, dvmb, TN)
        dckvn = _dot(dkpre, wuk_ref[...], NT) + _dot(dvmb, wuv_ref[...], NT)
        dgckv_ref[...] += _colsum(dckvn * ckvh)
        dp_ref[:, C_CKV:C_CKV + KVR] = _rms_bwd(dckvn, ckvh, ckvr, gckv_ref[...], KVR, o128).astype(BF16)

        z = misc + bf_ref[...]
        dz = dlf_ref[...] * (1.0 - jax.nn.sigmoid(z))
        dbf_ref[...] += _colsum(dz)
        dp_ref[:, C_MISC:C_MISC + 128] = (dkr + dz).astype(BF16)

    row = lambda w: pl.BlockSpec((tm, w), lambda i: (i, 0))
    full = lambda a: pl.BlockSpec(a.shape, lambda i: (0, 0))
    small = (gfq, gfk, gcq, gckv, gmq, gmk, bfv, wuq, wuk, wuv)
    acc_shapes = [(1, 128), (1, 128), (1, QR), (1, KVR), (1, 128), (1, 128), (1, 128),
                  (QR, 1024), (KVR, 1024), (KVR, 512)]
    return _call(
        body, name=name, grid=(T // tm,),
        in_specs=[row(PROJW // 2), row(128), row(128)] + [full(a) for a in small]
                 + [row(512), row(512), row(512), row(1024), row(1024), row(512), row(128)],
        out_specs=[row(PROJW // 2)] + [pl.BlockSpec(s, lambda i: (0, 0)) for s in acc_shapes],
        out_shape=[jax.ShapeDtypeStruct((T, PROJW // 2), BF16)] + [jax.ShapeDtypeStruct(s, F32) for s in acc_shapes],
        scratch_shapes=[pltpu.VMEM((tm, 1024), BF16), pltpu.VMEM((tm, 1024), BF16)],
        compiler_params=_cp(),
    )(proj, rc, rs, *small, dfq, dfk, dfv, dqm, dkm, dvm, dlf)


def _scan_lanes(x, reverse):
    n = x.shape[-1]
    ln = _lane(x.shape)
    k = 1
    while k < n:
        if reverse:
            x = x + jnp.where(ln < n - k, pltpu.roll(x, n - k, x.ndim - 1), 0.0)
        else:
            x = x + jnp.where(ln >= k, pltpu.roll(x, k, x.ndim - 1), 0.0)
        k *= 2
    return x


def _forget_scan(lf, reverse, name):
    def body(x_ref, o_ref):
        x = x_ref[...]
        ln = _lane(x.shape)
        pad = jnp.logical_and(ln >= NMETA, ln < MPAD)
        o_ref[...] = jnp.where(pad, 0.0, _scan_lanes(jnp.where(pad, 0.0, x), reverse))

    return _call(body, name=name, out_shape=jax.ShapeDtypeStruct(lf.shape, F32), compiler_params=_cp())(lf)


def _attn_blocks(LP, tq):
    return [(0, MPAD, MPAD)] + [(MPAD + i * tq, tq, MPAD + (i + 1) * tq) for i in range((LP - MPAD) // tq)]


def _attn_scores(q_ref, k_ref, e, r0, rn, kend, wide, scale, bias):
    if wide:
        qe = q_ref[r0:r0 + rn, 128 * e:128 * (e + 1)]
        ke = k_ref[0:kend, 128 * e:128 * (e + 1)]
    else:
        qb = q_ref[r0:r0 + rn, :]
        mine = (_lane(qb.shape) < 64) if e == 0 else (_lane(qb.shape) >= 64)
        qe = jnp.where(mine, qb, jnp.zeros_like(qb))
        ke = k_ref[0:kend, :]
    s = _dot(qe, ke, NT) * scale
    if bias is not None:
        ct_ref, cr_ref = bias
        s = s + ct_ref[0, r0:r0 + rn, e:e + 1] - cr_ref[0, :, 0:kend]
    neg = -1e30
    if r0 == 0:
        qi = lax.broadcasted_iota(jnp.int32, (rn, kend), 0)
        ki = lax.broadcasted_iota(jnp.int32, (rn, kend), 1)
        s = jnp.where(jnp.logical_and(ki <= qi, ki < NMETA), s, neg)
    else:
        d0 = kend - rn
        head = jnp.where(_lane((rn, MPAD)) < NMETA, s[:, :MPAD], neg)
        qi = lax.broadcasted_iota(jnp.int32, (rn, rn), 0)
        diag = jnp.where(_lane((rn, rn)) <= qi, s[:, d0:], neg)
        s = jnp.concatenate([head] + ([s[:, MPAD:d0]] if d0 > MPAD else []) + [diag], axis=1)
    m = jnp.max(s, axis=-1, keepdims=True)
    p = jnp.exp(s - m)
    l = jnp.sum(p, axis=-1, keepdims=True)
    return qe, ke, p, l


def _attn_specs(B, LP, wide, has_bias):
    qw = 256 if wide else 128
    specs = [pl.BlockSpec((LP, qw), lambda b, hp: (b, hp)),
             pl.BlockSpec((LP, qw), lambda b, hp: (b, hp)),
             pl.BlockSpec((LP, 128), lambda b, hp: (b, hp))]
    bias_specs = []
    if has_bias:
        bias_specs = [pl.BlockSpec((1, LP, 2), lambda b, hp: (b * 4 + hp, 0, 0)),
                      pl.BlockSpec((1, 1, LP), lambda b, hp: (b * 8 + 2 * hp, 0, 0)),
                      pl.BlockSpec((1, 1, LP), lambda b, hp: (b * 8 + 2 * hp + 1, 0, 0))]
    return qw, specs, bias_specs


def _attn_fwd(q, k, v, bias, B, LP, wide, scale, name, tasks=()):
    T = q.shape[0]
    blocks = _attn_blocks(LP, ATTN_TQ)
    qw, specs, bias_specs = _attn_specs(B, LP, wide, bias is not None)

    def body(*refs):
        if bias is not None:
            q_ref, k_ref, v_ref, ct_ref, cr0_ref, cr1_ref, o_ref = refs
            crs = (cr0_ref, cr1_ref)
        else:
            q_ref, k_ref, v_ref, o_ref = refs
        for (r0, rn, kend) in blocks:
            outs = []
            for e in (0, 1):
                bs = (ct_ref, crs[e]) if bias is not None else None
                _, _, p, l = _attn_scores(q_ref, k_ref, e, r0, rn, kend, wide, scale, bs)
                outs.append(_dot(p.astype(BF16), v_ref[0:kend, :]) / l)
            o = jnp.where(_lane(outs[0].shape) < 64, outs[0], outs[1])
            o_ref[r0:r0 + rn, :] = o.astype(BF16)

    args = (q, k, v) + ((bias[0], bias[1], bias[1]) if bias is not None else ())
    (out,), touts = _call_tasks(
        body, tasks, name=name, grid=(B, 4),
        in_specs=specs + bias_specs,
        out_specs=[pl.BlockSpec((LP, 128), lambda b, hp: (b, hp))],
        out_shape=[jax.ShapeDtypeStruct((T, 512), BF16)],
        args=args)
    return out, touts


def _attn_bwd(q, k, v, do, bias, B, LP, wide, scale, name, tasks=()):
    T = q.shape[0]
    blocks = _attn_blocks(LP, ATTN_TQ)
    qw, specs, bias_specs = _attn_specs(B, LP, wide, bias is not None)
    has_bias = bias is not None

    def body(*refs):
        if has_bias:
            (q_ref, k_ref, v_ref, do_ref, ct_ref, cr0_ref, cr1_ref,
             dq_ref, dk_ref, dv_ref, dc0_ref, dc1_ref) = refs
            crs = (cr0_ref, cr1_ref)
            dcs = (dc0_ref, dc1_ref)
            dc0_ref[...] = jnp.zeros_like(dc0_ref)
            dc1_ref[...] = jnp.zeros_like(dc1_ref)
        else:
            q_ref, k_ref, v_ref, do_ref, dq_ref, dk_ref, dv_ref = refs
        dk_ref[...] = jnp.zeros_like(dk_ref)
        dv_ref[...] = jnp.zeros_like(dv_ref)
        for (r0, rn, kend) in blocks:
            dqs = []
            for e in (0, 1):
                bs = (ct_ref, crs[e]) if has_bias else None
                qe, ke, p, l = _attn_scores(q_ref, k_ref, e, r0, rn, kend, wide, scale, bs)
                pn = p * (1.0 / l)
                dob = do_ref[r0:r0 + rn, :]
                mine = (_lane(dob.shape) < 64) if e == 0 else (_lane(dob.shape) >= 64)
                doe = jnp.where(mine, dob, jnp.zeros_like(dob))
                dp = _dot(doe, v_ref[0:kend, :], NT)
                delta = jnp.sum(pn * dp, axis=-1, keepdims=True)
                ds = pn * (dp - delta)
                dsb = ds.astype(BF16)
                dqe = _dot(dsb, ke) * scale
                dke = _dot(dsb, qe, TN) * scale
                if wide:
                    dq_ref[r0:r0 + rn, 128 * e:128 * (e + 1)] = dqe
                    dk_ref[0:kend, 128 * e:128 * (e + 1)] += dke
                else:
                    dqs.append(dqe)
                    dk_ref[0:kend, :] += dke
                dv_ref[0:kend, :] += _dot(pn.astype(BF16), doe, TN)
                if has_bias:
                    dcs[e][0, :, 0:kend] -= _colsum(ds)
            if not wide:
                dq_ref[r0:r0 + rn, :] = jnp.where(_lane(dqs[0].shape) < 64, dqs[0], dqs[1])

    args = (q, k, v, do) + ((bias[0], bias[1], bias[1]) if has_bias else ())
    out_specs = [pl.BlockSpec((LP, qw), lambda b, hp: (b, hp)),
                 pl.BlockSpec((LP, qw), lambda b, hp: (b, hp)),
                 pl.BlockSpec((LP, 128), lambda b, hp: (b, hp))]
    out_shape = [jax.ShapeDtypeStruct(q.shape, F32), jax.ShapeDtypeStruct(q.shape, F32),
                 jax.ShapeDtypeStruct((T, 512), F32)]
    if has_bias:
        out_specs += [pl.BlockSpec((1, 1, LP), lambda b, hp: (b * 4 + hp, 0, 0))] * 2
        out_shape += [jax.ShapeDtypeStruct((B * 4, 1, LP), F32)] * 2
    return _call_tasks(
        body, tasks, name=name, grid=(B, 4),
        in_specs=specs + [pl.BlockSpec((LP, 128), lambda b, hp: (b, hp))] + bias_specs,
        out_specs=out_specs, out_shape=out_shape, args=args)


def _post_fwd(h, of, om, proj, bg, wbf, wbm, wout, name):
    T = h.shape[0]
    tm = _tile(T, (512, 384, 256, 128))

    def body(h_ref, of_ref, om_ref, gl_ref, bg_ref, wbf_ref, wbm_ref, wo_ref, o_ref, mix_ref):
        gate = jax.nn.sigmoid(gl_ref[...] + bg_ref[...])
        mix = gate[:, :D] * _dot(of_ref[...], wbf_ref[...]) + gate[:, D:] * _dot(om_ref[...], wbm_ref[...])
        mb = mix.astype(BF16)
        mix_ref[...] = mb
        o_ref[...] = h_ref[...] + _dot(mb, wo_ref[...])

    row = lambda w: pl.BlockSpec((tm, w), lambda i: (i, 0))
    full = lambda a: pl.BlockSpec(a.shape, lambda i: (0, 0))
    return _call(
        body, name=name, grid=(T // tm,),
        in_specs=[row(D), row(512), row(512), pl.BlockSpec((tm, 2 * D), lambda i: (i, 1)),
                  full(bg), full(wbf), full(wbm), full(wout)],
        out_specs=[row(D), row(D)],
        out_shape=[jax.ShapeDtypeStruct((T, D), F32), jax.ShapeDtypeStruct((T, D), BF16)],
        compiler_params=_cp(),
    )(h, of, om, proj, bg, wbf, wbm, wout)


def _post_bwd(dh, of, om, proj, bg, wbf, wbm, wout, name):
    T = dh.shape[0]
    tm = _tile(T, (512, 384, 256, 128))

    def body(d_ref, of_ref, om_ref, gl_ref, bg_ref, wbf_ref, wbm_ref, wo_ref,
             dgl_ref, dbf_ref, dbm_ref, dof_ref, dom_ref, dbg_ref):
        @pl.when(pl.program_id(0) == 0)
        def _():
            dbg_ref[...] = jnp.zeros_like(dbg_ref)

        gate = jax.nn.sigmoid(gl_ref[...] + bg_ref[...])
        dmix = _dot(d_ref[...].astype(BF16), wo_ref[...], NT)
        ofx = _dot(of_ref[...], wbf_ref[...])
        omx = _dot(om_ref[...], wbm_ref[...])
        gf = gate[:, :D]
        gm = gate[:, D:]
        dof = (dmix * gf).astype(BF16)
        dom = (dmix * gm).astype(BF16)
        dglf = dmix * ofx * gf * (1.0 - gf)
        dglm = dmix * omx * gm * (1.0 - gm)
        dgl_ref[:, :D] = dglf.astype(BF16)
        dgl_ref[:, D:] = dglm.astype(BF16)
        dbg_ref[:, :D] += _colsum(dglf)
        dbg_ref[:, D:] += _colsum(dglm)
        dbf_ref[...] = dof
        dbm_ref[...] = dom
        dof_ref[...] = _dot(dof, wbf_ref[...], NT).astype(BF16)
        dom_ref[...] = _dot(dom, wbm_ref[...], NT).astype(BF16)

    row = lambda w: pl.BlockSpec((tm, w), lambda i: (i, 0))
    full = lambda a: pl.BlockSpec(a.shape, lambda i: (0, 0))
    return _call(
        body, name=name, grid=(T // tm,),
        in_specs=[row(D), row(512), row(512), pl.BlockSpec((tm, 2 * D), lambda i: (i, 1)),
                  full(bg), full(wbf), full(wbm), full(wout)],
        out_specs=[row(2 * D), row(D), row(D), row(512), row(512), pl.BlockSpec((1, 2 * D), lambda i: (0, 0))],
        out_shape=[jax.ShapeDtypeStruct((T, 2 * D), BF16), jax.ShapeDtypeStruct((T, D), BF16),
                   jax.ShapeDtypeStruct((T, D), BF16), jax.ShapeDtypeStruct((T, 512), BF16),
                   jax.ShapeDtypeStruct((T, 512), BF16), jax.ShapeDtypeStruct((1, 2 * D), F32)],
        compiler_params=_cp(),
    )(dh, of, om, proj, bg, wbf, wbm, wout)


def _loss_head(h3, target, B, LP, name):
    S = LP - MPAD
    half = LP // 2
    first = half - MPAD

    def body(h_ref, t_ref, dy_ref, l_ref):
        b = pl.program_id(0)
        k = pl.program_id(1)

        @pl.when(jnp.logical_and(b == 0, k == 0))
        def _():
            l_ref[...] = jnp.zeros_like(l_ref)

        @pl.when(k == 0)
        def _():
            e = h_ref[MPAD:, :] - t_ref[0, 0:first, :]
            dy_ref[0:MPAD, :] = jnp.zeros((MPAD, D), F32)
            dy_ref[MPAD:, :] = e * (1.0 / D)
            l_ref[...] += jnp.sum(e * e, axis=0, keepdims=True) * (0.5 / D)

        @pl.when(k == 1)
        def _():
            e = h_ref[...] - t_ref[0, first:S, :]
            dy_ref[...] = e * (1.0 / D)
            l_ref[...] += jnp.sum(e * e, axis=0, keepdims=True) * (0.5 / D)

    return _call(
        body, name=name, grid=(B, 2),
        in_specs=[pl.BlockSpec((half, D), lambda b, k: (b * 2 + k, 0)),
                  pl.BlockSpec((1, S, D), lambda b, k: (b, 0, 0))],
        out_specs=[pl.BlockSpec((half, D), lambda b, k: (b * 2 + k, 0)),
                   pl.BlockSpec((1, D), lambda b, k: (0, 0))],
        out_shape=[jax.ShapeDtypeStruct(h3.shape, F32), jax.ShapeDtypeStruct((1, D), F32)],
        compiler_params=_cp(),
    )(h3, target)


def _rope_tables(B, LP):
    pos = jnp.concatenate([jnp.arange(MPAD, dtype=F32), NMETA + jnp.arange(LP - MPAD, dtype=F32)])
    inv_freq = ROPE_THETA ** (-jnp.arange(0, ROPE, 2, dtype=F32) / ROPE)
    ang = pos[:, None] * inv_freq[None, :]
    cos, sin = jnp.cos(ang), jnp.sin(ang)
    z32 = jnp.zeros((LP, 32), F32)
    rc = jnp.concatenate([jnp.ones((LP, 64), F32), cos, cos, z32], axis=1)
    rs = jnp.concatenate([jnp.zeros((LP, 64), F32), -sin, sin, z32], axis=1)
    return jnp.tile(rc, (B, 1)), jnp.tile(rs, (B, 1))


def _pad_lanes(v, start, width=128):
    n = v.shape[1]
    return jnp.concatenate([jnp.zeros((1, start), F32), v, jnp.zeros((1, width - start - n), F32)], axis=1)


G_FFN1 = ["ffn1_w_gu", "ffn1_w_down"]
G_MIX = ["w_in", "mla_w_uq", "mla_w_ukv", "w_branch_fox", "w_branch_mla", "w_out"]
G_OUT = ["w_out", "w_branch_fox", "w_branch_mla"]
G_IN = ["w_in", "mla_w_uq", "mla_w_ukv"]


def _step(x, target, meta, vec, gath, shards):
    dist = shards is not None
    B, S, _ = x.shape
    LP = MPAD + S
    T = B * LP
    gath = dict(gath)

    def gather(names, wide=()):
        return [_gather_task([shards[n] for n in names], wide)] if dist else []

    def flat_gu(w):
        return w if w.ndim == 2 else _cols_from_shards(w)

    def gathered(names, touts):
        if dist:
            gath.update(zip(names, touts[0]))

    g4, sums, red = {}, {}, {}

    def scatter(names):
        return [_a2a_task([_pieces(g4[n]) for n in names])] if dist else []

    def scattered(names, tout, me):
        for n, r in zip(names, tout):
            sums[n] = _sum_pieces(r, _pieces(g4[n]), me, "rs_sum_" + n)

    def join(names):
        return [_join_task([sums[n] for n in names])] if dist else []

    def joined(names, tout):
        for n, r in zip(names, tout):
            red[n] = (sums[n], r)

    me = None
    if dist:
        me = jnp.tile((4 * lax.axis_index("x") + 2 * lax.axis_index("y") + lax.axis_index("c")).reshape(1), 2).astype(jnp.int32)

    h0 = jnp.concatenate([jnp.broadcast_to(meta[None], (B, NMETA, D)),
                          jnp.zeros((B, MPAD - NMETA, D), F32), x], axis=1).reshape(T, D)
    rc, rs = _rope_tables(B, LP)
    gfq = jnp.tile(vec["fox_q_norm"], (1, 2))
    gfk = jnp.tile(vec["fox_k_norm"], (1, 2))
    gmq = _pad_lanes(vec["mla_q_norm"], 0)
    gmk = _pad_lanes(vec["mla_k_norm"], 0)
    bfv = _pad_lanes(vec["b_forget"], L_FL)

    w1gu, w1d = flat_gu(gath["ffn1_w_gu"]), gath["ffn1_w_down"].reshape(DFF, D)
    h1, touts = _ffn_fwd(h0, vec["ffn1_norm"], w1gu, w1d, "ffn1_fwd", gather(G_MIX))
    gathered(G_MIX, touts)
    wm = _mixer_weights(gath)
    small = (gfq, gfk, vec["mla_cq_norm"], vec["mla_ckv_norm"], gmq, gmk, bfv, wm["wuq"], wm["wuk"], wm["wuv"])
    proj, u2 = _inproj_fwd(h1, vec["mix_norm"], wm["w_in"], "inproj_fwd")
    fq, fk, fv, qm, km, vm, lf = _prep_fwd(proj, rc, rs, *small, name="prep_fwd")
    lf_rows = lf[:, L_FL:L_FL + NH].reshape(B, LP, NH).transpose(0, 2, 1).reshape(B * NH, LP)
    crow = _forget_scan(lf_rows, False, "forget_scan")
    ctok = crow.reshape(B, 4, 2, LP).transpose(0, 1, 3, 2).reshape(B * 4, LP, 2)
    bias = (ctok, crow.reshape(B * NH, 1, LP))
    of, touts = _attn_fwd(fq, fk, fv, bias, B, LP, False, 64 ** -0.5, "fox_fwd", gather(["ffn2_w_gu"], wide=(0,)))
    gathered(["ffn2_w_gu"], touts)
    om, touts = _attn_fwd(qm, km, vm, None, B, LP, True, MLA_QK ** -0.5, "mla_fwd", gather(["ffn2_w_down"]))
    gathered(["ffn2_w_down"], touts)
    h2, mix = _post_fwd(h1, of, om, proj, vec["b_gate"], wm["wbf"], wm["wbm"], wm["w_out"], "post_fwd")
    w2gu, w2d = flat_gu(gath["ffn2_w_gu"]), gath["ffn2_w_down"].reshape(DFF, D)
    h3, _ = _ffn_fwd(h2, vec["ffn2_norm"], w2gu, w2d, "ffn2_fwd")
    dy, lpart = _loss_head(h3, target, B, LP, "loss_head")

    gv = {}
    (dh2, u3, a2, dgp2, gv["ffn2_norm"]), _ = _ffn_bwd(h2, dy, vec["ffn2_norm"], w2gu, w2d, "ffn2_bwd")
    g4["ffn2_w_gu"] = _wgrad(u3, dgp2, "ffn2_dwgu", bn=DFF, shards=2)[0]
    g4["ffn2_w_down"] = _wgrad(a2, dy, "ffn2_dwd", scale=0.5, bk=FH, bt=2176)[0].reshape(N_CHIPS, DFF // N_CHIPS, D)

    dgl, dbf, dbm, dof, dom, gv["b_gate"] = _post_bwd(dh2, of, om, proj, vec["b_gate"], wm["wbf"], wm["wbm"], wm["w_out"], "post_bwd")
    g4["w_out"] = _wgrad(mix, dh2, "dw_out", bt=2176)[0].reshape(N_CHIPS, D // N_CHIPS, D)
    g4["w_branch_fox"] = _cols_to_shards(_wgrad(of, dbf, "dw_bf", bt=2176)[0])
    g4["w_branch_mla"] = _cols_to_shards(_wgrad(om, dbm, "dw_bm", bt=2176)[0])
    G_FFN2 = ["ffn2_w_gu", "ffn2_w_down"]
    (dfq, dfk, dfv, dc0, dc1), touts = _attn_bwd(fq, fk, fv, dof, bias, B, LP, False, 64 ** -0.5, "fox_bwd", scatter(G_FFN2))
    if dist:
        scattered(G_FFN2, touts[0], me)
    (dqm, dkm, dvm), touts = _attn_bwd(qm, km, vm, dom, None, B, LP, True, MLA_QK ** -0.5, "mla_bwd",
                                       scatter(G_OUT) + join(G_FFN2))
    if dist:
        scattered(G_OUT, touts[0], me)
        joined(G_FFN2, touts[1])
    dc = jnp.concatenate([dc0, dc1], axis=1).reshape(B * NH, LP)
    dlf_rows = _forget_scan(dc, True, "forget_scan_bwd")
    dlf = dlf_rows.reshape(B, NH, LP).transpose(0, 2, 1).reshape(T, NH)
    dlf = jnp.concatenate([jnp.zeros((T, L_FL), F32), dlf, jnp.zeros((T, 128 - L_FL - NH), F32)], axis=1)
    (dlo, dgfq, dgfk, gv["mla_cq_norm"], gv["mla_ckv_norm"], dgmq, dgmk, dbfv,
     dwuq, dwuk, dwuv) = _prep_bwd(proj, rc, rs, *small, dfq, dfk, dfv, dqm, dkm, dvm, dlf, name="prep_bwd")
    gv["fox_q_norm"] = dgfq[:, :64] + dgfq[:, 64:]
    gv["fox_k_norm"] = dgfk[:, :64] + dgfk[:, 64:]
    gv["mla_q_norm"] = dgmq[:, :MLA_QK]
    gv["mla_k_norm"] = dgmk[:, :MLA_QK]
    gv["b_forget"] = dbfv[:, L_FL:L_FL + NH]
    dwin = jnp.concatenate([_wgrad(u2, dlo, "dw_in_lo")[0], _wgrad(u2, dgl, "dw_in_hi")[0]], axis=1)
    g4["w_in"] = _cols_to_shards(_win_from_kernel(dwin))
    g4["mla_w_uq"] = _cols_to_shards(
        dwuq.astype(GRAD_DTYPE).reshape(QR, NH, 128)[:, :, :MLA_QK].reshape(QR, NH * MLA_QK))
    dukv = jnp.concatenate([dwuk.reshape(KVR, NH, 128)[:, :, :64], dwuv.reshape(KVR, NH, 64)], axis=2)
    g4["mla_w_ukv"] = _cols_to_shards(dukv.astype(GRAD_DTYPE).reshape(KVR, NH * 128))
    dh1, gv["mix_norm"] = _inproj_bwd(h1, dh2, dlo, dgl, vec["mix_norm"], wm["w_in"], "inproj_bwd")

    (dh0, u1, a1, dgp1, gv["ffn1_norm"]), touts = _ffn_bwd(h0, dh1, vec["ffn1_norm"], w1gu, w1d, "ffn1_bwd",
                                                            scatter(G_IN) + join(G_OUT))
    if dist:
        scattered(G_IN, touts[0], me)
        joined(G_OUT, touts[1])
    dh0 = dh0.reshape(B, LP, D)
    grad_x = dh0[:, MPAD:]
    grad_meta = jnp.sum(dh0[:, :NMETA], axis=0)
    share = [_share_task([_stack_vectors([gv[n] for n in VECS]), grad_meta, lpart])] if dist else []
    dwd1, touts = _wgrad(a1, dh1, "ffn1_dwd", scale=0.5, bk=FH, bt=2176, tasks=share + join(G_IN))
    g4["ffn1_w_down"] = dwd1.reshape(N_CHIPS, DFF // N_CHIPS, D)
    shared = touts[0] if dist else None
    if dist:
        joined(G_IN, touts[1])
    half = D // 2
    gu = []
    for cr in (0, 1):
        tasks = (scatter(["ffn1_w_down"]) if cr == 0 else [_a2a_core_task([gu[0]], 0)]) if dist else []
        part, touts = _wgrad(u1, dgp1, "ffn1_dwgu_%d" % cr, bk=half, bn=DFF, shards=2, rows=(cr, 1), tasks=tasks)
        gu.append(part)
        if dist and cr == 0:
            scattered(["ffn1_w_down"], touts[0], me)
        elif dist:
            got0 = touts[0][0]
    g4["ffn1_w_gu"] = jnp.concatenate(gu, axis=1)
    if dist:
        got1 = _run_tasks([_a2a_core_task([gu[1]], 1)], "rs_ffn1_w_gu")[0][0]
        core = lax.axis_index("c")
        mine = me.at[1].set(2 * lax.axis_index("x") + lax.axis_index("y"))
        sums["ffn1_w_gu"] = _sum_pieces(jnp.where(core == 0, got0, got1), jnp.where(core == 0, gu[0], gu[1]),
                                        mine, "rs_sum_ffn1_w_gu")
        joined(G_FFN1, _run_tasks(join(G_FFN1), "rs_join_ffn1")[0])
    return lpart, grad_x, grad_meta, gv, (red if dist else g4), shared


def _cols_from_shards(g4):
    n, r, c = g4.shape
    return g4.transpose(1, 0, 2).reshape(r, n * c)


def _cols_to_shards(full):
    r, c4 = full.shape
    return full.reshape(r, N_CHIPS, c4 // N_CHIPS).transpose(1, 0, 2)


def _win_to_kernel(wfull):
    z = lambda n: jnp.zeros((D, n), wfull.dtype)
    fl, cq, ckv, kr, gate = (wfull[:, 1536:1544], wfull[:, 1544:1800], wfull[:, 1800:1928],
                             wfull[:, 1928:1960], wfull[:, 1960:4008])
    misc = jnp.concatenate([z(L_KR), kr, fl, z(128 - L_FL - NH)], axis=1)
    return jnp.concatenate([wfull[:, :1536], cq, ckv, misc, gate], axis=1)


def _win_from_kernel(gk):
    m = C_MISC
    return jnp.concatenate([gk[:, :1536], gk[:, m + L_FL:m + L_FL + NH], gk[:, C_CQ:C_CQ + QR],
                            gk[:, C_CKV:C_CKV + KVR], gk[:, m + L_KR:m + L_KR + ROPE], gk[:, C_GATE:]], axis=1)


def _pieces(g4):
    n, r, c = g4.shape
    return g4.reshape(2 * n, r // 2, c)


def _mixer_weights(gath):
    w = {}
    w["w_in"] = _win_to_kernel(_cols_from_shards(gath["w_in"]))
    uq = _cols_from_shards(gath["mla_w_uq"]).reshape(QR, NH, MLA_QK)
    w["wuq"] = jnp.pad(uq, ((0, 0), (0, 0), (0, 128 - MLA_QK))).reshape(QR, NH * 128)
    ukv = _cols_from_shards(gath["mla_w_ukv"]).reshape(KVR, NH, 128)
    w["wuk"] = jnp.pad(ukv[:, :, :64], ((0, 0), (0, 0), (0, 64))).reshape(KVR, NH * 128)
    w["wuv"] = ukv[:, :, 64:].reshape(KVR, NH * 64)
    w["wbf"] = _cols_from_shards(gath["w_branch_fox"])
    w["wbm"] = _cols_from_shards(gath["w_branch_mla"])
    w["w_out"] = gath["w_out"].reshape(D, D)
    return w


def _chip_peers(x, y):
    return [(1 - x, y), (x, 1 - y), (1 - x, 1 - y)]


RELS = [(dx, dy, dc) for dx in (0, 1) for dy in (0, 1) for dc in (0, 1)][1:]


def _here():
    return lax.axis_index("x"), lax.axis_index("y"), lax.axis_index("c")


def _flip(a, d):
    return (1 - a) if d else a


def _remote(src, dst, send, recv, i, dev):
    return functools.partial(pltpu.make_async_remote_copy, src_ref=src, dst_ref=dst, send_sem=send.at[i],
                             recv_sem=recv.at[i], device_id=dev, device_id_type=MESH)


def _gathered_shape(s, wide):
    return jax.ShapeDtypeStruct((s.shape[0], N_CHIPS * s.shape[1]) if wide else (N_CHIPS,) + s.shape, s.dtype)


def _slot(ref, j, shard, wide, rows=None):
    if wide:
        lanes = pl.ds(pl.multiple_of(j * shard.shape[1], 128), shard.shape[1])
        return ref.at[slice(None) if rows is None else rows, lanes]
    return ref.at[j] if rows is None else ref.at[j, rows]


def _gather_task(shards, wide=()):
    n = len(shards)

    def descs(ins, outs, sems):
        send, recv, loc = sems
        x, y, c = _here()
        j = 2 * x + y
        locs, pairs = [], []
        for k in range(n):
            at = functools.partial(_slot, outs[k], shard=shards[k], wide=k in wide)
            locs.append(functools.partial(pltpu.make_async_copy, ins[k], at(j), loc.at[k]))
            for r, (px, py) in enumerate(_chip_peers(x, y)):
                dev = (px, py, c)
                pairs.append((_remote(ins[k], at(j), send, recv, 3 * k + r, dev),
                              _remote(ins[k], at(2 * px + py), send, recv, 3 * k + r, dev)))
        return locs, pairs

    return _Task(shards, [_gathered_shape(s, k in wide) for k, s in enumerate(shards)],
                 [pltpu.SemaphoreType.DMA((3 * n,)), pltpu.SemaphoreType.DMA((3 * n,)), pltpu.SemaphoreType.DMA((n,))],
                 descs)


class _SplitGather(_Task):
    PARTS = 2

    def __init__(self, shards, wide=()):
        n = 3 * len(shards) * self.PARTS
        dma = pltpu.SemaphoreType.DMA
        self.wide = wide
        super().__init__(shards, [_gathered_shape(s, k in wide) for k, s in enumerate(shards)],
                         [dma((n,)), dma((n,)), dma((n,)), dma((n,)), dma((len(shards),))], None)

    def _plan(self, ins, outs, sems):
        send, recv, fsend, frecv, loc = sems
        x, y, c = _here()
        j = 2 * x + y
        locs, first, passed = [], [], []
        for k in range(len(ins)):
            h = self.ins[k].shape[0] // 2
            parts = self.PARTS if h % (32 * self.PARTS) == 0 else 1
            hp = h // parts
            at = functools.partial(_slot, outs[k], shard=self.ins[k], wide=k in self.wide)
            locs.append(functools.partial(pltpu.make_async_copy, ins[k], at(j), loc.at[k]))
            for r, (px, py) in enumerate(_chip_peers(x, y)):
                p = 2 * px + py
                for q in range(parts):
                    i = (3 * k + r) * self.PARTS + q
                    mine = pl.ds(pl.multiple_of(c * h + q * hp, 8), hp)
                    theirs = pl.ds(pl.multiple_of((1 - c) * h + q * hp, 8), hp)
                    first.append((_remote(ins[k].at[mine], at(j, rows=mine), send, recv, i, (px, py, c)),
                                  _remote(ins[k].at[mine], at(p, rows=mine), send, recv, i, (px, py, c))))
                    passed.append((_remote(at(p, rows=mine), at(p, rows=mine), fsend, frecv, i, (x, y, 1 - c)),
                                   _remote(at(p, rows=mine), at(p, rows=theirs), fsend, frecv, i, (x, y, 1 - c))))
        return locs, first, passed

    def start(self, ins, outs, sems):
        locs, first, _ = self._plan(ins, outs, sems)
        for lc in locs:
            lc().start()
        for snd, _ in first:
            snd().start()

    def wait(self, ins, outs, sems):
        locs, first, passed = self._plan(ins, outs, sems)
        for (_, landed), (pass_on, _) in zip(first, passed):
            landed().wait_recv()
            pass_on().start()
        for _, rcv in passed:
            rcv().wait_recv()
        for snd, _ in first + passed:
            snd().wait_send()
        for lc in locs:
            lc().wait()


def _a2a_task(ps):
    n = len(ps)
    nr = len(RELS)

    def descs(ins, outs, sems):
        send, recv = sems
        x, y, c = _here()
        me = 4 * x + 2 * y + c
        pairs = []
        for k in range(n):
            for i, (dx, dy, dc) in enumerate(RELS):
                dev = (_flip(x, dx), _flip(y, dy), _flip(c, dc))
                peer = 4 * dev[0] + 2 * dev[1] + dev[2]
                pairs.append((_remote(ins[k].at[peer], outs[k].at[me], send, recv, nr * k + i, dev),
                              _remote(ins[k].at[peer], outs[k].at[peer], send, recv, nr * k + i, dev)))
        return [], pairs

    return _Task(ps, [jax.ShapeDtypeStruct(p.shape, p.dtype) for p in ps],
                 [pltpu.SemaphoreType.DMA((nr * n,)), pltpu.SemaphoreType.DMA((nr * n,))], descs)


def _a2a_core_task(ps, core):
    n = len(ps)
    nr = len(RELS)

    def descs(ins, outs, sems):
        send, recv = sems
        x, y, c = _here()
        me = 4 * x + 2 * y + c
        pairs = []
        for k in range(n):
            for i, (dx, dy, dc) in enumerate(RELS):
                dev = (_flip(x, dx), _flip(y, dy), _flip(c, dc))
                peer = 4 * dev[0] + 2 * dev[1] + dev[2]
                pairs.append((_remote(ins[k].at[2 * dev[0] + dev[1]], outs[k].at[me], send, recv, nr * k + i, dev),
                              _remote(ins[k].at[2 * dev[0] + dev[1]], outs[k].at[peer], send, recv, nr * k + i, dev),
                              dev[2] == core, c == core))
        return [], pairs

    return _Task(ps, [jax.ShapeDtypeStruct((N_DEV,) + p.shape[1:], p.dtype) for p in ps],
                 [pltpu.SemaphoreType.DMA((nr * n,)), pltpu.SemaphoreType.DMA((nr * n,))], descs)


def _share_task(vs):
    n = len(vs)
    nr = len(RELS)

    def descs(ins, outs, sems):
        send, recv, loc = sems
        x, y, c = _here()
        me = 4 * x + 2 * y + c
        locs, pairs = [], []
        for k in range(n):
            locs.append(functools.partial(pltpu.make_async_copy, ins[k], outs[k].at[me], loc.at[k]))
            for i, (dx, dy, dc) in enumerate(RELS):
                dev = (_flip(x, dx), _flip(y, dy), _flip(c, dc))
                peer = 4 * dev[0] + 2 * dev[1] + dev[2]
                pairs.append((_remote(ins[k], outs[k].at[me], send, recv, nr * k + i, dev),
                              _remote(ins[k], outs[k].at[peer], send, recv, nr * k + i, dev)))
        return locs, pairs

    return _Task(vs, [jax.ShapeDtypeStruct((N_DEV,) + v.shape, v.dtype) for v in vs],
                 [pltpu.SemaphoreType.DMA((nr * n,)), pltpu.SemaphoreType.DMA((nr * n,)), pltpu.SemaphoreType.DMA((n,))],
                 descs)


def _join_task(ss):
    n = len(ss)

    def descs(ins, outs, sems):
        send, recv = sems
        x, y, c = _here()
        pairs = []
        for k in range(n):
            cp = _remote(ins[k], outs[k], send, recv, k, (x, y, 1 - c))
            pairs.append((cp, cp))
        return [], pairs

    return _Task(ss, [jax.ShapeDtypeStruct(s.shape, s.dtype) for s in ss],
                 [pltpu.SemaphoreType.DMA((n,)), pltpu.SemaphoreType.DMA((n,))], descs)


def _sum_pieces(recv, own, me, name):
    n, h, c = recv.shape
    tr = h

    def body(me_ref, ra_ref, rb_ref, o_ref, out_ref):
        s = pl.program_id(1)
        va = jnp.where(2 * s == me_ref[0], o_ref[0], ra_ref[0]).astype(F32)
        vb = jnp.where(2 * s + 1 == me_ref[0], o_ref[0], rb_ref[0]).astype(F32)

        @pl.when(s == 0)
        def _():
            out_ref[...] = va + vb

        @pl.when(s > 0)
        def _():
            out_ref[...] = (out_ref[...] + va) + vb

    def other(k, m):
        return jnp.where(k == m[0], (k + 1) % n, k)

    return _call(
        body, name=name,
        grid_spec=pltpu.PrefetchScalarGridSpec(
            num_scalar_prefetch=1, grid=(h // tr, n // 2),
            in_specs=[pl.BlockSpec((1, tr, c), lambda i, s, m: (other(2 * s, m), i, 0)),
                      pl.BlockSpec((1, tr, c), lambda i, s, m: (other(2 * s + 1, m), i, 0)),
                      pl.BlockSpec((1, tr, c), lambda i, s, m: (m[1], i, 0))],
            out_specs=pl.BlockSpec((tr, c), lambda i, s, m: (i, 0))),
        out_shape=jax.ShapeDtypeStruct((h, c), F32),
        compiler_params=_cp(),
    )(me, recv, recv, own)


def _adamw_update(gg, w, m, v):
    c1 = 1.0 / (1.0 - ADAM_B1 ** ADAM_STEP)
    c2 = 1.0 / (1.0 - ADAM_B2 ** ADAM_STEP)
    nm = ADAM_B1 * m + (1.0 - ADAM_B1) * gg
    nv = ADAM_B2 * v + (1.0 - ADAM_B2) * (gg * gg)
    return -ADAM_LR * ((nm * c1) / (jnp.sqrt(nv * c2) + ADAM_EPS) + ADAM_WD * w), nm, nv


def _adamw_small(gvec8, gmeta8, lp8, chip, ws, ms, vs, name):
    na = len(ws)

    def dev_sum(ref):
        acc = ref[0]
        for s in range(1, N_DEV):
            acc = acc + ref[s]
        return acc

    def body(c_ref, gv_ref, gm_ref, lp_ref, *refs):
        w_refs, m_refs, v_refs = refs[:na], refs[na:2 * na], refs[2 * na:3 * na]
        l_ref = refs[3 * na]
        outs = refs[3 * na + 1:]
        g_refs, d_refs, nm_refs, nv_refs = outs[:na], outs[na:2 * na], outs[2 * na:3 * na], outs[3 * na:]
        l_ref[...] = dev_sum(lp_ref)
        gvec = dev_sum(gv_ref)
        for k in range(na):
            gg = gvec[k:k + 1, 0:ws[k].shape[1]] if k < na - 1 else dev_sum(gm_ref)
            g_refs[k][...] = gg
            d_refs[k][...], nm_refs[k][...], nv_refs[k][...] = _adamw_update(gg, w_refs[k][...], m_refs[k][...], v_refs[k][...])

    whole = lambda a: pl.BlockSpec(a.shape, lambda i, c: (0,) * a.ndim)
    arrs = list(ws) + list(ms) + list(vs)
    res = _call(
        body, name=name,
        grid_spec=pltpu.PrefetchScalarGridSpec(
            num_scalar_prefetch=1, grid=(1,),
            in_specs=[whole(gvec8), pl.BlockSpec((N_DEV, NMETA, D // N_CHIPS), lambda i, c: (0, 0, c[0])), whole(lp8)]
                     + [whole(a) for a in arrs],
            out_specs=[pl.BlockSpec((1, D), lambda i, c: (0, 0))] + [whole(a) for a in ws] * 4),
        out_shape=[jax.ShapeDtypeStruct((1, D), F32)] + [jax.ShapeDtypeStruct(a.shape, F32) for a in ws] * 4,
        compiler_params=_cp(),
    )(chip, gvec8, gmeta8, lp8, *arrs)
    return res[0], [list(res[1 + i * na:1 + (i + 1) * na]) for i in range(4)]


def _adamw_halves(wt, mine, theirs, m, v, core, name):
    r, c = wt.shape
    h = r // 2
    tr = _tile(h, (256, 176, 128, 64))
    nh = h // tr

    def body(c_ref, w_ref, a_ref, b_ref, m_ref, v_ref, g_ref, d_ref, nm_ref, nv_ref):
        gg = jnp.where(pl.program_id(0) // nh == c_ref[0], a_ref[...], b_ref[...])
        g_ref[...] = gg
        d_ref[...], nm_ref[...], nv_ref[...] = _adamw_update(gg, w_ref[...], m_ref[...], v_ref[...])

    full = pl.BlockSpec((tr, c), lambda i, cr: (i, 0))
    half = pl.BlockSpec((tr, c), lambda i, cr: (i % nh, 0))
    return _call(
        body, name=name,
        grid_spec=pltpu.PrefetchScalarGridSpec(
            num_scalar_prefetch=1, grid=(2 * nh,),
            in_specs=[full, half, half, full, full], out_specs=[full] * 4),
        out_shape=[jax.ShapeDtypeStruct((r, c), F32)] * 4,
        compiler_params=_cp(),
    )(core, wt, mine, theirs, m, v)


MATS = ["ffn1_w_gu", "ffn1_w_down", "w_in", "mla_w_uq", "mla_w_ukv", "w_branch_fox", "w_branch_mla",
        "w_out", "ffn2_w_gu", "ffn2_w_down"]
VECS = ["ffn1_norm", "mix_norm", "b_forget", "b_gate", "fox_q_norm", "fox_k_norm", "mla_cq_norm",
        "mla_ckv_norm", "mla_q_norm", "mla_k_norm", "ffn2_norm"]
WEIGHTS = ["meta_tokens", "ffn1_norm", "ffn1_w_gu", "ffn1_w_down", "mix_norm", "w_in", "b_forget", "b_gate",
           "fox_q_norm", "fox_k_norm", "mla_cq_norm", "mla_w_uq", "mla_ckv_norm", "mla_w_ukv", "mla_q_norm",
           "mla_k_norm", "w_branch_fox", "w_branch_mla", "w_out", "ffn2_norm", "ffn2_w_gu", "ffn2_w_down"]


VEC_LANES = 2048


def _stack_vectors(parts):
    rows = [_pad_lanes(p, 0, VEC_LANES) for p in parts]
    rows.append(jnp.zeros((-len(parts) % 8, VEC_LANES), F32))
    return jnp.concatenate(rows, axis=0)


def kernel(x, meta_tokens, ffn1_norm, ffn1_w_gu, ffn1_w_down, mix_norm, w_in, b_forget, b_gate, fox_q_norm, fox_k_norm, mla_cq_norm, mla_w_uq, mla_ckv_norm, mla_w_ukv, mla_q_norm, mla_k_norm, w_branch_fox, w_branch_mla, w_out, ffn2_norm, ffn2_w_gu, ffn2_w_down, loss_target, m_meta_tokens, m_ffn1_norm, m_ffn1_w_gu, m_ffn1_w_down, m_mix_norm, m_w_in, m_b_forget, m_b_gate, m_fox_q_norm, m_fox_k_norm, m_mla_cq_norm, m_mla_w_uq, m_mla_ckv_norm, m_mla_w_ukv, m_mla_q_norm, m_mla_k_norm, m_w_branch_fox, m_w_branch_mla, m_w_out, m_ffn2_norm, m_ffn2_w_gu, m_ffn2_w_down, v_meta_tokens, v_ffn1_norm, v_ffn1_w_gu, v_ffn1_w_down, v_mix_norm, v_w_in, v_b_forget, v_b_gate, v_fox_q_norm, v_fox_k_norm, v_mla_cq_norm, v_mla_w_uq, v_mla_ckv_norm, v_mla_w_ukv, v_mla_q_norm, v_mla_k_norm, v_w_branch_fox, v_w_branch_mla, v_w_out, v_ffn2_norm, v_ffn2_w_gu, v_ffn2_w_down):
    a = dict(locals())
    wts = {n: a[n] for n in WEIGHTS}
    ms = {n: a["m_" + n] for n in WEIGHTS}
    vs = {n: a["v_" + n] for n in WEIGHTS}
    cx, cy, cc = lax.axis_index("x"), lax.axis_index("y"), lax.axis_index("c")
    chip = 2 * cx + cy

    shards = {n: wts[n][0].astype(BF16) for n in MATS}
    first = _run_tasks([_SplitGather([shards[n] for n in G_FFN1] + [meta_tokens], wide=(0,))], "gather_ffn1")[0]
    gath = dict(zip(G_FFN1, first[:-1]))
    meta_full = _cols_from_shards(first[-1])

    _, grad_x, _, _, gred, (gvec8, gmeta8, lp8) = _step(x, loss_target, meta_full, {n: wts[n] for n in VECS}, gath, shards)

    sm_names = VECS + ["meta_tokens"]
    lsum, sm = _adamw_small(gvec8, gmeta8, lp8, chip.reshape(1).astype(jnp.int32), [wts[n] for n in sm_names],
                            [ms[n] for n in sm_names], [vs[n] for n in sm_names], "adamw_small")
    loss = jnp.sum(lsum)

    grads, delta, new_m, new_v = {}, {}, {}, {}
    core = cc.reshape(1).astype(jnp.int32)
    for n in MATS:
        shp = wts[n].shape
        mine, theirs = gred[n]
        res = _adamw_halves(wts[n][0], mine, theirs, ms[n][0], vs[n][0], core, "adamw_" + n)
        grads[n], delta[n], new_m[n], new_v[n] = (t.reshape(shp) for t in res)
    for k, n in enumerate(sm_names):
        grads[n], delta[n], new_m[n], new_v[n] = (sm[i][k] for i in range(4))

    return (loss, grad_x, *[grads[n] for n in WEIGHTS], *[delta[n] for n in WEIGHTS],
            *[new_m[n] for n in WEIGHTS], *[new_v[n] for n in WEIGHTS])
```
